```python
import jax, jax.numpy as jnp
from jax import lax
import numpy as np

D_MODEL = 1024
BATCH = 8
SEQ = 4096
DEPTH = 2

PLE_DIM = 256
MIX_WIDTH = D_MODEL
POOL_WIDTH = MIX_WIDTH // 2
POOL_WINDOWS = (2, 4, 8, 16)
N_POOL_GROUPS = len(POOL_WINDOWS)
POOL_GROUP_DIM = POOL_WIDTH // N_POOL_GROUPS
SB_WIDTH = MIX_WIDTH - POOL_WIDTH
SB_HEAD_DIM = 64
SB_HEADS = SB_WIDTH // SB_HEAD_DIM
SB_BLOCK = 128
GDN_HEAD_DIM = 128
GDN_HEADS = MIX_WIDTH // GDN_HEAD_DIM
GDN_CONV = 4
GDN_CHUNK = 64
FFN_DIM = 2816
FFN_CONV = 3
EPS = 1e-6
N_EVEN = (DEPTH + 1) // 2
N_ODD = DEPTH // 2
EVEN_IN = POOL_WIDTH + 3 * SB_WIDTH
ODD_IN = 4 * MIX_WIDTH + 2 * GDN_HEADS

kernel_name = 'hybrid_pool_stickbreak_gdn_convffn_ple'


def rmsnorm(x, gain):
    xf = x.astype(jnp.float32)
    y = xf * lax.rsqrt(jnp.mean(xf * xf, axis=-1, keepdims=True) + EPS)
    return (y * gain.astype(jnp.float32)).astype(x.dtype)


def l2norm(x):
    return x * lax.rsqrt(jnp.sum(x * x, axis=-1, keepdims=True) + EPS)


def causal_dwconv(x, w):
    K = w.shape[0]
    T = x.shape[1]
    xp = jnp.pad(x, ((0, 0), (K - 1, 0), (0, 0)))
    return sum(xp[:, i:i + T] * w[i] for i in range(K))


def pool_mixer(u, pool_w, pool_scale):
    B, T, _ = u.shape
    ug = u.reshape(B, T, N_POOL_GROUPS, POOL_GROUP_DIM).astype(jnp.float32)
    cs = jnp.pad(jnp.cumsum(ug, axis=1), ((0, 0), (1, 0), (0, 0), (0, 0)))
    t = jnp.arange(T)
    win = jnp.array(POOL_WINDOWS, dtype=jnp.int32)
    start = jnp.maximum(t[:, None] + 1 - win[None, :], 0)
    g_idx = jnp.arange(N_POOL_GROUPS)[None, :]
    window_sum = cs[:, 1:] - cs[:, start, g_idx]
    count = (t[:, None] + 1 - start).astype(jnp.float32)
    y = window_sum / count[None, :, :, None] - ug
    y = jnp.einsum('btgc,gcd->btgd', y, pool_w.astype(jnp.float32))
    return (y.reshape(B, T, POOL_WIDTH) * pool_scale.astype(jnp.float32)).astype(u.dtype)


def stick_breaking_attention(q, k, v):
    T = q.shape[2]
    scale = SB_HEAD_DIM ** -0.5
    vf = v.astype(jnp.float32)
    outs = []
    for blk in range(T // SB_BLOCK):
        q0 = blk * SB_BLOCK
        end = q0 + SB_BLOCK
        z = jnp.einsum('bhqd,bhkd->bhqk', q[:, :, q0:end], k[:, :, :end]).astype(jnp.float32) * scale
        q_pos = q0 + jnp.arange(SB_BLOCK)
        k_pos = jnp.arange(end)
        valid = k_pos[None, :] < q_pos[:, None]
        log_1m = jnp.where(valid, jax.nn.log_sigmoid(-z), 0.0)
        log_keep = lax.cumsum(log_1m, axis=3, reverse=True) - log_1m
        a = jnp.where(valid, jnp.exp(jax.nn.log_sigmoid(z) + log_keep), 0.0)
        outs.append(jnp.einsum('bhqk,bhkd->bhqd', a, vf[:, :, :end]))
    return jnp.concatenate(outs, axis=2).astype(v.dtype)


def gated_delta_rule_chunked(q, k, v, g, beta):
    B, H, T, dk = q.shape
    dv = v.shape[-1]
    n = T // GDN_CHUNK
    q = q * dk ** -0.5

    def chunks(a):
        return a.reshape(B, H, n, GDN_CHUNK, *a.shape[3:])

    q, k, v, g, beta = chunks(q), chunks(k), chunks(v), chunks(g), chunks(beta)
    gc = jnp.cumsum(g, axis=-1)
    idx = jnp.arange(GDN_CHUNK)
    incl = idx[:, None] >= idx[None, :]
    strict = idx[:, None] > idx[None, :]
    decay = jnp.where(incl, jnp.exp(jnp.where(incl, gc[..., :, None] - gc[..., None, :], 0.0)), 0.0)
    k_beta = k * beta[..., None]
    a_mat = jnp.where(strict, jnp.einsum('bhncd,bhnsd->bhncs', k_beta, k) * decay, 0.0)
    rhs = jnp.concatenate([v * beta[..., None], k_beta * jnp.exp(gc)[..., None]], axis=-1)
    sol = lax.linalg.triangular_solve(jnp.eye(GDN_CHUNK, dtype=a_mat.dtype) + a_mat, rhs,
                                      left_side=True, lower=True, unit_diagonal=True)
    u, w = sol[..., :dv], sol[..., dv:]
    qk = jnp.einsum('bhncd,bhnsd->bhncs', q, k) * decay
    q_dec = q * jnp.exp(gc)[..., None]
    k_dec = k * jnp.exp(gc[..., -1:] - gc)[..., None]
    g_last = jnp.exp(gc[..., -1])
    xs = (jnp.moveaxis(qk, 2, 0), jnp.moveaxis(u, 2, 0), jnp.moveaxis(w, 2, 0),
          jnp.moveaxis(q_dec, 2, 0), jnp.moveaxis(k_dec, 2, 0), jnp.moveaxis(g_last, 2, 0))

    def step(state, inp):
        qk_c, u_c, w_c, q_c, k_c, gl = inp
        v_new = u_c - jnp.einsum('bhcd,bhde->bhce', w_c, state)
        o = jnp.einsum('bhcd,bhde->bhce', q_c, state) + jnp.einsum('bhcs,bhse->bhce', qk_c, v_new)
        state = state * gl[..., None, None] + jnp.einsum('bhcd,bhce->bhde', k_c, v_new)
        return state, o

    s0 = jnp.zeros((B, H, dk, dv), jnp.float32)
    _, o = lax.scan(step, s0, xs)
    return jnp.moveaxis(o, 0, 2).reshape(B, H, T, dv)


def even_mixer(h, w_in, pool_w, pool_scale, w_out):
    B, T, _ = h.shape
    proj = h @ w_in
    u, q, k, v = jnp.split(proj, [POOL_WIDTH, POOL_WIDTH + SB_WIDTH, POOL_WIDTH + 2 * SB_WIDTH], axis=-1)
    pool_out = pool_mixer(u, pool_w, pool_scale)

    def heads(a):
        return a.reshape(B, T, SB_HEADS, SB_HEAD_DIM).transpose(0, 2, 1, 3)

    attn = stick_breaking_attention(heads(q), heads(k), heads(v))
    attn = attn.transpose(0, 2, 1, 3).reshape(B, T, SB_WIDTH)
    return jnp.concatenate([pool_out, attn], axis=-1) @ w_out


def odd_mixer(h, w_in, conv_w, a_log, dt_bias, norm_w, w_out):
    B, T, _ = h.shape
    proj = h @ w_in
    qkv, z, b, a = jnp.split(proj, [3 * MIX_WIDTH, 4 * MIX_WIDTH, 4 * MIX_WIDTH + GDN_HEADS], axis=-1)
    qkv = jax.nn.silu(causal_dwconv(qkv, conv_w))
    q, k, v = jnp.split(qkv, 3, axis=-1)

    def heads(x_):
        return x_.reshape(B, T, GDN_HEADS, GDN_HEAD_DIM).transpose(0, 2, 1, 3).astype(jnp.float32)

    q, k, v = l2norm(heads(q)), l2norm(heads(k)), heads(v)
    beta = jax.nn.sigmoid(b.astype(jnp.float32)).transpose(0, 2, 1)
    g = (-jnp.exp(a_log.astype(jnp.float32))
         * jax.nn.softplus(a.astype(jnp.float32) + dt_bias.astype(jnp.float32))).transpose(0, 2, 1)
    o = gated_delta_rule_chunked(q, k, v, g, beta).transpose(0, 2, 1, 3)
    o = o * lax.rsqrt(jnp.mean(o * o, axis=-1, keepdims=True) + EPS) * norm_w.astype(jnp.float32)
    o = o * jax.nn.silu(z.reshape(B, T, GDN_HEADS, GDN_HEAD_DIM).astype(jnp.float32))
    return o.reshape(B, T, MIX_WIDTH).astype(h.dtype) @ w_out


def conv_ffn(h, w_up, conv_w, w_down):
    up = causal_dwconv(h @ w_up, conv_w)
    gate, val = jnp.split(up, 2, axis=-1)
    return (jax.nn.silu(gate) * val) @ w_down


def _fwd_setup_inputs(seed: int = 0) -> dict:
    key = jax.random.key(seed)
    ks = jax.random.split(key, 24)
    f32 = jnp.float32

    def dense(k_, shape, fan_in):
        return jax.random.normal(k_, shape, f32) * fan_in ** -0.5

    def gain(k_, shape):
        return 1.0 + 0.02 * jax.random.normal(k_, shape, f32)

    dt = jnp.exp(jax.random.uniform(ks[11], (N_ODD, GDN_HEADS), f32, np.log(1e-3), np.log(1e-1)))
    return {
        'x': jax.random.normal(ks[0], (BATCH, SEQ, D_MODEL), f32),
        'p': jax.random.normal(ks[1], (DEPTH, BATCH, SEQ, PLE_DIM), f32),
        'mix_norm_e': gain(ks[2], (N_EVEN, D_MODEL)),
        'w_in_e': dense(ks[3], (N_EVEN, D_MODEL, EVEN_IN), D_MODEL),
        'pool_w': dense(ks[4], (N_EVEN, N_POOL_GROUPS, POOL_GROUP_DIM, POOL_GROUP_DIM), POOL_GROUP_DIM),
        'pool_scale': 1.0 + 0.1 * jax.random.normal(ks[5], (N_EVEN, POOL_WIDTH), f32),
        'w_out_e': dense(ks[6], (N_EVEN, MIX_WIDTH, D_MODEL), MIX_WIDTH),
        'mix_norm_o': gain(ks[7], (N_ODD, D_MODEL)),
        'w_in_o': dense(ks[8], (N_ODD, D_MODEL, ODD_IN), D_MODEL),
        'conv_qkv_o': dense(ks[9], (N_ODD, GDN_CONV, 3 * MIX_WIDTH), GDN_CONV),
        'a_log_o': jnp.log(jax.random.uniform(ks[10], (N_ODD, GDN_HEADS), f32, 1.0, 16.0)),
        'dt_bias_o': dt + jnp.log(-jnp.expm1(-dt)),
        'gdn_norm_o': gain(ks[12], (N_ODD, GDN_HEAD_DIM)),
        'w_out_o': dense(ks[13], (N_ODD, MIX_WIDTH, D_MODEL), MIX_WIDTH),
        'ffn_norm': gain(ks[14], (DEPTH, D_MODEL)),
        'w_up': dense(ks[15], (DEPTH, D_MODEL, 2 * FFN_DIM), D_MODEL),
        'ffn_conv': dense(ks[16], (DEPTH, FFN_CONV, 2 * FFN_DIM), FFN_CONV),
        'w_down': dense(ks[17], (DEPTH, FFN_DIM, D_MODEL), FFN_DIM),
        'ple_norm': gain(ks[18], (DEPTH, D_MODEL)),
        'w_ple_gate': dense(ks[19], (DEPTH, D_MODEL, D_MODEL), D_MODEL),
        'w_ple': dense(ks[20], (DEPTH, PLE_DIM, D_MODEL), PLE_DIM),
        'final_norm': gain(ks[21], (D_MODEL,)),
    }


def _fwd_reference(x, p, mix_norm_e, w_in_e, pool_w, pool_scale, w_out_e,
              mix_norm_o, w_in_o, conv_qkv_o, a_log_o, dt_bias_o, gdn_norm_o, w_out_o,
              ffn_norm, w_up, ffn_conv, w_down, ple_norm, w_ple_gate, w_ple, final_norm):
    for i in range(DEPTH):
        j = i // 2
        if i % 2 == 0:
            x = x + even_mixer(rmsnorm(x, mix_norm_e[j]), w_in_e[j], pool_w[j], pool_scale[j], w_out_e[j])
        else:
            x = x + odd_mixer(rmsnorm(x, mix_norm_o[j]), w_in_o[j], conv_qkv_o[j], a_log_o[j],
                              dt_bias_o[j], gdn_norm_o[j], w_out_o[j])
        x = x + conv_ffn(rmsnorm(x, ffn_norm[i]), w_up[i], ffn_conv[i], w_down[i])
        gate = jax.nn.sigmoid(rmsnorm(x, ple_norm[i]) @ w_ple_gate[i])
        x = x + (p[i] @ w_ple[i]) * gate
    return rmsnorm(x, final_norm)


import jax as _jax
import jax.numpy as _jnp

TWIN_FORMAT = 'train_step'
FWD_PARAMS = ['x', 'p', 'mix_norm_e', 'w_in_e', 'pool_w', 'pool_scale', 'w_out_e', 'mix_norm_o', 'w_in_o', 'conv_qkv_o', 'a_log_o', 'dt_bias_o', 'gdn_norm_o', 'w_out_o', 'ffn_norm', 'w_up', 'ffn_conv', 'w_down', 'ple_norm', 'w_ple_gate', 'w_ple', 'final_norm']
TWIN_WEIGHTS = ['mix_norm_e', 'w_in_e', 'pool_w', 'pool_scale', 'w_out_e', 'mix_norm_o', 'w_in_o', 'conv_qkv_o', 'a_log_o', 'dt_bias_o', 'gdn_norm_o', 'w_out_o', 'ffn_norm', 'w_up', 'ffn_conv', 'w_down', 'ple_norm', 'w_ple_gate', 'w_ple', 'final_norm']
TWIN_DIFF_INPUT = 'x'
TWIN_INPUTS = ['x', 'p', 'mix_norm_e', 'w_in_e', 'pool_w', 'pool_scale', 'w_out_e', 'mix_norm_o', 'w_in_o', 'conv_qkv_o', 'a_log_o', 'dt_bias_o', 'gdn_norm_o', 'w_out_o', 'ffn_norm', 'w_up', 'ffn_conv', 'w_down', 'ple_norm', 'w_ple_gate', 'w_ple', 'final_norm', 'loss_target', 'm_mix_norm_e', 'm_w_in_e', 'm_pool_w', 'm_pool_scale', 'm_w_out_e', 'm_mix_norm_o', 'm_w_in_o', 'm_conv_qkv_o', 'm_a_log_o', 'm_dt_bias_o', 'm_gdn_norm_o', 'm_w_out_o', 'm_ffn_norm', 'm_w_up', 'm_ffn_conv', 'm_w_down', 'm_ple_norm', 'm_w_ple_gate', 'm_w_ple', 'm_final_norm', 'v_mix_norm_e', 'v_w_in_e', 'v_pool_w', 'v_pool_scale', 'v_w_out_e', 'v_mix_norm_o', 'v_w_in_o', 'v_conv_qkv_o', 'v_a_log_o', 'v_dt_bias_o', 'v_gdn_norm_o', 'v_w_out_o', 'v_ffn_norm', 'v_w_up', 'v_ffn_conv', 'v_w_down', 'v_ple_norm', 'v_w_ple_gate', 'v_w_ple', 'v_final_norm']
TWIN_OUTPUTS = ['loss', 'grad_x', 'grad_mix_norm_e', 'grad_w_in_e', 'grad_pool_w', 'grad_pool_scale', 'grad_w_out_e', 'grad_mix_norm_o', 'grad_w_in_o', 'grad_conv_qkv_o', 'grad_a_log_o', 'grad_dt_bias_o', 'grad_gdn_norm_o', 'grad_w_out_o', 'grad_ffn_norm', 'grad_w_up', 'grad_ffn_conv', 'grad_w_down', 'grad_ple_norm', 'grad_w_ple_gate', 'grad_w_ple', 'grad_final_norm', 'delta_mix_norm_e', 'delta_w_in_e', 'delta_pool_w', 'delta_pool_scale', 'delta_w_out_e', 'delta_mix_norm_o', 'delta_w_in_o', 'delta_conv_qkv_o', 'delta_a_log_o', 'delta_dt_bias_o', 'delta_gdn_norm_o', 'delta_w_out_o', 'delta_ffn_norm', 'delta_w_up', 'delta_ffn_conv', 'delta_w_down', 'delta_ple_norm', 'delta_w_ple_gate', 'delta_w_ple', 'delta_final_norm', 'new_m_mix_norm_e', 'new_m_w_in_e', 'new_m_pool_w', 'new_m_pool_scale', 'new_m_w_out_e', 'new_m_mix_norm_o', 'new_m_w_in_o', 'new_m_conv_qkv_o', 'new_m_a_log_o', 'new_m_dt_bias_o', 'new_m_gdn_norm_o', 'new_m_w_out_o', 'new_m_ffn_norm', 'new_m_w_up', 'new_m_ffn_conv', 'new_m_w_down', 'new_m_ple_norm', 'new_m_w_ple_gate', 'new_m_w_ple', 'new_m_final_norm', 'new_v_mix_norm_e', 'new_v_w_in_e', 'new_v_pool_w', 'new_v_pool_scale', 'new_v_w_out_e', 'new_v_mix_norm_o', 'new_v_w_in_o', 'new_v_conv_qkv_o', 'new_v_a_log_o', 'new_v_dt_bias_o', 'new_v_gdn_norm_o', 'new_v_w_out_o', 'new_v_ffn_norm', 'new_v_w_up', 'new_v_ffn_conv', 'new_v_w_down', 'new_v_ple_norm', 'new_v_w_ple_gate', 'new_v_w_ple', 'new_v_final_norm']
TWIN_LEAF_KINDS = {'loss': 'loss', 'grad_x': 'grad_x', 'grad_mix_norm_e': 'grad_w', 'grad_w_in_e': 'grad_w', 'grad_pool_w': 'grad_w', 'grad_pool_scale': 'grad_w', 'grad_w_out_e': 'grad_w', 'grad_mix_norm_o': 'grad_w', 'grad_w_in_o': 'grad_w', 'grad_conv_qkv_o': 'grad_w', 'grad_a_log_o': 'grad_w', 'grad_dt_bias_o': 'grad_w', 'grad_gdn_norm_o': 'grad_w', 'grad_w_out_o': 'grad_w', 'grad_ffn_norm': 'grad_w', 'grad_w_up': 'grad_w', 'grad_ffn_conv': 'grad_w', 'grad_w_down': 'grad_w', 'grad_ple_norm': 'grad_w', 'grad_w_ple_gate': 'grad_w', 'grad_w_ple': 'grad_w', 'grad_final_norm': 'grad_w', 'delta_mix_norm_e': 'delta_w', 'delta_w_in_e': 'delta_w', 'delta_pool_w': 'delta_w', 'delta_pool_scale': 'delta_w', 'delta_w_out_e': 'delta_w', 'delta_mix_norm_o': 'delta_w', 'delta_w_in_o': 'delta_w', 'delta_conv_qkv_o': 'delta_w', 'delta_a_log_o': 'delta_w', 'delta_dt_bias_o': 'delta_w', 'delta_gdn_norm_o': 'delta_w', 'delta_w_out_o': 'delta_w', 'delta_ffn_norm': 'delta_w', 'delta_w_up': 'delta_w', 'delta_ffn_conv': 'delta_w', 'delta_w_down': 'delta_w', 'delta_ple_norm': 'delta_w', 'delta_w_ple_gate': 'delta_w', 'delta_w_ple': 'delta_w', 'delta_final_norm': 'delta_w', 'new_m_mix_norm_e': 'new_m', 'new_m_w_in_e': 'new_m', 'new_m_pool_w': 'new_m', 'new_m_pool_scale': 'new_m', 'new_m_w_out_e': 'new_m', 'new_m_mix_norm_o': 'new_m', 'new_m_w_in_o': 'new_m', 'new_m_conv_qkv_o': 'new_m', 'new_m_a_log_o': 'new_m', 'new_m_dt_bias_o': 'new_m', 'new_m_gdn_norm_o': 'new_m', 'new_m_w_out_o': 'new_m', 'new_m_ffn_norm': 'new_m', 'new_m_w_up': 'new_m', 'new_m_ffn_conv': 'new_m', 'new_m_w_down': 'new_m', 'new_m_ple_norm': 'new_m', 'new_m_w_ple_gate': 'new_m', 'new_m_w_ple': 'new_m', 'new_m_final_norm': 'new_m', 'new_v_mix_norm_e': 'new_v', 'new_v_w_in_e': 'new_v', 'new_v_pool_w': 'new_v', 'new_v_pool_scale': 'new_v', 'new_v_w_out_e': 'new_v', 'new_v_mix_norm_o': 'new_v', 'new_v_w_in_o': 'new_v', 'new_v_conv_qkv_o': 'new_v', 'new_v_a_log_o': 'new_v', 'new_v_dt_bias_o': 'new_v', 'new_v_gdn_norm_o': 'new_v', 'new_v_w_out_o': 'new_v', 'new_v_ffn_norm': 'new_v', 'new_v_w_up': 'new_v', 'new_v_ffn_conv': 'new_v', 'new_v_w_down': 'new_v', 'new_v_ple_norm': 'new_v', 'new_v_w_ple_gate': 'new_v', 'new_v_w_ple': 'new_v', 'new_v_final_norm': 'new_v'}


def _forward(args):
    return _fwd_reference(*[args[k] for k in FWD_PARAMS])


def _output_shape():
    out = _jax.eval_shape(lambda: _forward(_fwd_setup_inputs(0)))
    return out.shape, out.dtype

N_MICROBATCH = 1
ADAM_LR = 0.001
ADAM_B1 = 0.9
ADAM_B2 = 0.999
ADAM_EPS = 1e-08
ADAM_WD = 0.01
ADAM_STEP = 10
PER_EXAMPLE_BATCH_AXIS = {'x': 0, 'p': 1, 'loss_target': 0}
SHARED_INPUTS = []
_WEIGHT_DTYPES = {'mix_norm_e': _jnp.float32, 'w_in_e': _jnp.float32, 'pool_w': _jnp.float32, 'pool_scale': _jnp.float32, 'w_out_e': _jnp.float32, 'mix_norm_o': _jnp.float32, 'w_in_o': _jnp.float32, 'conv_qkv_o': _jnp.float32, 'a_log_o': _jnp.float32, 'dt_bias_o': _jnp.float32, 'gdn_norm_o': _jnp.float32, 'w_out_o': _jnp.float32, 'ffn_norm': _jnp.float32, 'w_up': _jnp.float32, 'ffn_conv': _jnp.float32, 'w_down': _jnp.float32, 'ple_norm': _jnp.float32, 'w_ple_gate': _jnp.float32, 'w_ple': _jnp.float32, 'final_norm': _jnp.float32}
MOMENT_SCALE = {'mix_norm_e': 1.399094e-01, 'w_in_e': 9.753894e-02, 'pool_w': 1.447575e-01, 'pool_scale': 1.521024e-01, 'w_out_e': 1.283160e-01, 'mix_norm_o': 1.178686e-01, 'w_in_o': 5.592711e-02, 'conv_qkv_o': 5.079137e-02, 'a_log_o': 2.737483e-01, 'dt_bias_o': 2.689245e-01, 'gdn_norm_o': 1.822312e-01, 'w_out_o': 6.597708e-02, 'ffn_norm': 1.054369e-01, 'w_up': 4.449913e-02, 'ffn_conv': 4.535229e-02, 'w_down': 7.249530e-02, 'ple_norm': 2.562148e-02, 'w_ple_gate': 2.491129e-02, 'w_ple': 6.383181e-02, 'final_norm': 3.202199e+01}


def _to_microbatches(a, axis):
    t = _jnp.moveaxis(a, axis, 0)
    t = t.reshape((N_MICROBATCH, t.shape[0] // N_MICROBATCH) + t.shape[1:])
    return _jnp.moveaxis(t, 1, axis + 1)


def setup_inputs(seed: int = 0) -> dict:
    inp = _fwd_setup_inputs(seed)
    key = _jax.random.fold_in(_jax.random.key(seed), 7919)
    shape, _ = _output_shape()
    out = dict(inp)
    out["loss_target"] = _jax.random.normal(_jax.random.fold_in(key, 0), shape, _jnp.float32)
    for i, name in enumerate(TWIN_WEIGHTS):
        w = inp[name].astype(_jnp.float32)
        if MOMENT_SCALE is None:
            s = _jnp.sqrt(_jnp.mean(_jnp.square(w)) + 1e-30)
        else:
            s = MOMENT_SCALE[name]
        km, kv = _jax.random.split(_jax.random.fold_in(key, i + 1))
        out[name] = w
        out["m_" + name] = s * _jax.random.normal(km, w.shape, _jnp.float32)
        out["v_" + name] = (s * s) * _jax.random.uniform(kv, w.shape, _jnp.float32, 0.5, 1.5)
    if N_MICROBATCH > 1:
        for name, axis in PER_EXAMPLE_BATCH_AXIS.items():
            out[name] = _to_microbatches(out[name], axis)
    return {'x': out['x'], 'p': out['p'], 'mix_norm_e': out['mix_norm_e'], 'w_in_e': out['w_in_e'], 'pool_w': out['pool_w'], 'pool_scale': out['pool_scale'], 'w_out_e': out['w_out_e'], 'mix_norm_o': out['mix_norm_o'], 'w_in_o': out['w_in_o'], 'conv_qkv_o': out['conv_qkv_o'], 'a_log_o': out['a_log_o'], 'dt_bias_o': out['dt_bias_o'], 'gdn_norm_o': out['gdn_norm_o'], 'w_out_o': out['w_out_o'], 'ffn_norm': out['ffn_norm'], 'w_up': out['w_up'], 'ffn_conv': out['ffn_conv'], 'w_down': out['w_down'], 'ple_norm': out['ple_norm'], 'w_ple_gate': out['w_ple_gate'], 'w_ple': out['w_ple'], 'final_norm': out['final_norm'], 'loss_target': out['loss_target'], 'm_mix_norm_e': out['m_mix_norm_e'], 'm_w_in_e': out['m_w_in_e'], 'm_pool_w': out['m_pool_w'], 'm_pool_scale': out['m_pool_scale'], 'm_w_out_e': out['m_w_out_e'], 'm_mix_norm_o': out['m_mix_norm_o'], 'm_w_in_o': out['m_w_in_o'], 'm_conv_qkv_o': out['m_conv_qkv_o'], 'm_a_log_o': out['m_a_log_o'], 'm_dt_bias_o': out['m_dt_bias_o'], 'm_gdn_norm_o': out['m_gdn_norm_o'], 'm_w_out_o': out['m_w_out_o'], 'm_ffn_norm': out['m_ffn_norm'], 'm_w_up': out['m_w_up'], 'm_ffn_conv': out['m_ffn_conv'], 'm_w_down': out['m_w_down'], 'm_ple_norm': out['m_ple_norm'], 'm_w_ple_gate': out['m_w_ple_gate'], 'm_w_ple': out['m_w_ple'], 'm_final_norm': out['m_final_norm'], 'v_mix_norm_e': out['v_mix_norm_e'], 'v_w_in_e': out['v_w_in_e'], 'v_pool_w': out['v_pool_w'], 'v_pool_scale': out['v_pool_scale'], 'v_w_out_e': out['v_w_out_e'], 'v_mix_norm_o': out['v_mix_norm_o'], 'v_w_in_o': out['v_w_in_o'], 'v_conv_qkv_o': out['v_conv_qkv_o'], 'v_a_log_o': out['v_a_log_o'], 'v_dt_bias_o': out['v_dt_bias_o'], 'v_gdn_norm_o': out['v_gdn_norm_o'], 'v_w_out_o': out['v_w_out_o'], 'v_ffn_norm': out['v_ffn_norm'], 'v_w_up': out['v_w_up'], 'v_ffn_conv': out['v_ffn_conv'], 'v_w_down': out['v_w_down'], 'v_ple_norm': out['v_ple_norm'], 'v_w_ple_gate': out['v_w_ple_gate'], 'v_w_ple': out['v_w_ple'], 'v_final_norm': out['v_final_norm']}


def _loss(weights, diff, rest, loss_target):
    with _jax.named_scope("forward"):
        args = {**rest, TWIN_DIFF_INPUT: diff, **{k: w.astype(_WEIGHT_DTYPES[k]) for k, w in weights.items()}}
        y = _forward(args)
    with _jax.named_scope("loss_head"):
        err = _jnp.square(y.astype(_jnp.float32) - loss_target)
        return 0.5 * _jnp.sum(_jnp.mean(err, axis=-1)) if err.ndim else 0.5 * err


def _adamw(w, g, m, v):
    m = ADAM_B1 * m + (1.0 - ADAM_B1) * g
    v = ADAM_B2 * v + (1.0 - ADAM_B2) * _jnp.square(g)
    m_hat = m / (1.0 - ADAM_B1 ** ADAM_STEP)
    v_hat = v / (1.0 - ADAM_B2 ** ADAM_STEP)
    delta = -ADAM_LR * (m_hat / (_jnp.sqrt(v_hat) + ADAM_EPS) + ADAM_WD * w)
    return delta, m, v


def reference(x, p, mix_norm_e, w_in_e, pool_w, pool_scale, w_out_e, mix_norm_o, w_in_o, conv_qkv_o, a_log_o, dt_bias_o, gdn_norm_o, w_out_o, ffn_norm, w_up, ffn_conv, w_down, ple_norm, w_ple_gate, w_ple, final_norm, loss_target, m_mix_norm_e, m_w_in_e, m_pool_w, m_pool_scale, m_w_out_e, m_mix_norm_o, m_w_in_o, m_conv_qkv_o, m_a_log_o, m_dt_bias_o, m_gdn_norm_o, m_w_out_o, m_ffn_norm, m_w_up, m_ffn_conv, m_w_down, m_ple_norm, m_w_ple_gate, m_w_ple, m_final_norm, v_mix_norm_e, v_w_in_e, v_pool_w, v_pool_scale, v_w_out_e, v_mix_norm_o, v_w_in_o, v_conv_qkv_o, v_a_log_o, v_dt_bias_o, v_gdn_norm_o, v_w_out_o, v_ffn_norm, v_w_up, v_ffn_conv, v_w_down, v_ple_norm, v_w_ple_gate, v_w_ple, v_final_norm):
    given = dict(x=x, p=p, mix_norm_e=mix_norm_e, w_in_e=w_in_e, pool_w=pool_w, pool_scale=pool_scale, w_out_e=w_out_e, mix_norm_o=mix_norm_o, w_in_o=w_in_o, conv_qkv_o=conv_qkv_o, a_log_o=a_log_o, dt_bias_o=dt_bias_o, gdn_norm_o=gdn_norm_o, w_out_o=w_out_o, ffn_norm=ffn_norm, w_up=w_up, ffn_conv=ffn_conv, w_down=w_down, ple_norm=ple_norm, w_ple_gate=w_ple_gate, w_ple=w_ple, final_norm=final_norm, loss_target=loss_target, m_mix_norm_e=m_mix_norm_e, m_w_in_e=m_w_in_e, m_pool_w=m_pool_w, m_pool_scale=m_pool_scale, m_w_out_e=m_w_out_e, m_mix_norm_o=m_mix_norm_o, m_w_in_o=m_w_in_o, m_conv_qkv_o=m_conv_qkv_o, m_a_log_o=m_a_log_o, m_dt_bias_o=m_dt_bias_o, m_gdn_norm_o=m_gdn_norm_o, m_w_out_o=m_w_out_o, m_ffn_norm=m_ffn_norm, m_w_up=m_w_up, m_ffn_conv=m_ffn_conv, m_w_down=m_w_down, m_ple_norm=m_ple_norm, m_w_ple_gate=m_w_ple_gate, m_w_ple=m_w_ple, m_final_norm=m_final_norm, v_mix_norm_e=v_mix_norm_e, v_w_in_e=v_w_in_e, v_pool_w=v_pool_w, v_pool_scale=v_pool_scale, v_w_out_e=v_w_out_e, v_mix_norm_o=v_mix_norm_o, v_w_in_o=v_w_in_o, v_conv_qkv_o=v_conv_qkv_o, v_a_log_o=v_a_log_o, v_dt_bias_o=v_dt_bias_o, v_gdn_norm_o=v_gdn_norm_o, v_w_out_o=v_w_out_o, v_ffn_norm=v_ffn_norm, v_w_up=v_w_up, v_ffn_conv=v_ffn_conv, v_w_down=v_w_down, v_ple_norm=v_ple_norm, v_w_ple_gate=v_w_ple_gate, v_w_ple=v_w_ple, v_final_norm=v_final_norm)
    weights = {n: given[n] for n in TWIN_WEIGHTS}
    shared = {n: given[n] for n in SHARED_INPUTS}
    per_example = {n: given[n] for n in ['x', 'p']}
    grad_fn = _jax.value_and_grad(_loss, argnums=(0, 1))

    def one_microbatch(ex, loss_target):
        ex = dict(ex)
        diff = ex.pop(TWIN_DIFF_INPUT)
        return grad_fn(weights, diff, {**shared, **ex}, loss_target)

    if N_MICROBATCH == 1:
        loss, (grad_w, grad_x) = one_microbatch(per_example, given["loss_target"])
    else:
        def body(carry, xs):
            loss_sum, grad_sum = carry
            l_k, (gw_k, gx_k) = one_microbatch(xs[0], xs[1])
            with _jax.named_scope("update"):
                return (loss_sum + l_k, _jax.tree.map(_jnp.add, grad_sum, gw_k)), gx_k

        init = (_jnp.zeros((), _jnp.float32), _jax.tree.map(_jnp.zeros_like, weights))
        (loss, grad_w), grad_x = _jax.lax.scan(body, init, (per_example, given["loss_target"]))
    with _jax.named_scope("update"):
        delta_w, new_m, new_v = {}, {}, {}
        for n in TWIN_WEIGHTS:
            delta_w[n], new_m[n], new_v[n] = _adamw(weights[n], grad_w[n], given["m_" + n], given["v_" + n])
    return (loss, grad_x, *[grad_w[n] for n in TWIN_WEIGHTS], *[delta_w[n] for n in TWIN_WEIGHTS],
            *[new_m[n] for n in TWIN_WEIGHTS], *[new_v[n] for n in TWIN_WEIGHTS])
```

```python
import functools

import jax
import jax.numpy as jnp
from jax import lax
from jax.experimental import pallas as pl
from jax.experimental.pallas import tpu as pltpu

F32 = jnp.float32
BF16 = jnp.bfloat16

D_MODEL = 1024
PLE_DIM = 256
POOL_WINDOWS = (2, 4, 8, 16)
POOL_WIDTH = 512
SB_HEAD_DIM = 64
SB_BLOCK = 128
GDN_HEADS = 8
GDN_HEAD_DIM = 128
GDN_CONV = 4
GDN_CHUNK = 64
FFN_DIM = 2816
FFN_CONV = 3
EPS = 1e-6
ADAM_LR, ADAM_B1, ADAM_B2, ADAM_EPS, ADAM_WD, ADAM_STEP = 0.001, 0.9, 0.999, 1e-08, 0.01, 10
N_DEV = 8
MESH = pl.DeviceIdType.MESH
VMEM_LIMIT = 56 * 1024 * 1024

NN = (((1,), (0,)), ((), ()))
NT = (((1,), (1,)), ((), ()))
TN = (((0,), (0,)), ((), ()))


def _params(*sem):
    return pltpu.CompilerParams(dimension_semantics=sem if sem else None, vmem_limit_bytes=VMEM_LIMIT)


def _dot(a, b, dims):
    return lax.dot_general(a.astype(BF16), b.astype(BF16), dims, preferred_element_type=F32)


def _dotf(a, b, dims):
    return lax.dot_general(a, b, dims, precision=lax.Precision.HIGHEST, preferred_element_type=F32)


def _iota(shape, axis):
    return lax.broadcasted_iota(jnp.int32, shape, axis)


def _mm(a, b, *, ta=False, tb=False, res=None, out_dtype=F32, tm=512, tn=512, tk=None, name):
    M, K = (a.shape[1], a.shape[0]) if ta else a.shape
    N = b.shape[0] if tb else b.shape[1]
    tk = K if tk is None else min(tk, K)
    tm, tn = min(tm, M), min(tn, N)
    assert M % tm == 0 and N % tn == 0 and K % tk == 0, (name, M, N, K, tm, tn, tk)
    nk = K // tk
    dims = (((0 if ta else 1,), (1 if tb else 0,)), ((), ()))

    def body(*refs):
        if res is None:
            a_ref, b_ref, o_ref, *scr = refs
            r_ref = None
        else:
            a_ref, b_ref, r_ref, o_ref, *scr = refs
        p = _dot(a_ref[...], b_ref[...], dims)

        def fin(acc):
            if r_ref is not None:
                acc = acc + r_ref[...]
            o_ref[...] = acc.astype(out_dtype)

        if nk == 1:
            fin(p)
        else:
            acc_ref = scr[0]
            k = pl.program_id(2)

            @pl.when(k == 0)
            def _():
                acc_ref[...] = p

            @pl.when(k > 0)
            def _():
                acc_ref[...] += p

            @pl.when(k == nk - 1)
            def _():
                fin(acc_ref[...])

    a_spec = pl.BlockSpec((tk, tm), lambda i, j, k: (k, i)) if ta else pl.BlockSpec((tm, tk), lambda i, j, k: (i, k))
    b_spec = pl.BlockSpec((tn, tk), lambda i, j, k: (j, k)) if tb else pl.BlockSpec((tk, tn), lambda i, j, k: (k, j))
    o_spec = pl.BlockSpec((tm, tn), lambda i, j, k: (i, j))
    in_specs = [a_spec, b_spec] + ([o_spec] if res is not None else [])
    args = (a, b) + ((res,) if res is not None else ())
    return pl.pallas_call(
        body, name=name, grid=(M // tm, N // tn, nk), in_specs=in_specs, out_specs=o_spec,
        out_shape=jax.ShapeDtypeStruct((M, N), out_dtype),
        scratch_shapes=[pltpu.VMEM((tm, tn), F32)] if nk > 1 else [],
        compiler_params=_params("parallel", "parallel", "arbitrary"),
    )(*args)


def _rms_fwd(x, gain, *, name, tr=512):
    T, Dm = x.shape

    def body(x_ref, g_ref, o_ref):
        xv = x_ref[...]
        r = lax.rsqrt(jnp.mean(xv * xv, axis=-1, keepdims=True) + EPS)
        o_ref[...] = (xv * r * g_ref[...]).astype(BF16)

    return pl.pallas_call(
        body, name=name, grid=(T // tr,),
        in_specs=[pl.BlockSpec((tr, Dm), lambda i: (i, 0)), pl.BlockSpec((1, Dm), lambda i: (0, 0))],
        out_specs=pl.BlockSpec((tr, Dm), lambda i: (i, 0)),
        out_shape=jax.ShapeDtypeStruct((T, Dm), BF16), compiler_params=_params("parallel"),
    )(x, gain)


def _rms_bwd(x, dy, gain, dres, *, name, tr=512):
    T, Dm = x.shape

    def body(x_ref, dy_ref, g_ref, dres_ref, dx_ref, dg_ref):
        i = pl.program_id(0)
        xv = x_ref[...]
        dy_v = dy_ref[...].astype(F32)
        r = lax.rsqrt(jnp.mean(xv * xv, axis=-1, keepdims=True) + EPS)
        xn = xv * r
        dgp = jnp.sum(dy_v * xn, axis=0, keepdims=True)
        dyg = dy_v * g_ref[...]
        dx = r * (dyg - xn * jnp.mean(dyg * xn, axis=-1, keepdims=True))
        dx_ref[...] = dres_ref[...] + dx

        @pl.when(i == 0)
        def _():
            dg_ref[...] = dgp

        @pl.when(i > 0)
        def _():
            dg_ref[...] += dgp

    row = pl.BlockSpec((tr, Dm), lambda i: (i, 0))
    vec = pl.BlockSpec((1, Dm), lambda i: (0, 0))
    return pl.pallas_call(
        body, name=name, grid=(T // tr,), in_specs=[row, row, vec, row], out_specs=[row, vec],
        out_shape=[jax.ShapeDtypeStruct((T, Dm), F32), jax.ShapeDtypeStruct((1, Dm), F32)],
        compiler_params=_params("arbitrary"),
    )(x, dy, gain, dres)


def _final_loss(x, gain, target, *, tr=512):
    T, Dm = x.shape

    def body(x_ref, g_ref, t_ref, loss_ref, dx_ref, dg_ref):
        i = pl.program_id(0)
        xv = x_ref[...]
        g = g_ref[...]
        r = lax.rsqrt(jnp.mean(xv * xv, axis=-1, keepdims=True) + EPS)
        xn = xv * r
        err = xn * g - t_ref[...]
        lp = jnp.zeros((1, 128), F32) + 0.5 * jnp.sum(jnp.mean(err * err, axis=-1, keepdims=True))
        dy_v = err * (1.0 / Dm)
        dgp = jnp.sum(dy_v * xn, axis=0, keepdims=True)
        dyg = dy_v * g
        dx_ref[...] = r * (dyg - xn * jnp.mean(dyg * xn, axis=-1, keepdims=True))

        @pl.when(i == 0)
        def _():
            dg_ref[...] = dgp
            loss_ref[...] = lp

        @pl.when(i > 0)
        def _():
            dg_ref[...] += dgp
            loss_ref[...] += lp

    row = pl.BlockSpec((tr, Dm), lambda i: (i, 0))
    vec = pl.BlockSpec((1, Dm), lambda i: (0, 0))
    return pl.pallas_call(
        body, name="final_loss", grid=(T // tr,), in_specs=[row, vec, row],
        out_specs=[pl.BlockSpec((1, 128), lambda i: (0, 0)), row, vec],
        out_shape=[jax.ShapeDtypeStruct((1, 128), F32), jax.ShapeDtypeStruct((T, Dm), F32),
                   jax.ShapeDtypeStruct((1, Dm), F32)],
        compiler_params=_params("arbitrary"),
    )(x, gain, target)


def _prev_spec(tr, cb, pad, col):
    return pl.BlockSpec((pad, cb), lambda *g: (jnp.maximum(g[0] * (tr // pad) - 1, 0), col(*g)))


def _next_spec(tr, cb, pad, col, T):
    return pl.BlockSpec((pad, cb), lambda *g: (jnp.minimum((g[0] + 1) * (tr // pad), T // pad - 1), col(*g)))


def _conv_rows(x_ext, w_ref, K, pad):
    y = w_ref[K - 1:K, :] * x_ext
    for i in range(K - 1):
        y = y + w_ref[i:i + 1, :] * pltpu.roll(x_ext, K - 1 - i, 0)
    return y[pad:]


def _pool_y(u_ext, g, i, tr):
    s = u_ext
    for sh in (1, 2, 4, 8)[:g + 1]:
        s = s + pltpu.roll(s, sh, 0)
    t = i * tr + _iota((tr, 128), 0)
    cnt = jnp.minimum(t + 1, POOL_WINDOWS[g]).astype(F32)
    return s[16:] / cnt - u_ext[16:]


def _pool_fwd(proj, pool_w, pool_scale, *, tr=512):
    T = proj.shape[0]

    def body(u_ref, uh_ref, w_ref, s_ref, o_ref):
        i = pl.program_id(0)
        uh = jnp.where(i > 0, uh_ref[...], 0.0)
        for g in range(4):
            cs = slice(128 * g, 128 * (g + 1))
            y = _pool_y(jnp.concatenate([uh[:, cs], u_ref[:, cs]], axis=0), g, i, tr)
            o_ref[:, cs] = (_dot(y, w_ref[g], NN) * s_ref[:, cs]).astype(BF16)

    return pl.pallas_call(
        body, name="pool_fwd", grid=(T // tr,),
        in_specs=[pl.BlockSpec((tr, 512), lambda i: (i, 0)), _prev_spec(tr, 512, 16, lambda i: 0),
                  pl.BlockSpec((4, 128, 128), lambda i: (0, 0, 0)), pl.BlockSpec((1, 512), lambda i: (0, 0))],
        out_specs=pl.BlockSpec((tr, 512), lambda i: (i, 0)),
        out_shape=jax.ShapeDtypeStruct((T, 512), BF16), compiler_params=_params("parallel"),
    )(proj, proj, pool_w, pool_scale)


def _pool_bwd(proj, dout, pool_w, pool_scale, *, tr=512):
    T = proj.shape[0]
    nb = T // tr

    def body(u_ref, uh_ref, d_ref, dn_ref, w_ref, s_ref, du_ref, dw_ref, ds_ref):
        i = pl.program_id(0)
        uh = jnp.where(i > 0, uh_ref[...], 0.0)
        dn = jnp.where(i < nb - 1, dn_ref[...], 0.0)
        t_ext = i * tr + _iota((tr + 16, 128), 0)
        for g in range(4):
            cs = slice(128 * g, 128 * (g + 1))
            sc = s_ref[:, cs]
            wg = w_ref[g]
            y = _pool_y(jnp.concatenate([uh[:, cs], u_ref[:, cs]], axis=0), g, i, tr)
            dg = d_ref[:, cs]
            dsp = jnp.sum(dg * _dot(y, wg, NN), axis=0, keepdims=True)
            dyw = dg * sc
            dwp = _dot(y, dyw, TN)
            dy_ext = _dot(jnp.concatenate([dyw, dn[:, cs] * sc], axis=0), wg, NT)
            cnt = jnp.minimum(t_ext + 1, POOL_WINDOWS[g]).astype(F32)
            s = dy_ext / cnt
            for sh in (1, 2, 4, 8)[:g + 1]:
                s = s + pltpu.roll(s, tr + 16 - sh, 0)
            du_ref[:, cs] = (s[:tr] - dy_ext[:tr]).astype(BF16)

            @pl.when(i == 0)
            def _():
                dw_ref[g] = dwp
                ds_ref[:, cs] = dsp

            @pl.when(i > 0)
            def _():
                dw_ref[g] += dwp
                ds_ref[:, cs] += dsp

    row = pl.BlockSpec((tr, 512), lambda i: (i, 0))
    return pl.pallas_call(
        body, name="pool_bwd", grid=(nb,),
        in_specs=[row, _prev_spec(tr, 512, 16, lambda i: 0), row, _next_spec(tr, 512, 16, lambda i: 0, T),
                  pl.BlockSpec((4, 128, 128), lambda i: (0, 0, 0)), pl.BlockSpec((1, 512), lambda i: (0, 0))],
        out_specs=[row, pl.BlockSpec((4, 128, 128), lambda i: (0, 0, 0)), pl.BlockSpec((1, 512), lambda i: (0, 0))],
        out_shape=[jax.ShapeDtypeStruct((T, 512), BF16), jax.ShapeDtypeStruct((4, 128, 128), F32),
                   jax.ShapeDtypeStruct((1, 512), F32)],
        compiler_params=_params("arbitrary"),
    )(proj, proj, dout, dout, pool_w, pool_scale)


def _split_dot(x, tri):
    hi = x.astype(BF16)
    lo = (x - hi.astype(F32)).astype(BF16)
    return (lax.dot_general(hi, tri, NN, preferred_element_type=F32)
            + lax.dot_general(lo, tri, NN, preferred_element_type=F32))


def _log1m(z):
    return -(jnp.maximum(z, 0.0) + jnp.log(1.0 + jnp.exp(-jnp.abs(z))))


def _sb_fwd(proj):
    T = proj.shape[0]
    nq = T // SB_BLOCK
    scale = SB_HEAD_DIM ** -0.5

    def body(q_ref, k_ref, v_ref, o_ref, ls_ref):
        i = pl.program_id(1)
        lane = _iota((1, 128), 1)
        row, col = _iota((128, 128), 0), _iota((128, 128), 1)
        tri_gt = (row > col).astype(BF16)
        valid = col < row
        qv = q_ref[...] * scale
        out = jnp.zeros((128, 128), F32)
        lsum = jnp.zeros((128, 128), F32)
        for h in range(2):
            hm = (lane >= 64 * h) & (lane < 64 * (h + 1))
            qh = jnp.where(hm, qv, 0.0).astype(BF16)

            def tile(j, c, acc, diag):
                rows = pl.ds(pl.multiple_of(j * 128, 128), 128)
                z = lax.dot_general(qh, k_ref[rows, :].astype(BF16), NT, preferred_element_type=F32)
                lg = _log1m(z)
                if diag:
                    lg = jnp.where(valid, lg, 0.0)
                a = jnp.exp(z + lg + _split_dot(lg, tri_gt) + c)
                if diag:
                    a = jnp.where(valid, a, 0.0)
                acc = acc + _dot(a, v_ref[rows, :], NN)
                return c + jnp.sum(lg, axis=1, keepdims=True), acc

            c, acc = tile(i, jnp.zeros((128, 1), F32), jnp.zeros((128, 128), F32), True)
            c, acc = lax.fori_loop(0, i, lambda s, cr: tile(i - 1 - s, cr[0], cr[1], False), (c, acc))
            out = out + jnp.where(hm, acc, 0.0)
            lsum = lsum + jnp.where(hm, c, 0.0)
        o_ref[...] = out
        ls_ref[...] = lsum

    blk = pl.BlockSpec((128, 128), lambda hp, i: (i, hp))
    return pl.pallas_call(
        body, name="sb_fwd", grid=(4, nq),
        in_specs=[pl.BlockSpec((128, 128), lambda hp, i: (i, 4 + hp)),
                  pl.BlockSpec((T, 128), lambda hp, i: (0, 8 + hp)),
                  pl.BlockSpec((T, 128), lambda hp, i: (0, 12 + hp))],
        out_specs=[blk, blk],
        out_shape=[jax.ShapeDtypeStruct((T, 512), F32)] * 2, compiler_params=_params("parallel", "parallel"),
    )(proj, proj, proj)


def _sb_bwd(proj, lsum, dout):
    T = proj.shape[0]
    nq = T // SB_BLOCK
    scale = SB_HEAD_DIM ** -0.5

    def body(q_ref, k_ref, v_ref, do_ref, ls_ref, dq_ref, dk_ref, dv_ref):
        i = pl.program_id(1)

        @pl.when(i == 0)
        def _():
            dk_ref[...] = jnp.zeros_like(dk_ref)
            dv_ref[...] = jnp.zeros_like(dv_ref)

        lane = _iota((1, 128), 1)
        row, col = _iota((128, 128), 0), _iota((128, 128), 1)
        tri_le = (row <= col).astype(BF16)
        tri_lt = (row < col).astype(BF16)
        valid = col < row
        qv = q_ref[...] * scale
        dov = do_ref[...]
        dq = jnp.zeros((128, 128), F32)
        for h in range(2):
            hm = (lane >= 64 * h) & (lane < 64 * (h + 1))
            qh = jnp.where(hm, qv, 0.0).astype(BF16)
            do_h = jnp.where(hm, dov, 0.0).astype(BF16)
            ltot = ls_ref[:, 64 * h:64 * h + 1]

            def tile(j, lbef, ebef, dqa, diag):
                rows = pl.ds(pl.multiple_of(j * 128, 128), 128)
                kj = k_ref[rows, :].astype(BF16)
                z = lax.dot_general(qh, kj, NT, preferred_element_type=F32)
                lg = _log1m(z)
                if diag:
                    lg = jnp.where(valid, lg, 0.0)
                a = jnp.exp(z + lg + (ltot - lbef - _split_dot(lg, tri_le)))
                if diag:
                    a = jnp.where(valid, a, 0.0)
                e = a * lax.dot_general(do_h, v_ref[rows, :].astype(BF16), NT, preferred_element_type=F32)
                dz = e * jnp.exp(lg) - jnp.exp(z + lg) * (ebef + _split_dot(e, tri_lt))
                if diag:
                    dz = jnp.where(valid, dz, 0.0)
                dzb = dz.astype(BF16)
                dqa = dqa + jnp.where(hm, lax.dot_general(dzb, kj, NN, preferred_element_type=F32), 0.0)
                dk_ref[rows, :] += lax.dot_general(dzb, qh, TN, preferred_element_type=F32)
                dv_ref[rows, :] += lax.dot_general(a.astype(BF16), do_h, TN, preferred_element_type=F32)
                return lbef + jnp.sum(lg, axis=1, keepdims=True), ebef + jnp.sum(e, axis=1, keepdims=True), dqa

            zc = jnp.zeros((128, 1), F32)
            lbef, ebef, dq = lax.fori_loop(0, i, lambda j, cr: tile(j, cr[0], cr[1], cr[2], False), (zc, zc, dq))
            _, _, dq = tile(i, lbef, ebef, dq, True)
        dq_ref[...] = dq * scale

    full = pl.BlockSpec((T, 128), lambda hp, i: (0, hp))
    blk = pl.BlockSpec((128, 128), lambda hp, i: (i, hp))
    return pl.pallas_call(
        body, name="sb_bwd", grid=(4, nq),
        in_specs=[pl.BlockSpec((128, 128), lambda hp, i: (i, 4 + hp)),
                  pl.BlockSpec((T, 128), lambda hp, i: (0, 8 + hp)),
                  pl.BlockSpec((T, 128), lambda hp, i: (0, 12 + hp)),
                  pl.BlockSpec((128, 128), lambda hp, i: (i, 4 + hp)), blk],
        out_specs=[blk, full, full],
        out_shape=[jax.ShapeDtypeStruct((T, 512), F32)] * 3,
        compiler_params=_params("parallel", "arbitrary"),
    )(proj, proj, proj, dout, lsum)


def _sigmoid(x):
    return 1.0 / (1.0 + jnp.exp(-x))


def _silu_mul(cg, cv):
    return cg * _sigmoid(cg) * cv


def _ffn_act(up, conv_w, *, tr=512, cb=256):
    T, F2 = up.shape
    nc = F2 // 2 // cb
    K = FFN_CONV

    def body(g_ref, gh_ref, v_ref, vh_ref, wg_ref, wv_ref, o_ref):
        i = pl.program_id(0)
        gh = jnp.where(i > 0, gh_ref[...], 0.0)
        vh = jnp.where(i > 0, vh_ref[...], 0.0)
        cg = _conv_rows(jnp.concatenate([gh, g_ref[...]], axis=0), wg_ref, K, 8)
        cv = _conv_rows(jnp.concatenate([vh, v_ref[...]], axis=0), wv_ref, K, 8)
        o_ref[...] = _silu_mul(cg, cv).astype(BF16)

    return pl.pallas_call(
        body, name="ffn_act", grid=(T // tr, nc),
        in_specs=[pl.BlockSpec((tr, cb), lambda i, j: (i, j)), _prev_spec(tr, cb, 8, lambda i, j: j),
                  pl.BlockSpec((tr, cb), lambda i, j: (i, nc + j)), _prev_spec(tr, cb, 8, lambda i, j: nc + j),
                  pl.BlockSpec((K, cb), lambda i, j: (0, j)), pl.BlockSpec((K, cb), lambda i, j: (0, nc + j))],
        out_specs=pl.BlockSpec((tr, cb), lambda i, j: (i, j)),
        out_shape=jax.ShapeDtypeStruct((T, F2 // 2), BF16), compiler_params=_params("parallel", "parallel"),
    )(up, up, up, up, conv_w, conv_w)


def _conv_bwd_rows(dc_ext, x_ext, w_ref, K, tr):
    n = tr + 8
    dx = w_ref[K - 1:K, :] * dc_ext
    for i in range(K - 1):
        dx = dx + w_ref[i:i + 1, :] * pltpu.roll(dc_ext, n - (K - 1 - i), 0)
    dc = dc_ext[:tr]
    dws = [jnp.sum(dc * pltpu.roll(x_ext, K - 1 - i, 0)[8:8 + tr], axis=0, keepdims=True) for i in range(K)]
    return dx[:tr], dws


def _acc_rows(ref, rows, first):
    for i, r in enumerate(rows):
        @pl.when(first)
        def _():
            ref[i:i + 1, :] = r

        @pl.when(jnp.logical_not(first))
        def _():
            ref[i:i + 1, :] += r


def _ffn_act_bwd(up, conv_w, dact, *, tr=512, cb=256):
    T, F2 = up.shape
    F = F2 // 2
    nc, nb = F // cb, T // tr
    K = FFN_CONV

    def body(g_ref, gp_ref, gn_ref, v_ref, vp_ref, vn_ref, d_ref, dn_ref, wg_ref, wv_ref,
             dg_ref, dv_ref, dwg_ref, dwv_ref):
        i = pl.program_id(1)
        first, last = i == 0, i == nb - 1
        g_ext = jnp.concatenate([jnp.where(first, 0.0, gp_ref[...]), g_ref[...], jnp.where(last, 0.0, gn_ref[...])], axis=0)
        v_ext = jnp.concatenate([jnp.where(first, 0.0, vp_ref[...]), v_ref[...], jnp.where(last, 0.0, vn_ref[...])], axis=0)
        d_ext = jnp.concatenate([d_ref[...], jnp.where(last, 0.0, dn_ref[...])], axis=0)
        cg = _conv_rows(g_ext, wg_ref, K, 8)
        cv = _conv_rows(v_ext, wv_ref, K, 8)
        _, vjp = jax.vjp(_silu_mul, cg, cv)
        dcg, dcv = vjp(d_ext)
        dg, dwg = _conv_bwd_rows(dcg, g_ext, wg_ref, K, tr)
        dv, dwv = _conv_bwd_rows(dcv, v_ext, wv_ref, K, tr)
        dg_ref[...] = dg.astype(BF16)
        dv_ref[...] = dv.astype(BF16)
        _acc_rows(dwg_ref, dwg, first)
        _acc_rows(dwv_ref, dwv, first)

    blk = lambda off: pl.BlockSpec((tr, cb), lambda j, i: (i, off + j))
    prev = lambda off: pl.BlockSpec((8, cb), lambda j, i: (jnp.maximum(i * (tr // 8) - 1, 0), off + j))
    nxt = lambda off: pl.BlockSpec((8, cb), lambda j, i: (jnp.minimum((i + 1) * (tr // 8), T // 8 - 1), off + j))
    wsp = lambda off: pl.BlockSpec((K, cb), lambda j, i: (0, off + j))
    return pl.pallas_call(
        body, name="ffn_act_bwd", grid=(nc, nb),
        in_specs=[blk(0), prev(0), nxt(0), blk(nc), prev(nc), nxt(nc), blk(0), nxt(0), wsp(0), wsp(nc)],
        out_specs=[blk(0), blk(0), wsp(0), wsp(0)],
        out_shape=[jax.ShapeDtypeStruct((T, F), BF16)] * 2 + [jax.ShapeDtypeStruct((K, F), F32)] * 2,
        compiler_params=_params("parallel", "arbitrary"),
    )(up, up, up, up, up, up, dact, dact, conv_w, conv_w)


def _ple_fwd(hn, w_gate, p, w_ple, x, *, name, tm=512, tn=512):
    T, Dm = x.shape

    def body(a_ref, b_ref, p_ref, wp_ref, x_ref, o_ref, gl_ref, pe_ref):
        gl = _dot(a_ref[...], b_ref[...], NN)
        pe = _dot(p_ref[...], wp_ref[...], NN)
        gl_ref[...] = gl
        pe_ref[...] = pe
        o_ref[...] = x_ref[...] + pe * _sigmoid(gl)

    o_spec = pl.BlockSpec((tm, tn), lambda i, j: (i, j))
    return pl.pallas_call(
        body, name=name, grid=(T // tm, Dm // tn),
        in_specs=[pl.BlockSpec((tm, Dm), lambda i, j: (i, 0)), pl.BlockSpec((Dm, tn), lambda i, j: (0, j)),
                  pl.BlockSpec((tm, PLE_DIM), lambda i, j: (i, 0)), pl.BlockSpec((PLE_DIM, tn), lambda i, j: (0, j)),
                  o_spec],
        out_specs=[o_spec] * 3, out_shape=[jax.ShapeDtypeStruct((T, Dm), F32)] * 3,
        compiler_params=_params("parallel", "parallel"),
    )(hn, w_gate, p, w_ple, x)


def _ple_bwd(dx, gl, pe, *, name, tr=512):
    T, Dm = dx.shape

    def body(dx_ref, gl_ref, pe_ref, dpe_ref, dgl_ref):
        g = _sigmoid(gl_ref[...])
        d = dx_ref[...]
        dpe_ref[...] = (d * g).astype(BF16)
        dgl_ref[...] = (d * pe_ref[...] * g * (1.0 - g)).astype(BF16)

    row = pl.BlockSpec((tr, Dm), lambda i: (i, 0))
    return pl.pallas_call(
        body, name=name, grid=(T // tr,), in_specs=[row] * 3, out_specs=[row] * 2,
        out_shape=[jax.ShapeDtypeStruct((T, Dm), BF16)] * 2, compiler_params=_params("parallel"),
    )(dx, gl, pe)


def _qkv_act(c, cb):
    s = c * _sigmoid(c)
    n = s * lax.rsqrt(jnp.sum(s * s, axis=-1, keepdims=True) + EPS)
    n = n * jnp.where(cb < GDN_HEADS, GDN_HEAD_DIM ** -0.5, 1.0)
    return jnp.where(cb < 2 * GDN_HEADS, n, s)


def _gdn_pre(proj, conv_w, *, tr=512):
    T = proj.shape[0]
    K = GDN_CONV

    def body(x_ref, xh_ref, w_ref, o_ref):
        i, cb = pl.program_id(0), pl.program_id(1)
        xh = jnp.where(i > 0, xh_ref[...], 0.0)
        c = _conv_rows(jnp.concatenate([xh, x_ref[...]], axis=0), w_ref, K, 8)
        o_ref[0] = _qkv_act(c, cb)

    return pl.pallas_call(
        body, name="gdn_pre", grid=(T // tr, 24),
        in_specs=[pl.BlockSpec((tr, 128), lambda i, j: (i, j)), _prev_spec(tr, 128, 8, lambda i, j: j),
                  pl.BlockSpec((K, 128), lambda i, j: (0, j))],
        out_specs=pl.BlockSpec((1, tr, 128), lambda i, j: (j, i, 0)),
        out_shape=jax.ShapeDtypeStruct((24, T, 128), F32), compiler_params=_params("parallel", "parallel"),
    )(proj, proj, conv_w)


def _gdn_pre_bwd(proj, conv_w, dqkv, *, tr=512):
    T = proj.shape[0]
    nb = T // tr
    K = GDN_CONV

    def body(x_ref, xp_ref, xn_ref, d_ref, dn_ref, w_ref, dx_ref, dw_ref):
        cb, i = pl.program_id(0), pl.program_id(1)
        first, last = i == 0, i == nb - 1
        x_ext = jnp.concatenate([jnp.where(first, 0.0, xp_ref[...]), x_ref[...], jnp.where(last, 0.0, xn_ref[...])], axis=0)
        d_ext = jnp.concatenate([d_ref[0], jnp.where(last, 0.0, dn_ref[0])], axis=0)
        c = _conv_rows(x_ext, w_ref, K, 8)
        _, vjp = jax.vjp(lambda c_: _qkv_act(c_, cb), c)
        (dc,) = vjp(d_ext)
        dx, dws = _conv_bwd_rows(dc, x_ext, w_ref, K, tr)
        dx_ref[...] = dx.astype(BF16)
        _acc_rows(dw_ref, dws, first)

    return pl.pallas_call(
        body, name="gdn_pre_bwd", grid=(24, nb),
        in_specs=[pl.BlockSpec((tr, 128), lambda j, i: (i, j)),
                  pl.BlockSpec((8, 128), lambda j, i: (jnp.maximum(i * (tr // 8) - 1, 0), j)),
                  pl.BlockSpec((8, 128), lambda j, i: (jnp.minimum((i + 1) * (tr // 8), T // 8 - 1), j)),
                  pl.BlockSpec((1, tr, 128), lambda j, i: (j, i, 0)),
                  pl.BlockSpec((1, 8, 128), lambda j, i: (j, jnp.minimum((i + 1) * (tr // 8), T // 8 - 1), 0)),
                  pl.BlockSpec((K, 128), lambda j, i: (0, j))],
        out_specs=[pl.BlockSpec((tr, 128), lambda j, i: (i, j)), pl.BlockSpec((K, 128), lambda j, i: (0, j))],
        out_shape=[jax.ShapeDtypeStruct((T, 24 * 128), BF16), jax.ShapeDtypeStruct((K, 24 * 128), F32)],
        compiler_params=_params("parallel", "arbitrary"),
    )(proj, proj, proj, dqkv, dqkv, conv_w)


def _gate_fn(ba, alog_row, dt_row):
    lane = _iota((1, 128), 1)
    x = ba + dt_row
    sp = jnp.maximum(x, 0.0) + jnp.log(1.0 + jnp.exp(-jnp.abs(x)))
    return jnp.where(lane < GDN_HEADS, _sigmoid(ba), -jnp.exp(alog_row) * sp)


def _gdn_gate(ba, alog_row, dt_row, *, tr=512):
    T = ba.shape[0]

    def body(ba_ref, al_ref, dt_ref, b_ref, g_ref):
        val = _gate_fn(ba_ref[...], al_ref[...], dt_ref[...])
        for h in range(GDN_HEADS):
            b_ref[h] = val[:, h:h + 1]
            g_ref[h] = val[:, GDN_HEADS + h:GDN_HEADS + h + 1]

    vec = pl.BlockSpec((1, 128), lambda i: (0, 0))
    hm = pl.BlockSpec((GDN_HEADS, tr, 1), lambda i: (0, i, 0))
    return pl.pallas_call(
        body, name="gdn_gate", grid=(T // tr,), in_specs=[pl.BlockSpec((tr, 128), lambda i: (i, 0)), vec, vec],
        out_specs=[hm, hm], out_shape=[jax.ShapeDtypeStruct((GDN_HEADS, T, 1), F32)] * 2,
        compiler_params=_params("parallel"),
    )(ba, alog_row, dt_row)


def _gdn_gate_bwd(ba, alog_row, dt_row, dbeta, dg, *, tr=512):
    T = ba.shape[0]

    def body(ba_ref, al_ref, dt_ref, db_ref, dg_ref, dba_ref, dal_ref, ddt_ref):
        i = pl.program_id(0)
        lane = _iota((1, 128), 1)
        d = jnp.zeros((tr, 128), F32)
        for h in range(GDN_HEADS):
            d = d + jnp.where(lane == h, db_ref[h], 0.0) + jnp.where(lane == GDN_HEADS + h, dg_ref[h], 0.0)
        _, vjp = jax.vjp(_gate_fn, ba_ref[...], al_ref[...], dt_ref[...])
        dba, dal, ddt = vjp(d)
        dba_ref[...] = dba.astype(BF16)

        @pl.when(i == 0)
        def _():
            dal_ref[...] = dal
            ddt_ref[...] = ddt

        @pl.when(i > 0)
        def _():
            dal_ref[...] += dal
            ddt_ref[...] += ddt

    vec = pl.BlockSpec((1, 128), lambda i: (0, 0))
    hm = pl.BlockSpec((GDN_HEADS, tr, 1), lambda i: (0, i, 0))
    row = pl.BlockSpec((tr, 128), lambda i: (i, 0))
    return pl.pallas_call(
        body, name="gdn_gate_bwd", grid=(T // tr,), in_specs=[row, vec, vec, hm, hm], out_specs=[row, vec, vec],
        out_shape=[jax.ShapeDtypeStruct((T, 128), BF16), jax.ShapeDtypeStruct((1, 128), F32),
                   jax.ShapeDtypeStruct((1, 128), F32)],
        compiler_params=_params("arbitrary"),
    )(ba, alog_row, dt_row, dbeta, dg)


@jax.custom_vjp
def _mmf(a, b):
    return _dotf(a, b, NN)


def _mmf_fwd(a, b):
    return _dotf(a, b, NN), (a, b)


def _mmf_bwd(res, ct):
    a, b = res
    return _dotf(ct, b, NT), _dotf(a, ct, TN)


_mmf.defvjp(_mmf_fwd, _mmf_bwd)


@jax.custom_vjp
def _mmb_nt(a, b):
    return _dot(a, b, NT)


def _mmb_nt_fwd(a, b):
    return _dot(a, b, NT), (a, b)


def _mmb_nt_bwd(res, ct):
    a, b = res
    return _dot(ct, b, NN), _dot(ct, a, TN)


_mmb_nt.defvjp(_mmb_nt_fwd, _mmb_nt_bwd)


def _gdn_chunk(q, k, v, gcol, bcol):
    C = GDN_CHUNK
    row, col = _iota((C, C), 0), _iota((C, C), 1)
    incl, strict = row >= col, row > col
    eye = (row == col).astype(F32)
    lower = incl.astype(F32)
    ones = jnp.ones((C, C), F32)
    gwide = jnp.broadcast_to(gcol, (C, GDN_HEAD_DIM))
    gc = _mmf(lower, gwide)
    gtot = _mmf(ones, gwide)
    gc_c = _mmf(lower, jnp.broadcast_to(gcol, (C, C)))
    gc_s = _mmf(ones, gc_c * eye)
    decay = jnp.where(incl, jnp.exp(jnp.where(incl, gc_c - gc_s, 0.0)), 0.0)
    kb = k * bcol
    a = jnp.where(strict, _mmb_nt(kb, k) * decay, 0.0)
    pw = -a
    tinv = eye + pw
    for _ in range(5):
        pw = _mmf(pw, pw)
        tinv = tinv + _mmf(tinv, pw)
    egc = jnp.exp(gc)
    u = _mmf(tinv, v * bcol)
    w = _mmf(tinv, kb * egc)
    qk = jnp.where(incl, _mmb_nt(q, k) * decay, 0.0)
    return u, w, qk, q * egc, k * jnp.exp(gtot - gc), jnp.exp(gtot)


GDN_ROWS = 8 * GDN_CHUNK


def _gdn_specs(T):
    hd = lambda off: pl.BlockSpec((1, GDN_ROWS, 128), lambda h, i: (off + h, i, 0))
    col = pl.BlockSpec((1, GDN_ROWS, 1), lambda h, i: (h, i, 0))
    sq = pl.BlockSpec((1, GDN_ROWS, GDN_CHUNK), lambda h, i: (h, i, 0))
    gl = pl.BlockSpec((1, 8, 128), lambda h, i: (h, i, 0))
    return hd, col, sq, gl


def _gdn_local(qkv, g, beta):
    T = qkv.shape[1]
    hd, col, sq, gl_spec = _gdn_specs(T)

    def body(q_ref, k_ref, v_ref, g_ref, b_ref, u_ref, w_ref, qk_ref, qd_ref, kd_ref, gl_ref):
        for c in range(8):
            rs = slice(GDN_CHUNK * c, GDN_CHUNK * (c + 1))
            u, w, qk, qd, kd, gl = _gdn_chunk(q_ref[0, rs, :], k_ref[0, rs, :], v_ref[0, rs, :], g_ref[0, rs, :], b_ref[0, rs, :])
            u_ref[0, rs, :] = u
            w_ref[0, rs, :] = w.astype(BF16)
            qk_ref[0, rs, :] = qk.astype(BF16)
            qd_ref[0, rs, :] = qd.astype(BF16)
            kd_ref[0, rs, :] = kd.astype(BF16)
            gl_ref[0, c:c + 1, :] = gl[0:1, :]

    H = GDN_HEADS
    return pl.pallas_call(
        body, name="gdn_local", grid=(H, T // GDN_ROWS),
        in_specs=[hd(0), hd(H), hd(2 * H), col, col],
        out_specs=[hd(0), hd(0), sq, hd(0), hd(0), gl_spec],
        out_shape=[jax.ShapeDtypeStruct((H, T, 128), F32), jax.ShapeDtypeStruct((H, T, 128), BF16),
                   jax.ShapeDtypeStruct((H, T, GDN_CHUNK), BF16), jax.ShapeDtypeStruct((H, T, 128), BF16),
                   jax.ShapeDtypeStruct((H, T, 128), BF16), jax.ShapeDtypeStruct((H, T // GDN_CHUNK, 128), F32)],
        compiler_params=_params("parallel", "parallel"),
    )(qkv, qkv, qkv, g, beta)


def _gdn_local_bwd(qkv, g, beta, du, dw, dqk, dqd, dkd, dgl):
    T = qkv.shape[1]
    hd, col, sq, gl_spec = _gdn_specs(T)

    def body(q_ref, k_ref, v_ref, g_ref, b_ref, du_ref, dw_ref, dqk_ref, dqd_ref, dkd_ref, dgl_ref,
             dq_ref, dk_ref, dv_ref, dg_ref, db_ref):
        row0 = _iota((GDN_CHUNK, 128), 0) == 0
        for c in range(8):
            rs = slice(GDN_CHUNK * c, GDN_CHUNK * (c + 1))
            _, vjp = jax.vjp(_gdn_chunk, q_ref[0, rs, :], k_ref[0, rs, :], v_ref[0, rs, :], g_ref[0, rs, :], b_ref[0, rs, :])
            ct_gl = jnp.where(row0, dgl_ref[0, c:c + 1, :], 0.0)
            dq, dk, dv, dg, db = vjp((du_ref[0, rs, :], dw_ref[0, rs, :], dqk_ref[0, rs, :], dqd_ref[0, rs, :],
                                      dkd_ref[0, rs, :], ct_gl))
            dq_ref[0, rs, :] = dq
            dk_ref[0, rs, :] = dk
            dv_ref[0, rs, :] = dv
            dg_ref[0, rs, :] = dg
            db_ref[0, rs, :] = db

    H = GDN_HEADS
    big = jax.ShapeDtypeStruct((H, T, 128), F32)
    small = jax.ShapeDtypeStruct((H, T, 1), F32)
    return pl.pallas_call(
        body, name="gdn_local_bwd", grid=(H, T // GDN_ROWS),
        in_specs=[hd(0), hd(H), hd(2 * H), col, col, hd(0), hd(0), sq, hd(0), hd(0), gl_spec],
        out_specs=[hd(0), hd(0), hd(0), col, col], out_shape=[big, big, big, small, small],
        compiler_params=_params("parallel", "parallel"),
    )(qkv, qkv, qkv, g, beta, du, dw, dqk, dqd, dkd, dgl)


GDN_HB = 2


def _gdn_scan_specs(T, rev):
    nb = T // GDN_ROWS
    blk = (lambda i: nb - 1 - i) if rev else (lambda i: i)
    hd = pl.BlockSpec((GDN_HB, GDN_ROWS, 128), lambda h, i: (h, blk(i), 0))
    sq = pl.BlockSpec((GDN_HB, GDN_ROWS, GDN_CHUNK), lambda h, i: (h, blk(i), 0))
    gl = pl.BlockSpec((GDN_HB, 8, 128), lambda h, i: (h, blk(i), 0))
    st = pl.BlockSpec((GDN_HB, 8, 128, 128), lambda h, i: (h, blk(i), 0, 0))
    return hd, sq, gl, st


def _gdn_scan(u, w, qk, qd, kd, gl):
    H, T, _ = u.shape
    hd, sq, gl_spec, st = _gdn_scan_specs(T, False)

    def body(u_ref, w_ref, qk_ref, qd_ref, kd_ref, gl_ref, o_ref, ss_ref, vn_ref, s_scr):
        @pl.when(pl.program_id(1) == 0)
        def _():
            s_scr[...] = jnp.zeros_like(s_scr)

        for hh in range(GDN_HB):
            s = s_scr[hh]
            for c in range(8):
                rs = slice(GDN_CHUNK * c, GDN_CHUNK * (c + 1))
                ss_ref[hh, c] = s
                sb = s.astype(BF16)
                vn = u_ref[hh, rs, :] - lax.dot_general(w_ref[hh, rs, :], sb, NN, preferred_element_type=F32)
                vnb = vn.astype(BF16)
                o_ref[hh, rs, :] = (lax.dot_general(qd_ref[hh, rs, :], sb, NN, preferred_element_type=F32)
                                    + lax.dot_general(qk_ref[hh, rs, :], vnb, NN, preferred_element_type=F32))
                vn_ref[hh, rs, :] = vnb
                s = s * gl_ref[hh, c:c + 1, :] + lax.dot_general(kd_ref[hh, rs, :], vnb, TN, preferred_element_type=F32)
            s_scr[hh] = s

    return pl.pallas_call(
        body, name="gdn_scan", grid=(H // GDN_HB, T // GDN_ROWS),
        in_specs=[hd, hd, sq, hd, hd, gl_spec], out_specs=[hd, st, hd],
        out_shape=[jax.ShapeDtypeStruct((H, T, 128), F32), jax.ShapeDtypeStruct((H, T // GDN_CHUNK, 128, 128), F32),
                   jax.ShapeDtypeStruct((H, T, 128), BF16)],
        scratch_shapes=[pltpu.VMEM((GDN_HB, 128, 128), F32)],
        compiler_params=_params("parallel", "arbitrary"),
    )(u, w, qk, qd, kd, gl)


def _gdn_scan_bwd(do, ss, vn, w, qk, qd, kd, gl):
    H, T, _ = do.shape
    hd, sq, gl_spec, st = _gdn_scan_specs(T, True)

    def body(do_ref, ss_ref, vn_ref, w_ref, qk_ref, qd_ref, kd_ref, gl_ref,
             du_ref, dw_ref, dqk_ref, dqd_ref, dkd_ref, dgl_ref, ds_scr):
        @pl.when(pl.program_id(1) == 0)
        def _():
            ds_scr[...] = jnp.zeros_like(ds_scr)

        dot = lambda a, b, dims: lax.dot_general(a, b, dims, preferred_element_type=F32)
        for hh in range(GDN_HB):
            ds = ds_scr[hh]
            for c in reversed(range(8)):
                rs = slice(GDN_CHUNK * c, GDN_CHUNK * (c + 1))
                s = ss_ref[hh, c]
                sb, dsb = s.astype(BF16), ds.astype(BF16)
                dob = do_ref[hh, rs, :].astype(BF16)
                vnb = vn_ref[hh, rs, :]
                dvn = dot(qk_ref[hh, rs, :], dob, TN) + dot(kd_ref[hh, rs, :], dsb, NN)
                dvnb = dvn.astype(BF16)
                du_ref[hh, rs, :] = dvn
                dw_ref[hh, rs, :] = -dot(dvnb, sb, NT)
                dqk_ref[hh, rs, :] = dot(dob, vnb, NT)
                dqd_ref[hh, rs, :] = dot(dob, sb, NT)
                dkd_ref[hh, rs, :] = dot(vnb, dsb, NT)
                dgl_ref[hh, c:c + 1, :] = jnp.sum(ds * s, axis=0, keepdims=True)
                ds = dot(qd_ref[hh, rs, :], dob, TN) + ds * gl_ref[hh, c:c + 1, :] - dot(w_ref[hh, rs, :], dvnb, TN)
            ds_scr[hh] = ds

    big = jax.ShapeDtypeStruct((H, T, 128), F32)
    return pl.pallas_call(
        body, name="gdn_scan_bwd", grid=(H // GDN_HB, T // GDN_ROWS),
        in_specs=[hd, st, hd, hd, sq, hd, hd, gl_spec], out_specs=[hd, hd, sq, hd, hd, gl_spec],
        out_shape=[big, big, jax.ShapeDtypeStruct((H, T, GDN_CHUNK), F32), big, big,
                   jax.ShapeDtypeStruct((H, T // GDN_CHUNK, 128), F32)],
        scratch_shapes=[pltpu.VMEM((GDN_HB, 128, 128), F32)],
        compiler_params=_params("parallel", "arbitrary"),
    )(do, ss, vn, w, qk, qd, kd, gl)


def _gated_norm(o, z, nw):
    on = o * lax.rsqrt(jnp.mean(o * o, axis=-1, keepdims=True) + EPS) * nw
    return on * (z * _sigmoid(z))


def _gdn_post(o, proj, norm_w, *, tr=512):
    T = proj.shape[0]

    def body(o_ref, z_ref, n_ref, y_ref):
        y_ref[...] = _gated_norm(o_ref[0], z_ref[...], n_ref[...]).astype(BF16)

    return pl.pallas_call(
        body, name="gdn_post", grid=(T // tr, GDN_HEADS),
        in_specs=[pl.BlockSpec((1, tr, 128), lambda i, h: (h, i, 0)), pl.BlockSpec((tr, 128), lambda i, h: (i, 24 + h)),
                  pl.BlockSpec((1, 128), lambda i, h: (0, 0))],
        out_specs=pl.BlockSpec((tr, 128), lambda i, h: (i, h)),
        out_shape=jax.ShapeDtypeStruct((T, 1024), BF16), compiler_params=_params("parallel", "parallel"),
    )(o, proj, norm_w)


def _gdn_post_bwd(o, proj, norm_w, dy, *, tr=512):
    T = proj.shape[0]

    def body(o_ref, z_ref, n_ref, dy_ref, do_ref, dz_ref, dn_ref):
        first = (pl.program_id(0) == 0) & (pl.program_id(1) == 0)
        _, vjp = jax.vjp(_gated_norm, o_ref[0], z_ref[...], n_ref[...])
        do, dz, dn = vjp(dy_ref[...])
        do_ref[0] = do
        dz_ref[...] = dz.astype(BF16)

        @pl.when(first)
        def _():
            dn_ref[...] = dn

        @pl.when(jnp.logical_not(first))
        def _():
            dn_ref[...] += dn

    blk = pl.BlockSpec((tr, 128), lambda i, h: (i, h))
    hm = pl.BlockSpec((1, tr, 128), lambda i, h: (h, i, 0))
    vec = pl.BlockSpec((1, 128), lambda i, h: (0, 0))
    return pl.pallas_call(
        body, name="gdn_post_bwd", grid=(T // tr, GDN_HEADS),
        in_specs=[hm, pl.BlockSpec((tr, 128), lambda i, h: (i, 24 + h)), vec, blk], out_specs=[hm, blk, vec],
        out_shape=[jax.ShapeDtypeStruct((GDN_HEADS, T, 128), F32), jax.ShapeDtypeStruct((T, 1024), BF16),
                   jax.ShapeDtypeStruct((1, 128), F32)],
        compiler_params=_params("arbitrary", "arbitrary"),
    )(o, proj, norm_w, dy)


HBM_SPEC = pl.BlockSpec(memory_space=pltpu.HBM)


def _place():
    return lax.axis_index("x"), lax.axis_index("y"), lax.axis_index("c")


def _all_gather(v, *, name):
    def body(v_ref, out_ref, send_sems, recv_sems, local_sem):
        x, y, c = _place()
        me, sibling = (x, y, c), (x, y, 1 - c)
        chips = [(1 - x, y), (x, 1 - y), (1 - x, 1 - y)]

        def slot(px, py, pc):
            return out_ref.at[4 * px + 2 * py + pc]

        def copy(k, block, to, src=None):
            return pltpu.make_async_remote_copy(
                src_ref=slot(*block) if src is None else src, dst_ref=slot(*block),
                send_sem=send_sems.at[k], recv_sem=recv_sems.at[k], device_id=to, device_id_type=MESH)

        mine = pltpu.make_async_copy(v_ref, slot(*me), local_sem)
        mine.start()
        first = [copy(0, me, sibling, src=v_ref)]
        first += [copy(1 + j, me, (*chip, c), src=v_ref) for j, chip in enumerate(chips)]
        for cp in first:
            cp.start()
        passed = [copy(4 + j, (*chip, c), sibling) for j, chip in enumerate(chips)]
        for j, chip in enumerate(chips):
            copy(1 + j, (*chip, c), me).wait_recv()
            passed[j].start()
        copy(0, sibling, me).wait_recv()
        for j, chip in enumerate(chips):
            copy(4 + j, (*chip, 1 - c), me).wait_recv()
        for cp in first + passed:
            cp.wait_send()
        mine.wait()

    return pl.pallas_call(
        body, name=name, out_shape=jax.ShapeDtypeStruct((N_DEV,) + v.shape, v.dtype),
        in_specs=[HBM_SPEC], out_specs=HBM_SPEC,
        scratch_shapes=[pltpu.SemaphoreType.DMA((7,)), pltpu.SemaphoreType.DMA((7,)), pltpu.SemaphoreType.DMA],
    )(v)


def _exchange_sibling(g):
    def body(g_ref, out_ref, send_sems, recv_sems):
        x, y, c = _place()
        sibling = (x, y, 1 - c)
        copies = [pltpu.make_async_remote_copy(
            src_ref=g_ref.at[k, 1 - c], dst_ref=out_ref.at[k], send_sem=send_sems.at[k], recv_sem=recv_sems.at[k],
            device_id=sibling, device_id_type=MESH) for k in range(4)]
        for cp in copies:
            cp.start()
        for cp in copies:
            cp.wait()

    return pl.pallas_call(
        body, name="rs_sibling", out_shape=jax.ShapeDtypeStruct((4,) + g.shape[2:], g.dtype),
        in_specs=[HBM_SPEC], out_specs=HBM_SPEC,
        scratch_shapes=[pltpu.SemaphoreType.DMA((4,)), pltpu.SemaphoreType.DMA((4,))],
    )(g)


def _exchange_chips(pc):
    def body(p_ref, out_ref, send_sems, recv_sems):
        x, y, c = _place()
        chips = [(1 - x, y), (x, 1 - y), (1 - x, 1 - y)]
        copies = [pltpu.make_async_remote_copy(
            src_ref=p_ref.at[2 * cx + cy], dst_ref=out_ref.at[j], send_sem=send_sems.at[j], recv_sem=recv_sems.at[j],
            device_id=(cx, cy, c), device_id_type=MESH) for j, (cx, cy) in enumerate(chips)]
        for cp in copies:
            cp.start()
        for cp in copies:
            cp.wait()

    return pl.pallas_call(
        body, name="rs_chips", out_shape=jax.ShapeDtypeStruct((3,) + pc.shape[1:], pc.dtype),
        in_specs=[HBM_SPEC], out_specs=HBM_SPEC,
        scratch_shapes=[pltpu.SemaphoreType.DMA((3,)), pltpu.SemaphoreType.DMA((3,))],
    )(pc)


def _chip_partial(place, g, got, *, tr):
    R, W = g.shape[2:]

    def body(pl_ref, g_ref, r_ref, o_ref):
        o_ref[...] = (g_ref[0] + r_ref[...]).astype(BF16)

    return pl.pallas_call(
        body, name="rs_chip_partial", out_shape=jax.ShapeDtypeStruct((4, R, W), BF16),
        grid_spec=pltpu.PrefetchScalarGridSpec(
            num_scalar_prefetch=1, grid=(4, R // tr),
            in_specs=[pl.BlockSpec((1, 1, tr, W), lambda k, i, pr: (k, pr[2], i, 0)),
                      pl.BlockSpec((1, tr, W), lambda k, i, pr: (k, i, 0))],
            out_specs=pl.BlockSpec((1, tr, W), lambda k, i, pr: (k, i, 0))),
        compiler_params=_params("parallel", "parallel"),
    )(place, g, got)


def _adamw_math(g, w, m, v):
    m = ADAM_B1 * m + (1.0 - ADAM_B1) * g
    v = ADAM_B2 * v + (1.0 - ADAM_B2) * (g * g)
    m_hat = m / (1.0 - ADAM_B1 ** ADAM_STEP)
    v_hat = v / (1.0 - ADAM_B2 ** ADAM_STEP)
    return -ADAM_LR * (m_hat / (jnp.sqrt(v_hat) + ADAM_EPS) + ADAM_WD * w), m, v


def _adamw_shard(place, g, got1, got2, w, m, v, *, tr):
    R, W = w.shape

    def body(pl_ref, g_ref, r1_ref, r2_ref, w_ref, m_ref, v_ref, go_ref, d_ref, mo_ref, vo_ref):
        gs = g_ref[0, 0] + r1_ref[0]
        for j in range(3):
            gs = gs + r2_ref[j].astype(F32)
        go_ref[...] = gs
        d_ref[...], mo_ref[...], vo_ref[...] = _adamw_math(gs, w_ref[...], m_ref[...], v_ref[...])

    row = pl.BlockSpec((tr, W), lambda i, pr: (i, 0))
    out = jax.ShapeDtypeStruct((R, W), F32)
    return pl.pallas_call(
        body, name="adamw_shard", out_shape=[out] * 4,
        grid_spec=pltpu.PrefetchScalarGridSpec(
            num_scalar_prefetch=1, grid=(R // tr,),
            in_specs=[pl.BlockSpec((1, 1, tr, W), lambda i, pr: (2 * pr[0] + pr[1], pr[2], i, 0)),
                      pl.BlockSpec((1, tr, W), lambda i, pr: (2 * pr[0] + pr[1], i, 0)),
                      pl.BlockSpec((3, tr, W), lambda i, pr: (0, i, 0)), row, row, row],
            out_specs=[row] * 4),
        compiler_params=_params("parallel"),
    )(place, g, got1, got2, w, m, v)


def _adamw_replicated(parts, w, m, v):
    R, W = w.shape

    def body(p_ref, w_ref, m_ref, v_ref, go_ref, d_ref, mo_ref, vo_ref):
        gs = p_ref[0]
        for j in range(1, N_DEV):
            gs = gs + p_ref[j]
        go_ref[...] = gs
        d_ref[...], mo_ref[...], vo_ref[...] = _adamw_math(gs, w_ref[...], m_ref[...], v_ref[...])

    full = pl.BlockSpec((R, W), lambda i: (0, 0))
    out = jax.ShapeDtypeStruct((R, W), F32)
    return pl.pallas_call(
        body, name="adamw_replicated", grid=(1,), out_shape=[out] * 4,
        in_specs=[pl.BlockSpec((N_DEV, R, W), lambda i: (0, 0, 0)), full, full, full], out_specs=[full] * 4,
        compiler_params=_params("arbitrary"),
    )(parts, w, m, v)


PACK_W = 1024
SHARD_LAYOUT = (
    ("w_in_e", 0, (1024, 256), 256, 1), ("w_out_e", 0, (128, 1024), 128, 0),
    ("w_in_o", 0, (1024, 514), 528, 1), ("w_out_o", 0, (128, 1024), 128, 0),
    ("w_up", 0, (1024, 704), 704, 1), ("w_up", 1, (1024, 704), 704, 1),
    ("w_down", 0, (352, 1024), 352, 0), ("w_down", 1, (352, 1024), 352, 0),
    ("w_ple_gate", 0, (128, 1024), 128, 0), ("w_ple_gate", 1, (128, 1024), 128, 0),
    ("w_ple", 0, (256, 128), 32, 1), ("w_ple", 1, (256, 128), 32, 1),
    ("mix_norm_o", 0, (128,), 16, 0), ("conv_qkv_o", 0, (4, 384), 16, 1),
    ("ffn_conv", 0, (3, 704), 16, 1), ("ffn_conv", 1, (3, 704), 16, 1),
)
EXACT_F32 = ("mix_norm_o", "conv_qkv_o", "ffn_conv")
SHARD_ROWS = sum(e[3] for e in SHARD_LAYOUT)
SHARD_TR = 208
REPL_LAYOUT = (
    ("mix_norm_e", (1, 1024), 8), ("pool_w", (1, 4, 128, 128), 64), ("pool_scale", (1, 512), 8),
    ("a_log_o", (1, 8), 8), ("dt_bias_o", (1, 8), 8), ("gdn_norm_o", (1, 128), 8),
    ("ffn_norm", (2, 1024), 8), ("ple_norm", (2, 1024), 8), ("final_norm", (1024,), 8),
)
REPL_ROWS = sum(e[2] for e in REPL_LAYOUT)


def _rows(a, rows):
    flat = a.reshape(-1)
    return jnp.pad(flat, (0, rows * PACK_W - flat.shape[0])).reshape(rows, PACK_W)


def _pack_shards(get, *, as_bf16):
    parts = []
    for name, layer, shape, rows, _ in SHARD_LAYOUT:
        a = get(name, layer).reshape(shape)
        if as_bf16:
            a = lax.bitcast_convert_type(a, BF16) if name in EXACT_F32 else a.astype(BF16)
        parts.append(_rows(a, rows))
    return jnp.concatenate(parts, axis=0)


def _unpack_shards(buf):
    out, r0 = {}, 0
    for name, layer, shape, rows, _ in SHARD_LAYOUT:
        n = 1
        for s in shape:
            n *= s
        out[(name, layer)] = buf[r0:r0 + rows].reshape(-1)[:n].reshape(shape)
        r0 += rows
    return out


def _unpack_gathered(gw):
    out, r0 = {}, 0
    for name, layer, shape, rows, axis in SHARD_LAYOUT:
        piece = gw[:, r0:r0 + rows].reshape(N_DEV, -1)
        r0 += rows
        n = 1
        for s in shape:
            n *= s
        if name in EXACT_F32:
            piece = lax.bitcast_convert_type(piece[:, :2 * n].reshape(N_DEV, n, 2), F32)
        else:
            piece = piece[:, :n]
        piece = piece.reshape((N_DEV,) + shape)
        if axis == 0:
            full = piece.reshape((N_DEV * shape[0],) + shape[1:])
        else:
            full = jnp.moveaxis(piece, 0, 1).reshape(shape[0], N_DEV * shape[1])
        out[(name, layer)] = full
    return out


def _pack_grads(grads):
    parts = []
    for name, layer, shape, rows, axis in SHARD_LAYOUT:
        gfull = grads[(name, layer)]
        if axis == 0:
            sh = gfull.reshape((N_DEV,) + shape)
        else:
            sh = jnp.moveaxis(gfull.reshape(shape[0], N_DEV, shape[1]), 1, 0)
        flat = sh.reshape(N_DEV, -1)
        parts.append(jnp.pad(flat, ((0, 0), (0, rows * PACK_W - flat.shape[1]))).reshape(N_DEV, rows, PACK_W))
    return jnp.concatenate(parts, axis=1)


def _pack_repl(vals):
    return jnp.concatenate([_rows(vals[name].reshape(shape), rows) for name, shape, rows in REPL_LAYOUT], axis=0)


def _unpack_repl(buf):
    out, r0 = {}, 0
    for name, shape, rows in REPL_LAYOUT:
        n = 1
        for s in shape:
            n *= s
        out[name] = buf[r0:r0 + rows].reshape(-1)[:n].reshape(shape)
        r0 += rows
    return out


WEIGHTS = ("mix_norm_e", "w_in_e", "pool_w", "pool_scale", "w_out_e", "mix_norm_o", "w_in_o", "conv_qkv_o", "a_log_o",
           "dt_bias_o", "gdn_norm_o", "w_out_o", "ffn_norm", "w_up", "ffn_conv", "w_down", "ple_norm", "w_ple_gate",
           "w_ple", "final_norm")
SHARDED = tuple(dict.fromkeys(e[0] for e in SHARD_LAYOUT))


def _ffn_forward(x, norm_g, w_up, conv_w, w_down, tag):
    hn = _rms_fwd(x, norm_g, name="rms_ffn" + tag)
    up = _mm(hn, w_up, tm=1024, tn=512, name="ffn_up" + tag)
    act = _ffn_act(up, conv_w)
    out = _mm(act, w_down, res=x, tm=1024, tn=512, tk=1408, name="ffn_down" + tag)
    return out, (x, hn, up, act)


def _ffn_backward(dx, saved, norm_g, w_up, conv_w, w_down, tag):
    x, hn, up, act = saved
    dact = _mm(dx, w_down, tb=True, tm=512, tn=1408, name="ffn_dact" + tag)
    d_w_down = _mm(act, dx, ta=True, tm=1408, tn=512, tk=512, name="ffn_dwdown" + tag)
    dgate, dval, dcg, dcv = _ffn_act_bwd(up, conv_w, dact)
    dhn = _mm(dgate, w_up[:, :FFN_DIM], tb=True, tm=1024, tn=512, tk=1408, name="ffn_dhn_g" + tag)
    dhn = _mm(dval, w_up[:, FFN_DIM:], tb=True, res=dhn, tm=1024, tn=512, tk=1408, name="ffn_dhn_v" + tag)
    d_w_up = jnp.concatenate([_mm(hn, dgate, ta=True, tm=1024, tn=1408, tk=512, name="ffn_dwup_g" + tag),
                              _mm(hn, dval, ta=True, tm=1024, tn=1408, tk=512, name="ffn_dwup_v" + tag)], axis=1)
    dx, d_norm = _rms_bwd(x, dhn, norm_g, dx, name="rms_ffn_bwd" + tag)
    return dx, d_norm, d_w_up, jnp.concatenate([dcg, dcv], axis=1), d_w_down


def _ple_forward(x, norm_g, w_gate, p, w_ple, tag):
    hn = _rms_fwd(x, norm_g, name="rms_ple" + tag)
    out, gl, pe = _ple_fwd(hn, w_gate, p, w_ple, x, name="ple_fwd" + tag)
    return out, (x, hn, gl, pe)


def _ple_backward(dx, saved, norm_g, w_gate, p, tag):
    x, hn, gl, pe = saved
    dpe, dgl = _ple_bwd(dx, gl, pe, name="ple_bwd" + tag)
    d_w_ple = _mm(p, dpe, ta=True, tm=256, tn=512, tk=512, name="ple_dwple" + tag)
    d_w_gate = _mm(hn, dgl, ta=True, tm=1024, tn=512, tk=512, name="ple_dwgate" + tag)
    dhn = _mm(dgl, w_gate, tb=True, tm=1024, tn=512, name="ple_dhn" + tag)
    dx, d_norm = _rms_bwd(x, dhn, norm_g, dx, name="rms_ple_bwd" + tag)
    return dx, d_norm, d_w_gate, d_w_ple


def kernel(x, p, mix_norm_e, w_in_e, pool_w, pool_scale, w_out_e, mix_norm_o, w_in_o, conv_qkv_o, a_log_o, dt_bias_o, gdn_norm_o, w_out_o, ffn_norm, w_up, ffn_conv, w_down, ple_norm, w_ple_gate, w_ple, final_norm, loss_target, m_mix_norm_e, m_w_in_e, m_pool_w, m_pool_scale, m_w_out_e, m_mix_norm_o, m_w_in_o, m_conv_qkv_o, m_a_log_o, m_dt_bias_o, m_gdn_norm_o, m_w_out_o, m_ffn_norm, m_w_up, m_ffn_conv, m_w_down, m_ple_norm, m_w_ple_gate, m_w_ple, m_final_norm, v_mix_norm_e, v_w_in_e, v_pool_w, v_pool_scale, v_w_out_e, v_mix_norm_o, v_w_in_o, v_conv_qkv_o, v_a_log_o, v_dt_bias_o, v_gdn_norm_o, v_w_out_o, v_ffn_norm, v_w_up, v_ffn_conv, v_w_down, v_ple_norm, v_w_ple_gate, v_w_ple, v_final_norm):
    given = dict(locals())
    place = jnp.stack(_place()).astype(jnp.int32)
    x0, tgt = x[0], loss_target[0]

    def shard_of(prefix):
        def get(name, layer):
            a = given[prefix + name]
            return a[layer] if a.ndim == 3 or name in ("ffn_conv",) else a
        return get

    full = _unpack_gathered(_all_gather(_pack_shards(shard_of(""), as_bf16=True), name="ag_weights"))
    w_in_o_full = full[("w_in_o", 0)]
    w_in_o_main = w_in_o_full[:, :4096]
    w_in_o_ba = jnp.pad(w_in_o_full[:, 4096:], ((0, 0), (0, 112)))
    mix_norm_o_full = full[("mix_norm_o", 0)].reshape(1, D_MODEL)
    conv_qkv = full[("conv_qkv_o", 0)]
    alog_row = jnp.pad(a_log_o, ((0, 0), (8, 112)))
    dt_row = jnp.pad(dt_bias_o, ((0, 0), (8, 112)))
    lw = lambda name, i: full[(name, i)]

    h_e = _rms_fwd(x0, mix_norm_e, name="rms_mix_e")
    proj_e = _mm(h_e, lw("w_in_e", 0), tm=1024, tn=512, name="in_e")
    pool_o = _pool_fwd(proj_e, pool_w[0], pool_scale)
    att_o, lsum = _sb_fwd(proj_e)
    mix_e = jnp.concatenate([pool_o, att_o.astype(BF16)], axis=1)
    x1 = _mm(mix_e, lw("w_out_e", 0), res=x0, tm=1024, tn=512, name="out_e")
    x2, ffn0 = _ffn_forward(x1, ffn_norm[0:1], lw("w_up", 0), lw("ffn_conv", 0), lw("w_down", 0), "0")
    x3, ple0 = _ple_forward(x2, ple_norm[0:1], lw("w_ple_gate", 0), p[0, 0], lw("w_ple", 0), "0")

    h_o = _rms_fwd(x3, mix_norm_o_full, name="rms_mix_o")
    proj_o = _mm(h_o, w_in_o_main, tm=1024, tn=512, name="in_o")
    ba = _mm(h_o, w_in_o_ba, tm=1024, tn=128, name="in_o_ba")
    qkv = _gdn_pre(proj_o, conv_qkv)
    beta, g = _gdn_gate(ba, alog_row, dt_row)
    u, w_c, qk, qd, kd, gl = _gdn_local(qkv, g, beta)
    o, states, vnew = _gdn_scan(u, w_c, qk, qd, kd, gl)
    y_o = _gdn_post(o, proj_o, gdn_norm_o)
    x4 = _mm(y_o, lw("w_out_o", 0), res=x3, tm=1024, tn=512, name="out_o")
    x5, ffn1 = _ffn_forward(x4, ffn_norm[1:2], lw("w_up", 1), lw("ffn_conv", 1), lw("w_down", 1), "1")
    x6, ple1 = _ple_forward(x5, ple_norm[1:2], lw("w_ple_gate", 1), p[1, 0], lw("w_ple", 1), "1")
    loss_row, dx, d_final = _final_loss(x6, final_norm.reshape(1, D_MODEL), tgt)

    grads, rgrads = {}, {}
    dx, d_ple1, grads[("w_ple_gate", 1)], grads[("w_ple", 1)] = _ple_backward(dx, ple1, ple_norm[1:2], lw("w_ple_gate", 1), p[1, 0], "1")
    dx, d_ffn1, grads[("w_up", 1)], grads[("ffn_conv", 1)], grads[("w_down", 1)] = _ffn_backward(
        dx, ffn1, ffn_norm[1:2], lw("w_up", 1), lw("ffn_conv", 1), lw("w_down", 1), "1")
    grads[("w_out_o", 0)] = _mm(y_o, dx, ta=True, tm=1024, tn=512, tk=512, name="dw_out_o")
    dy_o = _mm(dx, lw("w_out_o", 0), tb=True, tm=1024, tn=512, name="dy_o")
    do, dz, rgrads["gdn_norm_o"] = _gdn_post_bwd(o, proj_o, gdn_norm_o, dy_o)
    du, dw_c, dqk, dqd, dkd, dgl = _gdn_scan_bwd(do, states, vnew, w_c, qk, qd, kd, gl)
    dq, dk, dv, dg, dbeta = _gdn_local_bwd(qkv, g, beta, du, dw_c, dqk, dqd, dkd, dgl)
    dqkv, grads[("conv_qkv_o", 0)] = _gdn_pre_bwd(proj_o, conv_qkv, jnp.concatenate([dq, dk, dv], axis=0))
    dba, d_alog, d_dt = _gdn_gate_bwd(ba, alog_row, dt_row, dbeta, dg)
    rgrads["a_log_o"], rgrads["dt_bias_o"] = d_alog[:, 8:16], d_dt[:, 8:16]
    dproj_o = jnp.concatenate([dqkv, dz], axis=1)
    dh = _mm(dproj_o, w_in_o_main, tb=True, tm=1024, tn=512, tk=1024, name="dh_o")
    dh = _mm(dba, w_in_o_ba, tb=True, res=dh, tm=1024, tn=512, name="dh_o_ba")
    grads[("w_in_o", 0)] = jnp.concatenate(
        [_mm(h_o, dproj_o, ta=True, tm=1024, tn=512, tk=512, name="dw_in_o"),
         _mm(h_o, dba, ta=True, tm=1024, tn=128, tk=512, name="dw_in_o_ba")[:, :16]], axis=1)
    dx, d_mix_o = _rms_bwd(x3, dh, mix_norm_o_full, dx, name="rms_mix_o_bwd")
    grads[("mix_norm_o", 0)] = d_mix_o.reshape(D_MODEL)

    dx, d_ple0, grads[("w_ple_gate", 0)], grads[("w_ple", 0)] = _ple_backward(dx, ple0, ple_norm[0:1], lw("w_ple_gate", 0), p[0, 0], "0")
    dx, d_ffn0, grads[("w_up", 0)], grads[("ffn_conv", 0)], grads[("w_down", 0)] = _ffn_backward(
        dx, ffn0, ffn_norm[0:1], lw("w_up", 0), lw("ffn_conv", 0), lw("w_down", 0), "0")
    grads[("w_out_e", 0)] = _mm(mix_e, dx, ta=True, tm=1024, tn=512, tk=512, name="dw_out_e")
    dmix = _mm(dx, lw("w_out_e", 0), tb=True, tm=1024, tn=512, name="dmix_e")
    du_e, d_pool_w, rgrads["pool_scale"] = _pool_bwd(proj_e, dmix, pool_w[0], pool_scale)
    rgrads["pool_w"] = d_pool_w[None]
    dq_e, dk_e, dv_e = _sb_bwd(proj_e, lsum, dmix)
    dproj_e = jnp.concatenate([du_e, dq_e.astype(BF16), dk_e.astype(BF16), dv_e.astype(BF16)], axis=1)
    dh = _mm(dproj_e, lw("w_in_e", 0), tb=True, tm=1024, tn=512, tk=1024, name="dh_e")
    grads[("w_in_e", 0)] = _mm(h_e, dproj_e, ta=True, tm=1024, tn=512, tk=512, name="dw_in_e")
    dx, rgrads["mix_norm_e"] = _rms_bwd(x0, dh, mix_norm_e, dx, name="rms_mix_e_bwd")
    rgrads["ffn_norm"] = jnp.concatenate([d_ffn0, d_ffn1], axis=0)
    rgrads["ple_norm"] = jnp.concatenate([d_ple0, d_ple1], axis=0)
    rgrads["final_norm"] = d_final.reshape(D_MODEL)

    gpack = _pack_grads(grads).reshape(4, 2, SHARD_ROWS, PACK_W)
    got1 = _exchange_sibling(gpack)
    got2 = _exchange_chips(_chip_partial(place, gpack, got1, tr=SHARD_TR))
    sh_out = _adamw_shard(place, gpack, got1, got2, _pack_shards(shard_of(""), as_bf16=False),
                          _pack_shards(shard_of("m_"), as_bf16=False), _pack_shards(shard_of("v_"), as_bf16=False), tr=SHARD_TR)
    sh_out = [_unpack_shards(b) for b in sh_out]

    rparts = _all_gather(_pack_repl(rgrads), name="ag_repl_grads")
    rp_out = _adamw_replicated(rparts, _pack_repl({n: given[n] for n, _, _ in REPL_LAYOUT}),
                               _pack_repl({n: given["m_" + n] for n, _, _ in REPL_LAYOUT}),
                               _pack_repl({n: given["v_" + n] for n, _, _ in REPL_LAYOUT}))
    rp_out = [_unpack_repl(b) for b in rp_out]

    def leaf(kind, name):
        if name in SHARDED:
            ref = given[name]
            if ref.ndim == 3 and ref.shape[0] == 2:
                return jnp.stack([sh_out[kind][(name, 0)], sh_out[kind][(name, 1)]], axis=0)
            return sh_out[kind][(name, 0)].reshape(ref.shape)
        return rp_out[kind][name]

    loss = lax.psum(loss_row[0, 0], ("x", "y", "c"))
    outs = [loss, dx[None]]
    for kind in range(4):
        outs += [leaf(kind, n) for n in WEIGHTS]
    return tuple(outs)
```

```python
import functools

import jax
import jax.numpy as jnp
from jax import lax
from jax.experimental import pallas as pl
from jax.experimental.pallas import tpu as pltpu

F32 = jnp.float32
BF16 = jnp.bfloat16

D_MODEL = 1024
PLE_DIM = 256
POOL_WINDOWS = (2, 4, 8, 16)
POOL_WIDTH = 512
SB_HEAD_DIM = 64
SB_BLOCK = 256
GDN_HEADS = 8
GDN_HEAD_DIM = 128
GDN_CONV = 4
GDN_CHUNK = 64
FFN_DIM = 2816
FFN_CONV = 3
EPS = 1e-6
ADAM_LR, ADAM_B1, ADAM_B2, ADAM_EPS, ADAM_WD, ADAM_STEP = 0.001, 0.9, 0.999, 1e-08, 0.01, 10
N_DEV = 8
MESH = pl.DeviceIdType.MESH
VMEM_LIMIT = 56 * 1024 * 1024

NN = (((1,), (0,)), ((), ()))
NT = (((1,), (1,)), ((), ()))
TN = (((0,), (0,)), ((), ()))


def _params(*sem):
    return pltpu.CompilerParams(dimension_semantics=sem if sem else None, vmem_limit_bytes=VMEM_LIMIT)


def _dot(a, b, dims):
    return lax.dot_general(a.astype(BF16), b.astype(BF16), dims, preferred_element_type=F32)


def _iota(shape, axis):
    return lax.broadcasted_iota(jnp.int32, shape, axis)


def _mm(a, b, *, ta=False, tb=False, res=None, out_dtype=F32, tm=512, tn=512, tk=None, name):
    M, K = (a.shape[1], a.shape[0]) if ta else a.shape
    N = b.shape[0] if tb else b.shape[1]
    tk = K if tk is None else min(tk, K)
    tm, tn = min(tm, M), min(tn, N)
    assert M % tm == 0 and N % tn == 0 and K % tk == 0, (name, M, N, K, tm, tn, tk)
    nk = K // tk
    dims = (((0 if ta else 1,), (1 if tb else 0,)), ((), ()))

    def body(*refs):
        if res is None:
            a_ref, b_ref, o_ref, *scr = refs
            r_ref = None
        else:
            a_ref, b_ref, r_ref, o_ref, *scr = refs
        p = _dot(a_ref[...], b_ref[...], dims)

        def fin(acc):
            if r_ref is not None:
                acc = acc + r_ref[...]
            o_ref[...] = acc.astype(out_dtype)

        if nk == 1:
            fin(p)
        else:
            acc_ref = scr[0]
            k = pl.program_id(2)

            @pl.when(k == 0)
            def _():
                acc_ref[...] = p

            @pl.when(k > 0)
            def _():
                acc_ref[...] += p

            @pl.when(k == nk - 1)
            def _():
                fin(acc_ref[...])

    a_spec = pl.BlockSpec((tk, tm), lambda i, j, k: (k, i)) if ta else pl.BlockSpec((tm, tk), lambda i, j, k: (i, k))
    b_spec = pl.BlockSpec((tn, tk), lambda i, j, k: (j, k)) if tb else pl.BlockSpec((tk, tn), lambda i, j, k: (k, j))
    o_spec = pl.BlockSpec((tm, tn), lambda i, j, k: (i, j))
    in_specs = [a_spec, b_spec] + ([o_spec] if res is not None else [])
    args = (a, b) + ((res,) if res is not None else ())
    return pl.pallas_call(
        body, name=name, grid=(M // tm, N // tn, nk), in_specs=in_specs, out_specs=o_spec,
        out_shape=jax.ShapeDtypeStruct((M, N), out_dtype),
        scratch_shapes=[pltpu.VMEM((tm, tn), F32)] if nk > 1 else [],
        compiler_params=_params("parallel", "parallel", "arbitrary"),
    )(*args)


def _rms_fwd(x, gain, *, name, tr=512):
    T, Dm = x.shape

    def body(x_ref, g_ref, o_ref):
        xv = x_ref[...]
        r = lax.rsqrt(jnp.mean(xv * xv, axis=-1, keepdims=True) + EPS)
        o_ref[...] = (xv * r * g_ref[...]).astype(BF16)

    return pl.pallas_call(
        body, name=name, grid=(T // tr,),
        in_specs=[pl.BlockSpec((tr, Dm), lambda i: (i, 0)), pl.BlockSpec((1, Dm), lambda i: (0, 0))],
        out_specs=pl.BlockSpec((tr, Dm), lambda i: (i, 0)),
        out_shape=jax.ShapeDtypeStruct((T, Dm), BF16), compiler_params=_params("parallel"),
    )(x, gain)


def _rms_bwd(x, dy, gain, dres, *, name, tr=512):
    T, Dm = x.shape

    def body(x_ref, dy_ref, g_ref, dres_ref, dx_ref, dg_ref):
        i = pl.program_id(0)
        xv = x_ref[...]
        dy_v = dy_ref[...].astype(F32)
        r = lax.rsqrt(jnp.mean(xv * xv, axis=-1, keepdims=True) + EPS)
        xn = xv * r
        dgp = jnp.sum(dy_v * xn, axis=0, keepdims=True)
        dyg = dy_v * g_ref[...]
        dx = r * (dyg - xn * jnp.mean(dyg * xn, axis=-1, keepdims=True))
        dx_ref[...] = dres_ref[...] + dx

        @pl.when(i == 0)
        def _():
            dg_ref[...] = dgp

        @pl.when(i > 0)
        def _():
            dg_ref[...] += dgp

    row = pl.BlockSpec((tr, Dm), lambda i: (i, 0))
    vec = pl.BlockSpec((1, Dm), lambda i: (0, 0))
    return pl.pallas_call(
        body, name=name, grid=(T // tr,), in_specs=[row, row, vec, row], out_specs=[row, vec],
        out_shape=[jax.ShapeDtypeStruct((T, Dm), F32), jax.ShapeDtypeStruct((1, Dm), F32)],
        compiler_params=_params("arbitrary"),
    )(x, dy, gain, dres)


def _final_loss(x, gain, target, *, tr=512):
    T, Dm = x.shape

    def body(x_ref, g_ref, t_ref, loss_ref, dx_ref, dg_ref):
        i = pl.program_id(0)
        xv = x_ref[...]
        g = g_ref[...]
        r = lax.rsqrt(jnp.mean(xv * xv, axis=-1, keepdims=True) + EPS)
        xn = xv * r
        err = xn * g - t_ref[...]
        lp = jnp.zeros((1, 128), F32) + 0.5 * jnp.sum(jnp.mean(err * err, axis=-1, keepdims=True))
        dy_v = err * (1.0 / Dm)
        dgp = jnp.sum(dy_v * xn, axis=0, keepdims=True)
        dyg = dy_v * g
        dx_ref[...] = r * (dyg - xn * jnp.mean(dyg * xn, axis=-1, keepdims=True))

        @pl.when(i == 0)
        def _():
            dg_ref[...] = dgp
            loss_ref[...] = lp

        @pl.when(i > 0)
        def _():
            dg_ref[...] += dgp
            loss_ref[...] += lp

    row = pl.BlockSpec((tr, Dm), lambda i: (i, 0))
    vec = pl.BlockSpec((1, Dm), lambda i: (0, 0))
    return pl.pallas_call(
        body, name="final_loss", grid=(T // tr,), in_specs=[row, vec, row],
        out_specs=[pl.BlockSpec((1, 128), lambda i: (0, 0)), row, vec],
        out_shape=[jax.ShapeDtypeStruct((1, 128), F32), jax.ShapeDtypeStruct((T, Dm), F32),
                   jax.ShapeDtypeStruct((1, Dm), F32)],
        compiler_params=_params("arbitrary"),
    )(x, gain, target)


def _prev_spec(tr, cb, pad, col):
    return pl.BlockSpec((pad, cb), lambda *g: (jnp.maximum(g[0] * (tr // pad) - 1, 0), col(*g)))


def _next_spec(tr, cb, pad, col, T):
    return pl.BlockSpec((pad, cb), lambda *g: (jnp.minimum((g[0] + 1) * (tr // pad), T // pad - 1), col(*g)))


def _conv_rows(x_ext, w_ref, K, pad):
    y = w_ref[K - 1:K, :] * x_ext
    for i in range(K - 1):
        y = y + w_ref[i:i + 1, :] * pltpu.roll(x_ext, K - 1 - i, 0)
    return y[pad:]


def _pool_y(u_ext, g, i, tr):
    s = u_ext
    for sh in (1, 2, 4, 8)[:g + 1]:
        s = s + pltpu.roll(s, sh, 0)
    t = i * tr + _iota((tr, 128), 0)
    cnt = jnp.minimum(t + 1, POOL_WINDOWS[g]).astype(F32)
    return s[16:] / cnt - u_ext[16:]


def _pool_fwd(proj, pool_w, pool_scale, *, tr=512):
    T = proj.shape[0]

    def body(u_ref, uh_ref, w_ref, s_ref, o_ref):
        i = pl.program_id(0)
        uh = jnp.where(i > 0, uh_ref[...], 0.0)
        for g in range(4):
            cs = slice(128 * g, 128 * (g + 1))
            y = _pool_y(jnp.concatenate([uh[:, cs], u_ref[:, cs]], axis=0), g, i, tr)
            o_ref[:, cs] = (_dot(y, w_ref[g], NN) * s_ref[:, cs]).astype(BF16)

    return pl.pallas_call(
        body, name="pool_fwd", grid=(T // tr,),
        in_specs=[pl.BlockSpec((tr, 512), lambda i: (i, 0)), _prev_spec(tr, 512, 16, lambda i: 0),
                  pl.BlockSpec((4, 128, 128), lambda i: (0, 0, 0)), pl.BlockSpec((1, 512), lambda i: (0, 0))],
        out_specs=pl.BlockSpec((tr, 512), lambda i: (i, 0)),
        out_shape=jax.ShapeDtypeStruct((T, 512), BF16), compiler_params=_params("parallel"),
    )(proj, proj, pool_w, pool_scale)


def _pool_bwd(proj, dout, pool_w, pool_scale, *, tr=512):
    T = proj.shape[0]
    nb = T // tr

    def body(u_ref, uh_ref, d_ref, dn_ref, w_ref, s_ref, du_ref, dw_ref, ds_ref):
        i = pl.program_id(0)
        uh = jnp.where(i > 0, uh_ref[...], 0.0)
        dn = jnp.where(i < nb - 1, dn_ref[...], 0.0)
        t_ext = i * tr + _iota((tr + 16, 128), 0)
        for g in range(4):
            cs = slice(128 * g, 128 * (g + 1))
            sc = s_ref[:, cs]
            wg = w_ref[g]
            y = _pool_y(jnp.concatenate([uh[:, cs], u_ref[:, cs]], axis=0), g, i, tr)
            dg = d_ref[:, cs]
            dsp = jnp.sum(dg * _dot(y, wg, NN), axis=0, keepdims=True)
            dyw = dg * sc
            dwp = _dot(y, dyw, TN)
            dy_ext = _dot(jnp.concatenate([dyw, dn[:, cs] * sc], axis=0), wg, NT)
            cnt = jnp.minimum(t_ext + 1, POOL_WINDOWS[g]).astype(F32)
            s = dy_ext / cnt
            for sh in (1, 2, 4, 8)[:g + 1]:
                s = s + pltpu.roll(s, tr + 16 - sh, 0)
            du_ref[:, cs] = (s[:tr] - dy_ext[:tr]).astype(BF16)

            @pl.when(i == 0)
            def _():
                dw_ref[g] = dwp
                ds_ref[:, cs] = dsp

            @pl.when(i > 0)
            def _():
                dw_ref[g] += dwp
                ds_ref[:, cs] += dsp

    row = pl.BlockSpec((tr, 512), lambda i: (i, 0))
    return pl.pallas_call(
        body, name="pool_bwd", grid=(nb,),
        in_specs=[row, _prev_spec(tr, 512, 16, lambda i: 0), row, _next_spec(tr, 512, 16, lambda i: 0, T),
                  pl.BlockSpec((4, 128, 128), lambda i: (0, 0, 0)), pl.BlockSpec((1, 512), lambda i: (0, 0))],
        out_specs=[row, pl.BlockSpec((4, 128, 128), lambda i: (0, 0, 0)), pl.BlockSpec((1, 512), lambda i: (0, 0))],
        out_shape=[jax.ShapeDtypeStruct((T, 512), BF16), jax.ShapeDtypeStruct((4, 128, 128), F32),
                   jax.ShapeDtypeStruct((1, 512), F32)],
        compiler_params=_params("arbitrary"),
    )(proj, proj, dout, dout, pool_w, pool_scale)


def _split_dot(x, tri):
    hi = x.astype(BF16)
    lo = (x - hi.astype(F32)).astype(BF16)
    return (lax.dot_general(hi, tri, NN, preferred_element_type=F32)
            + lax.dot_general(lo, tri, NN, preferred_element_type=F32))


def _log1m(z):
    return -(jnp.maximum(z, 0.0) + jnp.log(1.0 + jnp.exp(-jnp.abs(z))))


def _sb_fwd(proj):
    T = proj.shape[0]
    B = SB_BLOCK
    scale = SB_HEAD_DIM ** -0.5

    def body(q_ref, k_ref, v_ref, o_ref, ls_ref):
        i = pl.program_id(1)
        lane = _iota((1, 128), 1)
        row, col = _iota((B, B), 0), _iota((B, B), 1)
        tri_gt = (row > col).astype(BF16)
        valid = col < row
        qv = q_ref[...] * scale
        hms = [(lane >= 64 * h) & (lane < 64 * (h + 1)) for h in range(2)]
        qhs = [jnp.where(hm, qv, 0.0).astype(BF16) for hm in hms]

        def tile(j, carry, diag):
            rows = pl.ds(pl.multiple_of(j * B, B), B)
            kj = k_ref[rows, :].astype(BF16)
            vj = v_ref[rows, :].astype(BF16)
            out = []
            for h in range(2):
                c, acc = carry[h]
                z = lax.dot_general(qhs[h], kj, NT, preferred_element_type=F32)
                lg = _log1m(z)
                if diag:
                    lg = jnp.where(valid, lg, 0.0)
                a = jnp.exp(z + lg + _split_dot(lg, tri_gt) + c)
                if diag:
                    a = jnp.where(valid, a, 0.0)
                acc = acc + lax.dot_general(a.astype(BF16), vj, NN, preferred_element_type=F32)
                out.append((c + jnp.sum(lg, axis=1, keepdims=True), acc))
            return tuple(out)

        zero = (jnp.zeros((B, 1), F32), jnp.zeros((B, 128), F32))
        carry = tile(i, (zero, zero), True)
        carry = lax.fori_loop(0, i, lambda s, cr: tile(i - 1 - s, cr, False), carry)
        o_ref[...] = jnp.where(hms[0], carry[0][1], carry[1][1])
        ls_ref[...] = jnp.where(hms[0], carry[0][0], carry[1][0])

    blk = pl.BlockSpec((B, 128), lambda hp, i: (i, hp))
    return pl.pallas_call(
        body, name="sb_fwd", grid=(4, T // B),
        in_specs=[pl.BlockSpec((B, 128), lambda hp, i: (i, 4 + hp)),
                  pl.BlockSpec((T, 128), lambda hp, i: (0, 8 + hp)),
                  pl.BlockSpec((T, 128), lambda hp, i: (0, 12 + hp))],
        out_specs=[blk, blk],
        out_shape=[jax.ShapeDtypeStruct((T, 512), F32)] * 2, compiler_params=_params("parallel", "parallel"),
    )(proj, proj, proj)


def _sb_bwd(proj, lsum, dout):
    T = proj.shape[0]
    B = SB_BLOCK
    scale = SB_HEAD_DIM ** -0.5

    def body(q_ref, k_ref, v_ref, do_ref, ls_ref, dq_ref, dk_ref, dv_ref):
        i = pl.program_id(1)

        @pl.when(i == 0)
        def _():
            dk_ref[...] = jnp.zeros_like(dk_ref)
            dv_ref[...] = jnp.zeros_like(dv_ref)

        lane = _iota((1, 128), 1)
        row, col = _iota((B, B), 0), _iota((B, B), 1)
        tri_le = (row <= col).astype(BF16)
        tri_lt = (row < col).astype(BF16)
        valid = col < row
        qv = q_ref[...] * scale
        dov = do_ref[...]
        hms = [(lane >= 64 * h) & (lane < 64 * (h + 1)) for h in range(2)]
        qhs = [jnp.where(hm, qv, 0.0).astype(BF16) for hm in hms]
        dos = [jnp.where(hm, dov, 0.0).astype(BF16) for hm in hms]
        ltots = [ls_ref[:, 64 * h:64 * h + 1] for h in range(2)]

        def tile(j, carry, diag):
            rows = pl.ds(pl.multiple_of(j * B, B), B)
            kj = k_ref[rows, :].astype(BF16)
            vj = v_ref[rows, :].astype(BF16)
            out = []
            dkj = jnp.zeros((B, 128), F32)
            dvj = jnp.zeros((B, 128), F32)
            for h in range(2):
                lbef, ebef, dqa = carry[h]
                z = lax.dot_general(qhs[h], kj, NT, preferred_element_type=F32)
                lg = _log1m(z)
                if diag:
                    lg = jnp.where(valid, lg, 0.0)
                a = jnp.exp(z + lg + (ltots[h] - lbef - _split_dot(lg, tri_le)))
                if diag:
                    a = jnp.where(valid, a, 0.0)
                e = a * lax.dot_general(dos[h], vj, NT, preferred_element_type=F32)
                dz = e * jnp.exp(lg) - jnp.exp(z + lg) * (ebef + _split_dot(e, tri_lt))
                if diag:
                    dz = jnp.where(valid, dz, 0.0)
                dzb = dz.astype(BF16)
                dqa = dqa + lax.dot_general(dzb, kj, NN, preferred_element_type=F32)
                dkj = dkj + lax.dot_general(dzb, qhs[h], TN, preferred_element_type=F32)
                dvj = dvj + lax.dot_general(a.astype(BF16), dos[h], TN, preferred_element_type=F32)
                out.append((lbef + jnp.sum(lg, axis=1, keepdims=True), ebef + jnp.sum(e, axis=1, keepdims=True), dqa))
            dk_ref[rows, :] += dkj
            dv_ref[rows, :] += dvj
            return tuple(out)

        zero = (jnp.zeros((B, 1), F32), jnp.zeros((B, 1), F32), jnp.zeros((B, 128), F32))
        carry = lax.fori_loop(0, i, lambda j, cr: tile(j, cr, False), (zero, zero))
        carry = tile(i, carry, True)
        dq_ref[...] = jnp.where(hms[0], carry[0][2], carry[1][2]) * scale

    full = pl.BlockSpec((T, 128), lambda hp, i: (0, hp))
    blk = pl.BlockSpec((B, 128), lambda hp, i: (i, hp))
    return pl.pallas_call(
        body, name="sb_bwd", grid=(4, T // B),
        in_specs=[pl.BlockSpec((B, 128), lambda hp, i: (i, 4 + hp)),
                  pl.BlockSpec((T, 128), lambda hp, i: (0, 8 + hp)),
                  pl.BlockSpec((T, 128), lambda hp, i: (0, 12 + hp)),
                  pl.BlockSpec((B, 128), lambda hp, i: (i, 4 + hp)), blk],
        out_specs=[blk, full, full],
        out_shape=[jax.ShapeDtypeStruct((T, 512), F32)] * 3,
        compiler_params=_params("parallel", "arbitrary"),
    )(proj, proj, proj, dout, lsum)


def _sigmoid(x):
    return 1.0 / (1.0 + jnp.exp(-x))


def _silu_mul(cg, cv):
    return cg * _sigmoid(cg) * cv


def _ffn_act(up, conv_w, *, tr=512, cb=256):
    T, F2 = up.shape
    nc = F2 // 2 // cb
    K = FFN_CONV

    def body(g_ref, gh_ref, v_ref, vh_ref, wg_ref, wv_ref, o_ref):
        i = pl.program_id(0)
        gh = jnp.where(i > 0, gh_ref[...], 0.0)
        vh = jnp.where(i > 0, vh_ref[...], 0.0)
        cg = _conv_rows(jnp.concatenate([gh, g_ref[...]], axis=0), wg_ref, K, 8)
        cv = _conv_rows(jnp.concatenate([vh, v_ref[...]], axis=0), wv_ref, K, 8)
        o_ref[...] = _silu_mul(cg, cv).astype(BF16)

    return pl.pallas_call(
        body, name="ffn_act", grid=(T // tr, nc),
        in_specs=[pl.BlockSpec((tr, cb), lambda i, j: (i, j)), _prev_spec(tr, cb, 8, lambda i, j: j),
                  pl.BlockSpec((tr, cb), lambda i, j: (i, nc + j)), _prev_spec(tr, cb, 8, lambda i, j: nc + j),
                  pl.BlockSpec((K, cb), lambda i, j: (0, j)), pl.BlockSpec((K, cb), lambda i, j: (0, nc + j))],
        out_specs=pl.BlockSpec((tr, cb), lambda i, j: (i, j)),
        out_shape=jax.ShapeDtypeStruct((T, F2 // 2), BF16), compiler_params=_params("parallel", "parallel"),
    )(up, up, up, up, conv_w, conv_w)


def _conv_bwd_rows(dc_ext, x_ext, w_ref, K, tr):
    n = tr + 8
    dx = w_ref[K - 1:K, :] * dc_ext
    for i in range(K - 1):
        dx = dx + w_ref[i:i + 1, :] * pltpu.roll(dc_ext, n - (K - 1 - i), 0)
    dc = dc_ext[:tr]
    dws = [jnp.sum(dc * pltpu.roll(x_ext, K - 1 - i, 0)[8:8 + tr], axis=0, keepdims=True) for i in range(K)]
    return dx[:tr], dws


def _acc_rows(ref, rows, first):
    for i, r in enumerate(rows):
        @pl.when(first)
        def _():
            ref[i:i + 1, :] = r

        @pl.when(jnp.logical_not(first))
        def _():
            ref[i:i + 1, :] += r


def _ffn_act_bwd(up, conv_w, dact, *, tr=512, cb=256):
    T, F2 = up.shape
    F = F2 // 2
    nc, nb = F // cb, T // tr
    K = FFN_CONV

    def body(g_ref, gp_ref, gn_ref, v_ref, vp_ref, vn_ref, d_ref, dn_ref, wg_ref, wv_ref,
             dg_ref, dv_ref, dwg_ref, dwv_ref):
        i = pl.program_id(1)
        first, last = i == 0, i == nb - 1
        g_ext = jnp.concatenate([jnp.where(first, 0.0, gp_ref[...]), g_ref[...], jnp.where(last, 0.0, gn_ref[...])], axis=0)
        v_ext = jnp.concatenate([jnp.where(first, 0.0, vp_ref[...]), v_ref[...], jnp.where(last, 0.0, vn_ref[...])], axis=0)
        d_ext = jnp.concatenate([d_ref[...], jnp.where(last, 0.0, dn_ref[...])], axis=0)
        cg = _conv_rows(g_ext, wg_ref, K, 8)
        cv = _conv_rows(v_ext, wv_ref, K, 8)
        _, vjp = jax.vjp(_silu_mul, cg, cv)
        dcg, dcv = vjp(d_ext)
        dg, dwg = _conv_bwd_rows(dcg, g_ext, wg_ref, K, tr)
        dv, dwv = _conv_bwd_rows(dcv, v_ext, wv_ref, K, tr)
        dg_ref[...] = dg.astype(BF16)
        dv_ref[...] = dv.astype(BF16)
        _acc_rows(dwg_ref, dwg, first)
        _acc_rows(dwv_ref, dwv, first)

    blk = lambda off: pl.BlockSpec((tr, cb), lambda j, i: (i, off + j))
    prev = lambda off: pl.BlockSpec((8, cb), lambda j, i: (jnp.maximum(i * (tr // 8) - 1, 0), off + j))
    nxt = lambda off: pl.BlockSpec((8, cb), lambda j, i: (jnp.minimum((i + 1) * (tr // 8), T // 8 - 1), off + j))
    wsp = lambda off: pl.BlockSpec((K, cb), lambda j, i: (0, off + j))
    return pl.pallas_call(
        body, name="ffn_act_bwd", grid=(nc, nb),
        in_specs=[blk(0), prev(0), nxt(0), blk(nc), prev(nc), nxt(nc), blk(0), nxt(0), wsp(0), wsp(nc)],
        out_specs=[blk(0), blk(0), wsp(0), wsp(0)],
        out_shape=[jax.ShapeDtypeStruct((T, F), BF16)] * 2 + [jax.ShapeDtypeStruct((K, F), F32)] * 2,
        compiler_params=_params("parallel", "arbitrary"),
    )(up, up, up, up, up, up, dact, dact, conv_w, conv_w)


def _ple_fwd(hn, w_gate, p, w_ple, x, *, name, tm=512, tn=512):
    T, Dm = x.shape

    def body(a_ref, b_ref, p_ref, wp_ref, x_ref, o_ref, gl_ref, pe_ref):
        gl = _dot(a_ref[...], b_ref[...], NN)
        pe = _dot(p_ref[...], wp_ref[...], NN)
        gl_ref[...] = gl
        pe_ref[...] = pe
        o_ref[...] = x_ref[...] + pe * _sigmoid(gl)

    o_spec = pl.BlockSpec((tm, tn), lambda i, j: (i, j))
    return pl.pallas_call(
        body, name=name, grid=(T // tm, Dm // tn),
        in_specs=[pl.BlockSpec((tm, Dm), lambda i, j: (i, 0)), pl.BlockSpec((Dm, tn), lambda i, j: (0, j)),
                  pl.BlockSpec((tm, PLE_DIM), lambda i, j: (i, 0)), pl.BlockSpec((PLE_DIM, tn), lambda i, j: (0, j)),
                  o_spec],
        out_specs=[o_spec] * 3, out_shape=[jax.ShapeDtypeStruct((T, Dm), F32)] * 3,
        compiler_params=_params("parallel", "parallel"),
    )(hn, w_gate, p, w_ple, x)


def _ple_bwd(dx, gl, pe, *, name, tr=512):
    T, Dm = dx.shape

    def body(dx_ref, gl_ref, pe_ref, dpe_ref, dgl_ref):
        g = _sigmoid(gl_ref[...])
        d = dx_ref[...]
        dpe_ref[...] = (d * g).astype(BF16)
        dgl_ref[...] = (d * pe_ref[...] * g * (1.0 - g)).astype(BF16)

    row = pl.BlockSpec((tr, Dm), lambda i: (i, 0))
    return pl.pallas_call(
        body, name=name, grid=(T // tr,), in_specs=[row] * 3, out_specs=[row] * 2,
        out_shape=[jax.ShapeDtypeStruct((T, Dm), BF16)] * 2, compiler_params=_params("parallel"),
    )(dx, gl, pe)


def _qkv_act(c, cb):
    s = c * _sigmoid(c)
    n = s * lax.rsqrt(jnp.sum(s * s, axis=-1, keepdims=True) + EPS)
    n = n * jnp.where(cb < GDN_HEADS, GDN_HEAD_DIM ** -0.5, 1.0)
    return jnp.where(cb < 2 * GDN_HEADS, n, s)


def _gdn_pre(proj, conv_w, *, tr=512):
    T = proj.shape[0]
    K = GDN_CONV

    def body(x_ref, xh_ref, w_ref, o_ref):
        i, cb = pl.program_id(0), pl.program_id(1)
        xh = jnp.where(i > 0, xh_ref[...], 0.0)
        c = _conv_rows(jnp.concatenate([xh, x_ref[...]], axis=0), w_ref, K, 8)
        o_ref[0] = _qkv_act(c, cb)

    return pl.pallas_call(
        body, name="gdn_pre", grid=(T // tr, 24),
        in_specs=[pl.BlockSpec((tr, 128), lambda i, j: (i, j)), _prev_spec(tr, 128, 8, lambda i, j: j),
                  pl.BlockSpec((K, 128), lambda i, j: (0, j))],
        out_specs=pl.BlockSpec((1, tr, 128), lambda i, j: (j, i, 0)),
        out_shape=jax.ShapeDtypeStruct((24, T, 128), F32), compiler_params=_params("parallel", "parallel"),
    )(proj, proj, conv_w)


def _gdn_pre_bwd(proj, conv_w, dqkv, *, tr=512):
    T = proj.shape[0]
    nb = T // tr
    K = GDN_CONV

    def body(x_ref, xp_ref, xn_ref, d_ref, dn_ref, w_ref, dx_ref, dw_ref):
        cb, i = pl.program_id(0), pl.program_id(1)
        first, last = i == 0, i == nb - 1
        x_ext = jnp.concatenate([jnp.where(first, 0.0, xp_ref[...]), x_ref[...], jnp.where(last, 0.0, xn_ref[...])], axis=0)
        d_ext = jnp.concatenate([d_ref[0], jnp.where(last, 0.0, dn_ref[0])], axis=0)
        c = _conv_rows(x_ext, w_ref, K, 8)
        _, vjp = jax.vjp(lambda c_: _qkv_act(c_, cb), c)
        (dc,) = vjp(d_ext)
        dx, dws = _conv_bwd_rows(dc, x_ext, w_ref, K, tr)
        dx_ref[...] = dx.astype(BF16)
        _acc_rows(dw_ref, dws, first)

    return pl.pallas_call(
        body, name="gdn_pre_bwd", grid=(24, nb),
        in_specs=[pl.BlockSpec((tr, 128), lambda j, i: (i, j)),
                  pl.BlockSpec((8, 128), lambda j, i: (jnp.maximum(i * (tr // 8) - 1, 0), j)),
                  pl.BlockSpec((8, 128), lambda j, i: (jnp.minimum((i + 1) * (tr // 8), T // 8 - 1), j)),
                  pl.BlockSpec((1, tr, 128), lambda j, i: (j, i, 0)),
                  pl.BlockSpec((1, 8, 128), lambda j, i: (j, jnp.minimum((i + 1) * (tr // 8), T // 8 - 1), 0)),
                  pl.BlockSpec((K, 128), lambda j, i: (0, j))],
        out_specs=[pl.BlockSpec((tr, 128), lambda j, i: (i, j)), pl.BlockSpec((K, 128), lambda j, i: (0, j))],
        out_shape=[jax.ShapeDtypeStruct((T, 24 * 128), BF16), jax.ShapeDtypeStruct((K, 24 * 128), F32)],
        compiler_params=_params("parallel", "arbitrary"),
    )(proj, proj, proj, dqkv, dqkv, conv_w)


def _gate_fn(ba, alog_row, dt_row):
    lane = _iota((1, 128), 1)
    x = ba + dt_row
    sp = jnp.maximum(x, 0.0) + jnp.log(1.0 + jnp.exp(-jnp.abs(x)))
    return jnp.where(lane < GDN_HEADS, _sigmoid(ba), -jnp.exp(alog_row) * sp)


def _gdn_gate(ba, alog_row, dt_row, *, tr=512):
    T = ba.shape[0]

    def body(ba_ref, al_ref, dt_ref, b_ref, g_ref):
        val = _gate_fn(ba_ref[...], al_ref[...], dt_ref[...])
        for h in range(GDN_HEADS):
            b_ref[h] = val[:, h:h + 1]
            g_ref[h] = val[:, GDN_HEADS + h:GDN_HEADS + h + 1]

    vec = pl.BlockSpec((1, 128), lambda i: (0, 0))
    hm = pl.BlockSpec((GDN_HEADS, tr, 1), lambda i: (0, i, 0))
    return pl.pallas_call(
        body, name="gdn_gate", grid=(T // tr,), in_specs=[pl.BlockSpec((tr, 128), lambda i: (i, 0)), vec, vec],
        out_specs=[hm, hm], out_shape=[jax.ShapeDtypeStruct((GDN_HEADS, T, 1), F32)] * 2,
        compiler_params=_params("parallel"),
    )(ba, alog_row, dt_row)


def _gdn_gate_bwd(ba, alog_row, dt_row, dbeta, dg, *, tr=512):
    T = ba.shape[0]

    def body(ba_ref, al_ref, dt_ref, db_ref, dg_ref, dba_ref, dal_ref, ddt_ref):
        i = pl.program_id(0)
        lane = _iota((1, 128), 1)
        d = jnp.zeros((tr, 128), F32)
        for h in range(GDN_HEADS):
            d = d + jnp.where(lane == h, db_ref[h], 0.0) + jnp.where(lane == GDN_HEADS + h, dg_ref[h], 0.0)
        _, vjp = jax.vjp(_gate_fn, ba_ref[...], al_ref[...], dt_ref[...])
        dba, dal, ddt = vjp(d)
        dba_ref[...] = dba.astype(BF16)

        @pl.when(i == 0)
        def _():
            dal_ref[...] = dal
            ddt_ref[...] = ddt

        @pl.when(i > 0)
        def _():
            dal_ref[...] += dal
            ddt_ref[...] += ddt

    vec = pl.BlockSpec((1, 128), lambda i: (0, 0))
    hm = pl.BlockSpec((GDN_HEADS, tr, 1), lambda i: (0, i, 0))
    row = pl.BlockSpec((tr, 128), lambda i: (i, 0))
    return pl.pallas_call(
        body, name="gdn_gate_bwd", grid=(T // tr,), in_specs=[row, vec, vec, hm, hm], out_specs=[row, vec, vec],
        out_shape=[jax.ShapeDtypeStruct((T, 128), BF16), jax.ShapeDtypeStruct((1, 128), F32),
                   jax.ShapeDtypeStruct((1, 128), F32)],
        compiler_params=_params("arbitrary"),
    )(ba, alog_row, dt_row, dbeta, dg)


def _split3(x):
    x1 = x.astype(BF16)
    r = x - x1.astype(F32)
    x2 = r.astype(BF16)
    return x1, x2, (r - x2.astype(F32)).astype(BF16)


def _dot01(tri, x, dims):
    t = tri.astype(BF16)
    x1, x2, x3 = _split3(x)
    d = lambda xi: lax.dot_general(t, xi, dims, preferred_element_type=F32)
    return d(x1) + (d(x2) + d(x3))


def _dot3(a, b, dims):
    ah, al, _ = _split3(a)
    bh, bl, _ = _split3(b)
    d = lambda p, q: lax.dot_general(p, q, dims, preferred_element_type=F32)
    return d(ah, bh) + (d(ah, bl) + d(al, bh))


@jax.custom_vjp
def _mm01(tri, x):
    return _dot01(tri, x, NN)


def _mm01_fwd(tri, x):
    return _dot01(tri, x, NN), tri


def _mm01_bwd(tri, ct):
    return jnp.zeros_like(tri), _dot01(tri, ct, TN)


_mm01.defvjp(_mm01_fwd, _mm01_bwd)


def _unit_lower_inverse(a):
    C = a.shape[0]
    eye = (_iota((C, C), 0) == _iota((C, C), 1)).astype(F32)
    pw = -a
    tinv = eye + pw
    for _ in range(5):
        pw = _dot3(pw, pw, NN)
        tinv = tinv + _dot3(tinv, pw, NN)
    return tinv


@jax.custom_vjp
def _unit_lower_solve(a, rv, rw):
    return _unit_lower_solve_fwd(a, rv, rw)[0]


def _unit_lower_solve_fwd(a, rv, rw):
    tinv = _unit_lower_inverse(a)
    sol = _dot3(tinv, jnp.concatenate([rv, rw], axis=1), NN)
    n = rv.shape[1]
    return (sol[:, :n], sol[:, n:]), (tinv, sol)


def _unit_lower_solve_bwd(res, cts):
    tinv, sol = res
    n = cts[0].shape[1]
    d_rhs = _dot3(tinv, jnp.concatenate(cts, axis=1), TN)
    return -_dot3(d_rhs, sol, NT), d_rhs[:, :n], d_rhs[:, n:]


_unit_lower_solve.defvjp(_unit_lower_solve_fwd, _unit_lower_solve_bwd)


@jax.custom_vjp
def _mmb_nt(a, b):
    return _dot(a, b, NT)


def _mmb_nt_fwd(a, b):
    return _dot(a, b, NT), (a, b)


def _mmb_nt_bwd(res, ct):
    a, b = res
    return _dot(ct, b, NN), _dot(ct, a, TN)


_mmb_nt.defvjp(_mmb_nt_fwd, _mmb_nt_bwd)


def _gdn_chunk(q, k, v, gcol, bcol):
    C = GDN_CHUNK
    row, col = _iota((C, C), 0), _iota((C, C), 1)
    incl, strict = row >= col, row > col
    eye = (row == col).astype(F32)
    lower = incl.astype(F32)
    ones = jnp.ones((C, C), F32)
    gwide = jnp.broadcast_to(gcol, (C, GDN_HEAD_DIM))
    gc = _mm01(lower, gwide)
    gtot = _mm01(ones, gwide)
    gc_c = _mm01(lower, jnp.broadcast_to(gcol, (C, C)))
    gc_s = _mm01(ones, gc_c * eye)
    decay = jnp.where(incl, jnp.exp(jnp.where(incl, gc_c - gc_s, 0.0)), 0.0)
    kb = k * bcol
    a = jnp.where(strict, _mmb_nt(kb, k) * decay, 0.0)
    egc = jnp.exp(gc)
    u, w = _unit_lower_solve(a, v * bcol, kb * egc)
    qk = jnp.where(incl, _mmb_nt(q, k) * decay, 0.0)
    return u, w, qk, q * egc, k * jnp.exp(gtot - gc), jnp.exp(gtot)


GDN_ROWS = 8 * GDN_CHUNK


def _gdn_specs(T):
    hd = lambda off: pl.BlockSpec((1, GDN_ROWS, 128), lambda h, i: (off + h, i, 0))
    col = pl.BlockSpec((1, GDN_ROWS, 1), lambda h, i: (h, i, 0))
    sq = pl.BlockSpec((1, GDN_ROWS, GDN_CHUNK), lambda h, i: (h, i, 0))
    gl = pl.BlockSpec((1, 8, 128), lambda h, i: (h, i, 0))
    return hd, col, sq, gl


def _gdn_local(qkv, g, beta):
    T = qkv.shape[1]
    hd, col, sq, gl_spec = _gdn_specs(T)

    def body(q_ref, k_ref, v_ref, g_ref, b_ref, u_ref, w_ref, qk_ref, qd_ref, kd_ref, gl_ref):
        for c in range(8):
            rs = slice(GDN_CHUNK * c, GDN_CHUNK * (c + 1))
            u, w, qk, qd, kd, gl = _gdn_chunk(q_ref[0, rs, :], k_ref[0, rs, :], v_ref[0, rs, :], g_ref[0, rs, :], b_ref[0, rs, :])
            u_ref[0, rs, :] = u
            w_ref[0, rs, :] = w.astype(BF16)
            qk_ref[0, rs, :] = qk.astype(BF16)
            qd_ref[0, rs, :] = qd.astype(BF16)
            kd_ref[0, rs, :] = kd.astype(BF16)
            gl_ref[0, c:c + 1, :] = gl[0:1, :]

    H = GDN_HEADS
    return pl.pallas_call(
        body, name="gdn_local", grid=(H, T // GDN_ROWS),
        in_specs=[hd(0), hd(H), hd(2 * H), col, col],
        out_specs=[hd(0), hd(0), sq, hd(0), hd(0), gl_spec],
        out_shape=[jax.ShapeDtypeStruct((H, T, 128), F32), jax.ShapeDtypeStruct((H, T, 128), BF16),
                   jax.ShapeDtypeStruct((H, T, GDN_CHUNK), BF16), jax.ShapeDtypeStruct((H, T, 128), BF16),
                   jax.ShapeDtypeStruct((H, T, 128), BF16), jax.ShapeDtypeStruct((H, T // GDN_CHUNK, 128), F32)],
        compiler_params=_params("parallel", "parallel"),
    )(qkv, qkv, qkv, g, beta)


def _gdn_local_bwd(qkv, g, beta, du, dw, dqk, dqd, dkd, dgl):
    T = qkv.shape[1]
    hd, col, sq, gl_spec = _gdn_specs(T)

    def body(q_ref, k_ref, v_ref, g_ref, b_ref, du_ref, dw_ref, dqk_ref, dqd_ref, dkd_ref, dgl_ref,
             dq_ref, dk_ref, dv_ref, dg_ref, db_ref):
        row0 = _iota((GDN_CHUNK, 128), 0) == 0
        for c in range(8):
            rs = slice(GDN_CHUNK * c, GDN_CHUNK * (c + 1))
            _, vjp = jax.vjp(_gdn_chunk, q_ref[0, rs, :], k_ref[0, rs, :], v_ref[0, rs, :], g_ref[0, rs, :], b_ref[0, rs, :])
            ct_gl = jnp.where(row0, dgl_ref[0, c:c + 1, :], 0.0)
            dq, dk, dv, dg, db = vjp((du_ref[0, rs, :], dw_ref[0, rs, :], dqk_ref[0, rs, :], dqd_ref[0, rs, :],
                                      dkd_ref[0, rs, :], ct_gl))
            dq_ref[0, rs, :] = dq
            dk_ref[0, rs, :] = dk
            dv_ref[0, rs, :] = dv
            dg_ref[0, rs, :] = dg
            db_ref[0, rs, :] = db

    H = GDN_HEADS
    big = jax.ShapeDtypeStruct((H, T, 128), F32)
    small = jax.ShapeDtypeStruct((H, T, 1), F32)
    return pl.pallas_call(
        body, name="gdn_local_bwd", grid=(H, T // GDN_ROWS),
        in_specs=[hd(0), hd(H), hd(2 * H), col, col, hd(0), hd(0), sq, hd(0), hd(0), gl_spec],
        out_specs=[hd(0), hd(0), hd(0), col, col], out_shape=[big, big, big, small, small],
        compiler_params=_params("parallel", "parallel"),
    )(qkv, qkv, qkv, g, beta, du, dw, dqk, dqd, dkd, dgl)


GDN_HB = 2


def _gdn_scan_specs(T, rev):
    nb = T // GDN_ROWS
    blk = (lambda i: nb - 1 - i) if rev else (lambda i: i)
    hd = pl.BlockSpec((GDN_HB, GDN_ROWS, 128), lambda h, i: (h, blk(i), 0))
    sq = pl.BlockSpec((GDN_HB, GDN_ROWS, GDN_CHUNK), lambda h, i: (h, blk(i), 0))
    gl = pl.BlockSpec((GDN_HB, 8, 128), lambda h, i: (h, blk(i), 0))
    st = pl.BlockSpec((GDN_HB, 8, 128, 128), lambda h, i: (h, blk(i), 0, 0))
    return hd, sq, gl, st


def _gdn_scan(u, w, qk, qd, kd, gl):
    H, T, _ = u.shape
    hd, sq, gl_spec, st = _gdn_scan_specs(T, False)

    def body(u_ref, w_ref, qk_ref, qd_ref, kd_ref, gl_ref, o_ref, ss_ref, vn_ref, s_scr):
        @pl.when(pl.program_id(1) == 0)
        def _():
            s_scr[...] = jnp.zeros_like(s_scr)

        for hh in range(GDN_HB):
            s = s_scr[hh]
            for c in range(8):
                rs = slice(GDN_CHUNK * c, GDN_CHUNK * (c + 1))
                ss_ref[hh, c] = s
                sb = s.astype(BF16)
                vn = u_ref[hh, rs, :] - lax.dot_general(w_ref[hh, rs, :], sb, NN, preferred_element_type=F32)
                vnb = vn.astype(BF16)
                o_ref[hh, rs, :] = (lax.dot_general(qd_ref[hh, rs, :], sb, NN, preferred_element_type=F32)
                                    + lax.dot_general(qk_ref[hh, rs, :], vnb, NN, preferred_element_type=F32))
                vn_ref[hh, rs, :] = vnb
                s = s * gl_ref[hh, c:c + 1, :] + lax.dot_general(kd_ref[hh, rs, :], vnb, TN, preferred_element_type=F32)
            s_scr[hh] = s

    return pl.pallas_call(
        body, name="gdn_scan", grid=(H // GDN_HB, T // GDN_ROWS),
        in_specs=[hd, hd, sq, hd, hd, gl_spec], out_specs=[hd, st, hd],
        out_shape=[jax.ShapeDtypeStruct((H, T, 128), F32), jax.ShapeDtypeStruct((H, T // GDN_CHUNK, 128, 128), F32),
                   jax.ShapeDtypeStruct((H, T, 128), BF16)],
        scratch_shapes=[pltpu.VMEM((GDN_HB, 128, 128), F32)],
        compiler_params=_params("parallel", "arbitrary"),
    )(u, w, qk, qd, kd, gl)


def _gdn_scan_bwd(do, ss, vn, w, qk, qd, kd, gl):
    H, T, _ = do.shape
    hd, sq, gl_spec, st = _gdn_scan_specs(T, True)

    def body(do_ref, ss_ref, vn_ref, w_ref, qk_ref, qd_ref, kd_ref, gl_ref,
             du_ref, dw_ref, dqk_ref, dqd_ref, dkd_ref, dgl_ref, ds_scr):
        @pl.when(pl.program_id(1) == 0)
        def _():
            ds_scr[...] = jnp.zeros_like(ds_scr)

        dot = lambda a, b, dims: lax.dot_general(a, b, dims, preferred_element_type=F32)
        for hh in range(GDN_HB):
            ds = ds_scr[hh]
            for c in reversed(range(8)):
                rs = slice(GDN_CHUNK * c, GDN_CHUNK * (c + 1))
                s = ss_ref[hh, c]
                sb, dsb = s.astype(BF16), ds.astype(BF16)
                dob = do_ref[hh, rs, :].astype(BF16)
                vnb = vn_ref[hh, rs, :]
                dvn = dot(qk_ref[hh, rs, :], dob, TN) + dot(kd_ref[hh, rs, :], dsb, NN)
                dvnb = dvn.astype(BF16)
                du_ref[hh, rs, :] = dvn
                dw_ref[hh, rs, :] = -dot(dvnb, sb, NT)
                dqk_ref[hh, rs, :] = dot(dob, vnb, NT)
                dqd_ref[hh, rs, :] = dot(dob, sb, NT)
                dkd_ref[hh, rs, :] = dot(vnb, dsb, NT)
                dgl_ref[hh, c:c + 1, :] = jnp.sum(ds * s, axis=0, keepdims=True)
                ds = dot(qd_ref[hh, rs, :], dob, TN) + ds * gl_ref[hh, c:c + 1, :] - dot(w_ref[hh, rs, :], dvnb, TN)
            ds_scr[hh] = ds

    big = jax.ShapeDtypeStruct((H, T, 128), F32)
    return pl.pallas_call(
        body, name="gdn_scan_bwd", grid=(H // GDN_HB, T // GDN_ROWS),
        in_specs=[hd, st, hd, hd, sq, hd, hd, gl_spec], out_specs=[hd, hd, sq, hd, hd, gl_spec],
        out_shape=[big, big, jax.ShapeDtypeStruct((H, T, GDN_CHUNK), F32), big, big,
                   jax.ShapeDtypeStruct((H, T // GDN_CHUNK, 128), F32)],
        scratch_shapes=[pltpu.VMEM((GDN_HB, 128, 128), F32)],
        compiler_params=_params("parallel", "arbitrary"),
    )(do, ss, vn, w, qk, qd, kd, gl)


def _gated_norm(o, z, nw):
    on = o * lax.rsqrt(jnp.mean(o * o, axis=-1, keepdims=True) + EPS) * nw
    return on * (z * _sigmoid(z))


def _gdn_post(o, proj, norm_w, *, tr=512):
    T = proj.shape[0]

    def body(o_ref, z_ref, n_ref, y_ref):
        y_ref[...] = _gated_norm(o_ref[0], z_ref[...], n_ref[...]).astype(BF16)

    return pl.pallas_call(
        body, name="gdn_post", grid=(T // tr, GDN_HEADS),
        in_specs=[pl.BlockSpec((1, tr, 128), lambda i, h: (h, i, 0)), pl.BlockSpec((tr, 128), lambda i, h: (i, 24 + h)),
                  pl.BlockSpec((1, 128), lambda i, h: (0, 0))],
        out_specs=pl.BlockSpec((tr, 128), lambda i, h: (i, h)),
        out_shape=jax.ShapeDtypeStruct((T, 1024), BF16), compiler_params=_params("parallel", "parallel"),
    )(o, proj, norm_w)


def _gdn_post_bwd(o, proj, norm_w, dy, *, tr=512):
    T = proj.shape[0]

    def body(o_ref, z_ref, n_ref, dy_ref, do_ref, dz_ref, dn_ref):
        first = (pl.program_id(0) == 0) & (pl.program_id(1) == 0)
        _, vjp = jax.vjp(_gated_norm, o_ref[0], z_ref[...], n_ref[...])
        do, dz, dn = vjp(dy_ref[...])
        do_ref[0] = do
        dz_ref[...] = dz.astype(BF16)

        @pl.when(first)
        def _():
            dn_ref[...] = dn

        @pl.when(jnp.logical_not(first))
        def _():
            dn_ref[...] += dn

    blk = pl.BlockSpec((tr, 128), lambda i, h: (i, h))
    hm = pl.BlockSpec((1, tr, 128), lambda i, h: (h, i, 0))
    vec = pl.BlockSpec((1, 128), lambda i, h: (0, 0))
    return pl.pallas_call(
        body, name="gdn_post_bwd", grid=(T // tr, GDN_HEADS),
        in_specs=[hm, pl.BlockSpec((tr, 128), lambda i, h: (i, 24 + h)), vec, blk], out_specs=[hm, blk, vec],
        out_shape=[jax.ShapeDtypeStruct((GDN_HEADS, T, 128), F32), jax.ShapeDtypeStruct((T, 1024), BF16),
                   jax.ShapeDtypeStruct((1, 128), F32)],
        compiler_params=_params("arbitrary", "arbitrary"),
    )(o, proj, norm_w, dy)


HBM_SPEC = pl.BlockSpec(memory_space=pltpu.HBM)


def _place():
    return lax.axis_index("x"), lax.axis_index("y"), lax.axis_index("c")


def _all_gather(vs, *, name):
    n = len(vs)

    def body(*refs):
        v_refs, out_refs = refs[:n], refs[n:2 * n]
        send_sems, recv_sems, local_sems = refs[2 * n:]
        x, y, c = _place()
        me, sibling = (x, y, c), (x, y, 1 - c)
        chips = [(1 - x, y), (x, 1 - y), (1 - x, 1 - y)]

        def copy(a, k, block, to, from_input=False):
            slot = out_refs[a].at[4 * block[0] + 2 * block[1] + block[2]]
            return pltpu.make_async_remote_copy(
                src_ref=v_refs[a] if from_input else slot, dst_ref=slot,
                send_sem=send_sems.at[7 * a + k], recv_sem=recv_sems.at[7 * a + k], device_id=to, device_id_type=MESH)

        mine = [pltpu.make_async_copy(v_refs[a], out_refs[a].at[4 * x + 2 * y + c], local_sems.at[a]) for a in range(n)]
        first = [copy(a, 0, me, sibling, True) for a in range(n)]
        first += [copy(a, 1 + j, me, (*chip, c), True) for j, chip in enumerate(chips) for a in range(n)]
        for cp in mine + first:
            cp.start()
        passed = []
        for j, chip in enumerate(chips):
            for a in range(n):
                copy(a, 1 + j, (*chip, c), me).wait_recv()
                passed.append(copy(a, 4 + j, (*chip, c), sibling))
                passed[-1].start()
        for a in range(n):
            copy(a, 0, sibling, me).wait_recv()
            for j, chip in enumerate(chips):
                copy(a, 4 + j, (*chip, 1 - c), me).wait_recv()
        for cp in first + passed:
            cp.wait_send()
        for cp in mine:
            cp.wait()

    return pl.pallas_call(
        body, name=name, out_shape=[jax.ShapeDtypeStruct((N_DEV,) + v.shape, v.dtype) for v in vs],
        in_specs=[HBM_SPEC] * n, out_specs=[HBM_SPEC] * n,
        scratch_shapes=[pltpu.SemaphoreType.DMA((7 * n,)), pltpu.SemaphoreType.DMA((7 * n,)),
                        pltpu.SemaphoreType.DMA((n,))],
    )(*vs)


def _exchange_sibling(gs):
    n = len(gs)

    def body(*refs):
        g_refs, out_refs = refs[:n], refs[n:2 * n]
        send_sems, recv_sems = refs[2 * n:]
        x, y, c = _place()
        copies = [pltpu.make_async_remote_copy(
            src_ref=g_refs[a].at[k, 1 - c], dst_ref=out_refs[a].at[k], send_sem=send_sems.at[4 * a + k],
            recv_sem=recv_sems.at[4 * a + k], device_id=(x, y, 1 - c), device_id_type=MESH)
            for a in range(n) for k in range(4)]
        for cp in copies:
            cp.start()
        for cp in copies:
            cp.wait()

    return pl.pallas_call(
        body, name="rs_sibling", out_shape=[jax.ShapeDtypeStruct((4,) + g.shape[2:], g.dtype) for g in gs],
        in_specs=[HBM_SPEC] * n, out_specs=[HBM_SPEC] * n,
        scratch_shapes=[pltpu.SemaphoreType.DMA((4 * n,)), pltpu.SemaphoreType.DMA((4 * n,))],
    )(*gs)


def _exchange_chips(pcs):
    n = len(pcs)

    def body(*refs):
        p_refs, out_refs = refs[:n], refs[n:2 * n]
        send_sems, recv_sems = refs[2 * n:]
        x, y, c = _place()
        chips = [(1 - x, y), (x, 1 - y), (1 - x, 1 - y)]
        copies = [pltpu.make_async_remote_copy(
            src_ref=p_refs[a].at[2 * cx + cy], dst_ref=out_refs[a].at[j], send_sem=send_sems.at[3 * a + j],
            recv_sem=recv_sems.at[3 * a + j], device_id=(cx, cy, c), device_id_type=MESH)
            for j, (cx, cy) in enumerate(chips) for a in range(n)]
        for cp in copies:
            cp.start()
        for cp in copies:
            cp.wait()

    return pl.pallas_call(
        body, name="rs_chips", out_shape=[jax.ShapeDtypeStruct((3,) + pc.shape[1:], pc.dtype) for pc in pcs],
        in_specs=[HBM_SPEC] * n, out_specs=[HBM_SPEC] * n,
        scratch_shapes=[pltpu.SemaphoreType.DMA((3 * n,)), pltpu.SemaphoreType.DMA((3 * n,))],
    )(*pcs)


def _chip_partial(place, g, got, *, tr, name):
    R, W = g.shape[2:]

    def body(pl_ref, g_ref, r_ref, o_ref):
        o_ref[...] = (g_ref[0] + r_ref[...]).astype(BF16)

    return pl.pallas_call(
        body, name=name, out_shape=jax.ShapeDtypeStruct((4, R, W), BF16),
        grid_spec=pltpu.PrefetchScalarGridSpec(
            num_scalar_prefetch=1, grid=(4, R // tr),
            in_specs=[pl.BlockSpec((1, 1, tr, W), lambda k, i, pr: (k, pr[2], i, 0)),
                      pl.BlockSpec((1, tr, W), lambda k, i, pr: (k, i, 0))],
            out_specs=pl.BlockSpec((1, tr, W), lambda k, i, pr: (k, i, 0))),
        compiler_params=_params("parallel", "parallel"),
    )(place, g, got)


def _adamw_math(g, w, m, v):
    m = ADAM_B1 * m + (1.0 - ADAM_B1) * g
    v = ADAM_B2 * v + (1.0 - ADAM_B2) * (g * g)
    m_hat = m / (1.0 - ADAM_B1 ** ADAM_STEP)
    v_hat = v / (1.0 - ADAM_B2 ** ADAM_STEP)
    return -ADAM_LR * (m_hat / (jnp.sqrt(v_hat) + ADAM_EPS) + ADAM_WD * w), m, v


def _adamw_shard(place, g, got1, got2, w, m, v, *, tr, name):
    R, W = w.shape

    def body(pl_ref, g_ref, r1_ref, r2_ref, w_ref, m_ref, v_ref, go_ref, d_ref, mo_ref, vo_ref):
        gs = g_ref[0, 0] + r1_ref[0]
        for j in range(3):
            gs = gs + r2_ref[j].astype(F32)
        go_ref[...] = gs
        d_ref[...], mo_ref[...], vo_ref[...] = _adamw_math(gs, w_ref[...], m_ref[...], v_ref[...])

    row = pl.BlockSpec((tr, W), lambda i, pr: (i, 0))
    out = jax.ShapeDtypeStruct((R, W), F32)
    return pl.pallas_call(
        body, name=name, out_shape=[out] * 4,
        grid_spec=pltpu.PrefetchScalarGridSpec(
            num_scalar_prefetch=1, grid=(R // tr,),
            in_specs=[pl.BlockSpec((1, 1, tr, W), lambda i, pr: (2 * pr[0] + pr[1], pr[2], i, 0)),
                      pl.BlockSpec((1, tr, W), lambda i, pr: (2 * pr[0] + pr[1], i, 0)),
                      pl.BlockSpec((3, tr, W), lambda i, pr: (0, i, 0)), row, row, row],
            out_specs=[row] * 4),
        compiler_params=_params("parallel"),
    )(place, g, got1, got2, w, m, v)


def _adamw_replicated(parts, w, m, v):
    R, W = w.shape

    def body(p_ref, w_ref, m_ref, v_ref, go_ref, d_ref, mo_ref, vo_ref):
        gs = p_ref[0]
        for j in range(1, N_DEV):
            gs = gs + p_ref[j]
        go_ref[...] = gs
        d_ref[...], mo_ref[...], vo_ref[...] = _adamw_math(gs, w_ref[...], m_ref[...], v_ref[...])

    full = pl.BlockSpec((R, W), lambda i: (0, 0))
    out = jax.ShapeDtypeStruct((R, W), F32)
    return pl.pallas_call(
        body, name="adamw_replicated", grid=(1,), out_shape=[out] * 4,
        in_specs=[pl.BlockSpec((N_DEV, R, W), lambda i: (0, 0, 0)), full, full, full], out_specs=[full] * 4,
        compiler_params=_params("arbitrary"),
    )(parts, w, m, v)


GROUPS = {
    "g256": (256, 512, (("w_in_e", 1024, 1024),)),
    "g1024": (1024, 304, (("w_out_e", 128, 128), ("w_out_o", 128, 128), ("w_down", 704, 704), ("w_ple_gate", 256, 256))),
    "g514": (514, 512, (("w_in_o", 1024, 1024),)),
    "g704": (704, 272, (("w_up", 2048, 2048), ("ffn_conv", 6, 128))),
    "g128": (128, 528, (("w_ple", 512, 512), ("mix_norm_o", 1, 16))),
    "g384": (384, 8, (("conv_qkv_o", 4, 8),)),
}
SHARDED = tuple(p[0] for g in GROUPS.values() for p in g[2])
COLUMN_SHARDED = ("w_in_e", "w_in_o", "w_up", "ffn_conv", "w_ple", "conv_qkv_o", "mix_norm_o")
PACK_W = 1024
REPL_LAYOUT = (
    ("mix_norm_e", (1, 1024), 8), ("pool_w", (1, 4, 128, 128), 64), ("pool_scale", (1, 512), 8),
    ("a_log_o", (1, 8), 8), ("dt_bias_o", (1, 8), 8), ("gdn_norm_o", (1, 128), 8),
    ("ffn_norm", (2, 1024), 8), ("ple_norm", (2, 1024), 8), ("final_norm", (1024,), 8),
)


def _pad_rows(a, rows):
    extra = rows - a.shape[-2]
    return a if extra == 0 else jnp.pad(a, [(0, 0)] * (a.ndim - 2) + [(0, extra), (0, 0)])


def _group_rows(pieces, gname):
    parts = [_pad_rows(pieces[name], padded) for name, _, padded in GROUPS[gname][2]]
    return parts[0] if len(parts) == 1 else jnp.concatenate(parts, axis=-2)


def _ungroup_rows(buf, gname):
    out, r0 = {}, 0
    for name, rows, padded in GROUPS[gname][2]:
        out[name] = buf[..., r0:r0 + rows, :]
        r0 += padded
    return out


def _shard_major(name, gfull, n_layers):
    per_layer = []
    for g in gfull:
        if name in COLUMN_SHARDED:
            k = g.shape[0]
            per_layer.append(jnp.moveaxis(g.reshape(k, N_DEV, g.shape[1] // N_DEV), 1, 0))
        else:
            per_layer.append(g.reshape(N_DEV, g.shape[0] // N_DEV, -1))
    return per_layer[0] if n_layers == 1 else jnp.concatenate(per_layer, axis=1)


def _natural(name, gathered, n_layers):
    rows = gathered.shape[1] // n_layers
    out = []
    for layer in range(n_layers):
        piece = gathered[:, layer * rows:(layer + 1) * rows]
        if name in COLUMN_SHARDED:
            out.append(jnp.moveaxis(piece, 0, 1).reshape(rows, N_DEV * piece.shape[2]))
        else:
            out.append(piece.reshape(N_DEV * rows, piece.shape[2]))
    return out


def _rows(a, rows):
    flat = a.reshape(-1)
    return jnp.pad(flat, (0, rows * PACK_W - flat.shape[0])).reshape(rows, PACK_W)


def _pack_repl(vals):
    return jnp.concatenate([_rows(vals[name].reshape(shape), rows) for name, shape, rows in REPL_LAYOUT], axis=0)


def _unpack_repl(buf):
    out, r0 = {}, 0
    for name, shape, rows in REPL_LAYOUT:
        n = 1
        for s in shape:
            n *= s
        out[name] = buf[r0:r0 + rows].reshape(-1)[:n].reshape(shape)
        r0 += rows
    return out


WEIGHTS = ("mix_norm_e", "w_in_e", "pool_w", "pool_scale", "w_out_e", "mix_norm_o", "w_in_o", "conv_qkv_o", "a_log_o",
           "dt_bias_o", "gdn_norm_o", "w_out_o", "ffn_norm", "w_up", "ffn_conv", "w_down", "ple_norm", "w_ple_gate",
           "w_ple", "final_norm")


def _ffn_forward(x, norm_g, w_up, conv_w, w_down, tag):
    hn = _rms_fwd(x, norm_g, name="rms_ffn" + tag)
    up = _mm(hn, w_up, tm=1024, tn=512, name="ffn_up" + tag)
    act = _ffn_act(up, conv_w)
    out = _mm(act, w_down, res=x, tm=1024, tn=512, tk=1408, name="ffn_down" + tag)
    return out, (x, hn, up, act)


def _ffn_backward(dx, saved, norm_g, w_up, conv_w, w_down, tag):
    x, hn, up, act = saved
    dact = _mm(dx, w_down, tb=True, tm=512, tn=1408, name="ffn_dact" + tag)
    d_w_down = _mm(act, dx, ta=True, tm=1408, tn=512, tk=512, name="ffn_dwdown" + tag)
    dgate, dval, dcg, dcv = _ffn_act_bwd(up, conv_w, dact)
    dhn = _mm(dgate, w_up[:, :FFN_DIM], tb=True, tm=1024, tn=512, tk=1408, name="ffn_dhn_g" + tag)
    dhn = _mm(dval, w_up[:, FFN_DIM:], tb=True, res=dhn, tm=1024, tn=512, tk=1408, name="ffn_dhn_v" + tag)
    d_w_up = jnp.concatenate([_mm(hn, dgate, ta=True, tm=1024, tn=1408, tk=512, name="ffn_dwup_g" + tag),
                              _mm(hn, dval, ta=True, tm=1024, tn=1408, tk=512, name="ffn_dwup_v" + tag)], axis=1)
    dx, d_norm = _rms_bwd(x, dhn, norm_g, dx, name="rms_ffn_bwd" + tag)
    return dx, d_norm, d_w_up, jnp.concatenate([dcg, dcv], axis=1), d_w_down


def _ple_forward(x, norm_g, w_gate, p, w_ple, tag):
    hn = _rms_fwd(x, norm_g, name="rms_ple" + tag)
    out, gl, pe = _ple_fwd(hn, w_gate, p, w_ple, x, name="ple_fwd" + tag)
    return out, (x, hn, gl, pe)


def _ple_backward(dx, saved, norm_g, w_gate, p, tag):
    x, hn, gl, pe = saved
    dpe, dgl = _ple_bwd(dx, gl, pe, name="ple_bwd" + tag)
    d_w_ple = _mm(p, dpe, ta=True, tm=256, tn=512, tk=512, name="ple_dwple" + tag)
    d_w_gate = _mm(hn, dgl, ta=True, tm=1024, tn=512, tk=512, name="ple_dwgate" + tag)
    dhn = _mm(dgl, w_gate, tb=True, tm=1024, tn=512, name="ple_dhn" + tag)
    dx, d_norm = _rms_bwd(x, dhn, norm_g, dx, name="rms_ple_bwd" + tag)
    return dx, d_norm, d_w_gate, d_w_ple


def kernel(x, p, mix_norm_e, w_in_e, pool_w, pool_scale, w_out_e, mix_norm_o, w_in_o, conv_qkv_o, a_log_o, dt_bias_o, gdn_norm_o, w_out_o, ffn_norm, w_up, ffn_conv, w_down, ple_norm, w_ple_gate, w_ple, final_norm, loss_target, m_mix_norm_e, m_w_in_e, m_pool_w, m_pool_scale, m_w_out_e, m_mix_norm_o, m_w_in_o, m_conv_qkv_o, m_a_log_o, m_dt_bias_o, m_gdn_norm_o, m_w_out_o, m_ffn_norm, m_w_up, m_ffn_conv, m_w_down, m_ple_norm, m_w_ple_gate, m_w_ple, m_final_norm, v_mix_norm_e, v_w_in_e, v_pool_w, v_pool_scale, v_w_out_e, v_mix_norm_o, v_w_in_o, v_conv_qkv_o, v_a_log_o, v_dt_bias_o, v_gdn_norm_o, v_w_out_o, v_ffn_norm, v_w_up, v_ffn_conv, v_w_down, v_ple_norm, v_w_ple_gate, v_w_ple, v_final_norm):
    given = dict(locals())
    place = jnp.stack(_place()).astype(jnp.int32)
    x0, tgt = x[0], loss_target[0]

    def pieces(prefix):
        return {name: given[prefix + name].reshape(rows, GROUPS[g][0])
                for g in GROUPS for name, rows, _ in GROUPS[g][2]}

    loc = pieces("")
    small = ("ffn_conv", "mix_norm_o", "conv_qkv_o")
    send = [_group_rows({k: v.astype(BF16) for k, v in loc.items()}, g) for g in ("g256", "g1024", "g514")]
    send += [loc["w_up"].astype(BF16), loc["w_ple"].astype(BF16)] + [_pad_rows(loc[k], 8) for k in small]
    got = _all_gather(send, name="ag_weights")
    gathered = {"w_in_e": got[0], **_ungroup_rows(got[1], "g1024"), "w_in_o": got[2], "w_up": got[3], "w_ple": got[4]}
    layers = {name: given[name].shape[0] if given[name].ndim == 3 else 1 for name in gathered}
    full = {(name, i): w for name in gathered for i, w in enumerate(_natural(name, gathered[name], layers[name]))}
    for i in range(2):
        full[("ffn_conv", i)] = _natural("ffn_conv", got[5][:, 3 * i:3 * i + 3], 1)[0]
    w_in_o_full = full[("w_in_o", 0)]
    w_in_o_main = w_in_o_full[:, :4096]
    w_in_o_ba = jnp.pad(w_in_o_full[:, 4096:], ((0, 0), (0, 112)))
    mix_norm_o_full = got[6][:, 0].reshape(1, D_MODEL)
    conv_qkv = _natural("conv_qkv_o", got[7][:, :4], 1)[0]
    alog_row = jnp.pad(a_log_o, ((0, 0), (8, 112)))
    dt_row = jnp.pad(dt_bias_o, ((0, 0), (8, 112)))
    lw = lambda name, i: full[(name, i)]

    h_e = _rms_fwd(x0, mix_norm_e, name="rms_mix_e")
    proj_e = _mm(h_e, lw("w_in_e", 0), tm=1024, tn=512, name="in_e")
    pool_o = _pool_fwd(proj_e, pool_w[0], pool_scale)
    att_o, lsum = _sb_fwd(proj_e)
    mix_e = jnp.concatenate([pool_o, att_o.astype(BF16)], axis=1)
    x1 = _mm(mix_e, lw("w_out_e", 0), res=x0, tm=1024, tn=512, name="out_e")
    x2, ffn0 = _ffn_forward(x1, ffn_norm[0:1], lw("w_up", 0), lw("ffn_conv", 0), lw("w_down", 0), "0")
    x3, ple0 = _ple_forward(x2, ple_norm[0:1], lw("w_ple_gate", 0), p[0, 0], lw("w_ple", 0), "0")

    h_o = _rms_fwd(x3, mix_norm_o_full, name="rms_mix_o")
    proj_o = _mm(h_o, w_in_o_main, tm=1024, tn=512, name="in_o")
    ba = _mm(h_o, w_in_o_ba, tm=1024, tn=128, name="in_o_ba")
    qkv = _gdn_pre(proj_o, conv_qkv)
    beta, g = _gdn_gate(ba, alog_row, dt_row)
    u, w_c, qk, qd, kd, gl = _gdn_local(qkv, g, beta)
    o, states, vnew = _gdn_scan(u, w_c, qk, qd, kd, gl)
    y_o = _gdn_post(o, proj_o, gdn_norm_o)
    x4 = _mm(y_o, lw("w_out_o", 0), res=x3, tm=1024, tn=512, name="out_o")
    x5, ffn1 = _ffn_forward(x4, ffn_norm[1:2], lw("w_up", 1), lw("ffn_conv", 1), lw("w_down", 1), "1")
    x6, ple1 = _ple_forward(x5, ple_norm[1:2], lw("w_ple_gate", 1), p[1, 0], lw("w_ple", 1), "1")
    loss_row, dx, d_final = _final_loss(x6, final_norm.reshape(1, D_MODEL), tgt)

    grads, rgrads = {}, {}
    dx, d_ple1, grads[("w_ple_gate", 1)], grads[("w_ple", 1)] = _ple_backward(dx, ple1, ple_norm[1:2], lw("w_ple_gate", 1), p[1, 0], "1")
    dx, d_ffn1, grads[("w_up", 1)], grads[("ffn_conv", 1)], grads[("w_down", 1)] = _ffn_backward(
        dx, ffn1, ffn_norm[1:2], lw("w_up", 1), lw("ffn_conv", 1), lw("w_down", 1), "1")
    grads[("w_out_o", 0)] = _mm(y_o, dx, ta=True, tm=1024, tn=512, tk=512, name="dw_out_o")
    dy_o = _mm(dx, lw("w_out_o", 0), tb=True, tm=1024, tn=512, name="dy_o")
    do, dz, rgrads["gdn_norm_o"] = _gdn_post_bwd(o, proj_o, gdn_norm_o, dy_o)
    du, dw_c, dqk, dqd, dkd, dgl = _gdn_scan_bwd(do, states, vnew, w_c, qk, qd, kd, gl)
    dq, dk, dv, dg, dbeta = _gdn_local_bwd(qkv, g, beta, du, dw_c, dqk, dqd, dkd, dgl)
    dqkv, grads[("conv_qkv_o", 0)] = _gdn_pre_bwd(proj_o, conv_qkv, jnp.concatenate([dq, dk, dv], axis=0))
    dba, d_alog, d_dt = _gdn_gate_bwd(ba, alog_row, dt_row, dbeta, dg)
    rgrads["a_log_o"], rgrads["dt_bias_o"] = d_alog[:, 8:16], d_dt[:, 8:16]
    dproj_o = jnp.concatenate([dqkv, dz], axis=1)
    dh = _mm(dproj_o, w_in_o_main, tb=True, tm=1024, tn=512, tk=1024, name="dh_o")
    dh = _mm(dba, w_in_o_ba, tb=True, res=dh, tm=1024, tn=512, name="dh_o_ba")
    grads[("w_in_o", 0)] = jnp.concatenate(
        [_mm(h_o, dproj_o, ta=True, tm=1024, tn=512, tk=512, name="dw_in_o"),
         _mm(h_o, dba, ta=True, tm=1024, tn=128, tk=512, name="dw_in_o_ba")[:, :16]], axis=1)
    dx, d_mix_o = _rms_bwd(x3, dh, mix_norm_o_full, dx, name="rms_mix_o_bwd")
    grads[("mix_norm_o", 0)] = d_mix_o

    dx, d_ple0, grads[("w_ple_gate", 0)], grads[("w_ple", 0)] = _ple_backward(dx, ple0, ple_norm[0:1], lw("w_ple_gate", 0), p[0, 0], "0")
    dx, d_ffn0, grads[("w_up", 0)], grads[("ffn_conv", 0)], grads[("w_down", 0)] = _ffn_backward(
        dx, ffn0, ffn_norm[0:1], lw("w_up", 0), lw("ffn_conv", 0), lw("w_down", 0), "0")
    grads[("w_out_e", 0)] = _mm(mix_e, dx, ta=True, tm=1024, tn=512, tk=512, name="dw_out_e")
    dmix = _mm(dx, lw("w_out_e", 0), tb=True, tm=1024, tn=512, name="dmix_e")
    du_e, d_pool_w, rgrads["pool_scale"] = _pool_bwd(proj_e, dmix, pool_w[0], pool_scale)
    rgrads["pool_w"] = d_pool_w[None]
    dq_e, dk_e, dv_e = _sb_bwd(proj_e, lsum, dmix)
    dproj_e = jnp.concatenate([du_e, dq_e.astype(BF16), dk_e.astype(BF16), dv_e.astype(BF16)], axis=1)
    dh = _mm(dproj_e, lw("w_in_e", 0), tb=True, tm=1024, tn=512, tk=1024, name="dh_e")
    grads[("w_in_e", 0)] = _mm(h_e, dproj_e, ta=True, tm=1024, tn=512, tk=512, name="dw_in_e")
    dx, rgrads["mix_norm_e"] = _rms_bwd(x0, dh, mix_norm_e, dx, name="rms_mix_e_bwd")
    rgrads["ffn_norm"] = jnp.concatenate([d_ffn0, d_ffn1], axis=0)
    rgrads["ple_norm"] = jnp.concatenate([d_ple0, d_ple1], axis=0)
    rgrads["final_norm"] = d_final.reshape(D_MODEL)

    gnames = tuple(GROUPS)
    smaj = {name: _shard_major(name, [grads[(name, i)] for i in range(2) if (name, i) in grads],
                               sum((name, i) in grads for i in range(2))) for name in SHARDED}
    gbuf = [_group_rows(smaj, g) for g in gnames]
    gbuf = [b.reshape((4, 2) + b.shape[1:]) for b in gbuf]
    got1 = _exchange_sibling(gbuf)
    part = [_chip_partial(place, b, r, tr=GROUPS[g][1], name="rs_chip_partial_" + g) for g, b, r in zip(gnames, gbuf, got1)]
    got2 = _exchange_chips(part)
    wloc, mloc, vloc = pieces(""), pieces("m_"), pieces("v_")
    sh_out = [{}, {}, {}, {}]
    for g, b, r1, r2 in zip(gnames, gbuf, got1, got2):
        res = _adamw_shard(place, b, r1, r2, _group_rows(wloc, g), _group_rows(mloc, g), _group_rows(vloc, g),
                           tr=GROUPS[g][1], name="adamw_" + g)
        for kind in range(4):
            sh_out[kind].update(_ungroup_rows(res[kind], g))

    (rparts,) = _all_gather([_pack_repl(rgrads)], name="ag_repl_grads")
    rp_out = _adamw_replicated(rparts, _pack_repl({n: given[n] for n, _, _ in REPL_LAYOUT}),
                               _pack_repl({n: given["m_" + n] for n, _, _ in REPL_LAYOUT}),
                               _pack_repl({n: given["v_" + n] for n, _, _ in REPL_LAYOUT}))
    rp_out = [_unpack_repl(b) for b in rp_out]

    def leaf(kind, name):
        if name in SHARDED:
            return sh_out[kind][name].reshape(given[name].shape)
        return rp_out[kind][name]

    loss = lax.psum(loss_row[0, 0], ("x", "y", "c"))
    outs = [loss, dx[None]]
    for kind in range(4):
        outs += [leaf(kind, n) for n in WEIGHTS]
    return tuple(outs)
```

```python
import functools

import jax
import jax.numpy as jnp
from jax import lax
from jax.experimental import pallas as pl
from jax.experimental.pallas import tpu as pltpu

F32 = jnp.float32
BF16 = jnp.bfloat16

D_MODEL = 1024
PLE_DIM = 256
POOL_WINDOWS = (2, 4, 8, 16)
POOL_WIDTH = 512
SB_HEAD_DIM = 64
SB_BLOCK = 1024
SB_KBLOCK = 256
GDN_HEADS = 8
GDN_HEAD_DIM = 128
GDN_CONV = 4
GDN_CHUNK = 64
FFN_DIM = 2816
FFN_CONV = 3
EPS = 1e-6
ADAM_LR, ADAM_B1, ADAM_B2, ADAM_EPS, ADAM_WD, ADAM_STEP = 0.001, 0.9, 0.999, 1e-08, 0.01, 10
N_DEV = 8
MESH = pl.DeviceIdType.MESH
VMEM_LIMIT = 56 * 1024 * 1024

NN = (((1,), (0,)), ((), ()))
NT = (((1,), (1,)), ((), ()))
TN = (((0,), (0,)), ((), ()))


def _params(*sem):
    return pltpu.CompilerParams(dimension_semantics=sem if sem else None, vmem_limit_bytes=VMEM_LIMIT)


def _dot(a, b, dims):
    return lax.dot_general(a.astype(BF16), b.astype(BF16), dims, preferred_element_type=F32)


def _iota(shape, axis):
    return lax.broadcasted_iota(jnp.int32, shape, axis)


def _mm(a, b, *, ta=False, tb=False, res=None, out_dtype=F32, tm=512, tn=512, tk=None, name):
    M, K = (a.shape[1], a.shape[0]) if ta else a.shape
    N = b.shape[0] if tb else b.shape[1]
    tk = K if tk is None else min(tk, K)
    tm, tn = min(tm, M), min(tn, N)
    assert M % tm == 0 and N % tn == 0 and K % tk == 0, (name, M, N, K, tm, tn, tk)
    nk = K // tk
    dims = (((0 if ta else 1,), (1 if tb else 0,)), ((), ()))

    def body(*refs):
        if res is None:
            a_ref, b_ref, o_ref, *scr = refs
            r_ref = None
        else:
            a_ref, b_ref, r_ref, o_ref, *scr = refs
        p = _dot(a_ref[...], b_ref[...], dims)

        def fin(acc):
            if r_ref is not None:
                acc = acc + r_ref[...]
            o_ref[...] = acc.astype(out_dtype)

        if nk == 1:
            fin(p)
        else:
            acc_ref = scr[0]
            k = pl.program_id(2)

            @pl.when(k == 0)
            def _():
                acc_ref[...] = p

            @pl.when(k > 0)
            def _():
                acc_ref[...] += p

            @pl.when(k == nk - 1)
            def _():
                fin(acc_ref[...])

    a_spec = pl.BlockSpec((tk, tm), lambda i, j, k: (k, i)) if ta else pl.BlockSpec((tm, tk), lambda i, j, k: (i, k))
    b_spec = pl.BlockSpec((tn, tk), lambda i, j, k: (j, k)) if tb else pl.BlockSpec((tk, tn), lambda i, j, k: (k, j))
    o_spec = pl.BlockSpec((tm, tn), lambda i, j, k: (i, j))
    in_specs = [a_spec, b_spec] + ([o_spec] if res is not None else [])
    args = (a, b) + ((res,) if res is not None else ())
    return pl.pallas_call(
        body, name=name, grid=(M // tm, N // tn, nk), in_specs=in_specs, out_specs=o_spec,
        out_shape=jax.ShapeDtypeStruct((M, N), out_dtype),
        scratch_shapes=[pltpu.VMEM((tm, tn), F32)] if nk > 1 else [],
        compiler_params=_params("parallel", "parallel", "arbitrary"),
    )(*args)


def _rms_fwd(x, gain, *, name, tr=512):
    T, Dm = x.shape

    def body(x_ref, g_ref, o_ref):
        xv = x_ref[...]
        r = lax.rsqrt(jnp.mean(xv * xv, axis=-1, keepdims=True) + EPS)
        o_ref[...] = (xv * r * g_ref[...]).astype(BF16)

    return pl.pallas_call(
        body, name=name, grid=(T // tr,),
        in_specs=[pl.BlockSpec((tr, Dm), lambda i: (i, 0)), pl.BlockSpec((1, Dm), lambda i: (0, 0))],
        out_specs=pl.BlockSpec((tr, Dm), lambda i: (i, 0)),
        out_shape=jax.ShapeDtypeStruct((T, Dm), BF16), compiler_params=_params("parallel"),
    )(x, gain)


def _rms_bwd(x, dy, gain, dres, *, name, tr=512):
    T, Dm = x.shape

    def body(x_ref, dy_ref, g_ref, dres_ref, dx_ref, dg_ref):
        i = pl.program_id(0)
        xv = x_ref[...]
        dy_v = dy_ref[...].astype(F32)
        r = lax.rsqrt(jnp.mean(xv * xv, axis=-1, keepdims=True) + EPS)
        xn = xv * r
        dgp = jnp.sum(dy_v * xn, axis=0, keepdims=True)
        dyg = dy_v * g_ref[...]
        dx = r * (dyg - xn * jnp.mean(dyg * xn, axis=-1, keepdims=True))
        dx_ref[...] = dres_ref[...] + dx

        @pl.when(i == 0)
        def _():
            dg_ref[...] = dgp

        @pl.when(i > 0)
        def _():
            dg_ref[...] += dgp

    row = pl.BlockSpec((tr, Dm), lambda i: (i, 0))
    vec = pl.BlockSpec((1, Dm), lambda i: (0, 0))
    return pl.pallas_call(
        body, name=name, grid=(T // tr,), in_specs=[row, row, vec, row], out_specs=[row, vec],
        out_shape=[jax.ShapeDtypeStruct((T, Dm), F32), jax.ShapeDtypeStruct((1, Dm), F32)],
        compiler_params=_params("arbitrary"),
    )(x, dy, gain, dres)


def _final_loss(x, gain, target, *, tr=512):
    T, Dm = x.shape

    def body(x_ref, g_ref, t_ref, loss_ref, dx_ref, dg_ref):
        i = pl.program_id(0)
        xv = x_ref[...]
        g = g_ref[...]
        r = lax.rsqrt(jnp.mean(xv * xv, axis=-1, keepdims=True) + EPS)
        xn = xv * r
        err = xn * g - t_ref[...]
        lp = jnp.zeros((1, 128), F32) + 0.5 * jnp.sum(jnp.mean(err * err, axis=-1, keepdims=True))
        dy_v = err * (1.0 / Dm)
        dgp = jnp.sum(dy_v * xn, axis=0, keepdims=True)
        dyg = dy_v * g
        dx_ref[...] = r * (dyg - xn * jnp.mean(dyg * xn, axis=-1, keepdims=True))

        @pl.when(i == 0)
        def _():
            dg_ref[...] = dgp
            loss_ref[...] = lp

        @pl.when(i > 0)
        def _():
            dg_ref[...] += dgp
            loss_ref[...] += lp

    row = pl.BlockSpec((tr, Dm), lambda i: (i, 0))
    vec = pl.BlockSpec((1, Dm), lambda i: (0, 0))
    return pl.pallas_call(
        body, name="final_loss", grid=(T // tr,), in_specs=[row, vec, row],
        out_specs=[pl.BlockSpec((1, 128), lambda i: (0, 0)), row, vec],
        out_shape=[jax.ShapeDtypeStruct((1, 128), F32), jax.ShapeDtypeStruct((T, Dm), F32),
                   jax.ShapeDtypeStruct((1, Dm), F32)],
        compiler_params=_params("arbitrary"),
    )(x, gain, target)


def _prev_spec(tr, cb, pad, col):
    return pl.BlockSpec((pad, cb), lambda *g: (jnp.maximum(g[0] * (tr // pad) - 1, 0), col(*g)))


def _next_spec(tr, cb, pad, col, T):
    return pl.BlockSpec((pad, cb), lambda *g: (jnp.minimum((g[0] + 1) * (tr // pad), T // pad - 1), col(*g)))


def _conv_rows(x_ext, w_ref, K, pad):
    y = w_ref[K - 1:K, :] * x_ext
    for i in range(K - 1):
        y = y + w_ref[i:i + 1, :] * pltpu.roll(x_ext, K - 1 - i, 0)
    return y[pad:]


def _pool_y(u_ext, g, i, tr):
    s = u_ext
    for sh in (1, 2, 4, 8)[:g + 1]:
        s = s + pltpu.roll(s, sh, 0)
    t = i * tr + _iota((tr, 128), 0)
    cnt = jnp.minimum(t + 1, POOL_WINDOWS[g]).astype(F32)
    return s[16:] / cnt - u_ext[16:]


def _pool_fwd(proj, pool_w, pool_scale, *, tr=512):
    T = proj.shape[0]

    def body(u_ref, uh_ref, w_ref, s_ref, o_ref):
        i = pl.program_id(0)
        uh = jnp.where(i > 0, uh_ref[...], 0.0)
        for g in range(4):
            cs = slice(128 * g, 128 * (g + 1))
            y = _pool_y(jnp.concatenate([uh[:, cs], u_ref[:, cs]], axis=0), g, i, tr)
            o_ref[:, cs] = (_dot(y, w_ref[g], NN) * s_ref[:, cs]).astype(BF16)

    return pl.pallas_call(
        body, name="pool_fwd", grid=(T // tr,),
        in_specs=[pl.BlockSpec((tr, 512), lambda i: (i, 0)), _prev_spec(tr, 512, 16, lambda i: 0),
                  pl.BlockSpec((4, 128, 128), lambda i: (0, 0, 0)), pl.BlockSpec((1, 512), lambda i: (0, 0))],
        out_specs=pl.BlockSpec((tr, 512), lambda i: (i, 0)),
        out_shape=jax.ShapeDtypeStruct((T, 512), BF16), compiler_params=_params("parallel"),
    )(proj, proj, pool_w, pool_scale)


def _pool_bwd(proj, dout, pool_w, pool_scale, *, tr=512):
    T = proj.shape[0]
    nb = T // tr

    def body(u_ref, uh_ref, d_ref, dn_ref, w_ref, s_ref, du_ref, dw_ref, ds_ref):
        i = pl.program_id(0)
        uh = jnp.where(i > 0, uh_ref[...], 0.0)
        dn = jnp.where(i < nb - 1, dn_ref[...], 0.0)
        t_ext = i * tr + _iota((tr + 16, 128), 0)
        for g in range(4):
            cs = slice(128 * g, 128 * (g + 1))
            sc = s_ref[:, cs]
            wg = w_ref[g]
            y = _pool_y(jnp.concatenate([uh[:, cs], u_ref[:, cs]], axis=0), g, i, tr)
            dg = d_ref[:, cs]
            dsp = jnp.sum(dg * _dot(y, wg, NN), axis=0, keepdims=True)
            dyw = dg * sc
            dwp = _dot(y, dyw, TN)
            dy_ext = _dot(jnp.concatenate([dyw, dn[:, cs] * sc], axis=0), wg, NT)
            cnt = jnp.minimum(t_ext + 1, POOL_WINDOWS[g]).astype(F32)
            s = dy_ext / cnt
            for sh in (1, 2, 4, 8)[:g + 1]:
                s = s + pltpu.roll(s, tr + 16 - sh, 0)
            du_ref[:, cs] = (s[:tr] - dy_ext[:tr]).astype(BF16)

            @pl.when(i == 0)
            def _():
                dw_ref[g] = dwp
                ds_ref[:, cs] = dsp

            @pl.when(i > 0)
            def _():
                dw_ref[g] += dwp
                ds_ref[:, cs] += dsp

    row = pl.BlockSpec((tr, 512), lambda i: (i, 0))
    return pl.pallas_call(
        body, name="pool_bwd", grid=(nb,),
        in_specs=[row, _prev_spec(tr, 512, 16, lambda i: 0), row, _next_spec(tr, 512, 16, lambda i: 0, T),
                  pl.BlockSpec((4, 128, 128), lambda i: (0, 0, 0)), pl.BlockSpec((1, 512), lambda i: (0, 0))],
        out_specs=[row, pl.BlockSpec((4, 128, 128), lambda i: (0, 0, 0)), pl.BlockSpec((1, 512), lambda i: (0, 0))],
        out_shape=[jax.ShapeDtypeStruct((T, 512), BF16), jax.ShapeDtypeStruct((4, 128, 128), F32),
                   jax.ShapeDtypeStruct((1, 512), F32)],
        compiler_params=_params("arbitrary"),
    )(proj, proj, dout, dout, pool_w, pool_scale)


def _split_dot(x, tri):
    hi = x.astype(BF16)
    lo = (x - hi.astype(F32)).astype(BF16)
    return (lax.dot_general(hi, tri, NN, preferred_element_type=F32)
            + lax.dot_general(lo, tri, NN, preferred_element_type=F32))


def _log1m(z):
    return -(jnp.maximum(z, 0.0) + jnp.log(1.0 + jnp.exp(-jnp.abs(z))))


def _sb_fwd(proj):
    T = proj.shape[0]
    B, BK = min(SB_BLOCK, T), SB_KBLOCK
    R = B // BK
    scale = SB_HEAD_DIM ** -0.5

    def body(q_ref, k_ref, v_ref, o_ref, ls_ref):
        i = pl.program_id(1)
        lane = _iota((1, 128), 1)
        tri_gt = (_iota((BK, BK), 0) > _iota((BK, BK), 1)).astype(BF16)
        row, col = _iota((B, BK), 0), _iota((B, BK), 1)
        qv = q_ref[...] * scale
        hms = [(lane >= 64 * h) & (lane < 64 * (h + 1)) for h in range(2)]
        qhs = [jnp.where(hm, qv, 0.0).astype(BF16) for hm in hms]

        def tile(j, carry, d):
            rows = pl.ds(pl.multiple_of(j * BK, BK), BK)
            kj = k_ref[rows, :].astype(BF16)
            vj = v_ref[rows, :].astype(BF16)
            valid = None if d is None else (col + BK * d < row)
            out = []
            for h in range(2):
                c, acc = carry[h]
                z = lax.dot_general(qhs[h], kj, NT, preferred_element_type=F32)
                lg = _log1m(z)
                if d is not None:
                    lg = jnp.where(valid, lg, 0.0)
                a = jnp.exp(z + lg + _split_dot(lg, tri_gt) + c)
                if d is not None:
                    a = jnp.where(valid, a, 0.0)
                acc = acc + lax.dot_general(a.astype(BF16), vj, NN, preferred_element_type=F32)
                out.append((c + jnp.sum(lg, axis=1, keepdims=True), acc))
            return tuple(out)

        zero = (jnp.zeros((B, 1), F32), jnp.zeros((B, 128), F32))
        carry = (zero, zero)
        for d in reversed(range(R)):
            carry = tile(i * R + d, carry, d)
        carry = lax.fori_loop(0, i * R, lambda s, cr: tile(i * R - 1 - s, cr, None), carry)
        o_ref[...] = jnp.where(hms[0], carry[0][1], carry[1][1])
        ls_ref[...] = jnp.where(hms[0], carry[0][0], carry[1][0])

    blk = pl.BlockSpec((B, 128), lambda hp, i: (i, hp))
    return pl.pallas_call(
        body, name="sb_fwd", grid=(4, T // B),
        in_specs=[pl.BlockSpec((B, 128), lambda hp, i: (i, 4 + hp)),
                  pl.BlockSpec((T, 128), lambda hp, i: (0, 8 + hp)),
                  pl.BlockSpec((T, 128), lambda hp, i: (0, 12 + hp))],
        out_specs=[blk, blk],
        out_shape=[jax.ShapeDtypeStruct((T, 512), F32)] * 2, compiler_params=_params("parallel", "parallel"),
    )(proj, proj, proj)


def _sb_bwd(proj, lsum, dout):
    T = proj.shape[0]
    B, BK = min(SB_BLOCK, T), SB_KBLOCK
    R = B // BK
    scale = SB_HEAD_DIM ** -0.5

    def body(q_ref, k_ref, v_ref, do_ref, ls_ref, dq_ref, dk_ref, dv_ref):
        i = pl.program_id(1)

        @pl.when(i == 0)
        def _():
            dk_ref[...] = jnp.zeros_like(dk_ref)
            dv_ref[...] = jnp.zeros_like(dv_ref)

        lane = _iota((1, 128), 1)
        tri_le = (_iota((BK, BK), 0) <= _iota((BK, BK), 1)).astype(BF16)
        tri_lt = (_iota((BK, BK), 0) < _iota((BK, BK), 1)).astype(BF16)
        row, col = _iota((B, BK), 0), _iota((B, BK), 1)
        qv = q_ref[...] * scale
        dov = do_ref[...]
        hms = [(lane >= 64 * h) & (lane < 64 * (h + 1)) for h in range(2)]
        qhs = [jnp.where(hm, qv, 0.0).astype(BF16) for hm in hms]
        dos = [jnp.where(hm, dov, 0.0).astype(BF16) for hm in hms]
        ltots = [ls_ref[:, 64 * h:64 * h + 1] for h in range(2)]

        def tile(j, carry, d):
            rows = pl.ds(pl.multiple_of(j * BK, BK), BK)
            kj = k_ref[rows, :].astype(BF16)
            vj = v_ref[rows, :].astype(BF16)
            diag = d is not None
            valid = (col + BK * d < row) if diag else None
            out = []
            dkj = jnp.zeros((BK, 128), F32)
            dvj = jnp.zeros((BK, 128), F32)
            for h in range(2):
                lbef, ebef, dqa = carry[h]
                z = lax.dot_general(qhs[h], kj, NT, preferred_element_type=F32)
                lg = _log1m(z)
                if diag:
                    lg = jnp.where(valid, lg, 0.0)
                a = jnp.exp(z + lg + (ltots[h] - lbef - _split_dot(lg, tri_le)))
                if diag:
                    a = jnp.where(valid, a, 0.0)
                e = a * lax.dot_general(dos[h], vj, NT, preferred_element_type=F32)
                dz = e * jnp.exp(lg) - jnp.exp(z + lg) * (ebef + _split_dot(e, tri_lt))
                if diag:
                    dz = jnp.where(valid, dz, 0.0)
                dzb = dz.astype(BF16)
                dqa = dqa + lax.dot_general(dzb, kj, NN, preferred_element_type=F32)
                dkj = dkj + lax.dot_general(dzb, qhs[h], TN, preferred_element_type=F32)
                dvj = dvj + lax.dot_general(a.astype(BF16), dos[h], TN, preferred_element_type=F32)
                out.append((lbef + jnp.sum(lg, axis=1, keepdims=True), ebef + jnp.sum(e, axis=1, keepdims=True), dqa))
            dk_ref[rows, :] += dkj
            dv_ref[rows, :] += dvj
            return tuple(out)

        zero = (jnp.zeros((B, 1), F32), jnp.zeros((B, 1), F32), jnp.zeros((B, 128), F32))
        carry = lax.fori_loop(0, i * R, lambda j, cr: tile(j, cr, None), (zero, zero))
        for d in range(R):
            carry = tile(i * R + d, carry, d)
        dq_ref[...] = jnp.where(hms[0], carry[0][2], carry[1][2]) * scale

    full = pl.BlockSpec((T, 128), lambda hp, i: (0, hp))
    blk = pl.BlockSpec((B, 128), lambda hp, i: (i, hp))
    return pl.pallas_call(
        body, name="sb_bwd", grid=(4, T // B),
        in_specs=[pl.BlockSpec((B, 128), lambda hp, i: (i, 4 + hp)),
                  pl.BlockSpec((T, 128), lambda hp, i: (0, 8 + hp)),
                  pl.BlockSpec((T, 128), lambda hp, i: (0, 12 + hp)),
                  pl.BlockSpec((B, 128), lambda hp, i: (i, 4 + hp)), blk],
        out_specs=[blk, full, full],
        out_shape=[jax.ShapeDtypeStruct((T, 512), F32)] * 3,
        compiler_params=_params("parallel", "arbitrary"),
    )(proj, proj, proj, dout, lsum)


def _sigmoid(x):
    return 1.0 / (1.0 + jnp.exp(-x))


def _silu_mul(cg, cv):
    return cg * _sigmoid(cg) * cv


def _ffn_act(up, conv_w, *, tr=512, cb=256):
    T, F2 = up.shape
    nc = F2 // 2 // cb
    K = FFN_CONV

    def body(g_ref, gh_ref, v_ref, vh_ref, wg_ref, wv_ref, o_ref):
        i = pl.program_id(0)
        gh = jnp.where(i > 0, gh_ref[...], 0.0)
        vh = jnp.where(i > 0, vh_ref[...], 0.0)
        cg = _conv_rows(jnp.concatenate([gh, g_ref[...]], axis=0), wg_ref, K, 8)
        cv = _conv_rows(jnp.concatenate([vh, v_ref[...]], axis=0), wv_ref, K, 8)
        o_ref[...] = _silu_mul(cg, cv).astype(BF16)

    return pl.pallas_call(
        body, name="ffn_act", grid=(T // tr, nc),
        in_specs=[pl.BlockSpec((tr, cb), lambda i, j: (i, j)), _prev_spec(tr, cb, 8, lambda i, j: j),
                  pl.BlockSpec((tr, cb), lambda i, j: (i, nc + j)), _prev_spec(tr, cb, 8, lambda i, j: nc + j),
                  pl.BlockSpec((K, cb), lambda i, j: (0, j)), pl.BlockSpec((K, cb), lambda i, j: (0, nc + j))],
        out_specs=pl.BlockSpec((tr, cb), lambda i, j: (i, j)),
        out_shape=jax.ShapeDtypeStruct((T, F2 // 2), BF16), compiler_params=_params("parallel", "parallel"),
    )(up, up, up, up, conv_w, conv_w)


def _conv_bwd_rows(dc_ext, x_ext, w_ref, K, tr):
    n = tr + 8
    dx = w_ref[K - 1:K, :] * dc_ext
    for i in range(K - 1):
        dx = dx + w_ref[i:i + 1, :] * pltpu.roll(dc_ext, n - (K - 1 - i), 0)
    dc = dc_ext[:tr]
    dws = [jnp.sum(dc * pltpu.roll(x_ext, K - 1 - i, 0)[8:8 + tr], axis=0, keepdims=True) for i in range(K)]
    return dx[:tr], dws


def _acc_rows(ref, rows, first):
    for i, r in enumerate(rows):
        @pl.when(first)
        def _():
            ref[i:i + 1, :] = r

        @pl.when(jnp.logical_not(first))
        def _():
            ref[i:i + 1, :] += r


def _ffn_act_bwd(up, conv_w, dact, *, tr=512, cb=256):
    T, F2 = up.shape
    F = F2 // 2
    nc, nb = F // cb, T // tr
    K = FFN_CONV

    def body(g_ref, gp_ref, gn_ref, v_ref, vp_ref, vn_ref, d_ref, dn_ref, wg_ref, wv_ref,
             dg_ref, dv_ref, dwg_ref, dwv_ref):
        i = pl.program_id(1)
        first, last = i == 0, i == nb - 1
        g_ext = jnp.concatenate([jnp.where(first, 0.0, gp_ref[...]), g_ref[...], jnp.where(last, 0.0, gn_ref[...])], axis=0)
        v_ext = jnp.concatenate([jnp.where(first, 0.0, vp_ref[...]), v_ref[...], jnp.where(last, 0.0, vn_ref[...])], axis=0)
        d_ext = jnp.concatenate([d_ref[...], jnp.where(last, 0.0, dn_ref[...])], axis=0)
        cg = _conv_rows(g_ext, wg_ref, K, 8)
        cv = _conv_rows(v_ext, wv_ref, K, 8)
        _, vjp = jax.vjp(_silu_mul, cg, cv)
        dcg, dcv = vjp(d_ext)
        dg, dwg = _conv_bwd_rows(dcg, g_ext, wg_ref, K, tr)
        dv, dwv = _conv_bwd_rows(dcv, v_ext, wv_ref, K, tr)
        dg_ref[...] = dg.astype(BF16)
        dv_ref[...] = dv.astype(BF16)
        _acc_rows(dwg_ref, dwg, first)
        _acc_rows(dwv_ref, dwv, first)

    blk = lambda off: pl.BlockSpec((tr, cb), lambda j, i: (i, off + j))
    prev = lambda off: pl.BlockSpec((8, cb), lambda j, i: (jnp.maximum(i * (tr // 8) - 1, 0), off + j))
    nxt = lambda off: pl.BlockSpec((8, cb), lambda j, i: (jnp.minimum((i + 1) * (tr // 8), T // 8 - 1), off + j))
    wsp = lambda off: pl.BlockSpec((K, cb), lambda j, i: (0, off + j))
    return pl.pallas_call(
        body, name="ffn_act_bwd", grid=(nc, nb),
        in_specs=[blk(0), prev(0), nxt(0), blk(nc), prev(nc), nxt(nc), blk(0), nxt(0), wsp(0), wsp(nc)],
        out_specs=[blk(0), blk(0), wsp(0), wsp(0)],
        out_shape=[jax.ShapeDtypeStruct((T, F), BF16)] * 2 + [jax.ShapeDtypeStruct((K, F), F32)] * 2,
        compiler_params=_params("parallel", "arbitrary"),
    )(up, up, up, up, up, up, dact, dact, conv_w, conv_w)


def _ple_fwd(hn, w_gate, p, w_ple, x, *, name, tm=512, tn=512):
    T, Dm = x.shape

    def body(a_ref, b_ref, p_ref, wp_ref, x_ref, o_ref, gl_ref, pe_ref):
        gl = _dot(a_ref[...], b_ref[...], NN)
        pe = _dot(p_ref[...], wp_ref[...], NN)
        gl_ref[...] = gl
        pe_ref[...] = pe
        o_ref[...] = x_ref[...] + pe * _sigmoid(gl)

    o_spec = pl.BlockSpec((tm, tn), lambda i, j: (i, j))
    return pl.pallas_call(
        body, name=name, grid=(T // tm, Dm // tn),
        in_specs=[pl.BlockSpec((tm, Dm), lambda i, j: (i, 0)), pl.BlockSpec((Dm, tn), lambda i, j: (0, j)),
                  pl.BlockSpec((tm, PLE_DIM), lambda i, j: (i, 0)), pl.BlockSpec((PLE_DIM, tn), lambda i, j: (0, j)),
                  o_spec],
        out_specs=[o_spec] * 3, out_shape=[jax.ShapeDtypeStruct((T, Dm), F32)] * 3,
        compiler_params=_params("parallel", "parallel"),
    )(hn, w_gate, p, w_ple, x)


def _ple_bwd(dx, gl, pe, *, name, tr=512):
    T, Dm = dx.shape

    def body(dx_ref, gl_ref, pe_ref, dpe_ref, dgl_ref):
        g = _sigmoid(gl_ref[...])
        d = dx_ref[...]
        dpe_ref[...] = (d * g).astype(BF16)
        dgl_ref[...] = (d * pe_ref[...] * g * (1.0 - g)).astype(BF16)

    row = pl.BlockSpec((tr, Dm), lambda i: (i, 0))
    return pl.pallas_call(
        body, name=name, grid=(T // tr,), in_specs=[row] * 3, out_specs=[row] * 2,
        out_shape=[jax.ShapeDtypeStruct((T, Dm), BF16)] * 2, compiler_params=_params("parallel"),
    )(dx, gl, pe)


def _qkv_act(c, cb):
    s = c * _sigmoid(c)
    n = s * lax.rsqrt(jnp.sum(s * s, axis=-1, keepdims=True) + EPS)
    n = n * jnp.where(cb < GDN_HEADS, GDN_HEAD_DIM ** -0.5, 1.0)
    return jnp.where(cb < 2 * GDN_HEADS, n, s)


def _gdn_pre(proj, conv_w, *, tr=512):
    T = proj.shape[0]
    K = GDN_CONV

    def body(x_ref, xh_ref, w_ref, o_ref):
        i, cb = pl.program_id(0), pl.program_id(1)
        xh = jnp.where(i > 0, xh_ref[...], 0.0)
        c = _conv_rows(jnp.concatenate([xh, x_ref[...]], axis=0), w_ref, K, 8)
        o_ref[0] = _qkv_act(c, cb)

    return pl.pallas_call(
        body, name="gdn_pre", grid=(T // tr, 24),
        in_specs=[pl.BlockSpec((tr, 128), lambda i, j: (i, j)), _prev_spec(tr, 128, 8, lambda i, j: j),
                  pl.BlockSpec((K, 128), lambda i, j: (0, j))],
        out_specs=pl.BlockSpec((1, tr, 128), lambda i, j: (j, i, 0)),
        out_shape=jax.ShapeDtypeStruct((24, T, 128), F32), compiler_params=_params("parallel", "parallel"),
    )(proj, proj, conv_w)


def _gdn_pre_bwd(proj, conv_w, dqkv, *, tr=512):
    T = proj.shape[0]
    nb = T // tr
    K = GDN_CONV

    def body(x_ref, xp_ref, xn_ref, d_ref, dn_ref, w_ref, dx_ref, dw_ref):
        cb, i = pl.program_id(0), pl.program_id(1)
        first, last = i == 0, i == nb - 1
        x_ext = jnp.concatenate([jnp.where(first, 0.0, xp_ref[...]), x_ref[...], jnp.where(last, 0.0, xn_ref[...])], axis=0)
        d_ext = jnp.concatenate([d_ref[0], jnp.where(last, 0.0, dn_ref[0])], axis=0)
        c = _conv_rows(x_ext, w_ref, K, 8)
        _, vjp = jax.vjp(lambda c_: _qkv_act(c_, cb), c)
        (dc,) = vjp(d_ext)
        dx, dws = _conv_bwd_rows(dc, x_ext, w_ref, K, tr)
        dx_ref[...] = dx.astype(BF16)
        _acc_rows(dw_ref, dws, first)

    return pl.pallas_call(
        body, name="gdn_pre_bwd", grid=(24, nb),
        in_specs=[pl.BlockSpec((tr, 128), lambda j, i: (i, j)),
                  pl.BlockSpec((8, 128), lambda j, i: (jnp.maximum(i * (tr // 8) - 1, 0), j)),
                  pl.BlockSpec((8, 128), lambda j, i: (jnp.minimum((i + 1) * (tr // 8), T // 8 - 1), j)),
                  pl.BlockSpec((1, tr, 128), lambda j, i: (j, i, 0)),
                  pl.BlockSpec((1, 8, 128), lambda j, i: (j, jnp.minimum((i + 1) * (tr // 8), T // 8 - 1), 0)),
                  pl.BlockSpec((K, 128), lambda j, i: (0, j))],
        out_specs=[pl.BlockSpec((tr, 128), lambda j, i: (i, j)), pl.BlockSpec((K, 128), lambda j, i: (0, j))],
        out_shape=[jax.ShapeDtypeStruct((T, 24 * 128), BF16), jax.ShapeDtypeStruct((K, 24 * 128), F32)],
        compiler_params=_params("parallel", "arbitrary"),
    )(proj, proj, proj, dqkv, dqkv, conv_w)


def _gate_fn(ba, alog_row, dt_row):
    lane = _iota((1, 128), 1)
    x = ba + dt_row
    sp = jnp.maximum(x, 0.0) + jnp.log(1.0 + jnp.exp(-jnp.abs(x)))
    return jnp.where(lane < GDN_HEADS, _sigmoid(ba), -jnp.exp(alog_row) * sp)


def _gdn_gate(ba, alog_row, dt_row, *, tr=512):
    T = ba.shape[0]

    def body(ba_ref, al_ref, dt_ref, b_ref, g_ref):
        val = _gate_fn(ba_ref[...], al_ref[...], dt_ref[...])
        for h in range(GDN_HEADS):
            b_ref[h] = val[:, h:h + 1]
            g_ref[h] = val[:, GDN_HEADS + h:GDN_HEADS + h + 1]

    vec = pl.BlockSpec((1, 128), lambda i: (0, 0))
    hm = pl.BlockSpec((GDN_HEADS, tr, 1), lambda i: (0, i, 0))
    return pl.pallas_call(
        body, name="gdn_gate", grid=(T // tr,), in_specs=[pl.BlockSpec((tr, 128), lambda i: (i, 0)), vec, vec],
        out_specs=[hm, hm], out_shape=[jax.ShapeDtypeStruct((GDN_HEADS, T, 1), F32)] * 2,
        compiler_params=_params("parallel"),
    )(ba, alog_row, dt_row)


def _gdn_gate_bwd(ba, alog_row, dt_row, dbeta, dg, *, tr=512):
    T = ba.shape[0]

    def body(ba_ref, al_ref, dt_ref, db_ref, dg_ref, dba_ref, dal_ref, ddt_ref):
        i = pl.program_id(0)
        lane = _iota((1, 128), 1)
        d = jnp.zeros((tr, 128), F32)
        for h in range(GDN_HEADS):
            d = d + jnp.where(lane == h, db_ref[h], 0.0) + jnp.where(lane == GDN_HEADS + h, dg_ref[h], 0.0)
        _, vjp = jax.vjp(_gate_fn, ba_ref[...], al_ref[...], dt_ref[...])
        dba, dal, ddt = vjp(d)
        dba_ref[...] = dba.astype(BF16)

        @pl.when(i == 0)
        def _():
            dal_ref[...] = dal
            ddt_ref[...] = ddt

        @pl.when(i > 0)
        def _():
            dal_ref[...] += dal
            ddt_ref[...] += ddt

    vec = pl.BlockSpec((1, 128), lambda i: (0, 0))
    hm = pl.BlockSpec((GDN_HEADS, tr, 1), lambda i: (0, i, 0))
    row = pl.BlockSpec((tr, 128), lambda i: (i, 0))
    return pl.pallas_call(
        body, name="gdn_gate_bwd", grid=(T // tr,), in_specs=[row, vec, vec, hm, hm], out_specs=[row, vec, vec],
        out_shape=[jax.ShapeDtypeStruct((T, 128), BF16), jax.ShapeDtypeStruct((1, 128), F32),
                   jax.ShapeDtypeStruct((1, 128), F32)],
        compiler_params=_params("arbitrary"),
    )(ba, alog_row, dt_row, dbeta, dg)


def _split3(x):
    x1 = x.astype(BF16)
    r = x - x1.astype(F32)
    x2 = r.astype(BF16)
    return x1, x2, (r - x2.astype(F32)).astype(BF16)


def _dot01(tri, x, dims):
    t = tri.astype(BF16)
    x1, x2, x3 = _split3(x)
    d = lambda xi: lax.dot_general(t, xi, dims, preferred_element_type=F32)
    return d(x1) + (d(x2) + d(x3))


def _dot3(a, b, dims):
    ah, al, _ = _split3(a)
    bh, bl, _ = _split3(b)
    d = lambda p, q: lax.dot_general(p, q, dims, preferred_element_type=F32)
    return d(ah, bh) + (d(ah, bl) + d(al, bh))


BNN = (((2,), (1,)), ((0,), (0,)))
BNT = (((2,), (2,)), ((0,), (0,)))
BTN = (((1,), (1,)), ((0,), (0,)))


@jax.custom_vjp
def _mm01(tri, x):
    return _dot01(tri, x, BNN)


def _mm01_fwd(tri, x):
    return _dot01(tri, x, BNN), tri


def _mm01_bwd(tri, ct):
    return jnp.zeros_like(tri), _dot01(tri, ct, BTN)


_mm01.defvjp(_mm01_fwd, _mm01_bwd)


def _unit_lower_inverse(a):
    C = a.shape[-1]
    eye = (_iota(a.shape, 1) == _iota(a.shape, 2)).astype(F32)
    pw = -a
    tinv = eye + pw
    for _ in range(5):
        pw = _dot3(pw, pw, BNN)
        tinv = tinv + _dot3(tinv, pw, BNN)
    return tinv


@jax.custom_vjp
def _unit_lower_solve(a, rv, rw):
    return _unit_lower_solve_fwd(a, rv, rw)[0]


def _unit_lower_solve_fwd(a, rv, rw):
    tinv = _unit_lower_inverse(a)
    sol = _dot3(tinv, jnp.concatenate([rv, rw], axis=2), BNN)
    n = rv.shape[2]
    return (sol[:, :, :n], sol[:, :, n:]), (tinv, sol)


def _unit_lower_solve_bwd(res, cts):
    tinv, sol = res
    n = cts[0].shape[2]
    d_rhs = _dot3(tinv, jnp.concatenate(cts, axis=2), BTN)
    return -_dot3(d_rhs, sol, BNT), d_rhs[:, :, :n], d_rhs[:, :, n:]


_unit_lower_solve.defvjp(_unit_lower_solve_fwd, _unit_lower_solve_bwd)


@jax.custom_vjp
def _mmb_nt(a, b):
    return _dot(a, b, BNT)


def _mmb_nt_fwd(a, b):
    return _dot(a, b, BNT), (a, b)


def _mmb_nt_bwd(res, ct):
    a, b = res
    return _dot(ct, b, BNN), _dot(ct, a, BTN)


_mmb_nt.defvjp(_mmb_nt_fwd, _mmb_nt_bwd)


def _gdn_chunk(q, k, v, gcol, bcol):
    nb, C = q.shape[0], GDN_CHUNK
    row, col = _iota((nb, C, C), 1), _iota((nb, C, C), 2)
    incl, strict = row >= col, row > col
    eye = (row == col).astype(F32)
    lower = incl.astype(F32)
    ones = jnp.ones((nb, C, C), F32)
    gwide = jnp.broadcast_to(gcol, (nb, C, GDN_HEAD_DIM))
    gc = _mm01(lower, gwide)
    gtot = _mm01(ones, gwide)
    gc_c = _mm01(lower, jnp.broadcast_to(gcol, (nb, C, C)))
    gc_s = _mm01(ones, gc_c * eye)
    decay = jnp.where(incl, jnp.exp(jnp.where(incl, gc_c - gc_s, 0.0)), 0.0)
    kb = k * bcol
    a = jnp.where(strict, _mmb_nt(kb, k) * decay, 0.0)
    egc = jnp.exp(gc)
    u, w = _unit_lower_solve(a, v * bcol, kb * egc)
    qk = jnp.where(incl, _mmb_nt(q, k) * decay, 0.0)
    return u, w, qk, q * egc, k * jnp.exp(gtot - gc), jnp.exp(jnp.sum(gwide, axis=1))


GDN_ROWS = 8 * GDN_CHUNK


def _gdn_specs(T):
    hd = lambda off: pl.BlockSpec((1, GDN_ROWS, 128), lambda h, i: (off + h, i, 0))
    col = pl.BlockSpec((1, GDN_ROWS, 1), lambda h, i: (h, i, 0))
    sq = pl.BlockSpec((1, GDN_ROWS, GDN_CHUNK), lambda h, i: (h, i, 0))
    gl = pl.BlockSpec((1, 8, 128), lambda h, i: (h, i, 0))
    return hd, col, sq, gl


def _gdn_local(qkv, g, beta):
    T = qkv.shape[1]
    hd, col, sq, gl_spec = _gdn_specs(T)

    def body(q_ref, k_ref, v_ref, g_ref, b_ref, u_ref, w_ref, qk_ref, qd_ref, kd_ref, gl_ref):
        chunks = lambda ref: ref[0].reshape(8, GDN_CHUNK, ref.shape[2])
        rows = lambda val: val.reshape(GDN_ROWS, val.shape[2])
        u, w, qk, qd, kd, gl = _gdn_chunk(chunks(q_ref), chunks(k_ref), chunks(v_ref), chunks(g_ref), chunks(b_ref))
        u_ref[0] = rows(u)
        w_ref[0] = rows(w).astype(BF16)
        qk_ref[0] = rows(qk).astype(BF16)
        qd_ref[0] = rows(qd).astype(BF16)
        kd_ref[0] = rows(kd).astype(BF16)
        gl_ref[0] = gl

    H = GDN_HEADS
    return pl.pallas_call(
        body, name="gdn_local", grid=(H, T // GDN_ROWS),
        in_specs=[hd(0), hd(H), hd(2 * H), col, col],
        out_specs=[hd(0), hd(0), sq, hd(0), hd(0), gl_spec],
        out_shape=[jax.ShapeDtypeStruct((H, T, 128), F32), jax.ShapeDtypeStruct((H, T, 128), BF16),
                   jax.ShapeDtypeStruct((H, T, GDN_CHUNK), BF16), jax.ShapeDtypeStruct((H, T, 128), BF16),
                   jax.ShapeDtypeStruct((H, T, 128), BF16), jax.ShapeDtypeStruct((H, T // GDN_CHUNK, 128), F32)],
        compiler_params=_params("parallel", "parallel"),
    )(qkv, qkv, qkv, g, beta)


def _gdn_local_bwd(qkv, g, beta, du, dw, dqk, dqd, dkd, dgl):
    T = qkv.shape[1]
    hd, col, sq, gl_spec = _gdn_specs(T)

    def body(q_ref, k_ref, v_ref, g_ref, b_ref, du_ref, dw_ref, dqk_ref, dqd_ref, dkd_ref, dgl_ref,
             dq_ref, dk_ref, dv_ref, dg_ref, db_ref):
        chunks = lambda ref: ref[0].reshape(8, GDN_CHUNK, ref.shape[2])
        rows = lambda val: val.reshape(GDN_ROWS, val.shape[2])
        _, vjp = jax.vjp(_gdn_chunk, chunks(q_ref), chunks(k_ref), chunks(v_ref), chunks(g_ref), chunks(b_ref))
        dq, dk, dv, dg, db = vjp((chunks(du_ref), chunks(dw_ref), chunks(dqk_ref), chunks(dqd_ref), chunks(dkd_ref),
                                  dgl_ref[0]))
        dq_ref[0] = rows(dq)
        dk_ref[0] = rows(dk)
        dv_ref[0] = rows(dv)
        dg_ref[0] = rows(dg)
        db_ref[0] = rows(db)

    H = GDN_HEADS
    big = jax.ShapeDtypeStruct((H, T, 128), F32)
    small = jax.ShapeDtypeStruct((H, T, 1), F32)
    return pl.pallas_call(
        body, name="gdn_local_bwd", grid=(H, T // GDN_ROWS),
        in_specs=[hd(0), hd(H), hd(2 * H), col, col, hd(0), hd(0), sq, hd(0), hd(0), gl_spec],
        out_specs=[hd(0), hd(0), hd(0), col, col], out_shape=[big, big, big, small, small],
        compiler_params=_params("parallel", "parallel"),
    )(qkv, qkv, qkv, g, beta, du, dw, dqk, dqd, dkd, dgl)


GDN_HB = 4


def _gdn_scan_specs(T, rev):
    nb = T // GDN_ROWS
    blk = (lambda i: nb - 1 - i) if rev else (lambda i: i)
    hd = pl.BlockSpec((GDN_HB, GDN_ROWS, 128), lambda h, i: (h, blk(i), 0))
    sq = pl.BlockSpec((GDN_HB, GDN_ROWS, GDN_CHUNK), lambda h, i: (h, blk(i), 0))
    gl = pl.BlockSpec((GDN_HB, 8, 128), lambda h, i: (h, blk(i), 0))
    st = pl.BlockSpec((GDN_HB, 8, 128, 128), lambda h, i: (h, blk(i), 0, 0))
    return hd, sq, gl, st


def _gdn_scan(u, w, qk, qd, kd, gl):
    H, T, _ = u.shape
    hd, sq, gl_spec, st = _gdn_scan_specs(T, False)

    def body(u_ref, w_ref, qk_ref, qd_ref, kd_ref, gl_ref, o_ref, ss_ref, vn_ref, s_scr):
        @pl.when(pl.program_id(1) == 0)
        def _():
            s_scr[...] = jnp.zeros_like(s_scr)

        dot = lambda a, b, dims: lax.dot_general(a, b, dims, preferred_element_type=F32)
        s = s_scr[...]
        for c in range(8):
            rs = slice(GDN_CHUNK * c, GDN_CHUNK * (c + 1))
            ss_ref[:, c] = s
            sb = s.astype(BF16)
            vn = u_ref[:, rs, :] - dot(w_ref[:, rs, :], sb, BNN)
            vnb = vn.astype(BF16)
            o_ref[:, rs, :] = dot(qd_ref[:, rs, :], sb, BNN) + dot(qk_ref[:, rs, :], vnb, BNN)
            vn_ref[:, rs, :] = vnb
            s = s * gl_ref[:, c:c + 1, :] + dot(kd_ref[:, rs, :], vnb, BTN)
        s_scr[...] = s

    return pl.pallas_call(
        body, name="gdn_scan", grid=(H // GDN_HB, T // GDN_ROWS),
        in_specs=[hd, hd, sq, hd, hd, gl_spec], out_specs=[hd, st, hd],
        out_shape=[jax.ShapeDtypeStruct((H, T, 128), F32), jax.ShapeDtypeStruct((H, T // GDN_CHUNK, 128, 128), F32),
                   jax.ShapeDtypeStruct((H, T, 128), BF16)],
        scratch_shapes=[pltpu.VMEM((GDN_HB, 128, 128), F32)],
        compiler_params=_params("parallel", "arbitrary"),
    )(u, w, qk, qd, kd, gl)


def _gdn_scan_bwd(do, ss, vn, w, qk, qd, kd, gl):
    H, T, _ = do.shape
    hd, sq, gl_spec, st = _gdn_scan_specs(T, True)

    def body(do_ref, ss_ref, vn_ref, w_ref, qk_ref, qd_ref, kd_ref, gl_ref,
             du_ref, dw_ref, dqk_ref, dqd_ref, dkd_ref, dgl_ref, ds_scr):
        @pl.when(pl.program_id(1) == 0)
        def _():
            ds_scr[...] = jnp.zeros_like(ds_scr)

        dot = lambda a, b, dims: lax.dot_general(a, b, dims, preferred_element_type=F32)
        ds = ds_scr[...]
        for c in reversed(range(8)):
            rs = slice(GDN_CHUNK * c, GDN_CHUNK * (c + 1))
            s = ss_ref[:, c]
            sb, dsb = s.astype(BF16), ds.astype(BF16)
            dob = do_ref[:, rs, :].astype(BF16)
            vnb = vn_ref[:, rs, :]
            dvn = dot(qk_ref[:, rs, :], dob, BTN) + dot(kd_ref[:, rs, :], dsb, BNN)
            dvnb = dvn.astype(BF16)
            du_ref[:, rs, :] = dvn
            dw_ref[:, rs, :] = -dot(dvnb, sb, BNT)
            dqk_ref[:, rs, :] = dot(dob, vnb, BNT)
            dqd_ref[:, rs, :] = dot(dob, sb, BNT)
            dkd_ref[:, rs, :] = dot(vnb, dsb, BNT)
            dgl_ref[:, c:c + 1, :] = jnp.sum(ds * s, axis=1, keepdims=True)
            ds = dot(qd_ref[:, rs, :], dob, BTN) + ds * gl_ref[:, c:c + 1, :] - dot(w_ref[:, rs, :], dvnb, BTN)
        ds_scr[...] = ds

    big = jax.ShapeDtypeStruct((H, T, 128), F32)
    return pl.pallas_call(
        body, name="gdn_scan_bwd", grid=(H // GDN_HB, T // GDN_ROWS),
        in_specs=[hd, st, hd, hd, sq, hd, hd, gl_spec], out_specs=[hd, hd, sq, hd, hd, gl_spec],
        out_shape=[big, big, jax.ShapeDtypeStruct((H, T, GDN_CHUNK), F32), big, big,
                   jax.ShapeDtypeStruct((H, T // GDN_CHUNK, 128), F32)],
        scratch_shapes=[pltpu.VMEM((GDN_HB, 128, 128), F32)],
        compiler_params=_params("parallel", "arbitrary"),
    )(do, ss, vn, w, qk, qd, kd, gl)


def _gated_norm(o, z, nw):
    on = o * lax.rsqrt(jnp.mean(o * o, axis=-1, keepdims=True) + EPS) * nw
    return on * (z * _sigmoid(z))


def _gdn_post(o, proj, norm_w, *, tr=512):
    T = proj.shape[0]

    def body(o_ref, z_ref, n_ref, y_ref):
        y_ref[...] = _gated_norm(o_ref[0], z_ref[...], n_ref[...]).astype(BF16)

    return pl.pallas_call(
        body, name="gdn_post", grid=(T // tr, GDN_HEADS),
        in_specs=[pl.BlockSpec((1, tr, 128), lambda i, h: (h, i, 0)), pl.BlockSpec((tr, 128), lambda i, h: (i, 24 + h)),
                  pl.BlockSpec((1, 128), lambda i, h: (0, 0))],
        out_specs=pl.BlockSpec((tr, 128), lambda i, h: (i, h)),
        out_shape=jax.ShapeDtypeStruct((T, 1024), BF16), compiler_params=_params("parallel", "parallel"),
    )(o, proj, norm_w)


def _gdn_post_bwd(o, proj, norm_w, dy, *, tr=512):
    T = proj.shape[0]

    def body(o_ref, z_ref, n_ref, dy_ref, do_ref, dz_ref, dn_ref):
        first = (pl.program_id(0) == 0) & (pl.program_id(1) == 0)
        _, vjp = jax.vjp(_gated_norm, o_ref[0], z_ref[...], n_ref[...])
        do, dz, dn = vjp(dy_ref[...])
        do_ref[0] = do
        dz_ref[...] = dz.astype(BF16)

        @pl.when(first)
        def _():
            dn_ref[...] = dn

        @pl.when(jnp.logical_not(first))
        def _():
            dn_ref[...] += dn

    blk = pl.BlockSpec((tr, 128), lambda i, h: (i, h))
    hm = pl.BlockSpec((1, tr, 128), lambda i, h: (h, i, 0))
    vec = pl.BlockSpec((1, 128), lambda i, h: (0, 0))
    return pl.pallas_call(
        body, name="gdn_post_bwd", grid=(T // tr, GDN_HEADS),
        in_specs=[hm, pl.BlockSpec((tr, 128), lambda i, h: (i, 24 + h)), vec, blk], out_specs=[hm, blk, vec],
        out_shape=[jax.ShapeDtypeStruct((GDN_HEADS, T, 128), F32), jax.ShapeDtypeStruct((T, 1024), BF16),
                   jax.ShapeDtypeStruct((1, 128), F32)],
        compiler_params=_params("arbitrary", "arbitrary"),
    )(o, proj, norm_w, dy)


HBM_SPEC = pl.BlockSpec(memory_space=pltpu.HBM)


def _place():
    return lax.axis_index("x"), lax.axis_index("y"), lax.axis_index("c")


def _all_gather(vs, *, name):
    n = len(vs)

    def body(*refs):
        v_refs, out_refs = refs[:n], refs[n:2 * n]
        send_sems, recv_sems, local_sems = refs[2 * n:]
        x, y, c = _place()
        me, sibling = (x, y, c), (x, y, 1 - c)
        chips = [(1 - x, y), (x, 1 - y), (1 - x, 1 - y)]

        def copy(a, k, block, to, from_input=False):
            slot = out_refs[a].at[4 * block[0] + 2 * block[1] + block[2]]
            return pltpu.make_async_remote_copy(
                src_ref=v_refs[a] if from_input else slot, dst_ref=slot,
                send_sem=send_sems.at[7 * a + k], recv_sem=recv_sems.at[7 * a + k], device_id=to, device_id_type=MESH)

        mine = [pltpu.make_async_copy(v_refs[a], out_refs[a].at[4 * x + 2 * y + c], local_sems.at[a]) for a in range(n)]
        first = [copy(a, 0, me, sibling, True) for a in range(n)]
        first += [copy(a, 1 + j, me, (*chip, c), True) for j, chip in enumerate(chips) for a in range(n)]
        for cp in mine + first:
            cp.start()
        passed = []
        for j, chip in enumerate(chips):
            for a in range(n):
                copy(a, 1 + j, (*chip, c), me).wait_recv()
                passed.append(copy(a, 4 + j, (*chip, c), sibling))
                passed[-1].start()
        for a in range(n):
            copy(a, 0, sibling, me).wait_recv()
            for j, chip in enumerate(chips):
                copy(a, 4 + j, (*chip, 1 - c), me).wait_recv()
        for cp in first + passed:
            cp.wait_send()
        for cp in mine:
            cp.wait()

    return pl.pallas_call(
        body, name=name, out_shape=[jax.ShapeDtypeStruct((N_DEV,) + v.shape, v.dtype) for v in vs],
        in_specs=[HBM_SPEC] * n, out_specs=[HBM_SPEC] * n,
        scratch_shapes=[pltpu.SemaphoreType.DMA((7 * n,)), pltpu.SemaphoreType.DMA((7 * n,)),
                        pltpu.SemaphoreType.DMA((n,))],
    )(*vs)


def _exchange_sibling(gs):
    n = len(gs)

    def body(*refs):
        g_refs, out_refs = refs[:n], refs[n:2 * n]
        send_sems, recv_sems = refs[2 * n:]
        x, y, c = _place()
        copies = [pltpu.make_async_remote_copy(
            src_ref=g_refs[a].at[k, 1 - c], dst_ref=out_refs[a].at[k], send_sem=send_sems.at[4 * a + k],
            recv_sem=recv_sems.at[4 * a + k], device_id=(x, y, 1 - c), device_id_type=MESH)
            for a in range(n) for k in range(4)]
        for cp in copies:
            cp.start()
        for cp in copies:
            cp.wait()

    return pl.pallas_call(
        body, name="rs_sibling", out_shape=[jax.ShapeDtypeStruct((4,) + g.shape[2:], g.dtype) for g in gs],
        in_specs=[HBM_SPEC] * n, out_specs=[HBM_SPEC] * n,
        scratch_shapes=[pltpu.SemaphoreType.DMA((4 * n,)), pltpu.SemaphoreType.DMA((4 * n,))],
    )(*gs)


def _exchange_chips(pcs):
    n = len(pcs)

    def body(*refs):
        p_refs, out_refs = refs[:n], refs[n:2 * n]
        send_sems, recv_sems = refs[2 * n:]
        x, y, c = _place()
        chips = [(1 - x, y), (x, 1 - y), (1 - x, 1 - y)]
        copies = [pltpu.make_async_remote_copy(
            src_ref=p_refs[a].at[2 * cx + cy], dst_ref=out_refs[a].at[j], send_sem=send_sems.at[3 * a + j],
            recv_sem=recv_sems.at[3 * a + j], device_id=(cx, cy, c), device_id_type=MESH)
            for j, (cx, cy) in enumerate(chips) for a in range(n)]
        for cp in copies:
            cp.start()
        for cp in copies:
            cp.wait()

    return pl.pallas_call(
        body, name="rs_chips", out_shape=[jax.ShapeDtypeStruct((3,) + pc.shape[1:], pc.dtype) for pc in pcs],
        in_specs=[HBM_SPEC] * n, out_specs=[HBM_SPEC] * n,
        scratch_shapes=[pltpu.SemaphoreType.DMA((3 * n,)), pltpu.SemaphoreType.DMA((3 * n,))],
    )(*pcs)


def _chip_partial(place, g, got, *, tr, name):
    R, W = g.shape[2:]

    def body(pl_ref, g_ref, r_ref, o_ref):
        o_ref[...] = (g_ref[0] + r_ref[...]).astype(BF16)

    return pl.pallas_call(
        body, name=name, out_shape=jax.ShapeDtypeStruct((4, R, W), BF16),
        grid_spec=pltpu.PrefetchScalarGridSpec(
            num_scalar_prefetch=1, grid=(4, R // tr),
            in_specs=[pl.BlockSpec((1, 1, tr, W), lambda k, i, pr: (k, pr[2], i, 0)),
                      pl.BlockSpec((1, tr, W), lambda k, i, pr: (k, i, 0))],
            out_specs=pl.BlockSpec((1, tr, W), lambda k, i, pr: (k, i, 0))),
        compiler_params=_params("parallel", "parallel"),
    )(place, g, got)


def _adamw_math(g, w, m, v):
    m = ADAM_B1 * m + (1.0 - ADAM_B1) * g
    v = ADAM_B2 * v + (1.0 - ADAM_B2) * (g * g)
    m_hat = m / (1.0 - ADAM_B1 ** ADAM_STEP)
    v_hat = v / (1.0 - ADAM_B2 ** ADAM_STEP)
    return -ADAM_LR * (m_hat / (jnp.sqrt(v_hat) + ADAM_EPS) + ADAM_WD * w), m, v


def _adamw_shard(place, g, got1, got2, w, m, v, *, tr, name):
    R, W = w.shape

    def body(pl_ref, g_ref, r1_ref, r2_ref, w_ref, m_ref, v_ref, go_ref, d_ref, mo_ref, vo_ref):
        gs = g_ref[0, 0] + r1_ref[0]
        for j in range(3):
            gs = gs + r2_ref[j].astype(F32)
        go_ref[...] = gs
        d_ref[...], mo_ref[...], vo_ref[...] = _adamw_math(gs, w_ref[...], m_ref[...], v_ref[...])

    row = pl.BlockSpec((tr, W), lambda i, pr: (i, 0))
    out = jax.ShapeDtypeStruct((R, W), F32)
    return pl.pallas_call(
        body, name=name, out_shape=[out] * 4,
        grid_spec=pltpu.PrefetchScalarGridSpec(
            num_scalar_prefetch=1, grid=(R // tr,),
            in_specs=[pl.BlockSpec((1, 1, tr, W), lambda i, pr: (2 * pr[0] + pr[1], pr[2], i, 0)),
                      pl.BlockSpec((1, tr, W), lambda i, pr: (2 * pr[0] + pr[1], i, 0)),
                      pl.BlockSpec((3, tr, W), lambda i, pr: (0, i, 0)), row, row, row],
            out_specs=[row] * 4),
        compiler_params=_params("parallel"),
    )(place, g, got1, got2, w, m, v)


def _adamw_replicated(parts, w, m, v):
    R, W = w.shape

    def body(p_ref, w_ref, m_ref, v_ref, go_ref, d_ref, mo_ref, vo_ref):
        gs = p_ref[0]
        for j in range(1, N_DEV):
            gs = gs + p_ref[j]
        go_ref[...] = gs
        d_ref[...], mo_ref[...], vo_ref[...] = _adamw_math(gs, w_ref[...], m_ref[...], v_ref[...])

    full = pl.BlockSpec((R, W), lambda i: (0, 0))
    out = jax.ShapeDtypeStruct((R, W), F32)
    return pl.pallas_call(
        body, name="adamw_replicated", grid=(1,), out_shape=[out] * 4,
        in_specs=[pl.BlockSpec((N_DEV, R, W), lambda i: (0, 0, 0)), full, full, full], out_specs=[full] * 4,
        compiler_params=_params("arbitrary"),
    )(parts, w, m, v)


GROUPS = {
    "g256": (256, 512, (("w_in_e", 1024, 1024),)),
    "g1024": (1024, 304, (("w_out_e", 128, 128), ("w_out_o", 128, 128), ("w_down", 704, 704), ("w_ple_gate", 256, 256))),
    "g514": (514, 512, (("w_in_o", 1024, 1024),)),
    "g704": (704, 272, (("w_up", 2048, 2048), ("ffn_conv", 6, 128))),
    "g128": (128, 528, (("w_ple", 512, 512), ("mix_norm_o", 1, 16))),
    "g384": (384, 8, (("conv_qkv_o", 4, 8),)),
}
SHARDED = tuple(p[0] for g in GROUPS.values() for p in g[2])
COLUMN_SHARDED = ("w_in_e", "w_in_o", "w_up", "ffn_conv", "w_ple", "conv_qkv_o", "mix_norm_o")
PACK_W = 1024
REPL_LAYOUT = (
    ("mix_norm_e", (1, 1024), 8), ("pool_w", (1, 4, 128, 128), 64), ("pool_scale", (1, 512), 8),
    ("a_log_o", (1, 8), 8), ("dt_bias_o", (1, 8), 8), ("gdn_norm_o", (1, 128), 8),
    ("ffn_norm", (2, 1024), 8), ("ple_norm", (2, 1024), 8), ("final_norm", (1024,), 8),
)


def _pad_rows(a, rows):
    extra = rows - a.shape[-2]
    return a if extra == 0 else jnp.pad(a, [(0, 0)] * (a.ndim - 2) + [(0, extra), (0, 0)])


def _group_rows(pieces, gname):
    parts = [_pad_rows(pieces[name], padded) for name, _, padded in GROUPS[gname][2]]
    return parts[0] if len(parts) == 1 else jnp.concatenate(parts, axis=-2)


def _ungroup_rows(buf, gname):
    out, r0 = {}, 0
    for name, rows, padded in GROUPS[gname][2]:
        out[name] = buf[..., r0:r0 + rows, :]
        r0 += padded
    return out


def _shard_major(name, gfull, n_layers):
    per_layer = []
    for g in gfull:
        if name in COLUMN_SHARDED:
            k = g.shape[0]
            per_layer.append(jnp.moveaxis(g.reshape(k, N_DEV, g.shape[1] // N_DEV), 1, 0))
        else:
            per_layer.append(g.reshape(N_DEV, g.shape[0] // N_DEV, -1))
    return per_layer[0] if n_layers == 1 else jnp.concatenate(per_layer, axis=1)


def _natural(name, gathered, n_layers):
    rows = gathered.shape[1] // n_layers
    out = []
    for layer in range(n_layers):
        piece = gathered[:, layer * rows:(layer + 1) * rows]
        if name in COLUMN_SHARDED:
            out.append(jnp.moveaxis(piece, 0, 1).reshape(rows, N_DEV * piece.shape[2]))
        else:
            out.append(piece.reshape(N_DEV * rows, piece.shape[2]))
    return out


def _rows(a, rows):
    flat = a.reshape(-1)
    return jnp.pad(flat, (0, rows * PACK_W - flat.shape[0])).reshape(rows, PACK_W)


def _pack_repl(vals):
    return jnp.concatenate([_rows(vals[name].reshape(shape), rows) for name, shape, rows in REPL_LAYOUT], axis=0)


def _unpack_repl(buf):
    out, r0 = {}, 0
    for name, shape, rows in REPL_LAYOUT:
        n = 1
        for s in shape:
            n *= s
        out[name] = buf[r0:r0 + rows].reshape(-1)[:n].reshape(shape)
        r0 += rows
    return out


WEIGHTS = ("mix_norm_e", "w_in_e", "pool_w", "pool_scale", "w_out_e", "mix_norm_o", "w_in_o", "conv_qkv_o", "a_log_o",
           "dt_bias_o", "gdn_norm_o", "w_out_o", "ffn_norm", "w_up", "ffn_conv", "w_down", "ple_norm", "w_ple_gate",
           "w_ple", "final_norm")


def _ffn_forward(x, norm_g, w_up, conv_w, w_down, tag):
    hn = _rms_fwd(x, norm_g, name="rms_ffn" + tag)
    up = _mm(hn, w_up, tm=1024, tn=512, name="ffn_up" + tag)
    act = _ffn_act(up, conv_w)
    out = _mm(act, w_down, res=x, tm=1024, tn=512, tk=1408, name="ffn_down" + tag)
    return out, (x, hn, up, act)


def _ffn_backward(dx, saved, norm_g, w_up, conv_w, w_down, tag):
    x, hn, up, act = saved
    dact = _mm(dx, w_down, tb=True, tm=512, tn=1408, name="ffn_dact" + tag)
    d_w_down = _mm(act, dx, ta=True, tm=1408, tn=512, tk=512, name="ffn_dwdown" + tag)
    dgate, dval, dcg, dcv = _ffn_act_bwd(up, conv_w, dact)
    dhn = _mm(dgate, w_up[:, :FFN_DIM], tb=True, tm=1024, tn=512, tk=1408, name="ffn_dhn_g" + tag)
    dhn = _mm(dval, w_up[:, FFN_DIM:], tb=True, res=dhn, tm=1024, tn=512, tk=1408, name="ffn_dhn_v" + tag)
    d_w_up = jnp.concatenate([_mm(hn, dgate, ta=True, tm=1024, tn=1408, tk=512, name="ffn_dwup_g" + tag),
                              _mm(hn, dval, ta=True, tm=1024, tn=1408, tk=512, name="ffn_dwup_v" + tag)], axis=1)
    dx, d_norm = _rms_bwd(x, dhn, norm_g, dx, name="rms_ffn_bwd" + tag)
    return dx, d_norm, d_w_up, jnp.concatenate([dcg, dcv], axis=1), d_w_down


def _ple_forward(x, norm_g, w_gate, p, w_ple, tag):
    hn = _rms_fwd(x, norm_g, name="rms_ple" + tag)
    out, gl, pe = _ple_fwd(hn, w_gate, p, w_ple, x, name="ple_fwd" + tag)
    return out, (x, hn, gl, pe)


def _ple_backward(dx, saved, norm_g, w_gate, p, tag):
    x, hn, gl, pe = saved
    dpe, dgl = _ple_bwd(dx, gl, pe, name="ple_bwd" + tag)
    d_w_ple = _mm(p, dpe, ta=True, tm=256, tn=512, tk=512, name="ple_dwple" + tag)
    d_w_gate = _mm(hn, dgl, ta=True, tm=1024, tn=512, tk=512, name="ple_dwgate" + tag)
    dhn = _mm(dgl, w_gate, tb=True, tm=1024, tn=512, name="ple_dhn" + tag)
    dx, d_norm = _rms_bwd(x, dhn, norm_g, dx, name="rms_ple_bwd" + tag)
    return dx, d_norm, d_w_gate, d_w_ple


def kernel(x, p, mix_norm_e, w_in_e, pool_w, pool_scale, w_out_e, mix_norm_o, w_in_o, conv_qkv_o, a_log_o, dt_bias_o, gdn_norm_o, w_out_o, ffn_norm, w_up, ffn_conv, w_down, ple_norm, w_ple_gate, w_ple, final_norm, loss_target, m_mix_norm_e, m_w_in_e, m_pool_w, m_pool_scale, m_w_out_e, m_mix_norm_o, m_w_in_o, m_conv_qkv_o, m_a_log_o, m_dt_bias_o, m_gdn_norm_o, m_w_out_o, m_ffn_norm, m_w_up, m_ffn_conv, m_w_down, m_ple_norm, m_w_ple_gate, m_w_ple, m_final_norm, v_mix_norm_e, v_w_in_e, v_pool_w, v_pool_scale, v_w_out_e, v_mix_norm_o, v_w_in_o, v_conv_qkv_o, v_a_log_o, v_dt_bias_o, v_gdn_norm_o, v_w_out_o, v_ffn_norm, v_w_up, v_ffn_conv, v_w_down, v_ple_norm, v_w_ple_gate, v_w_ple, v_final_norm):
    given = dict(locals())
    place = jnp.stack(_place()).astype(jnp.int32)
    x0, tgt = x[0], loss_target[0]

    def pieces(prefix):
        return {name: given[prefix + name].reshape(rows, GROUPS[g][0])
                for g in GROUPS for name, rows, _ in GROUPS[g][2]}

    loc = pieces("")
    small = ("ffn_conv", "mix_norm_o", "conv_qkv_o")
    send = [_group_rows({k: v.astype(BF16) for k, v in loc.items()}, g) for g in ("g256", "g1024", "g514")]
    send += [loc["w_up"].astype(BF16), loc["w_ple"].astype(BF16)] + [_pad_rows(loc[k], 8) for k in small]
    got = _all_gather(send, name="ag_weights")
    gathered = {"w_in_e": got[0], **_ungroup_rows(got[1], "g1024"), "w_in_o": got[2], "w_up": got[3], "w_ple": got[4]}
    layers = {name: given[name].shape[0] if given[name].ndim == 3 else 1 for name in gathered}
    full = {(name, i): w for name in gathered for i, w in enumerate(_natural(name, gathered[name], layers[name]))}
    for i in range(2):
        full[("ffn_conv", i)] = _natural("ffn_conv", got[5][:, 3 * i:3 * i + 3], 1)[0]
    w_in_o_full = full[("w_in_o", 0)]
    w_in_o_main = w_in_o_full[:, :4096]
    w_in_o_ba = jnp.pad(w_in_o_full[:, 4096:], ((0, 0), (0, 112)))
    mix_norm_o_full = got[6][:, 0].reshape(1, D_MODEL)
    conv_qkv = _natural("conv_qkv_o", got[7][:, :4], 1)[0]
    alog_row = jnp.pad(a_log_o, ((0, 0), (8, 112)))
    dt_row = jnp.pad(dt_bias_o, ((0, 0), (8, 112)))
    lw = lambda name, i: full[(name, i)]

    h_e = _rms_fwd(x0, mix_norm_e, name="rms_mix_e")
    proj_e = _mm(h_e, lw("w_in_e", 0), tm=1024, tn=512, name="in_e")
    pool_o = _pool_fwd(proj_e, pool_w[0], pool_scale)
    att_o, lsum = _sb_fwd(proj_e)
    mix_e = jnp.concatenate([pool_o, att_o.astype(BF16)], axis=1)
    x1 = _mm(mix_e, lw("w_out_e", 0), res=x0, tm=1024, tn=512, name="out_e")
    x2, ffn0 = _ffn_forward(x1, ffn_norm[0:1], lw("w_up", 0), lw("ffn_conv", 0), lw("w_down", 0), "0")
    x3, ple0 = _ple_forward(x2, ple_norm[0:1], lw("w_ple_gate", 0), p[0, 0], lw("w_ple", 0), "0")

    h_o = _rms_fwd(x3, mix_norm_o_full, name="rms_mix_o")
    proj_o = _mm(h_o, w_in_o_main, tm=1024, tn=512, name="in_o")
    ba = _mm(h_o, w_in_o_ba, tm=1024, tn=128, name="in_o_ba")
    qkv = _gdn_pre(proj_o, conv_qkv)
    beta, g = _gdn_gate(ba, alog_row, dt_row)
    u, w_c, qk, qd, kd, gl = _gdn_local(qkv, g, beta)
    o, states, vnew = _gdn_scan(u, w_c, qk, qd, kd, gl)
    y_o = _gdn_post(o, proj_o, gdn_norm_o)
    x4 = _mm(y_o, lw("w_out_o", 0), res=x3, tm=1024, tn=512, name="out_o")
    x5, ffn1 = _ffn_forward(x4, ffn_norm[1:2], lw("w_up", 1), lw("ffn_conv", 1), lw("w_down", 1), "1")
    x6, ple1 = _ple_forward(x5, ple_norm[1:2], lw("w_ple_gate", 1), p[1, 0], lw("w_ple", 1), "1")
    loss_row, dx, d_final = _final_loss(x6, final_norm.reshape(1, D_MODEL), tgt)

    grads, rgrads = {}, {}
    dx, d_ple1, grads[("w_ple_gate", 1)], grads[("w_ple", 1)] = _ple_backward(dx, ple1, ple_norm[1:2], lw("w_ple_gate", 1), p[1, 0], "1")
    dx, d_ffn1, grads[("w_up", 1)], grads[("ffn_conv", 1)], grads[("w_down", 1)] = _ffn_backward(
        dx, ffn1, ffn_norm[1:2], lw("w_up", 1), lw("ffn_conv", 1), lw("w_down", 1), "1")
    grads[("w_out_o", 0)] = _mm(y_o, dx, ta=True, tm=1024, tn=512, tk=512, name="dw_out_o")
    dy_o = _mm(dx, lw("w_out_o", 0), tb=True, tm=1024, tn=512, name="dy_o")
    do, dz, rgrads["gdn_norm_o"] = _gdn_post_bwd(o, proj_o, gdn_norm_o, dy_o)
    du, dw_c, dqk, dqd, dkd, dgl = _gdn_scan_bwd(do, states, vnew, w_c, qk, qd, kd, gl)
    dq, dk, dv, dg, dbeta = _gdn_local_bwd(qkv, g, beta, du, dw_c, dqk, dqd, dkd, dgl)
    dqkv, grads[("conv_qkv_o", 0)] = _gdn_pre_bwd(proj_o, conv_qkv, jnp.concatenate([dq, dk, dv], axis=0))
    dba, d_alog, d_dt = _gdn_gate_bwd(ba, alog_row, dt_row, dbeta, dg)
    rgrads["a_log_o"], rgrads["dt_bias_o"] = d_alog[:, 8:16], d_dt[:, 8:16]
    dproj_o = jnp.concatenate([dqkv, dz], axis=1)
    dh = _mm(dproj_o, w_in_o_main, tb=True, tm=1024, tn=512, tk=1024, name="dh_o")
    dh = _mm(dba, w_in_o_ba, tb=True, res=dh, tm=1024, tn=512, name="dh_o_ba")
    grads[("w_in_o", 0)] = jnp.concatenate(
        [_mm(h_o, dproj_o, ta=True, tm=1024, tn=512, tk=512, name="dw_in_o"),
         _mm(h_o, dba, ta=True, tm=1024, tn=128, tk=512, name="dw_in_o_ba")[:, :16]], axis=1)
    dx, d_mix_o = _rms_bwd(x3, dh, mix_norm_o_full, dx, name="rms_mix_o_bwd")
    grads[("mix_norm_o", 0)] = d_mix_o

    dx, d_ple0, grads[("w_ple_gate", 0)], grads[("w_ple", 0)] = _ple_backward(dx, ple0, ple_norm[0:1], lw("w_ple_gate", 0), p[0, 0], "0")
    dx, d_ffn0, grads[("w_up", 0)], grads[("ffn_conv", 0)], grads[("w_down", 0)] = _ffn_backward(
        dx, ffn0, ffn_norm[0:1], lw("w_up", 0), lw("ffn_conv", 0), lw("w_down", 0), "0")
    grads[("w_out_e", 0)] = _mm(mix_e, dx, ta=True, tm=1024, tn=512, tk=512, name="dw_out_e")
    dmix = _mm(dx, lw("w_out_e", 0), tb=True, tm=1024, tn=512, name="dmix_e")
    du_e, d_pool_w, rgrads["pool_scale"] = _pool_bwd(proj_e, dmix, pool_w[0], pool_scale)
    rgrads["pool_w"] = d_pool_w[None]
    dq_e, dk_e, dv_e = _sb_bwd(proj_e, lsum, dmix)
    dproj_e = jnp.concatenate([du_e, dq_e.astype(BF16), dk_e.astype(BF16), dv_e.astype(BF16)], axis=1)
    dh = _mm(dproj_e, lw("w_in_e", 0), tb=True, tm=1024, tn=512, tk=1024, name="dh_e")
    grads[("w_in_e", 0)] = _mm(h_e, dproj_e, ta=True, tm=1024, tn=512, tk=512, name="dw_in_e")
    dx, rgrads["mix_norm_e"] = _rms_bwd(x0, dh, mix_norm_e, dx, name="rms_mix_e_bwd")
    rgrads["ffn_norm"] = jnp.concatenate([d_ffn0, d_ffn1], axis=0)
    rgrads["ple_norm"] = jnp.concatenate([d_ple0, d_ple1], axis=0)
    rgrads["final_norm"] = d_final.reshape(D_MODEL)

    gnames = tuple(GROUPS)
    smaj = {name: _shard_major(name, [grads[(name, i)] for i in range(2) if (name, i) in grads],
                               sum((name, i) in grads for i in range(2))) for name in SHARDED}
    gbuf = [_group_rows(smaj, g) for g in gnames]
    gbuf = [b.reshape((4, 2) + b.shape[1:]) for b in gbuf]
    got1 = _exchange_sibling(gbuf)
    part = [_chip_partial(place, b, r, tr=GROUPS[g][1], name="rs_chip_partial_" + g) for g, b, r in zip(gnames, gbuf, got1)]
    got2 = _exchange_chips(part)
    wloc, mloc, vloc = pieces(""), pieces("m_"), pieces("v_")
    sh_out = [{}, {}, {}, {}]
    for g, b, r1, r2 in zip(gnames, gbuf, got1, got2):
        res = _adamw_shard(place, b, r1, r2, _group_rows(wloc, g), _group_rows(mloc, g), _group_rows(vloc, g),
                           tr=GROUPS[g][1], name="adamw_" + g)
        for kind in range(4):
            sh_out[kind].update(_ungroup_rows(res[kind], g))

    (rparts,) = _all_gather([_pack_repl(rgrads)], name="ag_repl_grads")
    rp_out = _adamw_replicated(rparts, _pack_repl({n: given[n] for n, _, _ in REPL_LAYOUT}),
                               _pack_repl({n: given["m_" + n] for n, _, _ in REPL_LAYOUT}),
                               _pack_repl({n: given["v_" + n] for n, _, _ in REPL_LAYOUT}))
    rp_out = [_unpack_repl(b) for b in rp_out]

    def leaf(kind, name):
        if name in SHARDED:
            return sh_out[kind][name].reshape(given[name].shape)
        return rp_out[kind][name]

    loss = lax.psum(loss_row[0, 0], ("x", "y", "c"))
    outs = [loss, dx[None]]
    for kind in range(4):
        outs += [leaf(kind, n) for n in WEIGHTS]
    return tuple(outs)
```

```python
import functools

import jax
import jax.numpy as jnp
from jax import lax
from jax.experimental import pallas as pl
from jax.experimental.pallas import tpu as pltpu

F32 = jnp.float32
BF16 = jnp.bfloat16

D_MODEL = 1024
PLE_DIM = 256
POOL_WINDOWS = (2, 4, 8, 16)
POOL_WIDTH = 512
SB_HEAD_DIM = 64
SB_BLOCK = 1024
SB_KBLOCK = 256
GDN_HEADS = 8
GDN_HEAD_DIM = 128
GDN_CONV = 4
GDN_CHUNK = 64
FFN_DIM = 2816
FFN_CONV = 3
EPS = 1e-6
ADAM_LR, ADAM_B1, ADAM_B2, ADAM_EPS, ADAM_WD, ADAM_STEP = 0.001, 0.9, 0.999, 1e-08, 0.01, 10
N_DEV = 8
MESH = pl.DeviceIdType.MESH
VMEM_LIMIT = 56 * 1024 * 1024

NN = (((1,), (0,)), ((), ()))
NT = (((1,), (1,)), ((), ()))
TN = (((0,), (0,)), ((), ()))


def _params(*sem):
    return pltpu.CompilerParams(dimension_semantics=sem if sem else None, vmem_limit_bytes=VMEM_LIMIT)


def _dot(a, b, dims):
    return lax.dot_general(a.astype(BF16), b.astype(BF16), dims, preferred_element_type=F32)


def _iota(shape, axis):
    return lax.broadcasted_iota(jnp.int32, shape, axis)


def _mm(a, b, *, ta=False, tb=False, res=None, out_dtype=F32, tm=512, tn=512, tk=None, name):
    M, K = (a.shape[1], a.shape[0]) if ta else a.shape
    N = b.shape[0] if tb else b.shape[1]
    tk = K if tk is None else min(tk, K)
    tm, tn = min(tm, M), min(tn, N)
    assert M % tm == 0 and N % tn == 0 and K % tk == 0, (name, M, N, K, tm, tn, tk)
    nk = K // tk
    dims = (((0 if ta else 1,), (1 if tb else 0,)), ((), ()))

    def body(*refs):
        if res is None:
            a_ref, b_ref, o_ref, *scr = refs
            r_ref = None
        else:
            a_ref, b_ref, r_ref, o_ref, *scr = refs
        p = _dot(a_ref[...], b_ref[...], dims)

        def fin(acc):
            if r_ref is not None:
                acc = acc + r_ref[...]
            o_ref[...] = acc.astype(out_dtype)

        if nk == 1:
            fin(p)
        else:
            acc_ref = scr[0]
            k = pl.program_id(2)

            @pl.when(k == 0)
            def _():
                acc_ref[...] = p

            @pl.when(k > 0)
            def _():
                acc_ref[...] += p

            @pl.when(k == nk - 1)
            def _():
                fin(acc_ref[...])

    a_spec = pl.BlockSpec((tk, tm), lambda i, j, k: (k, i)) if ta else pl.BlockSpec((tm, tk), lambda i, j, k: (i, k))
    b_spec = pl.BlockSpec((tn, tk), lambda i, j, k: (j, k)) if tb else pl.BlockSpec((tk, tn), lambda i, j, k: (k, j))
    o_spec = pl.BlockSpec((tm, tn), lambda i, j, k: (i, j))
    in_specs = [a_spec, b_spec] + ([o_spec] if res is not None else [])
    args = (a, b) + ((res,) if res is not None else ())
    return pl.pallas_call(
        body, name=name, grid=(M // tm, N // tn, nk), in_specs=in_specs, out_specs=o_spec,
        out_shape=jax.ShapeDtypeStruct((M, N), out_dtype),
        scratch_shapes=[pltpu.VMEM((tm, tn), F32)] if nk > 1 else [],
        compiler_params=_params("parallel", "parallel", "arbitrary"),
    )(*args)


def _rms_fwd(x, gain, *, name, tr=512):
    T, Dm = x.shape

    def body(x_ref, g_ref, o_ref):
        xv = x_ref[...]
        r = lax.rsqrt(jnp.mean(xv * xv, axis=-1, keepdims=True) + EPS)
        o_ref[...] = (xv * r * g_ref[...]).astype(BF16)

    return pl.pallas_call(
        body, name=name, grid=(T // tr,),
        in_specs=[pl.BlockSpec((tr, Dm), lambda i: (i, 0)), pl.BlockSpec((1, Dm), lambda i: (0, 0))],
        out_specs=pl.BlockSpec((tr, Dm), lambda i: (i, 0)),
        out_shape=jax.ShapeDtypeStruct((T, Dm), BF16), compiler_params=_params("parallel"),
    )(x, gain)


def _rms_bwd(x, dy, gain, dres, *, name, tr=512):
    T, Dm = x.shape

    def body(x_ref, dy_ref, g_ref, dres_ref, dx_ref, dg_ref):
        i = pl.program_id(0)
        xv = x_ref[...]
        dy_v = dy_ref[...].astype(F32)
        r = lax.rsqrt(jnp.mean(xv * xv, axis=-1, keepdims=True) + EPS)
        xn = xv * r
        dgp = jnp.sum(dy_v * xn, axis=0, keepdims=True)
        dyg = dy_v * g_ref[...]
        dx = r * (dyg - xn * jnp.mean(dyg * xn, axis=-1, keepdims=True))
        dx_ref[...] = dres_ref[...] + dx

        @pl.when(i == 0)
        def _():
            dg_ref[...] = dgp

        @pl.when(i > 0)
        def _():
            dg_ref[...] += dgp

    row = pl.BlockSpec((tr, Dm), lambda i: (i, 0))
    vec = pl.BlockSpec((1, Dm), lambda i: (0, 0))
    return pl.pallas_call(
        body, name=name, grid=(T // tr,), in_specs=[row, row, vec, row], out_specs=[row, vec],
        out_shape=[jax.ShapeDtypeStruct((T, Dm), F32), jax.ShapeDtypeStruct((1, Dm), F32)],
        compiler_params=_params("arbitrary"),
    )(x, dy, gain, dres)


def _final_loss(x, gain, target, *, tr=512):
    T, Dm = x.shape

    def body(x_ref, g_ref, t_ref, loss_ref, dx_ref, dg_ref):
        i = pl.program_id(0)
        xv = x_ref[...]
        g = g_ref[...]
        r = lax.rsqrt(jnp.mean(xv * xv, axis=-1, keepdims=True) + EPS)
        xn = xv * r
        err = xn * g - t_ref[...]
        lp = jnp.zeros((1, 128), F32) + 0.5 * jnp.sum(jnp.mean(err * err, axis=-1, keepdims=True))
        dy_v = err * (1.0 / Dm)
        dgp = jnp.sum(dy_v * xn, axis=0, keepdims=True)
        dyg = dy_v * g
        dx_ref[...] = r * (dyg - xn * jnp.mean(dyg * xn, axis=-1, keepdims=True))

        @pl.when(i == 0)
        def _():
            dg_ref[...] = dgp
            loss_ref[...] = lp

        @pl.when(i > 0)
        def _():
            dg_ref[...] += dgp
            loss_ref[...] += lp

    row = pl.BlockSpec((tr, Dm), lambda i: (i, 0))
    vec = pl.BlockSpec((1, Dm), lambda i: (0, 0))
    return pl.pallas_call(
        body, name="final_loss", grid=(T // tr,), in_specs=[row, vec, row],
        out_specs=[pl.BlockSpec((1, 128), lambda i: (0, 0)), row, vec],
        out_shape=[jax.ShapeDtypeStruct((1, 128), F32), jax.ShapeDtypeStruct((T, Dm), F32),
                   jax.ShapeDtypeStruct((1, Dm), F32)],
        compiler_params=_params("arbitrary"),
    )(x, gain, target)


def _prev_spec(tr, cb, pad, col):
    return pl.BlockSpec((pad, cb), lambda *g: (jnp.maximum(g[0] * (tr // pad) - 1, 0), col(*g)))


def _next_spec(tr, cb, pad, col, T):
    return pl.BlockSpec((pad, cb), lambda *g: (jnp.minimum((g[0] + 1) * (tr // pad), T // pad - 1), col(*g)))


def _conv_rows(x_ext, w_ref, K, pad, cs=slice(None)):
    y = w_ref[K - 1:K, cs] * x_ext
    for i in range(K - 1):
        y = y + w_ref[i:i + 1, cs] * pltpu.roll(x_ext, K - 1 - i, 0)
    return y[pad:]


def _pool_y(u_ext, g, i, tr):
    s = u_ext
    for sh in (1, 2, 4, 8)[:g + 1]:
        s = s + pltpu.roll(s, sh, 0)
    t = i * tr + _iota((tr, 128), 0)
    cnt = jnp.minimum(t + 1, POOL_WINDOWS[g]).astype(F32)
    return s[16:] / cnt - u_ext[16:]


def _pool_fwd(proj, pool_w, pool_scale, *, tr=512):
    T = proj.shape[0]

    def body(u_ref, uh_ref, w_ref, s_ref, o_ref):
        i = pl.program_id(0)
        uh = jnp.where(i > 0, uh_ref[...], 0.0)
        for g in range(4):
            cs = slice(128 * g, 128 * (g + 1))
            y = _pool_y(jnp.concatenate([uh[:, cs], u_ref[:, cs]], axis=0), g, i, tr)
            o_ref[:, cs] = (_dot(y, w_ref[g], NN) * s_ref[:, cs]).astype(BF16)

    return pl.pallas_call(
        body, name="pool_fwd", grid=(T // tr,),
        in_specs=[pl.BlockSpec((tr, 512), lambda i: (i, 0)), _prev_spec(tr, 512, 16, lambda i: 0),
                  pl.BlockSpec((4, 128, 128), lambda i: (0, 0, 0)), pl.BlockSpec((1, 512), lambda i: (0, 0))],
        out_specs=pl.BlockSpec((tr, 512), lambda i: (i, 0)),
        out_shape=jax.ShapeDtypeStruct((T, 512), BF16), compiler_params=_params("parallel"),
    )(proj, proj, pool_w, pool_scale)


def _pool_bwd(proj, dout, pool_w, pool_scale, *, tr=512):
    T = proj.shape[0]
    nb = T // tr

    def body(u_ref, uh_ref, d_ref, dn_ref, w_ref, s_ref, du_ref, dw_ref, ds_ref):
        i = pl.program_id(0)
        uh = jnp.where(i > 0, uh_ref[...], 0.0)
        dn = jnp.where(i < nb - 1, dn_ref[...], 0.0)
        t_ext = i * tr + _iota((tr + 16, 128), 0)
        for g in range(4):
            cs = slice(128 * g, 128 * (g + 1))
            sc = s_ref[:, cs]
            wg = w_ref[g]
            y = _pool_y(jnp.concatenate([uh[:, cs], u_ref[:, cs]], axis=0), g, i, tr)
            dg = d_ref[:, cs]
            dsp = jnp.sum(dg * _dot(y, wg, NN), axis=0, keepdims=True)
            dyw = dg * sc
            dwp = _dot(y, dyw, TN)
            dy_ext = _dot(jnp.concatenate([dyw, dn[:, cs] * sc], axis=0), wg, NT)
            cnt = jnp.minimum(t_ext + 1, POOL_WINDOWS[g]).astype(F32)
            s = dy_ext / cnt
            for sh in (1, 2, 4, 8)[:g + 1]:
                s = s + pltpu.roll(s, tr + 16 - sh, 0)
            du_ref[:, cs] = (s[:tr] - dy_ext[:tr]).astype(BF16)

            @pl.when(i == 0)
            def _():
                dw_ref[g] = dwp
                ds_ref[:, cs] = dsp

            @pl.when(i > 0)
            def _():
                dw_ref[g] += dwp
                ds_ref[:, cs] += dsp

    row = pl.BlockSpec((tr, 512), lambda i: (i, 0))
    return pl.pallas_call(
        body, name="pool_bwd", grid=(nb,),
        in_specs=[row, _prev_spec(tr, 512, 16, lambda i: 0), row, _next_spec(tr, 512, 16, lambda i: 0, T),
                  pl.BlockSpec((4, 128, 128), lambda i: (0, 0, 0)), pl.BlockSpec((1, 512), lambda i: (0, 0))],
        out_specs=[row, pl.BlockSpec((4, 128, 128), lambda i: (0, 0, 0)), pl.BlockSpec((1, 512), lambda i: (0, 0))],
        out_shape=[jax.ShapeDtypeStruct((T, 512), BF16), jax.ShapeDtypeStruct((4, 128, 128), F32),
                   jax.ShapeDtypeStruct((1, 512), F32)],
        compiler_params=_params("arbitrary"),
    )(proj, proj, dout, dout, pool_w, pool_scale)


def _split_dot(x, tri):
    hi = x.astype(BF16)
    lo = (x - hi.astype(F32)).astype(BF16)
    return (lax.dot_general(hi, tri, NN, preferred_element_type=F32)
            + lax.dot_general(lo, tri, NN, preferred_element_type=F32))


def _log1m(z):
    return -(jnp.maximum(z, 0.0) + jnp.log(1.0 + jnp.exp(-jnp.abs(z))))


def _sb_fwd(proj, gather=()):
    T = proj.shape[0]
    B, BK = min(SB_BLOCK, T), SB_KBLOCK
    R = B // BK
    nq, n = T // B, len(gather)
    scale = SB_HEAD_DIM ** -0.5

    def body(q_ref, k_ref, v_ref, *rest):
        o_ref, ls_ref = rest[n:n + 2]
        hp, i = pl.program_id(0), pl.program_id(1)
        if n:
            stages = _gather_stages(rest[:n], rest[n + 2:2 * n + 2], *rest[2 * n + 2:])
            for stage, at in zip(stages, ((0, 0), (2, 0), (3, nq - 1))):
                pl.when((hp == at[0]) & (i == at[1]))(stage)
        lane = _iota((1, 128), 1)
        tri_gt = (_iota((BK, BK), 0) > _iota((BK, BK), 1)).astype(BF16)
        row, col = _iota((B, BK), 0), _iota((B, BK), 1)
        qv = q_ref[...] * scale
        hms = [(lane >= 64 * h) & (lane < 64 * (h + 1)) for h in range(2)]
        qhs = [jnp.where(hm, qv, 0.0).astype(BF16) for hm in hms]

        def tile(j, carry, d):
            rows = pl.ds(pl.multiple_of(j * BK, BK), BK)
            kj = k_ref[rows, :].astype(BF16)
            vj = v_ref[rows, :].astype(BF16)
            valid = None if d is None else (col + BK * d < row)
            out = []
            for h in range(2):
                c, acc = carry[h]
                z = lax.dot_general(qhs[h], kj, NT, preferred_element_type=F32)
                lg = _log1m(z)
                if d is not None:
                    lg = jnp.where(valid, lg, 0.0)
                a = jnp.exp(z + lg + _split_dot(lg, tri_gt) + c)
                if d is not None:
                    a = jnp.where(valid, a, 0.0)
                acc = acc + lax.dot_general(a.astype(BF16), vj, NN, preferred_element_type=F32)
                out.append((c + jnp.sum(lg, axis=1, keepdims=True), acc))
            return tuple(out)

        zero = (jnp.zeros((B, 1), F32), jnp.zeros((B, 128), F32))
        carry = (zero, zero)
        for d in reversed(range(R)):
            carry = tile(i * R + d, carry, d)
        carry = lax.fori_loop(0, i * R, lambda s, cr: tile(i * R - 1 - s, cr, None), carry)
        o_ref[...] = jnp.where(hms[0], carry[0][1], carry[1][1])
        ls_ref[...] = jnp.where(hms[0], carry[0][0], carry[1][0])

    blk = pl.BlockSpec((B, 128), lambda hp, i: (i, hp))
    out = pl.pallas_call(
        body, name="sb_fwd", grid=(4, nq),
        in_specs=[pl.BlockSpec((B, 128), lambda hp, i: (i, 4 + hp)),
                  pl.BlockSpec((T, 128), lambda hp, i: (0, 8 + hp)),
                  pl.BlockSpec((T, 128), lambda hp, i: (0, 12 + hp))] + [HBM_SPEC] * n,
        out_specs=[blk, blk] + [HBM_SPEC] * n,
        out_shape=[jax.ShapeDtypeStruct((T, 512), F32)] * 2 + _gather_shapes(gather),
        scratch_shapes=_gather_sems(n) if n else [],
        compiler_params=_params("arbitrary", "arbitrary"),
    )(proj, proj, proj, *gather)
    return out[0], out[1], list(out[2:])


def _sb_bwd(proj, lsum, dout, exchange=()):
    T = proj.shape[0]
    B, BK = min(SB_BLOCK, T), SB_KBLOCK
    R = B // BK
    nq, n = T // B, len(exchange)
    scale = SB_HEAD_DIM ** -0.5

    def body(q_ref, k_ref, v_ref, do_ref, ls_ref, *rest):
        dq_ref, dk_ref, dv_ref = rest[n:n + 3]
        hp, i = pl.program_id(0), pl.program_id(1)
        if n:
            stages = _chips_stages(rest[:n], rest[n + 3:2 * n + 3], *rest[2 * n + 3:])
            for stage, at in zip(stages, ((0, 0), (3, nq - 1))):
                pl.when((hp == at[0]) & (i == at[1]))(stage)

        @pl.when(i == 0)
        def _():
            dk_ref[...] = jnp.zeros_like(dk_ref)
            dv_ref[...] = jnp.zeros_like(dv_ref)

        lane = _iota((1, 128), 1)
        tri_le = (_iota((BK, BK), 0) <= _iota((BK, BK), 1)).astype(BF16)
        tri_lt = (_iota((BK, BK), 0) < _iota((BK, BK), 1)).astype(BF16)
        row, col = _iota((B, BK), 0), _iota((B, BK), 1)
        qv = q_ref[...] * scale
        dov = do_ref[...]
        hms = [(lane >= 64 * h) & (lane < 64 * (h + 1)) for h in range(2)]
        qhs = [jnp.where(hm, qv, 0.0).astype(BF16) for hm in hms]
        dos = [jnp.where(hm, dov, 0.0).astype(BF16) for hm in hms]
        ltots = [ls_ref[:, 64 * h:64 * h + 1] for h in range(2)]

        def tile(j, carry, d):
            rows = pl.ds(pl.multiple_of(j * BK, BK), BK)
            kj = k_ref[rows, :].astype(BF16)
            vj = v_ref[rows, :].astype(BF16)
            diag = d is not None
            valid = (col + BK * d < row) if diag else None
            out = []
            dkj = jnp.zeros((BK, 128), F32)
            dvj = jnp.zeros((BK, 128), F32)
            for h in range(2):
                lbef, ebef, dqa = carry[h]
                z = lax.dot_general(qhs[h], kj, NT, preferred_element_type=F32)
                lg = _log1m(z)
                if diag:
                    lg = jnp.where(valid, lg, 0.0)
                a = jnp.exp(z + lg + (ltots[h] - lbef - _split_dot(lg, tri_le)))
                if diag:
                    a = jnp.where(valid, a, 0.0)
                e = a * lax.dot_general(dos[h], vj, NT, preferred_element_type=F32)
                dz = e * jnp.exp(lg) - jnp.exp(z + lg) * (ebef + _split_dot(e, tri_lt))
                if diag:
                    dz = jnp.where(valid, dz, 0.0)
                dzb = dz.astype(BF16)
                dqa = dqa + lax.dot_general(dzb, kj, NN, preferred_element_type=F32)
                dkj = dkj + lax.dot_general(dzb, qhs[h], TN, preferred_element_type=F32)
                dvj = dvj + lax.dot_general(a.astype(BF16), dos[h], TN, preferred_element_type=F32)
                out.append((lbef + jnp.sum(lg, axis=1, keepdims=True), ebef + jnp.sum(e, axis=1, keepdims=True), dqa))
            dk_ref[rows, :] += dkj
            dv_ref[rows, :] += dvj
            return tuple(out)

        zero = (jnp.zeros((B, 1), F32), jnp.zeros((B, 1), F32), jnp.zeros((B, 128), F32))
        carry = lax.fori_loop(0, i * R, lambda j, cr: tile(j, cr, None), (zero, zero))
        for d in range(R):
            carry = tile(i * R + d, carry, d)
        dq_ref[...] = jnp.where(hms[0], carry[0][2], carry[1][2]) * scale

    full = pl.BlockSpec((T, 128), lambda hp, i: (0, hp))
    blk = pl.BlockSpec((B, 128), lambda hp, i: (i, hp))
    out = pl.pallas_call(
        body, name="sb_bwd", grid=(4, nq),
        in_specs=[pl.BlockSpec((B, 128), lambda hp, i: (i, 4 + hp)),
                  pl.BlockSpec((T, 128), lambda hp, i: (0, 8 + hp)),
                  pl.BlockSpec((T, 128), lambda hp, i: (0, 12 + hp)),
                  pl.BlockSpec((B, 128), lambda hp, i: (i, 4 + hp)), blk] + [HBM_SPEC] * n,
        out_specs=[blk, full, full] + [HBM_SPEC] * n,
        out_shape=[jax.ShapeDtypeStruct((T, 512), F32)] * 3 + _chips_shapes(exchange),
        scratch_shapes=_chips_sems(n) if n else [],
        compiler_params=_params("arbitrary", "arbitrary"),
    )(proj, proj, proj, dout, lsum, *exchange)
    return out[0], out[1], out[2], list(out[3:])


def _sigmoid(x):
    return 1.0 / (1.0 + jnp.exp(-x))


def _silu_mul(cg, cv):
    return cg * _sigmoid(cg) * cv


def _ffn_act(up, conv_w, *, tr=512, cb=256):
    T, F2 = up.shape
    nc = F2 // 2 // cb
    K = FFN_CONV

    def body(g_ref, gh_ref, v_ref, vh_ref, wg_ref, wv_ref, o_ref):
        i = pl.program_id(0)
        gh = jnp.where(i > 0, gh_ref[...], 0.0)
        vh = jnp.where(i > 0, vh_ref[...], 0.0)
        cg = _conv_rows(jnp.concatenate([gh, g_ref[...]], axis=0), wg_ref, K, 8)
        cv = _conv_rows(jnp.concatenate([vh, v_ref[...]], axis=0), wv_ref, K, 8)
        o_ref[...] = _silu_mul(cg, cv).astype(BF16)

    return pl.pallas_call(
        body, name="ffn_act", grid=(T // tr, nc),
        in_specs=[pl.BlockSpec((tr, cb), lambda i, j: (i, j)), _prev_spec(tr, cb, 8, lambda i, j: j),
                  pl.BlockSpec((tr, cb), lambda i, j: (i, nc + j)), _prev_spec(tr, cb, 8, lambda i, j: nc + j),
                  pl.BlockSpec((K, cb), lambda i, j: (0, j)), pl.BlockSpec((K, cb), lambda i, j: (0, nc + j))],
        out_specs=pl.BlockSpec((tr, cb), lambda i, j: (i, j)),
        out_shape=jax.ShapeDtypeStruct((T, F2 // 2), BF16), compiler_params=_params("parallel", "parallel"),
    )(up, up, up, up, conv_w, conv_w)


def _conv_bwd_rows(dc_ext, x_ext, w_ref, K, tr, cs=slice(None)):
    n = tr + 8
    dx = w_ref[K - 1:K, cs] * dc_ext
    for i in range(K - 1):
        dx = dx + w_ref[i:i + 1, cs] * pltpu.roll(dc_ext, n - (K - 1 - i), 0)
    dc = dc_ext[:tr]
    dws = [jnp.sum(dc * pltpu.roll(x_ext, K - 1 - i, 0)[8:8 + tr], axis=0, keepdims=True) for i in range(K)]
    return dx[:tr], dws


def _acc_rows(ref, rows, first, cs=slice(None)):
    for i, r in enumerate(rows):
        @pl.when(first)
        def _():
            ref[i:i + 1, cs] = r

        @pl.when(jnp.logical_not(first))
        def _():
            ref[i:i + 1, cs] += r


def _ffn_act_bwd(up, conv_w, dact, *, tr=512, cb=256):
    T, F2 = up.shape
    F = F2 // 2
    nc, nb = F // cb, T // tr
    K = FFN_CONV

    def body(g_ref, gp_ref, gn_ref, v_ref, vp_ref, vn_ref, d_ref, dn_ref, wg_ref, wv_ref,
             dg_ref, dv_ref, dwg_ref, dwv_ref):
        i = pl.program_id(1)
        first, last = i == 0, i == nb - 1
        g_ext = jnp.concatenate([jnp.where(first, 0.0, gp_ref[...]), g_ref[...], jnp.where(last, 0.0, gn_ref[...])], axis=0)
        v_ext = jnp.concatenate([jnp.where(first, 0.0, vp_ref[...]), v_ref[...], jnp.where(last, 0.0, vn_ref[...])], axis=0)
        d_ext = jnp.concatenate([d_ref[...], jnp.where(last, 0.0, dn_ref[...])], axis=0)
        cg = _conv_rows(g_ext, wg_ref, K, 8)
        cv = _conv_rows(v_ext, wv_ref, K, 8)
        _, vjp = jax.vjp(_silu_mul, cg, cv)
        dcg, dcv = vjp(d_ext)
        dg, dwg = _conv_bwd_rows(dcg, g_ext, wg_ref, K, tr)
        dv, dwv = _conv_bwd_rows(dcv, v_ext, wv_ref, K, tr)
        dg_ref[...] = dg.astype(BF16)
        dv_ref[...] = dv.astype(BF16)
        _acc_rows(dwg_ref, dwg, first)
        _acc_rows(dwv_ref, dwv, first)

    blk = lambda off: pl.BlockSpec((tr, cb), lambda j, i: (i, off + j))
    prev = lambda off: pl.BlockSpec((8, cb), lambda j, i: (jnp.maximum(i * (tr // 8) - 1, 0), off + j))
    nxt = lambda off: pl.BlockSpec((8, cb), lambda j, i: (jnp.minimum((i + 1) * (tr // 8), T // 8 - 1), off + j))
    wsp = lambda off: pl.BlockSpec((K, cb), lambda j, i: (0, off + j))
    return pl.pallas_call(
        body, name="ffn_act_bwd", grid=(nc, nb),
        in_specs=[blk(0), prev(0), nxt(0), blk(nc), prev(nc), nxt(nc), blk(0), nxt(0), wsp(0), wsp(nc)],
        out_specs=[blk(0), blk(0), wsp(0), wsp(0)],
        out_shape=[jax.ShapeDtypeStruct((T, F), BF16)] * 2 + [jax.ShapeDtypeStruct((K, F), F32)] * 2,
        compiler_params=_params("parallel", "arbitrary"),
    )(up, up, up, up, up, up, dact, dact, conv_w, conv_w)


def _ple_fwd(hn, w_gate, p, w_ple, x, *, name, tm=512, tn=512):
    T, Dm = x.shape

    def body(a_ref, b_ref, p_ref, wp_ref, x_ref, o_ref, gl_ref, pe_ref):
        gl = _dot(a_ref[...], b_ref[...], NN)
        pe = _dot(p_ref[...], wp_ref[...], NN)
        gl_ref[...] = gl
        pe_ref[...] = pe
        o_ref[...] = x_ref[...] + pe * _sigmoid(gl)

    o_spec = pl.BlockSpec((tm, tn), lambda i, j: (i, j))
    return pl.pallas_call(
        body, name=name, grid=(T // tm, Dm // tn),
        in_specs=[pl.BlockSpec((tm, Dm), lambda i, j: (i, 0)), pl.BlockSpec((Dm, tn), lambda i, j: (0, j)),
                  pl.BlockSpec((tm, PLE_DIM), lambda i, j: (i, 0)), pl.BlockSpec((PLE_DIM, tn), lambda i, j: (0, j)),
                  o_spec],
        out_specs=[o_spec] * 3, out_shape=[jax.ShapeDtypeStruct((T, Dm), F32)] * 3,
        compiler_params=_params("parallel", "parallel"),
    )(hn, w_gate, p, w_ple, x)


def _ple_bwd(dx, gl, pe, *, name, tr=512):
    T, Dm = dx.shape

    def body(dx_ref, gl_ref, pe_ref, dpe_ref, dgl_ref):
        g = _sigmoid(gl_ref[...])
        d = dx_ref[...]
        dpe_ref[...] = (d * g).astype(BF16)
        dgl_ref[...] = (d * pe_ref[...] * g * (1.0 - g)).astype(BF16)

    row = pl.BlockSpec((tr, Dm), lambda i: (i, 0))
    return pl.pallas_call(
        body, name=name, grid=(T // tr,), in_specs=[row] * 3, out_specs=[row] * 2,
        out_shape=[jax.ShapeDtypeStruct((T, Dm), BF16)] * 2, compiler_params=_params("parallel"),
    )(dx, gl, pe)


def _qkv_act(c, cb):
    s = c * _sigmoid(c)
    n = s * lax.rsqrt(jnp.sum(s * s, axis=-1, keepdims=True) + EPS)
    n = n * jnp.where(cb < GDN_HEADS, GDN_HEAD_DIM ** -0.5, 1.0)
    return jnp.where(cb < 2 * GDN_HEADS, n, s)


GDN_HPS = 4


def _gdn_pre(proj, conv_w, *, tr=512):
    T = proj.shape[0]
    K = GDN_CONV

    def body(x_ref, xh_ref, w_ref, o_ref):
        i, j = pl.program_id(0), pl.program_id(1)
        xh = jnp.where(i > 0, xh_ref[...], 0.0)
        for hh in range(GDN_HPS):
            cs = slice(128 * hh, 128 * (hh + 1))
            c = _conv_rows(jnp.concatenate([xh[:, cs], x_ref[:, cs]], axis=0), w_ref, K, 8, cs)
            o_ref[hh] = _qkv_act(c, GDN_HPS * j + hh)

    wide = 128 * GDN_HPS
    return pl.pallas_call(
        body, name="gdn_pre", grid=(T // tr, 24 // GDN_HPS),
        in_specs=[pl.BlockSpec((tr, wide), lambda i, j: (i, j)), _prev_spec(tr, wide, 8, lambda i, j: j),
                  pl.BlockSpec((K, wide), lambda i, j: (0, j))],
        out_specs=pl.BlockSpec((GDN_HPS, tr, 128), lambda i, j: (j, i, 0)),
        out_shape=jax.ShapeDtypeStruct((24, T, 128), F32), compiler_params=_params("parallel", "parallel"),
    )(proj, proj, conv_w)


def _gdn_pre_bwd(proj, conv_w, dqkv, *, tr=512):
    T = proj.shape[0]
    nb = T // tr
    K = GDN_CONV

    def body(x_ref, xp_ref, xn_ref, d_ref, dn_ref, w_ref, dx_ref, dw_ref):
        j, i = pl.program_id(0), pl.program_id(1)
        first, last = i == 0, i == nb - 1
        xp = jnp.where(first, 0.0, xp_ref[...])
        xn = jnp.where(last, 0.0, xn_ref[...])
        for hh in range(GDN_HPS):
            cs = slice(128 * hh, 128 * (hh + 1))
            x_ext = jnp.concatenate([xp[:, cs], x_ref[:, cs], xn[:, cs]], axis=0)
            d_ext = jnp.concatenate([d_ref[hh], jnp.where(last, 0.0, dn_ref[hh])], axis=0)
            c = _conv_rows(x_ext, w_ref, K, 8, cs)
            _, vjp = jax.vjp(lambda c_: _qkv_act(c_, GDN_HPS * j + hh), c)
            (dc,) = vjp(d_ext)
            dx, dws = _conv_bwd_rows(dc, x_ext, w_ref, K, tr, cs)
            dx_ref[:, cs] = dx.astype(BF16)
            _acc_rows(dw_ref, dws, first, cs)

    wide = 128 * GDN_HPS
    return pl.pallas_call(
        body, name="gdn_pre_bwd", grid=(24 // GDN_HPS, nb),
        in_specs=[pl.BlockSpec((tr, wide), lambda j, i: (i, j)),
                  pl.BlockSpec((8, wide), lambda j, i: (jnp.maximum(i * (tr // 8) - 1, 0), j)),
                  pl.BlockSpec((8, wide), lambda j, i: (jnp.minimum((i + 1) * (tr // 8), T // 8 - 1), j)),
                  pl.BlockSpec((GDN_HPS, tr, 128), lambda j, i: (j, i, 0)),
                  pl.BlockSpec((GDN_HPS, 8, 128), lambda j, i: (j, jnp.minimum((i + 1) * (tr // 8), T // 8 - 1), 0)),
                  pl.BlockSpec((K, wide), lambda j, i: (0, j))],
        out_specs=[pl.BlockSpec((tr, wide), lambda j, i: (i, j)), pl.BlockSpec((K, wide), lambda j, i: (0, j))],
        out_shape=[jax.ShapeDtypeStruct((T, 24 * 128), BF16), jax.ShapeDtypeStruct((K, 24 * 128), F32)],
        compiler_params=_params("parallel", "arbitrary"),
    )(proj, proj, proj, dqkv, dqkv, conv_w)


def _gate_fn(ba, alog_row, dt_row):
    lane = _iota((1, 128), 1)
    x = ba + dt_row
    sp = jnp.maximum(x, 0.0) + jnp.log(1.0 + jnp.exp(-jnp.abs(x)))
    return jnp.where(lane < GDN_HEADS, _sigmoid(ba), -jnp.exp(alog_row) * sp)


def _gdn_gate(ba, alog_row, dt_row, *, tr=512):
    T = ba.shape[0]

    def body(ba_ref, al_ref, dt_ref, b_ref, g_ref):
        val = _gate_fn(ba_ref[...], al_ref[...], dt_ref[...])
        for h in range(GDN_HEADS):
            b_ref[h] = val[:, h:h + 1]
            g_ref[h] = val[:, GDN_HEADS + h:GDN_HEADS + h + 1]

    vec = pl.BlockSpec((1, 128), lambda i: (0, 0))
    hm = pl.BlockSpec((GDN_HEADS, tr, 1), lambda i: (0, i, 0))
    return pl.pallas_call(
        body, name="gdn_gate", grid=(T // tr,), in_specs=[pl.BlockSpec((tr, 128), lambda i: (i, 0)), vec, vec],
        out_specs=[hm, hm], out_shape=[jax.ShapeDtypeStruct((GDN_HEADS, T, 1), F32)] * 2,
        compiler_params=_params("parallel"),
    )(ba, alog_row, dt_row)


def _gdn_gate_bwd(ba, alog_row, dt_row, dbeta, dg, *, tr=512):
    T = ba.shape[0]

    def body(ba_ref, al_ref, dt_ref, db_ref, dg_ref, dba_ref, dal_ref, ddt_ref):
        i = pl.program_id(0)
        lane = _iota((1, 128), 1)
        d = jnp.zeros((tr, 128), F32)
        for h in range(GDN_HEADS):
            d = d + jnp.where(lane == h, db_ref[h], 0.0) + jnp.where(lane == GDN_HEADS + h, dg_ref[h], 0.0)
        _, vjp = jax.vjp(_gate_fn, ba_ref[...], al_ref[...], dt_ref[...])
        dba, dal, ddt = vjp(d)
        dba_ref[...] = dba.astype(BF16)

        @pl.when(i == 0)
        def _():
            dal_ref[...] = dal
            ddt_ref[...] = ddt

        @pl.when(i > 0)
        def _():
            dal_ref[...] += dal
            ddt_ref[...] += ddt

    vec = pl.BlockSpec((1, 128), lambda i: (0, 0))
    hm = pl.BlockSpec((GDN_HEADS, tr, 1), lambda i: (0, i, 0))
    row = pl.BlockSpec((tr, 128), lambda i: (i, 0))
    return pl.pallas_call(
        body, name="gdn_gate_bwd", grid=(T // tr,), in_specs=[row, vec, vec, hm, hm], out_specs=[row, vec, vec],
        out_shape=[jax.ShapeDtypeStruct((T, 128), BF16), jax.ShapeDtypeStruct((1, 128), F32),
                   jax.ShapeDtypeStruct((1, 128), F32)],
        compiler_params=_params("arbitrary"),
    )(ba, alog_row, dt_row, dbeta, dg)


def _split3(x):
    x1 = x.astype(BF16)
    r = x - x1.astype(F32)
    x2 = r.astype(BF16)
    return x1, x2, (r - x2.astype(F32)).astype(BF16)


def _dot01(tri, x, dims):
    t = tri.astype(BF16)
    x1, x2, x3 = _split3(x)
    d = lambda xi: lax.dot_general(t, xi, dims, preferred_element_type=F32)
    return d(x1) + (d(x2) + d(x3))


def _dot3(a, b, dims):
    ah, al, _ = _split3(a)
    bh, bl, _ = _split3(b)
    d = lambda p, q: lax.dot_general(p, q, dims, preferred_element_type=F32)
    return d(ah, bh) + (d(ah, bl) + d(al, bh))


BNN = (((2,), (1,)), ((0,), (0,)))
BNT = (((2,), (2,)), ((0,), (0,)))
BTN = (((1,), (1,)), ((0,), (0,)))


@jax.custom_vjp
def _mm01(tri, x):
    return _dot01(tri, x, BNN)


def _mm01_fwd(tri, x):
    return _dot01(tri, x, BNN), tri


def _mm01_bwd(tri, ct):
    return jnp.zeros_like(tri), _dot01(tri, ct, BTN)


_mm01.defvjp(_mm01_fwd, _mm01_bwd)


def _unit_lower_inverse(a):
    C = a.shape[-1]
    eye = (_iota(a.shape, 1) == _iota(a.shape, 2)).astype(F32)
    pw = -a
    tinv = eye + pw
    for _ in range(5):
        pw = _dot3(pw, pw, BNN)
        tinv = tinv + _dot3(tinv, pw, BNN)
    return tinv


@jax.custom_vjp
def _unit_lower_solve(a, rv, rw):
    return _unit_lower_solve_fwd(a, rv, rw)[0]


def _unit_lower_solve_fwd(a, rv, rw):
    tinv = _unit_lower_inverse(a)
    sol = _dot3(tinv, jnp.concatenate([rv, rw], axis=2), BNN)
    n = rv.shape[2]
    return (sol[:, :, :n], sol[:, :, n:]), (tinv, sol)


def _unit_lower_solve_bwd(res, cts):
    tinv, sol = res
    n = cts[0].shape[2]
    d_rhs = _dot3(tinv, jnp.concatenate(cts, axis=2), BTN)
    return -_dot3(d_rhs, sol, BNT), d_rhs[:, :, :n], d_rhs[:, :, n:]


_unit_lower_solve.defvjp(_unit_lower_solve_fwd, _unit_lower_solve_bwd)


@jax.custom_vjp
def _mmb_nt(a, b):
    return _dot(a, b, BNT)


def _mmb_nt_fwd(a, b):
    return _dot(a, b, BNT), (a, b)


def _mmb_nt_bwd(res, ct):
    a, b = res
    return _dot(ct, b, BNN), _dot(ct, a, BTN)


_mmb_nt.defvjp(_mmb_nt_fwd, _mmb_nt_bwd)


def _gdn_chunk(q, k, v, gcol, bcol):
    nb, C = q.shape[0], GDN_CHUNK
    row, col = _iota((nb, C, C), 1), _iota((nb, C, C), 2)
    incl, strict = row >= col, row > col
    eye = (row == col).astype(F32)
    lower = incl.astype(F32)
    ones = jnp.ones((nb, C, C), F32)
    gwide = jnp.broadcast_to(gcol, (nb, C, GDN_HEAD_DIM))
    gc = _mm01(lower, gwide)
    gtot = _mm01(ones, gwide)
    gc_c = _mm01(lower, jnp.broadcast_to(gcol, (nb, C, C)))
    gc_s = _mm01(ones, gc_c * eye)
    decay = jnp.where(incl, jnp.exp(jnp.where(incl, gc_c - gc_s, 0.0)), 0.0)
    kb = k * bcol
    a = jnp.where(strict, _mmb_nt(kb, k) * decay, 0.0)
    egc = jnp.exp(gc)
    u, w = _unit_lower_solve(a, v * bcol, kb * egc)
    qk = jnp.where(incl, _mmb_nt(q, k) * decay, 0.0)
    return u, w, qk, q * egc, k * jnp.exp(gtot - gc), jnp.exp(jnp.sum(gwide, axis=1))


GDN_ROWS = 8 * GDN_CHUNK


def _gdn_specs(T):
    hd = lambda off: pl.BlockSpec((1, GDN_ROWS, 128), lambda h, i: (off + h, i, 0))
    col = pl.BlockSpec((1, GDN_ROWS, 1), lambda h, i: (h, i, 0))
    sq = pl.BlockSpec((1, GDN_ROWS, GDN_CHUNK), lambda h, i: (h, i, 0))
    gl = pl.BlockSpec((1, 8, 128), lambda h, i: (h, i, 0))
    return hd, col, sq, gl


def _gdn_local(qkv, g, beta):
    T = qkv.shape[1]
    hd, col, sq, gl_spec = _gdn_specs(T)

    def body(q_ref, k_ref, v_ref, g_ref, b_ref, u_ref, w_ref, qk_ref, qd_ref, kd_ref, gl_ref):
        chunks = lambda ref: ref[0].reshape(8, GDN_CHUNK, ref.shape[2])
        rows = lambda val: val.reshape(GDN_ROWS, val.shape[2])
        u, w, qk, qd, kd, gl = _gdn_chunk(chunks(q_ref), chunks(k_ref), chunks(v_ref), chunks(g_ref), chunks(b_ref))
        u_ref[0] = rows(u)
        w_ref[0] = rows(w).astype(BF16)
        qk_ref[0] = rows(qk).astype(BF16)
        qd_ref[0] = rows(qd).astype(BF16)
        kd_ref[0] = rows(kd).astype(BF16)
        gl_ref[0] = gl

    H = GDN_HEADS
    return pl.pallas_call(
        body, name="gdn_local", grid=(H, T // GDN_ROWS),
        in_specs=[hd(0), hd(H), hd(2 * H), col, col],
        out_specs=[hd(0), hd(0), sq, hd(0), hd(0), gl_spec],
        out_shape=[jax.ShapeDtypeStruct((H, T, 128), F32), jax.ShapeDtypeStruct((H, T, 128), BF16),
                   jax.ShapeDtypeStruct((H, T, GDN_CHUNK), BF16), jax.ShapeDtypeStruct((H, T, 128), BF16),
                   jax.ShapeDtypeStruct((H, T, 128), BF16), jax.ShapeDtypeStruct((H, T // GDN_CHUNK, 128), F32)],
        compiler_params=_params("parallel", "parallel"),
    )(qkv, qkv, qkv, g, beta)


def _gdn_local_bwd(qkv, g, beta, du, dw, dqk, dqd, dkd, dgl):
    T = qkv.shape[1]
    hd, col, sq, gl_spec = _gdn_specs(T)

    def body(q_ref, k_ref, v_ref, g_ref, b_ref, du_ref, dw_ref, dqk_ref, dqd_ref, dkd_ref, dgl_ref,
             dq_ref, dk_ref, dv_ref, dg_ref, db_ref):
        chunks = lambda ref: ref[0].reshape(8, GDN_CHUNK, ref.shape[2])
        rows = lambda val: val.reshape(GDN_ROWS, val.shape[2])
        _, vjp = jax.vjp(_gdn_chunk, chunks(q_ref), chunks(k_ref), chunks(v_ref), chunks(g_ref), chunks(b_ref))
        dq, dk, dv, dg, db = vjp((chunks(du_ref), chunks(dw_ref), chunks(dqk_ref), chunks(dqd_ref), chunks(dkd_ref),
                                  dgl_ref[0]))
        dq_ref[0] = rows(dq)
        dk_ref[0] = rows(dk)
        dv_ref[0] = rows(dv)
        dg_ref[0] = rows(dg)
        db_ref[0] = rows(db)

    H = GDN_HEADS
    big = jax.ShapeDtypeStruct((H, T, 128), F32)
    small = jax.ShapeDtypeStruct((H, T, 1), F32)
    return pl.pallas_call(
        body, name="gdn_local_bwd", grid=(H, T // GDN_ROWS),
        in_specs=[hd(0), hd(H), hd(2 * H), col, col, hd(0), hd(0), sq, hd(0), hd(0), gl_spec],
        out_specs=[hd(0), hd(0), hd(0), col, col], out_shape=[big, big, big, small, small],
        compiler_params=_params("parallel", "parallel"),
    )(qkv, qkv, qkv, g, beta, du, dw, dqk, dqd, dkd, dgl)


GDN_HB = 4


def _gdn_scan_specs(T, rev):
    nb = T // GDN_ROWS
    blk = (lambda i: nb - 1 - i) if rev else (lambda i: i)
    hd = pl.BlockSpec((GDN_HB, GDN_ROWS, 128), lambda h, i: (h, blk(i), 0))
    sq = pl.BlockSpec((GDN_HB, GDN_ROWS, GDN_CHUNK), lambda h, i: (h, blk(i), 0))
    gl = pl.BlockSpec((GDN_HB, 8, 128), lambda h, i: (h, blk(i), 0))
    st = pl.BlockSpec((GDN_HB, 8, 128, 128), lambda h, i: (h, blk(i), 0, 0))
    return hd, sq, gl, st


def _gdn_scan(u, w, qk, qd, kd, gl):
    H, T, _ = u.shape
    hd, sq, gl_spec, st = _gdn_scan_specs(T, False)

    def body(u_ref, w_ref, qk_ref, qd_ref, kd_ref, gl_ref, o_ref, ss_ref, vn_ref, s_scr):
        @pl.when(pl.program_id(1) == 0)
        def _():
            s_scr[...] = jnp.zeros_like(s_scr)

        dot = lambda a, b, dims: lax.dot_general(a, b, dims, preferred_element_type=F32)
        s = s_scr[...]
        for c in range(8):
            rs = slice(GDN_CHUNK * c, GDN_CHUNK * (c + 1))
            ss_ref[:, c] = s
            sb = s.astype(BF16)
            vn = u_ref[:, rs, :] - dot(w_ref[:, rs, :], sb, BNN)
            vnb = vn.astype(BF16)
            o_ref[:, rs, :] = dot(qd_ref[:, rs, :], sb, BNN) + dot(qk_ref[:, rs, :], vnb, BNN)
            vn_ref[:, rs, :] = vnb
            s = s * gl_ref[:, c:c + 1, :] + dot(kd_ref[:, rs, :], vnb, BTN)
        s_scr[...] = s

    return pl.pallas_call(
        body, name="gdn_scan", grid=(H // GDN_HB, T // GDN_ROWS),
        in_specs=[hd, hd, sq, hd, hd, gl_spec], out_specs=[hd, st, hd],
        out_shape=[jax.ShapeDtypeStruct((H, T, 128), F32), jax.ShapeDtypeStruct((H, T // GDN_CHUNK, 128, 128), F32),
                   jax.ShapeDtypeStruct((H, T, 128), BF16)],
        scratch_shapes=[pltpu.VMEM((GDN_HB, 128, 128), F32)],
        compiler_params=_params("parallel", "arbitrary"),
    )(u, w, qk, qd, kd, gl)


def _gdn_scan_bwd(do, ss, vn, w, qk, qd, kd, gl):
    H, T, _ = do.shape
    hd, sq, gl_spec, st = _gdn_scan_specs(T, True)

    def body(do_ref, ss_ref, vn_ref, w_ref, qk_ref, qd_ref, kd_ref, gl_ref,
             du_ref, dw_ref, dqk_ref, dqd_ref, dkd_ref, dgl_ref, ds_scr):
        @pl.when(pl.program_id(1) == 0)
        def _():
            ds_scr[...] = jnp.zeros_like(ds_scr)

        dot = lambda a, b, dims: lax.dot_general(a, b, dims, preferred_element_type=F32)
        ds = ds_scr[...]
        for c in reversed(range(8)):
            rs = slice(GDN_CHUNK * c, GDN_CHUNK * (c + 1))
            s = ss_ref[:, c]
            sb, dsb = s.astype(BF16), ds.astype(BF16)
            dob = do_ref[:, rs, :].astype(BF16)
            vnb = vn_ref[:, rs, :]
            dvn = dot(qk_ref[:, rs, :], dob, BTN) + dot(kd_ref[:, rs, :], dsb, BNN)
            dvnb = dvn.astype(BF16)
            du_ref[:, rs, :] = dvn
            dw_ref[:, rs, :] = -dot(dvnb, sb, BNT)
            dqk_ref[:, rs, :] = dot(dob, vnb, BNT)
            dqd_ref[:, rs, :] = dot(dob, sb, BNT)
            dkd_ref[:, rs, :] = dot(vnb, dsb, BNT)
            dgl_ref[:, c:c + 1, :] = jnp.sum(ds * s, axis=1, keepdims=True)
            ds = dot(qd_ref[:, rs, :], dob, BTN) + ds * gl_ref[:, c:c + 1, :] - dot(w_ref[:, rs, :], dvnb, BTN)
        ds_scr[...] = ds

    big = jax.ShapeDtypeStruct((H, T, 128), F32)
    return pl.pallas_call(
        body, name="gdn_scan_bwd", grid=(H // GDN_HB, T // GDN_ROWS),
        in_specs=[hd, st, hd, hd, sq, hd, hd, gl_spec], out_specs=[hd, hd, sq, hd, hd, gl_spec],
        out_shape=[big, big, jax.ShapeDtypeStruct((H, T, GDN_CHUNK), F32), big, big,
                   jax.ShapeDtypeStruct((H, T // GDN_CHUNK, 128), F32)],
        scratch_shapes=[pltpu.VMEM((GDN_HB, 128, 128), F32)],
        compiler_params=_params("parallel", "arbitrary"),
    )(do, ss, vn, w, qk, qd, kd, gl)


def _gated_norm(o, z, nw):
    on = o * lax.rsqrt(jnp.mean(o * o, axis=-1, keepdims=True) + EPS) * nw
    return on * (z * _sigmoid(z))


def _gdn_post(o, proj, norm_w, *, tr=512):
    T = proj.shape[0]

    def body(o_ref, z_ref, n_ref, y_ref):
        y_ref[...] = _gated_norm(o_ref[0], z_ref[...], n_ref[...]).astype(BF16)

    return pl.pallas_call(
        body, name="gdn_post", grid=(T // tr, GDN_HEADS),
        in_specs=[pl.BlockSpec((1, tr, 128), lambda i, h: (h, i, 0)), pl.BlockSpec((tr, 128), lambda i, h: (i, 24 + h)),
                  pl.BlockSpec((1, 128), lambda i, h: (0, 0))],
        out_specs=pl.BlockSpec((tr, 128), lambda i, h: (i, h)),
        out_shape=jax.ShapeDtypeStruct((T, 1024), BF16), compiler_params=_params("parallel", "parallel"),
    )(o, proj, norm_w)


def _gdn_post_bwd(o, proj, norm_w, dy, *, tr=512):
    T = proj.shape[0]

    def body(o_ref, z_ref, n_ref, dy_ref, do_ref, dz_ref, dn_ref):
        first = (pl.program_id(0) == 0) & (pl.program_id(1) == 0)
        _, vjp = jax.vjp(_gated_norm, o_ref[0], z_ref[...], n_ref[...])
        do, dz, dn = vjp(dy_ref[...])
        do_ref[0] = do
        dz_ref[...] = dz.astype(BF16)

        @pl.when(first)
        def _():
            dn_ref[...] = dn

        @pl.when(jnp.logical_not(first))
        def _():
            dn_ref[...] += dn

    blk = pl.BlockSpec((tr, 128), lambda i, h: (i, h))
    hm = pl.BlockSpec((1, tr, 128), lambda i, h: (h, i, 0))
    vec = pl.BlockSpec((1, 128), lambda i, h: (0, 0))
    return pl.pallas_call(
        body, name="gdn_post_bwd", grid=(T // tr, GDN_HEADS),
        in_specs=[hm, pl.BlockSpec((tr, 128), lambda i, h: (i, 24 + h)), vec, blk], out_specs=[hm, blk, vec],
        out_shape=[jax.ShapeDtypeStruct((GDN_HEADS, T, 128), F32), jax.ShapeDtypeStruct((T, 1024), BF16),
                   jax.ShapeDtypeStruct((1, 128), F32)],
        compiler_params=_params("arbitrary", "arbitrary"),
    )(o, proj, norm_w, dy)


HBM_SPEC = pl.BlockSpec(memory_space=pltpu.HBM)


def _place():
    return lax.axis_index("x"), lax.axis_index("y"), lax.axis_index("c")


def _all_gather(vs, *, name):
    n = len(vs)

    def body(*refs):
        start, forward, finish = _gather_stages(refs[:n], refs[n:2 * n], *refs[2 * n:])
        start()
        forward()
        finish()

    return pl.pallas_call(
        body, name=name, out_shape=_gather_shapes(vs), in_specs=[HBM_SPEC] * n, out_specs=[HBM_SPEC] * n,
        scratch_shapes=_gather_sems(n),
    )(*vs)


def _gather_shapes(vs):
    return [jax.ShapeDtypeStruct((N_DEV,) + v.shape, v.dtype) for v in vs]


def _gather_sems(n):
    return [pltpu.SemaphoreType.DMA((7 * n,)), pltpu.SemaphoreType.DMA((7 * n,)), pltpu.SemaphoreType.DMA((n,))]


def _gather_stages(v_refs, out_refs, send_sems, recv_sems, local_sems):
    n = len(v_refs)
    x, y, c = _place()
    me, sibling = (x, y, c), (x, y, 1 - c)
    chips = [(1 - x, y), (x, 1 - y), (1 - x, 1 - y)]

    def copy(a, k, block, to, from_input=False):
        slot = out_refs[a].at[4 * block[0] + 2 * block[1] + block[2]]
        return pltpu.make_async_remote_copy(
            src_ref=v_refs[a] if from_input else slot, dst_ref=slot,
            send_sem=send_sems.at[7 * a + k], recv_sem=recv_sems.at[7 * a + k], device_id=to, device_id_type=MESH)

    def mine():
        return [pltpu.make_async_copy(v_refs[a], out_refs[a].at[4 * x + 2 * y + c], local_sems.at[a]) for a in range(n)]

    def first():
        return ([copy(a, 0, me, sibling, True) for a in range(n)]
                + [copy(a, 1 + j, me, (*chip, c), True) for j, chip in enumerate(chips) for a in range(n)])

    def passed():
        return [copy(a, 4 + j, (*chip, c), sibling) for j, chip in enumerate(chips) for a in range(n)]

    def start():
        for cp in mine() + first():
            cp.start()

    def forward():
        for j, chip in enumerate(chips):
            for a in range(n):
                copy(a, 1 + j, (*chip, c), me).wait_recv()
                copy(a, 4 + j, (*chip, c), sibling).start()

    def finish():
        for a in range(n):
            copy(a, 0, sibling, me).wait_recv()
            for j, chip in enumerate(chips):
                copy(a, 4 + j, (*chip, 1 - c), me).wait_recv()
        for cp in first() + passed():
            cp.wait_send()
        for cp in mine():
            cp.wait()

    return start, forward, finish


def _exchange_sibling(gs, *, name):
    n = len(gs)

    def body(*refs):
        g_refs, out_refs = refs[:n], refs[n:2 * n]
        send_sems, recv_sems = refs[2 * n:]
        x, y, c = _place()
        copies = [pltpu.make_async_remote_copy(
            src_ref=g_refs[a].at[k, 1 - c], dst_ref=out_refs[a].at[k], send_sem=send_sems.at[4 * a + k],
            recv_sem=recv_sems.at[4 * a + k], device_id=(x, y, 1 - c), device_id_type=MESH)
            for a in range(n) for k in range(4)]
        for cp in copies:
            cp.start()
        for cp in copies:
            cp.wait()

    return pl.pallas_call(
        body, name=name, out_shape=[jax.ShapeDtypeStruct((4,) + g.shape[2:], g.dtype) for g in gs],
        in_specs=[HBM_SPEC] * n, out_specs=[HBM_SPEC] * n,
        scratch_shapes=[pltpu.SemaphoreType.DMA((4 * n,)), pltpu.SemaphoreType.DMA((4 * n,))],
    )(*gs)


def _exchange_chips(pcs):
    n = len(pcs)

    def body(*refs):
        start, finish = _chips_stages(refs[:n], refs[n:2 * n], *refs[2 * n:])
        start()
        finish()

    return pl.pallas_call(
        body, name="rs_chips", out_shape=_chips_shapes(pcs), in_specs=[HBM_SPEC] * n, out_specs=[HBM_SPEC] * n,
        scratch_shapes=_chips_sems(n),
    )(*pcs)


def _chips_shapes(pcs):
    return [jax.ShapeDtypeStruct((3,) + pc.shape[1:], pc.dtype) for pc in pcs]


def _chips_sems(n):
    return [pltpu.SemaphoreType.DMA((3 * n,)), pltpu.SemaphoreType.DMA((3 * n,))]


def _chips_stages(p_refs, out_refs, send_sems, recv_sems):
    n = len(p_refs)
    x, y, c = _place()
    chips = [(1 - x, y), (x, 1 - y), (1 - x, 1 - y)]

    def copies():
        return [pltpu.make_async_remote_copy(
            src_ref=p_refs[a].at[2 * cx + cy], dst_ref=out_refs[a].at[j], send_sem=send_sems.at[3 * a + j],
            recv_sem=recv_sems.at[3 * a + j], device_id=(cx, cy, c), device_id_type=MESH)
            for j, (cx, cy) in enumerate(chips) for a in range(n)]

    def start():
        for cp in copies():
            cp.start()

    def finish():
        for cp in copies():
            cp.wait()

    return start, finish


def _chip_partial(place, g, got, *, tr, name):
    R, W = g.shape[2:]

    def body(pl_ref, g_ref, r_ref, o_ref):
        o_ref[...] = (g_ref[0] + r_ref[...]).astype(BF16)

    return pl.pallas_call(
        body, name=name, out_shape=jax.ShapeDtypeStruct((4, R, W), BF16),
        grid_spec=pltpu.PrefetchScalarGridSpec(
            num_scalar_prefetch=1, grid=(4, R // tr),
            in_specs=[pl.BlockSpec((1, 1, tr, W), lambda k, i, pr: (k, pr[2], i, 0)),
                      pl.BlockSpec((1, tr, W), lambda k, i, pr: (k, i, 0))],
            out_specs=pl.BlockSpec((1, tr, W), lambda k, i, pr: (k, i, 0))),
        compiler_params=_params("parallel", "parallel"),
    )(place, g, got)


def _adamw_math(g, w, m, v):
    m = ADAM_B1 * m + (1.0 - ADAM_B1) * g
    v = ADAM_B2 * v + (1.0 - ADAM_B2) * (g * g)
    m_hat = m / (1.0 - ADAM_B1 ** ADAM_STEP)
    v_hat = v / (1.0 - ADAM_B2 ** ADAM_STEP)
    return -ADAM_LR * (m_hat / (jnp.sqrt(v_hat) + ADAM_EPS) + ADAM_WD * w), m, v


def _adamw_shard(place, g, got1, got2, w, m, v, *, tr, name):
    R, W = w.shape

    def body(pl_ref, g_ref, r1_ref, r2_ref, w_ref, m_ref, v_ref, go_ref, d_ref, mo_ref, vo_ref):
        gs = g_ref[0, 0] + r1_ref[0]
        for j in range(3):
            gs = gs + r2_ref[j].astype(F32)
        go_ref[...] = gs
        d_ref[...], mo_ref[...], vo_ref[...] = _adamw_math(gs, w_ref[...], m_ref[...], v_ref[...])

    row = pl.BlockSpec((tr, W), lambda i, pr: (i, 0))
    out = jax.ShapeDtypeStruct((R, W), F32)
    return pl.pallas_call(
        body, name=name, out_shape=[out] * 4,
        grid_spec=pltpu.PrefetchScalarGridSpec(
            num_scalar_prefetch=1, grid=(R // tr,),
            in_specs=[pl.BlockSpec((1, 1, tr, W), lambda i, pr: (2 * pr[0] + pr[1], pr[2], i, 0)),
                      pl.BlockSpec((1, tr, W), lambda i, pr: (2 * pr[0] + pr[1], i, 0)),
                      pl.BlockSpec((3, tr, W), lambda i, pr: (0, i, 0)), row, row, row],
            out_specs=[row] * 4),
        compiler_params=_params("parallel"),
    )(place, g, got1, got2, w, m, v)


def _adamw_replicated(parts, w, m, v):
    R, W = w.shape

    def body(p_ref, w_ref, m_ref, v_ref, go_ref, d_ref, mo_ref, vo_ref):
        gs = p_ref[0]
        for j in range(1, N_DEV):
            gs = gs + p_ref[j]
        go_ref[...] = gs
        d_ref[...], mo_ref[...], vo_ref[...] = _adamw_math(gs, w_ref[...], m_ref[...], v_ref[...])

    full = pl.BlockSpec((R, W), lambda i: (0, 0))
    out = jax.ShapeDtypeStruct((R, W), F32)
    return pl.pallas_call(
        body, name="adamw_replicated", grid=(1,), out_shape=[out] * 4,
        in_specs=[pl.BlockSpec((N_DEV, R, W), lambda i: (0, 0, 0)), full, full, full], out_specs=[full] * 4,
        compiler_params=_params("arbitrary"),
    )(parts, w, m, v)


GROUPS = {
    "g256": (256, 512, (("w_in_e", 1024, 1024),)),
    "g1024": (1024, 304, (("w_out_e", 128, 128), ("w_out_o", 128, 128), ("w_down", 704, 704), ("w_ple_gate", 256, 256))),
    "g514": (514, 512, (("w_in_o", 1024, 1024),)),
    "g704": (704, 272, (("w_up", 2048, 2048), ("ffn_conv", 6, 128))),
    "g128": (128, 528, (("w_ple", 512, 512), ("mix_norm_o", 1, 16))),
    "g384": (384, 8, (("conv_qkv_o", 4, 8),)),
}
SHARDED = tuple(p[0] for g in GROUPS.values() for p in g[2])
COLUMN_SHARDED = ("w_in_e", "w_in_o", "w_up", "ffn_conv", "w_ple", "conv_qkv_o", "mix_norm_o")
PACK_W = 1024
REPL_LAYOUT = (
    ("mix_norm_e", (1, 1024), 8), ("pool_w", (1, 4, 128, 128), 64), ("pool_scale", (1, 512), 8),
    ("a_log_o", (1, 8), 8), ("dt_bias_o", (1, 8), 8), ("gdn_norm_o", (1, 128), 8),
    ("ffn_norm", (2, 1024), 8), ("ple_norm", (2, 1024), 8), ("final_norm", (1024,), 8),
)


def _pad_rows(a, rows):
    extra = rows - a.shape[-2]
    return a if extra == 0 else jnp.pad(a, [(0, 0)] * (a.ndim - 2) + [(0, extra), (0, 0)])


def _group_rows(pieces, gname):
    parts = [_pad_rows(pieces[name], padded) for name, _, padded in GROUPS[gname][2]]
    return parts[0] if len(parts) == 1 else jnp.concatenate(parts, axis=-2)


def _ungroup_rows(buf, gname):
    out, r0 = {}, 0
    for name, rows, padded in GROUPS[gname][2]:
        out[name] = buf[..., r0:r0 + rows, :]
        r0 += padded
    return out


def _shard_major(name, gfull, n_layers):
    per_layer = []
    for g in gfull:
        if name in COLUMN_SHARDED:
            k = g.shape[0]
            per_layer.append(jnp.moveaxis(g.reshape(k, N_DEV, g.shape[1] // N_DEV), 1, 0))
        else:
            per_layer.append(g.reshape(N_DEV, g.shape[0] // N_DEV, -1))
    return per_layer[0] if n_layers == 1 else jnp.concatenate(per_layer, axis=1)


def _natural(name, gathered, n_layers):
    rows = gathered.shape[1] // n_layers
    out = []
    for layer in range(n_layers):
        piece = gathered[:, layer * rows:(layer + 1) * rows]
        if name in COLUMN_SHARDED:
            out.append(jnp.moveaxis(piece, 0, 1).reshape(rows, N_DEV * piece.shape[2]))
        else:
            out.append(piece.reshape(N_DEV * rows, piece.shape[2]))
    return out


def _rows(a, rows):
    flat = a.reshape(-1)
    return jnp.pad(flat, (0, rows * PACK_W - flat.shape[0])).reshape(rows, PACK_W)


def _pack_repl(vals):
    return jnp.concatenate([_rows(vals[name].reshape(shape), rows) for name, shape, rows in REPL_LAYOUT], axis=0)


def _unpack_repl(buf):
    out, r0 = {}, 0
    for name, shape, rows in REPL_LAYOUT:
        n = 1
        for s in shape:
            n *= s
        out[name] = buf[r0:r0 + rows].reshape(-1)[:n].reshape(shape)
        r0 += rows
    return out


WEIGHTS = ("mix_norm_e", "w_in_e", "pool_w", "pool_scale", "w_out_e", "mix_norm_o", "w_in_o", "conv_qkv_o", "a_log_o",
           "dt_bias_o", "gdn_norm_o", "w_out_o", "ffn_norm", "w_up", "ffn_conv", "w_down", "ple_norm", "w_ple_gate",
           "w_ple", "final_norm")


def _ffn_forward(x, norm_g, w_up, conv_w, w_down, tag):
    hn = _rms_fwd(x, norm_g, name="rms_ffn" + tag)
    up = _mm(hn, w_up, tm=1024, tn=512, name="ffn_up" + tag)
    act = _ffn_act(up, conv_w)
    out = _mm(act, w_down, res=x, tm=1024, tn=512, tk=1408, name="ffn_down" + tag)
    return out, (x, hn, up, act)


def _ffn_backward(dx, saved, norm_g, w_up, conv_w, w_down, tag):
    x, hn, up, act = saved
    dact = _mm(dx, w_down, tb=True, tm=512, tn=1408, name="ffn_dact" + tag)
    d_w_down = _mm(act, dx, ta=True, tm=1408, tn=512, tk=512, name="ffn_dwdown" + tag)
    dgate, dval, dcg, dcv = _ffn_act_bwd(up, conv_w, dact)
    dhn = _mm(dgate, w_up[:, :FFN_DIM], tb=True, tm=1024, tn=512, tk=1408, name="ffn_dhn_g" + tag)
    dhn = _mm(dval, w_up[:, FFN_DIM:], tb=True, res=dhn, tm=1024, tn=512, tk=1408, name="ffn_dhn_v" + tag)
    d_w_up = jnp.concatenate([_mm(hn, dgate, ta=True, tm=1024, tn=1408, tk=512, name="ffn_dwup_g" + tag),
                              _mm(hn, dval, ta=True, tm=1024, tn=1408, tk=512, name="ffn_dwup_v" + tag)], axis=1)
    dx, d_norm = _rms_bwd(x, dhn, norm_g, dx, name="rms_ffn_bwd" + tag)
    return dx, d_norm, d_w_up, jnp.concatenate([dcg, dcv], axis=1), d_w_down


def _ple_forward(x, norm_g, w_gate, p, w_ple, tag):
    hn = _rms_fwd(x, norm_g, name="rms_ple" + tag)
    out, gl, pe = _ple_fwd(hn, w_gate, p, w_ple, x, name="ple_fwd" + tag)
    return out, (x, hn, gl, pe)


def _ple_backward(dx, saved, norm_g, w_gate, p, tag):
    x, hn, gl, pe = saved
    dpe, dgl = _ple_bwd(dx, gl, pe, name="ple_bwd" + tag)
    d_w_ple = _mm(p, dpe, ta=True, tm=256, tn=512, tk=512, name="ple_dwple" + tag)
    d_w_gate = _mm(hn, dgl, ta=True, tm=1024, tn=512, tk=512, name="ple_dwgate" + tag)
    dhn = _mm(dgl, w_gate, tb=True, tm=1024, tn=512, name="ple_dhn" + tag)
    dx, d_norm = _rms_bwd(x, dhn, norm_g, dx, name="rms_ple_bwd" + tag)
    return dx, d_norm, d_w_gate, d_w_ple


def kernel(x, p, mix_norm_e, w_in_e, pool_w, pool_scale, w_out_e, mix_norm_o, w_in_o, conv_qkv_o, a_log_o, dt_bias_o, gdn_norm_o, w_out_o, ffn_norm, w_up, ffn_conv, w_down, ple_norm, w_ple_gate, w_ple, final_norm, loss_target, m_mix_norm_e, m_w_in_e, m_pool_w, m_pool_scale, m_w_out_e, m_mix_norm_o, m_w_in_o, m_conv_qkv_o, m_a_log_o, m_dt_bias_o, m_gdn_norm_o, m_w_out_o, m_ffn_norm, m_w_up, m_ffn_conv, m_w_down, m_ple_norm, m_w_ple_gate, m_w_ple, m_final_norm, v_mix_norm_e, v_w_in_e, v_pool_w, v_pool_scale, v_w_out_e, v_mix_norm_o, v_w_in_o, v_conv_qkv_o, v_a_log_o, v_dt_bias_o, v_gdn_norm_o, v_w_out_o, v_ffn_norm, v_w_up, v_ffn_conv, v_w_down, v_ple_norm, v_w_ple_gate, v_w_ple, v_final_norm):
    given = dict(locals())
    place = jnp.stack(_place()).astype(jnp.int32)
    x0, tgt = x[0], loss_target[0]

    def pieces(prefix):
        return {name: given[prefix + name].reshape(rows, GROUPS[g][0])
                for g in GROUPS for name, rows, _ in GROUPS[g][2]}

    loc = pieces("")
    small = ("ffn_conv", "mix_norm_o", "conv_qkv_o")
    got = _all_gather([loc["w_in_e"].astype(BF16)] + [_pad_rows(loc[k], 8) for k in small], name="ag_first")
    full = {("w_in_e", 0): _natural("w_in_e", got[0], 1)[0]}
    for i in range(2):
        full[("ffn_conv", i)] = _natural("ffn_conv", got[1][:, 3 * i:3 * i + 3], 1)[0]
    mix_norm_o_full = got[2][:, 0].reshape(1, D_MODEL)
    conv_qkv = _natural("conv_qkv_o", got[3][:, :4], 1)[0]
    alog_row = jnp.pad(a_log_o, ((0, 0), (8, 112)))
    dt_row = jnp.pad(dt_bias_o, ((0, 0), (8, 112)))
    lw = lambda name, i: full[(name, i)]

    h_e = _rms_fwd(x0, mix_norm_e, name="rms_mix_e")
    proj_e = _mm(h_e, lw("w_in_e", 0), tm=1024, tn=512, name="in_e")
    pool_o = _pool_fwd(proj_e, pool_w[0], pool_scale)
    send = [_group_rows({k: v.astype(BF16) for k, v in loc.items()}, g) for g in ("g1024", "g514")]
    att_o, lsum, got = _sb_fwd(proj_e, gather=send + [loc["w_up"].astype(BF16), loc["w_ple"].astype(BF16)])
    gathered = {**_ungroup_rows(got[0], "g1024"), "w_in_o": got[1], "w_up": got[2], "w_ple": got[3]}
    layers = {name: given[name].shape[0] if given[name].ndim == 3 else 1 for name in gathered}
    full.update({(name, i): w for name in gathered for i, w in enumerate(_natural(name, gathered[name], layers[name]))})
    w_in_o_full = full[("w_in_o", 0)]
    w_in_o_main = w_in_o_full[:, :4096]
    w_in_o_ba = jnp.pad(w_in_o_full[:, 4096:], ((0, 0), (0, 112)))
    mix_e = jnp.concatenate([pool_o, att_o.astype(BF16)], axis=1)
    x1 = _mm(mix_e, lw("w_out_e", 0), res=x0, tm=1024, tn=512, name="out_e")
    x2, ffn0 = _ffn_forward(x1, ffn_norm[0:1], lw("w_up", 0), lw("ffn_conv", 0), lw("w_down", 0), "0")
    x3, ple0 = _ple_forward(x2, ple_norm[0:1], lw("w_ple_gate", 0), p[0, 0], lw("w_ple", 0), "0")

    h_o = _rms_fwd(x3, mix_norm_o_full, name="rms_mix_o")
    proj_o = _mm(h_o, w_in_o_main, tm=1024, tn=512, name="in_o")
    ba = _mm(h_o, w_in_o_ba, tm=1024, tn=128, name="in_o_ba")
    qkv = _gdn_pre(proj_o, conv_qkv)
    beta, g = _gdn_gate(ba, alog_row, dt_row)
    u, w_c, qk, qd, kd, gl = _gdn_local(qkv, g, beta)
    o, states, vnew = _gdn_scan(u, w_c, qk, qd, kd, gl)
    y_o = _gdn_post(o, proj_o, gdn_norm_o)
    x4 = _mm(y_o, lw("w_out_o", 0), res=x3, tm=1024, tn=512, name="out_o")
    x5, ffn1 = _ffn_forward(x4, ffn_norm[1:2], lw("w_up", 1), lw("ffn_conv", 1), lw("w_down", 1), "1")
    x6, ple1 = _ple_forward(x5, ple_norm[1:2], lw("w_ple_gate", 1), p[1, 0], lw("w_ple", 1), "1")
    loss_row, dx, d_final = _final_loss(x6, final_norm.reshape(1, D_MODEL), tgt)

    grads, rgrads = {}, {}
    dx, d_ple1, grads[("w_ple_gate", 1)], grads[("w_ple", 1)] = _ple_backward(dx, ple1, ple_norm[1:2], lw("w_ple_gate", 1), p[1, 0], "1")
    dx, d_ffn1, grads[("w_up", 1)], grads[("ffn_conv", 1)], grads[("w_down", 1)] = _ffn_backward(
        dx, ffn1, ffn_norm[1:2], lw("w_up", 1), lw("ffn_conv", 1), lw("w_down", 1), "1")
    grads[("w_out_o", 0)] = _mm(y_o, dx, ta=True, tm=1024, tn=512, tk=512, name="dw_out_o")
    dy_o = _mm(dx, lw("w_out_o", 0), tb=True, tm=1024, tn=512, name="dy_o")
    do, dz, rgrads["gdn_norm_o"] = _gdn_post_bwd(o, proj_o, gdn_norm_o, dy_o)
    du, dw_c, dqk, dqd, dkd, dgl = _gdn_scan_bwd(do, states, vnew, w_c, qk, qd, kd, gl)
    dq, dk, dv, dg, dbeta = _gdn_local_bwd(qkv, g, beta, du, dw_c, dqk, dqd, dkd, dgl)
    dqkv, grads[("conv_qkv_o", 0)] = _gdn_pre_bwd(proj_o, conv_qkv, jnp.concatenate([dq, dk, dv], axis=0))
    dba, d_alog, d_dt = _gdn_gate_bwd(ba, alog_row, dt_row, dbeta, dg)
    rgrads["a_log_o"], rgrads["dt_bias_o"] = d_alog[:, 8:16], d_dt[:, 8:16]
    dproj_o = jnp.concatenate([dqkv, dz], axis=1)
    dh = _mm(dproj_o, w_in_o_main, tb=True, tm=1024, tn=512, tk=1024, name="dh_o")
    dh = _mm(dba, w_in_o_ba, tb=True, res=dh, tm=1024, tn=512, name="dh_o_ba")
    grads[("w_in_o", 0)] = jnp.concatenate(
        [_mm(h_o, dproj_o, ta=True, tm=1024, tn=512, tk=512, name="dw_in_o"),
         _mm(h_o, dba, ta=True, tm=1024, tn=128, tk=512, name="dw_in_o_ba")[:, :16]], axis=1)
    dx, d_mix_o = _rms_bwd(x3, dh, mix_norm_o_full, dx, name="rms_mix_o_bwd")
    grads[("mix_norm_o", 0)] = d_mix_o

    dx, d_ple0, grads[("w_ple_gate", 0)], grads[("w_ple", 0)] = _ple_backward(dx, ple0, ple_norm[0:1], lw("w_ple_gate", 0), p[0, 0], "0")
    dx, d_ffn0, grads[("w_up", 0)], grads[("ffn_conv", 0)], grads[("w_down", 0)] = _ffn_backward(
        dx, ffn0, ffn_norm[0:1], lw("w_up", 0), lw("ffn_conv", 0), lw("w_down", 0), "0")
    grads[("w_out_e", 0)] = _mm(mix_e, dx, ta=True, tm=1024, tn=512, tk=512, name="dw_out_e")
    dmix = _mm(dx, lw("w_out_e", 0), tb=True, tm=1024, tn=512, name="dmix_e")
    du_e, d_pool_w, rgrads["pool_scale"] = _pool_bwd(proj_e, dmix, pool_w[0], pool_scale)
    rgrads["pool_w"] = d_pool_w[None]

    def reduce_start(gnames, tag):
        smaj = {name: _shard_major(name, [grads[(name, i)] for i in range(2) if (name, i) in grads],
                                   sum((name, i) in grads for i in range(2)))
                for g in gnames for name, _, _ in GROUPS[g][2]}
        gbuf = [_group_rows(smaj, g) for g in gnames]
        gbuf = [b.reshape((4, 2) + b.shape[1:]) for b in gbuf]
        got1 = _exchange_sibling(gbuf, name="rs_sibling" + tag)
        part = [_chip_partial(place, b, r, tr=GROUPS[g][1], name="rs_chip_partial_" + g)
                for g, b, r in zip(gnames, gbuf, got1)]
        return gbuf, got1, part

    early = tuple(g for g in GROUPS if g != "g256")
    gbuf_e, got1_e, part_e = reduce_start(early, "_early")
    dq_e, dk_e, dv_e, got2_e = _sb_bwd(proj_e, lsum, dmix, exchange=part_e)
    dproj_e = jnp.concatenate([du_e, dq_e.astype(BF16), dk_e.astype(BF16), dv_e.astype(BF16)], axis=1)
    dh = _mm(dproj_e, lw("w_in_e", 0), tb=True, tm=1024, tn=512, tk=1024, name="dh_e")
    grads[("w_in_e", 0)] = _mm(h_e, dproj_e, ta=True, tm=1024, tn=512, tk=512, name="dw_in_e")
    dx, rgrads["mix_norm_e"] = _rms_bwd(x0, dh, mix_norm_e, dx, name="rms_mix_e_bwd")
    rgrads["ffn_norm"] = jnp.concatenate([d_ffn0, d_ffn1], axis=0)
    rgrads["ple_norm"] = jnp.concatenate([d_ple0, d_ple1], axis=0)
    rgrads["final_norm"] = d_final.reshape(D_MODEL)

    gbuf_l, got1_l, part_l = reduce_start(("g256",), "_late")
    got2_l = _exchange_chips(part_l)
    wloc, mloc, vloc = pieces(""), pieces("m_"), pieces("v_")
    sh_out = [{}, {}, {}, {}]
    for g, b, r1, r2 in zip(early + ("g256",), gbuf_e + gbuf_l, list(got1_e) + list(got1_l),
                            list(got2_e) + list(got2_l)):
        res = _adamw_shard(place, b, r1, r2, _group_rows(wloc, g), _group_rows(mloc, g), _group_rows(vloc, g),
                           tr=GROUPS[g][1], name="adamw_" + g)
        for kind in range(4):
            sh_out[kind].update(_ungroup_rows(res[kind], g))

    (rparts,) = _all_gather([_pack_repl(rgrads)], name="ag_repl_grads")
    rp_out = _adamw_replicated(rparts, _pack_repl({n: given[n] for n, _, _ in REPL_LAYOUT}),
                               _pack_repl({n: given["m_" + n] for n, _, _ in REPL_LAYOUT}),
                               _pack_repl({n: given["v_" + n] for n, _, _ in REPL_LAYOUT}))
    rp_out = [_unpack_repl(b) for b in rp_out]

    def leaf(kind, name):
        if name in SHARDED:
            return sh_out[kind][name].reshape(given[name].shape)
        return rp_out[kind][name]

    loss = lax.psum(loss_row[0, 0], ("x", "y", "c"))
    outs = [loss, dx[None]]
    for kind in range(4):
        outs += [leaf(kind, n) for n in WEIGHTS]
    return tuple(outs)
```

```python
import functools

import jax
import jax.numpy as jnp
from jax import lax
from jax.experimental import pallas as pl
from jax.experimental.pallas import tpu as pltpu

F32 = jnp.float32
BF16 = jnp.bfloat16

D_MODEL = 1024
PLE_DIM = 256
POOL_WINDOWS = (2, 4, 8, 16)
POOL_WIDTH = 512
SB_HEAD_DIM = 64
SB_BLOCK = 1024
SB_KBLOCK = 256
GDN_HEADS = 8
GDN_HEAD_DIM = 128
GDN_CONV = 4
GDN_CHUNK = 64
FFN_DIM = 2816
FFN_CONV = 3
EPS = 1e-6
ADAM_LR, ADAM_B1, ADAM_B2, ADAM_EPS, ADAM_WD, ADAM_STEP = 0.001, 0.9, 0.999, 1e-08, 0.01, 10
N_DEV = 8
MESH = pl.DeviceIdType.MESH
VMEM_LIMIT = 56 * 1024 * 1024

NN = (((1,), (0,)), ((), ()))
NT = (((1,), (1,)), ((), ()))
TN = (((0,), (0,)), ((), ()))


def _params(*sem):
    return pltpu.CompilerParams(dimension_semantics=sem if sem else None, vmem_limit_bytes=VMEM_LIMIT)


def _dot(a, b, dims):
    return lax.dot_general(a.astype(BF16), b.astype(BF16), dims, preferred_element_type=F32)


def _iota(shape, axis):
    return lax.broadcasted_iota(jnp.int32, shape, axis)


def _mm(a, b, *, ta=False, tb=False, res=None, out_dtype=F32, tm=512, tn=512, tk=None, name):
    M, K = (a.shape[1], a.shape[0]) if ta else a.shape
    N = b.shape[0] if tb else b.shape[1]
    tk = K if tk is None else min(tk, K)
    tm, tn = min(tm, M), min(tn, N)
    assert M % tm == 0 and N % tn == 0 and K % tk == 0, (name, M, N, K, tm, tn, tk)
    nk = K // tk
    dims = (((0 if ta else 1,), (1 if tb else 0,)), ((), ()))

    def body(*refs):
        if res is None:
            a_ref, b_ref, o_ref, *scr = refs
            r_ref = None
        else:
            a_ref, b_ref, r_ref, o_ref, *scr = refs
        p = _dot(a_ref[...], b_ref[...], dims)

        def fin(acc):
            if r_ref is not None:
                acc = acc + r_ref[...]
            o_ref[...] = acc.astype(out_dtype)

        if nk == 1:
            fin(p)
        else:
            acc_ref = scr[0]
            k = pl.program_id(2)

            @pl.when(k == 0)
            def _():
                acc_ref[...] = p

            @pl.when(k > 0)
            def _():
                acc_ref[...] += p

            @pl.when(k == nk - 1)
            def _():
                fin(acc_ref[...])

    a_spec = pl.BlockSpec((tk, tm), lambda i, j, k: (k, i)) if ta else pl.BlockSpec((tm, tk), lambda i, j, k: (i, k))
    b_spec = pl.BlockSpec((tn, tk), lambda i, j, k: (j, k)) if tb else pl.BlockSpec((tk, tn), lambda i, j, k: (k, j))
    o_spec = pl.BlockSpec((tm, tn), lambda i, j, k: (i, j))
    in_specs = [a_spec, b_spec] + ([o_spec] if res is not None else [])
    args = (a, b) + ((res,) if res is not None else ())
    return pl.pallas_call(
        body, name=name, grid=(M // tm, N // tn, nk), in_specs=in_specs, out_specs=o_spec,
        out_shape=jax.ShapeDtypeStruct((M, N), out_dtype),
        scratch_shapes=[pltpu.VMEM((tm, tn), F32)] if nk > 1 else [],
        compiler_params=_params("parallel", "parallel", "arbitrary"),
    )(*args)


def _rms_fwd(x, gain, *, name, tr=512):
    T, Dm = x.shape

    def body(x_ref, g_ref, o_ref):
        xv = x_ref[...]
        r = lax.rsqrt(jnp.mean(xv * xv, axis=-1, keepdims=True) + EPS)
        o_ref[...] = (xv * r * g_ref[...]).astype(BF16)

    return pl.pallas_call(
        body, name=name, grid=(T // tr,),
        in_specs=[pl.BlockSpec((tr, Dm), lambda i: (i, 0)), pl.BlockSpec((1, Dm), lambda i: (0, 0))],
        out_specs=pl.BlockSpec((tr, Dm), lambda i: (i, 0)),
        out_shape=jax.ShapeDtypeStruct((T, Dm), BF16), compiler_params=_params("parallel"),
    )(x, gain)


def _rms_bwd(x, dy, gain, dres, *, name, tr=512):
    T, Dm = x.shape

    def body(x_ref, dy_ref, g_ref, dres_ref, dx_ref, dg_ref):
        i = pl.program_id(0)
        xv = x_ref[...]
        dy_v = dy_ref[...].astype(F32)
        r = lax.rsqrt(jnp.mean(xv * xv, axis=-1, keepdims=True) + EPS)
        xn = xv * r
        dgp = jnp.sum(dy_v * xn, axis=0, keepdims=True)
        dyg = dy_v * g_ref[...]
        dx = r * (dyg - xn * jnp.mean(dyg * xn, axis=-1, keepdims=True))
        dx_ref[...] = dres_ref[...] + dx

        @pl.when(i == 0)
        def _():
            dg_ref[...] = dgp

        @pl.when(i > 0)
        def _():
            dg_ref[...] += dgp

    row = pl.BlockSpec((tr, Dm), lambda i: (i, 0))
    vec = pl.BlockSpec((1, Dm), lambda i: (0, 0))
    return pl.pallas_call(
        body, name=name, grid=(T // tr,), in_specs=[row, row, vec, row], out_specs=[row, vec],
        out_shape=[jax.ShapeDtypeStruct((T, Dm), F32), jax.ShapeDtypeStruct((1, Dm), F32)],
        compiler_params=_params("arbitrary"),
    )(x, dy, gain, dres)


def _final_loss(x, gain, target, *, tr=512):
    T, Dm = x.shape

    def body(x_ref, g_ref, t_ref, loss_ref, dx_ref, dg_ref):
        i = pl.program_id(0)
        xv = x_ref[...]
        g = g_ref[...]
        r = lax.rsqrt(jnp.mean(xv * xv, axis=-1, keepdims=True) + EPS)
        xn = xv * r
        err = xn * g - t_ref[...]
        lp = jnp.zeros((1, 128), F32) + 0.5 * jnp.sum(jnp.mean(err * err, axis=-1, keepdims=True))
        dy_v = err * (1.0 / Dm)
        dgp = jnp.sum(dy_v * xn, axis=0, keepdims=True)
        dyg = dy_v * g
        dx_ref[...] = r * (dyg - xn * jnp.mean(dyg * xn, axis=-1, keepdims=True))

        @pl.when(i == 0)
        def _():
            dg_ref[...] = dgp
            loss_ref[...] = lp

        @pl.when(i > 0)
        def _():
            dg_ref[...] += dgp
            loss_ref[...] += lp

    row = pl.BlockSpec((tr, Dm), lambda i: (i, 0))
    vec = pl.BlockSpec((1, Dm), lambda i: (0, 0))
    return pl.pallas_call(
        body, name="final_loss", grid=(T // tr,), in_specs=[row, vec, row],
        out_specs=[pl.BlockSpec((1, 128), lambda i: (0, 0)), row, vec],
        out_shape=[jax.ShapeDtypeStruct((1, 128), F32), jax.ShapeDtypeStruct((T, Dm), F32),
                   jax.ShapeDtypeStruct((1, Dm), F32)],
        compiler_params=_params("arbitrary"),
    )(x, gain, target)


def _prev_spec(tr, cb, pad, col):
    return pl.BlockSpec((pad, cb), lambda *g: (jnp.maximum(g[0] * (tr // pad) - 1, 0), col(*g)))


def _next_spec(tr, cb, pad, col, T):
    return pl.BlockSpec((pad, cb), lambda *g: (jnp.minimum((g[0] + 1) * (tr // pad), T // pad - 1), col(*g)))


def _conv_rows(x_ext, w_ref, K, pad, cs=slice(None)):
    y = w_ref[K - 1:K, cs] * x_ext
    for i in range(K - 1):
        y = y + w_ref[i:i + 1, cs] * pltpu.roll(x_ext, K - 1 - i, 0)
    return y[pad:]


def _pool_y(u_ext, g, i, tr):
    s = u_ext
    for sh in (1, 2, 4, 8)[:g + 1]:
        s = s + pltpu.roll(s, sh, 0)
    t = i * tr + _iota((tr, 128), 0)
    cnt = jnp.minimum(t + 1, POOL_WINDOWS[g]).astype(F32)
    return s[16:] / cnt - u_ext[16:]


def _pool_fwd(proj, pool_w, pool_scale, *, tr=512):
    T = proj.shape[0]

    def body(u_ref, uh_ref, w_ref, s_ref, o_ref):
        i = pl.program_id(0)
        uh = jnp.where(i > 0, uh_ref[...], 0.0)
        for g in range(4):
            cs = slice(128 * g, 128 * (g + 1))
            y = _pool_y(jnp.concatenate([uh[:, cs], u_ref[:, cs]], axis=0), g, i, tr)
            o_ref[:, cs] = (_dot(y, w_ref[g], NN) * s_ref[:, cs]).astype(BF16)

    return pl.pallas_call(
        body, name="pool_fwd", grid=(T // tr,),
        in_specs=[pl.BlockSpec((tr, 512), lambda i: (i, 0)), _prev_spec(tr, 512, 16, lambda i: 0),
                  pl.BlockSpec((4, 128, 128), lambda i: (0, 0, 0)), pl.BlockSpec((1, 512), lambda i: (0, 0))],
        out_specs=pl.BlockSpec((tr, 512), lambda i: (i, 0)),
        out_shape=jax.ShapeDtypeStruct((T, 512), BF16), compiler_params=_params("parallel"),
    )(proj, proj, pool_w, pool_scale)


def _pool_bwd(proj, dout, pool_w, pool_scale, *, tr=512):
    T = proj.shape[0]
    nb = T // tr

    def body(u_ref, uh_ref, d_ref, dn_ref, w_ref, s_ref, du_ref, dw_ref, ds_ref):
        i = pl.program_id(0)
        uh = jnp.where(i > 0, uh_ref[...], 0.0)
        dn = jnp.where(i < nb - 1, dn_ref[...], 0.0)
        t_ext = i * tr + _iota((tr + 16, 128), 0)
        for g in range(4):
            cs = slice(128 * g, 128 * (g + 1))
            sc = s_ref[:, cs]
            wg = w_ref[g]
            y = _pool_y(jnp.concatenate([uh[:, cs], u_ref[:, cs]], axis=0), g, i, tr)
            dg = d_ref[:, cs]
            dsp = jnp.sum(dg * _dot(y, wg, NN), axis=0, keepdims=True)
            dyw = dg * sc
            dwp = _dot(y, dyw, TN)
            dy_ext = _dot(jnp.concatenate([dyw, dn[:, cs] * sc], axis=0), wg, NT)
            cnt = jnp.minimum(t_ext + 1, POOL_WINDOWS[g]).astype(F32)
            s = dy_ext / cnt
            for sh in (1, 2, 4, 8)[:g + 1]:
                s = s + pltpu.roll(s, tr + 16 - sh, 0)
            du_ref[:, cs] = (s[:tr] - dy_ext[:tr]).astype(BF16)

            @pl.when(i == 0)
            def _():
                dw_ref[g] = dwp
                ds_ref[:, cs] = dsp

            @pl.when(i > 0)
            def _():
                dw_ref[g] += dwp
                ds_ref[:, cs] += dsp

    row = pl.BlockSpec((tr, 512), lambda i: (i, 0))
    return pl.pallas_call(
        body, name="pool_bwd", grid=(nb,),
        in_specs=[row, _prev_spec(tr, 512, 16, lambda i: 0), row, _next_spec(tr, 512, 16, lambda i: 0, T),
                  pl.BlockSpec((4, 128, 128), lambda i: (0, 0, 0)), pl.BlockSpec((1, 512), lambda i: (0, 0))],
        out_specs=[row, pl.BlockSpec((4, 128, 128), lambda i: (0, 0, 0)), pl.BlockSpec((1, 512), lambda i: (0, 0))],
        out_shape=[jax.ShapeDtypeStruct((T, 512), BF16), jax.ShapeDtypeStruct((4, 128, 128), F32),
                   jax.ShapeDtypeStruct((1, 512), F32)],
        compiler_params=_params("arbitrary"),
    )(proj, proj, dout, dout, pool_w, pool_scale)


def _split_dot(x, tri):
    hi = x.astype(BF16)
    lo = (x - hi.astype(F32)).astype(BF16)
    return (lax.dot_general(hi, tri, NN, preferred_element_type=F32)
            + lax.dot_general(lo, tri, NN, preferred_element_type=F32))


def _log1m(z):
    return -(jnp.maximum(z, 0.0) + jnp.log(1.0 + jnp.exp(-jnp.abs(z))))


def _sb_fwd(proj, gather=()):
    T = proj.shape[0]
    B, BK = min(SB_BLOCK, T), SB_KBLOCK
    R = B // BK
    nq, n = T // B, len(gather)
    scale = SB_HEAD_DIM ** -0.5

    def body(q_ref, k_ref, v_ref, *rest):
        o_ref, ls_ref = rest[n:n + 2]
        hp, i = pl.program_id(0), pl.program_id(1)
        if n:
            start, forward, finish = _gather_stages(rest[:n], rest[n + 2:2 * n + 2], *rest[2 * n + 2:])
            pl.when((hp == 0) & (i == 0))(start)
            pl.when((hp == 3) & (i == nq - 1))(forward)
        lane = _iota((1, 128), 1)
        tri_gt = (_iota((BK, BK), 0) > _iota((BK, BK), 1)).astype(BF16)
        row, col = _iota((B, BK), 0), _iota((B, BK), 1)
        qv = q_ref[...] * scale
        hms = [(lane >= 64 * h) & (lane < 64 * (h + 1)) for h in range(2)]
        qhs = [jnp.where(hm, qv, 0.0).astype(BF16) for hm in hms]

        def tile(j, carry, d):
            rows = pl.ds(pl.multiple_of(j * BK, BK), BK)
            kj = k_ref[rows, :].astype(BF16)
            vj = v_ref[rows, :].astype(BF16)
            r0 = 0 if d is None else BK * d
            valid = None if d is None else (col[r0:] < row[:B - r0])
            out = []
            for h in range(2):
                c, acc = carry[h]
                z = lax.dot_general(qhs[h][r0:], kj, NT, preferred_element_type=F32)
                lg = _log1m(z)
                if d is not None:
                    lg = jnp.where(valid, lg, 0.0)
                a = jnp.exp(z + lg + _split_dot(lg, tri_gt) + c[r0:])
                if d is not None:
                    a = jnp.where(valid, a, 0.0)
                upd = (c[r0:] + jnp.sum(lg, axis=1, keepdims=True),
                       acc[r0:] + lax.dot_general(a.astype(BF16), vj, NN, preferred_element_type=F32))
                out.append(upd if r0 == 0 else tuple(jnp.concatenate([old[:r0], new], axis=0)
                                                     for old, new in zip((c, acc), upd)))
            return tuple(out)

        zero = (jnp.zeros((B, 1), F32), jnp.zeros((B, 128), F32))
        carry = (zero, zero)
        for d in reversed(range(R)):
            carry = tile(i * R + d, carry, d)
        carry = lax.fori_loop(0, i * R, lambda s, cr: tile(i * R - 1 - s, cr, None), carry)
        o_ref[...] = jnp.where(hms[0], carry[0][1], carry[1][1])
        ls_ref[...] = jnp.where(hms[0], carry[0][0], carry[1][0])
        if n:
            pl.when((hp == 3) & (i == nq - 1))(finish)

    blk = pl.BlockSpec((B, 128), lambda hp, i: (i, hp))
    out = pl.pallas_call(
        body, name="sb_fwd", grid=(4, nq),
        in_specs=[pl.BlockSpec((B, 128), lambda hp, i: (i, 4 + hp)),
                  pl.BlockSpec((T, 128), lambda hp, i: (0, 8 + hp)),
                  pl.BlockSpec((T, 128), lambda hp, i: (0, 12 + hp))] + [HBM_SPEC] * n,
        out_specs=[blk, blk] + [HBM_SPEC] * n,
        out_shape=[jax.ShapeDtypeStruct((T, 512), F32)] * 2 + _gather_shapes(gather),
        scratch_shapes=_gather_sems(n) if n else [],
        compiler_params=_params("arbitrary", "arbitrary"),
    )(proj, proj, proj, *gather)
    return out[0], out[1], list(out[2:])


def _sb_bwd(proj, lsum, dout, exchange=()):
    T = proj.shape[0]
    B, BK = min(SB_BLOCK, T), SB_KBLOCK
    R = B // BK
    nq, n = T // B, len(exchange)
    scale = SB_HEAD_DIM ** -0.5

    def body(q_ref, k_ref, v_ref, do_ref, ls_ref, *rest):
        dq_ref, dk_ref, dv_ref = rest[n:n + 3]
        hp, i = pl.program_id(0), pl.program_id(1)
        if n:
            start, finish = _chips_stages(rest[:n], rest[n + 3:2 * n + 3], *rest[2 * n + 3:])
            pl.when((hp == 0) & (i == 0))(start)

        @pl.when(i == 0)
        def _():
            dk_ref[...] = jnp.zeros_like(dk_ref)
            dv_ref[...] = jnp.zeros_like(dv_ref)

        lane = _iota((1, 128), 1)
        tri_le = (_iota((BK, BK), 0) <= _iota((BK, BK), 1)).astype(BF16)
        tri_lt = (_iota((BK, BK), 0) < _iota((BK, BK), 1)).astype(BF16)
        row, col = _iota((B, BK), 0), _iota((B, BK), 1)
        qv = q_ref[...] * scale
        dov = do_ref[...]
        hms = [(lane >= 64 * h) & (lane < 64 * (h + 1)) for h in range(2)]
        qhs = [jnp.where(hm, qv, 0.0).astype(BF16) for hm in hms]
        dos = [jnp.where(hm, dov, 0.0).astype(BF16) for hm in hms]
        ltots = [ls_ref[:, 64 * h:64 * h + 1] for h in range(2)]

        def tile(j, carry, d):
            rows = pl.ds(pl.multiple_of(j * BK, BK), BK)
            kj = k_ref[rows, :].astype(BF16)
            vj = v_ref[rows, :].astype(BF16)
            diag = d is not None
            r0 = BK * d if diag else 0
            valid = (col[r0:] < row[:B - r0]) if diag else None
            out = []
            dkj = jnp.zeros((BK, 128), F32)
            dvj = jnp.zeros((BK, 128), F32)
            for h in range(2):
                lbef, ebef, dqa = carry[h]
                qh, do_h = qhs[h][r0:], dos[h][r0:]
                z = lax.dot_general(qh, kj, NT, preferred_element_type=F32)
                lg = _log1m(z)
                if diag:
                    lg = jnp.where(valid, lg, 0.0)
                a = jnp.exp(z + lg + (ltots[h][r0:] - lbef[r0:] - _split_dot(lg, tri_le)))
                if diag:
                    a = jnp.where(valid, a, 0.0)
                e = a * lax.dot_general(do_h, vj, NT, preferred_element_type=F32)
                dz = e * jnp.exp(lg) - jnp.exp(z + lg) * (ebef[r0:] + _split_dot(e, tri_lt))
                if diag:
                    dz = jnp.where(valid, dz, 0.0)
                dzb = dz.astype(BF16)
                dkj = dkj + lax.dot_general(dzb, qh, TN, preferred_element_type=F32)
                dvj = dvj + lax.dot_general(a.astype(BF16), do_h, TN, preferred_element_type=F32)
                upd = (lbef[r0:] + jnp.sum(lg, axis=1, keepdims=True), ebef[r0:] + jnp.sum(e, axis=1, keepdims=True),
                       dqa[r0:] + lax.dot_general(dzb, kj, NN, preferred_element_type=F32))
                out.append(upd if r0 == 0 else tuple(jnp.concatenate([old[:r0], new], axis=0)
                                                     for old, new in zip(carry[h], upd)))
            dk_ref[rows, :] += dkj
            dv_ref[rows, :] += dvj
            return tuple(out)

        zero = (jnp.zeros((B, 1), F32), jnp.zeros((B, 1), F32), jnp.zeros((B, 128), F32))
        carry = lax.fori_loop(0, i * R, lambda j, cr: tile(j, cr, None), (zero, zero))
        for d in range(R):
            carry = tile(i * R + d, carry, d)
        dq_ref[...] = jnp.where(hms[0], carry[0][2], carry[1][2]) * scale
        if n:
            pl.when((hp == 3) & (i == nq - 1))(finish)

    full = pl.BlockSpec((T, 128), lambda hp, i: (0, hp))
    blk = pl.BlockSpec((B, 128), lambda hp, i: (i, hp))
    out = pl.pallas_call(
        body, name="sb_bwd", grid=(4, nq),
        in_specs=[pl.BlockSpec((B, 128), lambda hp, i: (i, 4 + hp)),
                  pl.BlockSpec((T, 128), lambda hp, i: (0, 8 + hp)),
                  pl.BlockSpec((T, 128), lambda hp, i: (0, 12 + hp)),
                  pl.BlockSpec((B, 128), lambda hp, i: (i, 4 + hp)), blk] + [HBM_SPEC] * n,
        out_specs=[blk, full, full] + [HBM_SPEC] * n,
        out_shape=[jax.ShapeDtypeStruct((T, 512), F32)] * 3 + _chips_shapes(exchange),
        scratch_shapes=_chips_sems(n) if n else [],
        compiler_params=_params("arbitrary", "arbitrary"),
    )(proj, proj, proj, dout, lsum, *exchange)
    return out[0], out[1], out[2], list(out[3:])


def _sigmoid(x):
    return 1.0 / (1.0 + jnp.exp(-x))


def _silu_mul(cg, cv):
    return cg * _sigmoid(cg) * cv


def _ffn_act(up, conv_w, *, tr=512, cb=256):
    T, F2 = up.shape
    nc = F2 // 2 // cb
    K = FFN_CONV

    def body(g_ref, gh_ref, v_ref, vh_ref, wg_ref, wv_ref, o_ref):
        i = pl.program_id(0)
        gh = jnp.where(i > 0, gh_ref[...], 0.0)
        vh = jnp.where(i > 0, vh_ref[...], 0.0)
        cg = _conv_rows(jnp.concatenate([gh, g_ref[...]], axis=0), wg_ref, K, 8)
        cv = _conv_rows(jnp.concatenate([vh, v_ref[...]], axis=0), wv_ref, K, 8)
        o_ref[...] = _silu_mul(cg, cv).astype(BF16)

    return pl.pallas_call(
        body, name="ffn_act", grid=(T // tr, nc),
        in_specs=[pl.BlockSpec((tr, cb), lambda i, j: (i, j)), _prev_spec(tr, cb, 8, lambda i, j: j),
                  pl.BlockSpec((tr, cb), lambda i, j: (i, nc + j)), _prev_spec(tr, cb, 8, lambda i, j: nc + j),
                  pl.BlockSpec((K, cb), lambda i, j: (0, j)), pl.BlockSpec((K, cb), lambda i, j: (0, nc + j))],
        out_specs=pl.BlockSpec((tr, cb), lambda i, j: (i, j)),
        out_shape=jax.ShapeDtypeStruct((T, F2 // 2), BF16), compiler_params=_params("parallel", "parallel"),
    )(up, up, up, up, conv_w, conv_w)


def _conv_bwd_rows(dc_ext, x_ext, w_ref, K, tr, cs=slice(None)):
    n = tr + 8
    dx = w_ref[K - 1:K, cs] * dc_ext
    for i in range(K - 1):
        dx = dx + w_ref[i:i + 1, cs] * pltpu.roll(dc_ext, n - (K - 1 - i), 0)
    dc = dc_ext[:tr]
    dws = [jnp.sum(dc * pltpu.roll(x_ext, K - 1 - i, 0)[8:8 + tr], axis=0, keepdims=True) for i in range(K)]
    return dx[:tr], dws


def _acc_rows(ref, rows, first, cs=slice(None)):
    for i, r in enumerate(rows):
        @pl.when(first)
        def _():
            ref[i:i + 1, cs] = r

        @pl.when(jnp.logical_not(first))
        def _():
            ref[i:i + 1, cs] += r


def _ffn_act_bwd(up, conv_w, dact, *, tr=512, cb=256):
    T, F2 = up.shape
    F = F2 // 2
    nc, nb = F // cb, T // tr
    K = FFN_CONV

    def body(g_ref, gp_ref, gn_ref, v_ref, vp_ref, vn_ref, d_ref, dn_ref, wg_ref, wv_ref,
             dg_ref, dv_ref, dwg_ref, dwv_ref):
        i = pl.program_id(1)
        first, last = i == 0, i == nb - 1
        g_ext = jnp.concatenate([jnp.where(first, 0.0, gp_ref[...]), g_ref[...], jnp.where(last, 0.0, gn_ref[...])], axis=0)
        v_ext = jnp.concatenate([jnp.where(first, 0.0, vp_ref[...]), v_ref[...], jnp.where(last, 0.0, vn_ref[...])], axis=0)
        d_ext = jnp.concatenate([d_ref[...], jnp.where(last, 0.0, dn_ref[...])], axis=0)
        cg = _conv_rows(g_ext, wg_ref, K, 8)
        cv = _conv_rows(v_ext, wv_ref, K, 8)
        _, vjp = jax.vjp(_silu_mul, cg, cv)
        dcg, dcv = vjp(d_ext)
        dg, dwg = _conv_bwd_rows(dcg, g_ext, wg_ref, K, tr)
        dv, dwv = _conv_bwd_rows(dcv, v_ext, wv_ref, K, tr)
        dg_ref[...] = dg.astype(BF16)
        dv_ref[...] = dv.astype(BF16)
        _acc_rows(dwg_ref, dwg, first)
        _acc_rows(dwv_ref, dwv, first)

    blk = lambda off: pl.BlockSpec((tr, cb), lambda j, i: (i, off + j))
    prev = lambda off: pl.BlockSpec((8, cb), lambda j, i: (jnp.maximum(i * (tr // 8) - 1, 0), off + j))
    nxt = lambda off: pl.BlockSpec((8, cb), lambda j, i: (jnp.minimum((i + 1) * (tr // 8), T // 8 - 1), off + j))
    wsp = lambda off: pl.BlockSpec((K, cb), lambda j, i: (0, off + j))
    return pl.pallas_call(
        body, name="ffn_act_bwd", grid=(nc, nb),
        in_specs=[blk(0), prev(0), nxt(0), blk(nc), prev(nc), nxt(nc), blk(0), nxt(0), wsp(0), wsp(nc)],
        out_specs=[blk(0), blk(0), wsp(0), wsp(0)],
        out_shape=[jax.ShapeDtypeStruct((T, F), BF16)] * 2 + [jax.ShapeDtypeStruct((K, F), F32)] * 2,
        compiler_params=_params("parallel", "arbitrary"),
    )(up, up, up, up, up, up, dact, dact, conv_w, conv_w)


def _ple_fwd(hn, w_gate, p, w_ple, x, *, name, tm=512, tn=512):
    T, Dm = x.shape

    def body(a_ref, b_ref, p_ref, wp_ref, x_ref, o_ref, gl_ref, pe_ref):
        gl = _dot(a_ref[...], b_ref[...], NN)
        pe = _dot(p_ref[...], wp_ref[...], NN)
        gl_ref[...] = gl
        pe_ref[...] = pe
        o_ref[...] = x_ref[...] + pe * _sigmoid(gl)

    o_spec = pl.BlockSpec((tm, tn), lambda i, j: (i, j))
    return pl.pallas_call(
        body, name=name, grid=(T // tm, Dm // tn),
        in_specs=[pl.BlockSpec((tm, Dm), lambda i, j: (i, 0)), pl.BlockSpec((Dm, tn), lambda i, j: (0, j)),
                  pl.BlockSpec((tm, PLE_DIM), lambda i, j: (i, 0)), pl.BlockSpec((PLE_DIM, tn), lambda i, j: (0, j)),
                  o_spec],
        out_specs=[o_spec] * 3, out_shape=[jax.ShapeDtypeStruct((T, Dm), F32)] * 3,
        compiler_params=_params("parallel", "parallel"),
    )(hn, w_gate, p, w_ple, x)


def _ple_bwd(dx, gl, pe, *, name, tr=512):
    T, Dm = dx.shape

    def body(dx_ref, gl_ref, pe_ref, dpe_ref, dgl_ref):
        g = _sigmoid(gl_ref[...])
        d = dx_ref[...]
        dpe_ref[...] = (d * g).astype(BF16)
        dgl_ref[...] = (d * pe_ref[...] * g * (1.0 - g)).astype(BF16)

    row = pl.BlockSpec((tr, Dm), lambda i: (i, 0))
    return pl.pallas_call(
        body, name=name, grid=(T // tr,), in_specs=[row] * 3, out_specs=[row] * 2,
        out_shape=[jax.ShapeDtypeStruct((T, Dm), BF16)] * 2, compiler_params=_params("parallel"),
    )(dx, gl, pe)


def _qkv_act(c, cb):
    s = c * _sigmoid(c)
    n = s * lax.rsqrt(jnp.sum(s * s, axis=-1, keepdims=True) + EPS)
    n = n * jnp.where(cb < GDN_HEADS, GDN_HEAD_DIM ** -0.5, 1.0)
    return jnp.where(cb < 2 * GDN_HEADS, n, s)


GDN_HPS = 4


def _gdn_pre(proj, conv_w, *, tr=512):
    T = proj.shape[0]
    K = GDN_CONV

    def body(x_ref, xh_ref, w_ref, o_ref):
        i, j = pl.program_id(0), pl.program_id(1)
        xh = jnp.where(i > 0, xh_ref[...], 0.0)
        for hh in range(GDN_HPS):
            cs = slice(128 * hh, 128 * (hh + 1))
            c = _conv_rows(jnp.concatenate([xh[:, cs], x_ref[:, cs]], axis=0), w_ref, K, 8, cs)
            o_ref[hh] = _qkv_act(c, GDN_HPS * j + hh)

    wide = 128 * GDN_HPS
    return pl.pallas_call(
        body, name="gdn_pre", grid=(T // tr, 24 // GDN_HPS),
        in_specs=[pl.BlockSpec((tr, wide), lambda i, j: (i, j)), _prev_spec(tr, wide, 8, lambda i, j: j),
                  pl.BlockSpec((K, wide), lambda i, j: (0, j))],
        out_specs=pl.BlockSpec((GDN_HPS, tr, 128), lambda i, j: (j, i, 0)),
        out_shape=jax.ShapeDtypeStruct((24, T, 128), F32), compiler_params=_params("parallel", "parallel"),
    )(proj, proj, conv_w)


def _gdn_pre_bwd(proj, conv_w, dqkv, *, tr=512):
    T = proj.shape[0]
    nb = T // tr
    K = GDN_CONV

    def body(x_ref, xp_ref, xn_ref, d_ref, dn_ref, w_ref, dx_ref, dw_ref):
        j, i = pl.program_id(0), pl.program_id(1)
        first, last = i == 0, i == nb - 1
        xp = jnp.where(first, 0.0, xp_ref[...])
        xn = jnp.where(last, 0.0, xn_ref[...])
        for hh in range(GDN_HPS):
            cs = slice(128 * hh, 128 * (hh + 1))
            x_ext = jnp.concatenate([xp[:, cs], x_ref[:, cs], xn[:, cs]], axis=0)
            d_ext = jnp.concatenate([d_ref[hh], jnp.where(last, 0.0, dn_ref[hh])], axis=0)
            c = _conv_rows(x_ext, w_ref, K, 8, cs)
            _, vjp = jax.vjp(lambda c_: _qkv_act(c_, GDN_HPS * j + hh), c)
            (dc,) = vjp(d_ext)
            dx, dws = _conv_bwd_rows(dc, x_ext, w_ref, K, tr, cs)
            dx_ref[:, cs] = dx.astype(BF16)
            _acc_rows(dw_ref, dws, first, cs)

    wide = 128 * GDN_HPS
    return pl.pallas_call(
        body, name="gdn_pre_bwd", grid=(24 // GDN_HPS, nb),
        in_specs=[pl.BlockSpec((tr, wide), lambda j, i: (i, j)),
                  pl.BlockSpec((8, wide), lambda j, i: (jnp.maximum(i * (tr // 8) - 1, 0), j)),
                  pl.BlockSpec((8, wide), lambda j, i: (jnp.minimum((i + 1) * (tr // 8), T // 8 - 1), j)),
                  pl.BlockSpec((GDN_HPS, tr, 128), lambda j, i: (j, i, 0)),
                  pl.BlockSpec((GDN_HPS, 8, 128), lambda j, i: (j, jnp.minimum((i + 1) * (tr // 8), T // 8 - 1), 0)),
                  pl.BlockSpec((K, wide), lambda j, i: (0, j))],
        out_specs=[pl.BlockSpec((tr, wide), lambda j, i: (i, j)), pl.BlockSpec((K, wide), lambda j, i: (0, j))],
        out_shape=[jax.ShapeDtypeStruct((T, 24 * 128), BF16), jax.ShapeDtypeStruct((K, 24 * 128), F32)],
        compiler_params=_params("parallel", "arbitrary"),
    )(proj, proj, proj, dqkv, dqkv, conv_w)


def _gate_fn(ba, alog_row, dt_row):
    lane = _iota((1, 128), 1)
    x = ba + dt_row
    sp = jnp.maximum(x, 0.0) + jnp.log(1.0 + jnp.exp(-jnp.abs(x)))
    return jnp.where(lane < GDN_HEADS, _sigmoid(ba), -jnp.exp(alog_row) * sp)


def _gdn_gate(ba, alog_row, dt_row, *, tr=512):
    T = ba.shape[0]

    def body(ba_ref, al_ref, dt_ref, b_ref, g_ref):
        val = _gate_fn(ba_ref[...], al_ref[...], dt_ref[...])
        for h in range(GDN_HEADS):
            b_ref[h] = val[:, h:h + 1]
            g_ref[h] = val[:, GDN_HEADS + h:GDN_HEADS + h + 1]

    vec = pl.BlockSpec((1, 128), lambda i: (0, 0))
    hm = pl.BlockSpec((GDN_HEADS, tr, 1), lambda i: (0, i, 0))
    return pl.pallas_call(
        body, name="gdn_gate", grid=(T // tr,), in_specs=[pl.BlockSpec((tr, 128), lambda i: (i, 0)), vec, vec],
        out_specs=[hm, hm], out_shape=[jax.ShapeDtypeStruct((GDN_HEADS, T, 1), F32)] * 2,
        compiler_params=_params("parallel"),
    )(ba, alog_row, dt_row)


def _gdn_gate_bwd(ba, alog_row, dt_row, dbeta, dg, *, tr=512):
    T = ba.shape[0]

    def body(ba_ref, al_ref, dt_ref, db_ref, dg_ref, dba_ref, dal_ref, ddt_ref):
        i = pl.program_id(0)
        lane = _iota((1, 128), 1)
        d = jnp.zeros((tr, 128), F32)
        for h in range(GDN_HEADS):
            d = d + jnp.where(lane == h, db_ref[h], 0.0) + jnp.where(lane == GDN_HEADS + h, dg_ref[h], 0.0)
        _, vjp = jax.vjp(_gate_fn, ba_ref[...], al_ref[...], dt_ref[...])
        dba, dal, ddt = vjp(d)
        dba_ref[...] = dba.astype(BF16)

        @pl.when(i == 0)
        def _():
            dal_ref[...] = dal
            ddt_ref[...] = ddt

        @pl.when(i > 0)
        def _():
            dal_ref[...] += dal
            ddt_ref[...] += ddt

    vec = pl.BlockSpec((1, 128), lambda i: (0, 0))
    hm = pl.BlockSpec((GDN_HEADS, tr, 1), lambda i: (0, i, 0))
    row = pl.BlockSpec((tr, 128), lambda i: (i, 0))
    return pl.pallas_call(
        body, name="gdn_gate_bwd", grid=(T // tr,), in_specs=[row, vec, vec, hm, hm], out_specs=[row, vec, vec],
        out_shape=[jax.ShapeDtypeStruct((T, 128), BF16), jax.ShapeDtypeStruct((1, 128), F32),
                   jax.ShapeDtypeStruct((1, 128), F32)],
        compiler_params=_params("arbitrary"),
    )(ba, alog_row, dt_row, dbeta, dg)


def _split3(x):
    x1 = x.astype(BF16)
    r = x - x1.astype(F32)
    x2 = r.astype(BF16)
    return x1, x2, (r - x2.astype(F32)).astype(BF16)


def _dot01(tri, x, dims):
    t = tri.astype(BF16)
    x1, x2, x3 = _split3(x)
    d = lambda xi: lax.dot_general(t, xi, dims, preferred_element_type=F32)
    return d(x1) + (d(x2) + d(x3))


def _dot3(a, b, dims):
    ah, al, _ = _split3(a)
    bh, bl, _ = _split3(b)
    d = lambda p, q: lax.dot_general(p, q, dims, preferred_element_type=F32)
    return d(ah, bh) + (d(ah, bl) + d(al, bh))


BNN = (((2,), (1,)), ((0,), (0,)))
BNT = (((2,), (2,)), ((0,), (0,)))
BTN = (((1,), (1,)), ((0,), (0,)))


@jax.custom_vjp
def _mm01(tri, x):
    return _dot01(tri, x, BNN)


def _mm01_fwd(tri, x):
    return _dot01(tri, x, BNN), tri


def _mm01_bwd(tri, ct):
    return jnp.zeros_like(tri), _dot01(tri, ct, BTN)


_mm01.defvjp(_mm01_fwd, _mm01_bwd)


def _unit_lower_inverse(a):
    C = a.shape[-1]
    eye = (_iota(a.shape, 1) == _iota(a.shape, 2)).astype(F32)
    pw = -a
    tinv = eye + pw
    for _ in range(5):
        pw = _dot3(pw, pw, BNN)
        tinv = tinv + _dot3(tinv, pw, BNN)
    return tinv


@jax.custom_vjp
def _unit_lower_solve(a, rv, rw):
    return _unit_lower_solve_fwd(a, rv, rw)[0]


def _unit_lower_solve_fwd(a, rv, rw):
    tinv = _unit_lower_inverse(a)
    sol = _dot3(tinv, jnp.concatenate([rv, rw], axis=2), BNN)
    n = rv.shape[2]
    return (sol[:, :, :n], sol[:, :, n:]), (tinv, sol)


def _unit_lower_solve_bwd(res, cts):
    tinv, sol = res
    n = cts[0].shape[2]
    d_rhs = _dot3(tinv, jnp.concatenate(cts, axis=2), BTN)
    return -_dot3(d_rhs, sol, BNT), d_rhs[:, :, :n], d_rhs[:, :, n:]


_unit_lower_solve.defvjp(_unit_lower_solve_fwd, _unit_lower_solve_bwd)


@jax.custom_vjp
def _mmb_nt(a, b):
    return _dot(a, b, BNT)


def _mmb_nt_fwd(a, b):
    return _dot(a, b, BNT), (a, b)


def _mmb_nt_bwd(res, ct):
    a, b = res
    return _dot(ct, b, BNN), _dot(ct, a, BTN)


_mmb_nt.defvjp(_mmb_nt_fwd, _mmb_nt_bwd)


def _gdn_chunk(q, k, v, gcol, bcol):
    nb, C = q.shape[0], GDN_CHUNK
    row, col = _iota((nb, C, C), 1), _iota((nb, C, C), 2)
    incl, strict = row >= col, row > col
    eye = (row == col).astype(F32)
    lower = incl.astype(F32)
    ones = jnp.ones((nb, C, C), F32)
    gwide = jnp.broadcast_to(gcol, (nb, C, GDN_HEAD_DIM))
    gc = _mm01(lower, gwide)
    gtot = _mm01(ones, gwide)
    gc_c = _mm01(lower, jnp.broadcast_to(gcol, (nb, C, C)))
    gc_s = _mm01(ones, gc_c * eye)
    decay = jnp.where(incl, jnp.exp(jnp.where(incl, gc_c - gc_s, 0.0)), 0.0)
    kb = k * bcol
    a = jnp.where(strict, _mmb_nt(kb, k) * decay, 0.0)
    egc = jnp.exp(gc)
    u, w = _unit_lower_solve(a, v * bcol, kb * egc)
    qk = jnp.where(incl, _mmb_nt(q, k) * decay, 0.0)
    return u, w, qk, q * egc, k * jnp.exp(gtot - gc), jnp.exp(jnp.sum(gwide, axis=1))


GDN_ROWS = 8 * GDN_CHUNK


def _gdn_specs(T):
    hd = lambda off: pl.BlockSpec((1, GDN_ROWS, 128), lambda h, i: (off + h, i, 0))
    col = pl.BlockSpec((1, GDN_ROWS, 1), lambda h, i: (h, i, 0))
    sq = pl.BlockSpec((1, GDN_ROWS, GDN_CHUNK), lambda h, i: (h, i, 0))
    gl = pl.BlockSpec((1, 8, 128), lambda h, i: (h, i, 0))
    return hd, col, sq, gl


def _gdn_local(qkv, g, beta):
    T = qkv.shape[1]
    hd, col, sq, gl_spec = _gdn_specs(T)

    def body(q_ref, k_ref, v_ref, g_ref, b_ref, u_ref, w_ref, qk_ref, qd_ref, kd_ref, gl_ref):
        chunks = lambda ref: ref[0].reshape(8, GDN_CHUNK, ref.shape[2])
        rows = lambda val: val.reshape(GDN_ROWS, val.shape[2])
        u, w, qk, qd, kd, gl = _gdn_chunk(chunks(q_ref), chunks(k_ref), chunks(v_ref), chunks(g_ref), chunks(b_ref))
        u_ref[0] = rows(u)
        w_ref[0] = rows(w).astype(BF16)
        qk_ref[0] = rows(qk).astype(BF16)
        qd_ref[0] = rows(qd).astype(BF16)
        kd_ref[0] = rows(kd).astype(BF16)
        gl_ref[0] = gl

    H = GDN_HEADS
    return pl.pallas_call(
        body, name="gdn_local", grid=(H, T // GDN_ROWS),
        in_specs=[hd(0), hd(H), hd(2 * H), col, col],
        out_specs=[hd(0), hd(0), sq, hd(0), hd(0), gl_spec],
        out_shape=[jax.ShapeDtypeStruct((H, T, 128), F32), jax.ShapeDtypeStruct((H, T, 128), BF16),
                   jax.ShapeDtypeStruct((H, T, GDN_CHUNK), BF16), jax.ShapeDtypeStruct((H, T, 128), BF16),
                   jax.ShapeDtypeStruct((H, T, 128), BF16), jax.ShapeDtypeStruct((H, T // GDN_CHUNK, 128), F32)],
        compiler_params=_params("parallel", "parallel"),
    )(qkv, qkv, qkv, g, beta)


def _gdn_local_bwd(qkv, g, beta, du, dw, dqk, dqd, dkd, dgl):
    T = qkv.shape[1]
    hd, col, sq, gl_spec = _gdn_specs(T)

    def body(q_ref, k_ref, v_ref, g_ref, b_ref, du_ref, dw_ref, dqk_ref, dqd_ref, dkd_ref, dgl_ref,
             dqkv_ref, dg_ref, db_ref):
        chunks = lambda ref: ref[0].reshape(8, GDN_CHUNK, ref.shape[2])
        rows = lambda val: val.reshape(GDN_ROWS, val.shape[2])
        _, vjp = jax.vjp(_gdn_chunk, chunks(q_ref), chunks(k_ref), chunks(v_ref), chunks(g_ref), chunks(b_ref))
        dq, dk, dv, dg, db = vjp((chunks(du_ref), chunks(dw_ref), chunks(dqk_ref), chunks(dqd_ref), chunks(dkd_ref),
                                  dgl_ref[0]))
        dqkv_ref[0, 0] = rows(dq)
        dqkv_ref[1, 0] = rows(dk)
        dqkv_ref[2, 0] = rows(dv)
        dg_ref[0] = rows(dg)
        db_ref[0] = rows(db)

    H = GDN_HEADS
    small = jax.ShapeDtypeStruct((H, T, 1), F32)
    dqkv, dg, db = pl.pallas_call(
        body, name="gdn_local_bwd", grid=(H, T // GDN_ROWS),
        in_specs=[hd(0), hd(H), hd(2 * H), col, col, hd(0), hd(0), sq, hd(0), hd(0), gl_spec],
        out_specs=[pl.BlockSpec((3, 1, GDN_ROWS, 128), lambda h, i: (0, h, i, 0)), col, col],
        out_shape=[jax.ShapeDtypeStruct((3, H, T, 128), F32), small, small],
        compiler_params=_params("parallel", "parallel"),
    )(qkv, qkv, qkv, g, beta, du, dw, dqk, dqd, dkd, dgl)
    return dqkv.reshape(3 * H, T, 128), dg, db


GDN_HB = 4


def _gdn_scan_specs(T, rev):
    nb = T // GDN_ROWS
    blk = (lambda i: nb - 1 - i) if rev else (lambda i: i)
    hd = pl.BlockSpec((GDN_HB, GDN_ROWS, 128), lambda h, i: (h, blk(i), 0))
    sq = pl.BlockSpec((GDN_HB, GDN_ROWS, GDN_CHUNK), lambda h, i: (h, blk(i), 0))
    gl = pl.BlockSpec((GDN_HB, 8, 128), lambda h, i: (h, blk(i), 0))
    st = pl.BlockSpec((GDN_HB, 8, 128, 128), lambda h, i: (h, blk(i), 0, 0))
    return hd, sq, gl, st


def _gdn_scan(u, w, qk, qd, kd, gl):
    H, T, _ = u.shape
    hd, sq, gl_spec, st = _gdn_scan_specs(T, False)

    def body(u_ref, w_ref, qk_ref, qd_ref, kd_ref, gl_ref, o_ref, ss_ref, vn_ref, s_scr):
        @pl.when(pl.program_id(1) == 0)
        def _():
            s_scr[...] = jnp.zeros_like(s_scr)

        dot = lambda a, b, dims: lax.dot_general(a, b, dims, preferred_element_type=F32)
        s = s_scr[...]
        for c in range(8):
            rs = slice(GDN_CHUNK * c, GDN_CHUNK * (c + 1))
            ss_ref[:, c] = s
            sb = s.astype(BF16)
            vn = u_ref[:, rs, :] - dot(w_ref[:, rs, :], sb, BNN)
            vnb = vn.astype(BF16)
            o_ref[:, rs, :] = dot(qd_ref[:, rs, :], sb, BNN) + dot(qk_ref[:, rs, :], vnb, BNN)
            vn_ref[:, rs, :] = vnb
            s = s * gl_ref[:, c:c + 1, :] + dot(kd_ref[:, rs, :], vnb, BTN)
        s_scr[...] = s

    return pl.pallas_call(
        body, name="gdn_scan", grid=(H // GDN_HB, T // GDN_ROWS),
        in_specs=[hd, hd, sq, hd, hd, gl_spec], out_specs=[hd, st, hd],
        out_shape=[jax.ShapeDtypeStruct((H, T, 128), F32), jax.ShapeDtypeStruct((H, T // GDN_CHUNK, 128, 128), F32),
                   jax.ShapeDtypeStruct((H, T, 128), BF16)],
        scratch_shapes=[pltpu.VMEM((GDN_HB, 128, 128), F32)],
        compiler_params=_params("parallel", "arbitrary"),
    )(u, w, qk, qd, kd, gl)


def _gdn_scan_bwd(do, ss, vn, w, qk, qd, kd, gl):
    H, T, _ = do.shape
    hd, sq, gl_spec, st = _gdn_scan_specs(T, True)

    def body(do_ref, ss_ref, vn_ref, w_ref, qk_ref, qd_ref, kd_ref, gl_ref,
             du_ref, dw_ref, dqk_ref, dqd_ref, dkd_ref, dgl_ref, ds_scr):
        @pl.when(pl.program_id(1) == 0)
        def _():
            ds_scr[...] = jnp.zeros_like(ds_scr)

        dot = lambda a, b, dims: lax.dot_general(a, b, dims, preferred_element_type=F32)
        ds = ds_scr[...]
        for c in reversed(range(8)):
            rs = slice(GDN_CHUNK * c, GDN_CHUNK * (c + 1))
            s = ss_ref[:, c]
            sb, dsb = s.astype(BF16), ds.astype(BF16)
            dob = do_ref[:, rs, :].astype(BF16)
            vnb = vn_ref[:, rs, :]
            dvn = dot(qk_ref[:, rs, :], dob, BTN) + dot(kd_ref[:, rs, :], dsb, BNN)
            dvnb = dvn.astype(BF16)
            du_ref[:, rs, :] = dvn
            dw_ref[:, rs, :] = -dot(dvnb, sb, BNT)
            dqk_ref[:, rs, :] = dot(dob, vnb, BNT)
            dqd_ref[:, rs, :] = dot(dob, sb, BNT)
            dkd_ref[:, rs, :] = dot(vnb, dsb, BNT)
            dgl_ref[:, c:c + 1, :] = jnp.sum(ds * s, axis=1, keepdims=True)
            ds = dot(qd_ref[:, rs, :], dob, BTN) + ds * gl_ref[:, c:c + 1, :] - dot(w_ref[:, rs, :], dvnb, BTN)
        ds_scr[...] = ds

    big = jax.ShapeDtypeStruct((H, T, 128), F32)
    return pl.pallas_call(
        body, name="gdn_scan_bwd", grid=(H // GDN_HB, T // GDN_ROWS),
        in_specs=[hd, st, hd, hd, sq, hd, hd, gl_spec], out_specs=[hd, hd, sq, hd, hd, gl_spec],
        out_shape=[big, big, jax.ShapeDtypeStruct((H, T, GDN_CHUNK), F32), big, big,
                   jax.ShapeDtypeStruct((H, T // GDN_CHUNK, 128), F32)],
        scratch_shapes=[pltpu.VMEM((GDN_HB, 128, 128), F32)],
        compiler_params=_params("parallel", "arbitrary"),
    )(do, ss, vn, w, qk, qd, kd, gl)


def _gated_norm(o, z, nw):
    on = o * lax.rsqrt(jnp.mean(o * o, axis=-1, keepdims=True) + EPS) * nw
    return on * (z * _sigmoid(z))


def _gdn_post(o, proj, norm_w, *, tr=512):
    T = proj.shape[0]

    def body(o_ref, z_ref, n_ref, y_ref):
        y_ref[...] = _gated_norm(o_ref[0], z_ref[...], n_ref[...]).astype(BF16)

    return pl.pallas_call(
        body, name="gdn_post", grid=(T // tr, GDN_HEADS),
        in_specs=[pl.BlockSpec((1, tr, 128), lambda i, h: (h, i, 0)), pl.BlockSpec((tr, 128), lambda i, h: (i, 24 + h)),
                  pl.BlockSpec((1, 128), lambda i, h: (0, 0))],
        out_specs=pl.BlockSpec((tr, 128), lambda i, h: (i, h)),
        out_shape=jax.ShapeDtypeStruct((T, 1024), BF16), compiler_params=_params("parallel", "parallel"),
    )(o, proj, norm_w)


def _gdn_post_bwd(o, proj, norm_w, dy, *, tr=512):
    T = proj.shape[0]

    def body(o_ref, z_ref, n_ref, dy_ref, do_ref, dz_ref, dn_ref):
        first = (pl.program_id(0) == 0) & (pl.program_id(1) == 0)
        _, vjp = jax.vjp(_gated_norm, o_ref[0], z_ref[...], n_ref[...])
        do, dz, dn = vjp(dy_ref[...])
        do_ref[0] = do
        dz_ref[...] = dz.astype(BF16)

        @pl.when(first)
        def _():
            dn_ref[...] = dn

        @pl.when(jnp.logical_not(first))
        def _():
            dn_ref[...] += dn

    blk = pl.BlockSpec((tr, 128), lambda i, h: (i, h))
    hm = pl.BlockSpec((1, tr, 128), lambda i, h: (h, i, 0))
    vec = pl.BlockSpec((1, 128), lambda i, h: (0, 0))
    return pl.pallas_call(
        body, name="gdn_post_bwd", grid=(T // tr, GDN_HEADS),
        in_specs=[hm, pl.BlockSpec((tr, 128), lambda i, h: (i, 24 + h)), vec, blk], out_specs=[hm, blk, vec],
        out_shape=[jax.ShapeDtypeStruct((GDN_HEADS, T, 128), F32), jax.ShapeDtypeStruct((T, 1024), BF16),
                   jax.ShapeDtypeStruct((1, 128), F32)],
        compiler_params=_params("arbitrary", "arbitrary"),
    )(o, proj, norm_w, dy)


HBM_SPEC = pl.BlockSpec(memory_space=pltpu.HBM)


def _place():
    return lax.axis_index("x"), lax.axis_index("y"), lax.axis_index("c")


def _all_gather(vs, *, name):
    n = len(vs)

    def body(*refs):
        start, forward, finish = _gather_stages(refs[:n], refs[n:2 * n], *refs[2 * n:])
        start()
        forward()
        finish()

    return pl.pallas_call(
        body, name=name, out_shape=_gather_shapes(vs), in_specs=[HBM_SPEC] * n, out_specs=[HBM_SPEC] * n,
        scratch_shapes=_gather_sems(n),
    )(*vs)


def _gather_shapes(vs):
    return [jax.ShapeDtypeStruct((N_DEV,) + v.shape, v.dtype) for v in vs]


def _gather_sems(n):
    return [pltpu.SemaphoreType.DMA((7 * n,)), pltpu.SemaphoreType.DMA((7 * n,)), pltpu.SemaphoreType.DMA((n,))]


def _gather_stages(v_refs, out_refs, send_sems, recv_sems, local_sems):
    n = len(v_refs)
    x, y, c = _place()
    me, sibling = (x, y, c), (x, y, 1 - c)
    chips = [(1 - x, y), (x, 1 - y), (1 - x, 1 - y)]

    def copy(a, k, block, to, from_input=False):
        slot = out_refs[a].at[4 * block[0] + 2 * block[1] + block[2]]
        return pltpu.make_async_remote_copy(
            src_ref=v_refs[a] if from_input else slot, dst_ref=slot,
            send_sem=send_sems.at[7 * a + k], recv_sem=recv_sems.at[7 * a + k], device_id=to, device_id_type=MESH)

    def mine():
        return [pltpu.make_async_copy(v_refs[a], out_refs[a].at[4 * x + 2 * y + c], local_sems.at[a]) for a in range(n)]

    def first():
        return ([copy(a, 0, me, sibling, True) for a in range(n)]
                + [copy(a, 1 + j, me, (*chip, c), True) for j, chip in enumerate(chips) for a in range(n)])

    def passed():
        return [copy(a, 4 + j, (*chip, c), sibling) for j, chip in enumerate(chips) for a in range(n)]

    def start():
        for cp in mine() + first():
            cp.start()

    def forward():
        for j, chip in enumerate(chips):
            for a in range(n):
                copy(a, 1 + j, (*chip, c), me).wait_recv()
                copy(a, 4 + j, (*chip, c), sibling).start()

    def finish():
        for a in range(n):
            copy(a, 0, sibling, me).wait_recv()
            for j, chip in enumerate(chips):
                copy(a, 4 + j, (*chip, 1 - c), me).wait_recv()
        for cp in first() + passed():
            cp.wait_send()
        for cp in mine():
            cp.wait()

    return start, forward, finish


def _exchange_sibling(gs, *, name):
    n = len(gs)

    def body(*refs):
        g_refs, out_refs = refs[:n], refs[n:2 * n]
        send_sems, recv_sems = refs[2 * n:]
        x, y, c = _place()
        copies = [pltpu.make_async_remote_copy(
            src_ref=g_refs[a].at[k, 1 - c], dst_ref=out_refs[a].at[k], send_sem=send_sems.at[4 * a + k],
            recv_sem=recv_sems.at[4 * a + k], device_id=(x, y, 1 - c), device_id_type=MESH)
            for a in range(n) for k in range(4)]
        for cp in copies:
            cp.start()
        for cp in copies:
            cp.wait()

    return pl.pallas_call(
        body, name=name, out_shape=[jax.ShapeDtypeStruct((4,) + g.shape[2:], g.dtype) for g in gs],
        in_specs=[HBM_SPEC] * n, out_specs=[HBM_SPEC] * n,
        scratch_shapes=[pltpu.SemaphoreType.DMA((4 * n,)), pltpu.SemaphoreType.DMA((4 * n,))],
    )(*gs)


def _exchange_chips(pcs):
    n = len(pcs)

    def body(*refs):
        start, finish = _chips_stages(refs[:n], refs[n:2 * n], *refs[2 * n:])
        start()
        finish()

    return pl.pallas_call(
        body, name="rs_chips", out_shape=_chips_shapes(pcs), in_specs=[HBM_SPEC] * n, out_specs=[HBM_SPEC] * n,
        scratch_shapes=_chips_sems(n),
    )(*pcs)


def _chips_shapes(pcs):
    return [jax.ShapeDtypeStruct((3,) + pc.shape[1:], pc.dtype) for pc in pcs]


def _chips_sems(n):
    return [pltpu.SemaphoreType.DMA((3 * n,)), pltpu.SemaphoreType.DMA((3 * n,))]


def _chips_stages(p_refs, out_refs, send_sems, recv_sems):
    n = len(p_refs)
    x, y, c = _place()
    chips = [(1 - x, y), (x, 1 - y), (1 - x, 1 - y)]

    def copies():
        return [pltpu.make_async_remote_copy(
            src_ref=p_refs[a].at[2 * cx + cy], dst_ref=out_refs[a].at[j], send_sem=send_sems.at[3 * a + j],
            recv_sem=recv_sems.at[3 * a + j], device_id=(cx, cy, c), device_id_type=MESH)
            for j, (cx, cy) in enumerate(chips) for a in range(n)]

    def start():
        for cp in copies():
            cp.start()

    def finish():
        for cp in copies():
            cp.wait()

    return start, finish


def _chip_partial(place, g, got, *, tr, name):
    R, W = g.shape[2:]

    def body(pl_ref, g_ref, r_ref, o_ref):
        o_ref[...] = (g_ref[0] + r_ref[...]).astype(BF16)

    return pl.pallas_call(
        body, name=name, out_shape=jax.ShapeDtypeStruct((4, R, W), BF16),
        grid_spec=pltpu.PrefetchScalarGridSpec(
            num_scalar_prefetch=1, grid=(4, R // tr),
            in_specs=[pl.BlockSpec((1, 1, tr, W), lambda k, i, pr: (k, pr[2], i, 0)),
                      pl.BlockSpec((1, tr, W), lambda k, i, pr: (k, i, 0))],
            out_specs=pl.BlockSpec((1, tr, W), lambda k, i, pr: (k, i, 0))),
        compiler_params=_params("parallel", "parallel"),
    )(place, g, got)


def _adamw_math(g, w, m, v):
    m = ADAM_B1 * m + (1.0 - ADAM_B1) * g
    v = ADAM_B2 * v + (1.0 - ADAM_B2) * (g * g)
    m_hat = m / (1.0 - ADAM_B1 ** ADAM_STEP)
    v_hat = v / (1.0 - ADAM_B2 ** ADAM_STEP)
    return -ADAM_LR * (m_hat / (jnp.sqrt(v_hat) + ADAM_EPS) + ADAM_WD * w), m, v


def _adamw_shard(place, g, got1, got2, w, m, v, *, tr, name):
    R, W = w.shape

    def body(pl_ref, g_ref, r1_ref, r2_ref, w_ref, m_ref, v_ref, go_ref, d_ref, mo_ref, vo_ref):
        gs = g_ref[0, 0] + r1_ref[0]
        for j in range(3):
            gs = gs + r2_ref[j].astype(F32)
        go_ref[...] = gs
        d_ref[...], mo_ref[...], vo_ref[...] = _adamw_math(gs, w_ref[...], m_ref[...], v_ref[...])

    row = pl.BlockSpec((tr, W), lambda i, pr: (i, 0))
    out = jax.ShapeDtypeStruct((R, W), F32)
    return pl.pallas_call(
        body, name=name, out_shape=[out] * 4,
        grid_spec=pltpu.PrefetchScalarGridSpec(
            num_scalar_prefetch=1, grid=(R // tr,),
            in_specs=[pl.BlockSpec((1, 1, tr, W), lambda i, pr: (2 * pr[0] + pr[1], pr[2], i, 0)),
                      pl.BlockSpec((1, tr, W), lambda i, pr: (2 * pr[0] + pr[1], i, 0)),
                      pl.BlockSpec((3, tr, W), lambda i, pr: (0, i, 0)), row, row, row],
            out_specs=[row] * 4),
        compiler_params=_params("parallel"),
    )(place, g, got1, got2, w, m, v)


def _adamw_replicated(parts, w, m, v):
    R, W = w.shape

    def body(p_ref, w_ref, m_ref, v_ref, go_ref, d_ref, mo_ref, vo_ref):
        gs = p_ref[0]
        for j in range(1, N_DEV):
            gs = gs + p_ref[j]
        go_ref[...] = gs
        d_ref[...], mo_ref[...], vo_ref[...] = _adamw_math(gs, w_ref[...], m_ref[...], v_ref[...])

    full = pl.BlockSpec((R, W), lambda i: (0, 0))
    out = jax.ShapeDtypeStruct((R, W), F32)
    return pl.pallas_call(
        body, name="adamw_replicated", grid=(1,), out_shape=[out] * 4,
        in_specs=[pl.BlockSpec((N_DEV, R, W), lambda i: (0, 0, 0)), full, full, full], out_specs=[full] * 4,
        compiler_params=_params("arbitrary"),
    )(parts, w, m, v)


GROUPS = {
    "in_e": (256, 512, ((("w_in_e", None), 1024, 1024),)),
    "in_o": (514, 512, ((("w_in_o", None), 1024, 1024),)),
    "up0": (704, 256, ((("w_up", 0), 1024, 1024),)),
    "up1": (704, 256, ((("w_up", 1), 1024, 1024),)),
    "down0": (1024, 176, ((("w_down", 0), 352, 352),)),
    "down1": (1024, 176, ((("w_down", 1), 352, 352),)),
    "square": (1024, 256, ((("w_out_e", None), 128, 128), (("w_out_o", None), 128, 128), (("w_ple_gate", None), 256, 256))),
    "ple": (128, 512, ((("w_ple", None), 512, 512),)),
    "conv_f": (704, 8, ((("ffn_conv", None), 6, 8),)),
    "norm_o": (128, 8, ((("mix_norm_o", None), 1, 8),)),
    "conv_o": (384, 8, ((("conv_qkv_o", None), 4, 8),)),
}
SHARDED = tuple(dict.fromkeys(p[0][0] for g in GROUPS.values() for p in g[2]))
COLUMN_SHARDED = ("w_in_e", "w_in_o", "w_up", "ffn_conv", "w_ple", "conv_qkv_o", "mix_norm_o")
PACK_W = 1024
REPL_LAYOUT = (
    ("mix_norm_e", (1, 1024), 8), ("pool_w", (1, 4, 128, 128), 64), ("pool_scale", (1, 512), 8),
    ("a_log_o", (1, 8), 8), ("dt_bias_o", (1, 8), 8), ("gdn_norm_o", (1, 128), 8),
    ("ffn_norm", (2, 1024), 8), ("ple_norm", (2, 1024), 8), ("final_norm", (1024,), 8),
)


def _pad_rows(a, rows):
    extra = rows - a.shape[-2]
    return a if extra == 0 else jnp.pad(a, [(0, 0)] * (a.ndim - 2) + [(0, extra), (0, 0)])


def _group_rows(pieces, gname):
    parts = [_pad_rows(pieces[name], padded) for name, _, padded in GROUPS[gname][2]]
    return parts[0] if len(parts) == 1 else jnp.concatenate(parts, axis=-2)


def _ungroup_rows(buf, gname):
    out, r0 = {}, 0
    for name, rows, padded in GROUPS[gname][2]:
        out[name] = buf[..., r0:r0 + rows, :]
        r0 += padded
    return out


def _shard_major(name, gfull, n_layers):
    per_layer = []
    for g in gfull:
        if g.ndim == 3:
            per_layer.append(g)
        elif name in COLUMN_SHARDED:
            k = g.shape[0]
            per_layer.append(jnp.moveaxis(g.reshape(k, N_DEV, g.shape[1] // N_DEV), 1, 0))
        else:
            per_layer.append(g.reshape(N_DEV, g.shape[0] // N_DEV, -1))
    return per_layer[0] if n_layers == 1 else jnp.concatenate(per_layer, axis=1)


def _natural(name, gathered, n_layers):
    rows = gathered.shape[1] // n_layers
    out = []
    for layer in range(n_layers):
        piece = gathered[:, layer * rows:(layer + 1) * rows]
        if name in COLUMN_SHARDED:
            out.append(jnp.moveaxis(piece, 0, 1).reshape(rows, N_DEV * piece.shape[2]))
        else:
            out.append(piece.reshape(N_DEV * rows, piece.shape[2]))
    return out


def _rows(a, rows):
    flat = a.reshape(-1)
    return jnp.pad(flat, (0, rows * PACK_W - flat.shape[0])).reshape(rows, PACK_W)


def _pack_repl(vals):
    return jnp.concatenate([_rows(vals[name].reshape(shape), rows) for name, shape, rows in REPL_LAYOUT], axis=0)


def _unpack_repl(buf):
    out, r0 = {}, 0
    for name, shape, rows in REPL_LAYOUT:
        n = 1
        for s in shape:
            n *= s
        out[name] = buf[r0:r0 + rows].reshape(-1)[:n].reshape(shape)
        r0 += rows
    return out


WEIGHTS = ("mix_norm_e", "w_in_e", "pool_w", "pool_scale", "w_out_e", "mix_norm_o", "w_in_o", "conv_qkv_o", "a_log_o",
           "dt_bias_o", "gdn_norm_o", "w_out_o", "ffn_norm", "w_up", "ffn_conv", "w_down", "ple_norm", "w_ple_gate",
           "w_ple", "final_norm")


def _ffn_forward(x, norm_g, w_up, conv_w, w_down, tag):
    hn = _rms_fwd(x, norm_g, name="rms_ffn" + tag)
    up = _mm(hn, w_up, tm=1024, tn=512, name="ffn_up" + tag)
    act = _ffn_act(up, conv_w)
    out = _mm(act, w_down, res=x, tm=1024, tn=512, tk=1408, name="ffn_down" + tag)
    return out, (x, hn, up, act)


def _ffn_backward(dx, saved, norm_g, w_up, conv_w, w_down, tag):
    x, hn, up, act = saved
    dact = _mm(dx, w_down, tb=True, tm=512, tn=1408, name="ffn_dact" + tag)
    d_w_down = _mm(act, dx, ta=True, tm=1408, tn=512, tk=512, name="ffn_dwdown" + tag)
    dgate, dval, dcg, dcv = _ffn_act_bwd(up, conv_w, dact)
    dhn = _mm(dgate, w_up[:, :FFN_DIM], tb=True, tm=1024, tn=512, tk=1408, name="ffn_dhn_g" + tag)
    dhn = _mm(dval, w_up[:, FFN_DIM:], tb=True, res=dhn, tm=1024, tn=512, tk=1408, name="ffn_dhn_v" + tag)
    halves = [_mm(hn, d, ta=True, tm=1024, tn=1408, tk=512, name="ffn_dwup_" + side + tag)
              for side, d in (("g", dgate), ("v", dval))]
    d_w_up = jnp.concatenate([jnp.moveaxis(h.reshape(h.shape[0], 4, -1), 1, 0) for h in halves], axis=0)
    dx, d_norm = _rms_bwd(x, dhn, norm_g, dx, name="rms_ffn_bwd" + tag)
    return dx, d_norm, d_w_up, jnp.concatenate([dcg, dcv], axis=1), d_w_down


def _ple_forward(x, norm_g, w_gate, p, w_ple, tag):
    hn = _rms_fwd(x, norm_g, name="rms_ple" + tag)
    out, gl, pe = _ple_fwd(hn, w_gate, p, w_ple, x, name="ple_fwd" + tag)
    return out, (x, hn, gl, pe)


def _ple_backward(dx, saved, norm_g, w_gate, p, tag):
    x, hn, gl, pe = saved
    dpe, dgl = _ple_bwd(dx, gl, pe, name="ple_bwd" + tag)
    d_w_ple = _mm(p, dpe, ta=True, tm=256, tn=512, tk=512, name="ple_dwple" + tag)
    d_w_gate = _mm(hn, dgl, ta=True, tm=1024, tn=512, tk=512, name="ple_dwgate" + tag)
    dhn = _mm(dgl, w_gate, tb=True, tm=1024, tn=512, name="ple_dhn" + tag)
    dx, d_norm = _rms_bwd(x, dhn, norm_g, dx, name="rms_ple_bwd" + tag)
    return dx, d_norm, d_w_gate, d_w_ple


def kernel(x, p, mix_norm_e, w_in_e, pool_w, pool_scale, w_out_e, mix_norm_o, w_in_o, conv_qkv_o, a_log_o, dt_bias_o, gdn_norm_o, w_out_o, ffn_norm, w_up, ffn_conv, w_down, ple_norm, w_ple_gate, w_ple, final_norm, loss_target, m_mix_norm_e, m_w_in_e, m_pool_w, m_pool_scale, m_w_out_e, m_mix_norm_o, m_w_in_o, m_conv_qkv_o, m_a_log_o, m_dt_bias_o, m_gdn_norm_o, m_w_out_o, m_ffn_norm, m_w_up, m_ffn_conv, m_w_down, m_ple_norm, m_w_ple_gate, m_w_ple, m_final_norm, v_mix_norm_e, v_w_in_e, v_pool_w, v_pool_scale, v_w_out_e, v_mix_norm_o, v_w_in_o, v_conv_qkv_o, v_a_log_o, v_dt_bias_o, v_gdn_norm_o, v_w_out_o, v_ffn_norm, v_w_up, v_ffn_conv, v_w_down, v_ple_norm, v_w_ple_gate, v_w_ple, v_final_norm):
    given = dict(locals())
    place = jnp.stack(_place()).astype(jnp.int32)
    x0, tgt = x[0], loss_target[0]

    def pieces(prefix):
        out = {}
        for width, _, members in GROUPS.values():
            for (name, layer), rows, _ in members:
                a = given[prefix + name]
                out[(name, layer)] = (a if layer is None else a[layer]).reshape(rows, width)
        return out

    def flat2d(name):
        return given[name].reshape(-1, given[name].shape[-1])

    small = ("ffn_conv", "mix_norm_o", "conv_qkv_o")
    got = _all_gather([flat2d("w_in_e").astype(BF16)] + [_pad_rows(flat2d(k), 8) for k in small], name="ag_first")
    full = {("w_in_e", 0): _natural("w_in_e", got[0], 1)[0]}
    for i in range(2):
        full[("ffn_conv", i)] = _natural("ffn_conv", got[1][:, 3 * i:3 * i + 3], 1)[0]
    mix_norm_o_full = got[2][:, 0].reshape(1, D_MODEL)
    conv_qkv = _natural("conv_qkv_o", got[3][:, :4], 1)[0]
    alog_row = jnp.pad(a_log_o, ((0, 0), (8, 112)))
    dt_row = jnp.pad(dt_bias_o, ((0, 0), (8, 112)))
    lw = lambda name, i: full[(name, i)]

    h_e = _rms_fwd(x0, mix_norm_e, name="rms_mix_e")
    proj_e = _mm(h_e, lw("w_in_e", 0), tm=1024, tn=512, name="in_e")
    pool_o = _pool_fwd(proj_e, pool_w[0], pool_scale)
    wide = ("w_out_e", "w_out_o", "w_down", "w_ple_gate")
    send = [jnp.concatenate([flat2d(k).astype(BF16) for k in wide], axis=0)]
    att_o, lsum, got = _sb_fwd(proj_e, gather=send + [flat2d(k).astype(BF16) for k in ("w_in_o", "w_up", "w_ple")])
    gathered, r0 = {"w_in_o": got[1], "w_up": got[2], "w_ple": got[3]}, 0
    for k in wide:
        gathered[k] = got[0][:, r0:r0 + flat2d(k).shape[0]]
        r0 += flat2d(k).shape[0]
    layers = {name: given[name].shape[0] if given[name].ndim == 3 else 1 for name in gathered}
    full.update({(name, i): w for name in gathered for i, w in enumerate(_natural(name, gathered[name], layers[name]))})
    w_in_o_full = full[("w_in_o", 0)]
    w_in_o_main = w_in_o_full[:, :4096]
    w_in_o_ba = jnp.pad(w_in_o_full[:, 4096:], ((0, 0), (0, 112)))
    mix_e = jnp.concatenate([pool_o, att_o.astype(BF16)], axis=1)
    x1 = _mm(mix_e, lw("w_out_e", 0), res=x0, tm=1024, tn=512, name="out_e")
    x2, ffn0 = _ffn_forward(x1, ffn_norm[0:1], lw("w_up", 0), lw("ffn_conv", 0), lw("w_down", 0), "0")
    x3, ple0 = _ple_forward(x2, ple_norm[0:1], lw("w_ple_gate", 0), p[0, 0], lw("w_ple", 0), "0")

    h_o = _rms_fwd(x3, mix_norm_o_full, name="rms_mix_o")
    proj_o = _mm(h_o, w_in_o_main, tm=1024, tn=512, name="in_o")
    ba = _mm(h_o, w_in_o_ba, tm=1024, tn=128, name="in_o_ba")
    qkv = _gdn_pre(proj_o, conv_qkv)
    beta, g = _gdn_gate(ba, alog_row, dt_row)
    u, w_c, qk, qd, kd, gl = _gdn_local(qkv, g, beta)
    o, states, vnew = _gdn_scan(u, w_c, qk, qd, kd, gl)
    y_o = _gdn_post(o, proj_o, gdn_norm_o)
    x4 = _mm(y_o, lw("w_out_o", 0), res=x3, tm=1024, tn=512, name="out_o")
    x5, ffn1 = _ffn_forward(x4, ffn_norm[1:2], lw("w_up", 1), lw("ffn_conv", 1), lw("w_down", 1), "1")
    x6, ple1 = _ple_forward(x5, ple_norm[1:2], lw("w_ple_gate", 1), p[1, 0], lw("w_ple", 1), "1")
    loss_row, dx, d_final = _final_loss(x6, final_norm.reshape(1, D_MODEL), tgt)

    grads, rgrads = {}, {}
    dx, d_ple1, grads[("w_ple_gate", 1)], grads[("w_ple", 1)] = _ple_backward(dx, ple1, ple_norm[1:2], lw("w_ple_gate", 1), p[1, 0], "1")
    dx, d_ffn1, grads[("w_up", 1)], grads[("ffn_conv", 1)], grads[("w_down", 1)] = _ffn_backward(
        dx, ffn1, ffn_norm[1:2], lw("w_up", 1), lw("ffn_conv", 1), lw("w_down", 1), "1")
    grads[("w_out_o", 0)] = _mm(y_o, dx, ta=True, tm=1024, tn=512, tk=512, name="dw_out_o")
    dy_o = _mm(dx, lw("w_out_o", 0), tb=True, tm=1024, tn=512, name="dy_o")
    do, dz, rgrads["gdn_norm_o"] = _gdn_post_bwd(o, proj_o, gdn_norm_o, dy_o)
    du, dw_c, dqk, dqd, dkd, dgl = _gdn_scan_bwd(do, states, vnew, w_c, qk, qd, kd, gl)
    dqkv_heads, dg, dbeta = _gdn_local_bwd(qkv, g, beta, du, dw_c, dqk, dqd, dkd, dgl)
    dqkv, grads[("conv_qkv_o", 0)] = _gdn_pre_bwd(proj_o, conv_qkv, dqkv_heads)
    dba, d_alog, d_dt = _gdn_gate_bwd(ba, alog_row, dt_row, dbeta, dg)
    rgrads["a_log_o"], rgrads["dt_bias_o"] = d_alog[:, 8:16], d_dt[:, 8:16]
    dproj_o = jnp.concatenate([dqkv, dz], axis=1)
    dh = _mm(dproj_o, w_in_o_main, tb=True, tm=1024, tn=512, tk=1024, name="dh_o")
    dh = _mm(dba, w_in_o_ba, tb=True, res=dh, tm=1024, tn=512, name="dh_o_ba")
    grads[("w_in_o", 0)] = jnp.concatenate(
        [_mm(h_o, dproj_o, ta=True, tm=1024, tn=512, tk=512, name="dw_in_o"),
         _mm(h_o, dba, ta=True, tm=1024, tn=128, tk=512, name="dw_in_o_ba")[:, :16]], axis=1)
    dx, d_mix_o = _rms_bwd(x3, dh, mix_norm_o_full, dx, name="rms_mix_o_bwd")
    grads[("mix_norm_o", 0)] = d_mix_o

    dx, d_ple0, grads[("w_ple_gate", 0)], grads[("w_ple", 0)] = _ple_backward(dx, ple0, ple_norm[0:1], lw("w_ple_gate", 0), p[0, 0], "0")
    dx, d_ffn0, grads[("w_up", 0)], grads[("ffn_conv", 0)], grads[("w_down", 0)] = _ffn_backward(
        dx, ffn0, ffn_norm[0:1], lw("w_up", 0), lw("ffn_conv", 0), lw("w_down", 0), "0")
    grads[("w_out_e", 0)] = _mm(mix_e, dx, ta=True, tm=1024, tn=512, tk=512, name="dw_out_e")
    dmix = _mm(dx, lw("w_out_e", 0), tb=True, tm=1024, tn=512, name="dmix_e")
    du_e, d_pool_w, rgrads["pool_scale"] = _pool_bwd(proj_e, dmix, pool_w[0], pool_scale)
    rgrads["pool_w"] = d_pool_w[None]

    def reduce_start(gnames, tag):
        smaj = {}
        for g in gnames:
            for (name, layer), _, _ in GROUPS[g][2]:
                of = [grads[(name, i)] for i in ((0, 1) if layer is None else (layer,)) if (name, i) in grads]
                smaj[(name, layer)] = _shard_major(name, of, len(of))
        gbuf = [_group_rows(smaj, g) for g in gnames]
        gbuf = [b.reshape((4, 2) + b.shape[1:]) for b in gbuf]
        got1 = _exchange_sibling(gbuf, name="rs_sibling" + tag)
        part = [_chip_partial(place, b, r, tr=GROUPS[g][1], name="rs_chip_partial_" + g)
                for g, b, r in zip(gnames, gbuf, got1)]
        return gbuf, got1, part

    early = tuple(g for g in GROUPS if g != "in_e")
    gbuf_e, got1_e, part_e = reduce_start(early, "_early")
    dq_e, dk_e, dv_e, got2_e = _sb_bwd(proj_e, lsum, dmix, exchange=part_e)
    dproj_e = jnp.concatenate([du_e, dq_e.astype(BF16), dk_e.astype(BF16), dv_e.astype(BF16)], axis=1)
    dh = _mm(dproj_e, lw("w_in_e", 0), tb=True, tm=1024, tn=512, tk=1024, name="dh_e")
    grads[("w_in_e", 0)] = _mm(h_e, dproj_e, ta=True, tm=1024, tn=512, tk=512, name="dw_in_e")
    dx, rgrads["mix_norm_e"] = _rms_bwd(x0, dh, mix_norm_e, dx, name="rms_mix_e_bwd")
    rgrads["ffn_norm"] = jnp.concatenate([d_ffn0, d_ffn1], axis=0)
    rgrads["ple_norm"] = jnp.concatenate([d_ple0, d_ple1], axis=0)
    rgrads["final_norm"] = d_final.reshape(D_MODEL)

    gbuf_l, got1_l, part_l = reduce_start(("in_e",), "_late")
    got2_l = _exchange_chips(part_l)
    wloc, mloc, vloc = pieces(""), pieces("m_"), pieces("v_")
    sh_out = [{}, {}, {}, {}]
    for g, b, r1, r2 in zip(early + ("in_e",), gbuf_e + gbuf_l, list(got1_e) + list(got1_l),
                            list(got2_e) + list(got2_l)):
        res = _adamw_shard(place, b, r1, r2, _group_rows(wloc, g), _group_rows(mloc, g), _group_rows(vloc, g),
                           tr=GROUPS[g][1], name="adamw_" + g)
        for kind in range(4):
            sh_out[kind].update(_ungroup_rows(res[kind], g))

    (rparts,) = _all_gather([_pack_repl(rgrads)], name="ag_repl_grads")
    rp_out = _adamw_replicated(rparts, _pack_repl({n: given[n] for n, _, _ in REPL_LAYOUT}),
                               _pack_repl({n: given["m_" + n] for n, _, _ in REPL_LAYOUT}),
                               _pack_repl({n: given["v_" + n] for n, _, _ in REPL_LAYOUT}))
    rp_out = [_unpack_repl(b) for b in rp_out]

    def leaf(kind, name):
        if name in SHARDED:
            mine = sh_out[kind]
            whole = mine[(name, None)] if (name, None) in mine else jnp.stack([mine[(name, 0)], mine[(name, 1)]])
            return whole.reshape(given[name].shape)
        return rp_out[kind][name]

    loss = lax.psum(loss_row[0, 0], ("x", "y", "c"))
    outs = [loss, dx[None]]
    for kind in range(4):
        outs += [leaf(kind, n) for n in WEIGHTS]
    return tuple(outs)
```

```python
import functools

import jax
import jax.numpy as jnp
from jax import lax
from jax.experimental import pallas as pl
from jax.experimental.pallas import tpu as pltpu

F32 = jnp.float32
BF16 = jnp.bfloat16

D_MODEL = 1024
PLE_DIM = 256
POOL_WINDOWS = (2, 4, 8, 16)
POOL_WIDTH = 512
SB_HEAD_DIM = 64
SB_BLOCK = 1024
SB_KBLOCK = 256
GDN_HEADS = 8
GDN_HEAD_DIM = 128
GDN_CONV = 4
GDN_CHUNK = 64
FFN_DIM = 2816
FFN_CONV = 3
EPS = 1e-6
ADAM_LR, ADAM_B1, ADAM_B2, ADAM_EPS, ADAM_WD, ADAM_STEP = 0.001, 0.9, 0.999, 1e-08, 0.01, 10
N_DEV = 8
MESH = pl.DeviceIdType.MESH
VMEM_LIMIT = 56 * 1024 * 1024

NN = (((1,), (0,)), ((), ()))
NT = (((1,), (1,)), ((), ()))
TN = (((0,), (0,)), ((), ()))


def _params(*sem):
    return pltpu.CompilerParams(dimension_semantics=sem if sem else None, vmem_limit_bytes=VMEM_LIMIT)


def _dot(a, b, dims):
    return lax.dot_general(a.astype(BF16), b.astype(BF16), dims, preferred_element_type=F32)


def _iota(shape, axis):
    return lax.broadcasted_iota(jnp.int32, shape, axis)


MM_TILE, MM_TILE_11 = 1024, 1408


def _mm_tile(dim):
    if dim <= MM_TILE_11:
        return dim
    return MM_TILE if dim % MM_TILE == 0 else MM_TILE_11


def _mm(a, b, *, ta=False, tb=False, res=None, out_dtype=F32, name):
    M, K = (a.shape[1], a.shape[0]) if ta else a.shape
    N = b.shape[0] if tb else b.shape[1]
    tm, tn, tk = _mm_tile(M), _mm_tile(N), _mm_tile(K)
    assert M % tm == 0 and N % tn == 0 and K % tk == 0, (name, M, N, K, tm, tn, tk)
    nk = K // tk
    dims = (((0 if ta else 1,), (1 if tb else 0,)), ((), ()))

    def body(*refs):
        if res is None:
            a_ref, b_ref, o_ref, *scr = refs
            r_ref = None
        else:
            a_ref, b_ref, r_ref, o_ref, *scr = refs
        p = _dot(a_ref[...], b_ref[...], dims)

        def fin(acc):
            if r_ref is not None:
                acc = acc + r_ref[...]
            o_ref[...] = acc.astype(out_dtype)

        if nk == 1:
            fin(p)
        else:
            acc_ref = scr[0]
            k = pl.program_id(2)

            @pl.when(k == 0)
            def _():
                acc_ref[...] = p

            @pl.when(k > 0)
            def _():
                acc_ref[...] += p

            @pl.when(k == nk - 1)
            def _():
                fin(acc_ref[...])

    a_spec = pl.BlockSpec((tk, tm), lambda i, j, k: (k, i)) if ta else pl.BlockSpec((tm, tk), lambda i, j, k: (i, k))
    b_spec = pl.BlockSpec((tn, tk), lambda i, j, k: (j, k)) if tb else pl.BlockSpec((tk, tn), lambda i, j, k: (k, j))
    o_spec = pl.BlockSpec((tm, tn), lambda i, j, k: (i, j))
    in_specs = [a_spec, b_spec] + ([o_spec] if res is not None else [])
    args = (a, b) + ((res,) if res is not None else ())
    return pl.pallas_call(
        body, name=name, grid=(M // tm, N // tn, nk), in_specs=in_specs, out_specs=o_spec,
        out_shape=jax.ShapeDtypeStruct((M, N), out_dtype),
        scratch_shapes=[pltpu.VMEM((tm, tn), F32)] if nk > 1 else [],
        compiler_params=_params("parallel", "parallel", "arbitrary"),
    )(*args)


def _rms_fwd(x, gain, *, name, tr=512):
    T, Dm = x.shape

    def body(x_ref, g_ref, o_ref):
        xv = x_ref[...]
        r = lax.rsqrt(jnp.mean(xv * xv, axis=-1, keepdims=True) + EPS)
        o_ref[...] = (xv * r * g_ref[...]).astype(BF16)

    return pl.pallas_call(
        body, name=name, grid=(T // tr,),
        in_specs=[pl.BlockSpec((tr, Dm), lambda i: (i, 0)), pl.BlockSpec((1, Dm), lambda i: (0, 0))],
        out_specs=pl.BlockSpec((tr, Dm), lambda i: (i, 0)),
        out_shape=jax.ShapeDtypeStruct((T, Dm), BF16), compiler_params=_params("parallel"),
    )(x, gain)


def _rms_bwd(x, dy, gain, dres, *, name, tr=512):
    T, Dm = x.shape

    def body(x_ref, dy_ref, g_ref, dres_ref, dx_ref, dg_ref):
        i = pl.program_id(0)
        xv = x_ref[...]
        dy_v = dy_ref[...].astype(F32)
        r = lax.rsqrt(jnp.mean(xv * xv, axis=-1, keepdims=True) + EPS)
        xn = xv * r
        dgp = jnp.sum(dy_v * xn, axis=0, keepdims=True)
        dyg = dy_v * g_ref[...]
        dx = r * (dyg - xn * jnp.mean(dyg * xn, axis=-1, keepdims=True))
        dx_ref[...] = dres_ref[...] + dx

        @pl.when(i == 0)
        def _():
            dg_ref[...] = dgp

        @pl.when(i > 0)
        def _():
            dg_ref[...] += dgp

    row = pl.BlockSpec((tr, Dm), lambda i: (i, 0))
    vec = pl.BlockSpec((1, Dm), lambda i: (0, 0))
    return pl.pallas_call(
        body, name=name, grid=(T // tr,), in_specs=[row, row, vec, row], out_specs=[row, vec],
        out_shape=[jax.ShapeDtypeStruct((T, Dm), F32), jax.ShapeDtypeStruct((1, Dm), F32)],
        compiler_params=_params("arbitrary"),
    )(x, dy, gain, dres)


def _final_loss(x, gain, target, *, tr=512):
    T, Dm = x.shape

    def body(x_ref, g_ref, t_ref, loss_ref, dx_ref, dg_ref):
        i = pl.program_id(0)
        xv = x_ref[...]
        g = g_ref[...]
        r = lax.rsqrt(jnp.mean(xv * xv, axis=-1, keepdims=True) + EPS)
        xn = xv * r
        err = xn * g - t_ref[...]
        lp = jnp.zeros((1, 128), F32) + 0.5 * jnp.sum(jnp.mean(err * err, axis=-1, keepdims=True))
        dy_v = err * (1.0 / Dm)
        dgp = jnp.sum(dy_v * xn, axis=0, keepdims=True)
        dyg = dy_v * g
        dx_ref[...] = r * (dyg - xn * jnp.mean(dyg * xn, axis=-1, keepdims=True))

        @pl.when(i == 0)
        def _():
            dg_ref[...] = dgp
            loss_ref[...] = lp

        @pl.when(i > 0)
        def _():
            dg_ref[...] += dgp
            loss_ref[...] += lp

    row = pl.BlockSpec((tr, Dm), lambda i: (i, 0))
    vec = pl.BlockSpec((1, Dm), lambda i: (0, 0))
    return pl.pallas_call(
        body, name="final_loss", grid=(T // tr,), in_specs=[row, vec, row],
        out_specs=[pl.BlockSpec((1, 128), lambda i: (0, 0)), row, vec],
        out_shape=[jax.ShapeDtypeStruct((1, 128), F32), jax.ShapeDtypeStruct((T, Dm), F32),
                   jax.ShapeDtypeStruct((1, Dm), F32)],
        compiler_params=_params("arbitrary"),
    )(x, gain, target)


def _prev_spec(tr, cb, pad, col):
    return pl.BlockSpec((pad, cb), lambda *g: (jnp.maximum(g[0] * (tr // pad) - 1, 0), col(*g)))


def _next_spec(tr, cb, pad, col, T):
    return pl.BlockSpec((pad, cb), lambda *g: (jnp.minimum((g[0] + 1) * (tr // pad), T // pad - 1), col(*g)))


def _conv_rows(x_ext, w_ref, K, pad, cs=slice(None)):
    y = w_ref[K - 1:K, cs] * x_ext
    for i in range(K - 1):
        y = y + w_ref[i:i + 1, cs] * pltpu.roll(x_ext, K - 1 - i, 0)
    return y[pad:]


def _pool_y(u_ext, g, i, tr):
    s = u_ext
    for sh in (1, 2, 4, 8)[:g + 1]:
        s = s + pltpu.roll(s, sh, 0)
    t = i * tr + _iota((tr, 128), 0)
    cnt = jnp.minimum(t + 1, POOL_WINDOWS[g]).astype(F32)
    return s[16:] / cnt - u_ext[16:]


def _pool_fwd(proj, pool_w, pool_scale, *, tr=512):
    T = proj.shape[0]

    def body(u_ref, uh_ref, w_ref, s_ref, o_ref):
        i = pl.program_id(0)
        uh = jnp.where(i > 0, uh_ref[...], 0.0)
        for g in range(4):
            cs = slice(128 * g, 128 * (g + 1))
            y = _pool_y(jnp.concatenate([uh[:, cs], u_ref[:, cs]], axis=0), g, i, tr)
            o_ref[:, cs] = (_dot(y, w_ref[g], NN) * s_ref[:, cs]).astype(BF16)

    return pl.pallas_call(
        body, name="pool_fwd", grid=(T // tr,),
        in_specs=[pl.BlockSpec((tr, 512), lambda i: (i, 0)), _prev_spec(tr, 512, 16, lambda i: 0),
                  pl.BlockSpec((4, 128, 128), lambda i: (0, 0, 0)), pl.BlockSpec((1, 512), lambda i: (0, 0))],
        out_specs=pl.BlockSpec((tr, 512), lambda i: (i, 0)),
        out_shape=jax.ShapeDtypeStruct((T, 512), BF16), compiler_params=_params("parallel"),
    )(proj, proj, pool_w, pool_scale)


def _pool_bwd(proj, dout, pool_w, pool_scale, *, tr=512):
    T = proj.shape[0]
    nb = T // tr

    def body(u_ref, uh_ref, d_ref, dn_ref, w_ref, s_ref, du_ref, dw_ref, ds_ref):
        i = pl.program_id(0)
        uh = jnp.where(i > 0, uh_ref[...], 0.0)
        dn = jnp.where(i < nb - 1, dn_ref[...], 0.0)
        t_ext = i * tr + _iota((tr + 16, 128), 0)
        for g in range(4):
            cs = slice(128 * g, 128 * (g + 1))
            sc = s_ref[:, cs]
            wg = w_ref[g]
            y = _pool_y(jnp.concatenate([uh[:, cs], u_ref[:, cs]], axis=0), g, i, tr)
            dg = d_ref[:, cs]
            dsp = jnp.sum(dg * _dot(y, wg, NN), axis=0, keepdims=True)
            dyw = dg * sc
            dwp = _dot(y, dyw, TN)
            dy_ext = _dot(jnp.concatenate([dyw, dn[:, cs] * sc], axis=0), wg, NT)
            cnt = jnp.minimum(t_ext + 1, POOL_WINDOWS[g]).astype(F32)
            s = dy_ext / cnt
            for sh in (1, 2, 4, 8)[:g + 1]:
                s = s + pltpu.roll(s, tr + 16 - sh, 0)
            du_ref[:, cs] = (s[:tr] - dy_ext[:tr]).astype(BF16)

            @pl.when(i == 0)
            def _():
                dw_ref[g] = dwp
                ds_ref[:, cs] = dsp

            @pl.when(i > 0)
            def _():
                dw_ref[g] += dwp
                ds_ref[:, cs] += dsp

    row = pl.BlockSpec((tr, 512), lambda i: (i, 0))
    return pl.pallas_call(
        body, name="pool_bwd", grid=(nb,),
        in_specs=[row, _prev_spec(tr, 512, 16, lambda i: 0), row, _next_spec(tr, 512, 16, lambda i: 0, T),
                  pl.BlockSpec((4, 128, 128), lambda i: (0, 0, 0)), pl.BlockSpec((1, 512), lambda i: (0, 0))],
        out_specs=[row, pl.BlockSpec((4, 128, 128), lambda i: (0, 0, 0)), pl.BlockSpec((1, 512), lambda i: (0, 0))],
        out_shape=[jax.ShapeDtypeStruct((T, 512), BF16), jax.ShapeDtypeStruct((4, 128, 128), F32),
                   jax.ShapeDtypeStruct((1, 512), F32)],
        compiler_params=_params("arbitrary"),
    )(proj, proj, dout, dout, pool_w, pool_scale)


def _split_dot(x, tri):
    hi = x.astype(BF16)
    lo = (x - hi.astype(F32)).astype(BF16)
    return (lax.dot_general(hi, tri, NN, preferred_element_type=F32)
            + lax.dot_general(lo, tri, NN, preferred_element_type=F32))


def _log1m(z):
    return -(jnp.maximum(z, 0.0) + jnp.log(1.0 + jnp.exp(-jnp.abs(z))))


def _sb_fwd(proj, gather=()):
    T = proj.shape[0]
    B, BK = min(SB_BLOCK, T), SB_KBLOCK
    R = B // BK
    nq, n = T // B, len(gather)
    scale = SB_HEAD_DIM ** -0.5

    def body(q_ref, k_ref, v_ref, *rest):
        o_ref, ls_ref = rest[n:n + 2]
        hp, i = pl.program_id(0), pl.program_id(1)
        if n:
            start, forward, finish = _gather_stages(rest[:n], rest[n + 2:2 * n + 2], *rest[2 * n + 2:])
            pl.when((hp == 0) & (i == 0))(start)
            pl.when((hp == 3) & (i == nq - 1))(forward)
        lane = _iota((1, 128), 1)
        tri_gt = (_iota((BK, BK), 0) > _iota((BK, BK), 1)).astype(BF16)
        row, col = _iota((B, BK), 0), _iota((B, BK), 1)
        qv = q_ref[...] * scale
        hms = [(lane >= 64 * h) & (lane < 64 * (h + 1)) for h in range(2)]
        qhs = [jnp.where(hm, qv, 0.0).astype(BF16) for hm in hms]

        def tile(j, carry, d):
            rows = pl.ds(pl.multiple_of(j * BK, BK), BK)
            kj = k_ref[rows, :].astype(BF16)
            vj = v_ref[rows, :].astype(BF16)
            r0 = 0 if d is None else BK * d
            valid = None if d is None else (col[r0:] < row[:B - r0])
            out = []
            for h in range(2):
                c, acc = carry[h]
                z = lax.dot_general(qhs[h][r0:], kj, NT, preferred_element_type=F32)
                lg = _log1m(z)
                if d is not None:
                    lg = jnp.where(valid, lg, 0.0)
                a = jnp.exp(z + lg + _split_dot(lg, tri_gt) + c[r0:])
                if d is not None:
                    a = jnp.where(valid, a, 0.0)
                upd = (c[r0:] + jnp.sum(lg, axis=1, keepdims=True),
                       acc[r0:] + lax.dot_general(a.astype(BF16), vj, NN, preferred_element_type=F32))
                out.append(upd if r0 == 0 else tuple(jnp.concatenate([old[:r0], new], axis=0)
                                                     for old, new in zip((c, acc), upd)))
            return tuple(out)

        zero = (jnp.zeros((B, 1), F32), jnp.zeros((B, 128), F32))
        carry = (zero, zero)
        for d in reversed(range(R)):
            carry = tile(i * R + d, carry, d)
        carry = lax.fori_loop(0, i * R, lambda s, cr: tile(i * R - 1 - s, cr, None), carry)
        o_ref[...] = jnp.where(hms[0], carry[0][1], carry[1][1])
        ls_ref[...] = jnp.where(hms[0], carry[0][0], carry[1][0])
        if n:
            pl.when((hp == 3) & (i == nq - 1))(finish)

    blk = pl.BlockSpec((B, 128), lambda hp, i: (i, hp))
    out = pl.pallas_call(
        body, name="sb_fwd", grid=(4, nq),
        in_specs=[pl.BlockSpec((B, 128), lambda hp, i: (i, 4 + hp)),
                  pl.BlockSpec((T, 128), lambda hp, i: (0, 8 + hp)),
                  pl.BlockSpec((T, 128), lambda hp, i: (0, 12 + hp))] + [HBM_SPEC] * n,
        out_specs=[blk, blk] + [HBM_SPEC] * n,
        out_shape=[jax.ShapeDtypeStruct((T, 512), F32)] * 2 + _gather_shapes(gather),
        scratch_shapes=_gather_sems(n) if n else [],
        compiler_params=_params("arbitrary", "arbitrary"),
    )(proj, proj, proj, *gather)
    return out[0], out[1], list(out[2:])


def _sb_bwd(proj, lsum, dout, exchange=()):
    T = proj.shape[0]
    B, BK = min(SB_BLOCK, T), SB_KBLOCK
    R = B // BK
    nq, n = T // B, len(exchange)
    scale = SB_HEAD_DIM ** -0.5

    def body(q_ref, k_ref, v_ref, do_ref, ls_ref, *rest):
        dq_ref, dk_ref, dv_ref = rest[n:n + 3]
        hp, i = pl.program_id(0), pl.program_id(1)
        if n:
            start, finish = _chips_stages(rest[:n], rest[n + 3:2 * n + 3], *rest[2 * n + 3:])
            pl.when((hp == 0) & (i == 0))(start)

        @pl.when(i == 0)
        def _():
            dk_ref[...] = jnp.zeros_like(dk_ref)
            dv_ref[...] = jnp.zeros_like(dv_ref)

        lane = _iota((1, 128), 1)
        tri_le = (_iota((BK, BK), 0) <= _iota((BK, BK), 1)).astype(BF16)
        tri_lt = (_iota((BK, BK), 0) < _iota((BK, BK), 1)).astype(BF16)
        row, col = _iota((B, BK), 0), _iota((B, BK), 1)
        qv = q_ref[...] * scale
        dov = do_ref[...]
        hms = [(lane >= 64 * h) & (lane < 64 * (h + 1)) for h in range(2)]
        qhs = [jnp.where(hm, qv, 0.0).astype(BF16) for hm in hms]
        dos = [jnp.where(hm, dov, 0.0).astype(BF16) for hm in hms]
        ltots = [ls_ref[:, 64 * h:64 * h + 1] for h in range(2)]

        def tile(j, carry, d):
            rows = pl.ds(pl.multiple_of(j * BK, BK), BK)
            kj = k_ref[rows, :].astype(BF16)
            vj = v_ref[rows, :].astype(BF16)
            diag = d is not None
            r0 = BK * d if diag else 0
            valid = (col[r0:] < row[:B - r0]) if diag else None
            out = []
            dkj = jnp.zeros((BK, 128), F32)
            dvj = jnp.zeros((BK, 128), F32)
            for h in range(2):
                lbef, ebef, dqa = carry[h]
                qh, do_h = qhs[h][r0:], dos[h][r0:]
                z = lax.dot_general(qh, kj, NT, preferred_element_type=F32)
                lg = _log1m(z)
                if diag:
                    lg = jnp.where(valid, lg, 0.0)
                a = jnp.exp(z + lg + (ltots[h][r0:] - lbef[r0:] - _split_dot(lg, tri_le)))
                if diag:
                    a = jnp.where(valid, a, 0.0)
                e = a * lax.dot_general(do_h, vj, NT, preferred_element_type=F32)
                dz = e * jnp.exp(lg) - jnp.exp(z + lg) * (ebef[r0:] + _split_dot(e, tri_lt))
                if diag:
                    dz = jnp.where(valid, dz, 0.0)
                dzb = dz.astype(BF16)
                dkj = dkj + lax.dot_general(dzb, qh, TN, preferred_element_type=F32)
                dvj = dvj + lax.dot_general(a.astype(BF16), do_h, TN, preferred_element_type=F32)
                upd = (lbef[r0:] + jnp.sum(lg, axis=1, keepdims=True), ebef[r0:] + jnp.sum(e, axis=1, keepdims=True),
                       dqa[r0:] + lax.dot_general(dzb, kj, NN, preferred_element_type=F32))
                out.append(upd if r0 == 0 else tuple(jnp.concatenate([old[:r0], new], axis=0)
                                                     for old, new in zip(carry[h], upd)))
            dk_ref[rows, :] += dkj
            dv_ref[rows, :] += dvj
            return tuple(out)

        zero = (jnp.zeros((B, 1), F32), jnp.zeros((B, 1), F32), jnp.zeros((B, 128), F32))
        carry = lax.fori_loop(0, i * R, lambda j, cr: tile(j, cr, None), (zero, zero))
        for d in range(R):
            carry = tile(i * R + d, carry, d)
        dq_ref[...] = jnp.where(hms[0], carry[0][2], carry[1][2]) * scale
        if n:
            pl.when((hp == 3) & (i == nq - 1))(finish)

    full = pl.BlockSpec((T, 128), lambda hp, i: (0, hp))
    blk = pl.BlockSpec((B, 128), lambda hp, i: (i, hp))
    out = pl.pallas_call(
        body, name="sb_bwd", grid=(4, nq),
        in_specs=[pl.BlockSpec((B, 128), lambda hp, i: (i, 4 + hp)),
                  pl.BlockSpec((T, 128), lambda hp, i: (0, 8 + hp)),
                  pl.BlockSpec((T, 128), lambda hp, i: (0, 12 + hp)),
                  pl.BlockSpec((B, 128), lambda hp, i: (i, 4 + hp)), blk] + [HBM_SPEC] * n,
        out_specs=[blk, full, full] + [HBM_SPEC] * n,
        out_shape=[jax.ShapeDtypeStruct((T, 512), F32)] * 3 + _chips_shapes(exchange),
        scratch_shapes=_chips_sems(n) if n else [],
        compiler_params=_params("arbitrary", "arbitrary"),
    )(proj, proj, proj, dout, lsum, *exchange)
    return out[0], out[1], out[2], list(out[3:])


def _sigmoid(x):
    return 1.0 / (1.0 + jnp.exp(-x))


def _silu_mul(cg, cv):
    return cg * _sigmoid(cg) * cv


def _ffn_act(up, conv_w, *, tr=512, cb=256):
    T, F2 = up.shape
    nc = F2 // 2 // cb
    K = FFN_CONV

    def body(g_ref, gh_ref, v_ref, vh_ref, wg_ref, wv_ref, o_ref):
        i = pl.program_id(0)
        gh = jnp.where(i > 0, gh_ref[...], 0.0)
        vh = jnp.where(i > 0, vh_ref[...], 0.0)
        cg = _conv_rows(jnp.concatenate([gh, g_ref[...]], axis=0), wg_ref, K, 8)
        cv = _conv_rows(jnp.concatenate([vh, v_ref[...]], axis=0), wv_ref, K, 8)
        o_ref[...] = _silu_mul(cg, cv).astype(BF16)

    return pl.pallas_call(
        body, name="ffn_act", grid=(T // tr, nc),
        in_specs=[pl.BlockSpec((tr, cb), lambda i, j: (i, j)), _prev_spec(tr, cb, 8, lambda i, j: j),
                  pl.BlockSpec((tr, cb), lambda i, j: (i, nc + j)), _prev_spec(tr, cb, 8, lambda i, j: nc + j),
                  pl.BlockSpec((K, cb), lambda i, j: (0, j)), pl.BlockSpec((K, cb), lambda i, j: (0, nc + j))],
        out_specs=pl.BlockSpec((tr, cb), lambda i, j: (i, j)),
        out_shape=jax.ShapeDtypeStruct((T, F2 // 2), BF16), compiler_params=_params("parallel", "parallel"),
    )(up, up, up, up, conv_w, conv_w)


def _conv_bwd_rows(dc_ext, x_ext, w_ref, K, tr, cs=slice(None)):
    n = tr + 8
    dx = w_ref[K - 1:K, cs] * dc_ext
    for i in range(K - 1):
        dx = dx + w_ref[i:i + 1, cs] * pltpu.roll(dc_ext, n - (K - 1 - i), 0)
    dc = dc_ext[:tr]
    dws = [jnp.sum(dc * pltpu.roll(x_ext, K - 1 - i, 0)[8:8 + tr], axis=0, keepdims=True) for i in range(K)]
    return dx[:tr], dws


def _acc_rows(ref, rows, first, cs=slice(None)):
    for i, r in enumerate(rows):
        @pl.when(first)
        def _():
            ref[i:i + 1, cs] = r

        @pl.when(jnp.logical_not(first))
        def _():
            ref[i:i + 1, cs] += r


def _ffn_act_bwd(up, conv_w, dact, *, tr=512, cb=256):
    T, F2 = up.shape
    F = F2 // 2
    nc, nb = F // cb, T // tr
    K = FFN_CONV

    def body(g_ref, gp_ref, gn_ref, v_ref, vp_ref, vn_ref, d_ref, dn_ref, wg_ref, wv_ref,
             dg_ref, dv_ref, dwg_ref, dwv_ref):
        i = pl.program_id(1)
        first, last = i == 0, i == nb - 1
        g_ext = jnp.concatenate([jnp.where(first, 0.0, gp_ref[...]), g_ref[...], jnp.where(last, 0.0, gn_ref[...])], axis=0)
        v_ext = jnp.concatenate([jnp.where(first, 0.0, vp_ref[...]), v_ref[...], jnp.where(last, 0.0, vn_ref[...])], axis=0)
        d_ext = jnp.concatenate([d_ref[...], jnp.where(last, 0.0, dn_ref[...])], axis=0)
        cg = _conv_rows(g_ext, wg_ref, K, 8)
        cv = _conv_rows(v_ext, wv_ref, K, 8)
        _, vjp = jax.vjp(_silu_mul, cg, cv)
        dcg, dcv = vjp(d_ext)
        dg, dwg = _conv_bwd_rows(dcg, g_ext, wg_ref, K, tr)
        dv, dwv = _conv_bwd_rows(dcv, v_ext, wv_ref, K, tr)
        dg_ref[...] = dg.astype(BF16)
        dv_ref[...] = dv.astype(BF16)
        _acc_rows(dwg_ref, dwg, first)
        _acc_rows(dwv_ref, dwv, first)

    blk = lambda off: pl.BlockSpec((tr, cb), lambda j, i: (i, off + j))
    prev = lambda off: pl.BlockSpec((8, cb), lambda j, i: (jnp.maximum(i * (tr // 8) - 1, 0), off + j))
    nxt = lambda off: pl.BlockSpec((8, cb), lambda j, i: (jnp.minimum((i + 1) * (tr // 8), T // 8 - 1), off + j))
    wsp = lambda off: pl.BlockSpec((K, cb), lambda j, i: (0, off + j))
    return pl.pallas_call(
        body, name="ffn_act_bwd", grid=(nc, nb),
        in_specs=[blk(0), prev(0), nxt(0), blk(nc), prev(nc), nxt(nc), blk(0), nxt(0), wsp(0), wsp(nc)],
        out_specs=[blk(0), blk(0), wsp(0), wsp(0)],
        out_shape=[jax.ShapeDtypeStruct((T, F), BF16)] * 2 + [jax.ShapeDtypeStruct((K, F), F32)] * 2,
        compiler_params=_params("parallel", "arbitrary"),
    )(up, up, up, up, up, up, dact, dact, conv_w, conv_w)


def _ple_fwd(hn, w_gate, p, w_ple, x, *, name, tm=1024, tn=512):
    T, Dm = x.shape
    tm = min(tm, T)

    def body(a_ref, b_ref, p_ref, wp_ref, x_ref, o_ref, gl_ref, pe_ref):
        gl = _dot(a_ref[...], b_ref[...], NN)
        pe = _dot(p_ref[...], wp_ref[...], NN)
        gl_ref[...] = gl
        pe_ref[...] = pe
        o_ref[...] = x_ref[...] + pe * _sigmoid(gl)

    o_spec = pl.BlockSpec((tm, tn), lambda i, j: (i, j))
    return pl.pallas_call(
        body, name=name, grid=(T // tm, Dm // tn),
        in_specs=[pl.BlockSpec((tm, Dm), lambda i, j: (i, 0)), pl.BlockSpec((Dm, tn), lambda i, j: (0, j)),
                  pl.BlockSpec((tm, PLE_DIM), lambda i, j: (i, 0)), pl.BlockSpec((PLE_DIM, tn), lambda i, j: (0, j)),
                  o_spec],
        out_specs=[o_spec] * 3, out_shape=[jax.ShapeDtypeStruct((T, Dm), F32)] * 3,
        compiler_params=_params("parallel", "parallel"),
    )(hn, w_gate, p, w_ple, x)


def _ple_bwd(dx, gl, pe, *, name, tr=512):
    T, Dm = dx.shape

    def body(dx_ref, gl_ref, pe_ref, dpe_ref, dgl_ref):
        g = _sigmoid(gl_ref[...])
        d = dx_ref[...]
        dpe_ref[...] = (d * g).astype(BF16)
        dgl_ref[...] = (d * pe_ref[...] * g * (1.0 - g)).astype(BF16)

    row = pl.BlockSpec((tr, Dm), lambda i: (i, 0))
    return pl.pallas_call(
        body, name=name, grid=(T // tr,), in_specs=[row] * 3, out_specs=[row] * 2,
        out_shape=[jax.ShapeDtypeStruct((T, Dm), BF16)] * 2, compiler_params=_params("parallel"),
    )(dx, gl, pe)


def _qkv_act(c, cb):
    s = c * _sigmoid(c)
    n = s * lax.rsqrt(jnp.sum(s * s, axis=-1, keepdims=True) + EPS)
    n = n * jnp.where(cb < GDN_HEADS, GDN_HEAD_DIM ** -0.5, 1.0)
    return jnp.where(cb < 2 * GDN_HEADS, n, s)


GDN_HPS = 4


def _gdn_pre(proj, conv_w, *, tr=512):
    T = proj.shape[0]
    K = GDN_CONV

    def body(x_ref, xh_ref, w_ref, o_ref):
        i, j = pl.program_id(0), pl.program_id(1)
        xh = jnp.where(i > 0, xh_ref[...], 0.0)
        for hh in range(GDN_HPS):
            cs = slice(128 * hh, 128 * (hh + 1))
            c = _conv_rows(jnp.concatenate([xh[:, cs], x_ref[:, cs]], axis=0), w_ref, K, 8, cs)
            o_ref[hh] = _qkv_act(c, GDN_HPS * j + hh)

    wide = 128 * GDN_HPS
    return pl.pallas_call(
        body, name="gdn_pre", grid=(T // tr, 24 // GDN_HPS),
        in_specs=[pl.BlockSpec((tr, wide), lambda i, j: (i, j)), _prev_spec(tr, wide, 8, lambda i, j: j),
                  pl.BlockSpec((K, wide), lambda i, j: (0, j))],
        out_specs=pl.BlockSpec((GDN_HPS, tr, 128), lambda i, j: (j, i, 0)),
        out_shape=jax.ShapeDtypeStruct((24, T, 128), F32), compiler_params=_params("parallel", "parallel"),
    )(proj, proj, conv_w)


def _gdn_pre_bwd(proj, conv_w, dqkv, *, tr=512):
    T = proj.shape[0]
    nb = T // tr
    K = GDN_CONV

    def body(x_ref, xp_ref, xn_ref, d_ref, dn_ref, w_ref, dx_ref, dw_ref):
        j, i = pl.program_id(0), pl.program_id(1)
        first, last = i == 0, i == nb - 1
        xp = jnp.where(first, 0.0, xp_ref[...])
        xn = jnp.where(last, 0.0, xn_ref[...])
        for hh in range(GDN_HPS):
            cs = slice(128 * hh, 128 * (hh + 1))
            x_ext = jnp.concatenate([xp[:, cs], x_ref[:, cs], xn[:, cs]], axis=0)
            d_ext = jnp.concatenate([d_ref[hh], jnp.where(last, 0.0, dn_ref[hh])], axis=0)
            c = _conv_rows(x_ext, w_ref, K, 8, cs)
            _, vjp = jax.vjp(lambda c_: _qkv_act(c_, GDN_HPS * j + hh), c)
            (dc,) = vjp(d_ext)
            dx, dws = _conv_bwd_rows(dc, x_ext, w_ref, K, tr, cs)
            dx_ref[:, cs] = dx.astype(BF16)
            _acc_rows(dw_ref, dws, first, cs)

    wide = 128 * GDN_HPS
    return pl.pallas_call(
        body, name="gdn_pre_bwd", grid=(24 // GDN_HPS, nb),
        in_specs=[pl.BlockSpec((tr, wide), lambda j, i: (i, j)),
                  pl.BlockSpec((8, wide), lambda j, i: (jnp.maximum(i * (tr // 8) - 1, 0), j)),
                  pl.BlockSpec((8, wide), lambda j, i: (jnp.minimum((i + 1) * (tr // 8), T // 8 - 1), j)),
                  pl.BlockSpec((GDN_HPS, tr, 128), lambda j, i: (j, i, 0)),
                  pl.BlockSpec((GDN_HPS, 8, 128), lambda j, i: (j, jnp.minimum((i + 1) * (tr // 8), T // 8 - 1), 0)),
                  pl.BlockSpec((K, wide), lambda j, i: (0, j))],
        out_specs=[pl.BlockSpec((tr, wide), lambda j, i: (i, j)), pl.BlockSpec((K, wide), lambda j, i: (0, j))],
        out_shape=[jax.ShapeDtypeStruct((T, 24 * 128), BF16), jax.ShapeDtypeStruct((K, 24 * 128), F32)],
        compiler_params=_params("parallel", "arbitrary"),
    )(proj, proj, proj, dqkv, dqkv, conv_w)


def _gate_fn(ba, alog_row, dt_row):
    lane = _iota((1, 128), 1)
    x = ba + dt_row
    sp = jnp.maximum(x, 0.0) + jnp.log(1.0 + jnp.exp(-jnp.abs(x)))
    return jnp.where(lane < GDN_HEADS, _sigmoid(ba), -jnp.exp(alog_row) * sp)


def _gdn_gate(ba, alog_row, dt_row, *, tr=512):
    T = ba.shape[0]

    def body(ba_ref, al_ref, dt_ref, b_ref, g_ref):
        val = _gate_fn(ba_ref[...], al_ref[...], dt_ref[...])
        for h in range(GDN_HEADS):
            b_ref[h] = val[:, h:h + 1]
            g_ref[h] = val[:, GDN_HEADS + h:GDN_HEADS + h + 1]

    vec = pl.BlockSpec((1, 128), lambda i: (0, 0))
    hm = pl.BlockSpec((GDN_HEADS, tr, 1), lambda i: (0, i, 0))
    return pl.pallas_call(
        body, name="gdn_gate", grid=(T // tr,), in_specs=[pl.BlockSpec((tr, 128), lambda i: (i, 0)), vec, vec],
        out_specs=[hm, hm], out_shape=[jax.ShapeDtypeStruct((GDN_HEADS, T, 1), F32)] * 2,
        compiler_params=_params("parallel"),
    )(ba, alog_row, dt_row)


def _gdn_gate_bwd(ba, alog_row, dt_row, dbeta, dg, *, tr=512):
    T = ba.shape[0]

    def body(ba_ref, al_ref, dt_ref, db_ref, dg_ref, dba_ref, dal_ref, ddt_ref):
        i = pl.program_id(0)
        lane = _iota((1, 128), 1)
        d = jnp.zeros((tr, 128), F32)
        for h in range(GDN_HEADS):
            d = d + jnp.where(lane == h, db_ref[h], 0.0) + jnp.where(lane == GDN_HEADS + h, dg_ref[h], 0.0)
        _, vjp = jax.vjp(_gate_fn, ba_ref[...], al_ref[...], dt_ref[...])
        dba, dal, ddt = vjp(d)
        dba_ref[...] = dba.astype(BF16)

        @pl.when(i == 0)
        def _():
            dal_ref[...] = dal
            ddt_ref[...] = ddt

        @pl.when(i > 0)
        def _():
            dal_ref[...] += dal
            ddt_ref[...] += ddt

    vec = pl.BlockSpec((1, 128), lambda i: (0, 0))
    hm = pl.BlockSpec((GDN_HEADS, tr, 1), lambda i: (0, i, 0))
    row = pl.BlockSpec((tr, 128), lambda i: (i, 0))
    return pl.pallas_call(
        body, name="gdn_gate_bwd", grid=(T // tr,), in_specs=[row, vec, vec, hm, hm], out_specs=[row, vec, vec],
        out_shape=[jax.ShapeDtypeStruct((T, 128), BF16), jax.ShapeDtypeStruct((1, 128), F32),
                   jax.ShapeDtypeStruct((1, 128), F32)],
        compiler_params=_params("arbitrary"),
    )(ba, alog_row, dt_row, dbeta, dg)


def _split3(x):
    x1 = x.astype(BF16)
    r = x - x1.astype(F32)
    x2 = r.astype(BF16)
    return x1, x2, (r - x2.astype(F32)).astype(BF16)


def _dot01(tri, x, dims):
    t = tri.astype(BF16)
    x1, x2, x3 = _split3(x)
    d = lambda xi: lax.dot_general(t, xi, dims, preferred_element_type=F32)
    return d(x1) + (d(x2) + d(x3))


def _dot3(a, b, dims):
    ah, al, _ = _split3(a)
    bh, bl, _ = _split3(b)
    d = lambda p, q: lax.dot_general(p, q, dims, preferred_element_type=F32)
    return d(ah, bh) + (d(ah, bl) + d(al, bh))


BNN = (((2,), (1,)), ((0,), (0,)))
BNT = (((2,), (2,)), ((0,), (0,)))
BTN = (((1,), (1,)), ((0,), (0,)))


@jax.custom_vjp
def _mm01(tri, x):
    return _dot01(tri, x, BNN)


def _mm01_fwd(tri, x):
    return _dot01(tri, x, BNN), tri


def _mm01_bwd(tri, ct):
    return jnp.zeros_like(tri), _dot01(tri, ct, BTN)


_mm01.defvjp(_mm01_fwd, _mm01_bwd)


def _unit_lower_inverse(a):
    C = a.shape[-1]
    eye = (_iota(a.shape, 1) == _iota(a.shape, 2)).astype(F32)
    pw = -a
    tinv = eye + pw
    for _ in range(5):
        pw = _dot3(pw, pw, BNN)
        tinv = tinv + _dot3(tinv, pw, BNN)
    return tinv


@jax.custom_vjp
def _unit_lower_solve(a, rv, rw):
    return _unit_lower_solve_fwd(a, rv, rw)[0]


def _unit_lower_solve_fwd(a, rv, rw):
    tinv = _unit_lower_inverse(a)
    sol = _dot3(tinv, jnp.concatenate([rv, rw], axis=2), BNN)
    n = rv.shape[2]
    return (sol[:, :, :n], sol[:, :, n:]), (tinv, sol)


def _unit_lower_solve_bwd(res, cts):
    tinv, sol = res
    n = cts[0].shape[2]
    d_rhs = _dot3(tinv, jnp.concatenate(cts, axis=2), BTN)
    return -_dot3(d_rhs, sol, BNT), d_rhs[:, :, :n], d_rhs[:, :, n:]


_unit_lower_solve.defvjp(_unit_lower_solve_fwd, _unit_lower_solve_bwd)


@jax.custom_vjp
def _mmb_nt(a, b):
    return _dot(a, b, BNT)


def _mmb_nt_fwd(a, b):
    return _dot(a, b, BNT), (a, b)


def _mmb_nt_bwd(res, ct):
    a, b = res
    return _dot(ct, b, BNN), _dot(ct, a, BTN)


_mmb_nt.defvjp(_mmb_nt_fwd, _mmb_nt_bwd)


def _gdn_chunk(q, k, v, gcol, bcol):
    nb, C = q.shape[0], GDN_CHUNK
    row, col = _iota((nb, C, C), 1), _iota((nb, C, C), 2)
    incl, strict = row >= col, row > col
    eye = (row == col).astype(F32)
    lower = incl.astype(F32)
    ones = jnp.ones((nb, C, C), F32)
    gwide = jnp.broadcast_to(gcol, (nb, C, GDN_HEAD_DIM))
    gc = _mm01(lower, gwide)
    gtot = _mm01(ones, gwide)
    gc_c = _mm01(lower, jnp.broadcast_to(gcol, (nb, C, C)))
    gc_s = _mm01(ones, gc_c * eye)
    decay = jnp.where(incl, jnp.exp(jnp.where(incl, gc_c - gc_s, 0.0)), 0.0)
    kb = k * bcol
    a = jnp.where(strict, _mmb_nt(kb, k) * decay, 0.0)
    egc = jnp.exp(gc)
    u, w = _unit_lower_solve(a, v * bcol, kb * egc)
    qk = jnp.where(incl, _mmb_nt(q, k) * decay, 0.0)
    return u, w, qk, q * egc, k * jnp.exp(gtot - gc), jnp.exp(jnp.sum(gwide, axis=1))


GDN_ROWS = 8 * GDN_CHUNK


GDN_LOCAL_CHUNKS = 16


def _gdn_specs(T):
    nch = min(GDN_LOCAL_CHUNKS, T // GDN_CHUNK)
    L = nch * GDN_CHUNK
    hd = lambda off: pl.BlockSpec((1, L, 128), lambda h, i: (off + h, i, 0))
    col = pl.BlockSpec((1, L, 1), lambda h, i: (h, i, 0))
    sq = pl.BlockSpec((1, L, GDN_CHUNK), lambda h, i: (h, i, 0))
    gl = pl.BlockSpec((1, nch, 128), lambda h, i: (h, i, 0))
    return nch, hd, col, sq, gl


def _gdn_local(qkv, g, beta):
    T = qkv.shape[1]
    nch, hd, col, sq, gl_spec = _gdn_specs(T)

    def body(q_ref, k_ref, v_ref, g_ref, b_ref, u_ref, w_ref, qk_ref, qd_ref, kd_ref, gl_ref):
        chunks = lambda ref: ref[0].reshape(nch, GDN_CHUNK, ref.shape[2])
        rows = lambda val: val.reshape(nch * GDN_CHUNK, val.shape[2])
        u, w, qk, qd, kd, gl = _gdn_chunk(chunks(q_ref), chunks(k_ref), chunks(v_ref), chunks(g_ref), chunks(b_ref))
        u_ref[0] = rows(u)
        w_ref[0] = rows(w).astype(BF16)
        qk_ref[0] = rows(qk).astype(BF16)
        qd_ref[0] = rows(qd).astype(BF16)
        kd_ref[0] = rows(kd).astype(BF16)
        gl_ref[0] = gl

    H = GDN_HEADS
    return pl.pallas_call(
        body, name="gdn_local", grid=(H, T // (nch * GDN_CHUNK)),
        in_specs=[hd(0), hd(H), hd(2 * H), col, col],
        out_specs=[hd(0), hd(0), sq, hd(0), hd(0), gl_spec],
        out_shape=[jax.ShapeDtypeStruct((H, T, 128), F32), jax.ShapeDtypeStruct((H, T, 128), BF16),
                   jax.ShapeDtypeStruct((H, T, GDN_CHUNK), BF16), jax.ShapeDtypeStruct((H, T, 128), BF16),
                   jax.ShapeDtypeStruct((H, T, 128), BF16), jax.ShapeDtypeStruct((H, T // GDN_CHUNK, 128), F32)],
        compiler_params=_params("parallel", "parallel"),
    )(qkv, qkv, qkv, g, beta)


def _gdn_local_bwd(qkv, g, beta, du, dw, dqk, dqd, dkd, dgl):
    T = qkv.shape[1]
    nch, hd, col, sq, gl_spec = _gdn_specs(T)

    def body(q_ref, k_ref, v_ref, g_ref, b_ref, du_ref, dw_ref, dqk_ref, dqd_ref, dkd_ref, dgl_ref,
             dqkv_ref, dg_ref, db_ref):
        chunks = lambda ref: ref[0].reshape(nch, GDN_CHUNK, ref.shape[2])
        rows = lambda val: val.reshape(nch * GDN_CHUNK, val.shape[2])
        _, vjp = jax.vjp(_gdn_chunk, chunks(q_ref), chunks(k_ref), chunks(v_ref), chunks(g_ref), chunks(b_ref))
        dq, dk, dv, dg, db = vjp((chunks(du_ref), chunks(dw_ref), chunks(dqk_ref), chunks(dqd_ref), chunks(dkd_ref),
                                  dgl_ref[0]))
        dqkv_ref[0, 0] = rows(dq)
        dqkv_ref[1, 0] = rows(dk)
        dqkv_ref[2, 0] = rows(dv)
        dg_ref[0] = rows(dg)
        db_ref[0] = rows(db)

    H = GDN_HEADS
    small = jax.ShapeDtypeStruct((H, T, 1), F32)
    dqkv, dg, db = pl.pallas_call(
        body, name="gdn_local_bwd", grid=(H, T // (nch * GDN_CHUNK)),
        in_specs=[hd(0), hd(H), hd(2 * H), col, col, hd(0), hd(0), sq, hd(0), hd(0), gl_spec],
        out_specs=[pl.BlockSpec((3, 1, nch * GDN_CHUNK, 128), lambda h, i: (0, h, i, 0)), col, col],
        out_shape=[jax.ShapeDtypeStruct((3, H, T, 128), F32), small, small],
        compiler_params=_params("parallel", "parallel"),
    )(qkv, qkv, qkv, g, beta, du, dw, dqk, dqd, dkd, dgl)
    return dqkv.reshape(3 * H, T, 128), dg, db


GDN_HB = 4


def _gdn_scan_specs(T, rev):
    nb = T // GDN_ROWS
    blk = (lambda i: nb - 1 - i) if rev else (lambda i: i)
    hd = pl.BlockSpec((GDN_HB, GDN_ROWS, 128), lambda h, i: (h, blk(i), 0))
    sq = pl.BlockSpec((GDN_HB, GDN_ROWS, GDN_CHUNK), lambda h, i: (h, blk(i), 0))
    gl = pl.BlockSpec((GDN_HB, 8, 128), lambda h, i: (h, blk(i), 0))
    st = pl.BlockSpec((GDN_HB, 8, 128, 128), lambda h, i: (h, blk(i), 0, 0))
    return hd, sq, gl, st


def _gdn_scan(u, w, qk, qd, kd, gl):
    H, T, _ = u.shape
    hd, sq, gl_spec, st = _gdn_scan_specs(T, False)

    def body(u_ref, w_ref, qk_ref, qd_ref, kd_ref, gl_ref, o_ref, ss_ref, vn_ref, s_scr):
        @pl.when(pl.program_id(1) == 0)
        def _():
            s_scr[...] = jnp.zeros_like(s_scr)

        dot = lambda a, b, dims: lax.dot_general(a, b, dims, preferred_element_type=F32)
        s = s_scr[...]
        for c in range(8):
            rs = slice(GDN_CHUNK * c, GDN_CHUNK * (c + 1))
            ss_ref[:, c] = s
            sb = s.astype(BF16)
            vn = u_ref[:, rs, :] - dot(w_ref[:, rs, :], sb, BNN)
            vnb = vn.astype(BF16)
            o_ref[:, rs, :] = dot(qd_ref[:, rs, :], sb, BNN) + dot(qk_ref[:, rs, :], vnb, BNN)
            vn_ref[:, rs, :] = vnb
            s = s * gl_ref[:, c:c + 1, :] + dot(kd_ref[:, rs, :], vnb, BTN)
        s_scr[...] = s

    return pl.pallas_call(
        body, name="gdn_scan", grid=(H // GDN_HB, T // GDN_ROWS),
        in_specs=[hd, hd, sq, hd, hd, gl_spec], out_specs=[hd, st, hd],
        out_shape=[jax.ShapeDtypeStruct((H, T, 128), F32), jax.ShapeDtypeStruct((H, T // GDN_CHUNK, 128, 128), F32),
                   jax.ShapeDtypeStruct((H, T, 128), BF16)],
        scratch_shapes=[pltpu.VMEM((GDN_HB, 128, 128), F32)],
        compiler_params=_params("parallel", "arbitrary"),
    )(u, w, qk, qd, kd, gl)


def _gdn_scan_bwd(do, ss, vn, w, qk, qd, kd, gl):
    H, T, _ = do.shape
    hd, sq, gl_spec, st = _gdn_scan_specs(T, True)

    def body(do_ref, ss_ref, vn_ref, w_ref, qk_ref, qd_ref, kd_ref, gl_ref,
             du_ref, dw_ref, dqk_ref, dqd_ref, dkd_ref, dgl_ref, ds_scr):
        @pl.when(pl.program_id(1) == 0)
        def _():
            ds_scr[...] = jnp.zeros_like(ds_scr)

        dot = lambda a, b, dims: lax.dot_general(a, b, dims, preferred_element_type=F32)
        ds = ds_scr[...]
        for c in reversed(range(8)):
            rs = slice(GDN_CHUNK * c, GDN_CHUNK * (c + 1))
            s = ss_ref[:, c]
            sb, dsb = s.astype(BF16), ds.astype(BF16)
            dob = do_ref[:, rs, :].astype(BF16)
            vnb = vn_ref[:, rs, :]
            dvn = dot(qk_ref[:, rs, :], dob, BTN) + dot(kd_ref[:, rs, :], dsb, BNN)
            dvnb = dvn.astype(BF16)
            du_ref[:, rs, :] = dvn
            dw_ref[:, rs, :] = -dot(dvnb, sb, BNT)
            dqk_ref[:, rs, :] = dot(dob, vnb, BNT)
            dqd_ref[:, rs, :] = dot(dob, sb, BNT)
            dkd_ref[:, rs, :] = dot(vnb, dsb, BNT)
            dgl_ref[:, c:c + 1, :] = jnp.sum(ds * s, axis=1, keepdims=True)
            ds = dot(qd_ref[:, rs, :], dob, BTN) + ds * gl_ref[:, c:c + 1, :] - dot(w_ref[:, rs, :], dvnb, BTN)
        ds_scr[...] = ds

    big = jax.ShapeDtypeStruct((H, T, 128), F32)
    return pl.pallas_call(
        body, name="gdn_scan_bwd", grid=(H // GDN_HB, T // GDN_ROWS),
        in_specs=[hd, st, hd, hd, sq, hd, hd, gl_spec], out_specs=[hd, hd, sq, hd, hd, gl_spec],
        out_shape=[big, big, jax.ShapeDtypeStruct((H, T, GDN_CHUNK), F32), big, big,
                   jax.ShapeDtypeStruct((H, T // GDN_CHUNK, 128), F32)],
        scratch_shapes=[pltpu.VMEM((GDN_HB, 128, 128), F32)],
        compiler_params=_params("parallel", "arbitrary"),
    )(do, ss, vn, w, qk, qd, kd, gl)


def _gated_norm(o, z, nw):
    on = o * lax.rsqrt(jnp.mean(o * o, axis=-1, keepdims=True) + EPS) * nw
    return on * (z * _sigmoid(z))


def _gdn_post(o, proj, norm_w, *, tr=512):
    T = proj.shape[0]

    def body(o_ref, z_ref, n_ref, y_ref):
        y_ref[...] = _gated_norm(o_ref[0], z_ref[...], n_ref[...]).astype(BF16)

    return pl.pallas_call(
        body, name="gdn_post", grid=(T // tr, GDN_HEADS),
        in_specs=[pl.BlockSpec((1, tr, 128), lambda i, h: (h, i, 0)), pl.BlockSpec((tr, 128), lambda i, h: (i, 24 + h)),
                  pl.BlockSpec((1, 128), lambda i, h: (0, 0))],
        out_specs=pl.BlockSpec((tr, 128), lambda i, h: (i, h)),
        out_shape=jax.ShapeDtypeStruct((T, 1024), BF16), compiler_params=_params("parallel", "parallel"),
    )(o, proj, norm_w)


def _gdn_post_bwd(o, proj, norm_w, dy, *, tr=512):
    T = proj.shape[0]

    def body(o_ref, z_ref, n_ref, dy_ref, do_ref, dz_ref, dn_ref):
        first = (pl.program_id(0) == 0) & (pl.program_id(1) == 0)
        _, vjp = jax.vjp(_gated_norm, o_ref[0], z_ref[...], n_ref[...])
        do, dz, dn = vjp(dy_ref[...])
        do_ref[0] = do
        dz_ref[...] = dz.astype(BF16)

        @pl.when(first)
        def _():
            dn_ref[...] = dn

        @pl.when(jnp.logical_not(first))
        def _():
            dn_ref[...] += dn

    blk = pl.BlockSpec((tr, 128), lambda i, h: (i, h))
    hm = pl.BlockSpec((1, tr, 128), lambda i, h: (h, i, 0))
    vec = pl.BlockSpec((1, 128), lambda i, h: (0, 0))
    return pl.pallas_call(
        body, name="gdn_post_bwd", grid=(T // tr, GDN_HEADS),
        in_specs=[hm, pl.BlockSpec((tr, 128), lambda i, h: (i, 24 + h)), vec, blk], out_specs=[hm, blk, vec],
        out_shape=[jax.ShapeDtypeStruct((GDN_HEADS, T, 128), F32), jax.ShapeDtypeStruct((T, 1024), BF16),
                   jax.ShapeDtypeStruct((1, 128), F32)],
        compiler_params=_params("arbitrary", "arbitrary"),
    )(o, proj, norm_w, dy)


HBM_SPEC = pl.BlockSpec(memory_space=pltpu.HBM)


def _place():
    return lax.axis_index("x"), lax.axis_index("y"), lax.axis_index("c")


def _all_gather(vs, *, name):
    n = len(vs)

    def body(*refs):
        start, forward, finish = _gather_stages(refs[:n], refs[n:2 * n], *refs[2 * n:])
        start()
        forward()
        finish()

    return pl.pallas_call(
        body, name=name, out_shape=_gather_shapes(vs), in_specs=[HBM_SPEC] * n, out_specs=[HBM_SPEC] * n,
        scratch_shapes=_gather_sems(n),
    )(*vs)


def _gather_shapes(vs):
    return [jax.ShapeDtypeStruct((N_DEV,) + v.shape, v.dtype) for v in vs]


def _gather_sems(n):
    return [pltpu.SemaphoreType.DMA((7 * n,)), pltpu.SemaphoreType.DMA((7 * n,)), pltpu.SemaphoreType.DMA((n,))]


def _gather_stages(v_refs, out_refs, send_sems, recv_sems, local_sems):
    n = len(v_refs)
    x, y, c = _place()
    me, sibling = (x, y, c), (x, y, 1 - c)
    chips = [(1 - x, y), (x, 1 - y), (1 - x, 1 - y)]

    def copy(a, k, block, to, from_input=False):
        slot = out_refs[a].at[4 * block[0] + 2 * block[1] + block[2]]
        return pltpu.make_async_remote_copy(
            src_ref=v_refs[a] if from_input else slot, dst_ref=slot,
            send_sem=send_sems.at[7 * a + k], recv_sem=recv_sems.at[7 * a + k], device_id=to, device_id_type=MESH)

    def mine():
        return [pltpu.make_async_copy(v_refs[a], out_refs[a].at[4 * x + 2 * y + c], local_sems.at[a]) for a in range(n)]

    def first():
        return ([copy(a, 0, me, sibling, True) for a in range(n)]
                + [copy(a, 1 + j, me, (*chip, c), True) for j, chip in enumerate(chips) for a in range(n)])

    def passed():
        return [copy(a, 4 + j, (*chip, c), sibling) for j, chip in enumerate(chips) for a in range(n)]

    def start():
        for cp in mine() + first():
            cp.start()

    def forward():
        for j, chip in enumerate(chips):
            for a in range(n):
                copy(a, 1 + j, (*chip, c), me).wait_recv()
                copy(a, 4 + j, (*chip, c), sibling).start()

    def finish():
        for a in range(n):
            copy(a, 0, sibling, me).wait_recv()
            for j, chip in enumerate(chips):
                copy(a, 4 + j, (*chip, 1 - c), me).wait_recv()
        for cp in first() + passed():
            cp.wait_send()
        for cp in mine():
            cp.wait()

    return start, forward, finish


def _exchange_sibling(gs, *, name):
    n = len(gs)

    def body(*refs):
        g_refs, out_refs = refs[:n], refs[n:2 * n]
        send_sems, recv_sems = refs[2 * n:]
        x, y, c = _place()
        copies = [pltpu.make_async_remote_copy(
            src_ref=g_refs[a].at[k, 1 - c], dst_ref=out_refs[a].at[k], send_sem=send_sems.at[4 * a + k],
            recv_sem=recv_sems.at[4 * a + k], device_id=(x, y, 1 - c), device_id_type=MESH)
            for a in range(n) for k in range(4)]
        for cp in copies:
            cp.start()
        for cp in copies:
            cp.wait()

    return pl.pallas_call(
        body, name=name, out_shape=[jax.ShapeDtypeStruct((4,) + g.shape[2:], g.dtype) for g in gs],
        in_specs=[HBM_SPEC] * n, out_specs=[HBM_SPEC] * n,
        scratch_shapes=[pltpu.SemaphoreType.DMA((4 * n,)), pltpu.SemaphoreType.DMA((4 * n,))],
    )(*gs)


def _exchange_chips(pcs):
    n = len(pcs)

    def body(*refs):
        start, finish = _chips_stages(refs[:n], refs[n:2 * n], *refs[2 * n:])
        start()
        finish()

    return pl.pallas_call(
        body, name="rs_chips", out_shape=_chips_shapes(pcs), in_specs=[HBM_SPEC] * n, out_specs=[HBM_SPEC] * n,
        scratch_shapes=_chips_sems(n),
    )(*pcs)


def _chips_shapes(pcs):
    return [jax.ShapeDtypeStruct((3,) + pc.shape[1:], pc.dtype) for pc in pcs]


def _chips_sems(n):
    return [pltpu.SemaphoreType.DMA((3 * n,)), pltpu.SemaphoreType.DMA((3 * n,))]


def _chips_stages(p_refs, out_refs, send_sems, recv_sems):
    n = len(p_refs)
    x, y, c = _place()
    chips = [(1 - x, y), (x, 1 - y), (1 - x, 1 - y)]

    def copies():
        return [pltpu.make_async_remote_copy(
            src_ref=p_refs[a].at[2 * cx + cy], dst_ref=out_refs[a].at[j], send_sem=send_sems.at[3 * a + j],
            recv_sem=recv_sems.at[3 * a + j], device_id=(cx, cy, c), device_id_type=MESH)
            for j, (cx, cy) in enumerate(chips) for a in range(n)]

    def start():
        for cp in copies():
            cp.start()

    def finish():
        for cp in copies():
            cp.wait()

    return start, finish


def _chip_partial(place, g, got, *, tr, name):
    R, W = g.shape[2:]

    def body(pl_ref, g_ref, r_ref, o_ref):
        o_ref[...] = (g_ref[0] + r_ref[...]).astype(BF16)

    return pl.pallas_call(
        body, name=name, out_shape=jax.ShapeDtypeStruct((4, R, W), BF16),
        grid_spec=pltpu.PrefetchScalarGridSpec(
            num_scalar_prefetch=1, grid=(4, R // tr),
            in_specs=[pl.BlockSpec((1, 1, tr, W), lambda k, i, pr: (k, pr[2], i, 0)),
                      pl.BlockSpec((1, tr, W), lambda k, i, pr: (k, i, 0))],
            out_specs=pl.BlockSpec((1, tr, W), lambda k, i, pr: (k, i, 0))),
        compiler_params=_params("parallel", "parallel"),
    )(place, g, got)


def _adamw_math(g, w, m, v):
    m = ADAM_B1 * m + (1.0 - ADAM_B1) * g
    v = ADAM_B2 * v + (1.0 - ADAM_B2) * (g * g)
    m_hat = m / (1.0 - ADAM_B1 ** ADAM_STEP)
    v_hat = v / (1.0 - ADAM_B2 ** ADAM_STEP)
    return -ADAM_LR * (m_hat / (jnp.sqrt(v_hat) + ADAM_EPS) + ADAM_WD * w), m, v


def _adamw_shard(place, g, got1, got2, w, m, v, *, tr, name):
    R, W = w.shape

    def body(pl_ref, g_ref, r1_ref, r2_ref, w_ref, m_ref, v_ref, go_ref, d_ref, mo_ref, vo_ref):
        gs = g_ref[0, 0] + r1_ref[0]
        for j in range(3):
            gs = gs + r2_ref[j].astype(F32)
        go_ref[...] = gs
        d_ref[...], mo_ref[...], vo_ref[...] = _adamw_math(gs, w_ref[...], m_ref[...], v_ref[...])

    row = pl.BlockSpec((tr, W), lambda i, pr: (i, 0))
    out = jax.ShapeDtypeStruct((R, W), F32)
    return pl.pallas_call(
        body, name=name, out_shape=[out] * 4,
        grid_spec=pltpu.PrefetchScalarGridSpec(
            num_scalar_prefetch=1, grid=(R // tr,),
            in_specs=[pl.BlockSpec((1, 1, tr, W), lambda i, pr: (2 * pr[0] + pr[1], pr[2], i, 0)),
                      pl.BlockSpec((1, tr, W), lambda i, pr: (2 * pr[0] + pr[1], i, 0)),
                      pl.BlockSpec((3, tr, W), lambda i, pr: (0, i, 0)), row, row, row],
            out_specs=[row] * 4),
        compiler_params=_params("parallel"),
    )(place, g, got1, got2, w, m, v)


def _adamw_replicated(parts, w, m, v):
    R, W = w.shape

    def body(p_ref, w_ref, m_ref, v_ref, go_ref, d_ref, mo_ref, vo_ref):
        gs = p_ref[0]
        for j in range(1, N_DEV):
            gs = gs + p_ref[j]
        go_ref[...] = gs
        d_ref[...], mo_ref[...], vo_ref[...] = _adamw_math(gs, w_ref[...], m_ref[...], v_ref[...])

    full = pl.BlockSpec((R, W), lambda i: (0, 0))
    out = jax.ShapeDtypeStruct((R, W), F32)
    return pl.pallas_call(
        body, name="adamw_replicated", grid=(1,), out_shape=[out] * 4,
        in_specs=[pl.BlockSpec((N_DEV, R, W), lambda i: (0, 0, 0)), full, full, full], out_specs=[full] * 4,
        compiler_params=_params("arbitrary"),
    )(parts, w, m, v)


GROUPS = {
    "in_e": (256, 512, ((("w_in_e", None), 1024, 1024),)),
    "in_o": (514, 512, ((("w_in_o", None), 1024, 1024),)),
    "up0": (704, 256, ((("w_up", 0), 1024, 1024),)),
    "up1": (704, 256, ((("w_up", 1), 1024, 1024),)),
    "down0": (1024, 176, ((("w_down", 0), 352, 352),)),
    "down1": (1024, 176, ((("w_down", 1), 352, 352),)),
    "square": (1024, 256, ((("w_out_e", None), 128, 128), (("w_out_o", None), 128, 128), (("w_ple_gate", None), 256, 256))),
    "ple": (128, 512, ((("w_ple", None), 512, 512),)),
    "conv_f": (704, 8, ((("ffn_conv", None), 6, 8),)),
    "norm_o": (128, 8, ((("mix_norm_o", None), 1, 8),)),
    "conv_o": (384, 8, ((("conv_qkv_o", None), 4, 8),)),
}
SHARDED = tuple(dict.fromkeys(p[0][0] for g in GROUPS.values() for p in g[2]))
COLUMN_SHARDED = ("w_in_e", "w_in_o", "w_up", "ffn_conv", "w_ple", "conv_qkv_o", "mix_norm_o")
PACK_W = 1024
REPL_LAYOUT = (
    ("mix_norm_e", (1, 1024), 8), ("pool_w", (1, 4, 128, 128), 64), ("pool_scale", (1, 512), 8),
    ("a_log_o", (1, 8), 8), ("dt_bias_o", (1, 8), 8), ("gdn_norm_o", (1, 128), 8),
    ("ffn_norm", (2, 1024), 8), ("ple_norm", (2, 1024), 8), ("final_norm", (1024,), 8),
)


def _pad_rows(a, rows):
    extra = rows - a.shape[-2]
    return a if extra == 0 else jnp.pad(a, [(0, 0)] * (a.ndim - 2) + [(0, extra), (0, 0)])


def _group_rows(pieces, gname):
    parts = [_pad_rows(pieces[name], padded) for name, _, padded in GROUPS[gname][2]]
    return parts[0] if len(parts) == 1 else jnp.concatenate(parts, axis=-2)


def _ungroup_rows(buf, gname):
    out, r0 = {}, 0
    for name, rows, padded in GROUPS[gname][2]:
        out[name] = buf[..., r0:r0 + rows, :]
        r0 += padded
    return out


def _shard_major(name, gfull, n_layers):
    per_layer = []
    for g in gfull:
        if g.ndim == 3:
            per_layer.append(g)
        elif name in COLUMN_SHARDED:
            k = g.shape[0]
            per_layer.append(jnp.moveaxis(g.reshape(k, N_DEV, g.shape[1] // N_DEV), 1, 0))
        else:
            per_layer.append(g.reshape(N_DEV, g.shape[0] // N_DEV, -1))
    return per_layer[0] if n_layers == 1 else jnp.concatenate(per_layer, axis=1)


def _natural(name, gathered, n_layers):
    rows = gathered.shape[1] // n_layers
    out = []
    for layer in range(n_layers):
        piece = gathered[:, layer * rows:(layer + 1) * rows]
        if name in COLUMN_SHARDED:
            out.append(jnp.moveaxis(piece, 0, 1).reshape(rows, N_DEV * piece.shape[2]))
        else:
            out.append(piece.reshape(N_DEV * rows, piece.shape[2]))
    return out


def _rows(a, rows):
    flat = a.reshape(-1)
    return jnp.pad(flat, (0, rows * PACK_W - flat.shape[0])).reshape(rows, PACK_W)


def _pack_repl(vals):
    return jnp.concatenate([_rows(vals[name].reshape(shape), rows) for name, shape, rows in REPL_LAYOUT], axis=0)


def _unpack_repl(buf):
    out, r0 = {}, 0
    for name, shape, rows in REPL_LAYOUT:
        n = 1
        for s in shape:
            n *= s
        out[name] = buf[r0:r0 + rows].reshape(-1)[:n].reshape(shape)
        r0 += rows
    return out


WEIGHTS = ("mix_norm_e", "w_in_e", "pool_w", "pool_scale", "w_out_e", "mix_norm_o", "w_in_o", "conv_qkv_o", "a_log_o",
           "dt_bias_o", "gdn_norm_o", "w_out_o", "ffn_norm", "w_up", "ffn_conv", "w_down", "ple_norm", "w_ple_gate",
           "w_ple", "final_norm")


def _ffn_forward(x, norm_g, w_up, conv_w, w_down, tag):
    hn = _rms_fwd(x, norm_g, name="rms_ffn" + tag)
    up = _mm(hn, w_up, name="ffn_up" + tag)
    act = _ffn_act(up, conv_w)
    out = _mm(act, w_down, res=x, name="ffn_down" + tag)
    return out, (x, hn, up, act)


def _ffn_backward(dx, saved, norm_g, w_up, conv_w, w_down, tag):
    x, hn, up, act = saved
    dact = _mm(dx, w_down, tb=True, name="ffn_dact" + tag)
    d_w_down = _mm(act, dx, ta=True, name="ffn_dwdown" + tag)
    dgate, dval, dcg, dcv = _ffn_act_bwd(up, conv_w, dact)
    dhn = _mm(dgate, w_up[:, :FFN_DIM], tb=True, name="ffn_dhn_g" + tag)
    dhn = _mm(dval, w_up[:, FFN_DIM:], tb=True, res=dhn, name="ffn_dhn_v" + tag)
    halves = [_mm(hn, d, ta=True, name="ffn_dwup_" + side + tag)
              for side, d in (("g", dgate), ("v", dval))]
    d_w_up = jnp.concatenate([jnp.moveaxis(h.reshape(h.shape[0], 4, -1), 1, 0) for h in halves], axis=0)
    dx, d_norm = _rms_bwd(x, dhn, norm_g, dx, name="rms_ffn_bwd" + tag)
    return dx, d_norm, d_w_up, jnp.concatenate([dcg, dcv], axis=1), d_w_down


def _ple_forward(x, norm_g, w_gate, p, w_ple, tag):
    hn = _rms_fwd(x, norm_g, name="rms_ple" + tag)
    out, gl, pe = _ple_fwd(hn, w_gate, p, w_ple, x, name="ple_fwd" + tag)
    return out, (x, hn, gl, pe)


def _ple_backward(dx, saved, norm_g, w_gate, p, tag):
    x, hn, gl, pe = saved
    dpe, dgl = _ple_bwd(dx, gl, pe, name="ple_bwd" + tag)
    d_w_ple = _mm(p, dpe, ta=True, name="ple_dwple" + tag)
    d_w_gate = _mm(hn, dgl, ta=True, name="ple_dwgate" + tag)
    dhn = _mm(dgl, w_gate, tb=True, name="ple_dhn" + tag)
    dx, d_norm = _rms_bwd(x, dhn, norm_g, dx, name="rms_ple_bwd" + tag)
    return dx, d_norm, d_w_gate, d_w_ple


def kernel(x, p, mix_norm_e, w_in_e, pool_w, pool_scale, w_out_e, mix_norm_o, w_in_o, conv_qkv_o, a_log_o, dt_bias_o, gdn_norm_o, w_out_o, ffn_norm, w_up, ffn_conv, w_down, ple_norm, w_ple_gate, w_ple, final_norm, loss_target, m_mix_norm_e, m_w_in_e, m_pool_w, m_pool_scale, m_w_out_e, m_mix_norm_o, m_w_in_o, m_conv_qkv_o, m_a_log_o, m_dt_bias_o, m_gdn_norm_o, m_w_out_o, m_ffn_norm, m_w_up, m_ffn_conv, m_w_down, m_ple_norm, m_w_ple_gate, m_w_ple, m_final_norm, v_mix_norm_e, v_w_in_e, v_pool_w, v_pool_scale, v_w_out_e, v_mix_norm_o, v_w_in_o, v_conv_qkv_o, v_a_log_o, v_dt_bias_o, v_gdn_norm_o, v_w_out_o, v_ffn_norm, v_w_up, v_ffn_conv, v_w_down, v_ple_norm, v_w_ple_gate, v_w_ple, v_final_norm):
    given = dict(locals())
    place = jnp.stack(_place()).astype(jnp.int32)
    x0, tgt = x[0], loss_target[0]

    def pieces(prefix):
        out = {}
        for width, _, members in GROUPS.values():
            for (name, layer), rows, _ in members:
                a = given[prefix + name]
                out[(name, layer)] = (a if layer is None else a[layer]).reshape(rows, width)
        return out

    def flat2d(name):
        return given[name].reshape(-1, given[name].shape[-1])

    small = ("ffn_conv", "mix_norm_o", "conv_qkv_o")
    got = _all_gather([flat2d("w_in_e").astype(BF16)] + [_pad_rows(flat2d(k), 8) for k in small], name="ag_first")
    full = {("w_in_e", 0): _natural("w_in_e", got[0], 1)[0]}
    for i in range(2):
        full[("ffn_conv", i)] = _natural("ffn_conv", got[1][:, 3 * i:3 * i + 3], 1)[0]
    mix_norm_o_full = got[2][:, 0].reshape(1, D_MODEL)
    conv_qkv = _natural("conv_qkv_o", got[3][:, :4], 1)[0]
    alog_row = jnp.pad(a_log_o, ((0, 0), (8, 112)))
    dt_row = jnp.pad(dt_bias_o, ((0, 0), (8, 112)))
    lw = lambda name, i: full[(name, i)]

    h_e = _rms_fwd(x0, mix_norm_e, name="rms_mix_e")
    proj_e = _mm(h_e, lw("w_in_e", 0), name="in_e")
    pool_o = _pool_fwd(proj_e, pool_w[0], pool_scale)
    wide = ("w_out_e", "w_out_o", "w_down", "w_ple_gate")
    send = [jnp.concatenate([flat2d(k).astype(BF16) for k in wide], axis=0)]
    att_o, lsum, got = _sb_fwd(proj_e, gather=send + [flat2d(k).astype(BF16) for k in ("w_in_o", "w_up", "w_ple")])
    gathered, r0 = {"w_in_o": got[1], "w_up": got[2], "w_ple": got[3]}, 0
    for k in wide:
        gathered[k] = got[0][:, r0:r0 + flat2d(k).shape[0]]
        r0 += flat2d(k).shape[0]
    layers = {name: given[name].shape[0] if given[name].ndim == 3 else 1 for name in gathered}
    full.update({(name, i): w for name in gathered for i, w in enumerate(_natural(name, gathered[name], layers[name]))})
    w_in_o_full = full[("w_in_o", 0)]
    w_in_o_main = w_in_o_full[:, :4096]
    w_in_o_ba = jnp.pad(w_in_o_full[:, 4096:], ((0, 0), (0, 112)))
    mix_e = jnp.concatenate([pool_o, att_o.astype(BF16)], axis=1)
    x1 = _mm(mix_e, lw("w_out_e", 0), res=x0, name="out_e")
    x2, ffn0 = _ffn_forward(x1, ffn_norm[0:1], lw("w_up", 0), lw("ffn_conv", 0), lw("w_down", 0), "0")
    x3, ple0 = _ple_forward(x2, ple_norm[0:1], lw("w_ple_gate", 0), p[0, 0], lw("w_ple", 0), "0")

    h_o = _rms_fwd(x3, mix_norm_o_full, name="rms_mix_o")
    proj_o = _mm(h_o, w_in_o_main, name="in_o")
    ba = _mm(h_o, w_in_o_ba, name="in_o_ba")
    qkv = _gdn_pre(proj_o, conv_qkv)
    beta, g = _gdn_gate(ba, alog_row, dt_row)
    u, w_c, qk, qd, kd, gl = _gdn_local(qkv, g, beta)
    o, states, vnew = _gdn_scan(u, w_c, qk, qd, kd, gl)
    y_o = _gdn_post(o, proj_o, gdn_norm_o)
    x4 = _mm(y_o, lw("w_out_o", 0), res=x3, name="out_o")
    x5, ffn1 = _ffn_forward(x4, ffn_norm[1:2], lw("w_up", 1), lw("ffn_conv", 1), lw("w_down", 1), "1")
    x6, ple1 = _ple_forward(x5, ple_norm[1:2], lw("w_ple_gate", 1), p[1, 0], lw("w_ple", 1), "1")
    loss_row, dx, d_final = _final_loss(x6, final_norm.reshape(1, D_MODEL), tgt)

    grads, rgrads = {}, {}
    dx, d_ple1, grads[("w_ple_gate", 1)], grads[("w_ple", 1)] = _ple_backward(dx, ple1, ple_norm[1:2], lw("w_ple_gate", 1), p[1, 0], "1")
    dx, d_ffn1, grads[("w_up", 1)], grads[("ffn_conv", 1)], grads[("w_down", 1)] = _ffn_backward(
        dx, ffn1, ffn_norm[1:2], lw("w_up", 1), lw("ffn_conv", 1), lw("w_down", 1), "1")
    grads[("w_out_o", 0)] = _mm(y_o, dx, ta=True, name="dw_out_o")
    dy_o = _mm(dx, lw("w_out_o", 0), tb=True, name="dy_o")
    do, dz, rgrads["gdn_norm_o"] = _gdn_post_bwd(o, proj_o, gdn_norm_o, dy_o)
    du, dw_c, dqk, dqd, dkd, dgl = _gdn_scan_bwd(do, states, vnew, w_c, qk, qd, kd, gl)
    dqkv_heads, dg, dbeta = _gdn_local_bwd(qkv, g, beta, du, dw_c, dqk, dqd, dkd, dgl)
    dqkv, grads[("conv_qkv_o", 0)] = _gdn_pre_bwd(proj_o, conv_qkv, dqkv_heads)
    dba, d_alog, d_dt = _gdn_gate_bwd(ba, alog_row, dt_row, dbeta, dg)
    rgrads["a_log_o"], rgrads["dt_bias_o"] = d_alog[:, 8:16], d_dt[:, 8:16]
    dproj_o = jnp.concatenate([dqkv, dz], axis=1)
    dh = _mm(dproj_o, w_in_o_main, tb=True, name="dh_o")
    dh = _mm(dba, w_in_o_ba, tb=True, res=dh, name="dh_o_ba")
    grads[("w_in_o", 0)] = jnp.concatenate(
        [_mm(h_o, dproj_o, ta=True, name="dw_in_o"),
         _mm(h_o, dba, ta=True, name="dw_in_o_ba")[:, :16]], axis=1)
    dx, d_mix_o = _rms_bwd(x3, dh, mix_norm_o_full, dx, name="rms_mix_o_bwd")
    grads[("mix_norm_o", 0)] = d_mix_o

    dx, d_ple0, grads[("w_ple_gate", 0)], grads[("w_ple", 0)] = _ple_backward(dx, ple0, ple_norm[0:1], lw("w_ple_gate", 0), p[0, 0], "0")
    dx, d_ffn0, grads[("w_up", 0)], grads[("ffn_conv", 0)], grads[("w_down", 0)] = _ffn_backward(
        dx, ffn0, ffn_norm[0:1], lw("w_up", 0), lw("ffn_conv", 0), lw("w_down", 0), "0")
    grads[("w_out_e", 0)] = _mm(mix_e, dx, ta=True, name="dw_out_e")
    dmix = _mm(dx, lw("w_out_e", 0), tb=True, name="dmix_e")
    du_e, d_pool_w, rgrads["pool_scale"] = _pool_bwd(proj_e, dmix, pool_w[0], pool_scale)
    rgrads["pool_w"] = d_pool_w[None]

    def reduce_start(gnames, tag):
        smaj = {}
        for g in gnames:
            for (name, layer), _, _ in GROUPS[g][2]:
                of = [grads[(name, i)] for i in ((0, 1) if layer is None else (layer,)) if (name, i) in grads]
                smaj[(name, layer)] = _shard_major(name, of, len(of))
        gbuf = [_group_rows(smaj, g) for g in gnames]
        gbuf = [b.reshape((4, 2) + b.shape[1:]) for b in gbuf]
        got1 = _exchange_sibling(gbuf, name="rs_sibling" + tag)
        part = [_chip_partial(place, b, r, tr=GROUPS[g][1], name="rs_chip_partial_" + g)
                for g, b, r in zip(gnames, gbuf, got1)]
        return gbuf, got1, part

    early = tuple(g for g in GROUPS if g != "in_e")
    gbuf_e, got1_e, part_e = reduce_start(early, "_early")
    dq_e, dk_e, dv_e, got2_e = _sb_bwd(proj_e, lsum, dmix, exchange=part_e)
    dproj_e = jnp.concatenate([du_e, dq_e.astype(BF16), dk_e.astype(BF16), dv_e.astype(BF16)], axis=1)
    dh = _mm(dproj_e, lw("w_in_e", 0), tb=True, name="dh_e")
    grads[("w_in_e", 0)] = _mm(h_e, dproj_e, ta=True, name="dw_in_e")
    dx, rgrads["mix_norm_e"] = _rms_bwd(x0, dh, mix_norm_e, dx, name="rms_mix_e_bwd")
    rgrads["ffn_norm"] = jnp.concatenate([d_ffn0, d_ffn1], axis=0)
    rgrads["ple_norm"] = jnp.concatenate([d_ple0, d_ple1], axis=0)
    rgrads["final_norm"] = d_final.reshape(D_MODEL)

    gbuf_l, got1_l, part_l = reduce_start(("in_e",), "_late")
    got2_l = _exchange_chips(part_l)
    wloc, mloc, vloc = pieces(""), pieces("m_"), pieces("v_")
    sh_out = [{}, {}, {}, {}]
    for g, b, r1, r2 in zip(early + ("in_e",), gbuf_e + gbuf_l, list(got1_e) + list(got1_l),
                            list(got2_e) + list(got2_l)):
        res = _adamw_shard(place, b, r1, r2, _group_rows(wloc, g), _group_rows(mloc, g), _group_rows(vloc, g),
                           tr=GROUPS[g][1], name="adamw_" + g)
        for kind in range(4):
            sh_out[kind].update(_ungroup_rows(res[kind], g))

    (rparts,) = _all_gather([_pack_repl(rgrads)], name="ag_repl_grads")
    rp_out = _adamw_replicated(rparts, _pack_repl({n: given[n] for n, _, _ in REPL_LAYOUT}),
                               _pack_repl({n: given["m_" + n] for n, _, _ in REPL_LAYOUT}),
                               _pack_repl({n: given["v_" + n] for n, _, _ in REPL_LAYOUT}))
    rp_out = [_unpack_repl(b) for b in rp_out]

    def leaf(kind, name):
        if name in SHARDED:
            mine = sh_out[kind]
            whole = mine[(name, None)] if (name, None) in mine else jnp.stack([mine[(name, 0)], mine[(name, 1)]])
            return whole.reshape(given[name].shape)
        return rp_out[kind][name]

    loss = lax.psum(loss_row[0, 0], ("x", "y", "c"))
    outs = [loss, dx[None]]
    for kind in range(4):
        outs += [leaf(kind, n) for n in WEIGHTS]
    return tuple(outs)
```

```python
import functools

import jax
import jax.numpy as jnp
from jax import lax
from jax.experimental import pallas as pl
from jax.experimental.pallas import tpu as pltpu

F32 = jnp.float32
BF16 = jnp.bfloat16

D_MODEL = 1024
PLE_DIM = 256
POOL_WINDOWS = (2, 4, 8, 16)
POOL_WIDTH = 512
SB_HEAD_DIM = 64
SB_BLOCK = 1024
SB_KBLOCK = 256
GDN_HEADS = 8
GDN_HEAD_DIM = 128
GDN_CONV = 4
GDN_CHUNK = 64
FFN_DIM = 2816
FFN_CONV = 3
EPS = 1e-6
ADAM_LR, ADAM_B1, ADAM_B2, ADAM_EPS, ADAM_WD, ADAM_STEP = 0.001, 0.9, 0.999, 1e-08, 0.01, 10
N_DEV = 8
MESH = pl.DeviceIdType.MESH
VMEM_LIMIT = 56 * 1024 * 1024

NN = (((1,), (0,)), ((), ()))
NT = (((1,), (1,)), ((), ()))
TN = (((0,), (0,)), ((), ()))


def _params(*sem):
    return pltpu.CompilerParams(dimension_semantics=sem if sem else None, vmem_limit_bytes=VMEM_LIMIT)


def _dot(a, b, dims):
    return lax.dot_general(a.astype(BF16), b.astype(BF16), dims, preferred_element_type=F32)


def _iota(shape, axis):
    return lax.broadcasted_iota(jnp.int32, shape, axis)


MM_TILE, MM_TILE_11 = 1024, 1408


def _mm_tile(dim):
    if dim <= MM_TILE_11:
        return dim
    return MM_TILE if dim % MM_TILE == 0 else MM_TILE_11


def _mm(a, b, *, ta=False, tb=False, res=None, norm_gain=None, rms_bwd=None, name):
    M, K = (a.shape[1], a.shape[0]) if ta else a.shape
    N = b.shape[0] if tb else b.shape[1]
    tm, tn, tk = _mm_tile(M), _mm_tile(N), _mm_tile(K)
    if rms_bwd is not None:
        tm = min(tm, 512)
    assert M % tm == 0 and N % tn == 0 and K % tk == 0, (name, M, N, K, tm, tn, tk)
    assert (norm_gain is None and rms_bwd is None) or tn == N, name
    nk = K // tk
    dims = (((0 if ta else 1,), (1 if tb else 0,)), ((), ()))
    extra = [res] if res is not None else []
    vecs = [norm_gain] if norm_gain is not None else []
    if rms_bwd is not None:
        extra += [rms_bwd[0], rms_bwd[2]]
        vecs = [rms_bwd[1]]
    n_out = 1 if (norm_gain is None and rms_bwd is None) else 2

    def body(*refs):
        a_ref, b_ref = refs[:2]
        tiles = list(refs[2:2 + len(extra)])
        vec_refs = refs[2 + len(extra):2 + len(extra) + len(vecs)]
        outs = refs[2 + len(extra) + len(vecs):2 + len(extra) + len(vecs) + n_out]
        scr = refs[2 + len(extra) + len(vecs) + n_out:]
        p = _dot(a_ref[...], b_ref[...], dims)

        def fin(acc):
            if res is not None:
                acc = acc + tiles[0][...]
            if rms_bwd is not None:
                x_ref, dres_ref = tiles[-2:]
                xv = x_ref[...]
                r = lax.rsqrt(jnp.mean(xv * xv, axis=-1, keepdims=True) + EPS)
                xn = xv * r
                dgp = jnp.sum(acc * xn, axis=0, keepdims=True)
                dyg = acc * vec_refs[0][...]
                outs[0][...] = dres_ref[...] + r * (dyg - xn * jnp.mean(dyg * xn, axis=-1, keepdims=True))
                first = pl.program_id(0) == 0

                @pl.when(first)
                def _():
                    outs[1][...] = dgp

                @pl.when(jnp.logical_not(first))
                def _():
                    outs[1][...] += dgp
                return
            outs[0][...] = acc
            if norm_gain is not None:
                r = lax.rsqrt(jnp.mean(acc * acc, axis=-1, keepdims=True) + EPS)
                outs[1][...] = (acc * r * vec_refs[0][...]).astype(BF16)

        if nk == 1:
            fin(p)
        else:
            acc_ref = scr[0]
            k = pl.program_id(2)

            @pl.when(k == 0)
            def _():
                acc_ref[...] = p

            @pl.when(k > 0)
            def _():
                acc_ref[...] += p

            @pl.when(k == nk - 1)
            def _():
                fin(acc_ref[...])

    a_spec = pl.BlockSpec((tk, tm), lambda i, j, k: (k, i)) if ta else pl.BlockSpec((tm, tk), lambda i, j, k: (i, k))
    b_spec = pl.BlockSpec((tn, tk), lambda i, j, k: (j, k)) if tb else pl.BlockSpec((tk, tn), lambda i, j, k: (k, j))
    o_spec = pl.BlockSpec((tm, tn), lambda i, j, k: (i, j))
    v_spec = pl.BlockSpec((1, tn), lambda i, j, k: (0, j))
    out_specs, out_shape = [o_spec], [jax.ShapeDtypeStruct((M, N), F32)]
    if norm_gain is not None:
        out_specs, out_shape = out_specs + [o_spec], out_shape + [jax.ShapeDtypeStruct((M, N), BF16)]
    if rms_bwd is not None:
        out_specs, out_shape = out_specs + [v_spec], out_shape + [jax.ShapeDtypeStruct((1, N), F32)]
    out = pl.pallas_call(
        body, name=name, grid=(M // tm, N // tn, nk),
        in_specs=[a_spec, b_spec] + [o_spec] * len(extra) + [v_spec] * len(vecs), out_specs=out_specs,
        out_shape=out_shape, scratch_shapes=[pltpu.VMEM((tm, tn), F32)] if nk > 1 else [],
        compiler_params=_params("arbitrary" if rms_bwd is not None else "parallel", "parallel", "arbitrary"),
    )(a, b, *extra, *vecs)
    return out[0] if n_out == 1 else out


def _rms_fwd(x, gain, *, name, tr=512):
    T, Dm = x.shape

    def body(x_ref, g_ref, o_ref):
        xv = x_ref[...]
        r = lax.rsqrt(jnp.mean(xv * xv, axis=-1, keepdims=True) + EPS)
        o_ref[...] = (xv * r * g_ref[...]).astype(BF16)

    return pl.pallas_call(
        body, name=name, grid=(T // tr,),
        in_specs=[pl.BlockSpec((tr, Dm), lambda i: (i, 0)), pl.BlockSpec((1, Dm), lambda i: (0, 0))],
        out_specs=pl.BlockSpec((tr, Dm), lambda i: (i, 0)),
        out_shape=jax.ShapeDtypeStruct((T, Dm), BF16), compiler_params=_params("parallel"),
    )(x, gain)


def _final_loss(x, gain, target, *, tr=512):
    T, Dm = x.shape

    def body(x_ref, g_ref, t_ref, loss_ref, dx_ref, dg_ref):
        i = pl.program_id(0)
        xv = x_ref[...]
        g = g_ref[...]
        r = lax.rsqrt(jnp.mean(xv * xv, axis=-1, keepdims=True) + EPS)
        xn = xv * r
        err = xn * g - t_ref[...]
        lp = jnp.zeros((1, 128), F32) + 0.5 * jnp.sum(jnp.mean(err * err, axis=-1, keepdims=True))
        dy_v = err * (1.0 / Dm)
        dgp = jnp.sum(dy_v * xn, axis=0, keepdims=True)
        dyg = dy_v * g
        dx_ref[...] = r * (dyg - xn * jnp.mean(dyg * xn, axis=-1, keepdims=True))

        @pl.when(i == 0)
        def _():
            dg_ref[...] = dgp
            loss_ref[...] = lp

        @pl.when(i > 0)
        def _():
            dg_ref[...] += dgp
            loss_ref[...] += lp

    row = pl.BlockSpec((tr, Dm), lambda i: (i, 0))
    vec = pl.BlockSpec((1, Dm), lambda i: (0, 0))
    return pl.pallas_call(
        body, name="final_loss", grid=(T // tr,), in_specs=[row, vec, row],
        out_specs=[pl.BlockSpec((1, 128), lambda i: (0, 0)), row, vec],
        out_shape=[jax.ShapeDtypeStruct((1, 128), F32), jax.ShapeDtypeStruct((T, Dm), F32),
                   jax.ShapeDtypeStruct((1, Dm), F32)],
        compiler_params=_params("arbitrary"),
    )(x, gain, target)


def _prev_spec(tr, cb, pad, col):
    return pl.BlockSpec((pad, cb), lambda *g: (jnp.maximum(g[0] * (tr // pad) - 1, 0), col(*g)))


def _next_spec(tr, cb, pad, col, T):
    return pl.BlockSpec((pad, cb), lambda *g: (jnp.minimum((g[0] + 1) * (tr // pad), T // pad - 1), col(*g)))


def _conv_rows(x_ext, w_ref, K, pad, cs=slice(None)):
    y = w_ref[K - 1:K, cs] * x_ext
    for i in range(K - 1):
        y = y + w_ref[i:i + 1, cs] * pltpu.roll(x_ext, K - 1 - i, 0)
    return y[pad:]


def _pool_y(u_ext, g, i, tr):
    s = u_ext
    for sh in (1, 2, 4, 8)[:g + 1]:
        s = s + pltpu.roll(s, sh, 0)
    t = i * tr + _iota((tr, 128), 0)
    cnt = jnp.minimum(t + 1, POOL_WINDOWS[g]).astype(F32)
    return s[16:] / cnt - u_ext[16:]


def _pool_fwd(proj, pool_w, pool_scale, *, tr=512):
    T = proj.shape[0]

    def body(u_ref, uh_ref, w_ref, s_ref, o_ref):
        i = pl.program_id(0)
        uh = jnp.where(i > 0, uh_ref[...], 0.0)
        for g in range(4):
            cs = slice(128 * g, 128 * (g + 1))
            y = _pool_y(jnp.concatenate([uh[:, cs], u_ref[:, cs]], axis=0), g, i, tr)
            o_ref[:, cs] = (_dot(y, w_ref[g], NN) * s_ref[:, cs]).astype(BF16)

    return pl.pallas_call(
        body, name="pool_fwd", grid=(T // tr,),
        in_specs=[pl.BlockSpec((tr, 512), lambda i: (i, 0)), _prev_spec(tr, 512, 16, lambda i: 0),
                  pl.BlockSpec((4, 128, 128), lambda i: (0, 0, 0)), pl.BlockSpec((1, 512), lambda i: (0, 0))],
        out_specs=pl.BlockSpec((tr, 512), lambda i: (i, 0)),
        out_shape=jax.ShapeDtypeStruct((T, 512), BF16), compiler_params=_params("parallel"),
    )(proj, proj, pool_w, pool_scale)


def _pool_bwd(proj, dout, pool_w, pool_scale, *, tr=512):
    T = proj.shape[0]
    nb = T // tr

    def body(u_ref, uh_ref, d_ref, dn_ref, w_ref, s_ref, du_ref, dw_ref, ds_ref):
        i = pl.program_id(0)
        uh = jnp.where(i > 0, uh_ref[...], 0.0)
        dn = jnp.where(i < nb - 1, dn_ref[...], 0.0)
        t_ext = i * tr + _iota((tr + 16, 128), 0)
        for g in range(4):
            cs = slice(128 * g, 128 * (g + 1))
            sc = s_ref[:, cs]
            wg = w_ref[g]
            y = _pool_y(jnp.concatenate([uh[:, cs], u_ref[:, cs]], axis=0), g, i, tr)
            dg = d_ref[:, cs]
            dsp = jnp.sum(dg * _dot(y, wg, NN), axis=0, keepdims=True)
            dyw = dg * sc
            dwp = _dot(y, dyw, TN)
            dy_ext = _dot(jnp.concatenate([dyw, dn[:, cs] * sc], axis=0), wg, NT)
            cnt = jnp.minimum(t_ext + 1, POOL_WINDOWS[g]).astype(F32)
            s = dy_ext / cnt
            for sh in (1, 2, 4, 8)[:g + 1]:
                s = s + pltpu.roll(s, tr + 16 - sh, 0)
            du_ref[:, cs] = (s[:tr] - dy_ext[:tr]).astype(BF16)

            @pl.when(i == 0)
            def _():
                dw_ref[g] = dwp
                ds_ref[:, cs] = dsp

            @pl.when(i > 0)
            def _():
                dw_ref[g] += dwp
                ds_ref[:, cs] += dsp

    row = pl.BlockSpec((tr, 512), lambda i: (i, 0))
    return pl.pallas_call(
        body, name="pool_bwd", grid=(nb,),
        in_specs=[row, _prev_spec(tr, 512, 16, lambda i: 0), row, _next_spec(tr, 512, 16, lambda i: 0, T),
                  pl.BlockSpec((4, 128, 128), lambda i: (0, 0, 0)), pl.BlockSpec((1, 512), lambda i: (0, 0))],
        out_specs=[row, pl.BlockSpec((4, 128, 128), lambda i: (0, 0, 0)), pl.BlockSpec((1, 512), lambda i: (0, 0))],
        out_shape=[jax.ShapeDtypeStruct((T, 512), BF16), jax.ShapeDtypeStruct((4, 128, 128), F32),
                   jax.ShapeDtypeStruct((1, 512), F32)],
        compiler_params=_params("arbitrary"),
    )(proj, proj, dout, dout, pool_w, pool_scale)


def _split_dot(x, tri):
    hi = x.astype(BF16)
    lo = (x - hi.astype(F32)).astype(BF16)
    return (lax.dot_general(hi, tri, NN, preferred_element_type=F32)
            + lax.dot_general(lo, tri, NN, preferred_element_type=F32))


def _log1m(z):
    return -(jnp.maximum(z, 0.0) + jnp.log(1.0 + jnp.exp(-jnp.abs(z))))


def _sb_fwd(proj, gather=()):
    T = proj.shape[0]
    B, BK = min(SB_BLOCK, T), SB_KBLOCK
    R = B // BK
    nq, n = T // B, len(gather)
    scale = SB_HEAD_DIM ** -0.5

    def body(q_ref, k_ref, v_ref, *rest):
        o_ref, ls_ref = rest[n:n + 2]
        hp, i = pl.program_id(0), pl.program_id(1)
        if n:
            start, forward, finish = _gather_stages(rest[:n], rest[n + 2:2 * n + 2], *rest[2 * n + 2:])
            pl.when((hp == 0) & (i == 0))(start)
            pl.when((hp == 3) & (i == nq - 1))(forward)
        lane = _iota((1, 128), 1)
        tri_gt = (_iota((BK, BK), 0) > _iota((BK, BK), 1)).astype(BF16)
        row, col = _iota((B, BK), 0), _iota((B, BK), 1)
        qv = q_ref[...] * scale
        hms = [(lane >= 64 * h) & (lane < 64 * (h + 1)) for h in range(2)]
        qhs = [jnp.where(hm, qv, 0.0).astype(BF16) for hm in hms]

        def tile(j, carry, d):
            rows = pl.ds(pl.multiple_of(j * BK, BK), BK)
            kj = k_ref[rows, :].astype(BF16)
            vj = v_ref[rows, :].astype(BF16)
            r0 = 0 if d is None else BK * d
            valid = None if d is None else (col[r0:] < row[:B - r0])
            out = []
            for h in range(2):
                c, acc = carry[h]
                z = lax.dot_general(qhs[h][r0:], kj, NT, preferred_element_type=F32)
                lg = _log1m(z)
                if d is not None:
                    lg = jnp.where(valid, lg, 0.0)
                a = jnp.exp(z + lg + _split_dot(lg, tri_gt) + c[r0:])
                if d is not None:
                    a = jnp.where(valid, a, 0.0)
                upd = (c[r0:] + jnp.sum(lg, axis=1, keepdims=True),
                       acc[r0:] + lax.dot_general(a.astype(BF16), vj, NN, preferred_element_type=F32))
                out.append(upd if r0 == 0 else tuple(jnp.concatenate([old[:r0], new], axis=0)
                                                     for old, new in zip((c, acc), upd)))
            return tuple(out)

        zero = (jnp.zeros((B, 1), F32), jnp.zeros((B, 128), F32))
        carry = (zero, zero)
        for d in reversed(range(R)):
            carry = tile(i * R + d, carry, d)
        carry = lax.fori_loop(0, i * R, lambda s, cr: tile(i * R - 1 - s, cr, None), carry)
        o_ref[...] = jnp.where(hms[0], carry[0][1], carry[1][1])
        ls_ref[...] = jnp.where(hms[0], carry[0][0], carry[1][0])
        if n:
            pl.when((hp == 3) & (i == nq - 1))(finish)

    blk = pl.BlockSpec((B, 128), lambda hp, i: (i, hp))
    out = pl.pallas_call(
        body, name="sb_fwd", grid=(4, nq),
        in_specs=[pl.BlockSpec((B, 128), lambda hp, i: (i, 4 + hp)),
                  pl.BlockSpec((T, 128), lambda hp, i: (0, 8 + hp)),
                  pl.BlockSpec((T, 128), lambda hp, i: (0, 12 + hp))] + [HBM_SPEC] * n,
        out_specs=[blk, blk] + [HBM_SPEC] * n,
        out_shape=[jax.ShapeDtypeStruct((T, 512), F32)] * 2 + _gather_shapes(gather),
        scratch_shapes=_gather_sems(n) if n else [],
        compiler_params=_params("arbitrary", "arbitrary"),
    )(proj, proj, proj, *gather)
    return out[0], out[1], list(out[2:])


def _sb_bwd(proj, lsum, dout, exchange=()):
    T = proj.shape[0]
    B, BK = min(SB_BLOCK, T), SB_KBLOCK
    R = B // BK
    nq, n = T // B, len(exchange)
    scale = SB_HEAD_DIM ** -0.5

    def body(q_ref, k_ref, v_ref, do_ref, ls_ref, *rest):
        dq_ref, dk_ref, dv_ref = rest[n:n + 3]
        hp, i = pl.program_id(0), pl.program_id(1)
        if n:
            start, finish = _chips_stages(rest[:n], rest[n + 3:2 * n + 3], *rest[2 * n + 3:])
            pl.when((hp == 0) & (i == 0))(start)

        @pl.when(i == 0)
        def _():
            dk_ref[...] = jnp.zeros_like(dk_ref)
            dv_ref[...] = jnp.zeros_like(dv_ref)

        lane = _iota((1, 128), 1)
        tri_le = (_iota((BK, BK), 0) <= _iota((BK, BK), 1)).astype(BF16)
        tri_lt = (_iota((BK, BK), 0) < _iota((BK, BK), 1)).astype(BF16)
        row, col = _iota((B, BK), 0), _iota((B, BK), 1)
        qv = q_ref[...] * scale
        dov = do_ref[...]
        hms = [(lane >= 64 * h) & (lane < 64 * (h + 1)) for h in range(2)]
        qhs = [jnp.where(hm, qv, 0.0).astype(BF16) for hm in hms]
        dos = [jnp.where(hm, dov, 0.0).astype(BF16) for hm in hms]
        ltots = [ls_ref[:, 64 * h:64 * h + 1] for h in range(2)]

        def tile(j, carry, d):
            rows = pl.ds(pl.multiple_of(j * BK, BK), BK)
            kj = k_ref[rows, :].astype(BF16)
            vj = v_ref[rows, :].astype(BF16)
            diag = d is not None
            r0 = BK * d if diag else 0
            valid = (col[r0:] < row[:B - r0]) if diag else None
            out = []
            dkj = jnp.zeros((BK, 128), F32)
            dvj = jnp.zeros((BK, 128), F32)
            for h in range(2):
                lbef, ebef, dqa = carry[h]
                qh, do_h = qhs[h][r0:], dos[h][r0:]
                z = lax.dot_general(qh, kj, NT, preferred_element_type=F32)
                lg = _log1m(z)
                if diag:
                    lg = jnp.where(valid, lg, 0.0)
                a = jnp.exp(z + lg + (ltots[h][r0:] - lbef[r0:] - _split_dot(lg, tri_le)))
                if diag:
                    a = jnp.where(valid, a, 0.0)
                e = a * lax.dot_general(do_h, vj, NT, preferred_element_type=F32)
                dz = e * jnp.exp(lg) - jnp.exp(z + lg) * (ebef[r0:] + _split_dot(e, tri_lt))
                if diag:
                    dz = jnp.where(valid, dz, 0.0)
                dzb = dz.astype(BF16)
                dkj = dkj + lax.dot_general(dzb, qh, TN, preferred_element_type=F32)
                dvj = dvj + lax.dot_general(a.astype(BF16), do_h, TN, preferred_element_type=F32)
                upd = (lbef[r0:] + jnp.sum(lg, axis=1, keepdims=True), ebef[r0:] + jnp.sum(e, axis=1, keepdims=True),
                       dqa[r0:] + lax.dot_general(dzb, kj, NN, preferred_element_type=F32))
                out.append(upd if r0 == 0 else tuple(jnp.concatenate([old[:r0], new], axis=0)
                                                     for old, new in zip(carry[h], upd)))
            dk_ref[rows, :] += dkj
            dv_ref[rows, :] += dvj
            return tuple(out)

        zero = (jnp.zeros((B, 1), F32), jnp.zeros((B, 1), F32), jnp.zeros((B, 128), F32))
        carry = lax.fori_loop(0, i * R, lambda j, cr: tile(j, cr, None), (zero, zero))
        for d in range(R):
            carry = tile(i * R + d, carry, d)
        dq_ref[...] = jnp.where(hms[0], carry[0][2], carry[1][2]) * scale
        if n:
            pl.when((hp == 3) & (i == nq - 1))(finish)

    full = pl.BlockSpec((T, 128), lambda hp, i: (0, hp))
    blk = pl.BlockSpec((B, 128), lambda hp, i: (i, hp))
    out = pl.pallas_call(
        body, name="sb_bwd", grid=(4, nq),
        in_specs=[pl.BlockSpec((B, 128), lambda hp, i: (i, 4 + hp)),
                  pl.BlockSpec((T, 128), lambda hp, i: (0, 8 + hp)),
                  pl.BlockSpec((T, 128), lambda hp, i: (0, 12 + hp)),
                  pl.BlockSpec((B, 128), lambda hp, i: (i, 4 + hp)), blk] + [HBM_SPEC] * n,
        out_specs=[blk, full, full] + [HBM_SPEC] * n,
        out_shape=[jax.ShapeDtypeStruct((T, 512), F32)] * 3 + _chips_shapes(exchange),
        scratch_shapes=_chips_sems(n) if n else [],
        compiler_params=_params("arbitrary", "arbitrary"),
    )(proj, proj, proj, dout, lsum, *exchange)
    return out[0], out[1], out[2], list(out[3:])


def _sigmoid(x):
    return 1.0 / (1.0 + jnp.exp(-x))


def _silu_mul(cg, cv):
    return cg * _sigmoid(cg) * cv


def _ffn_act(up, conv_w, *, tr=512, cb=256):
    T, F2 = up.shape
    nc = F2 // 2 // cb
    K = FFN_CONV

    def body(g_ref, gh_ref, v_ref, vh_ref, wg_ref, wv_ref, o_ref):
        i = pl.program_id(0)
        gh = jnp.where(i > 0, gh_ref[...], 0.0)
        vh = jnp.where(i > 0, vh_ref[...], 0.0)
        cg = _conv_rows(jnp.concatenate([gh, g_ref[...]], axis=0), wg_ref, K, 8)
        cv = _conv_rows(jnp.concatenate([vh, v_ref[...]], axis=0), wv_ref, K, 8)
        o_ref[...] = _silu_mul(cg, cv).astype(BF16)

    return pl.pallas_call(
        body, name="ffn_act", grid=(T // tr, nc),
        in_specs=[pl.BlockSpec((tr, cb), lambda i, j: (i, j)), _prev_spec(tr, cb, 8, lambda i, j: j),
                  pl.BlockSpec((tr, cb), lambda i, j: (i, nc + j)), _prev_spec(tr, cb, 8, lambda i, j: nc + j),
                  pl.BlockSpec((K, cb), lambda i, j: (0, j)), pl.BlockSpec((K, cb), lambda i, j: (0, nc + j))],
        out_specs=pl.BlockSpec((tr, cb), lambda i, j: (i, j)),
        out_shape=jax.ShapeDtypeStruct((T, F2 // 2), BF16), compiler_params=_params("parallel", "parallel"),
    )(up, up, up, up, conv_w, conv_w)


def _conv_bwd_rows(dc_ext, x_ext, w_ref, K, tr, cs=slice(None)):
    n = tr + 8
    dx = w_ref[K - 1:K, cs] * dc_ext
    for i in range(K - 1):
        dx = dx + w_ref[i:i + 1, cs] * pltpu.roll(dc_ext, n - (K - 1 - i), 0)
    dc = dc_ext[:tr]
    dws = [jnp.sum(dc * pltpu.roll(x_ext, K - 1 - i, 0)[8:8 + tr], axis=0, keepdims=True) for i in range(K)]
    return dx[:tr], dws


def _acc_rows(ref, rows, first, cs=slice(None)):
    for i, r in enumerate(rows):
        @pl.when(first)
        def _():
            ref[i:i + 1, cs] = r

        @pl.when(jnp.logical_not(first))
        def _():
            ref[i:i + 1, cs] += r


def _ffn_act_bwd(up, conv_w, dact, *, tr=512, cb=256):
    T, F2 = up.shape
    F = F2 // 2
    nc, nb = F // cb, T // tr
    K = FFN_CONV

    def body(g_ref, gp_ref, gn_ref, v_ref, vp_ref, vn_ref, d_ref, dn_ref, wg_ref, wv_ref,
             dg_ref, dv_ref, dwg_ref, dwv_ref):
        i = pl.program_id(1)
        first, last = i == 0, i == nb - 1
        g_ext = jnp.concatenate([jnp.where(first, 0.0, gp_ref[...]), g_ref[...], jnp.where(last, 0.0, gn_ref[...])], axis=0)
        v_ext = jnp.concatenate([jnp.where(first, 0.0, vp_ref[...]), v_ref[...], jnp.where(last, 0.0, vn_ref[...])], axis=0)
        d_ext = jnp.concatenate([d_ref[...], jnp.where(last, 0.0, dn_ref[...])], axis=0)
        cg = _conv_rows(g_ext, wg_ref, K, 8)
        cv = _conv_rows(v_ext, wv_ref, K, 8)
        _, vjp = jax.vjp(_silu_mul, cg, cv)
        dcg, dcv = vjp(d_ext)
        dg, dwg = _conv_bwd_rows(dcg, g_ext, wg_ref, K, tr)
        dv, dwv = _conv_bwd_rows(dcv, v_ext, wv_ref, K, tr)
        dg_ref[...] = dg.astype(BF16)
        dv_ref[...] = dv.astype(BF16)
        _acc_rows(dwg_ref, dwg, first)
        _acc_rows(dwv_ref, dwv, first)

    blk = lambda off: pl.BlockSpec((tr, cb), lambda j, i: (i, off + j))
    prev = lambda off: pl.BlockSpec((8, cb), lambda j, i: (jnp.maximum(i * (tr // 8) - 1, 0), off + j))
    nxt = lambda off: pl.BlockSpec((8, cb), lambda j, i: (jnp.minimum((i + 1) * (tr // 8), T // 8 - 1), off + j))
    wsp = lambda off: pl.BlockSpec((K, cb), lambda j, i: (0, off + j))
    return pl.pallas_call(
        body, name="ffn_act_bwd", grid=(nc, nb),
        in_specs=[blk(0), prev(0), nxt(0), blk(nc), prev(nc), nxt(nc), blk(0), nxt(0), wsp(0), wsp(nc)],
        out_specs=[blk(0), blk(0), wsp(0), wsp(0)],
        out_shape=[jax.ShapeDtypeStruct((T, F), BF16)] * 2 + [jax.ShapeDtypeStruct((K, F), F32)] * 2,
        compiler_params=_params("parallel", "arbitrary"),
    )(up, up, up, up, up, up, dact, dact, conv_w, conv_w)


def _ple_fwd(hn, w_gate, p, w_ple, x, *, name, tm=1024, tn=512):
    T, Dm = x.shape
    tm = min(tm, T)

    def body(a_ref, b_ref, p_ref, wp_ref, x_ref, o_ref, gl_ref, pe_ref):
        gl = _dot(a_ref[...], b_ref[...], NN)
        pe = _dot(p_ref[...], wp_ref[...], NN)
        gl_ref[...] = gl
        pe_ref[...] = pe
        o_ref[...] = x_ref[...] + pe * _sigmoid(gl)

    o_spec = pl.BlockSpec((tm, tn), lambda i, j: (i, j))
    return pl.pallas_call(
        body, name=name, grid=(T // tm, Dm // tn),
        in_specs=[pl.BlockSpec((tm, Dm), lambda i, j: (i, 0)), pl.BlockSpec((Dm, tn), lambda i, j: (0, j)),
                  pl.BlockSpec((tm, PLE_DIM), lambda i, j: (i, 0)), pl.BlockSpec((PLE_DIM, tn), lambda i, j: (0, j)),
                  o_spec],
        out_specs=[o_spec] * 3, out_shape=[jax.ShapeDtypeStruct((T, Dm), F32)] * 3,
        compiler_params=_params("parallel", "parallel"),
    )(hn, w_gate, p, w_ple, x)


def _ple_bwd(dx, gl, pe, *, name, tr=512):
    T, Dm = dx.shape

    def body(dx_ref, gl_ref, pe_ref, dpe_ref, dgl_ref):
        g = _sigmoid(gl_ref[...])
        d = dx_ref[...]
        dpe_ref[...] = (d * g).astype(BF16)
        dgl_ref[...] = (d * pe_ref[...] * g * (1.0 - g)).astype(BF16)

    row = pl.BlockSpec((tr, Dm), lambda i: (i, 0))
    return pl.pallas_call(
        body, name=name, grid=(T // tr,), in_specs=[row] * 3, out_specs=[row] * 2,
        out_shape=[jax.ShapeDtypeStruct((T, Dm), BF16)] * 2, compiler_params=_params("parallel"),
    )(dx, gl, pe)


def _qkv_act(c, cb):
    s = c * _sigmoid(c)
    n = s * lax.rsqrt(jnp.sum(s * s, axis=-1, keepdims=True) + EPS)
    n = n * jnp.where(cb < GDN_HEADS, GDN_HEAD_DIM ** -0.5, 1.0)
    return jnp.where(cb < 2 * GDN_HEADS, n, s)


GDN_HPS = 4


def _gdn_pre(proj, conv_w, *, tr=512):
    T = proj.shape[0]
    K = GDN_CONV

    def body(x_ref, xh_ref, w_ref, o_ref):
        i, j = pl.program_id(0), pl.program_id(1)
        xh = jnp.where(i > 0, xh_ref[...], 0.0)
        for hh in range(GDN_HPS):
            cs = slice(128 * hh, 128 * (hh + 1))
            c = _conv_rows(jnp.concatenate([xh[:, cs], x_ref[:, cs]], axis=0), w_ref, K, 8, cs)
            o_ref[hh] = _qkv_act(c, GDN_HPS * j + hh)

    wide = 128 * GDN_HPS
    return pl.pallas_call(
        body, name="gdn_pre", grid=(T // tr, 24 // GDN_HPS),
        in_specs=[pl.BlockSpec((tr, wide), lambda i, j: (i, j)), _prev_spec(tr, wide, 8, lambda i, j: j),
                  pl.BlockSpec((K, wide), lambda i, j: (0, j))],
        out_specs=pl.BlockSpec((GDN_HPS, tr, 128), lambda i, j: (j, i, 0)),
        out_shape=jax.ShapeDtypeStruct((24, T, 128), F32), compiler_params=_params("parallel", "parallel"),
    )(proj, proj, conv_w)


def _gdn_pre_bwd(proj, conv_w, dqkv, *, tr=512):
    T = proj.shape[0]
    nb = T // tr
    K = GDN_CONV

    def body(x_ref, xp_ref, xn_ref, d_ref, dn_ref, w_ref, dx_ref, dw_ref):
        j, i = pl.program_id(0), pl.program_id(1)
        first, last = i == 0, i == nb - 1
        xp = jnp.where(first, 0.0, xp_ref[...])
        xn = jnp.where(last, 0.0, xn_ref[...])
        for hh in range(GDN_HPS):
            cs = slice(128 * hh, 128 * (hh + 1))
            x_ext = jnp.concatenate([xp[:, cs], x_ref[:, cs], xn[:, cs]], axis=0)
            d_ext = jnp.concatenate([d_ref[hh], jnp.where(last, 0.0, dn_ref[hh])], axis=0)
            c = _conv_rows(x_ext, w_ref, K, 8, cs)
            _, vjp = jax.vjp(lambda c_: _qkv_act(c_, GDN_HPS * j + hh), c)
            (dc,) = vjp(d_ext)
            dx, dws = _conv_bwd_rows(dc, x_ext, w_ref, K, tr, cs)
            dx_ref[:, cs] = dx.astype(BF16)
            _acc_rows(dw_ref, dws, first, cs)

    wide = 128 * GDN_HPS
    return pl.pallas_call(
        body, name="gdn_pre_bwd", grid=(24 // GDN_HPS, nb),
        in_specs=[pl.BlockSpec((tr, wide), lambda j, i: (i, j)),
                  pl.BlockSpec((8, wide), lambda j, i: (jnp.maximum(i * (tr // 8) - 1, 0), j)),
                  pl.BlockSpec((8, wide), lambda j, i: (jnp.minimum((i + 1) * (tr // 8), T // 8 - 1), j)),
                  pl.BlockSpec((GDN_HPS, tr, 128), lambda j, i: (j, i, 0)),
                  pl.BlockSpec((GDN_HPS, 8, 128), lambda j, i: (j, jnp.minimum((i + 1) * (tr // 8), T // 8 - 1), 0)),
                  pl.BlockSpec((K, wide), lambda j, i: (0, j))],
        out_specs=[pl.BlockSpec((tr, wide), lambda j, i: (i, j)), pl.BlockSpec((K, wide), lambda j, i: (0, j))],
        out_shape=[jax.ShapeDtypeStruct((T, 24 * 128), BF16), jax.ShapeDtypeStruct((K, 24 * 128), F32)],
        compiler_params=_params("parallel", "arbitrary"),
    )(proj, proj, proj, dqkv, dqkv, conv_w)


def _gate_fn(ba, alog_row, dt_row):
    lane = _iota((1, 128), 1)
    x = ba + dt_row
    sp = jnp.maximum(x, 0.0) + jnp.log(1.0 + jnp.exp(-jnp.abs(x)))
    return jnp.where(lane < GDN_HEADS, _sigmoid(ba), -jnp.exp(alog_row) * sp)


def _gdn_gate(ba, alog_row, dt_row, *, tr=512):
    T = ba.shape[0]

    def body(ba_ref, al_ref, dt_ref, b_ref, g_ref):
        val = _gate_fn(ba_ref[...], al_ref[...], dt_ref[...])
        for h in range(GDN_HEADS):
            b_ref[h] = val[:, h:h + 1]
            g_ref[h] = val[:, GDN_HEADS + h:GDN_HEADS + h + 1]

    vec = pl.BlockSpec((1, 128), lambda i: (0, 0))
    hm = pl.BlockSpec((GDN_HEADS, tr, 1), lambda i: (0, i, 0))
    return pl.pallas_call(
        body, name="gdn_gate", grid=(T // tr,), in_specs=[pl.BlockSpec((tr, 128), lambda i: (i, 0)), vec, vec],
        out_specs=[hm, hm], out_shape=[jax.ShapeDtypeStruct((GDN_HEADS, T, 1), F32)] * 2,
        compiler_params=_params("parallel"),
    )(ba, alog_row, dt_row)


def _gdn_gate_bwd(ba, alog_row, dt_row, dbeta, dg, *, tr=512):
    T = ba.shape[0]

    def body(ba_ref, al_ref, dt_ref, db_ref, dg_ref, dba_ref, dal_ref, ddt_ref):
        i = pl.program_id(0)
        lane = _iota((1, 128), 1)
        d = jnp.zeros((tr, 128), F32)
        for h in range(GDN_HEADS):
            d = d + jnp.where(lane == h, db_ref[h], 0.0) + jnp.where(lane == GDN_HEADS + h, dg_ref[h], 0.0)
        _, vjp = jax.vjp(_gate_fn, ba_ref[...], al_ref[...], dt_ref[...])
        dba, dal, ddt = vjp(d)
        dba_ref[...] = dba.astype(BF16)

        @pl.when(i == 0)
        def _():
            dal_ref[...] = dal
            ddt_ref[...] = ddt

        @pl.when(i > 0)
        def _():
            dal_ref[...] += dal
            ddt_ref[...] += ddt

    vec = pl.BlockSpec((1, 128), lambda i: (0, 0))
    hm = pl.BlockSpec((GDN_HEADS, tr, 1), lambda i: (0, i, 0))
    row = pl.BlockSpec((tr, 128), lambda i: (i, 0))
    return pl.pallas_call(
        body, name="gdn_gate_bwd", grid=(T // tr,), in_specs=[row, vec, vec, hm, hm], out_specs=[row, vec, vec],
        out_shape=[jax.ShapeDtypeStruct((T, 128), BF16), jax.ShapeDtypeStruct((1, 128), F32),
                   jax.ShapeDtypeStruct((1, 128), F32)],
        compiler_params=_params("arbitrary"),
    )(ba, alog_row, dt_row, dbeta, dg)


def _split3(x):
    x1 = x.astype(BF16)
    r = x - x1.astype(F32)
    x2 = r.astype(BF16)
    return x1, x2, (r - x2.astype(F32)).astype(BF16)


def _dot01(tri, x, dims):
    t = tri.astype(BF16)
    x1, x2, x3 = _split3(x)
    d = lambda xi: lax.dot_general(t, xi, dims, preferred_element_type=F32)
    return d(x1) + (d(x2) + d(x3))


def _dot3(a, b, dims):
    ah, al, _ = _split3(a)
    bh, bl, _ = _split3(b)
    d = lambda p, q: lax.dot_general(p, q, dims, preferred_element_type=F32)
    return d(ah, bh) + (d(ah, bl) + d(al, bh))


BNN = (((2,), (1,)), ((0,), (0,)))
BNT = (((2,), (2,)), ((0,), (0,)))
BTN = (((1,), (1,)), ((0,), (0,)))


@jax.custom_vjp
def _mm01(tri, x):
    return _dot01(tri, x, BNN)


def _mm01_fwd(tri, x):
    return _dot01(tri, x, BNN), tri


def _mm01_bwd(tri, ct):
    return jnp.zeros_like(tri), _dot01(tri, ct, BTN)


_mm01.defvjp(_mm01_fwd, _mm01_bwd)


def _unit_lower_inverse(a):
    C = a.shape[-1]
    eye = (_iota(a.shape, 1) == _iota(a.shape, 2)).astype(F32)
    pw = -a
    tinv = eye + pw
    for _ in range(5):
        pw = _dot3(pw, pw, BNN)
        tinv = tinv + _dot3(tinv, pw, BNN)
    return tinv


@jax.custom_vjp
def _unit_lower_solve(a, rv, rw):
    return _unit_lower_solve_fwd(a, rv, rw)[0]


def _unit_lower_solve_fwd(a, rv, rw):
    tinv = _unit_lower_inverse(a)
    sol = _dot3(tinv, jnp.concatenate([rv, rw], axis=2), BNN)
    n = rv.shape[2]
    return (sol[:, :, :n], sol[:, :, n:]), (tinv, sol)


def _unit_lower_solve_bwd(res, cts):
    tinv, sol = res
    n = cts[0].shape[2]
    d_rhs = _dot3(tinv, jnp.concatenate(cts, axis=2), BTN)
    return -_dot3(d_rhs, sol, BNT), d_rhs[:, :, :n], d_rhs[:, :, n:]


_unit_lower_solve.defvjp(_unit_lower_solve_fwd, _unit_lower_solve_bwd)


@jax.custom_vjp
def _mmb_nt(a, b):
    return _dot(a, b, BNT)


def _mmb_nt_fwd(a, b):
    return _dot(a, b, BNT), (a, b)


def _mmb_nt_bwd(res, ct):
    a, b = res
    return _dot(ct, b, BNN), _dot(ct, a, BTN)


_mmb_nt.defvjp(_mmb_nt_fwd, _mmb_nt_bwd)


def _gdn_chunk(q, k, v, gcol, bcol):
    nb, C = q.shape[0], GDN_CHUNK
    row, col = _iota((nb, C, C), 1), _iota((nb, C, C), 2)
    incl, strict = row >= col, row > col
    eye = (row == col).astype(F32)
    lower = incl.astype(F32)
    ones = jnp.ones((nb, C, C), F32)
    gwide = jnp.broadcast_to(gcol, (nb, C, GDN_HEAD_DIM))
    gc = _mm01(lower, gwide)
    gtot = _mm01(ones, gwide)
    gc_c = _mm01(lower, jnp.broadcast_to(gcol, (nb, C, C)))
    gc_s = _mm01(ones, gc_c * eye)
    decay = jnp.where(incl, jnp.exp(jnp.where(incl, gc_c - gc_s, 0.0)), 0.0)
    kb = k * bcol
    a = jnp.where(strict, _mmb_nt(kb, k) * decay, 0.0)
    egc = jnp.exp(gc)
    u, w = _unit_lower_solve(a, v * bcol, kb * egc)
    qk = jnp.where(incl, _mmb_nt(q, k) * decay, 0.0)
    return u, w, qk, q * egc, k * jnp.exp(gtot - gc), jnp.exp(jnp.sum(gwide, axis=1))


GDN_ROWS = 8 * GDN_CHUNK


GDN_LOCAL_CHUNKS = 16


def _gdn_specs(T):
    nch = min(GDN_LOCAL_CHUNKS, T // GDN_CHUNK)
    L = nch * GDN_CHUNK
    hd = lambda off: pl.BlockSpec((1, L, 128), lambda h, i: (off + h, i, 0))
    col = pl.BlockSpec((1, L, 1), lambda h, i: (h, i, 0))
    sq = pl.BlockSpec((1, L, GDN_CHUNK), lambda h, i: (h, i, 0))
    gl = pl.BlockSpec((1, nch, 128), lambda h, i: (h, i, 0))
    return nch, hd, col, sq, gl


def _gdn_local(qkv, g, beta):
    T = qkv.shape[1]
    nch, hd, col, sq, gl_spec = _gdn_specs(T)

    def body(q_ref, k_ref, v_ref, g_ref, b_ref, u_ref, w_ref, qk_ref, qd_ref, kd_ref, gl_ref):
        chunks = lambda ref: ref[0].reshape(nch, GDN_CHUNK, ref.shape[2])
        rows = lambda val: val.reshape(nch * GDN_CHUNK, val.shape[2])
        u, w, qk, qd, kd, gl = _gdn_chunk(chunks(q_ref), chunks(k_ref), chunks(v_ref), chunks(g_ref), chunks(b_ref))
        u_ref[0] = rows(u)
        w_ref[0] = rows(w).astype(BF16)
        qk_ref[0] = rows(qk).astype(BF16)
        qd_ref[0] = rows(qd).astype(BF16)
        kd_ref[0] = rows(kd).astype(BF16)
        gl_ref[0] = gl

    H = GDN_HEADS
    return pl.pallas_call(
        body, name="gdn_local", grid=(H, T // (nch * GDN_CHUNK)),
        in_specs=[hd(0), hd(H), hd(2 * H), col, col],
        out_specs=[hd(0), hd(0), sq, hd(0), hd(0), gl_spec],
        out_shape=[jax.ShapeDtypeStruct((H, T, 128), F32), jax.ShapeDtypeStruct((H, T, 128), BF16),
                   jax.ShapeDtypeStruct((H, T, GDN_CHUNK), BF16), jax.ShapeDtypeStruct((H, T, 128), BF16),
                   jax.ShapeDtypeStruct((H, T, 128), BF16), jax.ShapeDtypeStruct((H, T // GDN_CHUNK, 128), F32)],
        compiler_params=_params("parallel", "parallel"),
    )(qkv, qkv, qkv, g, beta)


def _gdn_local_bwd(qkv, g, beta, du, dw, dqk, dqd, dkd, dgl):
    T = qkv.shape[1]
    nch, hd, col, sq, gl_spec = _gdn_specs(T)

    def body(q_ref, k_ref, v_ref, g_ref, b_ref, du_ref, dw_ref, dqk_ref, dqd_ref, dkd_ref, dgl_ref,
             dqkv_ref, dg_ref, db_ref):
        chunks = lambda ref: ref[0].reshape(nch, GDN_CHUNK, ref.shape[2])
        rows = lambda val: val.reshape(nch * GDN_CHUNK, val.shape[2])
        _, vjp = jax.vjp(_gdn_chunk, chunks(q_ref), chunks(k_ref), chunks(v_ref), chunks(g_ref), chunks(b_ref))
        dq, dk, dv, dg, db = vjp((chunks(du_ref), chunks(dw_ref), chunks(dqk_ref), chunks(dqd_ref), chunks(dkd_ref),
                                  dgl_ref[0]))
        dqkv_ref[0, 0] = rows(dq)
        dqkv_ref[1, 0] = rows(dk)
        dqkv_ref[2, 0] = rows(dv)
        dg_ref[0] = rows(dg)
        db_ref[0] = rows(db)

    H = GDN_HEADS
    small = jax.ShapeDtypeStruct((H, T, 1), F32)
    dqkv, dg, db = pl.pallas_call(
        body, name="gdn_local_bwd", grid=(H, T // (nch * GDN_CHUNK)),
        in_specs=[hd(0), hd(H), hd(2 * H), col, col, hd(0), hd(0), sq, hd(0), hd(0), gl_spec],
        out_specs=[pl.BlockSpec((3, 1, nch * GDN_CHUNK, 128), lambda h, i: (0, h, i, 0)), col, col],
        out_shape=[jax.ShapeDtypeStruct((3, H, T, 128), F32), small, small],
        compiler_params=_params("parallel", "parallel"),
    )(qkv, qkv, qkv, g, beta, du, dw, dqk, dqd, dkd, dgl)
    return dqkv.reshape(3 * H, T, 128), dg, db


GDN_HB = 4


def _gdn_scan_specs(T, rev):
    nb = T // GDN_ROWS
    blk = (lambda i: nb - 1 - i) if rev else (lambda i: i)
    hd = pl.BlockSpec((GDN_HB, GDN_ROWS, 128), lambda h, i: (h, blk(i), 0))
    sq = pl.BlockSpec((GDN_HB, GDN_ROWS, GDN_CHUNK), lambda h, i: (h, blk(i), 0))
    gl = pl.BlockSpec((GDN_HB, 8, 128), lambda h, i: (h, blk(i), 0))
    st = pl.BlockSpec((GDN_HB, 8, 128, 128), lambda h, i: (h, blk(i), 0, 0))
    return hd, sq, gl, st


def _gdn_scan(u, w, qk, qd, kd, gl):
    H, T, _ = u.shape
    hd, sq, gl_spec, st = _gdn_scan_specs(T, False)

    def body(u_ref, w_ref, qk_ref, qd_ref, kd_ref, gl_ref, o_ref, ss_ref, vn_ref, s_scr):
        @pl.when(pl.program_id(1) == 0)
        def _():
            s_scr[...] = jnp.zeros_like(s_scr)

        dot = lambda a, b, dims: lax.dot_general(a, b, dims, preferred_element_type=F32)
        s = s_scr[...]
        for c in range(8):
            rs = slice(GDN_CHUNK * c, GDN_CHUNK * (c + 1))
            ss_ref[:, c] = s
            sb = s.astype(BF16)
            vn = u_ref[:, rs, :] - dot(w_ref[:, rs, :], sb, BNN)
            vnb = vn.astype(BF16)
            o_ref[:, rs, :] = dot(qd_ref[:, rs, :], sb, BNN) + dot(qk_ref[:, rs, :], vnb, BNN)
            vn_ref[:, rs, :] = vnb
            s = s * gl_ref[:, c:c + 1, :] + dot(kd_ref[:, rs, :], vnb, BTN)
        s_scr[...] = s

    return pl.pallas_call(
        body, name="gdn_scan", grid=(H // GDN_HB, T // GDN_ROWS),
        in_specs=[hd, hd, sq, hd, hd, gl_spec], out_specs=[hd, st, hd],
        out_shape=[jax.ShapeDtypeStruct((H, T, 128), F32), jax.ShapeDtypeStruct((H, T // GDN_CHUNK, 128, 128), F32),
                   jax.ShapeDtypeStruct((H, T, 128), BF16)],
        scratch_shapes=[pltpu.VMEM((GDN_HB, 128, 128), F32)],
        compiler_params=_params("parallel", "arbitrary"),
    )(u, w, qk, qd, kd, gl)


def _gdn_scan_bwd(do, ss, vn, w, qk, qd, kd, gl):
    H, T, _ = do.shape
    hd, sq, gl_spec, st = _gdn_scan_specs(T, True)

    def body(do_ref, ss_ref, vn_ref, w_ref, qk_ref, qd_ref, kd_ref, gl_ref,
             du_ref, dw_ref, dqk_ref, dqd_ref, dkd_ref, dgl_ref, ds_scr):
        @pl.when(pl.program_id(1) == 0)
        def _():
            ds_scr[...] = jnp.zeros_like(ds_scr)

        dot = lambda a, b, dims: lax.dot_general(a, b, dims, preferred_element_type=F32)
        ds = ds_scr[...]
        for c in reversed(range(8)):
            rs = slice(GDN_CHUNK * c, GDN_CHUNK * (c + 1))
            s = ss_ref[:, c]
            sb, dsb = s.astype(BF16), ds.astype(BF16)
            dob = do_ref[:, rs, :].astype(BF16)
            vnb = vn_ref[:, rs, :]
            dvn = dot(qk_ref[:, rs, :], dob, BTN) + dot(kd_ref[:, rs, :], dsb, BNN)
            dvnb = dvn.astype(BF16)
            du_ref[:, rs, :] = dvn
            dw_ref[:, rs, :] = -dot(dvnb, sb, BNT)
            dqk_ref[:, rs, :] = dot(dob, vnb, BNT)
            dqd_ref[:, rs, :] = dot(dob, sb, BNT)
            dkd_ref[:, rs, :] = dot(vnb, dsb, BNT)
            dgl_ref[:, c:c + 1, :] = jnp.sum(ds * s, axis=1, keepdims=True)
            ds = dot(qd_ref[:, rs, :], dob, BTN) + ds * gl_ref[:, c:c + 1, :] - dot(w_ref[:, rs, :], dvnb, BTN)
        ds_scr[...] = ds

    big = jax.ShapeDtypeStruct((H, T, 128), F32)
    return pl.pallas_call(
        body, name="gdn_scan_bwd", grid=(H // GDN_HB, T // GDN_ROWS),
        in_specs=[hd, st, hd, hd, sq, hd, hd, gl_spec], out_specs=[hd, hd, sq, hd, hd, gl_spec],
        out_shape=[big, big, jax.ShapeDtypeStruct((H, T, GDN_CHUNK), F32), big, big,
                   jax.ShapeDtypeStruct((H, T // GDN_CHUNK, 128), F32)],
        scratch_shapes=[pltpu.VMEM((GDN_HB, 128, 128), F32)],
        compiler_params=_params("parallel", "arbitrary"),
    )(do, ss, vn, w, qk, qd, kd, gl)


def _gated_norm(o, z, nw):
    on = o * lax.rsqrt(jnp.mean(o * o, axis=-1, keepdims=True) + EPS) * nw
    return on * (z * _sigmoid(z))


def _gdn_post(o, proj, norm_w, *, tr=512):
    T = proj.shape[0]

    def body(o_ref, z_ref, n_ref, y_ref):
        y_ref[...] = _gated_norm(o_ref[0], z_ref[...], n_ref[...]).astype(BF16)

    return pl.pallas_call(
        body, name="gdn_post", grid=(T // tr, GDN_HEADS),
        in_specs=[pl.BlockSpec((1, tr, 128), lambda i, h: (h, i, 0)), pl.BlockSpec((tr, 128), lambda i, h: (i, 24 + h)),
                  pl.BlockSpec((1, 128), lambda i, h: (0, 0))],
        out_specs=pl.BlockSpec((tr, 128), lambda i, h: (i, h)),
        out_shape=jax.ShapeDtypeStruct((T, 1024), BF16), compiler_params=_params("parallel", "parallel"),
    )(o, proj, norm_w)


def _gdn_post_bwd(o, proj, norm_w, dy, *, tr=512):
    T = proj.shape[0]

    def body(o_ref, z_ref, n_ref, dy_ref, do_ref, dz_ref, dn_ref):
        first = (pl.program_id(0) == 0) & (pl.program_id(1) == 0)
        _, vjp = jax.vjp(_gated_norm, o_ref[0], z_ref[...], n_ref[...])
        do, dz, dn = vjp(dy_ref[...])
        do_ref[0] = do
        dz_ref[...] = dz.astype(BF16)

        @pl.when(first)
        def _():
            dn_ref[...] = dn

        @pl.when(jnp.logical_not(first))
        def _():
            dn_ref[...] += dn

    blk = pl.BlockSpec((tr, 128), lambda i, h: (i, h))
    hm = pl.BlockSpec((1, tr, 128), lambda i, h: (h, i, 0))
    vec = pl.BlockSpec((1, 128), lambda i, h: (0, 0))
    return pl.pallas_call(
        body, name="gdn_post_bwd", grid=(T // tr, GDN_HEADS),
        in_specs=[hm, pl.BlockSpec((tr, 128), lambda i, h: (i, 24 + h)), vec, blk], out_specs=[hm, blk, vec],
        out_shape=[jax.ShapeDtypeStruct((GDN_HEADS, T, 128), F32), jax.ShapeDtypeStruct((T, 1024), BF16),
                   jax.ShapeDtypeStruct((1, 128), F32)],
        compiler_params=_params("arbitrary", "arbitrary"),
    )(o, proj, norm_w, dy)


HBM_SPEC = pl.BlockSpec(memory_space=pltpu.HBM)


def _place():
    return lax.axis_index("x"), lax.axis_index("y"), lax.axis_index("c")


def _all_gather(vs, *, name):
    n = len(vs)

    def body(*refs):
        start, forward, finish = _gather_stages(refs[:n], refs[n:2 * n], *refs[2 * n:])
        start()
        forward()
        finish()

    return pl.pallas_call(
        body, name=name, out_shape=_gather_shapes(vs), in_specs=[HBM_SPEC] * n, out_specs=[HBM_SPEC] * n,
        scratch_shapes=_gather_sems(n),
    )(*vs)


def _gather_shapes(vs):
    return [jax.ShapeDtypeStruct((N_DEV,) + v.shape, v.dtype) for v in vs]


def _gather_sems(n):
    return [pltpu.SemaphoreType.DMA((7 * n,)), pltpu.SemaphoreType.DMA((7 * n,)), pltpu.SemaphoreType.DMA((n,))]


def _gather_stages(v_refs, out_refs, send_sems, recv_sems, local_sems):
    n = len(v_refs)
    x, y, c = _place()
    me, sibling = (x, y, c), (x, y, 1 - c)
    chips = [(1 - x, y), (x, 1 - y), (1 - x, 1 - y)]

    def copy(a, k, block, to, from_input=False):
        slot = out_refs[a].at[4 * block[0] + 2 * block[1] + block[2]]
        return pltpu.make_async_remote_copy(
            src_ref=v_refs[a] if from_input else slot, dst_ref=slot,
            send_sem=send_sems.at[7 * a + k], recv_sem=recv_sems.at[7 * a + k], device_id=to, device_id_type=MESH)

    def mine():
        return [pltpu.make_async_copy(v_refs[a], out_refs[a].at[4 * x + 2 * y + c], local_sems.at[a]) for a in range(n)]

    def first():
        return ([copy(a, 0, me, sibling, True) for a in range(n)]
                + [copy(a, 1 + j, me, (*chip, c), True) for j, chip in enumerate(chips) for a in range(n)])

    def passed():
        return [copy(a, 4 + j, (*chip, c), sibling) for j, chip in enumerate(chips) for a in range(n)]

    def start():
        for cp in mine() + first():
            cp.start()

    def forward():
        for j, chip in enumerate(chips):
            for a in range(n):
                copy(a, 1 + j, (*chip, c), me).wait_recv()
                copy(a, 4 + j, (*chip, c), sibling).start()

    def finish():
        for a in range(n):
            copy(a, 0, sibling, me).wait_recv()
            for j, chip in enumerate(chips):
                copy(a, 4 + j, (*chip, 1 - c), me).wait_recv()
        for cp in first() + passed():
            cp.wait_send()
        for cp in mine():
            cp.wait()

    return start, forward, finish


def _exchange_sibling(gs, *, name):
    n = len(gs)

    def body(*refs):
        g_refs, out_refs = refs[:n], refs[n:2 * n]
        send_sems, recv_sems = refs[2 * n:]
        x, y, c = _place()
        copies = [pltpu.make_async_remote_copy(
            src_ref=g_refs[a].at[k, 1 - c], dst_ref=out_refs[a].at[k], send_sem=send_sems.at[4 * a + k],
            recv_sem=recv_sems.at[4 * a + k], device_id=(x, y, 1 - c), device_id_type=MESH)
            for a in range(n) for k in range(4)]
        for cp in copies:
            cp.start()
        for cp in copies:
            cp.wait()

    return pl.pallas_call(
        body, name=name, out_shape=[jax.ShapeDtypeStruct((4,) + g.shape[2:], g.dtype) for g in gs],
        in_specs=[HBM_SPEC] * n, out_specs=[HBM_SPEC] * n,
        scratch_shapes=[pltpu.SemaphoreType.DMA((4 * n,)), pltpu.SemaphoreType.DMA((4 * n,))],
    )(*gs)


def _exchange_chips(pcs):
    n = len(pcs)

    def body(*refs):
        start, finish = _chips_stages(refs[:n], refs[n:2 * n], *refs[2 * n:])
        start()
        finish()

    return pl.pallas_call(
        body, name="rs_chips", out_shape=_chips_shapes(pcs), in_specs=[HBM_SPEC] * n, out_specs=[HBM_SPEC] * n,
        scratch_shapes=_chips_sems(n),
    )(*pcs)


def _chips_shapes(pcs):
    return [jax.ShapeDtypeStruct((3,) + pc.shape[1:], pc.dtype) for pc in pcs]


def _chips_sems(n):
    return [pltpu.SemaphoreType.DMA((3 * n,)), pltpu.SemaphoreType.DMA((3 * n,))]


def _chips_stages(p_refs, out_refs, send_sems, recv_sems):
    n = len(p_refs)
    x, y, c = _place()
    chips = [(1 - x, y), (x, 1 - y), (1 - x, 1 - y)]

    def copies():
        return [pltpu.make_async_remote_copy(
            src_ref=p_refs[a].at[2 * cx + cy], dst_ref=out_refs[a].at[j], send_sem=send_sems.at[3 * a + j],
            recv_sem=recv_sems.at[3 * a + j], device_id=(cx, cy, c), device_id_type=MESH)
            for j, (cx, cy) in enumerate(chips) for a in range(n)]

    def start():
        for cp in copies():
            cp.start()

    def finish():
        for cp in copies():
            cp.wait()

    return start, finish


def _chip_partial(place, g, got, *, tr, name):
    R, W = g.shape[2:]

    def body(pl_ref, g_ref, r_ref, o_ref):
        o_ref[...] = (g_ref[0] + r_ref[...]).astype(BF16)

    return pl.pallas_call(
        body, name=name, out_shape=jax.ShapeDtypeStruct((4, R, W), BF16),
        grid_spec=pltpu.PrefetchScalarGridSpec(
            num_scalar_prefetch=1, grid=(4, R // tr),
            in_specs=[pl.BlockSpec((1, 1, tr, W), lambda k, i, pr: (k, pr[2], i, 0)),
                      pl.BlockSpec((1, tr, W), lambda k, i, pr: (k, i, 0))],
            out_specs=pl.BlockSpec((1, tr, W), lambda k, i, pr: (k, i, 0))),
        compiler_params=_params("parallel", "parallel"),
    )(place, g, got)


def _adamw_math(g, w, m, v):
    m = ADAM_B1 * m + (1.0 - ADAM_B1) * g
    v = ADAM_B2 * v + (1.0 - ADAM_B2) * (g * g)
    m_hat = m / (1.0 - ADAM_B1 ** ADAM_STEP)
    v_hat = v / (1.0 - ADAM_B2 ** ADAM_STEP)
    return -ADAM_LR * (m_hat / (jnp.sqrt(v_hat) + ADAM_EPS) + ADAM_WD * w), m, v


def _adamw_shard(place, g, got1, got2, w, m, v, *, tr, name):
    R, W = w.shape

    def body(pl_ref, g_ref, r1_ref, r2_ref, w_ref, m_ref, v_ref, go_ref, d_ref, mo_ref, vo_ref):
        gs = g_ref[0, 0] + r1_ref[0]
        for j in range(3):
            gs = gs + r2_ref[j].astype(F32)
        go_ref[...] = gs
        d_ref[...], mo_ref[...], vo_ref[...] = _adamw_math(gs, w_ref[...], m_ref[...], v_ref[...])

    row = pl.BlockSpec((tr, W), lambda i, pr: (i, 0))
    out = jax.ShapeDtypeStruct((R, W), F32)
    return pl.pallas_call(
        body, name=name, out_shape=[out] * 4,
        grid_spec=pltpu.PrefetchScalarGridSpec(
            num_scalar_prefetch=1, grid=(R // tr,),
            in_specs=[pl.BlockSpec((1, 1, tr, W), lambda i, pr: (2 * pr[0] + pr[1], pr[2], i, 0)),
                      pl.BlockSpec((1, tr, W), lambda i, pr: (2 * pr[0] + pr[1], i, 0)),
                      pl.BlockSpec((3, tr, W), lambda i, pr: (0, i, 0)), row, row, row],
            out_specs=[row] * 4),
        compiler_params=_params("parallel"),
    )(place, g, got1, got2, w, m, v)


def _adamw_replicated(parts, w, m, v):
    R, W = w.shape

    def body(p_ref, w_ref, m_ref, v_ref, go_ref, d_ref, mo_ref, vo_ref):
        gs = p_ref[0]
        for j in range(1, N_DEV):
            gs = gs + p_ref[j]
        go_ref[...] = gs
        d_ref[...], mo_ref[...], vo_ref[...] = _adamw_math(gs, w_ref[...], m_ref[...], v_ref[...])

    full = pl.BlockSpec((R, W), lambda i: (0, 0))
    out = jax.ShapeDtypeStruct((R, W), F32)
    return pl.pallas_call(
        body, name="adamw_replicated", grid=(1,), out_shape=[out] * 4,
        in_specs=[pl.BlockSpec((N_DEV, R, W), lambda i: (0, 0, 0)), full, full, full], out_specs=[full] * 4,
        compiler_params=_params("arbitrary"),
    )(parts, w, m, v)


GROUPS = {
    "in_e": (256, 512, ((("w_in_e", None), 1024, 1024),)),
    "in_o": (514, 512, ((("w_in_o", None), 1024, 1024),)),
    "up0": (704, 256, ((("w_up", 0), 1024, 1024),)),
    "up1": (704, 256, ((("w_up", 1), 1024, 1024),)),
    "down0": (1024, 176, ((("w_down", 0), 352, 352),)),
    "down1": (1024, 176, ((("w_down", 1), 352, 352),)),
    "square": (1024, 256, ((("w_out_e", None), 128, 128), (("w_out_o", None), 128, 128), (("w_ple_gate", None), 256, 256))),
    "ple": (128, 512, ((("w_ple", None), 512, 512),)),
    "conv_f": (704, 8, ((("ffn_conv", None), 6, 8),)),
    "norm_o": (128, 8, ((("mix_norm_o", None), 1, 8),)),
    "conv_o": (384, 8, ((("conv_qkv_o", None), 4, 8),)),
}
SHARDED = tuple(dict.fromkeys(p[0][0] for g in GROUPS.values() for p in g[2]))
COLUMN_SHARDED = ("w_in_e", "w_in_o", "w_up", "ffn_conv", "w_ple", "conv_qkv_o", "mix_norm_o")
PACK_W = 1024
REPL_LAYOUT = (
    ("mix_norm_e", (1, 1024), 8), ("pool_w", (1, 4, 128, 128), 64), ("pool_scale", (1, 512), 8),
    ("a_log_o", (1, 8), 8), ("dt_bias_o", (1, 8), 8), ("gdn_norm_o", (1, 128), 8),
    ("ffn_norm", (2, 1024), 8), ("ple_norm", (2, 1024), 8), ("final_norm", (1024,), 8),
)


def _pad_rows(a, rows):
    extra = rows - a.shape[-2]
    return a if extra == 0 else jnp.pad(a, [(0, 0)] * (a.ndim - 2) + [(0, extra), (0, 0)])


def _group_rows(pieces, gname):
    parts = [_pad_rows(pieces[name], padded) for name, _, padded in GROUPS[gname][2]]
    return parts[0] if len(parts) == 1 else jnp.concatenate(parts, axis=-2)


def _ungroup_rows(buf, gname):
    out, r0 = {}, 0
    for name, rows, padded in GROUPS[gname][2]:
        out[name] = buf[..., r0:r0 + rows, :]
        r0 += padded
    return out


def _shard_major(name, gfull, n_layers):
    per_layer = []
    for g in gfull:
        if g.ndim == 3:
            per_layer.append(g)
        elif name in COLUMN_SHARDED:
            k = g.shape[0]
            per_layer.append(jnp.moveaxis(g.reshape(k, N_DEV, g.shape[1] // N_DEV), 1, 0))
        else:
            per_layer.append(g.reshape(N_DEV, g.shape[0] // N_DEV, -1))
    return per_layer[0] if n_layers == 1 else jnp.concatenate(per_layer, axis=1)


def _natural(name, gathered, n_layers):
    rows = gathered.shape[1] // n_layers
    out = []
    for layer in range(n_layers):
        piece = gathered[:, layer * rows:(layer + 1) * rows]
        if name in COLUMN_SHARDED:
            out.append(jnp.moveaxis(piece, 0, 1).reshape(rows, N_DEV * piece.shape[2]))
        else:
            out.append(piece.reshape(N_DEV * rows, piece.shape[2]))
    return out


def _rows(a, rows):
    flat = a.reshape(-1)
    return jnp.pad(flat, (0, rows * PACK_W - flat.shape[0])).reshape(rows, PACK_W)


def _pack_repl(vals):
    return jnp.concatenate([_rows(vals[name].reshape(shape), rows) for name, shape, rows in REPL_LAYOUT], axis=0)


def _unpack_repl(buf):
    out, r0 = {}, 0
    for name, shape, rows in REPL_LAYOUT:
        n = 1
        for s in shape:
            n *= s
        out[name] = buf[r0:r0 + rows].reshape(-1)[:n].reshape(shape)
        r0 += rows
    return out


WEIGHTS = ("mix_norm_e", "w_in_e", "pool_w", "pool_scale", "w_out_e", "mix_norm_o", "w_in_o", "conv_qkv_o", "a_log_o",
           "dt_bias_o", "gdn_norm_o", "w_out_o", "ffn_norm", "w_up", "ffn_conv", "w_down", "ple_norm", "w_ple_gate",
           "w_ple", "final_norm")


def _ffn_forward(x, hn, w_up, conv_w, w_down, next_gain, tag):
    up = _mm(hn, w_up, name="ffn_up" + tag)
    act = _ffn_act(up, conv_w)
    out, out_n = _mm(act, w_down, res=x, norm_gain=next_gain, name="ffn_down" + tag)
    return out, out_n, (x, hn, up, act)


def _ffn_backward(dx, saved, norm_g, w_up, conv_w, w_down, tag):
    x, hn, up, act = saved
    dact = _mm(dx, w_down, tb=True, name="ffn_dact" + tag)
    d_w_down = _mm(act, dx, ta=True, name="ffn_dwdown" + tag)
    dgate, dval, dcg, dcv = _ffn_act_bwd(up, conv_w, dact)
    dhn = _mm(dgate, w_up[:, :FFN_DIM], tb=True, name="ffn_dhn_g" + tag)
    dx_in, d_norm = _mm(dval, w_up[:, FFN_DIM:], tb=True, res=dhn, rms_bwd=(x, norm_g, dx), name="ffn_dhn_v" + tag)
    halves = [_mm(hn, d, ta=True, name="ffn_dwup_" + side + tag)
              for side, d in (("g", dgate), ("v", dval))]
    d_w_up = jnp.concatenate([jnp.moveaxis(h.reshape(h.shape[0], 4, -1), 1, 0) for h in halves], axis=0)
    return dx_in, d_norm, d_w_up, jnp.concatenate([dcg, dcv], axis=1), d_w_down


def _ple_forward(x, hn, w_gate, p, w_ple, tag):
    out, gl, pe = _ple_fwd(hn, w_gate, p, w_ple, x, name="ple_fwd" + tag)
    return out, (x, hn, gl, pe)


def _ple_backward(dx, saved, norm_g, w_gate, p, tag):
    x, hn, gl, pe = saved
    dpe, dgl = _ple_bwd(dx, gl, pe, name="ple_bwd" + tag)
    d_w_ple = _mm(p, dpe, ta=True, name="ple_dwple" + tag)
    d_w_gate = _mm(hn, dgl, ta=True, name="ple_dwgate" + tag)
    dx, d_norm = _mm(dgl, w_gate, tb=True, rms_bwd=(x, norm_g, dx), name="ple_dhn" + tag)
    return dx, d_norm, d_w_gate, d_w_ple


def kernel(x, p, mix_norm_e, w_in_e, pool_w, pool_scale, w_out_e, mix_norm_o, w_in_o, conv_qkv_o, a_log_o, dt_bias_o, gdn_norm_o, w_out_o, ffn_norm, w_up, ffn_conv, w_down, ple_norm, w_ple_gate, w_ple, final_norm, loss_target, m_mix_norm_e, m_w_in_e, m_pool_w, m_pool_scale, m_w_out_e, m_mix_norm_o, m_w_in_o, m_conv_qkv_o, m_a_log_o, m_dt_bias_o, m_gdn_norm_o, m_w_out_o, m_ffn_norm, m_w_up, m_ffn_conv, m_w_down, m_ple_norm, m_w_ple_gate, m_w_ple, m_final_norm, v_mix_norm_e, v_w_in_e, v_pool_w, v_pool_scale, v_w_out_e, v_mix_norm_o, v_w_in_o, v_conv_qkv_o, v_a_log_o, v_dt_bias_o, v_gdn_norm_o, v_w_out_o, v_ffn_norm, v_w_up, v_ffn_conv, v_w_down, v_ple_norm, v_w_ple_gate, v_w_ple, v_final_norm):
    given = dict(locals())
    place = jnp.stack(_place()).astype(jnp.int32)
    x0, tgt = x[0], loss_target[0]

    def pieces(prefix):
        out = {}
        for width, _, members in GROUPS.values():
            for (name, layer), rows, _ in members:
                a = given[prefix + name]
                out[(name, layer)] = (a if layer is None else a[layer]).reshape(rows, width)
        return out

    def flat2d(name):
        return given[name].reshape(-1, given[name].shape[-1])

    small = ("ffn_conv", "mix_norm_o", "conv_qkv_o")
    got = _all_gather([flat2d("w_in_e").astype(BF16)] + [_pad_rows(flat2d(k), 8) for k in small], name="ag_first")
    full = {("w_in_e", 0): _natural("w_in_e", got[0], 1)[0]}
    for i in range(2):
        full[("ffn_conv", i)] = _natural("ffn_conv", got[1][:, 3 * i:3 * i + 3], 1)[0]
    mix_norm_o_full = got[2][:, 0].reshape(1, D_MODEL)
    conv_qkv = _natural("conv_qkv_o", got[3][:, :4], 1)[0]
    alog_row = jnp.pad(a_log_o, ((0, 0), (8, 112)))
    dt_row = jnp.pad(dt_bias_o, ((0, 0), (8, 112)))
    lw = lambda name, i: full[(name, i)]

    h_e = _rms_fwd(x0, mix_norm_e, name="rms_mix_e")
    proj_e = _mm(h_e, lw("w_in_e", 0), name="in_e")
    pool_o = _pool_fwd(proj_e, pool_w[0], pool_scale)
    wide = ("w_out_e", "w_out_o", "w_down", "w_ple_gate")
    send = [jnp.concatenate([flat2d(k).astype(BF16) for k in wide], axis=0)]
    att_o, lsum, got = _sb_fwd(proj_e, gather=send + [flat2d(k).astype(BF16) for k in ("w_in_o", "w_up", "w_ple")])
    gathered, r0 = {"w_in_o": got[1], "w_up": got[2], "w_ple": got[3]}, 0
    for k in wide:
        gathered[k] = got[0][:, r0:r0 + flat2d(k).shape[0]]
        r0 += flat2d(k).shape[0]
    layers = {name: given[name].shape[0] if given[name].ndim == 3 else 1 for name in gathered}
    full.update({(name, i): w for name in gathered for i, w in enumerate(_natural(name, gathered[name], layers[name]))})
    w_in_o_full = full[("w_in_o", 0)]
    w_in_o_main = w_in_o_full[:, :4096]
    w_in_o_ba = jnp.pad(w_in_o_full[:, 4096:], ((0, 0), (0, 112)))
    mix_e = jnp.concatenate([pool_o, att_o.astype(BF16)], axis=1)
    x1, hf0 = _mm(mix_e, lw("w_out_e", 0), res=x0, norm_gain=ffn_norm[0:1], name="out_e")
    x2, hp0, ffn0 = _ffn_forward(x1, hf0, lw("w_up", 0), lw("ffn_conv", 0), lw("w_down", 0), ple_norm[0:1], "0")
    x3, ple0 = _ple_forward(x2, hp0, lw("w_ple_gate", 0), p[0, 0], lw("w_ple", 0), "0")

    h_o = _rms_fwd(x3, mix_norm_o_full, name="rms_mix_o")
    proj_o = _mm(h_o, w_in_o_main, name="in_o")
    ba = _mm(h_o, w_in_o_ba, name="in_o_ba")
    qkv = _gdn_pre(proj_o, conv_qkv)
    beta, g = _gdn_gate(ba, alog_row, dt_row)
    u, w_c, qk, qd, kd, gl = _gdn_local(qkv, g, beta)
    o, states, vnew = _gdn_scan(u, w_c, qk, qd, kd, gl)
    y_o = _gdn_post(o, proj_o, gdn_norm_o)
    x4, hf1 = _mm(y_o, lw("w_out_o", 0), res=x3, norm_gain=ffn_norm[1:2], name="out_o")
    x5, hp1, ffn1 = _ffn_forward(x4, hf1, lw("w_up", 1), lw("ffn_conv", 1), lw("w_down", 1), ple_norm[1:2], "1")
    x6, ple1 = _ple_forward(x5, hp1, lw("w_ple_gate", 1), p[1, 0], lw("w_ple", 1), "1")
    loss_row, dx, d_final = _final_loss(x6, final_norm.reshape(1, D_MODEL), tgt)

    grads, rgrads = {}, {}
    dx, d_ple1, grads[("w_ple_gate", 1)], grads[("w_ple", 1)] = _ple_backward(dx, ple1, ple_norm[1:2], lw("w_ple_gate", 1), p[1, 0], "1")
    dx, d_ffn1, grads[("w_up", 1)], grads[("ffn_conv", 1)], grads[("w_down", 1)] = _ffn_backward(
        dx, ffn1, ffn_norm[1:2], lw("w_up", 1), lw("ffn_conv", 1), lw("w_down", 1), "1")
    grads[("w_out_o", 0)] = _mm(y_o, dx, ta=True, name="dw_out_o")
    dy_o = _mm(dx, lw("w_out_o", 0), tb=True, name="dy_o")
    do, dz, rgrads["gdn_norm_o"] = _gdn_post_bwd(o, proj_o, gdn_norm_o, dy_o)
    du, dw_c, dqk, dqd, dkd, dgl = _gdn_scan_bwd(do, states, vnew, w_c, qk, qd, kd, gl)
    dqkv_heads, dg, dbeta = _gdn_local_bwd(qkv, g, beta, du, dw_c, dqk, dqd, dkd, dgl)
    dqkv, grads[("conv_qkv_o", 0)] = _gdn_pre_bwd(proj_o, conv_qkv, dqkv_heads)
    dba, d_alog, d_dt = _gdn_gate_bwd(ba, alog_row, dt_row, dbeta, dg)
    rgrads["a_log_o"], rgrads["dt_bias_o"] = d_alog[:, 8:16], d_dt[:, 8:16]
    dproj_o = jnp.concatenate([dqkv, dz], axis=1)
    dh = _mm(dproj_o, w_in_o_main, tb=True, name="dh_o")
    dx_o, d_mix_o = _mm(dba, w_in_o_ba, tb=True, res=dh, rms_bwd=(x3, mix_norm_o_full, dx), name="dh_o_ba")
    grads[("w_in_o", 0)] = jnp.concatenate(
        [_mm(h_o, dproj_o, ta=True, name="dw_in_o"),
         _mm(h_o, dba, ta=True, name="dw_in_o_ba")[:, :16]], axis=1)
    dx = dx_o
    grads[("mix_norm_o", 0)] = d_mix_o

    dx, d_ple0, grads[("w_ple_gate", 0)], grads[("w_ple", 0)] = _ple_backward(dx, ple0, ple_norm[0:1], lw("w_ple_gate", 0), p[0, 0], "0")
    dx, d_ffn0, grads[("w_up", 0)], grads[("ffn_conv", 0)], grads[("w_down", 0)] = _ffn_backward(
        dx, ffn0, ffn_norm[0:1], lw("w_up", 0), lw("ffn_conv", 0), lw("w_down", 0), "0")
    grads[("w_out_e", 0)] = _mm(mix_e, dx, ta=True, name="dw_out_e")
    dmix = _mm(dx, lw("w_out_e", 0), tb=True, name="dmix_e")
    du_e, d_pool_w, rgrads["pool_scale"] = _pool_bwd(proj_e, dmix, pool_w[0], pool_scale)
    rgrads["pool_w"] = d_pool_w[None]

    def reduce_start(gnames, tag):
        smaj = {}
        for g in gnames:
            for (name, layer), _, _ in GROUPS[g][2]:
                of = [grads[(name, i)] for i in ((0, 1) if layer is None else (layer,)) if (name, i) in grads]
                smaj[(name, layer)] = _shard_major(name, of, len(of))
        gbuf = [_group_rows(smaj, g) for g in gnames]
        gbuf = [b.reshape((4, 2) + b.shape[1:]) for b in gbuf]
        got1 = _exchange_sibling(gbuf, name="rs_sibling" + tag)
        part = [_chip_partial(place, b, r, tr=GROUPS[g][1], name="rs_chip_partial_" + g)
                for g, b, r in zip(gnames, gbuf, got1)]
        return gbuf, got1, part

    early = tuple(g for g in GROUPS if g != "in_e")
    gbuf_e, got1_e, part_e = reduce_start(early, "_early")
    dq_e, dk_e, dv_e, got2_e = _sb_bwd(proj_e, lsum, dmix, exchange=part_e)
    dproj_e = jnp.concatenate([du_e, dq_e.astype(BF16), dk_e.astype(BF16), dv_e.astype(BF16)], axis=1)
    grads[("w_in_e", 0)] = _mm(h_e, dproj_e, ta=True, name="dw_in_e")
    dx, rgrads["mix_norm_e"] = _mm(dproj_e, lw("w_in_e", 0), tb=True, rms_bwd=(x0, mix_norm_e, dx), name="dh_e")
    rgrads["ffn_norm"] = jnp.concatenate([d_ffn0, d_ffn1], axis=0)
    rgrads["ple_norm"] = jnp.concatenate([d_ple0, d_ple1], axis=0)
    rgrads["final_norm"] = d_final.reshape(D_MODEL)

    gbuf_l, got1_l, part_l = reduce_start(("in_e",), "_late")
    got2_l = _exchange_chips(part_l)
    wloc, mloc, vloc = pieces(""), pieces("m_"), pieces("v_")
    sh_out = [{}, {}, {}, {}]
    for g, b, r1, r2 in zip(early + ("in_e",), gbuf_e + gbuf_l, list(got1_e) + list(got1_l),
                            list(got2_e) + list(got2_l)):
        res = _adamw_shard(place, b, r1, r2, _group_rows(wloc, g), _group_rows(mloc, g), _group_rows(vloc, g),
                           tr=GROUPS[g][1], name="adamw_" + g)
        for kind in range(4):
            sh_out[kind].update(_ungroup_rows(res[kind], g))

    (rparts,) = _all_gather([_pack_repl(rgrads)], name="ag_repl_grads")
    rp_out = _adamw_replicated(rparts, _pack_repl({n: given[n] for n, _, _ in REPL_LAYOUT}),
                               _pack_repl({n: given["m_" + n] for n, _, _ in REPL_LAYOUT}),
                               _pack_repl({n: given["v_" + n] for n, _, _ in REPL_LAYOUT}))
    rp_out = [_unpack_repl(b) for b in rp_out]

    def leaf(kind, name):
        if name in SHARDED:
            mine = sh_out[kind]
            whole = mine[(name, None)] if (name, None) in mine else jnp.stack([mine[(name, 0)], mine[(name, 1)]])
            return whole.reshape(given[name].shape)
        return rp_out[kind][name]

    loss = lax.psum(loss_row[0, 0], ("x", "y", "c"))
    outs = [loss, dx[None]]
    for kind in range(4):
        outs += [leaf(kind, n) for n in WEIGHTS]
    return tuple(outs)
```

```python
import jax
import jax.numpy as jnp
from jax import lax
from jax.experimental import pallas as pl
from jax.experimental.pallas import tpu as pltpu

F32 = jnp.float32
BF16 = jnp.bfloat16

D_MODEL = 1024
PLE_DIM = 256
POOL_WINDOWS = (2, 4, 8, 16)
SB_HEAD_DIM = 64
SB_BLOCK = 1024
SB_KBLOCK = 256
GDN_HEADS = 8
GDN_HEAD_DIM = 128
GDN_CONV = 4
GDN_CHUNK = 64
FFN_DIM = 2816
FFN_CONV = 3
EPS = 1e-6
ADAM_LR, ADAM_B1, ADAM_B2, ADAM_EPS, ADAM_WD, ADAM_STEP = 0.001, 0.9, 0.999, 1e-08, 0.01, 10
N_DEV = 8
MESH = pl.DeviceIdType.MESH
VMEM_LIMIT = 56 * 1024 * 1024

NN = (((1,), (0,)), ((), ()))
NT = (((1,), (1,)), ((), ()))
TN = (((0,), (0,)), ((), ()))


def _params(*sem):
    return pltpu.CompilerParams(dimension_semantics=sem if sem else None, vmem_limit_bytes=VMEM_LIMIT)


def _dot(a, b, dims):
    return lax.dot_general(a.astype(BF16), b.astype(BF16), dims, preferred_element_type=F32)


def _iota(shape, axis):
    return lax.broadcasted_iota(jnp.int32, shape, axis)


MM_TILE, MM_TILE_11 = 1024, 1408


def _mm_tile(dim):
    if dim <= MM_TILE_11:
        return dim
    return MM_TILE if dim % MM_TILE == 0 else MM_TILE_11


def _mm(a, b, *, ta=False, tb=False, res=None, norm_gain=None, rms_bwd=None, name):
    M, K = (a.shape[1], a.shape[0]) if ta else a.shape
    N = b.shape[0] if tb else b.shape[1]
    tm, tn, tk = _mm_tile(M), _mm_tile(N), _mm_tile(K)
    if rms_bwd is not None:
        tm = min(tm, 512)
    assert M % tm == 0 and N % tn == 0 and K % tk == 0, (name, M, N, K, tm, tn, tk)
    assert (norm_gain is None and rms_bwd is None) or tn == N, name
    nk = K // tk
    dims = (((0 if ta else 1,), (1 if tb else 0,)), ((), ()))
    extra = [res] if res is not None else []
    vecs = [norm_gain] if norm_gain is not None else []
    if rms_bwd is not None:
        extra += [rms_bwd[0], rms_bwd[2]]
        vecs = [rms_bwd[1]]
    n_out = 1 if (norm_gain is None and rms_bwd is None) else 2

    def body(*refs):
        a_ref, b_ref = refs[:2]
        tiles = list(refs[2:2 + len(extra)])
        vec_refs = refs[2 + len(extra):2 + len(extra) + len(vecs)]
        outs = refs[2 + len(extra) + len(vecs):2 + len(extra) + len(vecs) + n_out]
        scr = refs[2 + len(extra) + len(vecs) + n_out:]
        p = _dot(a_ref[...], b_ref[...], dims)

        def fin(acc):
            if res is not None:
                acc = acc + tiles[0][...]
            if rms_bwd is not None:
                x_ref, dres_ref = tiles[-2:]
                xv = x_ref[...]
                r = lax.rsqrt(jnp.mean(xv * xv, axis=-1, keepdims=True) + EPS)
                xn = xv * r
                dgp = jnp.sum(acc * xn, axis=0, keepdims=True)
                dyg = acc * vec_refs[0][...]
                outs[0][...] = dres_ref[...] + r * (dyg - xn * jnp.mean(dyg * xn, axis=-1, keepdims=True))
                first = pl.program_id(0) == 0

                @pl.when(first)
                def _():
                    outs[1][...] = dgp

                @pl.when(jnp.logical_not(first))
                def _():
                    outs[1][...] += dgp
                return
            outs[0][...] = acc
            if norm_gain is not None:
                r = lax.rsqrt(jnp.mean(acc * acc, axis=-1, keepdims=True) + EPS)
                outs[1][...] = (acc * r * vec_refs[0][...]).astype(BF16)

        if nk == 1:
            fin(p)
        else:
            acc_ref = scr[0]
            k = pl.program_id(2)

            @pl.when(k == 0)
            def _():
                acc_ref[...] = p

            @pl.when(k > 0)
            def _():
                acc_ref[...] += p

            @pl.when(k == nk - 1)
            def _():
                fin(acc_ref[...])

    a_spec = pl.BlockSpec((tk, tm), lambda i, j, k: (k, i)) if ta else pl.BlockSpec((tm, tk), lambda i, j, k: (i, k))
    b_spec = pl.BlockSpec((tn, tk), lambda i, j, k: (j, k)) if tb else pl.BlockSpec((tk, tn), lambda i, j, k: (k, j))
    o_spec = pl.BlockSpec((tm, tn), lambda i, j, k: (i, j))
    v_spec = pl.BlockSpec((1, tn), lambda i, j, k: (0, j))
    out_specs, out_shape = [o_spec], [jax.ShapeDtypeStruct((M, N), F32)]
    if norm_gain is not None:
        out_specs, out_shape = out_specs + [o_spec], out_shape + [jax.ShapeDtypeStruct((M, N), BF16)]
    if rms_bwd is not None:
        out_specs, out_shape = out_specs + [v_spec], out_shape + [jax.ShapeDtypeStruct((1, N), F32)]
    out = pl.pallas_call(
        body, name=name, grid=(M // tm, N // tn, nk),
        in_specs=[a_spec, b_spec] + [o_spec] * len(extra) + [v_spec] * len(vecs), out_specs=out_specs,
        out_shape=out_shape, scratch_shapes=[pltpu.VMEM((tm, tn), F32)] if nk > 1 else [],
        compiler_params=_params("arbitrary" if rms_bwd is not None else "parallel", "parallel", "arbitrary"),
    )(a, b, *extra, *vecs)
    return out[0] if n_out == 1 else out


def _rms_fwd(x, gain, *, name, tr=512):
    T, Dm = x.shape

    def body(x_ref, g_ref, o_ref):
        xv = x_ref[...]
        r = lax.rsqrt(jnp.mean(xv * xv, axis=-1, keepdims=True) + EPS)
        o_ref[...] = (xv * r * g_ref[...]).astype(BF16)

    return pl.pallas_call(
        body, name=name, grid=(T // tr,),
        in_specs=[pl.BlockSpec((tr, Dm), lambda i: (i, 0)), pl.BlockSpec((1, Dm), lambda i: (0, 0))],
        out_specs=pl.BlockSpec((tr, Dm), lambda i: (i, 0)),
        out_shape=jax.ShapeDtypeStruct((T, Dm), BF16), compiler_params=_params("parallel"),
    )(x, gain)


def _final_loss(x, gain, target, *, tr=512):
    T, Dm = x.shape

    def body(x_ref, g_ref, t_ref, loss_ref, dx_ref, dg_ref):
        i = pl.program_id(0)
        xv = x_ref[...]
        g = g_ref[...]
        r = lax.rsqrt(jnp.mean(xv * xv, axis=-1, keepdims=True) + EPS)
        xn = xv * r
        err = xn * g - t_ref[...]
        lp = jnp.zeros((1, 128), F32) + 0.5 * jnp.sum(jnp.mean(err * err, axis=-1, keepdims=True))
        dy_v = err * (1.0 / Dm)
        dgp = jnp.sum(dy_v * xn, axis=0, keepdims=True)
        dyg = dy_v * g
        dx_ref[...] = r * (dyg - xn * jnp.mean(dyg * xn, axis=-1, keepdims=True))

        @pl.when(i == 0)
        def _():
            dg_ref[...] = dgp
            loss_ref[...] = lp

        @pl.when(i > 0)
        def _():
            dg_ref[...] += dgp
            loss_ref[...] += lp

    row = pl.BlockSpec((tr, Dm), lambda i: (i, 0))
    vec = pl.BlockSpec((1, Dm), lambda i: (0, 0))
    return pl.pallas_call(
        body, name="final_loss", grid=(T // tr,), in_specs=[row, vec, row],
        out_specs=[pl.BlockSpec((1, 128), lambda i: (0, 0)), row, vec],
        out_shape=[jax.ShapeDtypeStruct((1, 128), F32), jax.ShapeDtypeStruct((T, Dm), F32),
                   jax.ShapeDtypeStruct((1, Dm), F32)],
        compiler_params=_params("arbitrary"),
    )(x, gain, target)


def _prev_spec(tr, cb, pad, col):
    return pl.BlockSpec((pad, cb), lambda *g: (jnp.maximum(g[0] * (tr // pad) - 1, 0), col(*g)))


def _next_spec(tr, cb, pad, col, T):
    return pl.BlockSpec((pad, cb), lambda *g: (jnp.minimum((g[0] + 1) * (tr // pad), T // pad - 1), col(*g)))


def _conv_rows(x_ext, w_ref, K, pad, cs=slice(None)):
    y = w_ref[K - 1:K, cs] * x_ext
    for i in range(K - 1):
        y = y + w_ref[i:i + 1, cs] * pltpu.roll(x_ext, K - 1 - i, 0)
    return y[pad:]


def _pool_y(u_ext, g, i, tr):
    s = u_ext
    for sh in (1, 2, 4, 8)[:g + 1]:
        s = s + pltpu.roll(s, sh, 0)
    t = i * tr + _iota((tr, 128), 0)
    cnt = jnp.minimum(t + 1, POOL_WINDOWS[g]).astype(F32)
    return s[16:] / cnt - u_ext[16:]


def _pool_fwd(proj, pool_w, pool_scale, *, tr=512):
    T = proj.shape[0]

    def body(u_ref, uh_ref, w_ref, s_ref, o_ref):
        i = pl.program_id(0)
        uh = jnp.where(i > 0, uh_ref[...], 0.0)
        for g in range(4):
            cs = slice(128 * g, 128 * (g + 1))
            y = _pool_y(jnp.concatenate([uh[:, cs], u_ref[:, cs]], axis=0), g, i, tr)
            o_ref[:, cs] = (_dot(y, w_ref[g], NN) * s_ref[:, cs]).astype(BF16)

    return pl.pallas_call(
        body, name="pool_fwd", grid=(T // tr,),
        in_specs=[pl.BlockSpec((tr, 512), lambda i: (i, 0)), _prev_spec(tr, 512, 16, lambda i: 0),
                  pl.BlockSpec((4, 128, 128), lambda i: (0, 0, 0)), pl.BlockSpec((1, 512), lambda i: (0, 0))],
        out_specs=pl.BlockSpec((tr, 512), lambda i: (i, 0)),
        out_shape=jax.ShapeDtypeStruct((T, 512), BF16), compiler_params=_params("parallel"),
    )(proj, proj, pool_w, pool_scale)


def _pool_bwd(proj, dout, pool_w, pool_scale, *, tr=512):
    T = proj.shape[0]
    nb = T // tr

    def body(u_ref, uh_ref, d_ref, dn_ref, w_ref, s_ref, du_ref, dw_ref, ds_ref):
        i = pl.program_id(0)
        uh = jnp.where(i > 0, uh_ref[...], 0.0)
        dn = jnp.where(i < nb - 1, dn_ref[...], 0.0)
        t_ext = i * tr + _iota((tr + 16, 128), 0)
        for g in range(4):
            cs = slice(128 * g, 128 * (g + 1))
            sc = s_ref[:, cs]
            wg = w_ref[g]
            y = _pool_y(jnp.concatenate([uh[:, cs], u_ref[:, cs]], axis=0), g, i, tr)
            dg = d_ref[:, cs]
            dsp = jnp.sum(dg * _dot(y, wg, NN), axis=0, keepdims=True)
            dyw = dg * sc
            dwp = _dot(y, dyw, TN)
            dy_ext = _dot(jnp.concatenate([dyw, dn[:, cs] * sc], axis=0), wg, NT)
            cnt = jnp.minimum(t_ext + 1, POOL_WINDOWS[g]).astype(F32)
            s = dy_ext / cnt
            for sh in (1, 2, 4, 8)[:g + 1]:
                s = s + pltpu.roll(s, tr + 16 - sh, 0)
            du_ref[:, cs] = (s[:tr] - dy_ext[:tr]).astype(BF16)

            @pl.when(i == 0)
            def _():
                dw_ref[g] = dwp
                ds_ref[:, cs] = dsp

            @pl.when(i > 0)
            def _():
                dw_ref[g] += dwp
                ds_ref[:, cs] += dsp

    row = pl.BlockSpec((tr, 512), lambda i: (i, 0))
    return pl.pallas_call(
        body, name="pool_bwd", grid=(nb,),
        in_specs=[row, _prev_spec(tr, 512, 16, lambda i: 0), row, _next_spec(tr, 512, 16, lambda i: 0, T),
                  pl.BlockSpec((4, 128, 128), lambda i: (0, 0, 0)), pl.BlockSpec((1, 512), lambda i: (0, 0))],
        out_specs=[row, pl.BlockSpec((4, 128, 128), lambda i: (0, 0, 0)), pl.BlockSpec((1, 512), lambda i: (0, 0))],
        out_shape=[jax.ShapeDtypeStruct((T, 512), BF16), jax.ShapeDtypeStruct((4, 128, 128), F32),
                   jax.ShapeDtypeStruct((1, 512), F32)],
        compiler_params=_params("arbitrary"),
    )(proj, proj, dout, dout, pool_w, pool_scale)


def _split_dot(x, tri):
    hi = x.astype(BF16)
    lo = (x - hi.astype(F32)).astype(BF16)
    return (lax.dot_general(hi, tri, NN, preferred_element_type=F32)
            + lax.dot_general(lo, tri, NN, preferred_element_type=F32))


def _log1m(z):
    return -(jnp.maximum(z, 0.0) + jnp.log(1.0 + jnp.exp(-jnp.abs(z))))


def _sb_fwd(proj, gather=()):
    T = proj.shape[0]
    B, BK = min(SB_BLOCK, T), SB_KBLOCK
    R = B // BK
    nq, n = T // B, len(gather)
    scale = SB_HEAD_DIM ** -0.5

    def body(q_ref, k_ref, v_ref, *rest):
        o_ref, ls_ref = rest[n:n + 2]
        hp, i = pl.program_id(0), pl.program_id(1)
        if n:
            start, forward, finish = _gather_stages(rest[:n], rest[n + 2:2 * n + 2], *rest[2 * n + 2:])
            pl.when((hp == 0) & (i == 0))(start)
            pl.when((hp == 3) & (i == nq - 1))(forward)
        lane = _iota((1, 128), 1)
        tri_gt = (_iota((BK, BK), 0) > _iota((BK, BK), 1)).astype(BF16)
        row, col = _iota((B, BK), 0), _iota((B, BK), 1)
        qv = q_ref[...] * scale
        hms = [(lane >= 64 * h) & (lane < 64 * (h + 1)) for h in range(2)]
        qhs = [jnp.where(hm, qv, 0.0).astype(BF16) for hm in hms]

        def tile(j, carry, d):
            rows = pl.ds(pl.multiple_of(j * BK, BK), BK)
            kj = k_ref[rows, :].astype(BF16)
            vj = v_ref[rows, :].astype(BF16)
            r0 = 0 if d is None else BK * d
            valid = None if d is None else (col[r0:] < row[:B - r0])
            out = []
            for h in range(2):
                c, acc = carry[h]
                z = lax.dot_general(qhs[h][r0:], kj, NT, preferred_element_type=F32)
                lg = _log1m(z)
                if d is not None:
                    lg = jnp.where(valid, lg, 0.0)
                a = jnp.exp(z + lg + _split_dot(lg, tri_gt) + c[r0:])
                if d is not None:
                    a = jnp.where(valid, a, 0.0)
                upd = (c[r0:] + jnp.sum(lg, axis=1, keepdims=True),
                       acc[r0:] + lax.dot_general(a.astype(BF16), vj, NN, preferred_element_type=F32))
                out.append(upd if r0 == 0 else tuple(jnp.concatenate([old[:r0], new], axis=0)
                                                     for old, new in zip((c, acc), upd)))
            return tuple(out)

        zero = (jnp.zeros((B, 1), F32), jnp.zeros((B, 128), F32))
        carry = (zero, zero)
        for d in reversed(range(R)):
            carry = tile(i * R + d, carry, d)
        carry = lax.fori_loop(0, i * R, lambda s, cr: tile(i * R - 1 - s, cr, None), carry)
        o_ref[...] = jnp.where(hms[0], carry[0][1], carry[1][1])
        ls_ref[...] = jnp.where(hms[0], carry[0][0], carry[1][0])
        if n:
            pl.when((hp == 3) & (i == nq - 1))(finish)

    blk = pl.BlockSpec((B, 128), lambda hp, i: (i, hp))
    out = pl.pallas_call(
        body, name="sb_fwd", grid=(4, nq),
        in_specs=[pl.BlockSpec((B, 128), lambda hp, i: (i, 4 + hp)),
                  pl.BlockSpec((T, 128), lambda hp, i: (0, 8 + hp)),
                  pl.BlockSpec((T, 128), lambda hp, i: (0, 12 + hp))] + [HBM_SPEC] * n,
        out_specs=[blk, blk] + [HBM_SPEC] * n,
        out_shape=[jax.ShapeDtypeStruct((T, 512), F32)] * 2 + _gather_shapes(gather),
        scratch_shapes=_gather_sems(n) if n else [],
        compiler_params=_params("arbitrary", "arbitrary"),
    )(proj, proj, proj, *gather)
    return out[0], out[1], list(out[2:])


def _sb_bwd(proj, lsum, dout, exchange=()):
    T = proj.shape[0]
    B, BK = min(SB_BLOCK, T), SB_KBLOCK
    R = B // BK
    nq, n = T // B, len(exchange)
    scale = SB_HEAD_DIM ** -0.5

    def body(q_ref, k_ref, v_ref, do_ref, ls_ref, *rest):
        dq_ref, dk_ref, dv_ref = rest[n:n + 3]
        hp, i = pl.program_id(0), pl.program_id(1)
        if n:
            start, finish = _chips_stages(rest[:n], rest[n + 3:2 * n + 3], *rest[2 * n + 3:])
            pl.when((hp == 0) & (i == 0))(start)

        @pl.when(i == 0)
        def _():
            dk_ref[...] = jnp.zeros_like(dk_ref)
            dv_ref[...] = jnp.zeros_like(dv_ref)

        lane = _iota((1, 128), 1)
        tri_le = (_iota((BK, BK), 0) <= _iota((BK, BK), 1)).astype(BF16)
        tri_lt = (_iota((BK, BK), 0) < _iota((BK, BK), 1)).astype(BF16)
        row, col = _iota((B, BK), 0), _iota((B, BK), 1)
        qv = q_ref[...] * scale
        dov = do_ref[...]
        hms = [(lane >= 64 * h) & (lane < 64 * (h + 1)) for h in range(2)]
        qhs = [jnp.where(hm, qv, 0.0).astype(BF16) for hm in hms]
        dos = [jnp.where(hm, dov, 0.0).astype(BF16) for hm in hms]
        ltots = [ls_ref[:, 64 * h:64 * h + 1] for h in range(2)]

        def tile(j, carry, d):
            rows = pl.ds(pl.multiple_of(j * BK, BK), BK)
            kj = k_ref[rows, :].astype(BF16)
            vj = v_ref[rows, :].astype(BF16)
            diag = d is not None
            r0 = BK * d if diag else 0
            valid = (col[r0:] < row[:B - r0]) if diag else None
            out = []
            dkj = jnp.zeros((BK, 128), F32)
            dvj = jnp.zeros((BK, 128), F32)
            for h in range(2):
                lbef, ebef, dqa = carry[h]
                qh, do_h = qhs[h][r0:], dos[h][r0:]
                z = lax.dot_general(qh, kj, NT, preferred_element_type=F32)
                lg = _log1m(z)
                if diag:
                    lg = jnp.where(valid, lg, 0.0)
                a = jnp.exp(z + lg + (ltots[h][r0:] - lbef[r0:] - _split_dot(lg, tri_le)))
                if diag:
                    a = jnp.where(valid, a, 0.0)
                e = a * lax.dot_general(do_h, vj, NT, preferred_element_type=F32)
                dz = e * jnp.exp(lg) - jnp.exp(z + lg) * (ebef[r0:] + _split_dot(e, tri_lt))
                if diag:
                    dz = jnp.where(valid, dz, 0.0)
                dzb = dz.astype(BF16)
                dkj = dkj + lax.dot_general(dzb, qh, TN, preferred_element_type=F32)
                dvj = dvj + lax.dot_general(a.astype(BF16), do_h, TN, preferred_element_type=F32)
                upd = (lbef[r0:] + jnp.sum(lg, axis=1, keepdims=True), ebef[r0:] + jnp.sum(e, axis=1, keepdims=True),
                       dqa[r0:] + lax.dot_general(dzb, kj, NN, preferred_element_type=F32))
                out.append(upd if r0 == 0 else tuple(jnp.concatenate([old[:r0], new], axis=0)
                                                     for old, new in zip(carry[h], upd)))
            dk_ref[rows, :] += dkj
            dv_ref[rows, :] += dvj
            return tuple(out)

        zero = (jnp.zeros((B, 1), F32), jnp.zeros((B, 1), F32), jnp.zeros((B, 128), F32))
        carry = lax.fori_loop(0, i * R, lambda j, cr: tile(j, cr, None), (zero, zero))
        for d in range(R):
            carry = tile(i * R + d, carry, d)
        dq_ref[...] = jnp.where(hms[0], carry[0][2], carry[1][2]) * scale
        if n:
            pl.when((hp == 3) & (i == nq - 1))(finish)

    full = pl.BlockSpec((T, 128), lambda hp, i: (0, hp))
    blk = pl.BlockSpec((B, 128), lambda hp, i: (i, hp))
    out = pl.pallas_call(
        body, name="sb_bwd", grid=(4, nq),
        in_specs=[pl.BlockSpec((B, 128), lambda hp, i: (i, 4 + hp)),
                  pl.BlockSpec((T, 128), lambda hp, i: (0, 8 + hp)),
                  pl.BlockSpec((T, 128), lambda hp, i: (0, 12 + hp)),
                  pl.BlockSpec((B, 128), lambda hp, i: (i, 4 + hp)), blk] + [HBM_SPEC] * n,
        out_specs=[blk, full, full] + [HBM_SPEC] * n,
        out_shape=[jax.ShapeDtypeStruct((T, 512), F32)] * 3 + _chips_shapes(exchange),
        scratch_shapes=_chips_sems(n) if n else [],
        compiler_params=_params("arbitrary", "arbitrary"),
    )(proj, proj, proj, dout, lsum, *exchange)
    return out[0], out[1], out[2], list(out[3:])


def _sigmoid(x):
    return 1.0 / (1.0 + jnp.exp(-x))


def _silu_mul(cg, cv):
    return cg * _sigmoid(cg) * cv


def _ffn_act(up, conv_w, *, tr=512, cb=256):
    T, F2 = up.shape
    nc = F2 // 2 // cb
    K = FFN_CONV

    def body(g_ref, gh_ref, v_ref, vh_ref, wg_ref, wv_ref, o_ref):
        i = pl.program_id(0)
        gh = jnp.where(i > 0, gh_ref[...], 0.0)
        vh = jnp.where(i > 0, vh_ref[...], 0.0)
        cg = _conv_rows(jnp.concatenate([gh, g_ref[...]], axis=0), wg_ref, K, 8)
        cv = _conv_rows(jnp.concatenate([vh, v_ref[...]], axis=0), wv_ref, K, 8)
        o_ref[...] = _silu_mul(cg, cv).astype(BF16)

    return pl.pallas_call(
        body, name="ffn_act", grid=(T // tr, nc),
        in_specs=[pl.BlockSpec((tr, cb), lambda i, j: (i, j)), _prev_spec(tr, cb, 8, lambda i, j: j),
                  pl.BlockSpec((tr, cb), lambda i, j: (i, nc + j)), _prev_spec(tr, cb, 8, lambda i, j: nc + j),
                  pl.BlockSpec((K, cb), lambda i, j: (0, j)), pl.BlockSpec((K, cb), lambda i, j: (0, nc + j))],
        out_specs=pl.BlockSpec((tr, cb), lambda i, j: (i, j)),
        out_shape=jax.ShapeDtypeStruct((T, F2 // 2), BF16), compiler_params=_params("parallel", "parallel"),
    )(up, up, up, up, conv_w, conv_w)


def _conv_bwd_rows(dc_ext, x_ext, w_ref, K, tr, cs=slice(None)):
    n = tr + 8
    dx = w_ref[K - 1:K, cs] * dc_ext
    for i in range(K - 1):
        dx = dx + w_ref[i:i + 1, cs] * pltpu.roll(dc_ext, n - (K - 1 - i), 0)
    dc = dc_ext[:tr]
    dws = [jnp.sum(dc * pltpu.roll(x_ext, K - 1 - i, 0)[8:8 + tr], axis=0, keepdims=True) for i in range(K)]
    return dx[:tr], dws


def _acc_rows(ref, rows, first, cs=slice(None)):
    for i, r in enumerate(rows):
        @pl.when(first)
        def _():
            ref[i:i + 1, cs] = r

        @pl.when(jnp.logical_not(first))
        def _():
            ref[i:i + 1, cs] += r


def _ffn_act_bwd(up, conv_w, dact, *, tr=512, cb=256):
    T, F2 = up.shape
    F = F2 // 2
    nc, nb = F // cb, T // tr
    K = FFN_CONV

    def body(g_ref, gp_ref, gn_ref, v_ref, vp_ref, vn_ref, d_ref, dn_ref, wg_ref, wv_ref,
             dg_ref, dv_ref, dwg_ref, dwv_ref):
        i = pl.program_id(1)
        first, last = i == 0, i == nb - 1
        g_ext = jnp.concatenate([jnp.where(first, 0.0, gp_ref[...]), g_ref[...], jnp.where(last, 0.0, gn_ref[...])], axis=0)
        v_ext = jnp.concatenate([jnp.where(first, 0.0, vp_ref[...]), v_ref[...], jnp.where(last, 0.0, vn_ref[...])], axis=0)
        d_ext = jnp.concatenate([d_ref[...], jnp.where(last, 0.0, dn_ref[...])], axis=0)
        cg = _conv_rows(g_ext, wg_ref, K, 8)
        cv = _conv_rows(v_ext, wv_ref, K, 8)
        s = _sigmoid(cg)
        t = cg * s
        dcv = d_ext * t
        dcg = d_ext * cv * (s + t * (1.0 - s))
        dg, dwg = _conv_bwd_rows(dcg, g_ext, wg_ref, K, tr)
        dv, dwv = _conv_bwd_rows(dcv, v_ext, wv_ref, K, tr)
        dg_ref[...] = dg.astype(BF16)
        dv_ref[...] = dv.astype(BF16)
        _acc_rows(dwg_ref, dwg, first)
        _acc_rows(dwv_ref, dwv, first)

    blk = lambda off: pl.BlockSpec((tr, cb), lambda j, i: (i, off + j))
    prev = lambda off: pl.BlockSpec((8, cb), lambda j, i: (jnp.maximum(i * (tr // 8) - 1, 0), off + j))
    nxt = lambda off: pl.BlockSpec((8, cb), lambda j, i: (jnp.minimum((i + 1) * (tr // 8), T // 8 - 1), off + j))
    wsp = lambda off: pl.BlockSpec((K, cb), lambda j, i: (0, off + j))
    return pl.pallas_call(
        body, name="ffn_act_bwd", grid=(nc, nb),
        in_specs=[blk(0), prev(0), nxt(0), blk(nc), prev(nc), nxt(nc), blk(0), nxt(0), wsp(0), wsp(nc)],
        out_specs=[blk(0), blk(0), wsp(0), wsp(0)],
        out_shape=[jax.ShapeDtypeStruct((T, F), BF16)] * 2 + [jax.ShapeDtypeStruct((K, F), F32)] * 2,
        compiler_params=_params("parallel", "arbitrary"),
    )(up, up, up, up, up, up, dact, dact, conv_w, conv_w)


def _ple_fwd(hn, w_gate, p, w_ple, x, *, name, tm=1024, tn=512):
    T, Dm = x.shape
    tm = min(tm, T)

    def body(a_ref, b_ref, p_ref, wp_ref, x_ref, o_ref, gl_ref, pe_ref):
        gl = _dot(a_ref[...], b_ref[...], NN)
        pe = _dot(p_ref[...], wp_ref[...], NN)
        gl_ref[...] = gl
        pe_ref[...] = pe
        o_ref[...] = x_ref[...] + pe * _sigmoid(gl)

    o_spec = pl.BlockSpec((tm, tn), lambda i, j: (i, j))
    return pl.pallas_call(
        body, name=name, grid=(T // tm, Dm // tn),
        in_specs=[pl.BlockSpec((tm, Dm), lambda i, j: (i, 0)), pl.BlockSpec((Dm, tn), lambda i, j: (0, j)),
                  pl.BlockSpec((tm, PLE_DIM), lambda i, j: (i, 0)), pl.BlockSpec((PLE_DIM, tn), lambda i, j: (0, j)),
                  o_spec],
        out_specs=[o_spec] * 3, out_shape=[jax.ShapeDtypeStruct((T, Dm), F32)] * 3,
        compiler_params=_params("parallel", "parallel"),
    )(hn, w_gate, p, w_ple, x)


def _ple_bwd(dx, gl, pe, *, name, tr=512):
    T, Dm = dx.shape

    def body(dx_ref, gl_ref, pe_ref, dpe_ref, dgl_ref):
        g = _sigmoid(gl_ref[...])
        d = dx_ref[...]
        dpe_ref[...] = (d * g).astype(BF16)
        dgl_ref[...] = (d * pe_ref[...] * g * (1.0 - g)).astype(BF16)

    row = pl.BlockSpec((tr, Dm), lambda i: (i, 0))
    return pl.pallas_call(
        body, name=name, grid=(T // tr,), in_specs=[row] * 3, out_specs=[row] * 2,
        out_shape=[jax.ShapeDtypeStruct((T, Dm), BF16)] * 2, compiler_params=_params("parallel"),
    )(dx, gl, pe)


def _qkv_act(c, cb):
    s = c * _sigmoid(c)
    n = s * lax.rsqrt(jnp.sum(s * s, axis=-1, keepdims=True) + EPS)
    n = n * jnp.where(cb < GDN_HEADS, GDN_HEAD_DIM ** -0.5, 1.0)
    return jnp.where(cb < 2 * GDN_HEADS, n, s)


GDN_HPS = 4


def _gdn_pre(proj, conv_w, *, tr=512):
    T = proj.shape[0]
    K = GDN_CONV

    def body(x_ref, xh_ref, w_ref, o_ref):
        i, j = pl.program_id(0), pl.program_id(1)
        xh = jnp.where(i > 0, xh_ref[...], 0.0)
        for hh in range(GDN_HPS):
            cs = slice(128 * hh, 128 * (hh + 1))
            c = _conv_rows(jnp.concatenate([xh[:, cs], x_ref[:, cs]], axis=0), w_ref, K, 8, cs)
            o_ref[hh] = _qkv_act(c, GDN_HPS * j + hh)

    wide = 128 * GDN_HPS
    return pl.pallas_call(
        body, name="gdn_pre", grid=(T // tr, 24 // GDN_HPS),
        in_specs=[pl.BlockSpec((tr, wide), lambda i, j: (i, j)), _prev_spec(tr, wide, 8, lambda i, j: j),
                  pl.BlockSpec((K, wide), lambda i, j: (0, j))],
        out_specs=pl.BlockSpec((GDN_HPS, tr, 128), lambda i, j: (j, i, 0)),
        out_shape=jax.ShapeDtypeStruct((24, T, 128), F32), compiler_params=_params("parallel", "parallel"),
    )(proj, proj, conv_w)


def _gdn_pre_bwd(proj, conv_w, dqkv, *, tr=512):
    T = proj.shape[0]
    nb = T // tr
    K = GDN_CONV

    def body(x_ref, xp_ref, xn_ref, d_ref, dn_ref, w_ref, dx_ref, dw_ref):
        j, i = pl.program_id(0), pl.program_id(1)
        first, last = i == 0, i == nb - 1
        xp = jnp.where(first, 0.0, xp_ref[...])
        xn = jnp.where(last, 0.0, xn_ref[...])
        for hh in range(GDN_HPS):
            cs = slice(128 * hh, 128 * (hh + 1))
            x_ext = jnp.concatenate([xp[:, cs], x_ref[:, cs], xn[:, cs]], axis=0)
            d_ext = jnp.concatenate([d_ref[hh], jnp.where(last, 0.0, dn_ref[hh])], axis=0)
            c = _conv_rows(x_ext, w_ref, K, 8, cs)
            _, vjp = jax.vjp(lambda c_: _qkv_act(c_, GDN_HPS * j + hh), c)
            (dc,) = vjp(d_ext)
            dx, dws = _conv_bwd_rows(dc, x_ext, w_ref, K, tr, cs)
            dx_ref[:, cs] = dx.astype(BF16)
            _acc_rows(dw_ref, dws, first, cs)

    wide = 128 * GDN_HPS
    return pl.pallas_call(
        body, name="gdn_pre_bwd", grid=(24 // GDN_HPS, nb),
        in_specs=[pl.BlockSpec((tr, wide), lambda j, i: (i, j)),
                  pl.BlockSpec((8, wide), lambda j, i: (jnp.maximum(i * (tr // 8) - 1, 0), j)),
                  pl.BlockSpec((8, wide), lambda j, i: (jnp.minimum((i + 1) * (tr // 8), T // 8 - 1), j)),
                  pl.BlockSpec((GDN_HPS, tr, 128), lambda j, i: (j, i, 0)),
                  pl.BlockSpec((GDN_HPS, 8, 128), lambda j, i: (j, jnp.minimum((i + 1) * (tr // 8), T // 8 - 1), 0)),
                  pl.BlockSpec((K, wide), lambda j, i: (0, j))],
        out_specs=[pl.BlockSpec((tr, wide), lambda j, i: (i, j)), pl.BlockSpec((K, wide), lambda j, i: (0, j))],
        out_shape=[jax.ShapeDtypeStruct((T, 24 * 128), BF16), jax.ShapeDtypeStruct((K, 24 * 128), F32)],
        compiler_params=_params("parallel", "arbitrary"),
    )(proj, proj, proj, dqkv, dqkv, conv_w)


def _gate_fn(ba, alog_row, dt_row):
    lane = _iota((1, 128), 1)
    x = ba + dt_row
    sp = jnp.maximum(x, 0.0) + jnp.log(1.0 + jnp.exp(-jnp.abs(x)))
    return jnp.where(lane < GDN_HEADS, _sigmoid(ba), -jnp.exp(alog_row) * sp)


def _gdn_gate(ba, alog_row, dt_row, *, tr=512):
    T = ba.shape[0]

    def body(ba_ref, al_ref, dt_ref, b_ref, g_ref):
        val = _gate_fn(ba_ref[...], al_ref[...], dt_ref[...])
        for h in range(GDN_HEADS):
            b_ref[h] = val[:, h:h + 1]
            g_ref[h] = val[:, GDN_HEADS + h:GDN_HEADS + h + 1]

    vec = pl.BlockSpec((1, 128), lambda i: (0, 0))
    hm = pl.BlockSpec((GDN_HEADS, tr, 1), lambda i: (0, i, 0))
    return pl.pallas_call(
        body, name="gdn_gate", grid=(T // tr,), in_specs=[pl.BlockSpec((tr, 128), lambda i: (i, 0)), vec, vec],
        out_specs=[hm, hm], out_shape=[jax.ShapeDtypeStruct((GDN_HEADS, T, 1), F32)] * 2,
        compiler_params=_params("parallel"),
    )(ba, alog_row, dt_row)


def _gdn_gate_bwd(ba, alog_row, dt_row, dbeta, dg, *, tr=512):
    T = ba.shape[0]

    def body(ba_ref, al_ref, dt_ref, db_ref, dg_ref, dba_ref, dal_ref, ddt_ref):
        i = pl.program_id(0)
        lane = _iota((1, 128), 1)
        d = jnp.zeros((tr, 128), F32)
        for h in range(GDN_HEADS):
            d = d + jnp.where(lane == h, db_ref[h], 0.0) + jnp.where(lane == GDN_HEADS + h, dg_ref[h], 0.0)
        _, vjp = jax.vjp(_gate_fn, ba_ref[...], al_ref[...], dt_ref[...])
        dba, dal, ddt = vjp(d)
        dba_ref[...] = dba.astype(BF16)

        @pl.when(i == 0)
        def _():
            dal_ref[...] = dal
            ddt_ref[...] = ddt

        @pl.when(i > 0)
        def _():
            dal_ref[...] += dal
            ddt_ref[...] += ddt

    vec = pl.BlockSpec((1, 128), lambda i: (0, 0))
    hm = pl.BlockSpec((GDN_HEADS, tr, 1), lambda i: (0, i, 0))
    row = pl.BlockSpec((tr, 128), lambda i: (i, 0))
    return pl.pallas_call(
        body, name="gdn_gate_bwd", grid=(T // tr,), in_specs=[row, vec, vec, hm, hm], out_specs=[row, vec, vec],
        out_shape=[jax.ShapeDtypeStruct((T, 128), BF16), jax.ShapeDtypeStruct((1, 128), F32),
                   jax.ShapeDtypeStruct((1, 128), F32)],
        compiler_params=_params("arbitrary"),
    )(ba, alog_row, dt_row, dbeta, dg)


def _split3(x):
    x1 = x.astype(BF16)
    r = x - x1.astype(F32)
    x2 = r.astype(BF16)
    return x1, x2, (r - x2.astype(F32)).astype(BF16)


def _dot01(tri, x, dims):
    t = tri.astype(BF16)
    x1, x2, x3 = _split3(x)
    d = lambda xi: lax.dot_general(t, xi, dims, preferred_element_type=F32)
    return d(x1) + (d(x2) + d(x3))


def _dot3(a, b, dims):
    ah, al, _ = _split3(a)
    bh, bl, _ = _split3(b)
    d = lambda p, q: lax.dot_general(p, q, dims, preferred_element_type=F32)
    return d(ah, bh) + (d(ah, bl) + d(al, bh))


BNN = (((2,), (1,)), ((0,), (0,)))
BNT = (((2,), (2,)), ((0,), (0,)))
BTN = (((1,), (1,)), ((0,), (0,)))


@jax.custom_vjp
def _mm01(tri, x):
    return _dot01(tri, x, BNN)


def _mm01_fwd(tri, x):
    return _dot01(tri, x, BNN), tri


def _mm01_bwd(tri, ct):
    return jnp.zeros_like(tri), _dot01(tri, ct, BTN)


_mm01.defvjp(_mm01_fwd, _mm01_bwd)


def _unit_lower_inverse(a):
    C = a.shape[-1]
    eye = (_iota(a.shape, 1) == _iota(a.shape, 2)).astype(F32)
    pw = -a
    tinv = eye + pw
    for _ in range(5):
        pw = _dot3(pw, pw, BNN)
        tinv = tinv + _dot3(tinv, pw, BNN)
    return tinv


@jax.custom_vjp
def _unit_lower_solve(a, rv, rw):
    return _unit_lower_solve_fwd(a, rv, rw)[0]


def _unit_lower_solve_fwd(a, rv, rw):
    tinv = _unit_lower_inverse(a)
    sol = _dot3(tinv, jnp.concatenate([rv, rw], axis=2), BNN)
    n = rv.shape[2]
    return (sol[:, :, :n], sol[:, :, n:]), (tinv, sol)


def _unit_lower_solve_bwd(res, cts):
    tinv, sol = res
    n = cts[0].shape[2]
    d_rhs = _dot3(tinv, jnp.concatenate(cts, axis=2), BTN)
    return -_dot3(d_rhs, sol, BNT), d_rhs[:, :, :n], d_rhs[:, :, n:]


_unit_lower_solve.defvjp(_unit_lower_solve_fwd, _unit_lower_solve_bwd)


@jax.custom_vjp
def _mmb_nt(a, b):
    return _dot(a, b, BNT)


def _mmb_nt_fwd(a, b):
    return _dot(a, b, BNT), (a, b)


def _mmb_nt_bwd(res, ct):
    a, b = res
    return _dot(ct, b, BNN), _dot(ct, a, BTN)


_mmb_nt.defvjp(_mmb_nt_fwd, _mmb_nt_bwd)


def _gdn_chunk(q, k, v, gcol, bcol):
    nb, C = q.shape[0], GDN_CHUNK
    row, col = _iota((nb, C, C), 1), _iota((nb, C, C), 2)
    incl, strict = row >= col, row > col
    eye = (row == col).astype(F32)
    lower = incl.astype(F32)
    ones = jnp.ones((nb, C, C), F32)
    gwide = jnp.broadcast_to(gcol, (nb, C, GDN_HEAD_DIM))
    gc = _mm01(lower, gwide)
    gtot = _mm01(ones, gwide)
    gc_c = _mm01(lower, jnp.broadcast_to(gcol, (nb, C, C)))
    gc_s = _mm01(ones, gc_c * eye)
    decay = jnp.where(incl, jnp.exp(jnp.where(incl, gc_c - gc_s, 0.0)), 0.0)
    kb = k * bcol
    a = jnp.where(strict, _mmb_nt(kb, k) * decay, 0.0)
    egc = jnp.exp(gc)
    u, w = _unit_lower_solve(a, v * bcol, kb * egc)
    qk = jnp.where(incl, _mmb_nt(q, k) * decay, 0.0)
    return u, w, qk, q * egc, k * jnp.exp(gtot - gc), jnp.exp(jnp.sum(gwide, axis=1))


GDN_ROWS = 8 * GDN_CHUNK


GDN_LOCAL_CHUNKS = 16


def _gdn_specs(T):
    nch = min(GDN_LOCAL_CHUNKS, T // GDN_CHUNK)
    L = nch * GDN_CHUNK
    hd = lambda off: pl.BlockSpec((1, L, 128), lambda h, i: (off + h, i, 0))
    col = pl.BlockSpec((1, L, 1), lambda h, i: (h, i, 0))
    sq = pl.BlockSpec((1, L, GDN_CHUNK), lambda h, i: (h, i, 0))
    gl = pl.BlockSpec((1, nch, 128), lambda h, i: (h, i, 0))
    return nch, hd, col, sq, gl


def _gdn_local(qkv, g, beta):
    T = qkv.shape[1]
    nch, hd, col, sq, gl_spec = _gdn_specs(T)

    def body(q_ref, k_ref, v_ref, g_ref, b_ref, u_ref, w_ref, qk_ref, qd_ref, kd_ref, gl_ref):
        chunks = lambda ref: ref[0].reshape(nch, GDN_CHUNK, ref.shape[2])
        rows = lambda val: val.reshape(nch * GDN_CHUNK, val.shape[2])
        u, w, qk, qd, kd, gl = _gdn_chunk(chunks(q_ref), chunks(k_ref), chunks(v_ref), chunks(g_ref), chunks(b_ref))
        u_ref[0] = rows(u)
        w_ref[0] = rows(w).astype(BF16)
        qk_ref[0] = rows(qk).astype(BF16)
        qd_ref[0] = rows(qd).astype(BF16)
        kd_ref[0] = rows(kd).astype(BF16)
        gl_ref[0] = gl

    H = GDN_HEADS
    return pl.pallas_call(
        body, name="gdn_local", grid=(H, T // (nch * GDN_CHUNK)),
        in_specs=[hd(0), hd(H), hd(2 * H), col, col],
        out_specs=[hd(0), hd(0), sq, hd(0), hd(0), gl_spec],
        out_shape=[jax.ShapeDtypeStruct((H, T, 128), F32), jax.ShapeDtypeStruct((H, T, 128), BF16),
                   jax.ShapeDtypeStruct((H, T, GDN_CHUNK), BF16), jax.ShapeDtypeStruct((H, T, 128), BF16),
                   jax.ShapeDtypeStruct((H, T, 128), BF16), jax.ShapeDtypeStruct((H, T // GDN_CHUNK, 128), F32)],
        compiler_params=_params("parallel", "parallel"),
    )(qkv, qkv, qkv, g, beta)


def _gdn_local_bwd(qkv, g, beta, du, dw, dqk, dqd, dkd, dgl):
    T = qkv.shape[1]
    nch, hd, col, sq, gl_spec = _gdn_specs(T)

    def body(q_ref, k_ref, v_ref, g_ref, b_ref, du_ref, dw_ref, dqk_ref, dqd_ref, dkd_ref, dgl_ref,
             dqkv_ref, dg_ref, db_ref):
        chunks = lambda ref: ref[0].reshape(nch, GDN_CHUNK, ref.shape[2])
        rows = lambda val: val.reshape(nch * GDN_CHUNK, val.shape[2])
        _, vjp = jax.vjp(_gdn_chunk, chunks(q_ref), chunks(k_ref), chunks(v_ref), chunks(g_ref), chunks(b_ref))
        dq, dk, dv, dg, db = vjp((chunks(du_ref), chunks(dw_ref), chunks(dqk_ref), chunks(dqd_ref), chunks(dkd_ref),
                                  dgl_ref[0]))
        dqkv_ref[0, 0] = rows(dq)
        dqkv_ref[1, 0] = rows(dk)
        dqkv_ref[2, 0] = rows(dv)
        dg_ref[0] = rows(dg)
        db_ref[0] = rows(db)

    H = GDN_HEADS
    small = jax.ShapeDtypeStruct((H, T, 1), F32)
    dqkv, dg, db = pl.pallas_call(
        body, name="gdn_local_bwd", grid=(H, T // (nch * GDN_CHUNK)),
        in_specs=[hd(0), hd(H), hd(2 * H), col, col, hd(0), hd(0), sq, hd(0), hd(0), gl_spec],
        out_specs=[pl.BlockSpec((3, 1, nch * GDN_CHUNK, 128), lambda h, i: (0, h, i, 0)), col, col],
        out_shape=[jax.ShapeDtypeStruct((3, H, T, 128), F32), small, small],
        compiler_params=_params("parallel", "parallel"),
    )(qkv, qkv, qkv, g, beta, du, dw, dqk, dqd, dkd, dgl)
    return dqkv.reshape(3 * H, T, 128), dg, db


GDN_HB = 4


def _gdn_scan_specs(T, rev):
    nb = T // GDN_ROWS
    blk = (lambda i: nb - 1 - i) if rev else (lambda i: i)
    hd = pl.BlockSpec((GDN_HB, GDN_ROWS, 128), lambda h, i: (h, blk(i), 0))
    sq = pl.BlockSpec((GDN_HB, GDN_ROWS, GDN_CHUNK), lambda h, i: (h, blk(i), 0))
    gl = pl.BlockSpec((GDN_HB, 8, 128), lambda h, i: (h, blk(i), 0))
    st = pl.BlockSpec((GDN_HB, 8, 128, 128), lambda h, i: (h, blk(i), 0, 0))
    return hd, sq, gl, st


def _gdn_scan(u, w, qk, qd, kd, gl):
    H, T, _ = u.shape
    hd, sq, gl_spec, st = _gdn_scan_specs(T, False)

    def body(u_ref, w_ref, qk_ref, qd_ref, kd_ref, gl_ref, o_ref, ss_ref, vn_ref, s_scr):
        @pl.when(pl.program_id(1) == 0)
        def _():
            s_scr[...] = jnp.zeros_like(s_scr)

        dot = lambda a, b, dims: lax.dot_general(a, b, dims, preferred_element_type=F32)
        s = s_scr[...]
        for c in range(8):
            rs = slice(GDN_CHUNK * c, GDN_CHUNK * (c + 1))
            ss_ref[:, c] = s
            sb = s.astype(BF16)
            vn = u_ref[:, rs, :] - dot(w_ref[:, rs, :], sb, BNN)
            vnb = vn.astype(BF16)
            o_ref[:, rs, :] = dot(qd_ref[:, rs, :], sb, BNN) + dot(qk_ref[:, rs, :], vnb, BNN)
            vn_ref[:, rs, :] = vnb
            s = s * gl_ref[:, c:c + 1, :] + dot(kd_ref[:, rs, :], vnb, BTN)
        s_scr[...] = s

    return pl.pallas_call(
        body, name="gdn_scan", grid=(H // GDN_HB, T // GDN_ROWS),
        in_specs=[hd, hd, sq, hd, hd, gl_spec], out_specs=[hd, st, hd],
        out_shape=[jax.ShapeDtypeStruct((H, T, 128), F32), jax.ShapeDtypeStruct((H, T // GDN_CHUNK, 128, 128), F32),
                   jax.ShapeDtypeStruct((H, T, 128), BF16)],
        scratch_shapes=[pltpu.VMEM((GDN_HB, 128, 128), F32)],
        compiler_params=_params("parallel", "arbitrary"),
    )(u, w, qk, qd, kd, gl)


def _gdn_scan_bwd(do, ss, vn, w, qk, qd, kd, gl):
    H, T, _ = do.shape
    hd, sq, gl_spec, st = _gdn_scan_specs(T, True)

    def body(do_ref, ss_ref, vn_ref, w_ref, qk_ref, qd_ref, kd_ref, gl_ref,
             du_ref, dw_ref, dqk_ref, dqd_ref, dkd_ref, dgl_ref, ds_scr):
        @pl.when(pl.program_id(1) == 0)
        def _():
            ds_scr[...] = jnp.zeros_like(ds_scr)

        dot = lambda a, b, dims: lax.dot_general(a, b, dims, preferred_element_type=F32)
        ds = ds_scr[...]
        for c in reversed(range(8)):
            rs = slice(GDN_CHUNK * c, GDN_CHUNK * (c + 1))
            s = ss_ref[:, c]
            sb, dsb = s.astype(BF16), ds.astype(BF16)
            dob = do_ref[:, rs, :].astype(BF16)
            vnb = vn_ref[:, rs, :]
            dvn = dot(qk_ref[:, rs, :], dob, BTN) + dot(kd_ref[:, rs, :], dsb, BNN)
            dvnb = dvn.astype(BF16)
            du_ref[:, rs, :] = dvn
            dw_ref[:, rs, :] = -dot(dvnb, sb, BNT)
            dqk_ref[:, rs, :] = dot(dob, vnb, BNT)
            dqd_ref[:, rs, :] = dot(dob, sb, BNT)
            dkd_ref[:, rs, :] = dot(vnb, dsb, BNT)
            dgl_ref[:, c:c + 1, :] = jnp.sum(ds * s, axis=1, keepdims=True)
            ds = dot(qd_ref[:, rs, :], dob, BTN) + ds * gl_ref[:, c:c + 1, :] - dot(w_ref[:, rs, :], dvnb, BTN)
        ds_scr[...] = ds

    big = jax.ShapeDtypeStruct((H, T, 128), F32)
    return pl.pallas_call(
        body, name="gdn_scan_bwd", grid=(H // GDN_HB, T // GDN_ROWS),
        in_specs=[hd, st, hd, hd, sq, hd, hd, gl_spec], out_specs=[hd, hd, sq, hd, hd, gl_spec],
        out_shape=[big, big, jax.ShapeDtypeStruct((H, T, GDN_CHUNK), F32), big, big,
                   jax.ShapeDtypeStruct((H, T // GDN_CHUNK, 128), F32)],
        scratch_shapes=[pltpu.VMEM((GDN_HB, 128, 128), F32)],
        compiler_params=_params("parallel", "arbitrary"),
    )(do, ss, vn, w, qk, qd, kd, gl)


def _gated_norm(o, z, nw):
    on = o * lax.rsqrt(jnp.mean(o * o, axis=-1, keepdims=True) + EPS) * nw
    return on * (z * _sigmoid(z))


def _gdn_post(o, proj, norm_w, *, tr=512):
    T = proj.shape[0]

    def body(o_ref, z_ref, n_ref, y_ref):
        y_ref[...] = _gated_norm(o_ref[0], z_ref[...], n_ref[...]).astype(BF16)

    return pl.pallas_call(
        body, name="gdn_post", grid=(T // tr, GDN_HEADS),
        in_specs=[pl.BlockSpec((1, tr, 128), lambda i, h: (h, i, 0)), pl.BlockSpec((tr, 128), lambda i, h: (i, 24 + h)),
                  pl.BlockSpec((1, 128), lambda i, h: (0, 0))],
        out_specs=pl.BlockSpec((tr, 128), lambda i, h: (i, h)),
        out_shape=jax.ShapeDtypeStruct((T, 1024), BF16), compiler_params=_params("parallel", "parallel"),
    )(o, proj, norm_w)


def _gdn_post_bwd(o, proj, norm_w, dy, *, tr=512):
    T = proj.shape[0]

    def body(o_ref, z_ref, n_ref, dy_ref, do_ref, dz_ref, dn_ref):
        first = (pl.program_id(0) == 0) & (pl.program_id(1) == 0)
        _, vjp = jax.vjp(_gated_norm, o_ref[0], z_ref[...], n_ref[...])
        do, dz, dn = vjp(dy_ref[...])
        do_ref[0] = do
        dz_ref[...] = dz.astype(BF16)

        @pl.when(first)
        def _():
            dn_ref[...] = dn

        @pl.when(jnp.logical_not(first))
        def _():
            dn_ref[...] += dn

    blk = pl.BlockSpec((tr, 128), lambda i, h: (i, h))
    hm = pl.BlockSpec((1, tr, 128), lambda i, h: (h, i, 0))
    vec = pl.BlockSpec((1, 128), lambda i, h: (0, 0))
    return pl.pallas_call(
        body, name="gdn_post_bwd", grid=(T // tr, GDN_HEADS),
        in_specs=[hm, pl.BlockSpec((tr, 128), lambda i, h: (i, 24 + h)), vec, blk], out_specs=[hm, blk, vec],
        out_shape=[jax.ShapeDtypeStruct((GDN_HEADS, T, 128), F32), jax.ShapeDtypeStruct((T, 1024), BF16),
                   jax.ShapeDtypeStruct((1, 128), F32)],
        compiler_params=_params("arbitrary", "arbitrary"),
    )(o, proj, norm_w, dy)


HBM_SPEC = pl.BlockSpec(memory_space=pltpu.HBM)


def _place():
    return lax.axis_index("x"), lax.axis_index("y"), lax.axis_index("c")


def _all_gather(vs, *, name):
    n = len(vs)

    def body(*refs):
        start, forward, finish = _gather_stages(refs[:n], refs[n:2 * n], *refs[2 * n:])
        start()
        forward()
        finish()

    return pl.pallas_call(
        body, name=name, out_shape=_gather_shapes(vs), in_specs=[HBM_SPEC] * n, out_specs=[HBM_SPEC] * n,
        scratch_shapes=_gather_sems(n),
    )(*vs)


def _gather_shapes(vs):
    return [jax.ShapeDtypeStruct((N_DEV,) + v.shape, v.dtype) for v in vs]


def _gather_sems(n):
    return [pltpu.SemaphoreType.DMA((7 * n,)), pltpu.SemaphoreType.DMA((7 * n,)), pltpu.SemaphoreType.DMA((n,))]


def _gather_stages(v_refs, out_refs, send_sems, recv_sems, local_sems):
    n = len(v_refs)
    x, y, c = _place()
    me, sibling = (x, y, c), (x, y, 1 - c)
    chips = [(1 - x, y), (x, 1 - y), (1 - x, 1 - y)]

    def copy(a, k, block, to, from_input=False):
        slot = out_refs[a].at[4 * block[0] + 2 * block[1] + block[2]]
        return pltpu.make_async_remote_copy(
            src_ref=v_refs[a] if from_input else slot, dst_ref=slot,
            send_sem=send_sems.at[7 * a + k], recv_sem=recv_sems.at[7 * a + k], device_id=to, device_id_type=MESH)

    def mine():
        return [pltpu.make_async_copy(v_refs[a], out_refs[a].at[4 * x + 2 * y + c], local_sems.at[a]) for a in range(n)]

    def first():
        return ([copy(a, 0, me, sibling, True) for a in range(n)]
                + [copy(a, 1 + j, me, (*chip, c), True) for j, chip in enumerate(chips) for a in range(n)])

    def passed():
        return [copy(a, 4 + j, (*chip, c), sibling) for j, chip in enumerate(chips) for a in range(n)]

    def start():
        for cp in mine() + first():
            cp.start()

    def forward():
        for j, chip in enumerate(chips):
            for a in range(n):
                copy(a, 1 + j, (*chip, c), me).wait_recv()
                copy(a, 4 + j, (*chip, c), sibling).start()

    def finish():
        for a in range(n):
            copy(a, 0, sibling, me).wait_recv()
            for j, chip in enumerate(chips):
                copy(a, 4 + j, (*chip, 1 - c), me).wait_recv()
        for cp in first() + passed():
            cp.wait_send()
        for cp in mine():
            cp.wait()

    return start, forward, finish


def _exchange_sibling(gs, *, name):
    n = len(gs)

    def body(*refs):
        g_refs, out_refs = refs[:n], refs[n:2 * n]
        send_sems, recv_sems = refs[2 * n:]
        x, y, c = _place()
        copies = [pltpu.make_async_remote_copy(
            src_ref=g_refs[a].at[k, 1 - c], dst_ref=out_refs[a].at[k], send_sem=send_sems.at[4 * a + k],
            recv_sem=recv_sems.at[4 * a + k], device_id=(x, y, 1 - c), device_id_type=MESH)
            for a in range(n) for k in range(4)]
        for cp in copies:
            cp.start()
        for cp in copies:
            cp.wait()

    return pl.pallas_call(
        body, name=name, out_shape=[jax.ShapeDtypeStruct((4,) + g.shape[2:], g.dtype) for g in gs],
        in_specs=[HBM_SPEC] * n, out_specs=[HBM_SPEC] * n,
        scratch_shapes=[pltpu.SemaphoreType.DMA((4 * n,)), pltpu.SemaphoreType.DMA((4 * n,))],
    )(*gs)


def _exchange_chips(pcs):
    n = len(pcs)

    def body(*refs):
        start, finish = _chips_stages(refs[:n], refs[n:2 * n], *refs[2 * n:])
        start()
        finish()

    return pl.pallas_call(
        body, name="rs_chips", out_shape=_chips_shapes(pcs), in_specs=[HBM_SPEC] * n, out_specs=[HBM_SPEC] * n,
        scratch_shapes=_chips_sems(n),
    )(*pcs)


def _chips_shapes(pcs):
    return [jax.ShapeDtypeStruct((3,) + pc.shape[1:], pc.dtype) for pc in pcs]


def _chips_sems(n):
    return [pltpu.SemaphoreType.DMA((3 * n,)), pltpu.SemaphoreType.DMA((3 * n,))]


def _chips_stages(p_refs, out_refs, send_sems, recv_sems):
    n = len(p_refs)
    x, y, c = _place()
    chips = [(1 - x, y), (x, 1 - y), (1 - x, 1 - y)]

    def copies():
        return [pltpu.make_async_remote_copy(
            src_ref=p_refs[a].at[2 * cx + cy], dst_ref=out_refs[a].at[j], send_sem=send_sems.at[3 * a + j],
            recv_sem=recv_sems.at[3 * a + j], device_id=(cx, cy, c), device_id_type=MESH)
            for j, (cx, cy) in enumerate(chips) for a in range(n)]

    def start():
        for cp in copies():
            cp.start()

    def finish():
        for cp in copies():
            cp.wait()

    return start, finish


def _chip_partial(place, g, got, *, tr, name):
    R, W = g.shape[2:]

    def body(pl_ref, g_ref, r_ref, o_ref):
        o_ref[...] = (g_ref[0] + r_ref[...]).astype(BF16)

    return pl.pallas_call(
        body, name=name, out_shape=jax.ShapeDtypeStruct((4, R, W), BF16),
        grid_spec=pltpu.PrefetchScalarGridSpec(
            num_scalar_prefetch=1, grid=(4, R // tr),
            in_specs=[pl.BlockSpec((1, 1, tr, W), lambda k, i, pr: (k, pr[2], i, 0)),
                      pl.BlockSpec((1, tr, W), lambda k, i, pr: (k, i, 0))],
            out_specs=pl.BlockSpec((1, tr, W), lambda k, i, pr: (k, i, 0))),
        compiler_params=_params("parallel", "parallel"),
    )(place, g, got)


def _adamw_math(g, w, m, v):
    m = ADAM_B1 * m + (1.0 - ADAM_B1) * g
    v = ADAM_B2 * v + (1.0 - ADAM_B2) * (g * g)
    m_hat = m / (1.0 - ADAM_B1 ** ADAM_STEP)
    v_hat = v / (1.0 - ADAM_B2 ** ADAM_STEP)
    return -ADAM_LR * (m_hat / (jnp.sqrt(v_hat) + ADAM_EPS) + ADAM_WD * w), m, v


def _adamw_shard(place, g, got1, got2, w, m, v, *, tr, name):
    R, W = w.shape

    def body(pl_ref, g_ref, r1_ref, r2_ref, w_ref, m_ref, v_ref, go_ref, d_ref, mo_ref, vo_ref):
        gs = g_ref[0, 0] + r1_ref[0]
        for j in range(3):
            gs = gs + r2_ref[j].astype(F32)
        go_ref[...] = gs
        d_ref[...], mo_ref[...], vo_ref[...] = _adamw_math(gs, w_ref[...], m_ref[...], v_ref[...])

    row = pl.BlockSpec((tr, W), lambda i, pr: (i, 0))
    out = jax.ShapeDtypeStruct((R, W), F32)
    return pl.pallas_call(
        body, name=name, out_shape=[out] * 4,
        grid_spec=pltpu.PrefetchScalarGridSpec(
            num_scalar_prefetch=1, grid=(R // tr,),
            in_specs=[pl.BlockSpec((1, 1, tr, W), lambda i, pr: (2 * pr[0] + pr[1], pr[2], i, 0)),
                      pl.BlockSpec((1, tr, W), lambda i, pr: (2 * pr[0] + pr[1], i, 0)),
                      pl.BlockSpec((3, tr, W), lambda i, pr: (0, i, 0)), row, row, row],
            out_specs=[row] * 4),
        compiler_params=_params("parallel"),
    )(place, g, got1, got2, w, m, v)


def _adamw_replicated(parts, w, m, v):
    R, W = w.shape

    def body(p_ref, w_ref, m_ref, v_ref, go_ref, d_ref, mo_ref, vo_ref):
        gs = p_ref[0]
        for j in range(1, N_DEV):
            gs = gs + p_ref[j]
        go_ref[...] = gs
        d_ref[...], mo_ref[...], vo_ref[...] = _adamw_math(gs, w_ref[...], m_ref[...], v_ref[...])

    full = pl.BlockSpec((R, W), lambda i: (0, 0))
    out = jax.ShapeDtypeStruct((R, W), F32)
    return pl.pallas_call(
        body, name="adamw_replicated", grid=(1,), out_shape=[out] * 4,
        in_specs=[pl.BlockSpec((N_DEV, R, W), lambda i: (0, 0, 0)), full, full, full], out_specs=[full] * 4,
        compiler_params=_params("arbitrary"),
    )(parts, w, m, v)


GROUPS = {
    "in_e": (256, 512, ((("w_in_e", None), 1024, 1024),)),
    "in_o": (514, 512, ((("w_in_o", None), 1024, 1024),)),
    "up0": (704, 256, ((("w_up", 0), 1024, 1024),)),
    "up1": (704, 256, ((("w_up", 1), 1024, 1024),)),
    "down0": (1024, 176, ((("w_down", 0), 352, 352),)),
    "down1": (1024, 176, ((("w_down", 1), 352, 352),)),
    "square": (1024, 256, ((("w_out_e", None), 128, 128), (("w_out_o", None), 128, 128), (("w_ple_gate", None), 256, 256))),
    "ple": (128, 512, ((("w_ple", None), 512, 512),)),
    "conv_f": (704, 8, ((("ffn_conv", None), 6, 8),)),
    "norm_o": (128, 8, ((("mix_norm_o", None), 1, 8),)),
    "conv_o": (384, 8, ((("conv_qkv_o", None), 4, 8),)),
}
SHARDED = tuple(dict.fromkeys(p[0][0] for g in GROUPS.values() for p in g[2]))
COLUMN_SHARDED = ("w_in_e", "w_in_o", "w_up", "ffn_conv", "w_ple", "conv_qkv_o", "mix_norm_o")
PACK_W = 1024
REPL_LAYOUT = (
    ("mix_norm_e", (1, 1024), 8), ("pool_w", (1, 4, 128, 128), 64), ("pool_scale", (1, 512), 8),
    ("a_log_o", (1, 8), 8), ("dt_bias_o", (1, 8), 8), ("gdn_norm_o", (1, 128), 8),
    ("ffn_norm", (2, 1024), 8), ("ple_norm", (2, 1024), 8), ("final_norm", (1024,), 8),
)


def _pad_rows(a, rows):
    extra = rows - a.shape[-2]
    return a if extra == 0 else jnp.pad(a, [(0, 0)] * (a.ndim - 2) + [(0, extra), (0, 0)])


def _group_rows(pieces, gname):
    parts = [_pad_rows(pieces[name], padded) for name, _, padded in GROUPS[gname][2]]
    return parts[0] if len(parts) == 1 else jnp.concatenate(parts, axis=-2)


def _ungroup_rows(buf, gname):
    out, r0 = {}, 0
    for name, rows, padded in GROUPS[gname][2]:
        out[name] = buf[..., r0:r0 + rows, :]
        r0 += padded
    return out


def _shard_major(name, gfull, n_layers):
    per_layer = []
    for g in gfull:
        if g.ndim == 3:
            per_layer.append(g)
        elif name in COLUMN_SHARDED:
            k = g.shape[0]
            per_layer.append(jnp.moveaxis(g.reshape(k, N_DEV, g.shape[1] // N_DEV), 1, 0))
        else:
            per_layer.append(g.reshape(N_DEV, g.shape[0] // N_DEV, -1))
    return per_layer[0] if n_layers == 1 else jnp.concatenate(per_layer, axis=1)


def _natural(name, gathered, n_layers):
    rows = gathered.shape[1] // n_layers
    out = []
    for layer in range(n_layers):
        piece = gathered[:, layer * rows:(layer + 1) * rows]
        if name in COLUMN_SHARDED:
            out.append(jnp.moveaxis(piece, 0, 1).reshape(rows, N_DEV * piece.shape[2]))
        else:
            out.append(piece.reshape(N_DEV * rows, piece.shape[2]))
    return out


def _rows(a, rows):
    flat = a.reshape(-1)
    return jnp.pad(flat, (0, rows * PACK_W - flat.shape[0])).reshape(rows, PACK_W)


def _pack_repl(vals):
    return jnp.concatenate([_rows(vals[name].reshape(shape), rows) for name, shape, rows in REPL_LAYOUT], axis=0)


def _unpack_repl(buf):
    out, r0 = {}, 0
    for name, shape, rows in REPL_LAYOUT:
        n = 1
        for s in shape:
            n *= s
        out[name] = buf[r0:r0 + rows].reshape(-1)[:n].reshape(shape)
        r0 += rows
    return out


WEIGHTS = ("mix_norm_e", "w_in_e", "pool_w", "pool_scale", "w_out_e", "mix_norm_o", "w_in_o", "conv_qkv_o", "a_log_o",
           "dt_bias_o", "gdn_norm_o", "w_out_o", "ffn_norm", "w_up", "ffn_conv", "w_down", "ple_norm", "w_ple_gate",
           "w_ple", "final_norm")


def _ffn_forward(x, hn, w_up, conv_w, w_down, next_gain, tag):
    up = _mm(hn, w_up, name="ffn_up" + tag)
    act = _ffn_act(up, conv_w)
    out, out_n = _mm(act, w_down, res=x, norm_gain=next_gain, name="ffn_down" + tag)
    return out, out_n, (x, hn, up, act)


def _ffn_backward(dx, saved, norm_g, w_up, conv_w, w_down, tag):
    x, hn, up, act = saved
    dact = _mm(dx, w_down, tb=True, name="ffn_dact" + tag)
    d_w_down = _mm(act, dx, ta=True, name="ffn_dwdown" + tag)
    dgate, dval, dcg, dcv = _ffn_act_bwd(up, conv_w, dact)
    dhn = _mm(dgate, w_up[:, :FFN_DIM], tb=True, name="ffn_dhn_g" + tag)
    dx_in, d_norm = _mm(dval, w_up[:, FFN_DIM:], tb=True, res=dhn, rms_bwd=(x, norm_g, dx), name="ffn_dhn_v" + tag)
    halves = [_mm(hn, d, ta=True, name="ffn_dwup_" + side + tag)
              for side, d in (("g", dgate), ("v", dval))]
    d_w_up = jnp.concatenate([jnp.moveaxis(h.reshape(h.shape[0], 4, -1), 1, 0) for h in halves], axis=0)
    return dx_in, d_norm, d_w_up, jnp.concatenate([dcg, dcv], axis=1), d_w_down


def _ple_forward(x, hn, w_gate, p, w_ple, tag):
    out, gl, pe = _ple_fwd(hn, w_gate, p, w_ple, x, name="ple_fwd" + tag)
    return out, (x, hn, gl, pe)


def _ple_backward(dx, saved, norm_g, w_gate, p, tag):
    x, hn, gl, pe = saved
    dpe, dgl = _ple_bwd(dx, gl, pe, name="ple_bwd" + tag)
    d_w_ple = _mm(p, dpe, ta=True, name="ple_dwple" + tag)
    d_w_gate = _mm(hn, dgl, ta=True, name="ple_dwgate" + tag)
    dx, d_norm = _mm(dgl, w_gate, tb=True, rms_bwd=(x, norm_g, dx), name="ple_dhn" + tag)
    return dx, d_norm, d_w_gate, d_w_ple


def kernel(x, p, mix_norm_e, w_in_e, pool_w, pool_scale, w_out_e, mix_norm_o, w_in_o, conv_qkv_o, a_log_o, dt_bias_o, gdn_norm_o, w_out_o, ffn_norm, w_up, ffn_conv, w_down, ple_norm, w_ple_gate, w_ple, final_norm, loss_target, m_mix_norm_e, m_w_in_e, m_pool_w, m_pool_scale, m_w_out_e, m_mix_norm_o, m_w_in_o, m_conv_qkv_o, m_a_log_o, m_dt_bias_o, m_gdn_norm_o, m_w_out_o, m_ffn_norm, m_w_up, m_ffn_conv, m_w_down, m_ple_norm, m_w_ple_gate, m_w_ple, m_final_norm, v_mix_norm_e, v_w_in_e, v_pool_w, v_pool_scale, v_w_out_e, v_mix_norm_o, v_w_in_o, v_conv_qkv_o, v_a_log_o, v_dt_bias_o, v_gdn_norm_o, v_w_out_o, v_ffn_norm, v_w_up, v_ffn_conv, v_w_down, v_ple_norm, v_w_ple_gate, v_w_ple, v_final_norm):
    given = dict(locals())
    place = jnp.stack(_place()).astype(jnp.int32)
    x0, tgt = x[0], loss_target[0]

    def pieces(prefix):
        out = {}
        for width, _, members in GROUPS.values():
            for (name, layer), rows, _ in members:
                a = given[prefix + name]
                out[(name, layer)] = (a if layer is None else a[layer]).reshape(rows, width)
        return out

    def flat2d(name):
        return given[name].reshape(-1, given[name].shape[-1])

    small = ("ffn_conv", "mix_norm_o", "conv_qkv_o")
    got = _all_gather([flat2d("w_in_e").astype(BF16)] + [_pad_rows(flat2d(k), 8) for k in small], name="ag_first")
    full = {("w_in_e", 0): _natural("w_in_e", got[0], 1)[0]}
    for i in range(2):
        full[("ffn_conv", i)] = _natural("ffn_conv", got[1][:, 3 * i:3 * i + 3], 1)[0]
    mix_norm_o_full = got[2][:, 0].reshape(1, D_MODEL)
    conv_qkv = _natural("conv_qkv_o", got[3][:, :4], 1)[0]
    alog_row = jnp.pad(a_log_o, ((0, 0), (8, 112)))
    dt_row = jnp.pad(dt_bias_o, ((0, 0), (8, 112)))
    lw = lambda name, i: full[(name, i)]

    h_e = _rms_fwd(x0, mix_norm_e, name="rms_mix_e")
    proj_e = _mm(h_e, lw("w_in_e", 0), name="in_e")
    pool_o = _pool_fwd(proj_e, pool_w[0], pool_scale)
    wide = ("w_out_e", "w_out_o", "w_down", "w_ple_gate")
    send = [jnp.concatenate([flat2d(k).astype(BF16) for k in wide], axis=0)]
    att_o, lsum, got = _sb_fwd(proj_e, gather=send + [flat2d(k).astype(BF16) for k in ("w_in_o", "w_up", "w_ple")])
    gathered, r0 = {"w_in_o": got[1], "w_up": got[2], "w_ple": got[3]}, 0
    for k in wide:
        gathered[k] = got[0][:, r0:r0 + flat2d(k).shape[0]]
        r0 += flat2d(k).shape[0]
    layers = {name: given[name].shape[0] if given[name].ndim == 3 else 1 for name in gathered}
    full.update({(name, i): w for name in gathered for i, w in enumerate(_natural(name, gathered[name], layers[name]))})
    w_in_o_full = full[("w_in_o", 0)]
    w_in_o_main = w_in_o_full[:, :4096]
    w_in_o_ba = jnp.pad(w_in_o_full[:, 4096:], ((0, 0), (0, 112)))
    mix_e = jnp.concatenate([pool_o, att_o.astype(BF16)], axis=1)
    x1, hf0 = _mm(mix_e, lw("w_out_e", 0), res=x0, norm_gain=ffn_norm[0:1], name="out_e")
    x2, hp0, ffn0 = _ffn_forward(x1, hf0, lw("w_up", 0), lw("ffn_conv", 0), lw("w_down", 0), ple_norm[0:1], "0")
    x3, ple0 = _ple_forward(x2, hp0, lw("w_ple_gate", 0), p[0, 0], lw("w_ple", 0), "0")

    h_o = _rms_fwd(x3, mix_norm_o_full, name="rms_mix_o")
    proj_o = _mm(h_o, w_in_o_main, name="in_o")
    ba = _mm(h_o, w_in_o_ba, name="in_o_ba")
    qkv = _gdn_pre(proj_o, conv_qkv)
    beta, g = _gdn_gate(ba, alog_row, dt_row)
    u, w_c, qk, qd, kd, gl = _gdn_local(qkv, g, beta)
    o, states, vnew = _gdn_scan(u, w_c, qk, qd, kd, gl)
    y_o = _gdn_post(o, proj_o, gdn_norm_o)
    x4, hf1 = _mm(y_o, lw("w_out_o", 0), res=x3, norm_gain=ffn_norm[1:2], name="out_o")
    x5, hp1, ffn1 = _ffn_forward(x4, hf1, lw("w_up", 1), lw("ffn_conv", 1), lw("w_down", 1), ple_norm[1:2], "1")
    x6, ple1 = _ple_forward(x5, hp1, lw("w_ple_gate", 1), p[1, 0], lw("w_ple", 1), "1")
    loss_row, dx, d_final = _final_loss(x6, final_norm.reshape(1, D_MODEL), tgt)

    grads, rgrads = {}, {}
    dx, d_ple1, grads[("w_ple_gate", 1)], grads[("w_ple", 1)] = _ple_backward(dx, ple1, ple_norm[1:2], lw("w_ple_gate", 1), p[1, 0], "1")
    dx, d_ffn1, grads[("w_up", 1)], grads[("ffn_conv", 1)], grads[("w_down", 1)] = _ffn_backward(
        dx, ffn1, ffn_norm[1:2], lw("w_up", 1), lw("ffn_conv", 1), lw("w_down", 1), "1")
    grads[("w_out_o", 0)] = _mm(y_o, dx, ta=True, name="dw_out_o")
    dy_o = _mm(dx, lw("w_out_o", 0), tb=True, name="dy_o")
    do, dz, rgrads["gdn_norm_o"] = _gdn_post_bwd(o, proj_o, gdn_norm_o, dy_o)
    du, dw_c, dqk, dqd, dkd, dgl = _gdn_scan_bwd(do, states, vnew, w_c, qk, qd, kd, gl)
    dqkv_heads, dg, dbeta = _gdn_local_bwd(qkv, g, beta, du, dw_c, dqk, dqd, dkd, dgl)
    dqkv, grads[("conv_qkv_o", 0)] = _gdn_pre_bwd(proj_o, conv_qkv, dqkv_heads)
    dba, d_alog, d_dt = _gdn_gate_bwd(ba, alog_row, dt_row, dbeta, dg)
    rgrads["a_log_o"], rgrads["dt_bias_o"] = d_alog[:, 8:16], d_dt[:, 8:16]
    dproj_o = jnp.concatenate([dqkv, dz], axis=1)
    dh = _mm(dproj_o, w_in_o_main, tb=True, name="dh_o")
    dx_o, d_mix_o = _mm(dba, w_in_o_ba, tb=True, res=dh, rms_bwd=(x3, mix_norm_o_full, dx), name="dh_o_ba")
    grads[("w_in_o", 0)] = jnp.concatenate(
        [_mm(h_o, dproj_o, ta=True, name="dw_in_o"),
         _mm(h_o, dba, ta=True, name="dw_in_o_ba")[:, :16]], axis=1)
    dx = dx_o
    grads[("mix_norm_o", 0)] = d_mix_o

    dx, d_ple0, grads[("w_ple_gate", 0)], grads[("w_ple", 0)] = _ple_backward(dx, ple0, ple_norm[0:1], lw("w_ple_gate", 0), p[0, 0], "0")
    dx, d_ffn0, grads[("w_up", 0)], grads[("ffn_conv", 0)], grads[("w_down", 0)] = _ffn_backward(
        dx, ffn0, ffn_norm[0:1], lw("w_up", 0), lw("ffn_conv", 0), lw("w_down", 0), "0")
    grads[("w_out_e", 0)] = _mm(mix_e, dx, ta=True, name="dw_out_e")
    dmix = _mm(dx, lw("w_out_e", 0), tb=True, name="dmix_e")
    du_e, d_pool_w, rgrads["pool_scale"] = _pool_bwd(proj_e, dmix, pool_w[0], pool_scale)
    rgrads["pool_w"] = d_pool_w[None]

    def reduce_start(gnames, tag):
        smaj = {}
        for g in gnames:
            for (name, layer), _, _ in GROUPS[g][2]:
                of = [grads[(name, i)] for i in ((0, 1) if layer is None else (layer,)) if (name, i) in grads]
                smaj[(name, layer)] = _shard_major(name, of, len(of))
        gbuf = [_group_rows(smaj, g) for g in gnames]
        gbuf = [b.reshape((4, 2) + b.shape[1:]) for b in gbuf]
        got1 = _exchange_sibling(gbuf, name="rs_sibling" + tag)
        part = [_chip_partial(place, b, r, tr=GROUPS[g][1], name="rs_chip_partial_" + g)
                for g, b, r in zip(gnames, gbuf, got1)]
        return gbuf, got1, part

    early = tuple(g for g in GROUPS if g != "in_e")
    gbuf_e, got1_e, part_e = reduce_start(early, "_early")
    dq_e, dk_e, dv_e, got2_e = _sb_bwd(proj_e, lsum, dmix, exchange=part_e)
    dproj_e = jnp.concatenate([du_e, dq_e.astype(BF16), dk_e.astype(BF16), dv_e.astype(BF16)], axis=1)
    grads[("w_in_e", 0)] = _mm(h_e, dproj_e, ta=True, name="dw_in_e")
    dx, rgrads["mix_norm_e"] = _mm(dproj_e, lw("w_in_e", 0), tb=True, rms_bwd=(x0, mix_norm_e, dx), name="dh_e")
    rgrads["ffn_norm"] = jnp.concatenate([d_ffn0, d_ffn1], axis=0)
    rgrads["ple_norm"] = jnp.concatenate([d_ple0, d_ple1], axis=0)
    rgrads["final_norm"] = d_final.reshape(D_MODEL)

    gbuf_l, got1_l, part_l = reduce_start(("in_e",), "_late")
    got2_l = _exchange_chips(part_l)
    wloc, mloc, vloc = pieces(""), pieces("m_"), pieces("v_")
    sh_out = [{}, {}, {}, {}]
    for g, b, r1, r2 in zip(early + ("in_e",), gbuf_e + gbuf_l, list(got1_e) + list(got1_l),
                            list(got2_e) + list(got2_l)):
        res = _adamw_shard(place, b, r1, r2, _group_rows(wloc, g), _group_rows(mloc, g), _group_rows(vloc, g),
                           tr=GROUPS[g][1], name="adamw_" + g)
        for kind in range(4):
            sh_out[kind].update(_ungroup_rows(res[kind], g))

    (rparts,) = _all_gather([_pack_repl(rgrads)], name="ag_repl_grads")
    rp_out = _adamw_replicated(rparts, _pack_repl({n: given[n] for n, _, _ in REPL_LAYOUT}),
                               _pack_repl({n: given["m_" + n] for n, _, _ in REPL_LAYOUT}),
                               _pack_repl({n: given["v_" + n] for n, _, _ in REPL_LAYOUT}))
    rp_out = [_unpack_repl(b) for b in rp_out]

    def leaf(kind, name):
        if name in SHARDED:
            mine = sh_out[kind]
            whole = mine[(name, None)] if (name, None) in mine else jnp.stack([mine[(name, 0)], mine[(name, 1)]])
            return whole.reshape(given[name].shape)
        return rp_out[kind][name]

    loss = lax.psum(loss_row[0, 0], ("x", "y", "c"))
    outs = [loss, dx[None]]
    for kind in range(4):
        outs += [leaf(kind, n) for n in WEIGHTS]
    return tuple(outs)
```

```python
import jax
import jax.numpy as jnp
from jax import lax
from jax.experimental import pallas as pl
from jax.experimental.pallas import tpu as pltpu

F32 = jnp.float32
BF16 = jnp.bfloat16

D_MODEL = 1024
PLE_DIM = 256
POOL_WINDOWS = (2, 4, 8, 16)
SB_HEAD_DIM = 64
SB_BLOCK = 1024
SB_KBLOCK = 256
GDN_HEADS = 8
GDN_HEAD_DIM = 128
GDN_CONV = 4
GDN_CHUNK = 64
FFN_DIM = 2816
FFN_CONV = 3
EPS = 1e-6
ADAM_LR, ADAM_B1, ADAM_B2, ADAM_EPS, ADAM_WD, ADAM_STEP = 0.001, 0.9, 0.999, 1e-08, 0.01, 10
N_DEV = 8
MESH = pl.DeviceIdType.MESH
VMEM_LIMIT = 56 * 1024 * 1024

NN = (((1,), (0,)), ((), ()))
NT = (((1,), (1,)), ((), ()))
TN = (((0,), (0,)), ((), ()))


def _params(*sem):
    return pltpu.CompilerParams(dimension_semantics=sem if sem else None, vmem_limit_bytes=VMEM_LIMIT)


def _dot(a, b, dims):
    return lax.dot_general(a.astype(BF16), b.astype(BF16), dims, preferred_element_type=F32)


def _iota(shape, axis):
    return lax.broadcasted_iota(jnp.int32, shape, axis)


MM_TILE, MM_TILE_11 = 1024, 1408


def _mm_tile(dim):
    if dim <= MM_TILE_11:
        return dim
    return MM_TILE if dim % MM_TILE == 0 else MM_TILE_11


def _mm(a, b, *, ta=False, tb=False, res=None, norm_gain=None, rms_bwd=None, k_block0=0, shard_cols=None, name):
    M, K = (a.shape[1], a.shape[0]) if ta else a.shape
    N = b.shape[0] if tb else b.shape[1]
    tm, tn, tk = _mm_tile(M), _mm_tile(N), _mm_tile(K)
    if rms_bwd is not None:
        tm = min(tm, 512)
    assert M % tm == 0 and N % tn == 0 and K % tk == 0, (name, M, N, K, tm, tn, tk)
    assert (norm_gain is None and rms_bwd is None) or tn == N, name
    nk = K // tk
    dims = (((0 if ta else 1,), (1 if tb else 0,)), ((), ()))
    extra = [res] if res is not None else []
    vecs = [norm_gain] if norm_gain is not None else []
    if rms_bwd is not None:
        extra += [rms_bwd[0], rms_bwd[2]]
        vecs = [rms_bwd[1]]
    n_out = 1 if (norm_gain is None and rms_bwd is None) else 2

    def body(*refs):
        a_ref, b_ref = refs[:2]
        tiles = list(refs[2:2 + len(extra)])
        vec_refs = refs[2 + len(extra):2 + len(extra) + len(vecs)]
        outs = refs[2 + len(extra) + len(vecs):2 + len(extra) + len(vecs) + n_out]
        scr = refs[2 + len(extra) + len(vecs) + n_out:]
        p = _dot(a_ref[...], b_ref[...], dims)

        def fin(acc):
            if res is not None:
                acc = acc + tiles[0][...]
            if rms_bwd is not None:
                x_ref, dres_ref = tiles[-2:]
                xv = x_ref[...]
                r = lax.rsqrt(jnp.mean(xv * xv, axis=-1, keepdims=True) + EPS)
                xn = xv * r
                dgp = jnp.sum(acc * xn, axis=0, keepdims=True)
                dyg = acc * vec_refs[0][...]
                outs[0][...] = dres_ref[...] + r * (dyg - xn * jnp.mean(dyg * xn, axis=-1, keepdims=True))
                first = pl.program_id(0) == 0

                @pl.when(first)
                def _():
                    outs[1][...] = dgp

                @pl.when(jnp.logical_not(first))
                def _():
                    outs[1][...] += dgp
                return
            if shard_cols is not None:
                for s in range(tn // shard_cols):
                    outs[0][s] = acc[:, shard_cols * s:shard_cols * (s + 1)]
                return
            outs[0][...] = acc
            if norm_gain is not None:
                r = lax.rsqrt(jnp.mean(acc * acc, axis=-1, keepdims=True) + EPS)
                outs[1][...] = (acc * r * vec_refs[0][...]).astype(BF16)

        if nk == 1:
            fin(p)
        else:
            acc_ref = scr[0]
            k = pl.program_id(2)

            @pl.when(k == 0)
            def _():
                acc_ref[...] = p

            @pl.when(k > 0)
            def _():
                acc_ref[...] += p

            @pl.when(k == nk - 1)
            def _():
                fin(acc_ref[...])

    a_spec = pl.BlockSpec((tk, tm), lambda i, j, k: (k, i)) if ta else pl.BlockSpec((tm, tk), lambda i, j, k: (i, k))
    b_spec = (pl.BlockSpec((tn, tk), lambda i, j, k: (j, k + k_block0)) if tb
              else pl.BlockSpec((tk, tn), lambda i, j, k: (k, j)))
    o_spec = pl.BlockSpec((tm, tn), lambda i, j, k: (i, j))
    v_spec = pl.BlockSpec((1, tn), lambda i, j, k: (0, j))
    out_specs, out_shape = [o_spec], [jax.ShapeDtypeStruct((M, N), F32)]
    if shard_cols is not None:
        per = tn // shard_cols
        out_specs = [pl.BlockSpec((per, tm, shard_cols), lambda i, j, k: (j, i, 0))]
        out_shape = [jax.ShapeDtypeStruct((N // shard_cols, M, shard_cols), F32)]
    if norm_gain is not None:
        out_specs, out_shape = out_specs + [o_spec], out_shape + [jax.ShapeDtypeStruct((M, N), BF16)]
    if rms_bwd is not None:
        out_specs, out_shape = out_specs + [v_spec], out_shape + [jax.ShapeDtypeStruct((1, N), F32)]
    out = pl.pallas_call(
        body, name=name, grid=(M // tm, N // tn, nk),
        in_specs=[a_spec, b_spec] + [o_spec] * len(extra) + [v_spec] * len(vecs), out_specs=out_specs,
        out_shape=out_shape, scratch_shapes=[pltpu.VMEM((tm, tn), F32)] if nk > 1 else [],
        compiler_params=_params("arbitrary" if rms_bwd is not None else "parallel", "parallel", "arbitrary"),
    )(a, b, *extra, *vecs)
    return out[0] if n_out == 1 else out


def _rms_fwd(x, gain, *, name, tr=512):
    T, Dm = x.shape

    def body(x_ref, g_ref, o_ref):
        xv = x_ref[...]
        r = lax.rsqrt(jnp.mean(xv * xv, axis=-1, keepdims=True) + EPS)
        o_ref[...] = (xv * r * g_ref[...]).astype(BF16)

    return pl.pallas_call(
        body, name=name, grid=(T // tr,),
        in_specs=[pl.BlockSpec((tr, Dm), lambda i: (i, 0)), pl.BlockSpec((1, Dm), lambda i: (0, 0))],
        out_specs=pl.BlockSpec((tr, Dm), lambda i: (i, 0)),
        out_shape=jax.ShapeDtypeStruct((T, Dm), BF16), compiler_params=_params("parallel"),
    )(x, gain)


def _final_loss(x, gain, target, *, tr=512):
    T, Dm = x.shape

    def body(x_ref, g_ref, t_ref, loss_ref, dx_ref, dg_ref):
        i = pl.program_id(0)
        xv = x_ref[...]
        g = g_ref[...]
        r = lax.rsqrt(jnp.mean(xv * xv, axis=-1, keepdims=True) + EPS)
        xn = xv * r
        err = xn * g - t_ref[...]
        lp = jnp.zeros((1, 128), F32) + 0.5 * jnp.sum(jnp.mean(err * err, axis=-1, keepdims=True))
        dy_v = err * (1.0 / Dm)
        dgp = jnp.sum(dy_v * xn, axis=0, keepdims=True)
        dyg = dy_v * g
        dx_ref[...] = r * (dyg - xn * jnp.mean(dyg * xn, axis=-1, keepdims=True))

        @pl.when(i == 0)
        def _():
            dg_ref[...] = dgp
            loss_ref[...] = lp

        @pl.when(i > 0)
        def _():
            dg_ref[...] += dgp
            loss_ref[...] += lp

    row = pl.BlockSpec((tr, Dm), lambda i: (i, 0))
    vec = pl.BlockSpec((1, Dm), lambda i: (0, 0))
    return pl.pallas_call(
        body, name="final_loss", grid=(T // tr,), in_specs=[row, vec, row],
        out_specs=[pl.BlockSpec((1, 128), lambda i: (0, 0)), row, vec],
        out_shape=[jax.ShapeDtypeStruct((1, 128), F32), jax.ShapeDtypeStruct((T, Dm), F32),
                   jax.ShapeDtypeStruct((1, Dm), F32)],
        compiler_params=_params("arbitrary"),
    )(x, gain, target)


def _prev_spec(tr, cb, pad, col):
    return pl.BlockSpec((pad, cb), lambda *g: (jnp.maximum(g[0] * (tr // pad) - 1, 0), col(*g)))


def _next_spec(tr, cb, pad, col, T):
    return pl.BlockSpec((pad, cb), lambda *g: (jnp.minimum((g[0] + 1) * (tr // pad), T // pad - 1), col(*g)))


def _conv_rows(x_ext, w_ref, K, pad, cs=slice(None)):
    y = w_ref[K - 1:K, cs] * x_ext
    for i in range(K - 1):
        y = y + w_ref[i:i + 1, cs] * pltpu.roll(x_ext, K - 1 - i, 0)
    return y[pad:]


def _pool_y(u_ext, g, i, tr):
    s = u_ext
    for sh in (1, 2, 4, 8)[:g + 1]:
        s = s + pltpu.roll(s, sh, 0)
    t = i * tr + _iota((tr, 128), 0)
    cnt = jnp.minimum(t + 1, POOL_WINDOWS[g]).astype(F32)
    return s[16:] / cnt - u_ext[16:]


def _pool_fwd(proj, pool_w, pool_scale, *, tr=512):
    T = proj.shape[0]

    def body(u_ref, uh_ref, w_ref, s_ref, o_ref):
        i = pl.program_id(0)
        uh = jnp.where(i > 0, uh_ref[...], 0.0)
        for g in range(4):
            cs = slice(128 * g, 128 * (g + 1))
            y = _pool_y(jnp.concatenate([uh[:, cs], u_ref[:, cs]], axis=0), g, i, tr)
            o_ref[:, cs] = (_dot(y, w_ref[g], NN) * s_ref[:, cs]).astype(BF16)

    return pl.pallas_call(
        body, name="pool_fwd", grid=(T // tr,),
        in_specs=[pl.BlockSpec((tr, 512), lambda i: (i, 0)), _prev_spec(tr, 512, 16, lambda i: 0),
                  pl.BlockSpec((4, 128, 128), lambda i: (0, 0, 0)), pl.BlockSpec((1, 512), lambda i: (0, 0))],
        out_specs=pl.BlockSpec((tr, 512), lambda i: (i, 0)),
        out_shape=jax.ShapeDtypeStruct((T, 512), BF16), compiler_params=_params("parallel"),
    )(proj, proj, pool_w, pool_scale)


def _pool_bwd(proj, dout, pool_w, pool_scale, *, tr=512):
    T = proj.shape[0]
    nb = T // tr

    def body(u_ref, uh_ref, d_ref, dn_ref, w_ref, s_ref, du_ref, dw_ref, ds_ref):
        i = pl.program_id(0)
        uh = jnp.where(i > 0, uh_ref[...], 0.0)
        dn = jnp.where(i < nb - 1, dn_ref[...], 0.0)
        t_ext = i * tr + _iota((tr + 16, 128), 0)
        for g in range(4):
            cs = slice(128 * g, 128 * (g + 1))
            sc = s_ref[:, cs]
            wg = w_ref[g]
            y = _pool_y(jnp.concatenate([uh[:, cs], u_ref[:, cs]], axis=0), g, i, tr)
            dg = d_ref[:, cs]
            dsp = jnp.sum(dg * _dot(y, wg, NN), axis=0, keepdims=True)
            dyw = dg * sc
            dwp = _dot(y, dyw, TN)
            dy_ext = _dot(jnp.concatenate([dyw, dn[:, cs] * sc], axis=0), wg, NT)
            cnt = jnp.minimum(t_ext + 1, POOL_WINDOWS[g]).astype(F32)
            s = dy_ext / cnt
            for sh in (1, 2, 4, 8)[:g + 1]:
                s = s + pltpu.roll(s, tr + 16 - sh, 0)
            du_ref[:, cs] = (s[:tr] - dy_ext[:tr]).astype(BF16)

            @pl.when(i == 0)
            def _():
                dw_ref[g] = dwp
                ds_ref[:, cs] = dsp

            @pl.when(i > 0)
            def _():
                dw_ref[g] += dwp
                ds_ref[:, cs] += dsp

    row = pl.BlockSpec((tr, 512), lambda i: (i, 0))
    return pl.pallas_call(
        body, name="pool_bwd", grid=(nb,),
        in_specs=[row, _prev_spec(tr, 512, 16, lambda i: 0), row, _next_spec(tr, 512, 16, lambda i: 0, T),
                  pl.BlockSpec((4, 128, 128), lambda i: (0, 0, 0)), pl.BlockSpec((1, 512), lambda i: (0, 0))],
        out_specs=[row, pl.BlockSpec((4, 128, 128), lambda i: (0, 0, 0)), pl.BlockSpec((1, 512), lambda i: (0, 0))],
        out_shape=[jax.ShapeDtypeStruct((T, 512), BF16), jax.ShapeDtypeStruct((4, 128, 128), F32),
                   jax.ShapeDtypeStruct((1, 512), F32)],
        compiler_params=_params("arbitrary"),
    )(proj, proj, dout, dout, pool_w, pool_scale)


def _split_dot(x, tri):
    hi = x.astype(BF16)
    lo = (x - hi.astype(F32)).astype(BF16)
    return (lax.dot_general(hi, tri, NN, preferred_element_type=F32)
            + lax.dot_general(lo, tri, NN, preferred_element_type=F32))


def _log1m(z):
    return -(jnp.maximum(z, 0.0) + jnp.log(1.0 + jnp.exp(-jnp.abs(z))))


def _sb_fwd(proj, gather=()):
    T = proj.shape[0]
    B, BK = min(SB_BLOCK, T), SB_KBLOCK
    R = B // BK
    nq, n = T // B, len(gather)
    scale = SB_HEAD_DIM ** -0.5

    def body(q_ref, k_ref, v_ref, *rest):
        o_ref, ls_ref = rest[n:n + 2]
        hp, i = pl.program_id(0), pl.program_id(1)
        if n:
            start, forward, finish = _gather_stages(rest[:n], rest[n + 2:2 * n + 2], *rest[2 * n + 2:])
            pl.when((hp == 0) & (i == 0))(start)
            pl.when((hp == 3) & (i == nq - 1))(forward)
        lane = _iota((1, 128), 1)
        tri_gt = (_iota((BK, BK), 0) > _iota((BK, BK), 1)).astype(BF16)
        row, col = _iota((B, BK), 0), _iota((B, BK), 1)
        qv = q_ref[...] * scale
        hms = [(lane >= 64 * h) & (lane < 64 * (h + 1)) for h in range(2)]
        qhs = [jnp.where(hm, qv, 0.0).astype(BF16) for hm in hms]

        def tile(j, carry, d):
            rows = pl.ds(pl.multiple_of(j * BK, BK), BK)
            kj = k_ref[rows, :].astype(BF16)
            vj = v_ref[rows, :].astype(BF16)
            r0 = 0 if d is None else BK * d
            valid = None if d is None else (col[r0:] < row[:B - r0])
            out = []
            for h in range(2):
                c, acc = carry[h]
                z = lax.dot_general(qhs[h][r0:], kj, NT, preferred_element_type=F32)
                lg = _log1m(z)
                if d is not None:
                    lg = jnp.where(valid, lg, 0.0)
                a = jnp.exp(z + lg + _split_dot(lg, tri_gt) + c[r0:])
                if d is not None:
                    a = jnp.where(valid, a, 0.0)
                upd = (c[r0:] + jnp.sum(lg, axis=1, keepdims=True),
                       acc[r0:] + lax.dot_general(a.astype(BF16), vj, NN, preferred_element_type=F32))
                out.append(upd if r0 == 0 else tuple(jnp.concatenate([old[:r0], new], axis=0)
                                                     for old, new in zip((c, acc), upd)))
            return tuple(out)

        zero = (jnp.zeros((B, 1), F32), jnp.zeros((B, 128), F32))
        carry = (zero, zero)
        for d in reversed(range(R)):
            carry = tile(i * R + d, carry, d)
        carry = lax.fori_loop(0, i * R, lambda s, cr: tile(i * R - 1 - s, cr, None), carry)
        o_ref[...] = jnp.where(hms[0], carry[0][1], carry[1][1])
        ls_ref[...] = jnp.where(hms[0], carry[0][0], carry[1][0])
        if n:
            pl.when((hp == 3) & (i == nq - 1))(finish)

    blk = pl.BlockSpec((B, 128), lambda hp, i: (i, hp))
    out = pl.pallas_call(
        body, name="sb_fwd", grid=(4, nq),
        in_specs=[pl.BlockSpec((B, 128), lambda hp, i: (i, 4 + hp)),
                  pl.BlockSpec((T, 128), lambda hp, i: (0, 8 + hp)),
                  pl.BlockSpec((T, 128), lambda hp, i: (0, 12 + hp))] + [HBM_SPEC] * n,
        out_specs=[blk, blk] + [HBM_SPEC] * n,
        out_shape=[jax.ShapeDtypeStruct((T, 512), F32)] * 2 + _gather_shapes(gather),
        scratch_shapes=_gather_sems(n) if n else [],
        compiler_params=_params("arbitrary", "arbitrary"),
    )(proj, proj, proj, *gather)
    return out[0], out[1], list(out[2:])


def _sb_bwd(proj, lsum, dout, exchange=()):
    T = proj.shape[0]
    B, BK = min(SB_BLOCK, T), SB_KBLOCK
    R = B // BK
    nq, n = T // B, len(exchange)
    scale = SB_HEAD_DIM ** -0.5

    def body(q_ref, k_ref, v_ref, do_ref, ls_ref, *rest):
        dq_ref, dk_ref, dv_ref = rest[n:n + 3]
        hp, i = pl.program_id(0), pl.program_id(1)
        if n:
            start, finish = _chips_stages(rest[:n], rest[n + 3:2 * n + 3], *rest[2 * n + 3:])
            pl.when((hp == 0) & (i == 0))(start)

        @pl.when(i == 0)
        def _():
            dk_ref[...] = jnp.zeros_like(dk_ref)
            dv_ref[...] = jnp.zeros_like(dv_ref)

        lane = _iota((1, 128), 1)
        tri_le = (_iota((BK, BK), 0) <= _iota((BK, BK), 1)).astype(BF16)
        tri_lt = (_iota((BK, BK), 0) < _iota((BK, BK), 1)).astype(BF16)
        row, col = _iota((B, BK), 0), _iota((B, BK), 1)
        qv = q_ref[...] * scale
        dov = do_ref[...]
        hms = [(lane >= 64 * h) & (lane < 64 * (h + 1)) for h in range(2)]
        qhs = [jnp.where(hm, qv, 0.0).astype(BF16) for hm in hms]
        dos = [jnp.where(hm, dov, 0.0).astype(BF16) for hm in hms]
        ltots = [ls_ref[:, 64 * h:64 * h + 1] for h in range(2)]

        def tile(j, carry, d):
            rows = pl.ds(pl.multiple_of(j * BK, BK), BK)
            kj = k_ref[rows, :].astype(BF16)
            vj = v_ref[rows, :].astype(BF16)
            diag = d is not None
            r0 = BK * d if diag else 0
            valid = (col[r0:] < row[:B - r0]) if diag else None
            out = []
            dkj = jnp.zeros((BK, 128), F32)
            dvj = jnp.zeros((BK, 128), F32)
            for h in range(2):
                lbef, ebef, dqa = carry[h]
                qh, do_h = qhs[h][r0:], dos[h][r0:]
                z = lax.dot_general(qh, kj, NT, preferred_element_type=F32)
                lg = _log1m(z)
                if diag:
                    lg = jnp.where(valid, lg, 0.0)
                a = jnp.exp(z + lg + (ltots[h][r0:] - lbef[r0:] - _split_dot(lg, tri_le)))
                if diag:
                    a = jnp.where(valid, a, 0.0)
                e = a * lax.dot_general(do_h, vj, NT, preferred_element_type=F32)
                dz = e * jnp.exp(lg) - jnp.exp(z + lg) * (ebef[r0:] + _split_dot(e, tri_lt))
                if diag:
                    dz = jnp.where(valid, dz, 0.0)
                dzb = dz.astype(BF16)
                dkj = dkj + lax.dot_general(dzb, qh, TN, preferred_element_type=F32)
                dvj = dvj + lax.dot_general(a.astype(BF16), do_h, TN, preferred_element_type=F32)
                upd = (lbef[r0:] + jnp.sum(lg, axis=1, keepdims=True), ebef[r0:] + jnp.sum(e, axis=1, keepdims=True),
                       dqa[r0:] + lax.dot_general(dzb, kj, NN, preferred_element_type=F32))
                out.append(upd if r0 == 0 else tuple(jnp.concatenate([old[:r0], new], axis=0)
                                                     for old, new in zip(carry[h], upd)))
            dk_ref[rows, :] += dkj
            dv_ref[rows, :] += dvj
            return tuple(out)

        zero = (jnp.zeros((B, 1), F32), jnp.zeros((B, 1), F32), jnp.zeros((B, 128), F32))
        carry = lax.fori_loop(0, i * R, lambda j, cr: tile(j, cr, None), (zero, zero))
        for d in range(R):
            carry = tile(i * R + d, carry, d)
        dq_ref[...] = jnp.where(hms[0], carry[0][2], carry[1][2]) * scale
        if n:
            pl.when((hp == 3) & (i == nq - 1))(finish)

    full = pl.BlockSpec((T, 128), lambda hp, i: (0, hp))
    blk = pl.BlockSpec((B, 128), lambda hp, i: (i, hp))
    out = pl.pallas_call(
        body, name="sb_bwd", grid=(4, nq),
        in_specs=[pl.BlockSpec((B, 128), lambda hp, i: (i, 4 + hp)),
                  pl.BlockSpec((T, 128), lambda hp, i: (0, 8 + hp)),
                  pl.BlockSpec((T, 128), lambda hp, i: (0, 12 + hp)),
                  pl.BlockSpec((B, 128), lambda hp, i: (i, 4 + hp)), blk] + [HBM_SPEC] * n,
        out_specs=[blk, full, full] + [HBM_SPEC] * n,
        out_shape=[jax.ShapeDtypeStruct((T, 512), F32)] * 3 + _chips_shapes(exchange),
        scratch_shapes=_chips_sems(n) if n else [],
        compiler_params=_params("arbitrary", "arbitrary"),
    )(proj, proj, proj, dout, lsum, *exchange)
    return out[0], out[1], out[2], list(out[3:])


def _sigmoid(x):
    return 1.0 / (1.0 + jnp.exp(-x))


def _silu_mul(cg, cv):
    return cg * _sigmoid(cg) * cv


def _ffn_act(up, conv_w, *, tr=512, cb=256):
    T, F2 = up.shape
    nc = F2 // 2 // cb
    K = FFN_CONV

    def body(g_ref, gh_ref, v_ref, vh_ref, wg_ref, wv_ref, o_ref):
        i = pl.program_id(0)
        gh = jnp.where(i > 0, gh_ref[...], 0.0)
        vh = jnp.where(i > 0, vh_ref[...], 0.0)
        cg = _conv_rows(jnp.concatenate([gh, g_ref[...]], axis=0), wg_ref, K, 8)
        cv = _conv_rows(jnp.concatenate([vh, v_ref[...]], axis=0), wv_ref, K, 8)
        o_ref[...] = _silu_mul(cg, cv).astype(BF16)

    return pl.pallas_call(
        body, name="ffn_act", grid=(T // tr, nc),
        in_specs=[pl.BlockSpec((tr, cb), lambda i, j: (i, j)), _prev_spec(tr, cb, 8, lambda i, j: j),
                  pl.BlockSpec((tr, cb), lambda i, j: (i, nc + j)), _prev_spec(tr, cb, 8, lambda i, j: nc + j),
                  pl.BlockSpec((K, cb), lambda i, j: (0, j)), pl.BlockSpec((K, cb), lambda i, j: (0, nc + j))],
        out_specs=pl.BlockSpec((tr, cb), lambda i, j: (i, j)),
        out_shape=jax.ShapeDtypeStruct((T, F2 // 2), BF16), compiler_params=_params("parallel", "parallel"),
    )(up, up, up, up, conv_w, conv_w)


def _conv_bwd_rows(dc_ext, x_ext, w_ref, K, tr, cs=slice(None)):
    n = tr + 8
    dx = w_ref[K - 1:K, cs] * dc_ext
    for i in range(K - 1):
        dx = dx + w_ref[i:i + 1, cs] * pltpu.roll(dc_ext, n - (K - 1 - i), 0)
    dc = dc_ext[:tr]
    dws = [jnp.sum(dc * pltpu.roll(x_ext, K - 1 - i, 0)[8:8 + tr], axis=0, keepdims=True) for i in range(K)]
    return dx[:tr], dws


def _acc_rows(ref, rows, first, cs=slice(None)):
    for i, r in enumerate(rows):
        @pl.when(first)
        def _():
            ref[i:i + 1, cs] = r

        @pl.when(jnp.logical_not(first))
        def _():
            ref[i:i + 1, cs] += r


def _ffn_act_bwd(up, conv_w, dact, *, tr=512, cb=256):
    T, F2 = up.shape
    F = F2 // 2
    nc, nb = F // cb, T // tr
    K = FFN_CONV

    def body(g_ref, gp_ref, gn_ref, v_ref, vp_ref, vn_ref, d_ref, dn_ref, wg_ref, wv_ref,
             dg_ref, dv_ref, dwg_ref, dwv_ref):
        i = pl.program_id(1)
        first, last = i == 0, i == nb - 1
        g_ext = jnp.concatenate([jnp.where(first, 0.0, gp_ref[...]), g_ref[...], jnp.where(last, 0.0, gn_ref[...])], axis=0)
        v_ext = jnp.concatenate([jnp.where(first, 0.0, vp_ref[...]), v_ref[...], jnp.where(last, 0.0, vn_ref[...])], axis=0)
        d_ext = jnp.concatenate([d_ref[...], jnp.where(last, 0.0, dn_ref[...])], axis=0)
        cg = _conv_rows(g_ext, wg_ref, K, 8)
        cv = _conv_rows(v_ext, wv_ref, K, 8)
        s = _sigmoid(cg)
        t = cg * s
        dcv = d_ext * t
        dcg = d_ext * cv * (s + t * (1.0 - s))
        dg, dwg = _conv_bwd_rows(dcg, g_ext, wg_ref, K, tr)
        dv, dwv = _conv_bwd_rows(dcv, v_ext, wv_ref, K, tr)
        dg_ref[...] = dg.astype(BF16)
        dv_ref[...] = dv.astype(BF16)
        _acc_rows(dwg_ref, dwg, first)
        _acc_rows(dwv_ref, dwv, first)

    blk = lambda off: pl.BlockSpec((tr, cb), lambda j, i: (i, off + j))
    prev = lambda off: pl.BlockSpec((8, cb), lambda j, i: (jnp.maximum(i * (tr // 8) - 1, 0), off + j))
    nxt = lambda off: pl.BlockSpec((8, cb), lambda j, i: (jnp.minimum((i + 1) * (tr // 8), T // 8 - 1), off + j))
    wsp = lambda off: pl.BlockSpec((K, cb), lambda j, i: (0, off + j))
    return pl.pallas_call(
        body, name="ffn_act_bwd", grid=(nc, nb),
        in_specs=[blk(0), prev(0), nxt(0), blk(nc), prev(nc), nxt(nc), blk(0), nxt(0), wsp(0), wsp(nc)],
        out_specs=[blk(0), blk(0), wsp(0), wsp(0)],
        out_shape=[jax.ShapeDtypeStruct((T, F), BF16)] * 2 + [jax.ShapeDtypeStruct((K, F), F32)] * 2,
        compiler_params=_params("parallel", "arbitrary"),
    )(up, up, up, up, up, up, dact, dact, conv_w, conv_w)


def _ple_fwd(hn, w_gate, p, w_ple, x, *, name, tm=1024, tn=512):
    T, Dm = x.shape
    tm = min(tm, T)

    def body(a_ref, b_ref, p_ref, wp_ref, x_ref, o_ref, gl_ref, pe_ref):
        gl = _dot(a_ref[...], b_ref[...], NN)
        pe = _dot(p_ref[...], wp_ref[...], NN)
        gl_ref[...] = gl
        pe_ref[...] = pe
        o_ref[...] = x_ref[...] + pe * _sigmoid(gl)

    o_spec = pl.BlockSpec((tm, tn), lambda i, j: (i, j))
    return pl.pallas_call(
        body, name=name, grid=(T // tm, Dm // tn),
        in_specs=[pl.BlockSpec((tm, Dm), lambda i, j: (i, 0)), pl.BlockSpec((Dm, tn), lambda i, j: (0, j)),
                  pl.BlockSpec((tm, PLE_DIM), lambda i, j: (i, 0)), pl.BlockSpec((PLE_DIM, tn), lambda i, j: (0, j)),
                  o_spec],
        out_specs=[o_spec] * 3, out_shape=[jax.ShapeDtypeStruct((T, Dm), F32)] * 3,
        compiler_params=_params("parallel", "parallel"),
    )(hn, w_gate, p, w_ple, x)


def _ple_bwd(dx, gl, pe, *, name, tr=512):
    T, Dm = dx.shape

    def body(dx_ref, gl_ref, pe_ref, dpe_ref, dgl_ref):
        g = _sigmoid(gl_ref[...])
        d = dx_ref[...]
        dpe_ref[...] = (d * g).astype(BF16)
        dgl_ref[...] = (d * pe_ref[...] * g * (1.0 - g)).astype(BF16)

    row = pl.BlockSpec((tr, Dm), lambda i: (i, 0))
    return pl.pallas_call(
        body, name=name, grid=(T // tr,), in_specs=[row] * 3, out_specs=[row] * 2,
        out_shape=[jax.ShapeDtypeStruct((T, Dm), BF16)] * 2, compiler_params=_params("parallel"),
    )(dx, gl, pe)


def _qkv_act(c, cb):
    s = c * _sigmoid(c)
    n = s * lax.rsqrt(jnp.sum(s * s, axis=-1, keepdims=True) + EPS)
    n = n * jnp.where(cb < GDN_HEADS, GDN_HEAD_DIM ** -0.5, 1.0)
    return jnp.where(cb < 2 * GDN_HEADS, n, s)


GDN_HPS = 4


def _gdn_pre(proj, conv_w, *, tr=512):
    T = proj.shape[0]
    K = GDN_CONV

    def body(x_ref, xh_ref, w_ref, o_ref):
        i, j = pl.program_id(0), pl.program_id(1)
        xh = jnp.where(i > 0, xh_ref[...], 0.0)
        for hh in range(GDN_HPS):
            cs = slice(128 * hh, 128 * (hh + 1))
            c = _conv_rows(jnp.concatenate([xh[:, cs], x_ref[:, cs]], axis=0), w_ref, K, 8, cs)
            o_ref[hh] = _qkv_act(c, GDN_HPS * j + hh)

    wide = 128 * GDN_HPS
    return pl.pallas_call(
        body, name="gdn_pre", grid=(T // tr, 24 // GDN_HPS),
        in_specs=[pl.BlockSpec((tr, wide), lambda i, j: (i, j)), _prev_spec(tr, wide, 8, lambda i, j: j),
                  pl.BlockSpec((K, wide), lambda i, j: (0, j))],
        out_specs=pl.BlockSpec((GDN_HPS, tr, 128), lambda i, j: (j, i, 0)),
        out_shape=jax.ShapeDtypeStruct((24, T, 128), F32), compiler_params=_params("parallel", "parallel"),
    )(proj, proj, conv_w)


def _gdn_pre_bwd(proj, conv_w, dqkv, *, tr=512):
    T = proj.shape[0]
    nb = T // tr
    K = GDN_CONV

    def body(x_ref, xp_ref, xn_ref, d_ref, dn_ref, w_ref, dx_ref, dw_ref):
        j, i = pl.program_id(0), pl.program_id(1)
        first, last = i == 0, i == nb - 1
        xp = jnp.where(first, 0.0, xp_ref[...])
        xn = jnp.where(last, 0.0, xn_ref[...])
        for hh in range(GDN_HPS):
            cs = slice(128 * hh, 128 * (hh + 1))
            x_ext = jnp.concatenate([xp[:, cs], x_ref[:, cs], xn[:, cs]], axis=0)
            d_ext = jnp.concatenate([d_ref[hh], jnp.where(last, 0.0, dn_ref[hh])], axis=0)
            c = _conv_rows(x_ext, w_ref, K, 8, cs)
            _, vjp = jax.vjp(lambda c_: _qkv_act(c_, GDN_HPS * j + hh), c)
            (dc,) = vjp(d_ext)
            dx, dws = _conv_bwd_rows(dc, x_ext, w_ref, K, tr, cs)
            dx_ref[:, cs] = dx.astype(BF16)
            _acc_rows(dw_ref, dws, first, cs)

    wide = 128 * GDN_HPS
    return pl.pallas_call(
        body, name="gdn_pre_bwd", grid=(24 // GDN_HPS, nb),
        in_specs=[pl.BlockSpec((tr, wide), lambda j, i: (i, j)),
                  pl.BlockSpec((8, wide), lambda j, i: (jnp.maximum(i * (tr // 8) - 1, 0), j)),
                  pl.BlockSpec((8, wide), lambda j, i: (jnp.minimum((i + 1) * (tr // 8), T // 8 - 1), j)),
                  pl.BlockSpec((GDN_HPS, tr, 128), lambda j, i: (j, i, 0)),
                  pl.BlockSpec((GDN_HPS, 8, 128), lambda j, i: (j, jnp.minimum((i + 1) * (tr // 8), T // 8 - 1), 0)),
                  pl.BlockSpec((K, wide), lambda j, i: (0, j))],
        out_specs=[pl.BlockSpec((tr, wide), lambda j, i: (i, j)), pl.BlockSpec((K, wide), lambda j, i: (0, j))],
        out_shape=[jax.ShapeDtypeStruct((T, 24 * 128), BF16), jax.ShapeDtypeStruct((K, 24 * 128), F32)],
        compiler_params=_params("parallel", "arbitrary"),
    )(proj, proj, proj, dqkv, dqkv, conv_w)


def _gate_fn(ba, alog_row, dt_row):
    lane = _iota((1, 128), 1)
    x = ba + dt_row
    sp = jnp.maximum(x, 0.0) + jnp.log(1.0 + jnp.exp(-jnp.abs(x)))
    return jnp.where(lane < GDN_HEADS, _sigmoid(ba), -jnp.exp(alog_row) * sp)


def _gdn_gate(ba, alog_row, dt_row, *, tr=512):
    T = ba.shape[0]

    def body(ba_ref, al_ref, dt_ref, b_ref, g_ref):
        val = _gate_fn(ba_ref[...], al_ref[...], dt_ref[...])
        for h in range(GDN_HEADS):
            b_ref[h] = val[:, h:h + 1]
            g_ref[h] = val[:, GDN_HEADS + h:GDN_HEADS + h + 1]

    vec = pl.BlockSpec((1, 128), lambda i: (0, 0))
    hm = pl.BlockSpec((GDN_HEADS, tr, 1), lambda i: (0, i, 0))
    return pl.pallas_call(
        body, name="gdn_gate", grid=(T // tr,), in_specs=[pl.BlockSpec((tr, 128), lambda i: (i, 0)), vec, vec],
        out_specs=[hm, hm], out_shape=[jax.ShapeDtypeStruct((GDN_HEADS, T, 1), F32)] * 2,
        compiler_params=_params("parallel"),
    )(ba, alog_row, dt_row)


def _gdn_gate_bwd(ba, alog_row, dt_row, dbeta, dg, *, tr=512):
    T = ba.shape[0]

    def body(ba_ref, al_ref, dt_ref, db_ref, dg_ref, dba_ref, dal_ref, ddt_ref):
        i = pl.program_id(0)
        lane = _iota((1, 128), 1)
        d = jnp.zeros((tr, 128), F32)
        for h in range(GDN_HEADS):
            d = d + jnp.where(lane == h, db_ref[h], 0.0) + jnp.where(lane == GDN_HEADS + h, dg_ref[h], 0.0)
        _, vjp = jax.vjp(_gate_fn, ba_ref[...], al_ref[...], dt_ref[...])
        dba, dal, ddt = vjp(d)
        dba_ref[...] = dba.astype(BF16)

        @pl.when(i == 0)
        def _():
            dal_ref[...] = dal
            ddt_ref[...] = ddt

        @pl.when(i > 0)
        def _():
            dal_ref[...] += dal
            ddt_ref[...] += ddt

    vec = pl.BlockSpec((1, 128), lambda i: (0, 0))
    hm = pl.BlockSpec((GDN_HEADS, tr, 1), lambda i: (0, i, 0))
    row = pl.BlockSpec((tr, 128), lambda i: (i, 0))
    return pl.pallas_call(
        body, name="gdn_gate_bwd", grid=(T // tr,), in_specs=[row, vec, vec, hm, hm], out_specs=[row, vec, vec],
        out_shape=[jax.ShapeDtypeStruct((T, 128), BF16), jax.ShapeDtypeStruct((1, 128), F32),
                   jax.ShapeDtypeStruct((1, 128), F32)],
        compiler_params=_params("arbitrary"),
    )(ba, alog_row, dt_row, dbeta, dg)


def _split3(x):
    x1 = x.astype(BF16)
    r = x - x1.astype(F32)
    x2 = r.astype(BF16)
    return x1, x2, (r - x2.astype(F32)).astype(BF16)


def _dot01(tri, x, dims):
    t = tri.astype(BF16)
    x1, x2, x3 = _split3(x)
    d = lambda xi: lax.dot_general(t, xi, dims, preferred_element_type=F32)
    return d(x1) + (d(x2) + d(x3))


def _dot3(a, b, dims):
    ah, al, _ = _split3(a)
    bh, bl, _ = _split3(b)
    d = lambda p, q: lax.dot_general(p, q, dims, preferred_element_type=F32)
    return d(ah, bh) + (d(ah, bl) + d(al, bh))


BNN = (((2,), (1,)), ((0,), (0,)))
BNT = (((2,), (2,)), ((0,), (0,)))
BTN = (((1,), (1,)), ((0,), (0,)))


@jax.custom_vjp
def _mm01(tri, x):
    return _dot01(tri, x, BNN)


def _mm01_fwd(tri, x):
    return _dot01(tri, x, BNN), tri


def _mm01_bwd(tri, ct):
    return jnp.zeros_like(tri), _dot01(tri, ct, BTN)


_mm01.defvjp(_mm01_fwd, _mm01_bwd)


def _unit_lower_inverse(a):
    C = a.shape[-1]
    eye = (_iota(a.shape, 1) == _iota(a.shape, 2)).astype(F32)
    pw = -a
    tinv = eye + pw
    for _ in range(5):
        pw = _dot3(pw, pw, BNN)
        tinv = tinv + _dot3(tinv, pw, BNN)
    return tinv


@jax.custom_vjp
def _unit_lower_solve(a, rv, rw):
    return _unit_lower_solve_fwd(a, rv, rw)[0]


def _unit_lower_solve_fwd(a, rv, rw):
    tinv = _unit_lower_inverse(a)
    sol = _dot3(tinv, jnp.concatenate([rv, rw], axis=2), BNN)
    n = rv.shape[2]
    return (sol[:, :, :n], sol[:, :, n:]), (tinv, sol)


def _unit_lower_solve_bwd(res, cts):
    tinv, sol = res
    n = cts[0].shape[2]
    d_rhs = _dot3(tinv, jnp.concatenate(cts, axis=2), BTN)
    return -_dot3(d_rhs, sol, BNT), d_rhs[:, :, :n], d_rhs[:, :, n:]


_unit_lower_solve.defvjp(_unit_lower_solve_fwd, _unit_lower_solve_bwd)


@jax.custom_vjp
def _mmb_nt(a, b):
    return _dot(a, b, BNT)


def _mmb_nt_fwd(a, b):
    return _dot(a, b, BNT), (a, b)


def _mmb_nt_bwd(res, ct):
    a, b = res
    return _dot(ct, b, BNN), _dot(ct, a, BTN)


_mmb_nt.defvjp(_mmb_nt_fwd, _mmb_nt_bwd)


def _gdn_chunk(q, k, v, gcol, bcol):
    nb, C = q.shape[0], GDN_CHUNK
    row, col = _iota((nb, C, C), 1), _iota((nb, C, C), 2)
    incl, strict = row >= col, row > col
    eye = (row == col).astype(F32)
    lower = incl.astype(F32)
    ones = jnp.ones((nb, C, C), F32)
    gwide = jnp.broadcast_to(gcol, (nb, C, GDN_HEAD_DIM))
    gc = _mm01(lower, gwide)
    gtot = _mm01(ones, gwide)
    gc_c = _mm01(lower, jnp.broadcast_to(gcol, (nb, C, C)))
    gc_s = _mm01(ones, gc_c * eye)
    decay = jnp.where(incl, jnp.exp(jnp.where(incl, gc_c - gc_s, 0.0)), 0.0)
    kb = k * bcol
    a = jnp.where(strict, _mmb_nt(kb, k) * decay, 0.0)
    egc = jnp.exp(gc)
    u, w = _unit_lower_solve(a, v * bcol, kb * egc)
    qk = jnp.where(incl, _mmb_nt(q, k) * decay, 0.0)
    return u, w, qk, q * egc, k * jnp.exp(gtot - gc), jnp.exp(jnp.sum(gwide, axis=1))


GDN_ROWS = 8 * GDN_CHUNK


GDN_LOCAL_CHUNKS = 16


def _gdn_specs(T):
    nch = min(GDN_LOCAL_CHUNKS, T // GDN_CHUNK)
    L = nch * GDN_CHUNK
    hd = lambda off: pl.BlockSpec((1, L, 128), lambda h, i: (off + h, i, 0))
    col = pl.BlockSpec((1, L, 1), lambda h, i: (h, i, 0))
    sq = pl.BlockSpec((1, L, GDN_CHUNK), lambda h, i: (h, i, 0))
    gl = pl.BlockSpec((1, nch, 128), lambda h, i: (h, i, 0))
    return nch, hd, col, sq, gl


def _gdn_local(qkv, g, beta):
    T = qkv.shape[1]
    nch, hd, col, sq, gl_spec = _gdn_specs(T)

    def body(q_ref, k_ref, v_ref, g_ref, b_ref, u_ref, w_ref, qk_ref, qd_ref, kd_ref, gl_ref):
        chunks = lambda ref: ref[0].reshape(nch, GDN_CHUNK, ref.shape[2])
        rows = lambda val: val.reshape(nch * GDN_CHUNK, val.shape[2])
        u, w, qk, qd, kd, gl = _gdn_chunk(chunks(q_ref), chunks(k_ref), chunks(v_ref), chunks(g_ref), chunks(b_ref))
        u_ref[0] = rows(u)
        w_ref[0] = rows(w).astype(BF16)
        qk_ref[0] = rows(qk).astype(BF16)
        qd_ref[0] = rows(qd).astype(BF16)
        kd_ref[0] = rows(kd).astype(BF16)
        gl_ref[0] = gl

    H = GDN_HEADS
    return pl.pallas_call(
        body, name="gdn_local", grid=(H, T // (nch * GDN_CHUNK)),
        in_specs=[hd(0), hd(H), hd(2 * H), col, col],
        out_specs=[hd(0), hd(0), sq, hd(0), hd(0), gl_spec],
        out_shape=[jax.ShapeDtypeStruct((H, T, 128), F32), jax.ShapeDtypeStruct((H, T, 128), BF16),
                   jax.ShapeDtypeStruct((H, T, GDN_CHUNK), BF16), jax.ShapeDtypeStruct((H, T, 128), BF16),
                   jax.ShapeDtypeStruct((H, T, 128), BF16), jax.ShapeDtypeStruct((H, T // GDN_CHUNK, 128), F32)],
        compiler_params=_params("parallel", "parallel"),
    )(qkv, qkv, qkv, g, beta)


def _gdn_local_bwd(qkv, g, beta, du, dw, dqk, dqd, dkd, dgl):
    T = qkv.shape[1]
    nch, hd, col, sq, gl_spec = _gdn_specs(T)

    def body(q_ref, k_ref, v_ref, g_ref, b_ref, du_ref, dw_ref, dqk_ref, dqd_ref, dkd_ref, dgl_ref,
             dqkv_ref, dg_ref, db_ref):
        chunks = lambda ref: ref[0].reshape(nch, GDN_CHUNK, ref.shape[2])
        rows = lambda val: val.reshape(nch * GDN_CHUNK, val.shape[2])
        _, vjp = jax.vjp(_gdn_chunk, chunks(q_ref), chunks(k_ref), chunks(v_ref), chunks(g_ref), chunks(b_ref))
        dq, dk, dv, dg, db = vjp((chunks(du_ref), chunks(dw_ref), chunks(dqk_ref), chunks(dqd_ref), chunks(dkd_ref),
                                  dgl_ref[0]))
        dqkv_ref[0, 0] = rows(dq)
        dqkv_ref[1, 0] = rows(dk)
        dqkv_ref[2, 0] = rows(dv)
        dg_ref[0] = rows(dg)
        db_ref[0] = rows(db)

    H = GDN_HEADS
    small = jax.ShapeDtypeStruct((H, T, 1), F32)
    dqkv, dg, db = pl.pallas_call(
        body, name="gdn_local_bwd", grid=(H, T // (nch * GDN_CHUNK)),
        in_specs=[hd(0), hd(H), hd(2 * H), col, col, hd(0), hd(0), sq, hd(0), hd(0), gl_spec],
        out_specs=[pl.BlockSpec((3, 1, nch * GDN_CHUNK, 128), lambda h, i: (0, h, i, 0)), col, col],
        out_shape=[jax.ShapeDtypeStruct((3, H, T, 128), F32), small, small],
        compiler_params=_params("parallel", "parallel"),
    )(qkv, qkv, qkv, g, beta, du, dw, dqk, dqd, dkd, dgl)
    return dqkv.reshape(3 * H, T, 128), dg, db


GDN_HB = 4


def _gdn_scan_specs(T, rev):
    nb = T // GDN_ROWS
    blk = (lambda i: nb - 1 - i) if rev else (lambda i: i)
    hd = pl.BlockSpec((GDN_HB, GDN_ROWS, 128), lambda h, i: (h, blk(i), 0))
    sq = pl.BlockSpec((GDN_HB, GDN_ROWS, GDN_CHUNK), lambda h, i: (h, blk(i), 0))
    gl = pl.BlockSpec((GDN_HB, 8, 128), lambda h, i: (h, blk(i), 0))
    st = pl.BlockSpec((GDN_HB, 8, 128, 128), lambda h, i: (h, blk(i), 0, 0))
    return hd, sq, gl, st


def _gdn_scan(u, w, qk, qd, kd, gl):
    H, T, _ = u.shape
    hd, sq, gl_spec, st = _gdn_scan_specs(T, False)

    def body(u_ref, w_ref, qk_ref, qd_ref, kd_ref, gl_ref, o_ref, ss_ref, vn_ref, s_scr):
        @pl.when(pl.program_id(1) == 0)
        def _():
            s_scr[...] = jnp.zeros_like(s_scr)

        dot = lambda a, b, dims: lax.dot_general(a, b, dims, preferred_element_type=F32)
        s = s_scr[...]
        for c in range(8):
            rs = slice(GDN_CHUNK * c, GDN_CHUNK * (c + 1))
            ss_ref[:, c] = s
            sb = s.astype(BF16)
            vn = u_ref[:, rs, :] - dot(w_ref[:, rs, :], sb, BNN)
            vnb = vn.astype(BF16)
            o_ref[:, rs, :] = dot(qd_ref[:, rs, :], sb, BNN) + dot(qk_ref[:, rs, :], vnb, BNN)
            vn_ref[:, rs, :] = vnb
            s = s * gl_ref[:, c:c + 1, :] + dot(kd_ref[:, rs, :], vnb, BTN)
        s_scr[...] = s

    return pl.pallas_call(
        body, name="gdn_scan", grid=(H // GDN_HB, T // GDN_ROWS),
        in_specs=[hd, hd, sq, hd, hd, gl_spec], out_specs=[hd, st, hd],
        out_shape=[jax.ShapeDtypeStruct((H, T, 128), F32), jax.ShapeDtypeStruct((H, T // GDN_CHUNK, 128, 128), F32),
                   jax.ShapeDtypeStruct((H, T, 128), BF16)],
        scratch_shapes=[pltpu.VMEM((GDN_HB, 128, 128), F32)],
        compiler_params=_params("parallel", "arbitrary"),
    )(u, w, qk, qd, kd, gl)


def _gdn_scan_bwd(do, ss, vn, w, qk, qd, kd, gl):
    H, T, _ = do.shape
    hd, sq, gl_spec, st = _gdn_scan_specs(T, True)

    def body(do_ref, ss_ref, vn_ref, w_ref, qk_ref, qd_ref, kd_ref, gl_ref,
             du_ref, dw_ref, dqk_ref, dqd_ref, dkd_ref, dgl_ref, ds_scr):
        @pl.when(pl.program_id(1) == 0)
        def _():
            ds_scr[...] = jnp.zeros_like(ds_scr)

        dot = lambda a, b, dims: lax.dot_general(a, b, dims, preferred_element_type=F32)
        ds = ds_scr[...]
        for c in reversed(range(8)):
            rs = slice(GDN_CHUNK * c, GDN_CHUNK * (c + 1))
            s = ss_ref[:, c]
            sb, dsb = s.astype(BF16), ds.astype(BF16)
            dob = do_ref[:, rs, :].astype(BF16)
            vnb = vn_ref[:, rs, :]
            dvn = dot(qk_ref[:, rs, :], dob, BTN) + dot(kd_ref[:, rs, :], dsb, BNN)
            dvnb = dvn.astype(BF16)
            du_ref[:, rs, :] = dvn
            dw_ref[:, rs, :] = -dot(dvnb, sb, BNT)
            dqk_ref[:, rs, :] = dot(dob, vnb, BNT)
            dqd_ref[:, rs, :] = dot(dob, sb, BNT)
            dkd_ref[:, rs, :] = dot(vnb, dsb, BNT)
            dgl_ref[:, c:c + 1, :] = jnp.sum(ds * s, axis=1, keepdims=True)
            ds = dot(qd_ref[:, rs, :], dob, BTN) + ds * gl_ref[:, c:c + 1, :] - dot(w_ref[:, rs, :], dvnb, BTN)
        ds_scr[...] = ds

    big = jax.ShapeDtypeStruct((H, T, 128), F32)
    return pl.pallas_call(
        body, name="gdn_scan_bwd", grid=(H // GDN_HB, T // GDN_ROWS),
        in_specs=[hd, st, hd, hd, sq, hd, hd, gl_spec], out_specs=[hd, hd, sq, hd, hd, gl_spec],
        out_shape=[big, big, jax.ShapeDtypeStruct((H, T, GDN_CHUNK), F32), big, big,
                   jax.ShapeDtypeStruct((H, T // GDN_CHUNK, 128), F32)],
        scratch_shapes=[pltpu.VMEM((GDN_HB, 128, 128), F32)],
        compiler_params=_params("parallel", "arbitrary"),
    )(do, ss, vn, w, qk, qd, kd, gl)


def _gated_norm(o, z, nw):
    on = o * lax.rsqrt(jnp.mean(o * o, axis=-1, keepdims=True) + EPS) * nw
    return on * (z * _sigmoid(z))


def _gdn_post(o, proj, norm_w, *, tr=512):
    T = proj.shape[0]

    def body(o_ref, z_ref, n_ref, y_ref):
        y_ref[...] = _gated_norm(o_ref[0], z_ref[...], n_ref[...]).astype(BF16)

    return pl.pallas_call(
        body, name="gdn_post", grid=(T // tr, GDN_HEADS),
        in_specs=[pl.BlockSpec((1, tr, 128), lambda i, h: (h, i, 0)), pl.BlockSpec((tr, 128), lambda i, h: (i, 24 + h)),
                  pl.BlockSpec((1, 128), lambda i, h: (0, 0))],
        out_specs=pl.BlockSpec((tr, 128), lambda i, h: (i, h)),
        out_shape=jax.ShapeDtypeStruct((T, 1024), BF16), compiler_params=_params("parallel", "parallel"),
    )(o, proj, norm_w)


def _gdn_post_bwd(o, proj, norm_w, dy, *, tr=512):
    T = proj.shape[0]

    def body(o_ref, z_ref, n_ref, dy_ref, do_ref, dz_ref, dn_ref):
        first = (pl.program_id(0) == 0) & (pl.program_id(1) == 0)
        _, vjp = jax.vjp(_gated_norm, o_ref[0], z_ref[...], n_ref[...])
        do, dz, dn = vjp(dy_ref[...])
        do_ref[0] = do
        dz_ref[...] = dz.astype(BF16)

        @pl.when(first)
        def _():
            dn_ref[...] = dn

        @pl.when(jnp.logical_not(first))
        def _():
            dn_ref[...] += dn

    blk = pl.BlockSpec((tr, 128), lambda i, h: (i, h))
    hm = pl.BlockSpec((1, tr, 128), lambda i, h: (h, i, 0))
    vec = pl.BlockSpec((1, 128), lambda i, h: (0, 0))
    return pl.pallas_call(
        body, name="gdn_post_bwd", grid=(T // tr, GDN_HEADS),
        in_specs=[hm, pl.BlockSpec((tr, 128), lambda i, h: (i, 24 + h)), vec, blk], out_specs=[hm, blk, vec],
        out_shape=[jax.ShapeDtypeStruct((GDN_HEADS, T, 128), F32), jax.ShapeDtypeStruct((T, 1024), BF16),
                   jax.ShapeDtypeStruct((1, 128), F32)],
        compiler_params=_params("arbitrary", "arbitrary"),
    )(o, proj, norm_w, dy)


HBM_SPEC = pl.BlockSpec(memory_space=pltpu.HBM)


def _place():
    return lax.axis_index("x"), lax.axis_index("y"), lax.axis_index("c")


def _all_gather(vs, *, name):
    n = len(vs)

    def body(*refs):
        start, forward, finish = _gather_stages(refs[:n], refs[n:2 * n], *refs[2 * n:])
        start()
        forward()
        finish()

    return pl.pallas_call(
        body, name=name, out_shape=_gather_shapes(vs), in_specs=[HBM_SPEC] * n, out_specs=[HBM_SPEC] * n,
        scratch_shapes=_gather_sems(n),
    )(*vs)


def _gather_shapes(vs):
    return [jax.ShapeDtypeStruct((N_DEV,) + v.shape, v.dtype) for v in vs]


def _gather_sems(n):
    return [pltpu.SemaphoreType.DMA((7 * n,)), pltpu.SemaphoreType.DMA((7 * n,)), pltpu.SemaphoreType.DMA((n,))]


def _gather_stages(v_refs, out_refs, send_sems, recv_sems, local_sems):
    n = len(v_refs)
    x, y, c = _place()
    me, sibling = (x, y, c), (x, y, 1 - c)
    chips = [(1 - x, y), (x, 1 - y), (1 - x, 1 - y)]

    def copy(a, k, block, to, from_input=False):
        slot = out_refs[a].at[4 * block[0] + 2 * block[1] + block[2]]
        return pltpu.make_async_remote_copy(
            src_ref=v_refs[a] if from_input else slot, dst_ref=slot,
            send_sem=send_sems.at[7 * a + k], recv_sem=recv_sems.at[7 * a + k], device_id=to, device_id_type=MESH)

    def mine():
        return [pltpu.make_async_copy(v_refs[a], out_refs[a].at[4 * x + 2 * y + c], local_sems.at[a]) for a in range(n)]

    def first():
        return ([copy(a, 0, me, sibling, True) for a in range(n)]
                + [copy(a, 1 + j, me, (*chip, c), True) for j, chip in enumerate(chips) for a in range(n)])

    def passed():
        return [copy(a, 4 + j, (*chip, c), sibling) for j, chip in enumerate(chips) for a in range(n)]

    def start():
        for cp in mine() + first():
            cp.start()

    def forward():
        for j, chip in enumerate(chips):
            for a in range(n):
                copy(a, 1 + j, (*chip, c), me).wait_recv()
                copy(a, 4 + j, (*chip, c), sibling).start()

    def finish():
        for a in range(n):
            copy(a, 0, sibling, me).wait_recv()
            for j, chip in enumerate(chips):
                copy(a, 4 + j, (*chip, 1 - c), me).wait_recv()
        for cp in first() + passed():
            cp.wait_send()
        for cp in mine():
            cp.wait()

    return start, forward, finish


def _exchange_sibling(gs, *, name):
    n = len(gs)

    def body(*refs):
        g_refs, out_refs = refs[:n], refs[n:2 * n]
        send_sems, recv_sems = refs[2 * n:]
        x, y, c = _place()
        copies = [pltpu.make_async_remote_copy(
            src_ref=g_refs[a].at[k, 1 - c], dst_ref=out_refs[a].at[k], send_sem=send_sems.at[4 * a + k],
            recv_sem=recv_sems.at[4 * a + k], device_id=(x, y, 1 - c), device_id_type=MESH)
            for a in range(n) for k in range(4)]
        for cp in copies:
            cp.start()
        for cp in copies:
            cp.wait()

    return pl.pallas_call(
        body, name=name, out_shape=[jax.ShapeDtypeStruct((4,) + g.shape[2:], g.dtype) for g in gs],
        in_specs=[HBM_SPEC] * n, out_specs=[HBM_SPEC] * n,
        scratch_shapes=[pltpu.SemaphoreType.DMA((4 * n,)), pltpu.SemaphoreType.DMA((4 * n,))],
    )(*gs)


def _exchange_chips(pcs):
    n = len(pcs)

    def body(*refs):
        start, finish = _chips_stages(refs[:n], refs[n:2 * n], *refs[2 * n:])
        start()
        finish()

    return pl.pallas_call(
        body, name="rs_chips", out_shape=_chips_shapes(pcs), in_specs=[HBM_SPEC] * n, out_specs=[HBM_SPEC] * n,
        scratch_shapes=_chips_sems(n),
    )(*pcs)


def _chips_shapes(pcs):
    return [jax.ShapeDtypeStruct((3,) + pc.shape[1:], pc.dtype) for pc in pcs]


def _chips_sems(n):
    return [pltpu.SemaphoreType.DMA((3 * n,)), pltpu.SemaphoreType.DMA((3 * n,))]


def _chips_stages(p_refs, out_refs, send_sems, recv_sems):
    n = len(p_refs)
    x, y, c = _place()
    chips = [(1 - x, y), (x, 1 - y), (1 - x, 1 - y)]

    def copies():
        return [pltpu.make_async_remote_copy(
            src_ref=p_refs[a].at[2 * cx + cy], dst_ref=out_refs[a].at[j], send_sem=send_sems.at[3 * a + j],
            recv_sem=recv_sems.at[3 * a + j], device_id=(cx, cy, c), device_id_type=MESH)
            for j, (cx, cy) in enumerate(chips) for a in range(n)]

    def start():
        for cp in copies():
            cp.start()

    def finish():
        for cp in copies():
            cp.wait()

    return start, finish


def _chip_partial(place, g, got, *, tr, name):
    R, W = g.shape[2:]

    def body(pl_ref, g_ref, r_ref, o_ref):
        o_ref[...] = (g_ref[0] + r_ref[...]).astype(BF16)

    return pl.pallas_call(
        body, name=name, out_shape=jax.ShapeDtypeStruct((4, R, W), BF16),
        grid_spec=pltpu.PrefetchScalarGridSpec(
            num_scalar_prefetch=1, grid=(4, R // tr),
            in_specs=[pl.BlockSpec((1, 1, tr, W), lambda k, i, pr: (k, pr[2], i, 0)),
                      pl.BlockSpec((1, tr, W), lambda k, i, pr: (k, i, 0))],
            out_specs=pl.BlockSpec((1, tr, W), lambda k, i, pr: (k, i, 0))),
        compiler_params=_params("parallel", "parallel"),
    )(place, g, got)


def _adamw_math(g, w, m, v):
    m = ADAM_B1 * m + (1.0 - ADAM_B1) * g
    v = ADAM_B2 * v + (1.0 - ADAM_B2) * (g * g)
    m_hat = m / (1.0 - ADAM_B1 ** ADAM_STEP)
    v_hat = v / (1.0 - ADAM_B2 ** ADAM_STEP)
    return -ADAM_LR * (m_hat / (jnp.sqrt(v_hat) + ADAM_EPS) + ADAM_WD * w), m, v


def _adamw_shard(place, g, got1, got2, w, m, v, *, tr, name):
    R, W = w.shape

    def body(pl_ref, g_ref, r1_ref, r2_ref, w_ref, m_ref, v_ref, go_ref, d_ref, mo_ref, vo_ref):
        gs = g_ref[0, 0] + r1_ref[0]
        for j in range(3):
            gs = gs + r2_ref[j].astype(F32)
        go_ref[...] = gs
        d_ref[...], mo_ref[...], vo_ref[...] = _adamw_math(gs, w_ref[...], m_ref[...], v_ref[...])

    row = pl.BlockSpec((tr, W), lambda i, pr: (i, 0))
    out = jax.ShapeDtypeStruct((R, W), F32)
    return pl.pallas_call(
        body, name=name, out_shape=[out] * 4,
        grid_spec=pltpu.PrefetchScalarGridSpec(
            num_scalar_prefetch=1, grid=(R // tr,),
            in_specs=[pl.BlockSpec((1, 1, tr, W), lambda i, pr: (2 * pr[0] + pr[1], pr[2], i, 0)),
                      pl.BlockSpec((1, tr, W), lambda i, pr: (2 * pr[0] + pr[1], i, 0)),
                      pl.BlockSpec((3, tr, W), lambda i, pr: (0, i, 0)), row, row, row],
            out_specs=[row] * 4),
        compiler_params=_params("parallel"),
    )(place, g, got1, got2, w, m, v)


def _adamw_replicated(parts, w, m, v):
    R, W = w.shape

    def body(p_ref, w_ref, m_ref, v_ref, go_ref, d_ref, mo_ref, vo_ref):
        gs = p_ref[0]
        for j in range(1, N_DEV):
            gs = gs + p_ref[j]
        go_ref[...] = gs
        d_ref[...], mo_ref[...], vo_ref[...] = _adamw_math(gs, w_ref[...], m_ref[...], v_ref[...])

    full = pl.BlockSpec((R, W), lambda i: (0, 0))
    out = jax.ShapeDtypeStruct((R, W), F32)
    return pl.pallas_call(
        body, name="adamw_replicated", grid=(1,), out_shape=[out] * 4,
        in_specs=[pl.BlockSpec((N_DEV, R, W), lambda i: (0, 0, 0)), full, full, full], out_specs=[full] * 4,
        compiler_params=_params("arbitrary"),
    )(parts, w, m, v)


GROUPS = {
    "in_e": (256, 512, ((("w_in_e", None), 1024, 1024),)),
    "in_o": (514, 512, ((("w_in_o", None), 1024, 1024),)),
    "up0": (704, 256, ((("w_up", 0), 1024, 1024),)),
    "up1": (704, 256, ((("w_up", 1), 1024, 1024),)),
    "down0": (1024, 176, ((("w_down", 0), 352, 352),)),
    "down1": (1024, 176, ((("w_down", 1), 352, 352),)),
    "square": (1024, 256, ((("w_out_e", None), 128, 128), (("w_out_o", None), 128, 128), (("w_ple_gate", None), 256, 256))),
    "ple": (128, 512, ((("w_ple", None), 512, 512),)),
    "conv_f": (704, 8, ((("ffn_conv", None), 6, 8),)),
    "norm_o": (128, 8, ((("mix_norm_o", None), 1, 8),)),
    "conv_o": (384, 8, ((("conv_qkv_o", None), 4, 8),)),
}
SHARDED = tuple(dict.fromkeys(p[0][0] for g in GROUPS.values() for p in g[2]))
COLUMN_SHARDED = ("w_in_e", "w_in_o", "w_up", "ffn_conv", "w_ple", "conv_qkv_o", "mix_norm_o")
PACK_W = 1024
REPL_LAYOUT = (
    ("mix_norm_e", (1, 1024), 8), ("pool_w", (1, 4, 128, 128), 64), ("pool_scale", (1, 512), 8),
    ("a_log_o", (1, 8), 8), ("dt_bias_o", (1, 8), 8), ("gdn_norm_o", (1, 128), 8),
    ("ffn_norm", (2, 1024), 8), ("ple_norm", (2, 1024), 8), ("final_norm", (1024,), 8),
)


def _pad_rows(a, rows):
    extra = rows - a.shape[-2]
    return a if extra == 0 else jnp.pad(a, [(0, 0)] * (a.ndim - 2) + [(0, extra), (0, 0)])


def _group_rows(pieces, gname):
    parts = [_pad_rows(pieces[name], padded) for name, _, padded in GROUPS[gname][2]]
    return parts[0] if len(parts) == 1 else jnp.concatenate(parts, axis=-2)


def _ungroup_rows(buf, gname):
    out, r0 = {}, 0
    for name, rows, padded in GROUPS[gname][2]:
        out[name] = buf[..., r0:r0 + rows, :]
        r0 += padded
    return out


def _shard_major(name, gfull, n_layers):
    per_layer = []
    for g in gfull:
        if g.ndim == 3:
            per_layer.append(g)
        elif name in COLUMN_SHARDED:
            k = g.shape[0]
            per_layer.append(jnp.moveaxis(g.reshape(k, N_DEV, g.shape[1] // N_DEV), 1, 0))
        else:
            per_layer.append(g.reshape(N_DEV, g.shape[0] // N_DEV, -1))
    return per_layer[0] if n_layers == 1 else jnp.concatenate(per_layer, axis=1)


def _natural(name, gathered, n_layers):
    rows = gathered.shape[1] // n_layers
    out = []
    for layer in range(n_layers):
        piece = gathered[:, layer * rows:(layer + 1) * rows]
        if name in COLUMN_SHARDED:
            out.append(jnp.moveaxis(piece, 0, 1).reshape(rows, N_DEV * piece.shape[2]))
        else:
            out.append(piece.reshape(N_DEV * rows, piece.shape[2]))
    return out


def _unshard_pairs(g, layer, rows):
    n = g.shape[2]

    def body(g_ref, o_ref):
        o_ref[...] = jnp.concatenate([g_ref[0], g_ref[1]], axis=1)

    return pl.pallas_call(
        body, name="unshard_pairs%d" % layer, grid=(N_DEV // 2,),
        in_specs=[pl.BlockSpec((2, rows, n), lambda p: (p, layer, 0))],
        out_specs=pl.BlockSpec((rows, 2 * n), lambda p: (0, p)),
        out_shape=jax.ShapeDtypeStruct((rows, N_DEV * n), g.dtype), compiler_params=_params("parallel"),
    )(g)


def _unshard_in_o(g, *, tr=256):
    rows, n = g.shape[1:]
    main = (N_DEV * n) // 128 * 128

    def body(g_ref, m_ref, b_ref):
        cat = jnp.concatenate([g_ref[i] for i in range(N_DEV)], axis=1)
        m_ref[...] = cat[:, :main]
        b_ref[...] = jnp.concatenate([cat[:, main:], jnp.zeros((tr, 128 - (N_DEV * n - main)), g.dtype)], axis=1)

    return pl.pallas_call(
        body, name="unshard_in_o", grid=(rows // tr,),
        in_specs=[pl.BlockSpec((N_DEV, tr, n), lambda p: (0, p, 0))],
        out_specs=[pl.BlockSpec((tr, main), lambda p: (p, 0)), pl.BlockSpec((tr, 128), lambda p: (p, 0))],
        out_shape=[jax.ShapeDtypeStruct((rows, main), g.dtype), jax.ShapeDtypeStruct((rows, 128), g.dtype)],
        compiler_params=_params("parallel"),
    )(g)


def _reshard_in_o(d_main, d_ba, n, *, tr=256):
    rows, main = d_main.shape
    tail = N_DEV * n - main

    def body(m_ref, b_ref, o_ref):
        cat = jnp.concatenate([m_ref[...], b_ref[:, :tail]], axis=1)
        for i in range(N_DEV):
            o_ref[i] = cat[:, n * i:n * (i + 1)]

    return pl.pallas_call(
        body, name="reshard_in_o", grid=(rows // tr,),
        in_specs=[pl.BlockSpec((tr, main), lambda p: (p, 0)), pl.BlockSpec((tr, 128), lambda p: (p, 0))],
        out_specs=pl.BlockSpec((N_DEV, tr, n), lambda p: (0, p, 0)),
        out_shape=jax.ShapeDtypeStruct((N_DEV, rows, n), F32), compiler_params=_params("parallel"),
    )(d_main, d_ba)


def _rows(a, rows):
    flat = a.reshape(-1)
    return jnp.pad(flat, (0, rows * PACK_W - flat.shape[0])).reshape(rows, PACK_W)


def _pack_repl(vals):
    return jnp.concatenate([_rows(vals[name].reshape(shape), rows) for name, shape, rows in REPL_LAYOUT], axis=0)


def _unpack_repl(buf):
    out, r0 = {}, 0
    for name, shape, rows in REPL_LAYOUT:
        n = 1
        for s in shape:
            n *= s
        out[name] = buf[r0:r0 + rows].reshape(-1)[:n].reshape(shape)
        r0 += rows
    return out


WEIGHTS = ("mix_norm_e", "w_in_e", "pool_w", "pool_scale", "w_out_e", "mix_norm_o", "w_in_o", "conv_qkv_o", "a_log_o",
           "dt_bias_o", "gdn_norm_o", "w_out_o", "ffn_norm", "w_up", "ffn_conv", "w_down", "ple_norm", "w_ple_gate",
           "w_ple", "final_norm")


def _ffn_forward(x, hn, w_up, conv_w, w_down, next_gain, tag):
    up = _mm(hn, w_up, name="ffn_up" + tag)
    act = _ffn_act(up, conv_w)
    out, out_n = _mm(act, w_down, res=x, norm_gain=next_gain, name="ffn_down" + tag)
    return out, out_n, (x, hn, up, act)


def _ffn_backward(dx, saved, norm_g, w_up, conv_w, w_down, tag):
    x, hn, up, act = saved
    dact = _mm(dx, w_down, tb=True, name="ffn_dact" + tag)
    d_w_down = _mm(act, dx, ta=True, name="ffn_dwdown" + tag)
    dgate, dval, dcg, dcv = _ffn_act_bwd(up, conv_w, dact)
    dhn = _mm(dgate, w_up, tb=True, name="ffn_dhn_g" + tag)
    dx_in, d_norm = _mm(dval, w_up, tb=True, k_block0=FFN_DIM // _mm_tile(FFN_DIM), res=dhn,
                        rms_bwd=(x, norm_g, dx), name="ffn_dhn_v" + tag)
    shard = w_up.shape[1] // N_DEV
    d_w_up = jnp.concatenate([_mm(hn, d, ta=True, shard_cols=shard, name="ffn_dwup_" + side + tag)
                              for side, d in (("g", dgate), ("v", dval))], axis=0)
    return dx_in, d_norm, d_w_up, jnp.concatenate([dcg, dcv], axis=1), d_w_down


def _ple_forward(x, hn, w_gate, p, w_ple, tag):
    out, gl, pe = _ple_fwd(hn, w_gate, p, w_ple, x, name="ple_fwd" + tag)
    return out, (x, hn, gl, pe)


def _ple_backward(dx, saved, norm_g, w_gate, p, tag):
    x, hn, gl, pe = saved
    dpe, dgl = _ple_bwd(dx, gl, pe, name="ple_bwd" + tag)
    d_w_ple = _mm(p, dpe, ta=True, name="ple_dwple" + tag)
    d_w_gate = _mm(hn, dgl, ta=True, name="ple_dwgate" + tag)
    dx, d_norm = _mm(dgl, w_gate, tb=True, rms_bwd=(x, norm_g, dx), name="ple_dhn" + tag)
    return dx, d_norm, d_w_gate, d_w_ple


def kernel(x, p, mix_norm_e, w_in_e, pool_w, pool_scale, w_out_e, mix_norm_o, w_in_o, conv_qkv_o, a_log_o, dt_bias_o, gdn_norm_o, w_out_o, ffn_norm, w_up, ffn_conv, w_down, ple_norm, w_ple_gate, w_ple, final_norm, loss_target, m_mix_norm_e, m_w_in_e, m_pool_w, m_pool_scale, m_w_out_e, m_mix_norm_o, m_w_in_o, m_conv_qkv_o, m_a_log_o, m_dt_bias_o, m_gdn_norm_o, m_w_out_o, m_ffn_norm, m_w_up, m_ffn_conv, m_w_down, m_ple_norm, m_w_ple_gate, m_w_ple, m_final_norm, v_mix_norm_e, v_w_in_e, v_pool_w, v_pool_scale, v_w_out_e, v_mix_norm_o, v_w_in_o, v_conv_qkv_o, v_a_log_o, v_dt_bias_o, v_gdn_norm_o, v_w_out_o, v_ffn_norm, v_w_up, v_ffn_conv, v_w_down, v_ple_norm, v_w_ple_gate, v_w_ple, v_final_norm):
    given = dict(locals())
    place = jnp.stack(_place()).astype(jnp.int32)
    x0, tgt = x[0], loss_target[0]

    def pieces(prefix):
        out = {}
        for width, _, members in GROUPS.values():
            for (name, layer), rows, _ in members:
                a = given[prefix + name]
                out[(name, layer)] = (a if layer is None else a[layer]).reshape(rows, width)
        return out

    def flat2d(name):
        return given[name].reshape(-1, given[name].shape[-1])

    small = ("ffn_conv", "mix_norm_o", "conv_qkv_o")
    got = _all_gather([flat2d("w_in_e").astype(BF16)] + [_pad_rows(flat2d(k), 8) for k in small], name="ag_first")
    full = {("w_in_e", 0): _natural("w_in_e", got[0], 1)[0]}
    for i in range(2):
        full[("ffn_conv", i)] = _natural("ffn_conv", got[1][:, 3 * i:3 * i + 3], 1)[0]
    mix_norm_o_full = got[2][:, 0].reshape(1, D_MODEL)
    conv_qkv = _natural("conv_qkv_o", got[3][:, :4], 1)[0]
    alog_row = jnp.pad(a_log_o, ((0, 0), (8, 112)))
    dt_row = jnp.pad(dt_bias_o, ((0, 0), (8, 112)))
    lw = lambda name, i: full[(name, i)]

    h_e = _rms_fwd(x0, mix_norm_e, name="rms_mix_e")
    proj_e = _mm(h_e, lw("w_in_e", 0), name="in_e")
    pool_o = _pool_fwd(proj_e, pool_w[0], pool_scale)
    wide = ("w_out_e", "w_out_o", "w_down", "w_ple_gate")
    send = [jnp.concatenate([flat2d(k).astype(BF16) for k in wide], axis=0)]
    att_o, lsum, got = _sb_fwd(proj_e, gather=send + [flat2d(k).astype(BF16) for k in ("w_in_o", "w_up", "w_ple")])
    gathered, r0 = {"w_ple": got[3]}, 0
    for k in wide:
        gathered[k] = got[0][:, r0:r0 + flat2d(k).shape[0]]
        r0 += flat2d(k).shape[0]
    layers = {name: given[name].shape[0] if given[name].ndim == 3 else 1 for name in gathered}
    full.update({(name, i): w for name in gathered for i, w in enumerate(_natural(name, gathered[name], layers[name]))})
    w_in_o_main, w_in_o_ba = _unshard_in_o(got[1])
    for i in range(2):
        full[("w_up", i)] = _unshard_pairs(got[2], i, D_MODEL)
    mix_e = jnp.concatenate([pool_o, att_o.astype(BF16)], axis=1)
    x1, hf0 = _mm(mix_e, lw("w_out_e", 0), res=x0, norm_gain=ffn_norm[0:1], name="out_e")
    x2, hp0, ffn0 = _ffn_forward(x1, hf0, lw("w_up", 0), lw("ffn_conv", 0), lw("w_down", 0), ple_norm[0:1], "0")
    x3, ple0 = _ple_forward(x2, hp0, lw("w_ple_gate", 0), p[0, 0], lw("w_ple", 0), "0")

    h_o = _rms_fwd(x3, mix_norm_o_full, name="rms_mix_o")
    proj_o = _mm(h_o, w_in_o_main, name="in_o")
    ba = _mm(h_o, w_in_o_ba, name="in_o_ba")
    qkv = _gdn_pre(proj_o, conv_qkv)
    beta, g = _gdn_gate(ba, alog_row, dt_row)
    u, w_c, qk, qd, kd, gl = _gdn_local(qkv, g, beta)
    o, states, vnew = _gdn_scan(u, w_c, qk, qd, kd, gl)
    y_o = _gdn_post(o, proj_o, gdn_norm_o)
    x4, hf1 = _mm(y_o, lw("w_out_o", 0), res=x3, norm_gain=ffn_norm[1:2], name="out_o")
    x5, hp1, ffn1 = _ffn_forward(x4, hf1, lw("w_up", 1), lw("ffn_conv", 1), lw("w_down", 1), ple_norm[1:2], "1")
    x6, ple1 = _ple_forward(x5, hp1, lw("w_ple_gate", 1), p[1, 0], lw("w_ple", 1), "1")
    loss_row, dx, d_final = _final_loss(x6, final_norm.reshape(1, D_MODEL), tgt)

    grads, rgrads = {}, {}
    dx, d_ple1, grads[("w_ple_gate", 1)], grads[("w_ple", 1)] = _ple_backward(dx, ple1, ple_norm[1:2], lw("w_ple_gate", 1), p[1, 0], "1")
    dx, d_ffn1, grads[("w_up", 1)], grads[("ffn_conv", 1)], grads[("w_down", 1)] = _ffn_backward(
        dx, ffn1, ffn_norm[1:2], lw("w_up", 1), lw("ffn_conv", 1), lw("w_down", 1), "1")
    grads[("w_out_o", 0)] = _mm(y_o, dx, ta=True, name="dw_out_o")
    dy_o = _mm(dx, lw("w_out_o", 0), tb=True, name="dy_o")
    do, dz, rgrads["gdn_norm_o"] = _gdn_post_bwd(o, proj_o, gdn_norm_o, dy_o)
    du, dw_c, dqk, dqd, dkd, dgl = _gdn_scan_bwd(do, states, vnew, w_c, qk, qd, kd, gl)
    dqkv_heads, dg, dbeta = _gdn_local_bwd(qkv, g, beta, du, dw_c, dqk, dqd, dkd, dgl)
    dqkv, grads[("conv_qkv_o", 0)] = _gdn_pre_bwd(proj_o, conv_qkv, dqkv_heads)
    dba, d_alog, d_dt = _gdn_gate_bwd(ba, alog_row, dt_row, dbeta, dg)
    rgrads["a_log_o"], rgrads["dt_bias_o"] = d_alog[:, 8:16], d_dt[:, 8:16]
    dproj_o = jnp.concatenate([dqkv, dz], axis=1)
    dh = _mm(dproj_o, w_in_o_main, tb=True, name="dh_o")
    dx_o, d_mix_o = _mm(dba, w_in_o_ba, tb=True, res=dh, rms_bwd=(x3, mix_norm_o_full, dx), name="dh_o_ba")
    grads[("w_in_o", 0)] = _reshard_in_o(_mm(h_o, dproj_o, ta=True, name="dw_in_o"),
                                         _mm(h_o, dba, ta=True, name="dw_in_o_ba"), w_in_o.shape[2])
    dx = dx_o
    grads[("mix_norm_o", 0)] = d_mix_o

    dx, d_ple0, grads[("w_ple_gate", 0)], grads[("w_ple", 0)] = _ple_backward(dx, ple0, ple_norm[0:1], lw("w_ple_gate", 0), p[0, 0], "0")
    dx, d_ffn0, grads[("w_up", 0)], grads[("ffn_conv", 0)], grads[("w_down", 0)] = _ffn_backward(
        dx, ffn0, ffn_norm[0:1], lw("w_up", 0), lw("ffn_conv", 0), lw("w_down", 0), "0")
    grads[("w_out_e", 0)] = _mm(mix_e, dx, ta=True, name="dw_out_e")
    dmix = _mm(dx, lw("w_out_e", 0), tb=True, name="dmix_e")
    du_e, d_pool_w, rgrads["pool_scale"] = _pool_bwd(proj_e, dmix, pool_w[0], pool_scale)
    rgrads["pool_w"] = d_pool_w[None]

    def reduce_start(gnames, tag):
        smaj = {}
        for g in gnames:
            for (name, layer), _, _ in GROUPS[g][2]:
                of = [grads[(name, i)] for i in ((0, 1) if layer is None else (layer,)) if (name, i) in grads]
                smaj[(name, layer)] = _shard_major(name, of, len(of))
        gbuf = [_group_rows(smaj, g) for g in gnames]
        gbuf = [b.reshape((4, 2) + b.shape[1:]) for b in gbuf]
        got1 = _exchange_sibling(gbuf, name="rs_sibling" + tag)
        part = [_chip_partial(place, b, r, tr=GROUPS[g][1], name="rs_chip_partial_" + g)
                for g, b, r in zip(gnames, gbuf, got1)]
        return gbuf, got1, part

    early = tuple(g for g in GROUPS if g != "in_e")
    gbuf_e, got1_e, part_e = reduce_start(early, "_early")
    dq_e, dk_e, dv_e, got2_e = _sb_bwd(proj_e, lsum, dmix, exchange=part_e)
    dproj_e = jnp.concatenate([du_e, dq_e.astype(BF16), dk_e.astype(BF16), dv_e.astype(BF16)], axis=1)
    grads[("w_in_e", 0)] = _mm(h_e, dproj_e, ta=True, name="dw_in_e")
    dx, rgrads["mix_norm_e"] = _mm(dproj_e, lw("w_in_e", 0), tb=True, rms_bwd=(x0, mix_norm_e, dx), name="dh_e")
    rgrads["ffn_norm"] = jnp.concatenate([d_ffn0, d_ffn1], axis=0)
    rgrads["ple_norm"] = jnp.concatenate([d_ple0, d_ple1], axis=0)
    rgrads["final_norm"] = d_final.reshape(D_MODEL)

    gbuf_l, got1_l, part_l = reduce_start(("in_e",), "_late")
    got2_l = _exchange_chips(part_l)
    wloc, mloc, vloc = pieces(""), pieces("m_"), pieces("v_")
    sh_out = [{}, {}, {}, {}]
    for g, b, r1, r2 in zip(early + ("in_e",), gbuf_e + gbuf_l, list(got1_e) + list(got1_l),
                            list(got2_e) + list(got2_l)):
        res = _adamw_shard(place, b, r1, r2, _group_rows(wloc, g), _group_rows(mloc, g), _group_rows(vloc, g),
                           tr=GROUPS[g][1], name="adamw_" + g)
        for kind in range(4):
            sh_out[kind].update(_ungroup_rows(res[kind], g))

    (rparts,) = _all_gather([_pack_repl(rgrads)], name="ag_repl_grads")
    rp_out = _adamw_replicated(rparts, _pack_repl({n: given[n] for n, _, _ in REPL_LAYOUT}),
                               _pack_repl({n: given["m_" + n] for n, _, _ in REPL_LAYOUT}),
                               _pack_repl({n: given["v_" + n] for n, _, _ in REPL_LAYOUT}))
    rp_out = [_unpack_repl(b) for b in rp_out]

    def leaf(kind, name):
        if name in SHARDED:
            mine = sh_out[kind]
            whole = mine[(name, None)] if (name, None) in mine else jnp.stack([mine[(name, 0)], mine[(name, 1)]])
            return whole.reshape(given[name].shape)
        return rp_out[kind][name]

    loss = lax.psum(loss_row[0, 0], ("x", "y", "c"))
    outs = [loss, dx[None]]
    for kind in range(4):
        outs += [leaf(kind, n) for n in WEIGHTS]
    return tuple(outs)
```

```python
import jax
import jax.numpy as jnp
from jax import lax
from jax.experimental import pallas as pl
from jax.experimental.pallas import tpu as pltpu

F32 = jnp.float32
BF16 = jnp.bfloat16

D_MODEL = 1024
PLE_DIM = 256
POOL_WINDOWS = (2, 4, 8, 16)
SB_HEAD_DIM = 64
SB_BLOCK = 1024
SB_KBLOCK = 256
GDN_HEADS = 8
GDN_HEAD_DIM = 128
GDN_CONV = 4
GDN_CHUNK = 64
FFN_DIM = 2816
FFN_CONV = 3
EPS = 1e-6
ADAM_LR, ADAM_B1, ADAM_B2, ADAM_EPS, ADAM_WD, ADAM_STEP = 0.001, 0.9, 0.999, 1e-08, 0.01, 10
N_DEV = 8
MESH = pl.DeviceIdType.MESH
VMEM_LIMIT = 56 * 1024 * 1024

NN = (((1,), (0,)), ((), ()))
NT = (((1,), (1,)), ((), ()))
TN = (((0,), (0,)), ((), ()))


def _params(*sem):
    return pltpu.CompilerParams(dimension_semantics=sem if sem else None, vmem_limit_bytes=VMEM_LIMIT)


def _dot(a, b, dims):
    return lax.dot_general(a.astype(BF16), b.astype(BF16), dims, preferred_element_type=F32)


def _iota(shape, axis):
    return lax.broadcasted_iota(jnp.int32, shape, axis)


MM_TILE, MM_TILE_11 = 1024, 1408


def _mm_tile(dim):
    if dim <= MM_TILE_11:
        return dim
    return MM_TILE if dim % MM_TILE == 0 else MM_TILE_11


def _mm(a, b, *, ta=False, tb=False, res=None, norm_gain=None, rms_bwd=None, k_block0=0, shard_cols=None,
        shard_into=None, name):
    M, K = (a.shape[1], a.shape[0]) if ta else a.shape
    N = b.shape[0] if tb else b.shape[1]
    tm, tn, tk = _mm_tile(M), _mm_tile(N), _mm_tile(K)
    if rms_bwd is not None:
        tm = min(tm, 512)
    assert M % tm == 0 and N % tn == 0 and K % tk == 0, (name, M, N, K, tm, tn, tk)
    assert (norm_gain is None and rms_bwd is None) or tn == N, name
    nk = K // tk
    dims = (((0 if ta else 1,), (1 if tb else 0,)), ((), ()))
    extra = [res] if res is not None else []
    vecs = [norm_gain] if norm_gain is not None else []
    if rms_bwd is not None:
        extra += [rms_bwd[0], rms_bwd[2]]
        vecs = [rms_bwd[1]]
    n_out = 1 if (norm_gain is None and rms_bwd is None) else 2

    def body(*refs):
        a_ref, b_ref = refs[:2]
        tiles = list(refs[2:2 + len(extra)])
        vec_refs = refs[2 + len(extra):2 + len(extra) + len(vecs)]
        n_in = 2 + len(extra) + len(vecs) + len(held)
        outs = refs[n_in:n_in + n_out]
        scr = refs[n_in + n_out:]
        p = _dot(a_ref[...], b_ref[...], dims)

        def fin(acc):
            if res is not None:
                acc = acc + tiles[0][...]
            if rms_bwd is not None:
                x_ref, dres_ref = tiles[-2:]
                xv = x_ref[...]
                r = lax.rsqrt(jnp.mean(xv * xv, axis=-1, keepdims=True) + EPS)
                xn = xv * r
                dgp = jnp.sum(acc * xn, axis=0, keepdims=True)
                dyg = acc * vec_refs[0][...]
                outs[0][...] = dres_ref[...] + r * (dyg - xn * jnp.mean(dyg * xn, axis=-1, keepdims=True))
                first = pl.program_id(0) == 0

                @pl.when(first)
                def _():
                    outs[1][...] = dgp

                @pl.when(jnp.logical_not(first))
                def _():
                    outs[1][...] += dgp
                return
            if shard_cols is not None:
                for s in range(tn // shard_cols):
                    outs[0][s] = acc[:, shard_cols * s:shard_cols * (s + 1)]
                return
            outs[0][...] = acc
            if norm_gain is not None:
                r = lax.rsqrt(jnp.mean(acc * acc, axis=-1, keepdims=True) + EPS)
                outs[1][...] = (acc * r * vec_refs[0][...]).astype(BF16)

        if nk == 1:
            fin(p)
        else:
            acc_ref = scr[0]
            k = pl.program_id(2)

            @pl.when(k == 0)
            def _():
                acc_ref[...] = p

            @pl.when(k > 0)
            def _():
                acc_ref[...] += p

            @pl.when(k == nk - 1)
            def _():
                fin(acc_ref[...])

    a_spec = pl.BlockSpec((tk, tm), lambda i, j, k: (k, i)) if ta else pl.BlockSpec((tm, tk), lambda i, j, k: (i, k))
    b_spec = (pl.BlockSpec((tn, tk), lambda i, j, k: (j, k + k_block0)) if tb
              else pl.BlockSpec((tk, tn), lambda i, j, k: (k, j)))
    o_spec = pl.BlockSpec((tm, tn), lambda i, j, k: (i, j))
    v_spec = pl.BlockSpec((1, tn), lambda i, j, k: (0, j))
    out_specs, out_shape = [o_spec], [jax.ShapeDtypeStruct((M, N), F32)]
    held = []
    if shard_cols is not None:
        per = tn // shard_cols
        total, first, buf = shard_into if shard_into is not None else (N // shard_cols, 0, None)
        out_specs = [pl.BlockSpec((per, tm, shard_cols), lambda i, j, k: (j + first // per, i, 0))]
        out_shape = [jax.ShapeDtypeStruct((total, M, shard_cols), F32)]
        held = [buf] if buf is not None else []
    if norm_gain is not None:
        out_specs, out_shape = out_specs + [o_spec], out_shape + [jax.ShapeDtypeStruct((M, N), BF16)]
    if rms_bwd is not None:
        out_specs, out_shape = out_specs + [v_spec], out_shape + [jax.ShapeDtypeStruct((1, N), F32)]
    out = pl.pallas_call(
        body, name=name, grid=(M // tm, N // tn, nk),
        in_specs=([a_spec, b_spec] + [o_spec] * len(extra) + [v_spec] * len(vecs)
                  + [pl.BlockSpec(memory_space=pl.ANY)] * len(held)),
        out_specs=out_specs, out_shape=out_shape, scratch_shapes=[pltpu.VMEM((tm, tn), F32)] if nk > 1 else [],
        input_output_aliases={2 + len(extra) + len(vecs): 0} if held else {},
        compiler_params=_params("arbitrary" if rms_bwd is not None else "parallel", "parallel", "arbitrary"),
    )(a, b, *extra, *vecs, *held)
    return out[0] if n_out == 1 else out


def _rms_fwd(x, gain, *, name, tr=512):
    T, Dm = x.shape

    def body(x_ref, g_ref, o_ref):
        xv = x_ref[...]
        r = lax.rsqrt(jnp.mean(xv * xv, axis=-1, keepdims=True) + EPS)
        o_ref[...] = (xv * r * g_ref[...]).astype(BF16)

    return pl.pallas_call(
        body, name=name, grid=(T // tr,),
        in_specs=[pl.BlockSpec((tr, Dm), lambda i: (i, 0)), pl.BlockSpec((1, Dm), lambda i: (0, 0))],
        out_specs=pl.BlockSpec((tr, Dm), lambda i: (i, 0)),
        out_shape=jax.ShapeDtypeStruct((T, Dm), BF16), compiler_params=_params("parallel"),
    )(x, gain)


def _final_loss(x, gain, target, *, tr=512):
    T, Dm = x.shape

    def body(x_ref, g_ref, t_ref, loss_ref, dx_ref, dg_ref):
        i = pl.program_id(0)
        xv = x_ref[...]
        g = g_ref[...]
        r = lax.rsqrt(jnp.mean(xv * xv, axis=-1, keepdims=True) + EPS)
        xn = xv * r
        err = xn * g - t_ref[...]
        lp = jnp.zeros((1, 128), F32) + 0.5 * jnp.sum(jnp.mean(err * err, axis=-1, keepdims=True))
        dy_v = err * (1.0 / Dm)
        dgp = jnp.sum(dy_v * xn, axis=0, keepdims=True)
        dyg = dy_v * g
        dx_ref[...] = r * (dyg - xn * jnp.mean(dyg * xn, axis=-1, keepdims=True))

        @pl.when(i == 0)
        def _():
            dg_ref[...] = dgp
            loss_ref[...] = lp

        @pl.when(i > 0)
        def _():
            dg_ref[...] += dgp
            loss_ref[...] += lp

    row = pl.BlockSpec((tr, Dm), lambda i: (i, 0))
    vec = pl.BlockSpec((1, Dm), lambda i: (0, 0))
    return pl.pallas_call(
        body, name="final_loss", grid=(T // tr,), in_specs=[row, vec, row],
        out_specs=[pl.BlockSpec((1, 128), lambda i: (0, 0)), row, vec],
        out_shape=[jax.ShapeDtypeStruct((1, 128), F32), jax.ShapeDtypeStruct((T, Dm), F32),
                   jax.ShapeDtypeStruct((1, Dm), F32)],
        compiler_params=_params("arbitrary"),
    )(x, gain, target)


def _prev_spec(tr, cb, pad, col):
    return pl.BlockSpec((pad, cb), lambda *g: (jnp.maximum(g[0] * (tr // pad) - 1, 0), col(*g)))


def _next_spec(tr, cb, pad, col, T):
    return pl.BlockSpec((pad, cb), lambda *g: (jnp.minimum((g[0] + 1) * (tr // pad), T // pad - 1), col(*g)))


def _conv_rows(x_ext, w_ref, K, pad, cs=slice(None)):
    y = w_ref[K - 1:K, cs] * x_ext
    for i in range(K - 1):
        y = y + w_ref[i:i + 1, cs] * pltpu.roll(x_ext, K - 1 - i, 0)
    return y[pad:]


def _pool_y(u_ext, g, i, tr):
    s = u_ext
    for sh in (1, 2, 4, 8)[:g + 1]:
        s = s + pltpu.roll(s, sh, 0)
    t = i * tr + _iota((tr, 128), 0)
    cnt = jnp.minimum(t + 1, POOL_WINDOWS[g]).astype(F32)
    return s[16:] / cnt - u_ext[16:]


def _pool_fwd(proj, pool_w, pool_scale, *, tr=512):
    T = proj.shape[0]

    def body(u_ref, uh_ref, w_ref, s_ref, o_ref):
        i = pl.program_id(0)
        uh = jnp.where(i > 0, uh_ref[...], 0.0)
        for g in range(4):
            cs = slice(128 * g, 128 * (g + 1))
            y = _pool_y(jnp.concatenate([uh[:, cs], u_ref[:, cs]], axis=0), g, i, tr)
            o_ref[:, cs] = (_dot(y, w_ref[g], NN) * s_ref[:, cs]).astype(BF16)

    return pl.pallas_call(
        body, name="pool_fwd", grid=(T // tr,),
        in_specs=[pl.BlockSpec((tr, 512), lambda i: (i, 0)), _prev_spec(tr, 512, 16, lambda i: 0),
                  pl.BlockSpec((4, 128, 128), lambda i: (0, 0, 0)), pl.BlockSpec((1, 512), lambda i: (0, 0))],
        out_specs=pl.BlockSpec((tr, 512), lambda i: (i, 0)),
        out_shape=jax.ShapeDtypeStruct((T, 512), BF16), compiler_params=_params("parallel"),
    )(proj, proj, pool_w, pool_scale)


def _pool_bwd(proj, dout, pool_w, pool_scale, *, tr=512):
    T = proj.shape[0]
    nb = T // tr

    def body(u_ref, uh_ref, d_ref, dn_ref, w_ref, s_ref, du_ref, dw_ref, ds_ref):
        i = pl.program_id(0)
        uh = jnp.where(i > 0, uh_ref[...], 0.0)
        dn = jnp.where(i < nb - 1, dn_ref[...], 0.0)
        t_ext = i * tr + _iota((tr + 16, 128), 0)
        for g in range(4):
            cs = slice(128 * g, 128 * (g + 1))
            sc = s_ref[:, cs]
            wg = w_ref[g]
            y = _pool_y(jnp.concatenate([uh[:, cs], u_ref[:, cs]], axis=0), g, i, tr)
            dg = d_ref[:, cs]
            dsp = jnp.sum(dg * _dot(y, wg, NN), axis=0, keepdims=True)
            dyw = dg * sc
            dwp = _dot(y, dyw, TN)
            dy_ext = _dot(jnp.concatenate([dyw, dn[:, cs] * sc], axis=0), wg, NT)
            cnt = jnp.minimum(t_ext + 1, POOL_WINDOWS[g]).astype(F32)
            s = dy_ext / cnt
            for sh in (1, 2, 4, 8)[:g + 1]:
                s = s + pltpu.roll(s, tr + 16 - sh, 0)
            du_ref[:, cs] = (s[:tr] - dy_ext[:tr]).astype(BF16)

            @pl.when(i == 0)
            def _():
                dw_ref[g] = dwp
                ds_ref[:, cs] = dsp

            @pl.when(i > 0)
            def _():
                dw_ref[g] += dwp
                ds_ref[:, cs] += dsp

    row = pl.BlockSpec((tr, 512), lambda i: (i, 0))
    return pl.pallas_call(
        body, name="pool_bwd", grid=(nb,),
        in_specs=[row, _prev_spec(tr, 512, 16, lambda i: 0), row, _next_spec(tr, 512, 16, lambda i: 0, T),
                  pl.BlockSpec((4, 128, 128), lambda i: (0, 0, 0)), pl.BlockSpec((1, 512), lambda i: (0, 0))],
        out_specs=[row, pl.BlockSpec((4, 128, 128), lambda i: (0, 0, 0)), pl.BlockSpec((1, 512), lambda i: (0, 0))],
        out_shape=[jax.ShapeDtypeStruct((T, 512), BF16), jax.ShapeDtypeStruct((4, 128, 128), F32),
                   jax.ShapeDtypeStruct((1, 512), F32)],
        compiler_params=_params("arbitrary"),
    )(proj, proj, dout, dout, pool_w, pool_scale)


def _split_dot(x, tri):
    hi = x.astype(BF16)
    lo = (x - hi.astype(F32)).astype(BF16)
    return (lax.dot_general(hi, tri, NN, preferred_element_type=F32)
            + lax.dot_general(lo, tri, NN, preferred_element_type=F32))


def _log1m(z):
    return -(jnp.maximum(z, 0.0) + jnp.log(1.0 + jnp.exp(-jnp.abs(z))))


def _sb_fwd(proj, gather=()):
    T = proj.shape[0]
    B, BK = min(SB_BLOCK, T), SB_KBLOCK
    R = B // BK
    nq, n = T // B, len(gather)
    scale = SB_HEAD_DIM ** -0.5

    def body(q_ref, k_ref, v_ref, *rest):
        o_ref, ls_ref = rest[n:n + 2]
        hp, i = pl.program_id(0), pl.program_id(1)
        if n:
            start, forward, finish = _gather_stages(rest[:n], rest[n + 2:2 * n + 2], *rest[2 * n + 2:])
            pl.when((hp == 0) & (i == 0))(start)
            pl.when((hp == 3) & (i == nq - 1))(forward)
        lane = _iota((1, 128), 1)
        tri_gt = (_iota((BK, BK), 0) > _iota((BK, BK), 1)).astype(BF16)
        row, col = _iota((B, BK), 0), _iota((B, BK), 1)
        qv = q_ref[...] * scale
        hms = [(lane >= 64 * h) & (lane < 64 * (h + 1)) for h in range(2)]
        qhs = [jnp.where(hm, qv, 0.0).astype(BF16) for hm in hms]

        def tile(j, carry, d):
            rows = pl.ds(pl.multiple_of(j * BK, BK), BK)
            kj = k_ref[rows, :].astype(BF16)
            vj = v_ref[rows, :].astype(BF16)
            r0 = 0 if d is None else BK * d
            valid = None if d is None else (col[r0:] < row[:B - r0])
            out = []
            for h in range(2):
                c, acc = carry[h]
                z = lax.dot_general(qhs[h][r0:], kj, NT, preferred_element_type=F32)
                lg = _log1m(z)
                if d is not None:
                    lg = jnp.where(valid, lg, 0.0)
                a = jnp.exp(z + lg + _split_dot(lg, tri_gt) + c[r0:])
                if d is not None:
                    a = jnp.where(valid, a, 0.0)
                upd = (c[r0:] + jnp.sum(lg, axis=1, keepdims=True),
                       acc[r0:] + lax.dot_general(a.astype(BF16), vj, NN, preferred_element_type=F32))
                out.append(upd if r0 == 0 else tuple(jnp.concatenate([old[:r0], new], axis=0)
                                                     for old, new in zip((c, acc), upd)))
            return tuple(out)

        zero = (jnp.zeros((B, 1), F32), jnp.zeros((B, 128), F32))
        carry = (zero, zero)
        for d in reversed(range(R)):
            carry = tile(i * R + d, carry, d)
        carry = lax.fori_loop(0, i * R, lambda s, cr: tile(i * R - 1 - s, cr, None), carry)
        o_ref[...] = jnp.where(hms[0], carry[0][1], carry[1][1])
        ls_ref[...] = jnp.where(hms[0], carry[0][0], carry[1][0])
        if n:
            pl.when((hp == 3) & (i == nq - 1))(finish)

    blk = pl.BlockSpec((B, 128), lambda hp, i: (i, hp))
    out = pl.pallas_call(
        body, name="sb_fwd", grid=(4, nq),
        in_specs=[pl.BlockSpec((B, 128), lambda hp, i: (i, 4 + hp)),
                  pl.BlockSpec((T, 128), lambda hp, i: (0, 8 + hp)),
                  pl.BlockSpec((T, 128), lambda hp, i: (0, 12 + hp))] + [HBM_SPEC] * n,
        out_specs=[blk, blk] + [HBM_SPEC] * n,
        out_shape=[jax.ShapeDtypeStruct((T, 512), F32)] * 2 + _gather_shapes(gather),
        scratch_shapes=_gather_sems(n) if n else [],
        compiler_params=_params("arbitrary", "arbitrary"),
    )(proj, proj, proj, *gather)
    return out[0], out[1], list(out[2:])


def _sb_bwd(proj, lsum, dout, exchange=()):
    T = proj.shape[0]
    B, BK = min(SB_BLOCK, T), SB_KBLOCK
    R = B // BK
    nq, n = T // B, len(exchange)
    scale = SB_HEAD_DIM ** -0.5

    def body(q_ref, k_ref, v_ref, do_ref, ls_ref, *rest):
        dq_ref, dk_ref, dv_ref = rest[n:n + 3]
        hp, i = pl.program_id(0), pl.program_id(1)
        if n:
            start, finish = _chips_stages(rest[:n], rest[n + 3:2 * n + 3], *rest[2 * n + 3:])
            pl.when((hp == 0) & (i == 0))(start)

        @pl.when(i == 0)
        def _():
            dk_ref[...] = jnp.zeros_like(dk_ref)
            dv_ref[...] = jnp.zeros_like(dv_ref)

        lane = _iota((1, 128), 1)
        tri_le = (_iota((BK, BK), 0) <= _iota((BK, BK), 1)).astype(BF16)
        tri_lt = (_iota((BK, BK), 0) < _iota((BK, BK), 1)).astype(BF16)
        row, col = _iota((B, BK), 0), _iota((B, BK), 1)
        qv = q_ref[...] * scale
        dov = do_ref[...]
        hms = [(lane >= 64 * h) & (lane < 64 * (h + 1)) for h in range(2)]
        qhs = [jnp.where(hm, qv, 0.0).astype(BF16) for hm in hms]
        dos = [jnp.where(hm, dov, 0.0).astype(BF16) for hm in hms]
        ltots = [ls_ref[:, 64 * h:64 * h + 1] for h in range(2)]

        def tile(j, carry, d):
            rows = pl.ds(pl.multiple_of(j * BK, BK), BK)
            kj = k_ref[rows, :].astype(BF16)
            vj = v_ref[rows, :].astype(BF16)
            diag = d is not None
            r0 = BK * d if diag else 0
            valid = (col[r0:] < row[:B - r0]) if diag else None
            out = []
            dkj = jnp.zeros((BK, 128), F32)
            dvj = jnp.zeros((BK, 128), F32)
            for h in range(2):
                lbef, ebef, dqa = carry[h]
                qh, do_h = qhs[h][r0:], dos[h][r0:]
                z = lax.dot_general(qh, kj, NT, preferred_element_type=F32)
                lg = _log1m(z)
                if diag:
                    lg = jnp.where(valid, lg, 0.0)
                a = jnp.exp(z + lg + (ltots[h][r0:] - lbef[r0:] - _split_dot(lg, tri_le)))
                if diag:
                    a = jnp.where(valid, a, 0.0)
                e = a * lax.dot_general(do_h, vj, NT, preferred_element_type=F32)
                dz = e * jnp.exp(lg) - jnp.exp(z + lg) * (ebef[r0:] + _split_dot(e, tri_lt))
                if diag:
                    dz = jnp.where(valid, dz, 0.0)
                dzb = dz.astype(BF16)
                dkj = dkj + lax.dot_general(dzb, qh, TN, preferred_element_type=F32)
                dvj = dvj + lax.dot_general(a.astype(BF16), do_h, TN, preferred_element_type=F32)
                upd = (lbef[r0:] + jnp.sum(lg, axis=1, keepdims=True), ebef[r0:] + jnp.sum(e, axis=1, keepdims=True),
                       dqa[r0:] + lax.dot_general(dzb, kj, NN, preferred_element_type=F32))
                out.append(upd if r0 == 0 else tuple(jnp.concatenate([old[:r0], new], axis=0)
                                                     for old, new in zip(carry[h], upd)))
            dk_ref[rows, :] += dkj
            dv_ref[rows, :] += dvj
            return tuple(out)

        zero = (jnp.zeros((B, 1), F32), jnp.zeros((B, 1), F32), jnp.zeros((B, 128), F32))
        carry = lax.fori_loop(0, i * R, lambda j, cr: tile(j, cr, None), (zero, zero))
        for d in range(R):
            carry = tile(i * R + d, carry, d)
        dq_ref[...] = jnp.where(hms[0], carry[0][2], carry[1][2]) * scale
        if n:
            pl.when((hp == 3) & (i == nq - 1))(finish)

    full = pl.BlockSpec((T, 128), lambda hp, i: (0, hp))
    blk = pl.BlockSpec((B, 128), lambda hp, i: (i, hp))
    out = pl.pallas_call(
        body, name="sb_bwd", grid=(4, nq),
        in_specs=[pl.BlockSpec((B, 128), lambda hp, i: (i, 4 + hp)),
                  pl.BlockSpec((T, 128), lambda hp, i: (0, 8 + hp)),
                  pl.BlockSpec((T, 128), lambda hp, i: (0, 12 + hp)),
                  pl.BlockSpec((B, 128), lambda hp, i: (i, 4 + hp)), blk] + [HBM_SPEC] * n,
        out_specs=[blk, full, full] + [HBM_SPEC] * n,
        out_shape=[jax.ShapeDtypeStruct((T, 512), F32)] * 3 + _chips_shapes(exchange),
        scratch_shapes=_chips_sems(n) if n else [],
        compiler_params=_params("arbitrary", "arbitrary"),
    )(proj, proj, proj, dout, lsum, *exchange)
    return out[0], out[1], out[2], list(out[3:])


def _sigmoid(x):
    return 1.0 / (1.0 + jnp.exp(-x))


def _silu_mul(cg, cv):
    return cg * _sigmoid(cg) * cv


def _ffn_act(up, conv_w, *, tr=512, cb=256):
    T, F2 = up.shape
    nc = F2 // 2 // cb
    K = FFN_CONV

    def body(g_ref, gh_ref, v_ref, vh_ref, wg_ref, wv_ref, o_ref):
        i = pl.program_id(0)
        gh = jnp.where(i > 0, gh_ref[...], 0.0)
        vh = jnp.where(i > 0, vh_ref[...], 0.0)
        cg = _conv_rows(jnp.concatenate([gh, g_ref[...]], axis=0), wg_ref, K, 8)
        cv = _conv_rows(jnp.concatenate([vh, v_ref[...]], axis=0), wv_ref, K, 8)
        o_ref[...] = _silu_mul(cg, cv).astype(BF16)

    return pl.pallas_call(
        body, name="ffn_act", grid=(T // tr, nc),
        in_specs=[pl.BlockSpec((tr, cb), lambda i, j: (i, j)), _prev_spec(tr, cb, 8, lambda i, j: j),
                  pl.BlockSpec((tr, cb), lambda i, j: (i, nc + j)), _prev_spec(tr, cb, 8, lambda i, j: nc + j),
                  pl.BlockSpec((K, cb), lambda i, j: (0, j)), pl.BlockSpec((K, cb), lambda i, j: (0, nc + j))],
        out_specs=pl.BlockSpec((tr, cb), lambda i, j: (i, j)),
        out_shape=jax.ShapeDtypeStruct((T, F2 // 2), BF16), compiler_params=_params("parallel", "parallel"),
    )(up, up, up, up, conv_w, conv_w)


def _conv_bwd_rows(dc_ext, x_ext, w_ref, K, tr, cs=slice(None)):
    n = tr + 8
    dx = w_ref[K - 1:K, cs] * dc_ext
    for i in range(K - 1):
        dx = dx + w_ref[i:i + 1, cs] * pltpu.roll(dc_ext, n - (K - 1 - i), 0)
    dc = dc_ext[:tr]
    dws = [jnp.sum(dc * pltpu.roll(x_ext, K - 1 - i, 0)[8:8 + tr], axis=0, keepdims=True) for i in range(K)]
    return dx[:tr], dws


def _acc_rows(ref, rows, first, cs=slice(None)):
    for i, r in enumerate(rows):
        @pl.when(first)
        def _():
            ref[i:i + 1, cs] = r

        @pl.when(jnp.logical_not(first))
        def _():
            ref[i:i + 1, cs] += r


def _ffn_act_bwd(up, conv_w, dact, *, tr=512, cb=256):
    T, F2 = up.shape
    F = F2 // 2
    nc, nb = F // cb, T // tr
    K = FFN_CONV

    def body(g_ref, gp_ref, gn_ref, v_ref, vp_ref, vn_ref, d_ref, dn_ref, wg_ref, wv_ref,
             dg_ref, dv_ref, dwg_ref, dwv_ref):
        i = pl.program_id(1)
        first, last = i == 0, i == nb - 1
        g_ext = jnp.concatenate([jnp.where(first, 0.0, gp_ref[...]), g_ref[...], jnp.where(last, 0.0, gn_ref[...])], axis=0)
        v_ext = jnp.concatenate([jnp.where(first, 0.0, vp_ref[...]), v_ref[...], jnp.where(last, 0.0, vn_ref[...])], axis=0)
        d_ext = jnp.concatenate([d_ref[...], jnp.where(last, 0.0, dn_ref[...])], axis=0)
        cg = _conv_rows(g_ext, wg_ref, K, 8)
        cv = _conv_rows(v_ext, wv_ref, K, 8)
        s = _sigmoid(cg)
        t = cg * s
        dcv = d_ext * t
        dcg = d_ext * cv * (s + t * (1.0 - s))
        dg, dwg = _conv_bwd_rows(dcg, g_ext, wg_ref, K, tr)
        dv, dwv = _conv_bwd_rows(dcv, v_ext, wv_ref, K, tr)
        dg_ref[...] = dg.astype(BF16)
        dv_ref[...] = dv.astype(BF16)
        _acc_rows(dwg_ref, dwg, first)
        _acc_rows(dwv_ref, dwv, first)

    blk = lambda off: pl.BlockSpec((tr, cb), lambda j, i: (i, off + j))
    prev = lambda off: pl.BlockSpec((8, cb), lambda j, i: (jnp.maximum(i * (tr // 8) - 1, 0), off + j))
    nxt = lambda off: pl.BlockSpec((8, cb), lambda j, i: (jnp.minimum((i + 1) * (tr // 8), T // 8 - 1), off + j))
    wsp = lambda off: pl.BlockSpec((K, cb), lambda j, i: (0, off + j))
    return pl.pallas_call(
        body, name="ffn_act_bwd", grid=(nc, nb),
        in_specs=[blk(0), prev(0), nxt(0), blk(nc), prev(nc), nxt(nc), blk(0), nxt(0), wsp(0), wsp(nc)],
        out_specs=[blk(0), blk(0), wsp(0), wsp(0)],
        out_shape=[jax.ShapeDtypeStruct((T, F), BF16)] * 2 + [jax.ShapeDtypeStruct((K, F), F32)] * 2,
        compiler_params=_params("parallel", "arbitrary"),
    )(up, up, up, up, up, up, dact, dact, conv_w, conv_w)


def _ple_fwd(hn, w_gate, p, w_ple, x, *, name, tm=1024, tn=512):
    T, Dm = x.shape
    tm = min(tm, T)

    def body(a_ref, b_ref, p_ref, wp_ref, x_ref, o_ref, gl_ref, pe_ref):
        gl = _dot(a_ref[...], b_ref[...], NN)
        pe = _dot(p_ref[...], wp_ref[...], NN)
        gl_ref[...] = gl
        pe_ref[...] = pe
        o_ref[...] = x_ref[...] + pe * _sigmoid(gl)

    o_spec = pl.BlockSpec((tm, tn), lambda i, j: (i, j))
    return pl.pallas_call(
        body, name=name, grid=(T // tm, Dm // tn),
        in_specs=[pl.BlockSpec((tm, Dm), lambda i, j: (i, 0)), pl.BlockSpec((Dm, tn), lambda i, j: (0, j)),
                  pl.BlockSpec((tm, PLE_DIM), lambda i, j: (i, 0)), pl.BlockSpec((PLE_DIM, tn), lambda i, j: (0, j)),
                  o_spec],
        out_specs=[o_spec] * 3, out_shape=[jax.ShapeDtypeStruct((T, Dm), F32)] * 3,
        compiler_params=_params("parallel", "parallel"),
    )(hn, w_gate, p, w_ple, x)


def _ple_bwd(dx, gl, pe, *, name, tr=512):
    T, Dm = dx.shape

    def body(dx_ref, gl_ref, pe_ref, dpe_ref, dgl_ref):
        g = _sigmoid(gl_ref[...])
        d = dx_ref[...]
        dpe_ref[...] = (d * g).astype(BF16)
        dgl_ref[...] = (d * pe_ref[...] * g * (1.0 - g)).astype(BF16)

    row = pl.BlockSpec((tr, Dm), lambda i: (i, 0))
    return pl.pallas_call(
        body, name=name, grid=(T // tr,), in_specs=[row] * 3, out_specs=[row] * 2,
        out_shape=[jax.ShapeDtypeStruct((T, Dm), BF16)] * 2, compiler_params=_params("parallel"),
    )(dx, gl, pe)


def _qkv_act(c, cb):
    s = c * _sigmoid(c)
    n = s * lax.rsqrt(jnp.sum(s * s, axis=-1, keepdims=True) + EPS)
    n = n * jnp.where(cb < GDN_HEADS, GDN_HEAD_DIM ** -0.5, 1.0)
    return jnp.where(cb < 2 * GDN_HEADS, n, s)


GDN_HPS = 4


def _gdn_pre(proj, conv_w, *, tr=512):
    T = proj.shape[0]
    K = GDN_CONV

    def body(x_ref, xh_ref, w_ref, o_ref):
        i, j = pl.program_id(0), pl.program_id(1)
        xh = jnp.where(i > 0, xh_ref[...], 0.0)
        for hh in range(GDN_HPS):
            cs = slice(128 * hh, 128 * (hh + 1))
            c = _conv_rows(jnp.concatenate([xh[:, cs], x_ref[:, cs]], axis=0), w_ref, K, 8, cs)
            o_ref[hh] = _qkv_act(c, GDN_HPS * j + hh)

    wide = 128 * GDN_HPS
    return pl.pallas_call(
        body, name="gdn_pre", grid=(T // tr, 24 // GDN_HPS),
        in_specs=[pl.BlockSpec((tr, wide), lambda i, j: (i, j)), _prev_spec(tr, wide, 8, lambda i, j: j),
                  pl.BlockSpec((K, wide), lambda i, j: (0, j))],
        out_specs=pl.BlockSpec((GDN_HPS, tr, 128), lambda i, j: (j, i, 0)),
        out_shape=jax.ShapeDtypeStruct((24, T, 128), F32), compiler_params=_params("parallel", "parallel"),
    )(proj, proj, conv_w)


def _gdn_pre_bwd(proj, conv_w, dqkv, *, tr=512):
    T = proj.shape[0]
    nb = T // tr
    K = GDN_CONV

    def body(x_ref, xp_ref, xn_ref, d_ref, dn_ref, w_ref, dx_ref, dw_ref):
        j, i = pl.program_id(0), pl.program_id(1)
        first, last = i == 0, i == nb - 1
        xp = jnp.where(first, 0.0, xp_ref[...])
        xn = jnp.where(last, 0.0, xn_ref[...])
        for hh in range(GDN_HPS):
            cs = slice(128 * hh, 128 * (hh + 1))
            x_ext = jnp.concatenate([xp[:, cs], x_ref[:, cs], xn[:, cs]], axis=0)
            d_ext = jnp.concatenate([d_ref[hh], jnp.where(last, 0.0, dn_ref[hh])], axis=0)
            c = _conv_rows(x_ext, w_ref, K, 8, cs)
            _, vjp = jax.vjp(lambda c_: _qkv_act(c_, GDN_HPS * j + hh), c)
            (dc,) = vjp(d_ext)
            dx, dws = _conv_bwd_rows(dc, x_ext, w_ref, K, tr, cs)
            dx_ref[:, cs] = dx.astype(BF16)
            _acc_rows(dw_ref, dws, first, cs)

    wide = 128 * GDN_HPS
    return pl.pallas_call(
        body, name="gdn_pre_bwd", grid=(24 // GDN_HPS, nb),
        in_specs=[pl.BlockSpec((tr, wide), lambda j, i: (i, j)),
                  pl.BlockSpec((8, wide), lambda j, i: (jnp.maximum(i * (tr // 8) - 1, 0), j)),
                  pl.BlockSpec((8, wide), lambda j, i: (jnp.minimum((i + 1) * (tr // 8), T // 8 - 1), j)),
                  pl.BlockSpec((GDN_HPS, tr, 128), lambda j, i: (j, i, 0)),
                  pl.BlockSpec((GDN_HPS, 8, 128), lambda j, i: (j, jnp.minimum((i + 1) * (tr // 8), T // 8 - 1), 0)),
                  pl.BlockSpec((K, wide), lambda j, i: (0, j))],
        out_specs=[pl.BlockSpec((tr, wide), lambda j, i: (i, j)), pl.BlockSpec((K, wide), lambda j, i: (0, j))],
        out_shape=[jax.ShapeDtypeStruct((T, 24 * 128), BF16), jax.ShapeDtypeStruct((K, 24 * 128), F32)],
        compiler_params=_params("parallel", "arbitrary"),
    )(proj, proj, proj, dqkv, dqkv, conv_w)


def _gate_fn(ba, alog_row, dt_row):
    lane = _iota((1, 128), 1)
    x = ba + dt_row
    sp = jnp.maximum(x, 0.0) + jnp.log(1.0 + jnp.exp(-jnp.abs(x)))
    return jnp.where(lane < GDN_HEADS, _sigmoid(ba), -jnp.exp(alog_row) * sp)


def _gdn_gate(ba, alog_row, dt_row, *, tr=512):
    T = ba.shape[0]

    def body(ba_ref, al_ref, dt_ref, b_ref, g_ref):
        val = _gate_fn(ba_ref[...], al_ref[...], dt_ref[...])
        for h in range(GDN_HEADS):
            b_ref[h] = val[:, h:h + 1]
            g_ref[h] = val[:, GDN_HEADS + h:GDN_HEADS + h + 1]

    vec = pl.BlockSpec((1, 128), lambda i: (0, 0))
    hm = pl.BlockSpec((GDN_HEADS, tr, 1), lambda i: (0, i, 0))
    return pl.pallas_call(
        body, name="gdn_gate", grid=(T // tr,), in_specs=[pl.BlockSpec((tr, 128), lambda i: (i, 0)), vec, vec],
        out_specs=[hm, hm], out_shape=[jax.ShapeDtypeStruct((GDN_HEADS, T, 1), F32)] * 2,
        compiler_params=_params("parallel"),
    )(ba, alog_row, dt_row)


def _gdn_gate_bwd(ba, alog_row, dt_row, dbeta, dg, *, tr=512):
    T = ba.shape[0]

    def body(ba_ref, al_ref, dt_ref, db_ref, dg_ref, dba_ref, dal_ref, ddt_ref):
        i = pl.program_id(0)
        lane = _iota((1, 128), 1)
        d = jnp.zeros((tr, 128), F32)
        for h in range(GDN_HEADS):
            d = d + jnp.where(lane == h, db_ref[h], 0.0) + jnp.where(lane == GDN_HEADS + h, dg_ref[h], 0.0)
        _, vjp = jax.vjp(_gate_fn, ba_ref[...], al_ref[...], dt_ref[...])
        dba, dal, ddt = vjp(d)
        dba_ref[...] = dba.astype(BF16)

        @pl.when(i == 0)
        def _():
            dal_ref[...] = dal
            ddt_ref[...] = ddt

        @pl.when(i > 0)
        def _():
            dal_ref[...] += dal
            ddt_ref[...] += ddt

    vec = pl.BlockSpec((1, 128), lambda i: (0, 0))
    hm = pl.BlockSpec((GDN_HEADS, tr, 1), lambda i: (0, i, 0))
    row = pl.BlockSpec((tr, 128), lambda i: (i, 0))
    return pl.pallas_call(
        body, name="gdn_gate_bwd", grid=(T // tr,), in_specs=[row, vec, vec, hm, hm], out_specs=[row, vec, vec],
        out_shape=[jax.ShapeDtypeStruct((T, 128), BF16), jax.ShapeDtypeStruct((1, 128), F32),
                   jax.ShapeDtypeStruct((1, 128), F32)],
        compiler_params=_params("arbitrary"),
    )(ba, alog_row, dt_row, dbeta, dg)


def _split3(x):
    x1 = x.astype(BF16)
    r = x - x1.astype(F32)
    x2 = r.astype(BF16)
    return x1, x2, (r - x2.astype(F32)).astype(BF16)


def _dot01(tri, x, dims):
    t = tri.astype(BF16)
    x1, x2, x3 = _split3(x)
    d = lambda xi: lax.dot_general(t, xi, dims, preferred_element_type=F32)
    return d(x1) + (d(x2) + d(x3))


def _dot3(a, b, dims):
    ah, al, _ = _split3(a)
    bh, bl, _ = _split3(b)
    d = lambda p, q: lax.dot_general(p, q, dims, preferred_element_type=F32)
    return d(ah, bh) + (d(ah, bl) + d(al, bh))


BNN = (((2,), (1,)), ((0,), (0,)))
BNT = (((2,), (2,)), ((0,), (0,)))
BTN = (((1,), (1,)), ((0,), (0,)))


@jax.custom_vjp
def _mm01(tri, x):
    return _dot01(tri, x, BNN)


def _mm01_fwd(tri, x):
    return _dot01(tri, x, BNN), tri


def _mm01_bwd(tri, ct):
    return jnp.zeros_like(tri), _dot01(tri, ct, BTN)


_mm01.defvjp(_mm01_fwd, _mm01_bwd)


def _unit_lower_inverse(a):
    C = a.shape[-1]
    eye = (_iota(a.shape, 1) == _iota(a.shape, 2)).astype(F32)
    pw = -a
    tinv = eye + pw
    for _ in range(5):
        pw = _dot3(pw, pw, BNN)
        tinv = tinv + _dot3(tinv, pw, BNN)
    return tinv


@jax.custom_vjp
def _unit_lower_solve(a, rv, rw):
    return _unit_lower_solve_fwd(a, rv, rw)[0]


def _unit_lower_solve_fwd(a, rv, rw):
    tinv = _unit_lower_inverse(a)
    sol = _dot3(tinv, jnp.concatenate([rv, rw], axis=2), BNN)
    n = rv.shape[2]
    return (sol[:, :, :n], sol[:, :, n:]), (tinv, sol)


def _unit_lower_solve_bwd(res, cts):
    tinv, sol = res
    n = cts[0].shape[2]
    d_rhs = _dot3(tinv, jnp.concatenate(cts, axis=2), BTN)
    return -_dot3(d_rhs, sol, BNT), d_rhs[:, :, :n], d_rhs[:, :, n:]


_unit_lower_solve.defvjp(_unit_lower_solve_fwd, _unit_lower_solve_bwd)


@jax.custom_vjp
def _mmb_nt(a, b):
    return _dot(a, b, BNT)


def _mmb_nt_fwd(a, b):
    return _dot(a, b, BNT), (a, b)


def _mmb_nt_bwd(res, ct):
    a, b = res
    return _dot(ct, b, BNN), _dot(ct, a, BTN)


_mmb_nt.defvjp(_mmb_nt_fwd, _mmb_nt_bwd)


def _gdn_chunk(q, k, v, gcol, bcol):
    nb, C = q.shape[0], GDN_CHUNK
    row, col = _iota((nb, C, C), 1), _iota((nb, C, C), 2)
    incl, strict = row >= col, row > col
    eye = (row == col).astype(F32)
    lower = incl.astype(F32)
    ones = jnp.ones((nb, C, C), F32)
    gwide = jnp.broadcast_to(gcol, (nb, C, GDN_HEAD_DIM))
    gc = _mm01(lower, gwide)
    gtot = _mm01(ones, gwide)
    gc_c = _mm01(lower, jnp.broadcast_to(gcol, (nb, C, C)))
    gc_s = _mm01(ones, gc_c * eye)
    decay = jnp.where(incl, jnp.exp(jnp.where(incl, gc_c - gc_s, 0.0)), 0.0)
    kb = k * bcol
    a = jnp.where(strict, _mmb_nt(kb, k) * decay, 0.0)
    egc = jnp.exp(gc)
    u, w = _unit_lower_solve(a, v * bcol, kb * egc)
    qk = jnp.where(incl, _mmb_nt(q, k) * decay, 0.0)
    return u, w, qk, q * egc, k * jnp.exp(gtot - gc), jnp.exp(jnp.sum(gwide, axis=1))


GDN_ROWS = 8 * GDN_CHUNK


GDN_LOCAL_CHUNKS = 16


def _gdn_specs(T):
    nch = min(GDN_LOCAL_CHUNKS, T // GDN_CHUNK)
    L = nch * GDN_CHUNK
    hd = lambda off: pl.BlockSpec((1, L, 128), lambda h, i: (off + h, i, 0))
    col = pl.BlockSpec((1, L, 1), lambda h, i: (h, i, 0))
    sq = pl.BlockSpec((1, L, GDN_CHUNK), lambda h, i: (h, i, 0))
    gl = pl.BlockSpec((1, nch, 128), lambda h, i: (h, i, 0))
    return nch, hd, col, sq, gl


def _gdn_local(qkv, g, beta):
    T = qkv.shape[1]
    nch, hd, col, sq, gl_spec = _gdn_specs(T)

    def body(q_ref, k_ref, v_ref, g_ref, b_ref, u_ref, w_ref, qk_ref, qd_ref, kd_ref, gl_ref):
        chunks = lambda ref: ref[0].reshape(nch, GDN_CHUNK, ref.shape[2])
        rows = lambda val: val.reshape(nch * GDN_CHUNK, val.shape[2])
        u, w, qk, qd, kd, gl = _gdn_chunk(chunks(q_ref), chunks(k_ref), chunks(v_ref), chunks(g_ref), chunks(b_ref))
        u_ref[0] = rows(u)
        w_ref[0] = rows(w).astype(BF16)
        qk_ref[0] = rows(qk).astype(BF16)
        qd_ref[0] = rows(qd).astype(BF16)
        kd_ref[0] = rows(kd).astype(BF16)
        gl_ref[0] = gl

    H = GDN_HEADS
    return pl.pallas_call(
        body, name="gdn_local", grid=(H, T // (nch * GDN_CHUNK)),
        in_specs=[hd(0), hd(H), hd(2 * H), col, col],
        out_specs=[hd(0), hd(0), sq, hd(0), hd(0), gl_spec],
        out_shape=[jax.ShapeDtypeStruct((H, T, 128), F32), jax.ShapeDtypeStruct((H, T, 128), BF16),
                   jax.ShapeDtypeStruct((H, T, GDN_CHUNK), BF16), jax.ShapeDtypeStruct((H, T, 128), BF16),
                   jax.ShapeDtypeStruct((H, T, 128), BF16), jax.ShapeDtypeStruct((H, T // GDN_CHUNK, 128), F32)],
        compiler_params=_params("parallel", "parallel"),
    )(qkv, qkv, qkv, g, beta)


def _gdn_local_bwd(qkv, g, beta, du, dw, dqk, dqd, dkd, dgl):
    T = qkv.shape[1]
    nch, hd, col, sq, gl_spec = _gdn_specs(T)

    def body(q_ref, k_ref, v_ref, g_ref, b_ref, du_ref, dw_ref, dqk_ref, dqd_ref, dkd_ref, dgl_ref,
             dqkv_ref, dg_ref, db_ref):
        chunks = lambda ref: ref[0].reshape(nch, GDN_CHUNK, ref.shape[2])
        rows = lambda val: val.reshape(nch * GDN_CHUNK, val.shape[2])
        _, vjp = jax.vjp(_gdn_chunk, chunks(q_ref), chunks(k_ref), chunks(v_ref), chunks(g_ref), chunks(b_ref))
        dq, dk, dv, dg, db = vjp((chunks(du_ref), chunks(dw_ref), chunks(dqk_ref), chunks(dqd_ref), chunks(dkd_ref),
                                  dgl_ref[0]))
        dqkv_ref[0, 0] = rows(dq)
        dqkv_ref[1, 0] = rows(dk)
        dqkv_ref[2, 0] = rows(dv)
        dg_ref[0] = rows(dg)
        db_ref[0] = rows(db)

    H = GDN_HEADS
    small = jax.ShapeDtypeStruct((H, T, 1), F32)
    dqkv, dg, db = pl.pallas_call(
        body, name="gdn_local_bwd", grid=(H, T // (nch * GDN_CHUNK)),
        in_specs=[hd(0), hd(H), hd(2 * H), col, col, hd(0), hd(0), sq, hd(0), hd(0), gl_spec],
        out_specs=[pl.BlockSpec((3, 1, nch * GDN_CHUNK, 128), lambda h, i: (0, h, i, 0)), col, col],
        out_shape=[jax.ShapeDtypeStruct((3, H, T, 128), F32), small, small],
        compiler_params=_params("parallel", "parallel"),
    )(qkv, qkv, qkv, g, beta, du, dw, dqk, dqd, dkd, dgl)
    return dqkv.reshape(3 * H, T, 128), dg, db


GDN_HB = 4


def _gdn_scan_specs(T, rev):
    nb = T // GDN_ROWS
    blk = (lambda i: nb - 1 - i) if rev else (lambda i: i)
    hd = pl.BlockSpec((GDN_HB, GDN_ROWS, 128), lambda h, i: (h, blk(i), 0))
    sq = pl.BlockSpec((GDN_HB, GDN_ROWS, GDN_CHUNK), lambda h, i: (h, blk(i), 0))
    gl = pl.BlockSpec((GDN_HB, 8, 128), lambda h, i: (h, blk(i), 0))
    st = pl.BlockSpec((GDN_HB, 8, 128, 128), lambda h, i: (h, blk(i), 0, 0))
    return hd, sq, gl, st


def _gdn_scan(u, w, qk, qd, kd, gl):
    H, T, _ = u.shape
    hd, sq, gl_spec, st = _gdn_scan_specs(T, False)

    def body(u_ref, w_ref, qk_ref, qd_ref, kd_ref, gl_ref, o_ref, ss_ref, vn_ref, s_scr):
        @pl.when(pl.program_id(1) == 0)
        def _():
            s_scr[...] = jnp.zeros_like(s_scr)

        dot = lambda a, b, dims: lax.dot_general(a, b, dims, preferred_element_type=F32)
        s = s_scr[...]
        for c in range(8):
            rs = slice(GDN_CHUNK * c, GDN_CHUNK * (c + 1))
            ss_ref[:, c] = s
            sb = s.astype(BF16)
            vn = u_ref[:, rs, :] - dot(w_ref[:, rs, :], sb, BNN)
            vnb = vn.astype(BF16)
            o_ref[:, rs, :] = dot(qd_ref[:, rs, :], sb, BNN) + dot(qk_ref[:, rs, :], vnb, BNN)
            vn_ref[:, rs, :] = vnb
            s = s * gl_ref[:, c:c + 1, :] + dot(kd_ref[:, rs, :], vnb, BTN)
        s_scr[...] = s

    return pl.pallas_call(
        body, name="gdn_scan", grid=(H // GDN_HB, T // GDN_ROWS),
        in_specs=[hd, hd, sq, hd, hd, gl_spec], out_specs=[hd, st, hd],
        out_shape=[jax.ShapeDtypeStruct((H, T, 128), F32), jax.ShapeDtypeStruct((H, T // GDN_CHUNK, 128, 128), F32),
                   jax.ShapeDtypeStruct((H, T, 128), BF16)],
        scratch_shapes=[pltpu.VMEM((GDN_HB, 128, 128), F32)],
        compiler_params=_params("parallel", "arbitrary"),
    )(u, w, qk, qd, kd, gl)


def _gdn_scan_bwd(do, ss, vn, w, qk, qd, kd, gl):
    H, T, _ = do.shape
    hd, sq, gl_spec, st = _gdn_scan_specs(T, True)

    def body(do_ref, ss_ref, vn_ref, w_ref, qk_ref, qd_ref, kd_ref, gl_ref,
             du_ref, dw_ref, dqk_ref, dqd_ref, dkd_ref, dgl_ref, ds_scr):
        @pl.when(pl.program_id(1) == 0)
        def _():
            ds_scr[...] = jnp.zeros_like(ds_scr)

        dot = lambda a, b, dims: lax.dot_general(a, b, dims, preferred_element_type=F32)
        ds = ds_scr[...]
        for c in reversed(range(8)):
            rs = slice(GDN_CHUNK * c, GDN_CHUNK * (c + 1))
            s = ss_ref[:, c]
            sb, dsb = s.astype(BF16), ds.astype(BF16)
            dob = do_ref[:, rs, :].astype(BF16)
            vnb = vn_ref[:, rs, :]
            dvn = dot(qk_ref[:, rs, :], dob, BTN) + dot(kd_ref[:, rs, :], dsb, BNN)
            dvnb = dvn.astype(BF16)
            du_ref[:, rs, :] = dvn
            dw_ref[:, rs, :] = -dot(dvnb, sb, BNT)
            dqk_ref[:, rs, :] = dot(dob, vnb, BNT)
            dqd_ref[:, rs, :] = dot(dob, sb, BNT)
            dkd_ref[:, rs, :] = dot(vnb, dsb, BNT)
            dgl_ref[:, c:c + 1, :] = jnp.sum(ds * s, axis=1, keepdims=True)
            ds = dot(qd_ref[:, rs, :], dob, BTN) + ds * gl_ref[:, c:c + 1, :] - dot(w_ref[:, rs, :], dvnb, BTN)
        ds_scr[...] = ds

    big = jax.ShapeDtypeStruct((H, T, 128), F32)
    return pl.pallas_call(
        body, name="gdn_scan_bwd", grid=(H // GDN_HB, T // GDN_ROWS),
        in_specs=[hd, st, hd, hd, sq, hd, hd, gl_spec], out_specs=[hd, hd, sq, hd, hd, gl_spec],
        out_shape=[big, big, jax.ShapeDtypeStruct((H, T, GDN_CHUNK), F32), big, big,
                   jax.ShapeDtypeStruct((H, T // GDN_CHUNK, 128), F32)],
        scratch_shapes=[pltpu.VMEM((GDN_HB, 128, 128), F32)],
        compiler_params=_params("parallel", "arbitrary"),
    )(do, ss, vn, w, qk, qd, kd, gl)


def _gated_norm(o, z, nw):
    on = o * lax.rsqrt(jnp.mean(o * o, axis=-1, keepdims=True) + EPS) * nw
    return on * (z * _sigmoid(z))


def _gdn_post(o, proj, norm_w, *, tr=512):
    T = proj.shape[0]

    def body(o_ref, z_ref, n_ref, y_ref):
        y_ref[...] = _gated_norm(o_ref[0], z_ref[...], n_ref[...]).astype(BF16)

    return pl.pallas_call(
        body, name="gdn_post", grid=(T // tr, GDN_HEADS),
        in_specs=[pl.BlockSpec((1, tr, 128), lambda i, h: (h, i, 0)), pl.BlockSpec((tr, 128), lambda i, h: (i, 24 + h)),
                  pl.BlockSpec((1, 128), lambda i, h: (0, 0))],
        out_specs=pl.BlockSpec((tr, 128), lambda i, h: (i, h)),
        out_shape=jax.ShapeDtypeStruct((T, 1024), BF16), compiler_params=_params("parallel", "parallel"),
    )(o, proj, norm_w)


def _gdn_post_bwd(o, proj, norm_w, dy, *, tr=512):
    T = proj.shape[0]

    def body(o_ref, z_ref, n_ref, dy_ref, do_ref, dz_ref, dn_ref):
        first = (pl.program_id(0) == 0) & (pl.program_id(1) == 0)
        _, vjp = jax.vjp(_gated_norm, o_ref[0], z_ref[...], n_ref[...])
        do, dz, dn = vjp(dy_ref[...])
        do_ref[0] = do
        dz_ref[...] = dz.astype(BF16)

        @pl.when(first)
        def _():
            dn_ref[...] = dn

        @pl.when(jnp.logical_not(first))
        def _():
            dn_ref[...] += dn

    blk = pl.BlockSpec((tr, 128), lambda i, h: (i, h))
    hm = pl.BlockSpec((1, tr, 128), lambda i, h: (h, i, 0))
    vec = pl.BlockSpec((1, 128), lambda i, h: (0, 0))
    return pl.pallas_call(
        body, name="gdn_post_bwd", grid=(T // tr, GDN_HEADS),
        in_specs=[hm, pl.BlockSpec((tr, 128), lambda i, h: (i, 24 + h)), vec, blk], out_specs=[hm, blk, vec],
        out_shape=[jax.ShapeDtypeStruct((GDN_HEADS, T, 128), F32), jax.ShapeDtypeStruct((T, 1024), BF16),
                   jax.ShapeDtypeStruct((1, 128), F32)],
        compiler_params=_params("arbitrary", "arbitrary"),
    )(o, proj, norm_w, dy)


HBM_SPEC = pl.BlockSpec(memory_space=pltpu.HBM)


def _place():
    return lax.axis_index("x"), lax.axis_index("y"), lax.axis_index("c")


def _all_gather(vs, *, name):
    n = len(vs)

    def body(*refs):
        start, forward, finish = _gather_stages(refs[:n], refs[n:2 * n], *refs[2 * n:])
        start()
        forward()
        finish()

    return pl.pallas_call(
        body, name=name, out_shape=_gather_shapes(vs), in_specs=[HBM_SPEC] * n, out_specs=[HBM_SPEC] * n,
        scratch_shapes=_gather_sems(n),
    )(*vs)


def _gather_shapes(vs):
    return [jax.ShapeDtypeStruct((N_DEV,) + v.shape, v.dtype) for v in vs]


def _gather_sems(n):
    return [pltpu.SemaphoreType.DMA((7 * n,)), pltpu.SemaphoreType.DMA((7 * n,)), pltpu.SemaphoreType.DMA((n,))]


def _gather_stages(v_refs, out_refs, send_sems, recv_sems, local_sems):
    n = len(v_refs)
    x, y, c = _place()
    me, sibling = (x, y, c), (x, y, 1 - c)
    chips = [(1 - x, y), (x, 1 - y), (1 - x, 1 - y)]

    def copy(a, k, block, to, from_input=False):
        slot = out_refs[a].at[4 * block[0] + 2 * block[1] + block[2]]
        return pltpu.make_async_remote_copy(
            src_ref=v_refs[a] if from_input else slot, dst_ref=slot,
            send_sem=send_sems.at[7 * a + k], recv_sem=recv_sems.at[7 * a + k], device_id=to, device_id_type=MESH)

    def mine():
        return [pltpu.make_async_copy(v_refs[a], out_refs[a].at[4 * x + 2 * y + c], local_sems.at[a]) for a in range(n)]

    def first():
        return ([copy(a, 0, me, sibling, True) for a in range(n)]
                + [copy(a, 1 + j, me, (*chip, c), True) for j, chip in enumerate(chips) for a in range(n)])

    def passed():
        return [copy(a, 4 + j, (*chip, c), sibling) for j, chip in enumerate(chips) for a in range(n)]

    def start():
        for cp in mine() + first():
            cp.start()

    def forward():
        for j, chip in enumerate(chips):
            for a in range(n):
                copy(a, 1 + j, (*chip, c), me).wait_recv()
                copy(a, 4 + j, (*chip, c), sibling).start()

    def finish():
        for a in range(n):
            copy(a, 0, sibling, me).wait_recv()
            for j, chip in enumerate(chips):
                copy(a, 4 + j, (*chip, 1 - c), me).wait_recv()
        for cp in first() + passed():
            cp.wait_send()
        for cp in mine():
            cp.wait()

    return start, forward, finish


def _exchange_sibling(gs, *, name):
    n = len(gs)

    def body(*refs):
        g_refs, out_refs = refs[:n], refs[n:2 * n]
        send_sems, recv_sems = refs[2 * n:]
        x, y, c = _place()
        copies = [pltpu.make_async_remote_copy(
            src_ref=g_refs[a].at[k, 1 - c], dst_ref=out_refs[a].at[k], send_sem=send_sems.at[4 * a + k],
            recv_sem=recv_sems.at[4 * a + k], device_id=(x, y, 1 - c), device_id_type=MESH)
            for a in range(n) for k in range(4)]
        for cp in copies:
            cp.start()
        for cp in copies:
            cp.wait()

    return pl.pallas_call(
        body, name=name, out_shape=[jax.ShapeDtypeStruct((4,) + g.shape[2:], g.dtype) for g in gs],
        in_specs=[HBM_SPEC] * n, out_specs=[HBM_SPEC] * n,
        scratch_shapes=[pltpu.SemaphoreType.DMA((4 * n,)), pltpu.SemaphoreType.DMA((4 * n,))],
    )(*gs)


def _exchange_chips(pcs):
    n = len(pcs)

    def body(*refs):
        start, finish = _chips_stages(refs[:n], refs[n:2 * n], *refs[2 * n:])
        start()
        finish()

    return pl.pallas_call(
        body, name="rs_chips", out_shape=_chips_shapes(pcs), in_specs=[HBM_SPEC] * n, out_specs=[HBM_SPEC] * n,
        scratch_shapes=_chips_sems(n),
    )(*pcs)


def _chips_shapes(pcs):
    return [jax.ShapeDtypeStruct((3,) + pc.shape[1:], pc.dtype) for pc in pcs]


def _chips_sems(n):
    return [pltpu.SemaphoreType.DMA((3 * n,)), pltpu.SemaphoreType.DMA((3 * n,))]


def _chips_stages(p_refs, out_refs, send_sems, recv_sems):
    n = len(p_refs)
    x, y, c = _place()
    chips = [(1 - x, y), (x, 1 - y), (1 - x, 1 - y)]

    def copies():
        return [pltpu.make_async_remote_copy(
            src_ref=p_refs[a].at[2 * cx + cy], dst_ref=out_refs[a].at[j], send_sem=send_sems.at[3 * a + j],
            recv_sem=recv_sems.at[3 * a + j], device_id=(cx, cy, c), device_id_type=MESH)
            for j, (cx, cy) in enumerate(chips) for a in range(n)]

    def start():
        for cp in copies():
            cp.start()

    def finish():
        for cp in copies():
            cp.wait()

    return start, finish


def _chip_partial(place, g, got, *, tr, name):
    R, W = g.shape[2:]

    def body(pl_ref, g_ref, r_ref, o_ref):
        o_ref[...] = (g_ref[0] + r_ref[...]).astype(BF16)

    return pl.pallas_call(
        body, name=name, out_shape=jax.ShapeDtypeStruct((4, R, W), BF16),
        grid_spec=pltpu.PrefetchScalarGridSpec(
            num_scalar_prefetch=1, grid=(4, R // tr),
            in_specs=[pl.BlockSpec((1, 1, tr, W), lambda k, i, pr: (k, pr[2], i, 0)),
                      pl.BlockSpec((1, tr, W), lambda k, i, pr: (k, i, 0))],
            out_specs=pl.BlockSpec((1, tr, W), lambda k, i, pr: (k, i, 0))),
        compiler_params=_params("parallel", "parallel"),
    )(place, g, got)


def _adamw_math(g, w, m, v):
    m = ADAM_B1 * m + (1.0 - ADAM_B1) * g
    v = ADAM_B2 * v + (1.0 - ADAM_B2) * (g * g)
    m_hat = m / (1.0 - ADAM_B1 ** ADAM_STEP)
    v_hat = v / (1.0 - ADAM_B2 ** ADAM_STEP)
    return -ADAM_LR * (m_hat / (jnp.sqrt(v_hat) + ADAM_EPS) + ADAM_WD * w), m, v


def _adamw_shard(place, g, got1, got2, w, m, v, *, tr, name):
    R, W = w.shape

    def body(pl_ref, g_ref, r1_ref, r2_ref, w_ref, m_ref, v_ref, go_ref, d_ref, mo_ref, vo_ref):
        gs = g_ref[0, 0] + r1_ref[0]
        for j in range(3):
            gs = gs + r2_ref[j].astype(F32)
        go_ref[...] = gs
        d_ref[...], mo_ref[...], vo_ref[...] = _adamw_math(gs, w_ref[...], m_ref[...], v_ref[...])

    row = pl.BlockSpec((tr, W), lambda i, pr: (i, 0))
    out = jax.ShapeDtypeStruct((R, W), F32)
    return pl.pallas_call(
        body, name=name, out_shape=[out] * 4,
        grid_spec=pltpu.PrefetchScalarGridSpec(
            num_scalar_prefetch=1, grid=(R // tr,),
            in_specs=[pl.BlockSpec((1, 1, tr, W), lambda i, pr: (2 * pr[0] + pr[1], pr[2], i, 0)),
                      pl.BlockSpec((1, tr, W), lambda i, pr: (2 * pr[0] + pr[1], i, 0)),
                      pl.BlockSpec((3, tr, W), lambda i, pr: (0, i, 0)), row, row, row],
            out_specs=[row] * 4),
        compiler_params=_params("parallel"),
    )(place, g, got1, got2, w, m, v)


def _adamw_replicated(parts, w, m, v):
    R, W = w.shape

    def body(p_ref, w_ref, m_ref, v_ref, go_ref, d_ref, mo_ref, vo_ref):
        gs = p_ref[0]
        for j in range(1, N_DEV):
            gs = gs + p_ref[j]
        go_ref[...] = gs
        d_ref[...], mo_ref[...], vo_ref[...] = _adamw_math(gs, w_ref[...], m_ref[...], v_ref[...])

    full = pl.BlockSpec((R, W), lambda i: (0, 0))
    out = jax.ShapeDtypeStruct((R, W), F32)
    return pl.pallas_call(
        body, name="adamw_replicated", grid=(1,), out_shape=[out] * 4,
        in_specs=[pl.BlockSpec((N_DEV, R, W), lambda i: (0, 0, 0)), full, full, full], out_specs=[full] * 4,
        compiler_params=_params("arbitrary"),
    )(parts, w, m, v)


GROUPS = {
    "in_e": (256, 512, ((("w_in_e", None), 1024, 1024),)),
    "in_o": (514, 512, ((("w_in_o", None), 1024, 1024),)),
    "up0": (704, 256, ((("w_up", 0), 1024, 1024),)),
    "up1": (704, 256, ((("w_up", 1), 1024, 1024),)),
    "down0": (1024, 176, ((("w_down", 0), 352, 352),)),
    "down1": (1024, 176, ((("w_down", 1), 352, 352),)),
    "square": (1024, 256, ((("w_out_e", None), 128, 128), (("w_out_o", None), 128, 128), (("w_ple_gate", None), 256, 256))),
    "ple": (128, 512, ((("w_ple", None), 512, 512),)),
    "conv_f": (704, 8, ((("ffn_conv", None), 6, 8),)),
    "norm_o": (128, 8, ((("mix_norm_o", None), 1, 8),)),
    "conv_o": (384, 8, ((("conv_qkv_o", None), 4, 8),)),
}
SHARDED = tuple(dict.fromkeys(p[0][0] for g in GROUPS.values() for p in g[2]))
COLUMN_SHARDED = ("w_in_e", "w_in_o", "w_up", "ffn_conv", "w_ple", "conv_qkv_o", "mix_norm_o")
PACK_W = 1024
REPL_LAYOUT = (
    ("mix_norm_e", (1, 1024), 8), ("pool_w", (1, 4, 128, 128), 64), ("pool_scale", (1, 512), 8),
    ("a_log_o", (1, 8), 8), ("dt_bias_o", (1, 8), 8), ("gdn_norm_o", (1, 128), 8),
    ("ffn_norm", (2, 1024), 8), ("ple_norm", (2, 1024), 8), ("final_norm", (1024,), 8),
)


def _pad_rows(a, rows):
    extra = rows - a.shape[-2]
    return a if extra == 0 else jnp.pad(a, [(0, 0)] * (a.ndim - 2) + [(0, extra), (0, 0)])


def _group_rows(pieces, gname):
    parts = [_pad_rows(pieces[name], padded) for name, _, padded in GROUPS[gname][2]]
    return parts[0] if len(parts) == 1 else jnp.concatenate(parts, axis=-2)


def _ungroup_rows(buf, gname):
    out, r0 = {}, 0
    for name, rows, padded in GROUPS[gname][2]:
        out[name] = buf[..., r0:r0 + rows, :]
        r0 += padded
    return out


def _shard_major(name, gfull, n_layers):
    per_layer = []
    for g in gfull:
        if g.ndim == 3:
            per_layer.append(g)
        elif name in COLUMN_SHARDED:
            k = g.shape[0]
            per_layer.append(jnp.moveaxis(g.reshape(k, N_DEV, g.shape[1] // N_DEV), 1, 0))
        else:
            per_layer.append(g.reshape(N_DEV, g.shape[0] // N_DEV, -1))
    return per_layer[0] if n_layers == 1 else jnp.concatenate(per_layer, axis=1)


def _natural(name, gathered, n_layers):
    rows = gathered.shape[1] // n_layers
    out = []
    for layer in range(n_layers):
        piece = gathered[:, layer * rows:(layer + 1) * rows]
        if name in COLUMN_SHARDED:
            out.append(jnp.moveaxis(piece, 0, 1).reshape(rows, N_DEV * piece.shape[2]))
        else:
            out.append(piece.reshape(N_DEV * rows, piece.shape[2]))
    return out


def _unshard_pairs(g, layer, rows, tag):
    n = g.shape[2]

    def body(g_ref, o_ref):
        o_ref[...] = jnp.concatenate([g_ref[0], g_ref[1]], axis=1)

    return pl.pallas_call(
        body, name="unshard_%s%d" % (tag, layer), grid=(N_DEV // 2,),
        in_specs=[pl.BlockSpec((2, rows, n), lambda p: (p, layer, 0))],
        out_specs=pl.BlockSpec((rows, 2 * n), lambda p: (0, p)),
        out_shape=jax.ShapeDtypeStruct((rows, N_DEV * n), g.dtype), compiler_params=_params("parallel"),
    )(g)


def _unshard_in_o(g, *, tr=256):
    rows, n = g.shape[1:]
    main = (N_DEV * n) // 128 * 128

    def body(g_ref, m_ref, b_ref):
        cat = jnp.concatenate([g_ref[i] for i in range(N_DEV)], axis=1)
        m_ref[...] = cat[:, :main]
        b_ref[...] = jnp.concatenate([cat[:, main:], jnp.zeros((tr, 128 - (N_DEV * n - main)), g.dtype)], axis=1)

    return pl.pallas_call(
        body, name="unshard_in_o", grid=(rows // tr,),
        in_specs=[pl.BlockSpec((N_DEV, tr, n), lambda p: (0, p, 0))],
        out_specs=[pl.BlockSpec((tr, main), lambda p: (p, 0)), pl.BlockSpec((tr, 128), lambda p: (p, 0))],
        out_shape=[jax.ShapeDtypeStruct((rows, main), g.dtype), jax.ShapeDtypeStruct((rows, 128), g.dtype)],
        compiler_params=_params("parallel"),
    )(g)


def _reshard_in_o(d_main, d_ba, n, *, tr=256):
    rows, main = d_main.shape
    tail = N_DEV * n - main

    def body(m_ref, b_ref, o_ref):
        cat = jnp.concatenate([m_ref[...], b_ref[:, :tail]], axis=1)
        for i in range(N_DEV):
            o_ref[i] = cat[:, n * i:n * (i + 1)]

    return pl.pallas_call(
        body, name="reshard_in_o", grid=(rows // tr,),
        in_specs=[pl.BlockSpec((tr, main), lambda p: (p, 0)), pl.BlockSpec((tr, 128), lambda p: (p, 0))],
        out_specs=pl.BlockSpec((N_DEV, tr, n), lambda p: (0, p, 0)),
        out_shape=jax.ShapeDtypeStruct((N_DEV, rows, n), F32), compiler_params=_params("parallel"),
    )(d_main, d_ba)


def _rows(a, rows):
    flat = a.reshape(-1)
    return jnp.pad(flat, (0, rows * PACK_W - flat.shape[0])).reshape(rows, PACK_W)


def _pack_repl(vals):
    return jnp.concatenate([_rows(vals[name].reshape(shape), rows) for name, shape, rows in REPL_LAYOUT], axis=0)


def _unpack_repl(buf):
    out, r0 = {}, 0
    for name, shape, rows in REPL_LAYOUT:
        n = 1
        for s in shape:
            n *= s
        out[name] = buf[r0:r0 + rows].reshape(-1)[:n].reshape(shape)
        r0 += rows
    return out


WEIGHTS = ("mix_norm_e", "w_in_e", "pool_w", "pool_scale", "w_out_e", "mix_norm_o", "w_in_o", "conv_qkv_o", "a_log_o",
           "dt_bias_o", "gdn_norm_o", "w_out_o", "ffn_norm", "w_up", "ffn_conv", "w_down", "ple_norm", "w_ple_gate",
           "w_ple", "final_norm")


def _ffn_forward(x, hn, w_up, conv_w, w_down, next_gain, tag):
    up = _mm(hn, w_up, name="ffn_up" + tag)
    act = _ffn_act(up, conv_w)
    out, out_n = _mm(act, w_down, res=x, norm_gain=next_gain, name="ffn_down" + tag)
    return out, out_n, (x, hn, up, act)


def _ffn_backward(dx, saved, norm_g, w_up, conv_w, w_down, tag):
    x, hn, up, act = saved
    dact = _mm(dx, w_down, tb=True, name="ffn_dact" + tag)
    d_w_down = _mm(act, dx, ta=True, name="ffn_dwdown" + tag)
    dgate, dval, dcg, dcv = _ffn_act_bwd(up, conv_w, dact)
    dhn = _mm(dgate, w_up, tb=True, name="ffn_dhn_g" + tag)
    dx_in, d_norm = _mm(dval, w_up, tb=True, k_block0=FFN_DIM // _mm_tile(FFN_DIM), res=dhn,
                        rms_bwd=(x, norm_g, dx), name="ffn_dhn_v" + tag)
    shard = w_up.shape[1] // N_DEV
    d_w_up = _mm(hn, dgate, ta=True, shard_cols=shard, shard_into=(N_DEV, 0, None), name="ffn_dwup_g" + tag)
    d_w_up = _mm(hn, dval, ta=True, shard_cols=shard, shard_into=(N_DEV, N_DEV // 2, d_w_up), name="ffn_dwup_v" + tag)
    return dx_in, d_norm, d_w_up, jnp.concatenate([dcg, dcv], axis=1), d_w_down


def _ple_forward(x, hn, w_gate, p, w_ple, tag):
    out, gl, pe = _ple_fwd(hn, w_gate, p, w_ple, x, name="ple_fwd" + tag)
    return out, (x, hn, gl, pe)


def _ple_backward(dx, saved, norm_g, w_gate, p, tag):
    x, hn, gl, pe = saved
    dpe, dgl = _ple_bwd(dx, gl, pe, name="ple_bwd" + tag)
    d_w_ple = _mm(p, dpe, ta=True, shard_cols=dpe.shape[1] // N_DEV, name="ple_dwple" + tag)
    d_w_gate = _mm(hn, dgl, ta=True, name="ple_dwgate" + tag)
    dx, d_norm = _mm(dgl, w_gate, tb=True, rms_bwd=(x, norm_g, dx), name="ple_dhn" + tag)
    return dx, d_norm, d_w_gate, d_w_ple


def kernel(x, p, mix_norm_e, w_in_e, pool_w, pool_scale, w_out_e, mix_norm_o, w_in_o, conv_qkv_o, a_log_o, dt_bias_o, gdn_norm_o, w_out_o, ffn_norm, w_up, ffn_conv, w_down, ple_norm, w_ple_gate, w_ple, final_norm, loss_target, m_mix_norm_e, m_w_in_e, m_pool_w, m_pool_scale, m_w_out_e, m_mix_norm_o, m_w_in_o, m_conv_qkv_o, m_a_log_o, m_dt_bias_o, m_gdn_norm_o, m_w_out_o, m_ffn_norm, m_w_up, m_ffn_conv, m_w_down, m_ple_norm, m_w_ple_gate, m_w_ple, m_final_norm, v_mix_norm_e, v_w_in_e, v_pool_w, v_pool_scale, v_w_out_e, v_mix_norm_o, v_w_in_o, v_conv_qkv_o, v_a_log_o, v_dt_bias_o, v_gdn_norm_o, v_w_out_o, v_ffn_norm, v_w_up, v_ffn_conv, v_w_down, v_ple_norm, v_w_ple_gate, v_w_ple, v_final_norm):
    given = dict(locals())
    place = jnp.stack(_place()).astype(jnp.int32)
    x0, tgt = x[0], loss_target[0]

    def pieces(prefix):
        out = {}
        for width, _, members in GROUPS.values():
            for (name, layer), rows, _ in members:
                a = given[prefix + name]
                out[(name, layer)] = (a if layer is None else a[layer]).reshape(rows, width)
        return out

    def flat2d(name):
        return given[name].reshape(-1, given[name].shape[-1])

    small = ("ffn_conv", "mix_norm_o", "conv_qkv_o")
    got = _all_gather([flat2d("w_in_e").astype(BF16)] + [_pad_rows(flat2d(k), 8) for k in small], name="ag_first")
    full = {("w_in_e", 0): _unshard_pairs(got[0], 0, D_MODEL, "in_e")}
    for i in range(2):
        full[("ffn_conv", i)] = _natural("ffn_conv", got[1][:, 3 * i:3 * i + 3], 1)[0]
    mix_norm_o_full = got[2][:, 0].reshape(1, D_MODEL)
    conv_qkv = _natural("conv_qkv_o", got[3][:, :4], 1)[0]
    alog_row = jnp.pad(a_log_o, ((0, 0), (8, 112)))
    dt_row = jnp.pad(dt_bias_o, ((0, 0), (8, 112)))
    lw = lambda name, i: full[(name, i)]

    h_e = _rms_fwd(x0, mix_norm_e, name="rms_mix_e")
    proj_e = _mm(h_e, lw("w_in_e", 0), name="in_e")
    pool_o = _pool_fwd(proj_e, pool_w[0], pool_scale)
    wide = ("w_out_e", "w_out_o", "w_down", "w_ple_gate")
    send = [jnp.concatenate([flat2d(k).astype(BF16) for k in wide], axis=0)]
    att_o, lsum, got = _sb_fwd(proj_e, gather=send + [flat2d(k).astype(BF16) for k in ("w_in_o", "w_up", "w_ple")])
    gathered, r0 = {}, 0
    for k in wide:
        gathered[k] = got[0][:, r0:r0 + flat2d(k).shape[0]]
        r0 += flat2d(k).shape[0]
    layers = {name: given[name].shape[0] if given[name].ndim == 3 else 1 for name in gathered}
    full.update({(name, i): w for name in gathered for i, w in enumerate(_natural(name, gathered[name], layers[name]))})
    w_in_o_main, w_in_o_ba = _unshard_in_o(got[1])
    for i in range(2):
        full[("w_up", i)] = _unshard_pairs(got[2], i, D_MODEL, "up")
        full[("w_ple", i)] = _unshard_pairs(got[3], i, PLE_DIM, "ple")
    mix_e = jnp.concatenate([pool_o, att_o.astype(BF16)], axis=1)
    x1, hf0 = _mm(mix_e, lw("w_out_e", 0), res=x0, norm_gain=ffn_norm[0:1], name="out_e")
    x2, hp0, ffn0 = _ffn_forward(x1, hf0, lw("w_up", 0), lw("ffn_conv", 0), lw("w_down", 0), ple_norm[0:1], "0")
    x3, ple0 = _ple_forward(x2, hp0, lw("w_ple_gate", 0), p[0, 0], lw("w_ple", 0), "0")

    h_o = _rms_fwd(x3, mix_norm_o_full, name="rms_mix_o")
    proj_o = _mm(h_o, w_in_o_main, name="in_o")
    ba = _mm(h_o, w_in_o_ba, name="in_o_ba")
    qkv = _gdn_pre(proj_o, conv_qkv)
    beta, g = _gdn_gate(ba, alog_row, dt_row)
    u, w_c, qk, qd, kd, gl = _gdn_local(qkv, g, beta)
    o, states, vnew = _gdn_scan(u, w_c, qk, qd, kd, gl)
    y_o = _gdn_post(o, proj_o, gdn_norm_o)
    x4, hf1 = _mm(y_o, lw("w_out_o", 0), res=x3, norm_gain=ffn_norm[1:2], name="out_o")
    x5, hp1, ffn1 = _ffn_forward(x4, hf1, lw("w_up", 1), lw("ffn_conv", 1), lw("w_down", 1), ple_norm[1:2], "1")
    x6, ple1 = _ple_forward(x5, hp1, lw("w_ple_gate", 1), p[1, 0], lw("w_ple", 1), "1")
    loss_row, dx, d_final = _final_loss(x6, final_norm.reshape(1, D_MODEL), tgt)

    grads, rgrads = {}, {}
    dx, d_ple1, grads[("w_ple_gate", 1)], grads[("w_ple", 1)] = _ple_backward(dx, ple1, ple_norm[1:2], lw("w_ple_gate", 1), p[1, 0], "1")
    dx, d_ffn1, grads[("w_up", 1)], grads[("ffn_conv", 1)], grads[("w_down", 1)] = _ffn_backward(
        dx, ffn1, ffn_norm[1:2], lw("w_up", 1), lw("ffn_conv", 1), lw("w_down", 1), "1")
    grads[("w_out_o", 0)] = _mm(y_o, dx, ta=True, name="dw_out_o")
    dy_o = _mm(dx, lw("w_out_o", 0), tb=True, name="dy_o")
    do, dz, rgrads["gdn_norm_o"] = _gdn_post_bwd(o, proj_o, gdn_norm_o, dy_o)
    du, dw_c, dqk, dqd, dkd, dgl = _gdn_scan_bwd(do, states, vnew, w_c, qk, qd, kd, gl)
    dqkv_heads, dg, dbeta = _gdn_local_bwd(qkv, g, beta, du, dw_c, dqk, dqd, dkd, dgl)
    dqkv, grads[("conv_qkv_o", 0)] = _gdn_pre_bwd(proj_o, conv_qkv, dqkv_heads)
    dba, d_alog, d_dt = _gdn_gate_bwd(ba, alog_row, dt_row, dbeta, dg)
    rgrads["a_log_o"], rgrads["dt_bias_o"] = d_alog[:, 8:16], d_dt[:, 8:16]
    dproj_o = jnp.concatenate([dqkv, dz], axis=1)
    dh = _mm(dproj_o, w_in_o_main, tb=True, name="dh_o")
    dx_o, d_mix_o = _mm(dba, w_in_o_ba, tb=True, res=dh, rms_bwd=(x3, mix_norm_o_full, dx), name="dh_o_ba")
    grads[("w_in_o", 0)] = _reshard_in_o(_mm(h_o, dproj_o, ta=True, name="dw_in_o"),
                                         _mm(h_o, dba, ta=True, name="dw_in_o_ba"), w_in_o.shape[2])
    dx = dx_o
    grads[("mix_norm_o", 0)] = d_mix_o

    dx, d_ple0, grads[("w_ple_gate", 0)], grads[("w_ple", 0)] = _ple_backward(dx, ple0, ple_norm[0:1], lw("w_ple_gate", 0), p[0, 0], "0")
    dx, d_ffn0, grads[("w_up", 0)], grads[("ffn_conv", 0)], grads[("w_down", 0)] = _ffn_backward(
        dx, ffn0, ffn_norm[0:1], lw("w_up", 0), lw("ffn_conv", 0), lw("w_down", 0), "0")
    grads[("w_out_e", 0)] = _mm(mix_e, dx, ta=True, name="dw_out_e")
    dmix = _mm(dx, lw("w_out_e", 0), tb=True, name="dmix_e")
    du_e, d_pool_w, rgrads["pool_scale"] = _pool_bwd(proj_e, dmix, pool_w[0], pool_scale)
    rgrads["pool_w"] = d_pool_w[None]

    def reduce_start(gnames, tag):
        smaj = {}
        for g in gnames:
            for (name, layer), _, _ in GROUPS[g][2]:
                of = [grads[(name, i)] for i in ((0, 1) if layer is None else (layer,)) if (name, i) in grads]
                smaj[(name, layer)] = _shard_major(name, of, len(of))
        gbuf = [_group_rows(smaj, g) for g in gnames]
        gbuf = [b.reshape((4, 2) + b.shape[1:]) for b in gbuf]
        got1 = _exchange_sibling(gbuf, name="rs_sibling" + tag)
        part = [_chip_partial(place, b, r, tr=GROUPS[g][1], name="rs_chip_partial_" + g)
                for g, b, r in zip(gnames, gbuf, got1)]
        return gbuf, got1, part

    early = tuple(g for g in GROUPS if g != "in_e")
    gbuf_e, got1_e, part_e = reduce_start(early, "_early")
    dq_e, dk_e, dv_e, got2_e = _sb_bwd(proj_e, lsum, dmix, exchange=part_e)
    dproj_e = jnp.concatenate([du_e, dq_e.astype(BF16), dk_e.astype(BF16), dv_e.astype(BF16)], axis=1)
    grads[("w_in_e", 0)] = _mm(h_e, dproj_e, ta=True, shard_cols=w_in_e.shape[2], name="dw_in_e")
    dx, rgrads["mix_norm_e"] = _mm(dproj_e, lw("w_in_e", 0), tb=True, rms_bwd=(x0, mix_norm_e, dx), name="dh_e")
    rgrads["ffn_norm"] = jnp.concatenate([d_ffn0, d_ffn1], axis=0)
    rgrads["ple_norm"] = jnp.concatenate([d_ple0, d_ple1], axis=0)
    rgrads["final_norm"] = d_final.reshape(D_MODEL)

    gbuf_l, got1_l, part_l = reduce_start(("in_e",), "_late")
    got2_l = _exchange_chips(part_l)
    wloc, mloc, vloc = pieces(""), pieces("m_"), pieces("v_")
    sh_out = [{}, {}, {}, {}]
    for g, b, r1, r2 in zip(early + ("in_e",), gbuf_e + gbuf_l, list(got1_e) + list(got1_l),
                            list(got2_e) + list(got2_l)):
        res = _adamw_shard(place, b, r1, r2, _group_rows(wloc, g), _group_rows(mloc, g), _group_rows(vloc, g),
                           tr=GROUPS[g][1], name="adamw_" + g)
        for kind in range(4):
            sh_out[kind].update(_ungroup_rows(res[kind], g))

    (rparts,) = _all_gather([_pack_repl(rgrads)], name="ag_repl_grads")
    rp_out = _adamw_replicated(rparts, _pack_repl({n: given[n] for n, _, _ in REPL_LAYOUT}),
                               _pack_repl({n: given["m_" + n] for n, _, _ in REPL_LAYOUT}),
                               _pack_repl({n: given["v_" + n] for n, _, _ in REPL_LAYOUT}))
    rp_out = [_unpack_repl(b) for b in rp_out]

    def leaf(kind, name):
        if name in SHARDED:
            mine = sh_out[kind]
            whole = mine[(name, None)] if (name, None) in mine else jnp.stack([mine[(name, 0)], mine[(name, 1)]])
            return whole.reshape(given[name].shape)
        return rp_out[kind][name]

    loss = lax.psum(loss_row[0, 0], ("x", "y", "c"))
    outs = [loss, dx[None]]
    for kind in range(4):
        outs += [leaf(kind, n) for n in WEIGHTS]
    return tuple(outs)
```

```python
import jax
import jax.numpy as jnp
from jax import lax
from jax.experimental import pallas as pl
from jax.experimental.pallas import tpu as pltpu

F32 = jnp.float32
BF16 = jnp.bfloat16

D_MODEL = 1024
PLE_DIM = 256
POOL_WINDOWS = (2, 4, 8, 16)
SB_HEAD_DIM = 64
SB_BLOCK = 1024
SB_KBLOCK = 256
GDN_HEADS = 8
GDN_HEAD_DIM = 128
GDN_CONV = 4
GDN_CHUNK = 64
FFN_DIM = 2816
FFN_CONV = 3
EPS = 1e-6
ADAM_LR, ADAM_B1, ADAM_B2, ADAM_EPS, ADAM_WD, ADAM_STEP = 0.001, 0.9, 0.999, 1e-08, 0.01, 10
N_DEV = 8
MESH = pl.DeviceIdType.MESH
VMEM_LIMIT = 56 * 1024 * 1024

NN = (((1,), (0,)), ((), ()))
NT = (((1,), (1,)), ((), ()))
TN = (((0,), (0,)), ((), ()))


def _params(*sem):
    return pltpu.CompilerParams(dimension_semantics=sem if sem else None, vmem_limit_bytes=VMEM_LIMIT)


def _dot(a, b, dims):
    return lax.dot_general(a.astype(BF16), b.astype(BF16), dims, preferred_element_type=F32)


def _iota(shape, axis):
    return lax.broadcasted_iota(jnp.int32, shape, axis)


MM_TILE, MM_TILE_11 = 1024, 1408


def _mm_tile(dim):
    if dim <= MM_TILE_11:
        return dim
    return MM_TILE if dim % MM_TILE == 0 else MM_TILE_11


def _mm(a, b, *, ta=False, tb=False, res=None, norm_gain=None, rms_bwd=None, k_block0=0, shard_cols=None,
        shard_into=None, name):
    M, K = (a.shape[1], a.shape[0]) if ta else a.shape
    N = b.shape[0] if tb else b.shape[1]
    tm, tn, tk = _mm_tile(M), _mm_tile(N), _mm_tile(K)
    if rms_bwd is not None:
        tm = min(tm, 512)
    assert M % tm == 0 and N % tn == 0 and K % tk == 0, (name, M, N, K, tm, tn, tk)
    assert (norm_gain is None and rms_bwd is None) or tn == N, name
    nk = K // tk
    dims = (((0 if ta else 1,), (1 if tb else 0,)), ((), ()))
    extra = [res] if res is not None else []
    vecs = [norm_gain] if norm_gain is not None else []
    if rms_bwd is not None:
        extra += [rms_bwd[0], rms_bwd[2]]
        vecs = [rms_bwd[1]]
    n_out = 1 if (norm_gain is None and rms_bwd is None) else 2

    def body(*refs):
        a_ref, b_ref = refs[:2]
        tiles = list(refs[2:2 + len(extra)])
        vec_refs = refs[2 + len(extra):2 + len(extra) + len(vecs)]
        n_in = 2 + len(extra) + len(vecs) + len(held)
        outs = refs[n_in:n_in + n_out]
        scr = refs[n_in + n_out:]
        p = _dot(a_ref[...], b_ref[...], dims)

        def fin(acc):
            if res is not None:
                acc = acc + tiles[0][...]
            if rms_bwd is not None:
                x_ref, dres_ref = tiles[-2:]
                xv = x_ref[...]
                r = lax.rsqrt(jnp.mean(xv * xv, axis=-1, keepdims=True) + EPS)
                xn = xv * r
                dgp = jnp.sum(acc * xn, axis=0, keepdims=True)
                dyg = acc * vec_refs[0][...]
                outs[0][...] = dres_ref[...] + r * (dyg - xn * jnp.mean(dyg * xn, axis=-1, keepdims=True))
                first = pl.program_id(0) == 0

                @pl.when(first)
                def _():
                    outs[1][...] = dgp

                @pl.when(jnp.logical_not(first))
                def _():
                    outs[1][...] += dgp
                return
            if shard_cols is not None:
                for s in range(tn // shard_cols):
                    outs[0][s] = acc[:, shard_cols * s:shard_cols * (s + 1)]
                return
            outs[0][...] = acc
            if norm_gain is not None:
                r = lax.rsqrt(jnp.mean(acc * acc, axis=-1, keepdims=True) + EPS)
                outs[1][...] = (acc * r * vec_refs[0][...]).astype(BF16)

        if nk == 1:
            fin(p)
        else:
            acc_ref = scr[0]
            k = pl.program_id(2)

            @pl.when(k == 0)
            def _():
                acc_ref[...] = p

            @pl.when(k > 0)
            def _():
                acc_ref[...] += p

            @pl.when(k == nk - 1)
            def _():
                fin(acc_ref[...])

    a_spec = pl.BlockSpec((tk, tm), lambda i, j, k: (k, i)) if ta else pl.BlockSpec((tm, tk), lambda i, j, k: (i, k))
    b_spec = (pl.BlockSpec((tn, tk), lambda i, j, k: (j, k + k_block0)) if tb
              else pl.BlockSpec((tk, tn), lambda i, j, k: (k, j)))
    o_spec = pl.BlockSpec((tm, tn), lambda i, j, k: (i, j))
    v_spec = pl.BlockSpec((1, tn), lambda i, j, k: (0, j))
    out_specs, out_shape = [o_spec], [jax.ShapeDtypeStruct((M, N), F32)]
    held = []
    if shard_cols is not None:
        per = tn // shard_cols
        total, first, buf = shard_into if shard_into is not None else (N // shard_cols, 0, None)
        out_specs = [pl.BlockSpec((per, tm, shard_cols), lambda i, j, k: (j + first // per, i, 0))]
        out_shape = [jax.ShapeDtypeStruct((total, M, shard_cols), F32)]
        held = [buf] if buf is not None else []
    if norm_gain is not None:
        out_specs, out_shape = out_specs + [o_spec], out_shape + [jax.ShapeDtypeStruct((M, N), BF16)]
    if rms_bwd is not None:
        out_specs, out_shape = out_specs + [v_spec], out_shape + [jax.ShapeDtypeStruct((1, N), F32)]
    out = pl.pallas_call(
        body, name=name, grid=(M // tm, N // tn, nk),
        in_specs=([a_spec, b_spec] + [o_spec] * len(extra) + [v_spec] * len(vecs)
                  + [pl.BlockSpec(memory_space=pl.ANY)] * len(held)),
        out_specs=out_specs, out_shape=out_shape, scratch_shapes=[pltpu.VMEM((tm, tn), F32)] if nk > 1 else [],
        input_output_aliases={2 + len(extra) + len(vecs): 0} if held else {},
        compiler_params=_params("arbitrary" if rms_bwd is not None else "parallel", "parallel", "arbitrary"),
    )(a, b, *extra, *vecs, *held)
    return out[0] if n_out == 1 else out


def _rms_fwd(x, gain, *, name, tr=512):
    T, Dm = x.shape

    def body(x_ref, g_ref, o_ref):
        xv = x_ref[...]
        r = lax.rsqrt(jnp.mean(xv * xv, axis=-1, keepdims=True) + EPS)
        o_ref[...] = (xv * r * g_ref[...]).astype(BF16)

    return pl.pallas_call(
        body, name=name, grid=(T // tr,),
        in_specs=[pl.BlockSpec((tr, Dm), lambda i: (i, 0)), pl.BlockSpec((1, Dm), lambda i: (0, 0))],
        out_specs=pl.BlockSpec((tr, Dm), lambda i: (i, 0)),
        out_shape=jax.ShapeDtypeStruct((T, Dm), BF16), compiler_params=_params("parallel"),
    )(x, gain)


def _final_loss(x, gain, target, *, tr=512):
    T, Dm = x.shape

    def body(x_ref, g_ref, t_ref, loss_ref, dx_ref, dg_ref):
        i = pl.program_id(0)
        xv = x_ref[...]
        g = g_ref[...]
        r = lax.rsqrt(jnp.mean(xv * xv, axis=-1, keepdims=True) + EPS)
        xn = xv * r
        err = xn * g - t_ref[...]
        lp = jnp.zeros((1, 128), F32) + 0.5 * jnp.sum(jnp.mean(err * err, axis=-1, keepdims=True))
        dy_v = err * (1.0 / Dm)
        dgp = jnp.sum(dy_v * xn, axis=0, keepdims=True)
        dyg = dy_v * g
        dx_ref[...] = r * (dyg - xn * jnp.mean(dyg * xn, axis=-1, keepdims=True))

        @pl.when(i == 0)
        def _():
            dg_ref[...] = dgp
            loss_ref[...] = lp

        @pl.when(i > 0)
        def _():
            dg_ref[...] += dgp
            loss_ref[...] += lp

    row = pl.BlockSpec((tr, Dm), lambda i: (i, 0))
    vec = pl.BlockSpec((1, Dm), lambda i: (0, 0))
    return pl.pallas_call(
        body, name="final_loss", grid=(T // tr,), in_specs=[row, vec, row],
        out_specs=[pl.BlockSpec((1, 128), lambda i: (0, 0)), row, vec],
        out_shape=[jax.ShapeDtypeStruct((1, 128), F32), jax.ShapeDtypeStruct((T, Dm), F32),
                   jax.ShapeDtypeStruct((1, Dm), F32)],
        compiler_params=_params("arbitrary"),
    )(x, gain, target)


def _prev_spec(tr, cb, pad, col):
    return pl.BlockSpec((pad, cb), lambda *g: (jnp.maximum(g[0] * (tr // pad) - 1, 0), col(*g)))


def _next_spec(tr, cb, pad, col, T):
    return pl.BlockSpec((pad, cb), lambda *g: (jnp.minimum((g[0] + 1) * (tr // pad), T // pad - 1), col(*g)))


def _conv_rows(x_ext, w_ref, K, pad, cs=slice(None)):
    y = w_ref[K - 1:K, cs] * x_ext
    for i in range(K - 1):
        y = y + w_ref[i:i + 1, cs] * pltpu.roll(x_ext, K - 1 - i, 0)
    return y[pad:]


def _pool_y(u_ext, g, i, tr):
    s = u_ext
    for sh in (1, 2, 4, 8)[:g + 1]:
        s = s + pltpu.roll(s, sh, 0)
    t = i * tr + _iota((tr, 128), 0)
    cnt = jnp.minimum(t + 1, POOL_WINDOWS[g]).astype(F32)
    return s[16:] / cnt - u_ext[16:]


def _pool_fwd(proj, pool_w, pool_scale, *, tr=512):
    T = proj.shape[0]

    def body(u_ref, uh_ref, w_ref, s_ref, o_ref):
        i = pl.program_id(0)
        uh = jnp.where(i > 0, uh_ref[...], 0.0)
        for g in range(4):
            cs = slice(128 * g, 128 * (g + 1))
            y = _pool_y(jnp.concatenate([uh[:, cs], u_ref[:, cs]], axis=0), g, i, tr)
            o_ref[:, cs] = (_dot(y, w_ref[g], NN) * s_ref[:, cs]).astype(BF16)

    return pl.pallas_call(
        body, name="pool_fwd", grid=(T // tr,),
        in_specs=[pl.BlockSpec((tr, 512), lambda i: (i, 0)), _prev_spec(tr, 512, 16, lambda i: 0),
                  pl.BlockSpec((4, 128, 128), lambda i: (0, 0, 0)), pl.BlockSpec((1, 512), lambda i: (0, 0))],
        out_specs=pl.BlockSpec((tr, 512), lambda i: (i, 0)),
        out_shape=jax.ShapeDtypeStruct((T, 512), BF16), compiler_params=_params("parallel"),
    )(proj, proj, pool_w, pool_scale)


def _pool_bwd(proj, dout, pool_w, pool_scale, *, tr=512):
    T = proj.shape[0]
    nb = T // tr

    def body(u_ref, uh_ref, d_ref, dn_ref, w_ref, s_ref, du_ref, dw_ref, ds_ref):
        i = pl.program_id(0)
        uh = jnp.where(i > 0, uh_ref[...], 0.0)
        dn = jnp.where(i < nb - 1, dn_ref[...], 0.0)
        t_ext = i * tr + _iota((tr + 16, 128), 0)
        for g in range(4):
            cs = slice(128 * g, 128 * (g + 1))
            sc = s_ref[:, cs]
            wg = w_ref[g]
            y = _pool_y(jnp.concatenate([uh[:, cs], u_ref[:, cs]], axis=0), g, i, tr)
            dg = d_ref[:, cs]
            dsp = jnp.sum(dg * _dot(y, wg, NN), axis=0, keepdims=True)
            dyw = dg * sc
            dwp = _dot(y, dyw, TN)
            dy_ext = _dot(jnp.concatenate([dyw, dn[:, cs] * sc], axis=0), wg, NT)
            cnt = jnp.minimum(t_ext + 1, POOL_WINDOWS[g]).astype(F32)
            s = dy_ext / cnt
            for sh in (1, 2, 4, 8)[:g + 1]:
                s = s + pltpu.roll(s, tr + 16 - sh, 0)
            du_ref[:, cs] = (s[:tr] - dy_ext[:tr]).astype(BF16)

            @pl.when(i == 0)
            def _():
                dw_ref[g] = dwp
                ds_ref[:, cs] = dsp

            @pl.when(i > 0)
            def _():
                dw_ref[g] += dwp
                ds_ref[:, cs] += dsp

    row = pl.BlockSpec((tr, 512), lambda i: (i, 0))
    return pl.pallas_call(
        body, name="pool_bwd", grid=(nb,),
        in_specs=[row, _prev_spec(tr, 512, 16, lambda i: 0), row, _next_spec(tr, 512, 16, lambda i: 0, T),
                  pl.BlockSpec((4, 128, 128), lambda i: (0, 0, 0)), pl.BlockSpec((1, 512), lambda i: (0, 0))],
        out_specs=[row, pl.BlockSpec((4, 128, 128), lambda i: (0, 0, 0)), pl.BlockSpec((1, 512), lambda i: (0, 0))],
        out_shape=[jax.ShapeDtypeStruct((T, 512), BF16), jax.ShapeDtypeStruct((4, 128, 128), F32),
                   jax.ShapeDtypeStruct((1, 512), F32)],
        compiler_params=_params("arbitrary"),
    )(proj, proj, dout, dout, pool_w, pool_scale)


def _split_dot(x, tri):
    hi = x.astype(BF16)
    lo = (x - hi.astype(F32)).astype(BF16)
    return (lax.dot_general(hi, tri, NN, preferred_element_type=F32)
            + lax.dot_general(lo, tri, NN, preferred_element_type=F32))


def _log1m(z):
    return -(jnp.maximum(z, 0.0) + jnp.log(1.0 + jnp.exp(-jnp.abs(z))))


def _sb_fwd(proj, gather=()):
    T = proj.shape[0]
    B, BK = min(SB_BLOCK, T), SB_KBLOCK
    R = B // BK
    nq, n = T // B, len(gather)
    scale = SB_HEAD_DIM ** -0.5

    def body(q_ref, k_ref, v_ref, *rest):
        o_ref, ls_ref = rest[n:n + 2]
        hp, i = pl.program_id(0), pl.program_id(1)
        if n:
            start, forward, finish = _gather_stages(rest[:n], rest[n + 2:2 * n + 2], *rest[2 * n + 2:])
            pl.when((hp == 0) & (i == 0))(start)
            pl.when((hp == 3) & (i == nq - 1))(forward)
        lane = _iota((1, 128), 1)
        tri_gt = (_iota((BK, BK), 0) > _iota((BK, BK), 1)).astype(BF16)
        row, col = _iota((B, BK), 0), _iota((B, BK), 1)
        qv = q_ref[...] * scale
        hms = [(lane >= 64 * h) & (lane < 64 * (h + 1)) for h in range(2)]
        qhs = [jnp.where(hm, qv, 0.0).astype(BF16) for hm in hms]

        def tile(j, carry, d):
            rows = pl.ds(pl.multiple_of(j * BK, BK), BK)
            kj = k_ref[rows, :].astype(BF16)
            vj = v_ref[rows, :].astype(BF16)
            r0 = 0 if d is None else BK * d
            valid = None if d is None else (col[r0:] < row[:B - r0])
            out = []
            for h in range(2):
                c, acc = carry[h]
                z = lax.dot_general(qhs[h][r0:], kj, NT, preferred_element_type=F32)
                lg = _log1m(z)
                if d is not None:
                    lg = jnp.where(valid, lg, 0.0)
                a = jnp.exp(z + lg + _split_dot(lg, tri_gt) + c[r0:])
                if d is not None:
                    a = jnp.where(valid, a, 0.0)
                upd = (c[r0:] + jnp.sum(lg, axis=1, keepdims=True),
                       acc[r0:] + lax.dot_general(a.astype(BF16), vj, NN, preferred_element_type=F32))
                out.append(upd if r0 == 0 else tuple(jnp.concatenate([old[:r0], new], axis=0)
                                                     for old, new in zip((c, acc), upd)))
            return tuple(out)

        zero = (jnp.zeros((B, 1), F32), jnp.zeros((B, 128), F32))
        carry = (zero, zero)
        for d in reversed(range(R)):
            carry = tile(i * R + d, carry, d)
        carry = lax.fori_loop(0, i * R, lambda s, cr: tile(i * R - 1 - s, cr, None), carry)
        o_ref[...] = jnp.where(hms[0], carry[0][1], carry[1][1])
        ls_ref[...] = jnp.where(hms[0], carry[0][0], carry[1][0])
        if n:
            pl.when((hp == 3) & (i == nq - 1))(finish)

    blk = pl.BlockSpec((B, 128), lambda hp, i: (i, hp))
    out = pl.pallas_call(
        body, name="sb_fwd", grid=(4, nq),
        in_specs=[pl.BlockSpec((B, 128), lambda hp, i: (i, 4 + hp)),
                  pl.BlockSpec((T, 128), lambda hp, i: (0, 8 + hp)),
                  pl.BlockSpec((T, 128), lambda hp, i: (0, 12 + hp))] + [HBM_SPEC] * n,
        out_specs=[blk, blk] + [HBM_SPEC] * n,
        out_shape=[jax.ShapeDtypeStruct((T, 512), F32)] * 2 + _gather_shapes(gather),
        scratch_shapes=_gather_sems(n) if n else [],
        compiler_params=_params("arbitrary", "arbitrary"),
    )(proj, proj, proj, *gather)
    return out[0], out[1], list(out[2:])


def _sb_bwd(proj, lsum, dout, exchange=()):
    T = proj.shape[0]
    B, BK = min(SB_BLOCK, T), SB_KBLOCK
    R = B // BK
    nq, n = T // B, len(exchange)
    scale = SB_HEAD_DIM ** -0.5

    def body(q_ref, k_ref, v_ref, do_ref, ls_ref, *rest):
        dq_ref, dk_ref, dv_ref = rest[n:n + 3]
        hp, i = pl.program_id(0), pl.program_id(1)
        if n:
            start, finish = _chips_stages(rest[:n], rest[n + 3:2 * n + 3], *rest[2 * n + 3:])
            pl.when((hp == 0) & (i == 0))(start)

        @pl.when(i == 0)
        def _():
            dk_ref[...] = jnp.zeros_like(dk_ref)
            dv_ref[...] = jnp.zeros_like(dv_ref)

        lane = _iota((1, 128), 1)
        tri_le = (_iota((BK, BK), 0) <= _iota((BK, BK), 1)).astype(BF16)
        tri_lt = (_iota((BK, BK), 0) < _iota((BK, BK), 1)).astype(BF16)
        row, col = _iota((B, BK), 0), _iota((B, BK), 1)
        qv = q_ref[...] * scale
        dov = do_ref[...]
        hms = [(lane >= 64 * h) & (lane < 64 * (h + 1)) for h in range(2)]
        qhs = [jnp.where(hm, qv, 0.0).astype(BF16) for hm in hms]
        dos = [jnp.where(hm, dov, 0.0).astype(BF16) for hm in hms]
        ltots = [ls_ref[:, 64 * h:64 * h + 1] for h in range(2)]

        def tile(j, carry, d):
            rows = pl.ds(pl.multiple_of(j * BK, BK), BK)
            kj = k_ref[rows, :].astype(BF16)
            vj = v_ref[rows, :].astype(BF16)
            diag = d is not None
            r0 = BK * d if diag else 0
            valid = (col[r0:] < row[:B - r0]) if diag else None
            out = []
            dkj = jnp.zeros((BK, 128), F32)
            dvj = jnp.zeros((BK, 128), F32)
            for h in range(2):
                lbef, ebef, dqa = carry[h]
                qh, do_h = qhs[h][r0:], dos[h][r0:]
                z = lax.dot_general(qh, kj, NT, preferred_element_type=F32)
                lg = _log1m(z)
                if diag:
                    lg = jnp.where(valid, lg, 0.0)
                a = jnp.exp(z + lg + (ltots[h][r0:] - lbef[r0:] - _split_dot(lg, tri_le)))
                if diag:
                    a = jnp.where(valid, a, 0.0)
                e = a * lax.dot_general(do_h, vj, NT, preferred_element_type=F32)
                dz = e * jnp.exp(lg) - jnp.exp(z + lg) * (ebef[r0:] + _split_dot(e, tri_lt))
                if diag:
                    dz = jnp.where(valid, dz, 0.0)
                dzb = dz.astype(BF16)
                dkj = dkj + lax.dot_general(dzb, qh, TN, preferred_element_type=F32)
                dvj = dvj + lax.dot_general(a.astype(BF16), do_h, TN, preferred_element_type=F32)
                upd = (lbef[r0:] + jnp.sum(lg, axis=1, keepdims=True), ebef[r0:] + jnp.sum(e, axis=1, keepdims=True),
                       dqa[r0:] + lax.dot_general(dzb, kj, NN, preferred_element_type=F32))
                out.append(upd if r0 == 0 else tuple(jnp.concatenate([old[:r0], new], axis=0)
                                                     for old, new in zip(carry[h], upd)))
            dk_ref[rows, :] += dkj
            dv_ref[rows, :] += dvj
            return tuple(out)

        zero = (jnp.zeros((B, 1), F32), jnp.zeros((B, 1), F32), jnp.zeros((B, 128), F32))
        carry = lax.fori_loop(0, i * R, lambda j, cr: tile(j, cr, None), (zero, zero))
        for d in range(R):
            carry = tile(i * R + d, carry, d)
        dq_ref[...] = jnp.where(hms[0], carry[0][2], carry[1][2]) * scale
        if n:
            pl.when((hp == 3) & (i == nq - 1))(finish)

    full = pl.BlockSpec((T, 128), lambda hp, i: (0, hp))
    blk = pl.BlockSpec((B, 128), lambda hp, i: (i, hp))
    out = pl.pallas_call(
        body, name="sb_bwd", grid=(4, nq),
        in_specs=[pl.BlockSpec((B, 128), lambda hp, i: (i, 4 + hp)),
                  pl.BlockSpec((T, 128), lambda hp, i: (0, 8 + hp)),
                  pl.BlockSpec((T, 128), lambda hp, i: (0, 12 + hp)),
                  pl.BlockSpec((B, 128), lambda hp, i: (i, 4 + hp)), blk] + [HBM_SPEC] * n,
        out_specs=[blk, full, full] + [HBM_SPEC] * n,
        out_shape=[jax.ShapeDtypeStruct((T, 512), F32)] * 3 + _chips_shapes(exchange),
        scratch_shapes=_chips_sems(n) if n else [],
        compiler_params=_params("arbitrary", "arbitrary"),
    )(proj, proj, proj, dout, lsum, *exchange)
    return out[0], out[1], out[2], list(out[3:])


def _sigmoid(x):
    return 1.0 / (1.0 + jnp.exp(-x))


def _silu_mul(cg, cv):
    return cg * _sigmoid(cg) * cv


def _ffn_act(up, conv_w, *, tr=512, cb=256):
    T, F2 = up.shape
    nc = F2 // 2 // cb
    K = FFN_CONV

    def body(g_ref, gh_ref, v_ref, vh_ref, wg_ref, wv_ref, o_ref):
        i = pl.program_id(0)
        gh = jnp.where(i > 0, gh_ref[...], 0.0)
        vh = jnp.where(i > 0, vh_ref[...], 0.0)
        cg = _conv_rows(jnp.concatenate([gh, g_ref[...]], axis=0), wg_ref, K, 8)
        cv = _conv_rows(jnp.concatenate([vh, v_ref[...]], axis=0), wv_ref, K, 8)
        o_ref[...] = _silu_mul(cg, cv).astype(BF16)

    return pl.pallas_call(
        body, name="ffn_act", grid=(T // tr, nc),
        in_specs=[pl.BlockSpec((tr, cb), lambda i, j: (i, j)), _prev_spec(tr, cb, 8, lambda i, j: j),
                  pl.BlockSpec((tr, cb), lambda i, j: (i, nc + j)), _prev_spec(tr, cb, 8, lambda i, j: nc + j),
                  pl.BlockSpec((K, cb), lambda i, j: (0, j)), pl.BlockSpec((K, cb), lambda i, j: (0, nc + j))],
        out_specs=pl.BlockSpec((tr, cb), lambda i, j: (i, j)),
        out_shape=jax.ShapeDtypeStruct((T, F2 // 2), BF16), compiler_params=_params("parallel", "parallel"),
    )(up, up, up, up, conv_w, conv_w)


def _conv_bwd_rows(dc_ext, x_ext, w_ref, K, tr, cs=slice(None)):
    n = tr + 8
    dx = w_ref[K - 1:K, cs] * dc_ext
    for i in range(K - 1):
        dx = dx + w_ref[i:i + 1, cs] * pltpu.roll(dc_ext, n - (K - 1 - i), 0)
    dc = dc_ext[:tr]
    dws = [jnp.sum(dc * pltpu.roll(x_ext, K - 1 - i, 0)[8:8 + tr], axis=0, keepdims=True) for i in range(K)]
    return dx[:tr], dws


def _acc_rows(ref, rows, first, cs=slice(None)):
    for i, r in enumerate(rows):
        @pl.when(first)
        def _():
            ref[i:i + 1, cs] = r

        @pl.when(jnp.logical_not(first))
        def _():
            ref[i:i + 1, cs] += r


def _ffn_act_bwd(up, conv_w, dact, *, tr=512, cb=256):
    T, F2 = up.shape
    F = F2 // 2
    nc, nb = F // cb, T // tr
    K = FFN_CONV

    def body(g_ref, gp_ref, gn_ref, v_ref, vp_ref, vn_ref, d_ref, dn_ref, wg_ref, wv_ref,
             dg_ref, dv_ref, dwg_ref, dwv_ref):
        i = pl.program_id(1)
        first, last = i == 0, i == nb - 1
        g_ext = jnp.concatenate([jnp.where(first, 0.0, gp_ref[...]), g_ref[...], jnp.where(last, 0.0, gn_ref[...])], axis=0)
        v_ext = jnp.concatenate([jnp.where(first, 0.0, vp_ref[...]), v_ref[...], jnp.where(last, 0.0, vn_ref[...])], axis=0)
        d_ext = jnp.concatenate([d_ref[...], jnp.where(last, 0.0, dn_ref[...])], axis=0)
        cg = _conv_rows(g_ext, wg_ref, K, 8)
        cv = _conv_rows(v_ext, wv_ref, K, 8)
        s = _sigmoid(cg)
        t = cg * s
        dcv = d_ext * t
        dcg = d_ext * cv * (s + t * (1.0 - s))
        dg, dwg = _conv_bwd_rows(dcg, g_ext, wg_ref, K, tr)
        dv, dwv = _conv_bwd_rows(dcv, v_ext, wv_ref, K, tr)
        dg_ref[...] = dg.astype(BF16)
        dv_ref[...] = dv.astype(BF16)
        _acc_rows(dwg_ref, dwg, first)
        _acc_rows(dwv_ref, dwv, first)

    blk = lambda off: pl.BlockSpec((tr, cb), lambda j, i: (i, off + j))
    prev = lambda off: pl.BlockSpec((8, cb), lambda j, i: (jnp.maximum(i * (tr // 8) - 1, 0), off + j))
    nxt = lambda off: pl.BlockSpec((8, cb), lambda j, i: (jnp.minimum((i + 1) * (tr // 8), T // 8 - 1), off + j))
    wsp = lambda off: pl.BlockSpec((K, cb), lambda j, i: (0, off + j))
    return pl.pallas_call(
        body, name="ffn_act_bwd", grid=(nc, nb),
        in_specs=[blk(0), prev(0), nxt(0), blk(nc), prev(nc), nxt(nc), blk(0), nxt(0), wsp(0), wsp(nc)],
        out_specs=[blk(0), blk(0), wsp(0), wsp(0)],
        out_shape=[jax.ShapeDtypeStruct((T, F), BF16)] * 2 + [jax.ShapeDtypeStruct((K, F), F32)] * 2,
        compiler_params=_params("parallel", "arbitrary"),
    )(up, up, up, up, up, up, dact, dact, conv_w, conv_w)


def _ple_fwd(hn, w_gate, p, w_ple, x, *, name, tm=1024, tn=512):
    T, Dm = x.shape
    tm = min(tm, T)

    def body(a_ref, b_ref, p_ref, wp_ref, x_ref, o_ref, gl_ref, pe_ref):
        gl = _dot(a_ref[...], b_ref[...], NN)
        pe = _dot(p_ref[...], wp_ref[...], NN)
        gl_ref[...] = gl
        pe_ref[...] = pe
        o_ref[...] = x_ref[...] + pe * _sigmoid(gl)

    o_spec = pl.BlockSpec((tm, tn), lambda i, j: (i, j))
    return pl.pallas_call(
        body, name=name, grid=(T // tm, Dm // tn),
        in_specs=[pl.BlockSpec((tm, Dm), lambda i, j: (i, 0)), pl.BlockSpec((Dm, tn), lambda i, j: (0, j)),
                  pl.BlockSpec((tm, PLE_DIM), lambda i, j: (i, 0)), pl.BlockSpec((PLE_DIM, tn), lambda i, j: (0, j)),
                  o_spec],
        out_specs=[o_spec] * 3, out_shape=[jax.ShapeDtypeStruct((T, Dm), F32)] * 3,
        compiler_params=_params("parallel", "parallel"),
    )(hn, w_gate, p, w_ple, x)


def _ple_bwd(dx, gl, pe, *, name, tr=512):
    T, Dm = dx.shape

    def body(dx_ref, gl_ref, pe_ref, dpe_ref, dgl_ref):
        g = _sigmoid(gl_ref[...])
        d = dx_ref[...]
        dpe_ref[...] = (d * g).astype(BF16)
        dgl_ref[...] = (d * pe_ref[...] * g * (1.0 - g)).astype(BF16)

    row = pl.BlockSpec((tr, Dm), lambda i: (i, 0))
    return pl.pallas_call(
        body, name=name, grid=(T // tr,), in_specs=[row] * 3, out_specs=[row] * 2,
        out_shape=[jax.ShapeDtypeStruct((T, Dm), BF16)] * 2, compiler_params=_params("parallel"),
    )(dx, gl, pe)


def _qkv_act(c, cb):
    s = c * _sigmoid(c)
    n = s * lax.rsqrt(jnp.sum(s * s, axis=-1, keepdims=True) + EPS)
    n = n * jnp.where(cb < GDN_HEADS, GDN_HEAD_DIM ** -0.5, 1.0)
    return jnp.where(cb < 2 * GDN_HEADS, n, s)


GDN_HPS = 4


def _gdn_pre(proj, conv_w, *, tr=512):
    T = proj.shape[0]
    K = GDN_CONV

    def body(x_ref, xh_ref, w_ref, o_ref):
        i, j = pl.program_id(0), pl.program_id(1)
        xh = jnp.where(i > 0, xh_ref[...], 0.0)
        for hh in range(GDN_HPS):
            cs = slice(128 * hh, 128 * (hh + 1))
            c = _conv_rows(jnp.concatenate([xh[:, cs], x_ref[:, cs]], axis=0), w_ref, K, 8, cs)
            o_ref[hh] = _qkv_act(c, GDN_HPS * j + hh)

    wide = 128 * GDN_HPS
    return pl.pallas_call(
        body, name="gdn_pre", grid=(T // tr, 24 // GDN_HPS),
        in_specs=[pl.BlockSpec((tr, wide), lambda i, j: (i, j)), _prev_spec(tr, wide, 8, lambda i, j: j),
                  pl.BlockSpec((K, wide), lambda i, j: (0, j))],
        out_specs=pl.BlockSpec((GDN_HPS, tr, 128), lambda i, j: (j, i, 0)),
        out_shape=jax.ShapeDtypeStruct((24, T, 128), F32), compiler_params=_params("parallel", "parallel"),
    )(proj, proj, conv_w)


def _gdn_pre_bwd(proj, conv_w, dqkv, *, tr=512):
    T = proj.shape[0]
    nb = T // tr
    K = GDN_CONV

    def body(x_ref, xp_ref, xn_ref, d_ref, dn_ref, w_ref, dx_ref, dw_ref):
        j, i = pl.program_id(0), pl.program_id(1)
        first, last = i == 0, i == nb - 1
        xp = jnp.where(first, 0.0, xp_ref[...])
        xn = jnp.where(last, 0.0, xn_ref[...])
        for hh in range(GDN_HPS):
            cs = slice(128 * hh, 128 * (hh + 1))
            x_ext = jnp.concatenate([xp[:, cs], x_ref[:, cs], xn[:, cs]], axis=0)
            d_ext = jnp.concatenate([d_ref[hh], jnp.where(last, 0.0, dn_ref[hh])], axis=0)
            c = _conv_rows(x_ext, w_ref, K, 8, cs)
            _, vjp = jax.vjp(lambda c_: _qkv_act(c_, GDN_HPS * j + hh), c)
            (dc,) = vjp(d_ext)
            dx, dws = _conv_bwd_rows(dc, x_ext, w_ref, K, tr, cs)
            dx_ref[:, cs] = dx.astype(BF16)
            _acc_rows(dw_ref, dws, first, cs)

    wide = 128 * GDN_HPS
    return pl.pallas_call(
        body, name="gdn_pre_bwd", grid=(24 // GDN_HPS, nb),
        in_specs=[pl.BlockSpec((tr, wide), lambda j, i: (i, j)),
                  pl.BlockSpec((8, wide), lambda j, i: (jnp.maximum(i * (tr // 8) - 1, 0), j)),
                  pl.BlockSpec((8, wide), lambda j, i: (jnp.minimum((i + 1) * (tr // 8), T // 8 - 1), j)),
                  pl.BlockSpec((GDN_HPS, tr, 128), lambda j, i: (j, i, 0)),
                  pl.BlockSpec((GDN_HPS, 8, 128), lambda j, i: (j, jnp.minimum((i + 1) * (tr // 8), T // 8 - 1), 0)),
                  pl.BlockSpec((K, wide), lambda j, i: (0, j))],
        out_specs=[pl.BlockSpec((tr, wide), lambda j, i: (i, j)), pl.BlockSpec((K, wide), lambda j, i: (0, j))],
        out_shape=[jax.ShapeDtypeStruct((T, 24 * 128), BF16), jax.ShapeDtypeStruct((K, 24 * 128), F32)],
        compiler_params=_params("parallel", "arbitrary"),
    )(proj, proj, proj, dqkv, dqkv, conv_w)


def _gate_fn(ba, alog_row, dt_row):
    lane = _iota((1, 128), 1)
    x = ba + dt_row
    sp = jnp.maximum(x, 0.0) + jnp.log(1.0 + jnp.exp(-jnp.abs(x)))
    return jnp.where(lane < GDN_HEADS, _sigmoid(ba), -jnp.exp(alog_row) * sp)


def _gdn_gate(ba, alog_row, dt_row, *, tr=512):
    T = ba.shape[0]

    def body(ba_ref, al_ref, dt_ref, b_ref, g_ref):
        val = _gate_fn(ba_ref[...], al_ref[...], dt_ref[...])
        for h in range(GDN_HEADS):
            b_ref[h] = val[:, h:h + 1]
            g_ref[h] = val[:, GDN_HEADS + h:GDN_HEADS + h + 1]

    vec = pl.BlockSpec((1, 128), lambda i: (0, 0))
    hm = pl.BlockSpec((GDN_HEADS, tr, 1), lambda i: (0, i, 0))
    return pl.pallas_call(
        body, name="gdn_gate", grid=(T // tr,), in_specs=[pl.BlockSpec((tr, 128), lambda i: (i, 0)), vec, vec],
        out_specs=[hm, hm], out_shape=[jax.ShapeDtypeStruct((GDN_HEADS, T, 1), F32)] * 2,
        compiler_params=_params("parallel"),
    )(ba, alog_row, dt_row)


def _gdn_gate_bwd(ba, alog_row, dt_row, dbeta, dg, *, tr=512):
    T = ba.shape[0]

    def body(ba_ref, al_ref, dt_ref, db_ref, dg_ref, dba_ref, dal_ref, ddt_ref):
        i = pl.program_id(0)
        lane = _iota((1, 128), 1)
        d = jnp.zeros((tr, 128), F32)
        for h in range(GDN_HEADS):
            d = d + jnp.where(lane == h, db_ref[h], 0.0) + jnp.where(lane == GDN_HEADS + h, dg_ref[h], 0.0)
        _, vjp = jax.vjp(_gate_fn, ba_ref[...], al_ref[...], dt_ref[...])
        dba, dal, ddt = vjp(d)
        dba_ref[...] = dba.astype(BF16)

        @pl.when(i == 0)
        def _():
            dal_ref[...] = dal
            ddt_ref[...] = ddt

        @pl.when(i > 0)
        def _():
            dal_ref[...] += dal
            ddt_ref[...] += ddt

    vec = pl.BlockSpec((1, 128), lambda i: (0, 0))
    hm = pl.BlockSpec((GDN_HEADS, tr, 1), lambda i: (0, i, 0))
    row = pl.BlockSpec((tr, 128), lambda i: (i, 0))
    return pl.pallas_call(
        body, name="gdn_gate_bwd", grid=(T // tr,), in_specs=[row, vec, vec, hm, hm], out_specs=[row, vec, vec],
        out_shape=[jax.ShapeDtypeStruct((T, 128), BF16), jax.ShapeDtypeStruct((1, 128), F32),
                   jax.ShapeDtypeStruct((1, 128), F32)],
        compiler_params=_params("arbitrary"),
    )(ba, alog_row, dt_row, dbeta, dg)


def _split3(x):
    x1 = x.astype(BF16)
    r = x - x1.astype(F32)
    x2 = r.astype(BF16)
    return x1, x2, (r - x2.astype(F32)).astype(BF16)


def _dot01(tri, x, dims):
    t = tri.astype(BF16)
    x1, x2, x3 = _split3(x)
    d = lambda xi: lax.dot_general(t, xi, dims, preferred_element_type=F32)
    return d(x1) + (d(x2) + d(x3))


def _dot3(a, b, dims):
    ah, al, _ = _split3(a)
    bh, bl, _ = _split3(b)
    d = lambda p, q: lax.dot_general(p, q, dims, preferred_element_type=F32)
    return d(ah, bh) + (d(ah, bl) + d(al, bh))


BNN = (((2,), (1,)), ((0,), (0,)))
BNT = (((2,), (2,)), ((0,), (0,)))
BTN = (((1,), (1,)), ((0,), (0,)))


@jax.custom_vjp
def _mm01(tri, x):
    return _dot01(tri, x, BNN)


def _mm01_fwd(tri, x):
    return _dot01(tri, x, BNN), tri


def _mm01_bwd(tri, ct):
    return jnp.zeros_like(tri), _dot01(tri, ct, BTN)


_mm01.defvjp(_mm01_fwd, _mm01_bwd)


def _unit_lower_inverse(a):
    C = a.shape[-1]
    eye = (_iota(a.shape, 1) == _iota(a.shape, 2)).astype(F32)
    pw = -a
    tinv = eye + pw
    for _ in range(5):
        pw = _dot3(pw, pw, BNN)
        tinv = tinv + _dot3(tinv, pw, BNN)
    return tinv


@jax.custom_vjp
def _unit_lower_solve(a, rv, rw):
    return _unit_lower_solve_fwd(a, rv, rw)[0]


def _unit_lower_solve_fwd(a, rv, rw):
    tinv = _unit_lower_inverse(a)
    sol = _dot3(tinv, jnp.concatenate([rv, rw], axis=2), BNN)
    n = rv.shape[2]
    return (sol[:, :, :n], sol[:, :, n:]), (tinv, sol)


def _unit_lower_solve_bwd(res, cts):
    tinv, sol = res
    n = cts[0].shape[2]
    d_rhs = _dot3(tinv, jnp.concatenate(cts, axis=2), BTN)
    return -_dot3(d_rhs, sol, BNT), d_rhs[:, :, :n], d_rhs[:, :, n:]


_unit_lower_solve.defvjp(_unit_lower_solve_fwd, _unit_lower_solve_bwd)


@jax.custom_vjp
def _mmb_nt(a, b):
    return _dot(a, b, BNT)


def _mmb_nt_fwd(a, b):
    return _dot(a, b, BNT), (a, b)


def _mmb_nt_bwd(res, ct):
    a, b = res
    return _dot(ct, b, BNN), _dot(ct, a, BTN)


_mmb_nt.defvjp(_mmb_nt_fwd, _mmb_nt_bwd)


def _gdn_chunk(q, k, v, gcol, bcol):
    nb, C = q.shape[0], GDN_CHUNK
    row, col = _iota((nb, C, C), 1), _iota((nb, C, C), 2)
    incl, strict = row >= col, row > col
    eye = (row == col).astype(F32)
    lower = incl.astype(F32)
    ones = jnp.ones((nb, C, C), F32)
    gwide = jnp.broadcast_to(gcol, (nb, C, GDN_HEAD_DIM))
    gc = _mm01(lower, gwide)
    gtot = _mm01(ones, gwide)
    gc_c = _mm01(lower, jnp.broadcast_to(gcol, (nb, C, C)))
    gc_s = _mm01(ones, gc_c * eye)
    decay = jnp.where(incl, jnp.exp(jnp.where(incl, gc_c - gc_s, 0.0)), 0.0)
    kb = k * bcol
    a = jnp.where(strict, _mmb_nt(kb, k) * decay, 0.0)
    egc = jnp.exp(gc)
    u, w = _unit_lower_solve(a, v * bcol, kb * egc)
    qk = jnp.where(incl, _mmb_nt(q, k) * decay, 0.0)
    return u, w, qk, q * egc, k * jnp.exp(gtot - gc), jnp.exp(jnp.sum(gwide, axis=1))


GDN_ROWS = 8 * GDN_CHUNK


GDN_LOCAL_CHUNKS = 32


def _gdn_specs(T):
    nch = min(GDN_LOCAL_CHUNKS, T // GDN_CHUNK)
    L = nch * GDN_CHUNK
    hd = lambda off: pl.BlockSpec((1, L, 128), lambda h, i: (off + h, i, 0))
    col = pl.BlockSpec((1, L, 1), lambda h, i: (h, i, 0))
    sq = pl.BlockSpec((1, L, GDN_CHUNK), lambda h, i: (h, i, 0))
    gl = pl.BlockSpec((1, nch, 128), lambda h, i: (h, i, 0))
    return nch, hd, col, sq, gl


def _gdn_local(qkv, g, beta):
    T = qkv.shape[1]
    nch, hd, col, sq, gl_spec = _gdn_specs(T)

    def body(q_ref, k_ref, v_ref, g_ref, b_ref, u_ref, w_ref, qk_ref, qd_ref, kd_ref, gl_ref):
        chunks = lambda ref: ref[0].reshape(nch, GDN_CHUNK, ref.shape[2])
        rows = lambda val: val.reshape(nch * GDN_CHUNK, val.shape[2])
        u, w, qk, qd, kd, gl = _gdn_chunk(chunks(q_ref), chunks(k_ref), chunks(v_ref), chunks(g_ref), chunks(b_ref))
        u_ref[0] = rows(u)
        w_ref[0] = rows(w).astype(BF16)
        qk_ref[0] = rows(qk).astype(BF16)
        qd_ref[0] = rows(qd).astype(BF16)
        kd_ref[0] = rows(kd).astype(BF16)
        gl_ref[0] = gl

    H = GDN_HEADS
    return pl.pallas_call(
        body, name="gdn_local", grid=(H, T // (nch * GDN_CHUNK)),
        in_specs=[hd(0), hd(H), hd(2 * H), col, col],
        out_specs=[hd(0), hd(0), sq, hd(0), hd(0), gl_spec],
        out_shape=[jax.ShapeDtypeStruct((H, T, 128), F32), jax.ShapeDtypeStruct((H, T, 128), BF16),
                   jax.ShapeDtypeStruct((H, T, GDN_CHUNK), BF16), jax.ShapeDtypeStruct((H, T, 128), BF16),
                   jax.ShapeDtypeStruct((H, T, 128), BF16), jax.ShapeDtypeStruct((H, T // GDN_CHUNK, 128), F32)],
        compiler_params=_params("parallel", "parallel"),
    )(qkv, qkv, qkv, g, beta)


def _gdn_local_bwd(qkv, g, beta, du, dw, dqk, dqd, dkd, dgl):
    T = qkv.shape[1]
    nch, hd, col, sq, gl_spec = _gdn_specs(T)

    def body(q_ref, k_ref, v_ref, g_ref, b_ref, du_ref, dw_ref, dqk_ref, dqd_ref, dkd_ref, dgl_ref,
             dqkv_ref, dg_ref, db_ref):
        chunks = lambda ref: ref[0].reshape(nch, GDN_CHUNK, ref.shape[2])
        rows = lambda val: val.reshape(nch * GDN_CHUNK, val.shape[2])
        _, vjp = jax.vjp(_gdn_chunk, chunks(q_ref), chunks(k_ref), chunks(v_ref), chunks(g_ref), chunks(b_ref))
        dq, dk, dv, dg, db = vjp((chunks(du_ref), chunks(dw_ref), chunks(dqk_ref), chunks(dqd_ref), chunks(dkd_ref),
                                  dgl_ref[0]))
        dqkv_ref[0, 0] = rows(dq)
        dqkv_ref[1, 0] = rows(dk)
        dqkv_ref[2, 0] = rows(dv)
        dg_ref[0] = rows(dg)
        db_ref[0] = rows(db)

    H = GDN_HEADS
    small = jax.ShapeDtypeStruct((H, T, 1), F32)
    dqkv, dg, db = pl.pallas_call(
        body, name="gdn_local_bwd", grid=(H, T // (nch * GDN_CHUNK)),
        in_specs=[hd(0), hd(H), hd(2 * H), col, col, hd(0), hd(0), sq, hd(0), hd(0), gl_spec],
        out_specs=[pl.BlockSpec((3, 1, nch * GDN_CHUNK, 128), lambda h, i: (0, h, i, 0)), col, col],
        out_shape=[jax.ShapeDtypeStruct((3, H, T, 128), F32), small, small],
        compiler_params=_params("parallel", "parallel"),
    )(qkv, qkv, qkv, g, beta, du, dw, dqk, dqd, dkd, dgl)
    return dqkv.reshape(3 * H, T, 128), dg, db


GDN_HB = 8


def _gdn_scan_specs(T, rev):
    nb = T // GDN_ROWS
    blk = (lambda i: nb - 1 - i) if rev else (lambda i: i)
    hd = pl.BlockSpec((GDN_HB, GDN_ROWS, 128), lambda h, i: (h, blk(i), 0))
    sq = pl.BlockSpec((GDN_HB, GDN_ROWS, GDN_CHUNK), lambda h, i: (h, blk(i), 0))
    gl = pl.BlockSpec((GDN_HB, 8, 128), lambda h, i: (h, blk(i), 0))
    st = pl.BlockSpec((GDN_HB, 8, 128, 128), lambda h, i: (h, blk(i), 0, 0))
    return hd, sq, gl, st


def _gdn_scan(u, w, qk, qd, kd, gl):
    H, T, _ = u.shape
    hd, sq, gl_spec, st = _gdn_scan_specs(T, False)

    def body(u_ref, w_ref, qk_ref, qd_ref, kd_ref, gl_ref, o_ref, ss_ref, vn_ref, s_scr):
        @pl.when(pl.program_id(1) == 0)
        def _():
            s_scr[...] = jnp.zeros_like(s_scr)

        dot = lambda a, b, dims: lax.dot_general(a, b, dims, preferred_element_type=F32)
        s = s_scr[...]
        for c in range(8):
            rs = slice(GDN_CHUNK * c, GDN_CHUNK * (c + 1))
            ss_ref[:, c] = s
            sb = s.astype(BF16)
            vn = u_ref[:, rs, :] - dot(w_ref[:, rs, :], sb, BNN)
            vnb = vn.astype(BF16)
            o_ref[:, rs, :] = dot(qd_ref[:, rs, :], sb, BNN) + dot(qk_ref[:, rs, :], vnb, BNN)
            vn_ref[:, rs, :] = vnb
            s = s * gl_ref[:, c:c + 1, :] + dot(kd_ref[:, rs, :], vnb, BTN)
        s_scr[...] = s

    return pl.pallas_call(
        body, name="gdn_scan", grid=(H // GDN_HB, T // GDN_ROWS),
        in_specs=[hd, hd, sq, hd, hd, gl_spec], out_specs=[hd, st, hd],
        out_shape=[jax.ShapeDtypeStruct((H, T, 128), F32), jax.ShapeDtypeStruct((H, T // GDN_CHUNK, 128, 128), F32),
                   jax.ShapeDtypeStruct((H, T, 128), BF16)],
        scratch_shapes=[pltpu.VMEM((GDN_HB, 128, 128), F32)],
        compiler_params=_params("parallel", "arbitrary"),
    )(u, w, qk, qd, kd, gl)


def _gdn_scan_bwd(do, ss, vn, w, qk, qd, kd, gl):
    H, T, _ = do.shape
    hd, sq, gl_spec, st = _gdn_scan_specs(T, True)

    def body(do_ref, ss_ref, vn_ref, w_ref, qk_ref, qd_ref, kd_ref, gl_ref,
             du_ref, dw_ref, dqk_ref, dqd_ref, dkd_ref, dgl_ref, ds_scr):
        @pl.when(pl.program_id(1) == 0)
        def _():
            ds_scr[...] = jnp.zeros_like(ds_scr)

        dot = lambda a, b, dims: lax.dot_general(a, b, dims, preferred_element_type=F32)
        ds = ds_scr[...]
        for c in reversed(range(8)):
            rs = slice(GDN_CHUNK * c, GDN_CHUNK * (c + 1))
            s = ss_ref[:, c]
            sb, dsb = s.astype(BF16), ds.astype(BF16)
            dob = do_ref[:, rs, :].astype(BF16)
            vnb = vn_ref[:, rs, :]
            dvn = dot(qk_ref[:, rs, :], dob, BTN) + dot(kd_ref[:, rs, :], dsb, BNN)
            dvnb = dvn.astype(BF16)
            du_ref[:, rs, :] = dvn
            dw_ref[:, rs, :] = -dot(dvnb, sb, BNT)
            dqk_ref[:, rs, :] = dot(dob, vnb, BNT)
            dqd_ref[:, rs, :] = dot(dob, sb, BNT)
            dkd_ref[:, rs, :] = dot(vnb, dsb, BNT)
            dgl_ref[:, c:c + 1, :] = jnp.sum(ds * s, axis=1, keepdims=True)
            ds = dot(qd_ref[:, rs, :], dob, BTN) + ds * gl_ref[:, c:c + 1, :] - dot(w_ref[:, rs, :], dvnb, BTN)
        ds_scr[...] = ds

    big = jax.ShapeDtypeStruct((H, T, 128), F32)
    return pl.pallas_call(
        body, name="gdn_scan_bwd", grid=(H // GDN_HB, T // GDN_ROWS),
        in_specs=[hd, st, hd, hd, sq, hd, hd, gl_spec], out_specs=[hd, hd, sq, hd, hd, gl_spec],
        out_shape=[big, big, jax.ShapeDtypeStruct((H, T, GDN_CHUNK), F32), big, big,
                   jax.ShapeDtypeStruct((H, T // GDN_CHUNK, 128), F32)],
        scratch_shapes=[pltpu.VMEM((GDN_HB, 128, 128), F32)],
        compiler_params=_params("parallel", "arbitrary"),
    )(do, ss, vn, w, qk, qd, kd, gl)


def _gated_norm(o, z, nw):
    on = o * lax.rsqrt(jnp.mean(o * o, axis=-1, keepdims=True) + EPS) * nw
    return on * (z * _sigmoid(z))


def _gdn_post(o, proj, norm_w, *, tr=512):
    T = proj.shape[0]

    def body(o_ref, z_ref, n_ref, y_ref):
        y_ref[...] = _gated_norm(o_ref[0], z_ref[...], n_ref[...]).astype(BF16)

    return pl.pallas_call(
        body, name="gdn_post", grid=(T // tr, GDN_HEADS),
        in_specs=[pl.BlockSpec((1, tr, 128), lambda i, h: (h, i, 0)), pl.BlockSpec((tr, 128), lambda i, h: (i, 24 + h)),
                  pl.BlockSpec((1, 128), lambda i, h: (0, 0))],
        out_specs=pl.BlockSpec((tr, 128), lambda i, h: (i, h)),
        out_shape=jax.ShapeDtypeStruct((T, 1024), BF16), compiler_params=_params("parallel", "parallel"),
    )(o, proj, norm_w)


def _gdn_post_bwd(o, proj, norm_w, dy, *, tr=512):
    T = proj.shape[0]

    def body(o_ref, z_ref, n_ref, dy_ref, do_ref, dz_ref, dn_ref):
        first = (pl.program_id(0) == 0) & (pl.program_id(1) == 0)
        _, vjp = jax.vjp(_gated_norm, o_ref[0], z_ref[...], n_ref[...])
        do, dz, dn = vjp(dy_ref[...])
        do_ref[0] = do
        dz_ref[...] = dz.astype(BF16)

        @pl.when(first)
        def _():
            dn_ref[...] = dn

        @pl.when(jnp.logical_not(first))
        def _():
            dn_ref[...] += dn

    blk = pl.BlockSpec((tr, 128), lambda i, h: (i, h))
    hm = pl.BlockSpec((1, tr, 128), lambda i, h: (h, i, 0))
    vec = pl.BlockSpec((1, 128), lambda i, h: (0, 0))
    return pl.pallas_call(
        body, name="gdn_post_bwd", grid=(T // tr, GDN_HEADS),
        in_specs=[hm, pl.BlockSpec((tr, 128), lambda i, h: (i, 24 + h)), vec, blk], out_specs=[hm, blk, vec],
        out_shape=[jax.ShapeDtypeStruct((GDN_HEADS, T, 128), F32), jax.ShapeDtypeStruct((T, 1024), BF16),
                   jax.ShapeDtypeStruct((1, 128), F32)],
        compiler_params=_params("arbitrary", "arbitrary"),
    )(o, proj, norm_w, dy)


HBM_SPEC = pl.BlockSpec(memory_space=pltpu.HBM)


def _place():
    return lax.axis_index("x"), lax.axis_index("y"), lax.axis_index("c")


def _all_gather(vs, *, name):
    n = len(vs)

    def body(*refs):
        start, forward, finish = _gather_stages(refs[:n], refs[n:2 * n], *refs[2 * n:])
        start()
        forward()
        finish()

    return pl.pallas_call(
        body, name=name, out_shape=_gather_shapes(vs), in_specs=[HBM_SPEC] * n, out_specs=[HBM_SPEC] * n,
        scratch_shapes=_gather_sems(n),
    )(*vs)


def _gather_shapes(vs):
    return [jax.ShapeDtypeStruct((N_DEV,) + v.shape, v.dtype) for v in vs]


def _gather_sems(n):
    return [pltpu.SemaphoreType.DMA((7 * n,)), pltpu.SemaphoreType.DMA((7 * n,)), pltpu.SemaphoreType.DMA((n,))]


def _gather_stages(v_refs, out_refs, send_sems, recv_sems, local_sems):
    n = len(v_refs)
    x, y, c = _place()
    me, sibling = (x, y, c), (x, y, 1 - c)
    chips = [(1 - x, y), (x, 1 - y), (1 - x, 1 - y)]

    def copy(a, k, block, to, from_input=False):
        slot = out_refs[a].at[4 * block[0] + 2 * block[1] + block[2]]
        return pltpu.make_async_remote_copy(
            src_ref=v_refs[a] if from_input else slot, dst_ref=slot,
            send_sem=send_sems.at[7 * a + k], recv_sem=recv_sems.at[7 * a + k], device_id=to, device_id_type=MESH)

    def mine():
        return [pltpu.make_async_copy(v_refs[a], out_refs[a].at[4 * x + 2 * y + c], local_sems.at[a]) for a in range(n)]

    def first():
        return ([copy(a, 0, me, sibling, True) for a in range(n)]
                + [copy(a, 1 + j, me, (*chip, c), True) for j, chip in enumerate(chips) for a in range(n)])

    def passed():
        return [copy(a, 4 + j, (*chip, c), sibling) for j, chip in enumerate(chips) for a in range(n)]

    def start():
        for cp in mine() + first():
            cp.start()

    def forward():
        for j, chip in enumerate(chips):
            for a in range(n):
                copy(a, 1 + j, (*chip, c), me).wait_recv()
                copy(a, 4 + j, (*chip, c), sibling).start()

    def finish():
        for a in range(n):
            copy(a, 0, sibling, me).wait_recv()
            for j, chip in enumerate(chips):
                copy(a, 4 + j, (*chip, 1 - c), me).wait_recv()
        for cp in first() + passed():
            cp.wait_send()
        for cp in mine():
            cp.wait()

    return start, forward, finish


def _exchange_sibling(gs, *, name):
    n = len(gs)

    def body(*refs):
        g_refs, out_refs = refs[:n], refs[n:2 * n]
        send_sems, recv_sems = refs[2 * n:]
        x, y, c = _place()
        copies = [pltpu.make_async_remote_copy(
            src_ref=g_refs[a].at[k, 1 - c], dst_ref=out_refs[a].at[k], send_sem=send_sems.at[4 * a + k],
            recv_sem=recv_sems.at[4 * a + k], device_id=(x, y, 1 - c), device_id_type=MESH)
            for a in range(n) for k in range(4)]
        for cp in copies:
            cp.start()
        for cp in copies:
            cp.wait()

    return pl.pallas_call(
        body, name=name, out_shape=[jax.ShapeDtypeStruct((4,) + g.shape[2:], g.dtype) for g in gs],
        in_specs=[HBM_SPEC] * n, out_specs=[HBM_SPEC] * n,
        scratch_shapes=[pltpu.SemaphoreType.DMA((4 * n,)), pltpu.SemaphoreType.DMA((4 * n,))],
    )(*gs)


def _exchange_chips(pcs):
    n = len(pcs)

    def body(*refs):
        start, finish = _chips_stages(refs[:n], refs[n:2 * n], *refs[2 * n:])
        start()
        finish()

    return pl.pallas_call(
        body, name="rs_chips", out_shape=_chips_shapes(pcs), in_specs=[HBM_SPEC] * n, out_specs=[HBM_SPEC] * n,
        scratch_shapes=_chips_sems(n),
    )(*pcs)


def _chips_shapes(pcs):
    return [jax.ShapeDtypeStruct((3,) + pc.shape[1:], pc.dtype) for pc in pcs]


def _chips_sems(n):
    return [pltpu.SemaphoreType.DMA((3 * n,)), pltpu.SemaphoreType.DMA((3 * n,))]


def _chips_stages(p_refs, out_refs, send_sems, recv_sems):
    n = len(p_refs)
    x, y, c = _place()
    chips = [(1 - x, y), (x, 1 - y), (1 - x, 1 - y)]

    def copies():
        return [pltpu.make_async_remote_copy(
            src_ref=p_refs[a].at[2 * cx + cy], dst_ref=out_refs[a].at[j], send_sem=send_sems.at[3 * a + j],
            recv_sem=recv_sems.at[3 * a + j], device_id=(cx, cy, c), device_id_type=MESH)
            for j, (cx, cy) in enumerate(chips) for a in range(n)]

    def start():
        for cp in copies():
            cp.start()

    def finish():
        for cp in copies():
            cp.wait()

    return start, finish


def _chip_partial(place, g, got, *, tr, name):
    R, W = g.shape[2:]

    def body(pl_ref, g_ref, r_ref, o_ref):
        o_ref[...] = (g_ref[0] + r_ref[...]).astype(BF16)

    return pl.pallas_call(
        body, name=name, out_shape=jax.ShapeDtypeStruct((4, R, W), BF16),
        grid_spec=pltpu.PrefetchScalarGridSpec(
            num_scalar_prefetch=1, grid=(4, R // tr),
            in_specs=[pl.BlockSpec((1, 1, tr, W), lambda k, i, pr: (k, pr[2], i, 0)),
                      pl.BlockSpec((1, tr, W), lambda k, i, pr: (k, i, 0))],
            out_specs=pl.BlockSpec((1, tr, W), lambda k, i, pr: (k, i, 0))),
        compiler_params=_params("parallel", "parallel"),
    )(place, g, got)


def _adamw_math(g, w, m, v):
    m = ADAM_B1 * m + (1.0 - ADAM_B1) * g
    v = ADAM_B2 * v + (1.0 - ADAM_B2) * (g * g)
    m_hat = m / (1.0 - ADAM_B1 ** ADAM_STEP)
    v_hat = v / (1.0 - ADAM_B2 ** ADAM_STEP)
    return -ADAM_LR * (m_hat / (jnp.sqrt(v_hat) + ADAM_EPS) + ADAM_WD * w), m, v


def _adamw_shard(place, g, got1, got2, w, m, v, *, tr, name):
    R, W = w.shape

    def body(pl_ref, g_ref, r1_ref, r2_ref, w_ref, m_ref, v_ref, go_ref, d_ref, mo_ref, vo_ref):
        gs = g_ref[0, 0] + r1_ref[0]
        for j in range(3):
            gs = gs + r2_ref[j].astype(F32)
        go_ref[...] = gs
        d_ref[...], mo_ref[...], vo_ref[...] = _adamw_math(gs, w_ref[...], m_ref[...], v_ref[...])

    row = pl.BlockSpec((tr, W), lambda i, pr: (i, 0))
    out = jax.ShapeDtypeStruct((R, W), F32)
    return pl.pallas_call(
        body, name=name, out_shape=[out] * 4,
        grid_spec=pltpu.PrefetchScalarGridSpec(
            num_scalar_prefetch=1, grid=(R // tr,),
            in_specs=[pl.BlockSpec((1, 1, tr, W), lambda i, pr: (2 * pr[0] + pr[1], pr[2], i, 0)),
                      pl.BlockSpec((1, tr, W), lambda i, pr: (2 * pr[0] + pr[1], i, 0)),
                      pl.BlockSpec((3, tr, W), lambda i, pr: (0, i, 0)), row, row, row],
            out_specs=[row] * 4),
        compiler_params=_params("parallel"),
    )(place, g, got1, got2, w, m, v)


def _adamw_replicated(parts, w, m, v):
    R, W = w.shape

    def body(p_ref, w_ref, m_ref, v_ref, go_ref, d_ref, mo_ref, vo_ref):
        gs = p_ref[0]
        for j in range(1, N_DEV):
            gs = gs + p_ref[j]
        go_ref[...] = gs
        d_ref[...], mo_ref[...], vo_ref[...] = _adamw_math(gs, w_ref[...], m_ref[...], v_ref[...])

    full = pl.BlockSpec((R, W), lambda i: (0, 0))
    out = jax.ShapeDtypeStruct((R, W), F32)
    return pl.pallas_call(
        body, name="adamw_replicated", grid=(1,), out_shape=[out] * 4,
        in_specs=[pl.BlockSpec((N_DEV, R, W), lambda i: (0, 0, 0)), full, full, full], out_specs=[full] * 4,
        compiler_params=_params("arbitrary"),
    )(parts, w, m, v)


GROUPS = {
    "in_e": (256, 512, ((("w_in_e", None), 1024, 1024),)),
    "in_o": (514, 512, ((("w_in_o", None), 1024, 1024),)),
    "up0": (704, 256, ((("w_up", 0), 1024, 1024),)),
    "up1": (704, 256, ((("w_up", 1), 1024, 1024),)),
    "down0": (1024, 176, ((("w_down", 0), 352, 352),)),
    "down1": (1024, 176, ((("w_down", 1), 352, 352),)),
    "square": (1024, 256, ((("w_out_e", None), 128, 128), (("w_out_o", None), 128, 128), (("w_ple_gate", None), 256, 256))),
    "ple": (128, 512, ((("w_ple", None), 512, 512),)),
    "conv_f": (704, 8, ((("ffn_conv", None), 6, 8),)),
    "norm_o": (128, 8, ((("mix_norm_o", None), 1, 8),)),
    "conv_o": (384, 8, ((("conv_qkv_o", None), 4, 8),)),
}
SHARDED = tuple(dict.fromkeys(p[0][0] for g in GROUPS.values() for p in g[2]))
COLUMN_SHARDED = ("w_in_e", "w_in_o", "w_up", "ffn_conv", "w_ple", "conv_qkv_o", "mix_norm_o")
PACK_W = 1024
REPL_LAYOUT = (
    ("mix_norm_e", (1, 1024), 8), ("pool_w", (1, 4, 128, 128), 64), ("pool_scale", (1, 512), 8),
    ("a_log_o", (1, 8), 8), ("dt_bias_o", (1, 8), 8), ("gdn_norm_o", (1, 128), 8),
    ("ffn_norm", (2, 1024), 8), ("ple_norm", (2, 1024), 8), ("final_norm", (1024,), 8),
)


def _pad_rows(a, rows):
    extra = rows - a.shape[-2]
    return a if extra == 0 else jnp.pad(a, [(0, 0)] * (a.ndim - 2) + [(0, extra), (0, 0)])


def _group_rows(pieces, gname):
    parts = [_pad_rows(pieces[name], padded) for name, _, padded in GROUPS[gname][2]]
    return parts[0] if len(parts) == 1 else jnp.concatenate(parts, axis=-2)


def _ungroup_rows(buf, gname):
    out, r0 = {}, 0
    for name, rows, padded in GROUPS[gname][2]:
        out[name] = buf[..., r0:r0 + rows, :]
        r0 += padded
    return out


def _shard_major(name, gfull, n_layers):
    per_layer = []
    for g in gfull:
        if g.ndim == 3:
            per_layer.append(g)
        elif name in COLUMN_SHARDED:
            k = g.shape[0]
            per_layer.append(jnp.moveaxis(g.reshape(k, N_DEV, g.shape[1] // N_DEV), 1, 0))
        else:
            per_layer.append(g.reshape(N_DEV, g.shape[0] // N_DEV, -1))
    return per_layer[0] if n_layers == 1 else jnp.concatenate(per_layer, axis=1)


def _natural(name, gathered, n_layers):
    rows = gathered.shape[1] // n_layers
    out = []
    for layer in range(n_layers):
        piece = gathered[:, layer * rows:(layer + 1) * rows]
        if name in COLUMN_SHARDED:
            out.append(jnp.moveaxis(piece, 0, 1).reshape(rows, N_DEV * piece.shape[2]))
        else:
            out.append(piece.reshape(N_DEV * rows, piece.shape[2]))
    return out


def _unshard_pairs(g, layer, rows, tag):
    n = g.shape[2]

    def body(g_ref, o_ref):
        o_ref[...] = jnp.concatenate([g_ref[0], g_ref[1]], axis=1)

    return pl.pallas_call(
        body, name="unshard_%s%d" % (tag, layer), grid=(N_DEV // 2,),
        in_specs=[pl.BlockSpec((2, rows, n), lambda p: (p, layer, 0))],
        out_specs=pl.BlockSpec((rows, 2 * n), lambda p: (0, p)),
        out_shape=jax.ShapeDtypeStruct((rows, N_DEV * n), g.dtype), compiler_params=_params("parallel"),
    )(g)


def _unshard_in_o(g, *, tr=256):
    rows, n = g.shape[1:]
    main = (N_DEV * n) // 128 * 128

    def body(g_ref, m_ref, b_ref):
        cat = jnp.concatenate([g_ref[i] for i in range(N_DEV)], axis=1)
        m_ref[...] = cat[:, :main]
        b_ref[...] = jnp.concatenate([cat[:, main:], jnp.zeros((tr, 128 - (N_DEV * n - main)), g.dtype)], axis=1)

    return pl.pallas_call(
        body, name="unshard_in_o", grid=(rows // tr,),
        in_specs=[pl.BlockSpec((N_DEV, tr, n), lambda p: (0, p, 0))],
        out_specs=[pl.BlockSpec((tr, main), lambda p: (p, 0)), pl.BlockSpec((tr, 128), lambda p: (p, 0))],
        out_shape=[jax.ShapeDtypeStruct((rows, main), g.dtype), jax.ShapeDtypeStruct((rows, 128), g.dtype)],
        compiler_params=_params("parallel"),
    )(g)


def _reshard_in_o(d_main, d_ba, n, *, tr=256):
    rows, main = d_main.shape
    tail = N_DEV * n - main

    def body(m_ref, b_ref, o_ref):
        cat = jnp.concatenate([m_ref[...], b_ref[:, :tail]], axis=1)
        for i in range(N_DEV):
            o_ref[i] = cat[:, n * i:n * (i + 1)]

    return pl.pallas_call(
        body, name="reshard_in_o", grid=(rows // tr,),
        in_specs=[pl.BlockSpec((tr, main), lambda p: (p, 0)), pl.BlockSpec((tr, 128), lambda p: (p, 0))],
        out_specs=pl.BlockSpec((N_DEV, tr, n), lambda p: (0, p, 0)),
        out_shape=jax.ShapeDtypeStruct((N_DEV, rows, n), F32), compiler_params=_params("parallel"),
    )(d_main, d_ba)


def _rows(a, rows):
    flat = a.reshape(-1)
    return jnp.pad(flat, (0, rows * PACK_W - flat.shape[0])).reshape(rows, PACK_W)


def _pack_repl(vals):
    return jnp.concatenate([_rows(vals[name].reshape(shape), rows) for name, shape, rows in REPL_LAYOUT], axis=0)


def _unpack_repl(buf):
    out, r0 = {}, 0
    for name, shape, rows in REPL_LAYOUT:
        n = 1
        for s in shape:
            n *= s
        out[name] = buf[r0:r0 + rows].reshape(-1)[:n].reshape(shape)
        r0 += rows
    return out


WEIGHTS = ("mix_norm_e", "w_in_e", "pool_w", "pool_scale", "w_out_e", "mix_norm_o", "w_in_o", "conv_qkv_o", "a_log_o",
           "dt_bias_o", "gdn_norm_o", "w_out_o", "ffn_norm", "w_up", "ffn_conv", "w_down", "ple_norm", "w_ple_gate",
           "w_ple", "final_norm")


def _ffn_forward(x, hn, w_up, conv_w, w_down, next_gain, tag):
    up = _mm(hn, w_up, name="ffn_up" + tag)
    act = _ffn_act(up, conv_w)
    out, out_n = _mm(act, w_down, res=x, norm_gain=next_gain, name="ffn_down" + tag)
    return out, out_n, (x, hn, up, act)


def _ffn_backward(dx, saved, norm_g, w_up, conv_w, w_down, tag):
    x, hn, up, act = saved
    dact = _mm(dx, w_down, tb=True, name="ffn_dact" + tag)
    d_w_down = _mm(act, dx, ta=True, name="ffn_dwdown" + tag)
    dgate, dval, dcg, dcv = _ffn_act_bwd(up, conv_w, dact)
    dhn = _mm(dgate, w_up, tb=True, name="ffn_dhn_g" + tag)
    dx_in, d_norm = _mm(dval, w_up, tb=True, k_block0=FFN_DIM // _mm_tile(FFN_DIM), res=dhn,
                        rms_bwd=(x, norm_g, dx), name="ffn_dhn_v" + tag)
    shard = w_up.shape[1] // N_DEV
    d_w_up = _mm(hn, dgate, ta=True, shard_cols=shard, shard_into=(N_DEV, 0, None), name="ffn_dwup_g" + tag)
    d_w_up = _mm(hn, dval, ta=True, shard_cols=shard, shard_into=(N_DEV, N_DEV // 2, d_w_up), name="ffn_dwup_v" + tag)
    return dx_in, d_norm, d_w_up, jnp.concatenate([dcg, dcv], axis=1), d_w_down


def _ple_forward(x, hn, w_gate, p, w_ple, tag):
    out, gl, pe = _ple_fwd(hn, w_gate, p, w_ple, x, name="ple_fwd" + tag)
    return out, (x, hn, gl, pe)


def _ple_backward(dx, saved, norm_g, w_gate, p, tag):
    x, hn, gl, pe = saved
    dpe, dgl = _ple_bwd(dx, gl, pe, name="ple_bwd" + tag)
    d_w_ple = _mm(p, dpe, ta=True, shard_cols=dpe.shape[1] // N_DEV, name="ple_dwple" + tag)
    d_w_gate = _mm(hn, dgl, ta=True, name="ple_dwgate" + tag)
    dx, d_norm = _mm(dgl, w_gate, tb=True, rms_bwd=(x, norm_g, dx), name="ple_dhn" + tag)
    return dx, d_norm, d_w_gate, d_w_ple


def kernel(x, p, mix_norm_e, w_in_e, pool_w, pool_scale, w_out_e, mix_norm_o, w_in_o, conv_qkv_o, a_log_o, dt_bias_o, gdn_norm_o, w_out_o, ffn_norm, w_up, ffn_conv, w_down, ple_norm, w_ple_gate, w_ple, final_norm, loss_target, m_mix_norm_e, m_w_in_e, m_pool_w, m_pool_scale, m_w_out_e, m_mix_norm_o, m_w_in_o, m_conv_qkv_o, m_a_log_o, m_dt_bias_o, m_gdn_norm_o, m_w_out_o, m_ffn_norm, m_w_up, m_ffn_conv, m_w_down, m_ple_norm, m_w_ple_gate, m_w_ple, m_final_norm, v_mix_norm_e, v_w_in_e, v_pool_w, v_pool_scale, v_w_out_e, v_mix_norm_o, v_w_in_o, v_conv_qkv_o, v_a_log_o, v_dt_bias_o, v_gdn_norm_o, v_w_out_o, v_ffn_norm, v_w_up, v_ffn_conv, v_w_down, v_ple_norm, v_w_ple_gate, v_w_ple, v_final_norm):
    given = dict(locals())
    place = jnp.stack(_place()).astype(jnp.int32)
    x0, tgt = x[0], loss_target[0]

    def pieces(prefix):
        out = {}
        for width, _, members in GROUPS.values():
            for (name, layer), rows, _ in members:
                a = given[prefix + name]
                out[(name, layer)] = (a if layer is None else a[layer]).reshape(rows, width)
        return out

    def flat2d(name):
        return given[name].reshape(-1, given[name].shape[-1])

    small = ("ffn_conv", "mix_norm_o", "conv_qkv_o")
    got = _all_gather([flat2d("w_in_e").astype(BF16)] + [_pad_rows(flat2d(k), 8) for k in small], name="ag_first")
    full = {("w_in_e", 0): _unshard_pairs(got[0], 0, D_MODEL, "in_e")}
    for i in range(2):
        full[("ffn_conv", i)] = _natural("ffn_conv", got[1][:, 3 * i:3 * i + 3], 1)[0]
    mix_norm_o_full = got[2][:, 0].reshape(1, D_MODEL)
    conv_qkv = _natural("conv_qkv_o", got[3][:, :4], 1)[0]
    alog_row = jnp.pad(a_log_o, ((0, 0), (8, 112)))
    dt_row = jnp.pad(dt_bias_o, ((0, 0), (8, 112)))
    lw = lambda name, i: full[(name, i)]

    h_e = _rms_fwd(x0, mix_norm_e, name="rms_mix_e")
    proj_e = _mm(h_e, lw("w_in_e", 0), name="in_e")
    pool_o = _pool_fwd(proj_e, pool_w[0], pool_scale)
    wide = ("w_out_e", "w_out_o", "w_down", "w_ple_gate")
    send = [jnp.concatenate([flat2d(k).astype(BF16) for k in wide], axis=0)]
    att_o, lsum, got = _sb_fwd(proj_e, gather=send + [flat2d(k).astype(BF16) for k in ("w_in_o", "w_up", "w_ple")])
    gathered, r0 = {}, 0
    for k in wide:
        gathered[k] = got[0][:, r0:r0 + flat2d(k).shape[0]]
        r0 += flat2d(k).shape[0]
    layers = {name: given[name].shape[0] if given[name].ndim == 3 else 1 for name in gathered}
    full.update({(name, i): w for name in gathered for i, w in enumerate(_natural(name, gathered[name], layers[name]))})
    w_in_o_main, w_in_o_ba = _unshard_in_o(got[1])
    for i in range(2):
        full[("w_up", i)] = _unshard_pairs(got[2], i, D_MODEL, "up")
        full[("w_ple", i)] = _unshard_pairs(got[3], i, PLE_DIM, "ple")
    mix_e = jnp.concatenate([pool_o, att_o.astype(BF16)], axis=1)
    x1, hf0 = _mm(mix_e, lw("w_out_e", 0), res=x0, norm_gain=ffn_norm[0:1], name="out_e")
    x2, hp0, ffn0 = _ffn_forward(x1, hf0, lw("w_up", 0), lw("ffn_conv", 0), lw("w_down", 0), ple_norm[0:1], "0")
    x3, ple0 = _ple_forward(x2, hp0, lw("w_ple_gate", 0), p[0, 0], lw("w_ple", 0), "0")

    h_o = _rms_fwd(x3, mix_norm_o_full, name="rms_mix_o")
    proj_o = _mm(h_o, w_in_o_main, name="in_o")
    ba = _mm(h_o, w_in_o_ba, name="in_o_ba")
    qkv = _gdn_pre(proj_o, conv_qkv)
    beta, g = _gdn_gate(ba, alog_row, dt_row)
    u, w_c, qk, qd, kd, gl = _gdn_local(qkv, g, beta)
    o, states, vnew = _gdn_scan(u, w_c, qk, qd, kd, gl)
    y_o = _gdn_post(o, proj_o, gdn_norm_o)
    x4, hf1 = _mm(y_o, lw("w_out_o", 0), res=x3, norm_gain=ffn_norm[1:2], name="out_o")
    x5, hp1, ffn1 = _ffn_forward(x4, hf1, lw("w_up", 1), lw("ffn_conv", 1), lw("w_down", 1), ple_norm[1:2], "1")
    x6, ple1 = _ple_forward(x5, hp1, lw("w_ple_gate", 1), p[1, 0], lw("w_ple", 1), "1")
    loss_row, dx, d_final = _final_loss(x6, final_norm.reshape(1, D_MODEL), tgt)

    grads, rgrads = {}, {}
    dx, d_ple1, grads[("w_ple_gate", 1)], grads[("w_ple", 1)] = _ple_backward(dx, ple1, ple_norm[1:2], lw("w_ple_gate", 1), p[1, 0], "1")
    dx, d_ffn1, grads[("w_up", 1)], grads[("ffn_conv", 1)], grads[("w_down", 1)] = _ffn_backward(
        dx, ffn1, ffn_norm[1:2], lw("w_up", 1), lw("ffn_conv", 1), lw("w_down", 1), "1")
    grads[("w_out_o", 0)] = _mm(y_o, dx, ta=True, name="dw_out_o")
    dy_o = _mm(dx, lw("w_out_o", 0), tb=True, name="dy_o")
    do, dz, rgrads["gdn_norm_o"] = _gdn_post_bwd(o, proj_o, gdn_norm_o, dy_o)
    du, dw_c, dqk, dqd, dkd, dgl = _gdn_scan_bwd(do, states, vnew, w_c, qk, qd, kd, gl)
    dqkv_heads, dg, dbeta = _gdn_local_bwd(qkv, g, beta, du, dw_c, dqk, dqd, dkd, dgl)
    dqkv, grads[("conv_qkv_o", 0)] = _gdn_pre_bwd(proj_o, conv_qkv, dqkv_heads)
    dba, d_alog, d_dt = _gdn_gate_bwd(ba, alog_row, dt_row, dbeta, dg)
    rgrads["a_log_o"], rgrads["dt_bias_o"] = d_alog[:, 8:16], d_dt[:, 8:16]
    dproj_o = jnp.concatenate([dqkv, dz], axis=1)
    dh = _mm(dproj_o, w_in_o_main, tb=True, name="dh_o")
    dx_o, d_mix_o = _mm(dba, w_in_o_ba, tb=True, res=dh, rms_bwd=(x3, mix_norm_o_full, dx), name="dh_o_ba")
    grads[("w_in_o", 0)] = _reshard_in_o(_mm(h_o, dproj_o, ta=True, name="dw_in_o"),
                                         _mm(h_o, dba, ta=True, name="dw_in_o_ba"), w_in_o.shape[2])
    dx = dx_o
    grads[("mix_norm_o", 0)] = d_mix_o

    dx, d_ple0, grads[("w_ple_gate", 0)], grads[("w_ple", 0)] = _ple_backward(dx, ple0, ple_norm[0:1], lw("w_ple_gate", 0), p[0, 0], "0")
    dx, d_ffn0, grads[("w_up", 0)], grads[("ffn_conv", 0)], grads[("w_down", 0)] = _ffn_backward(
        dx, ffn0, ffn_norm[0:1], lw("w_up", 0), lw("ffn_conv", 0), lw("w_down", 0), "0")
    grads[("w_out_e", 0)] = _mm(mix_e, dx, ta=True, name="dw_out_e")
    dmix = _mm(dx, lw("w_out_e", 0), tb=True, name="dmix_e")
    du_e, d_pool_w, rgrads["pool_scale"] = _pool_bwd(proj_e, dmix, pool_w[0], pool_scale)
    rgrads["pool_w"] = d_pool_w[None]

    def reduce_start(gnames, tag):
        smaj = {}
        for g in gnames:
            for (name, layer), _, _ in GROUPS[g][2]:
                of = [grads[(name, i)] for i in ((0, 1) if layer is None else (layer,)) if (name, i) in grads]
                smaj[(name, layer)] = _shard_major(name, of, len(of))
        gbuf = [_group_rows(smaj, g) for g in gnames]
        gbuf = [b.reshape((4, 2) + b.shape[1:]) for b in gbuf]
        got1 = _exchange_sibling(gbuf, name="rs_sibling" + tag)
        part = [_chip_partial(place, b, r, tr=GROUPS[g][1], name="rs_chip_partial_" + g)
                for g, b, r in zip(gnames, gbuf, got1)]
        return gbuf, got1, part

    early = tuple(g for g in GROUPS if g != "in_e")
    gbuf_e, got1_e, part_e = reduce_start(early, "_early")
    dq_e, dk_e, dv_e, got2_e = _sb_bwd(proj_e, lsum, dmix, exchange=part_e)
    dproj_e = jnp.concatenate([du_e, dq_e.astype(BF16), dk_e.astype(BF16), dv_e.astype(BF16)], axis=1)
    grads[("w_in_e", 0)] = _mm(h_e, dproj_e, ta=True, shard_cols=w_in_e.shape[2], name="dw_in_e")
    dx, rgrads["mix_norm_e"] = _mm(dproj_e, lw("w_in_e", 0), tb=True, rms_bwd=(x0, mix_norm_e, dx), name="dh_e")
    rgrads["ffn_norm"] = jnp.concatenate([d_ffn0, d_ffn1], axis=0)
    rgrads["ple_norm"] = jnp.concatenate([d_ple0, d_ple1], axis=0)
    rgrads["final_norm"] = d_final.reshape(D_MODEL)

    gbuf_l, got1_l, part_l = reduce_start(("in_e",), "_late")
    got2_l = _exchange_chips(part_l)
    wloc, mloc, vloc = pieces(""), pieces("m_"), pieces("v_")
    sh_out = [{}, {}, {}, {}]
    for g, b, r1, r2 in zip(early + ("in_e",), gbuf_e + gbuf_l, list(got1_e) + list(got1_l),
                            list(got2_e) + list(got2_l)):
        res = _adamw_shard(place, b, r1, r2, _group_rows(wloc, g), _group_rows(mloc, g), _group_rows(vloc, g),
                           tr=GROUPS[g][1], name="adamw_" + g)
        for kind in range(4):
            sh_out[kind].update(_ungroup_rows(res[kind], g))

    (rparts,) = _all_gather([_pack_repl(rgrads)], name="ag_repl_grads")
    rp_out = _adamw_replicated(rparts, _pack_repl({n: given[n] for n, _, _ in REPL_LAYOUT}),
                               _pack_repl({n: given["m_" + n] for n, _, _ in REPL_LAYOUT}),
                               _pack_repl({n: given["v_" + n] for n, _, _ in REPL_LAYOUT}))
    rp_out = [_unpack_repl(b) for b in rp_out]

    def leaf(kind, name):
        if name in SHARDED:
            mine = sh_out[kind]
            whole = mine[(name, None)] if (name, None) in mine else jnp.stack([mine[(name, 0)], mine[(name, 1)]])
            return whole.reshape(given[name].shape)
        return rp_out[kind][name]

    loss = lax.psum(loss_row[0, 0], ("x", "y", "c"))
    outs = [loss, dx[None]]
    for kind in range(4):
        outs += [leaf(kind, n) for n in WEIGHTS]
    return tuple(outs)
```

```python
import jax
import jax.numpy as jnp
from jax import lax
from jax.experimental import pallas as pl
from jax.experimental.pallas import tpu as pltpu

F32 = jnp.float32
BF16 = jnp.bfloat16

D_MODEL = 1024
PLE_DIM = 256
POOL_WINDOWS = (2, 4, 8, 16)
SB_HEAD_DIM = 64
SB_BLOCK = 1024
SB_BLOCK_FWD = 2048
SB_KBLOCK = 256
GDN_HEADS = 8
GDN_HEAD_DIM = 128
GDN_CONV = 4
GDN_CHUNK = 64
FFN_DIM = 2816
FFN_CONV = 3
EPS = 1e-6
ADAM_LR, ADAM_B1, ADAM_B2, ADAM_EPS, ADAM_WD, ADAM_STEP = 0.001, 0.9, 0.999, 1e-08, 0.01, 10
N_DEV = 8
MESH = pl.DeviceIdType.MESH
VMEM_LIMIT = 56 * 1024 * 1024

NN = (((1,), (0,)), ((), ()))
NT = (((1,), (1,)), ((), ()))
TN = (((0,), (0,)), ((), ()))


def _params(*sem):
    return pltpu.CompilerParams(dimension_semantics=sem if sem else None, vmem_limit_bytes=VMEM_LIMIT)


def _dot(a, b, dims):
    return lax.dot_general(a.astype(BF16), b.astype(BF16), dims, preferred_element_type=F32)


def _iota(shape, axis):
    return lax.broadcasted_iota(jnp.int32, shape, axis)


MM_TILE, MM_TILE_11 = 1024, 1408


def _mm_tile(dim):
    if dim <= MM_TILE_11:
        return dim
    return MM_TILE if dim % MM_TILE == 0 else MM_TILE_11


def _mm(a, b, *, ta=False, tb=False, res=None, norm_gain=None, rms_bwd=None, k_block0=0, shard_cols=None,
        shard_into=None, name):
    M, K = (a.shape[1], a.shape[0]) if ta else a.shape
    N = b.shape[0] if tb else b.shape[1]
    tm, tn, tk = _mm_tile(M), _mm_tile(N), _mm_tile(K)
    if rms_bwd is not None:
        tm = min(tm, 512)
    assert M % tm == 0 and N % tn == 0 and K % tk == 0, (name, M, N, K, tm, tn, tk)
    assert (norm_gain is None and rms_bwd is None) or tn == N, name
    nk = K // tk
    dims = (((0 if ta else 1,), (1 if tb else 0,)), ((), ()))
    extra = [res] if res is not None else []
    vecs = [norm_gain] if norm_gain is not None else []
    if rms_bwd is not None:
        extra += [rms_bwd[0], rms_bwd[2]]
        vecs = [rms_bwd[1]]
    n_out = 1 if (norm_gain is None and rms_bwd is None) else 2

    def body(*refs):
        a_ref, b_ref = refs[:2]
        tiles = list(refs[2:2 + len(extra)])
        vec_refs = refs[2 + len(extra):2 + len(extra) + len(vecs)]
        n_in = 2 + len(extra) + len(vecs) + len(held)
        outs = refs[n_in:n_in + n_out]
        scr = refs[n_in + n_out:]
        p = _dot(a_ref[...], b_ref[...], dims)

        def fin(acc):
            if res is not None:
                acc = acc + tiles[0][...]
            if rms_bwd is not None:
                x_ref, dres_ref = tiles[-2:]
                xv = x_ref[...]
                r = lax.rsqrt(jnp.mean(xv * xv, axis=-1, keepdims=True) + EPS)
                xn = xv * r
                dgp = jnp.sum(acc * xn, axis=0, keepdims=True)
                dyg = acc * vec_refs[0][...]
                outs[0][...] = dres_ref[...] + r * (dyg - xn * jnp.mean(dyg * xn, axis=-1, keepdims=True))
                first = pl.program_id(0) == 0

                @pl.when(first)
                def _():
                    outs[1][...] = dgp

                @pl.when(jnp.logical_not(first))
                def _():
                    outs[1][...] += dgp
                return
            if shard_cols is not None:
                for s in range(tn // shard_cols):
                    outs[0][s] = acc[:, shard_cols * s:shard_cols * (s + 1)]
                return
            outs[0][...] = acc
            if norm_gain is not None:
                r = lax.rsqrt(jnp.mean(acc * acc, axis=-1, keepdims=True) + EPS)
                outs[1][...] = (acc * r * vec_refs[0][...]).astype(BF16)

        if nk == 1:
            fin(p)
        else:
            acc_ref = scr[0]
            k = pl.program_id(2)

            @pl.when(k == 0)
            def _():
                acc_ref[...] = p

            @pl.when(k > 0)
            def _():
                acc_ref[...] += p

            @pl.when(k == nk - 1)
            def _():
                fin(acc_ref[...])

    a_spec = pl.BlockSpec((tk, tm), lambda i, j, k: (k, i)) if ta else pl.BlockSpec((tm, tk), lambda i, j, k: (i, k))
    b_spec = (pl.BlockSpec((tn, tk), lambda i, j, k: (j, k + k_block0)) if tb
              else pl.BlockSpec((tk, tn), lambda i, j, k: (k, j)))
    o_spec = pl.BlockSpec((tm, tn), lambda i, j, k: (i, j))
    v_spec = pl.BlockSpec((1, tn), lambda i, j, k: (0, j))
    out_specs, out_shape = [o_spec], [jax.ShapeDtypeStruct((M, N), F32)]
    held = []
    if shard_cols is not None:
        per = tn // shard_cols
        total, first, buf = shard_into if shard_into is not None else (N // shard_cols, 0, None)
        out_specs = [pl.BlockSpec((per, tm, shard_cols), lambda i, j, k: (j + first // per, i, 0))]
        out_shape = [jax.ShapeDtypeStruct((total, M, shard_cols), F32)]
        held = [buf] if buf is not None else []
    if norm_gain is not None:
        out_specs, out_shape = out_specs + [o_spec], out_shape + [jax.ShapeDtypeStruct((M, N), BF16)]
    if rms_bwd is not None:
        out_specs, out_shape = out_specs + [v_spec], out_shape + [jax.ShapeDtypeStruct((1, N), F32)]
    out = pl.pallas_call(
        body, name=name, grid=(M // tm, N // tn, nk),
        in_specs=([a_spec, b_spec] + [o_spec] * len(extra) + [v_spec] * len(vecs)
                  + [pl.BlockSpec(memory_space=pl.ANY)] * len(held)),
        out_specs=out_specs, out_shape=out_shape, scratch_shapes=[pltpu.VMEM((tm, tn), F32)] if nk > 1 else [],
        input_output_aliases={2 + len(extra) + len(vecs): 0} if held else {},
        compiler_params=_params("arbitrary" if rms_bwd is not None else "parallel", "parallel", "arbitrary"),
    )(a, b, *extra, *vecs, *held)
    return out[0] if n_out == 1 else out


def _rms_fwd(x, gain, *, name, tr=512):
    T, Dm = x.shape

    def body(x_ref, g_ref, o_ref):
        xv = x_ref[...]
        r = lax.rsqrt(jnp.mean(xv * xv, axis=-1, keepdims=True) + EPS)
        o_ref[...] = (xv * r * g_ref[...]).astype(BF16)

    return pl.pallas_call(
        body, name=name, grid=(T // tr,),
        in_specs=[pl.BlockSpec((tr, Dm), lambda i: (i, 0)), pl.BlockSpec((1, Dm), lambda i: (0, 0))],
        out_specs=pl.BlockSpec((tr, Dm), lambda i: (i, 0)),
        out_shape=jax.ShapeDtypeStruct((T, Dm), BF16), compiler_params=_params("parallel"),
    )(x, gain)


def _final_loss(x, gain, target, *, tr=512):
    T, Dm = x.shape

    def body(x_ref, g_ref, t_ref, loss_ref, dx_ref, dg_ref):
        i = pl.program_id(0)
        xv = x_ref[...]
        g = g_ref[...]
        r = lax.rsqrt(jnp.mean(xv * xv, axis=-1, keepdims=True) + EPS)
        xn = xv * r
        err = xn * g - t_ref[...]
        lp = jnp.zeros((1, 128), F32) + 0.5 * jnp.sum(jnp.mean(err * err, axis=-1, keepdims=True))
        dy_v = err * (1.0 / Dm)
        dgp = jnp.sum(dy_v * xn, axis=0, keepdims=True)
        dyg = dy_v * g
        dx_ref[...] = r * (dyg - xn * jnp.mean(dyg * xn, axis=-1, keepdims=True))

        @pl.when(i == 0)
        def _():
            dg_ref[...] = dgp
            loss_ref[...] = lp

        @pl.when(i > 0)
        def _():
            dg_ref[...] += dgp
            loss_ref[...] += lp

    row = pl.BlockSpec((tr, Dm), lambda i: (i, 0))
    vec = pl.BlockSpec((1, Dm), lambda i: (0, 0))
    return pl.pallas_call(
        body, name="final_loss", grid=(T // tr,), in_specs=[row, vec, row],
        out_specs=[pl.BlockSpec((1, 128), lambda i: (0, 0)), row, vec],
        out_shape=[jax.ShapeDtypeStruct((1, 128), F32), jax.ShapeDtypeStruct((T, Dm), F32),
                   jax.ShapeDtypeStruct((1, Dm), F32)],
        compiler_params=_params("arbitrary"),
    )(x, gain, target)


def _prev_spec(tr, cb, pad, col):
    return pl.BlockSpec((pad, cb), lambda *g: (jnp.maximum(g[0] * (tr // pad) - 1, 0), col(*g)))


def _next_spec(tr, cb, pad, col, T):
    return pl.BlockSpec((pad, cb), lambda *g: (jnp.minimum((g[0] + 1) * (tr // pad), T // pad - 1), col(*g)))


def _conv_rows(x_ext, w_ref, K, pad, cs=slice(None)):
    y = w_ref[K - 1:K, cs] * x_ext
    for i in range(K - 1):
        y = y + w_ref[i:i + 1, cs] * pltpu.roll(x_ext, K - 1 - i, 0)
    return y[pad:]


def _pool_y(u_ext, g, i, tr):
    s = u_ext
    for sh in (1, 2, 4, 8)[:g + 1]:
        s = s + pltpu.roll(s, sh, 0)
    t = i * tr + _iota((tr, 128), 0)
    cnt = jnp.minimum(t + 1, POOL_WINDOWS[g]).astype(F32)
    return s[16:] / cnt - u_ext[16:]


def _pool_fwd(proj, pool_w, pool_scale, *, tr=512):
    T = proj.shape[0]

    def body(u_ref, uh_ref, w_ref, s_ref, o_ref):
        i = pl.program_id(0)
        uh = jnp.where(i > 0, uh_ref[...], 0.0)
        for g in range(4):
            cs = slice(128 * g, 128 * (g + 1))
            y = _pool_y(jnp.concatenate([uh[:, cs], u_ref[:, cs]], axis=0), g, i, tr)
            o_ref[:, cs] = (_dot(y, w_ref[g], NN) * s_ref[:, cs]).astype(BF16)

    return pl.pallas_call(
        body, name="pool_fwd", grid=(T // tr,),
        in_specs=[pl.BlockSpec((tr, 512), lambda i: (i, 0)), _prev_spec(tr, 512, 16, lambda i: 0),
                  pl.BlockSpec((4, 128, 128), lambda i: (0, 0, 0)), pl.BlockSpec((1, 512), lambda i: (0, 0))],
        out_specs=pl.BlockSpec((tr, 512), lambda i: (i, 0)),
        out_shape=jax.ShapeDtypeStruct((T, 512), BF16), compiler_params=_params("parallel"),
    )(proj, proj, pool_w, pool_scale)


def _pool_bwd(proj, dout, pool_w, pool_scale, *, tr=512):
    T = proj.shape[0]
    nb = T // tr

    def body(u_ref, uh_ref, d_ref, dn_ref, w_ref, s_ref, du_ref, dw_ref, ds_ref):
        i = pl.program_id(0)
        uh = jnp.where(i > 0, uh_ref[...], 0.0)
        dn = jnp.where(i < nb - 1, dn_ref[...], 0.0)
        t_ext = i * tr + _iota((tr + 16, 128), 0)
        for g in range(4):
            cs = slice(128 * g, 128 * (g + 1))
            sc = s_ref[:, cs]
            wg = w_ref[g]
            y = _pool_y(jnp.concatenate([uh[:, cs], u_ref[:, cs]], axis=0), g, i, tr)
            dg = d_ref[:, cs]
            dsp = jnp.sum(dg * _dot(y, wg, NN), axis=0, keepdims=True)
            dyw = dg * sc
            dwp = _dot(y, dyw, TN)
            dy_ext = _dot(jnp.concatenate([dyw, dn[:, cs] * sc], axis=0), wg, NT)
            cnt = jnp.minimum(t_ext + 1, POOL_WINDOWS[g]).astype(F32)
            s = dy_ext / cnt
            for sh in (1, 2, 4, 8)[:g + 1]:
                s = s + pltpu.roll(s, tr + 16 - sh, 0)
            du_ref[:, cs] = (s[:tr] - dy_ext[:tr]).astype(BF16)

            @pl.when(i == 0)
            def _():
                dw_ref[g] = dwp
                ds_ref[:, cs] = dsp

            @pl.when(i > 0)
            def _():
                dw_ref[g] += dwp
                ds_ref[:, cs] += dsp

    row = pl.BlockSpec((tr, 512), lambda i: (i, 0))
    return pl.pallas_call(
        body, name="pool_bwd", grid=(nb,),
        in_specs=[row, _prev_spec(tr, 512, 16, lambda i: 0), row, _next_spec(tr, 512, 16, lambda i: 0, T),
                  pl.BlockSpec((4, 128, 128), lambda i: (0, 0, 0)), pl.BlockSpec((1, 512), lambda i: (0, 0))],
        out_specs=[row, pl.BlockSpec((4, 128, 128), lambda i: (0, 0, 0)), pl.BlockSpec((1, 512), lambda i: (0, 0))],
        out_shape=[jax.ShapeDtypeStruct((T, 512), BF16), jax.ShapeDtypeStruct((4, 128, 128), F32),
                   jax.ShapeDtypeStruct((1, 512), F32)],
        compiler_params=_params("arbitrary"),
    )(proj, proj, dout, dout, pool_w, pool_scale)


def _split_dot(x, tri):
    hi = x.astype(BF16)
    lo = (x - hi.astype(F32)).astype(BF16)
    return (lax.dot_general(hi, tri, NN, preferred_element_type=F32)
            + lax.dot_general(lo, tri, NN, preferred_element_type=F32))


def _log1m(z):
    return -(jnp.maximum(z, 0.0) + jnp.log(1.0 + jnp.exp(-jnp.abs(z))))


def _sb_fwd(proj, gather=()):
    T = proj.shape[0]
    B, BK = min(SB_BLOCK_FWD, T), SB_KBLOCK
    R = B // BK
    nq, n = T // B, len(gather)
    scale = SB_HEAD_DIM ** -0.5

    def body(q_ref, k_ref, v_ref, *rest):
        o_ref, ls_ref = rest[n:n + 2]
        hp, i = pl.program_id(0), pl.program_id(1)
        if n:
            start, forward, finish = _gather_stages(rest[:n], rest[n + 2:2 * n + 2], *rest[2 * n + 2:])
            pl.when((hp == 0) & (i == 0))(start)
            pl.when((hp == 3) & (i == nq - 1))(forward)
        lane = _iota((1, 128), 1)
        tri_gt = (_iota((BK, BK), 0) > _iota((BK, BK), 1)).astype(BF16)
        row, col = _iota((B, BK), 0), _iota((B, BK), 1)
        qv = q_ref[...] * scale
        hms = [(lane >= 64 * h) & (lane < 64 * (h + 1)) for h in range(2)]
        qhs = [jnp.where(hm, qv, 0.0).astype(BF16) for hm in hms]

        def tile(j, carry, d):
            rows = pl.ds(pl.multiple_of(j * BK, BK), BK)
            kj = k_ref[rows, :].astype(BF16)
            vj = v_ref[rows, :].astype(BF16)
            r0 = 0 if d is None else BK * d
            valid = None if d is None else (col[r0:] < row[:B - r0])
            out = []
            for h in range(2):
                c, acc = carry[h]
                z = lax.dot_general(qhs[h][r0:], kj, NT, preferred_element_type=F32)
                lg = _log1m(z)
                if d is not None:
                    lg = jnp.where(valid, lg, 0.0)
                a = jnp.exp(z + lg + _split_dot(lg, tri_gt) + c[r0:])
                if d is not None:
                    a = jnp.where(valid, a, 0.0)
                upd = (c[r0:] + jnp.sum(lg, axis=1, keepdims=True),
                       acc[r0:] + lax.dot_general(a.astype(BF16), vj, NN, preferred_element_type=F32))
                out.append(upd if r0 == 0 else tuple(jnp.concatenate([old[:r0], new], axis=0)
                                                     for old, new in zip((c, acc), upd)))
            return tuple(out)

        zero = (jnp.zeros((B, 1), F32), jnp.zeros((B, 128), F32))
        carry = (zero, zero)
        for d in reversed(range(R)):
            carry = tile(i * R + d, carry, d)
        carry = lax.fori_loop(0, i * R, lambda s, cr: tile(i * R - 1 - s, cr, None), carry)
        o_ref[...] = jnp.where(hms[0], carry[0][1], carry[1][1])
        ls_ref[...] = jnp.where(hms[0], carry[0][0], carry[1][0])
        if n:
            pl.when((hp == 3) & (i == nq - 1))(finish)

    blk = pl.BlockSpec((B, 128), lambda hp, i: (i, hp))
    out = pl.pallas_call(
        body, name="sb_fwd", grid=(4, nq),
        in_specs=[pl.BlockSpec((B, 128), lambda hp, i: (i, 4 + hp)),
                  pl.BlockSpec((T, 128), lambda hp, i: (0, 8 + hp)),
                  pl.BlockSpec((T, 128), lambda hp, i: (0, 12 + hp))] + [HBM_SPEC] * n,
        out_specs=[blk, blk] + [HBM_SPEC] * n,
        out_shape=[jax.ShapeDtypeStruct((T, 512), F32)] * 2 + _gather_shapes(gather),
        scratch_shapes=_gather_sems(n) if n else [],
        compiler_params=_params("arbitrary", "arbitrary"),
    )(proj, proj, proj, *gather)
    return out[0], out[1], list(out[2:])


def _sb_bwd(proj, lsum, dout, exchange=()):
    T = proj.shape[0]
    B, BK = min(SB_BLOCK, T), SB_KBLOCK
    R = B // BK
    nq, n = T // B, len(exchange)
    scale = SB_HEAD_DIM ** -0.5

    def body(q_ref, k_ref, v_ref, do_ref, ls_ref, *rest):
        dq_ref, dk_ref, dv_ref = rest[n:n + 3]
        hp, i = pl.program_id(0), pl.program_id(1)
        if n:
            start, finish = _chips_stages(rest[:n], rest[n + 3:2 * n + 3], *rest[2 * n + 3:])
            pl.when((hp == 0) & (i == 0))(start)

        @pl.when(i == 0)
        def _():
            dk_ref[...] = jnp.zeros_like(dk_ref)
            dv_ref[...] = jnp.zeros_like(dv_ref)

        lane = _iota((1, 128), 1)
        tri_le = (_iota((BK, BK), 0) <= _iota((BK, BK), 1)).astype(BF16)
        tri_lt = (_iota((BK, BK), 0) < _iota((BK, BK), 1)).astype(BF16)
        row, col = _iota((B, BK), 0), _iota((B, BK), 1)
        qv = q_ref[...] * scale
        dov = do_ref[...]
        hms = [(lane >= 64 * h) & (lane < 64 * (h + 1)) for h in range(2)]
        qhs = [jnp.where(hm, qv, 0.0).astype(BF16) for hm in hms]
        dos = [jnp.where(hm, dov, 0.0).astype(BF16) for hm in hms]
        ltots = [ls_ref[:, 64 * h:64 * h + 1] for h in range(2)]

        def tile(j, carry, d):
            rows = pl.ds(pl.multiple_of(j * BK, BK), BK)
            kj = k_ref[rows, :].astype(BF16)
            vj = v_ref[rows, :].astype(BF16)
            diag = d is not None
            r0 = BK * d if diag else 0
            valid = (col[r0:] < row[:B - r0]) if diag else None
            out = []
            dkj = jnp.zeros((BK, 128), F32)
            dvj = jnp.zeros((BK, 128), F32)
            for h in range(2):
                lbef, ebef, dqa = carry[h]
                qh, do_h = qhs[h][r0:], dos[h][r0:]
                z = lax.dot_general(qh, kj, NT, preferred_element_type=F32)
                lg = _log1m(z)
                if diag:
                    lg = jnp.where(valid, lg, 0.0)
                a = jnp.exp(z + lg + (ltots[h][r0:] - lbef[r0:] - _split_dot(lg, tri_le)))
                if diag:
                    a = jnp.where(valid, a, 0.0)
                e = a * lax.dot_general(do_h, vj, NT, preferred_element_type=F32)
                dz = e * jnp.exp(lg) - jnp.exp(z + lg) * (ebef[r0:] + _split_dot(e, tri_lt))
                if diag:
                    dz = jnp.where(valid, dz, 0.0)
                dzb = dz.astype(BF16)
                dkj = dkj + lax.dot_general(dzb, qh, TN, preferred_element_type=F32)
                dvj = dvj + lax.dot_general(a.astype(BF16), do_h, TN, preferred_element_type=F32)
                upd = (lbef[r0:] + jnp.sum(lg, axis=1, keepdims=True), ebef[r0:] + jnp.sum(e, axis=1, keepdims=True),
                       dqa[r0:] + lax.dot_general(dzb, kj, NN, preferred_element_type=F32))
                out.append(upd if r0 == 0 else tuple(jnp.concatenate([old[:r0], new], axis=0)
                                                     for old, new in zip(carry[h], upd)))
            dk_ref[rows, :] += dkj
            dv_ref[rows, :] += dvj
            return tuple(out)

        zero = (jnp.zeros((B, 1), F32), jnp.zeros((B, 1), F32), jnp.zeros((B, 128), F32))
        carry = lax.fori_loop(0, i * R, lambda j, cr: tile(j, cr, None), (zero, zero))
        for d in range(R):
            carry = tile(i * R + d, carry, d)
        dq_ref[...] = jnp.where(hms[0], carry[0][2], carry[1][2]) * scale
        if n:
            pl.when((hp == 3) & (i == nq - 1))(finish)

    full = pl.BlockSpec((T, 128), lambda hp, i: (0, hp))
    blk = pl.BlockSpec((B, 128), lambda hp, i: (i, hp))
    out = pl.pallas_call(
        body, name="sb_bwd", grid=(4, nq),
        in_specs=[pl.BlockSpec((B, 128), lambda hp, i: (i, 4 + hp)),
                  pl.BlockSpec((T, 128), lambda hp, i: (0, 8 + hp)),
                  pl.BlockSpec((T, 128), lambda hp, i: (0, 12 + hp)),
                  pl.BlockSpec((B, 128), lambda hp, i: (i, 4 + hp)), blk] + [HBM_SPEC] * n,
        out_specs=[blk, full, full] + [HBM_SPEC] * n,
        out_shape=[jax.ShapeDtypeStruct((T, 512), F32)] * 3 + _chips_shapes(exchange),
        scratch_shapes=_chips_sems(n) if n else [],
        compiler_params=_params("arbitrary", "arbitrary"),
    )(proj, proj, proj, dout, lsum, *exchange)
    return out[0], out[1], out[2], list(out[3:])


def _sigmoid(x):
    return 1.0 / (1.0 + jnp.exp(-x))


def _silu_mul(cg, cv):
    return cg * _sigmoid(cg) * cv


def _ffn_act(up, conv_w, *, tr=512, cb=256):
    T, F2 = up.shape
    nc = F2 // 2 // cb
    K = FFN_CONV

    def body(g_ref, gh_ref, v_ref, vh_ref, wg_ref, wv_ref, o_ref):
        i = pl.program_id(0)
        gh = jnp.where(i > 0, gh_ref[...], 0.0)
        vh = jnp.where(i > 0, vh_ref[...], 0.0)
        cg = _conv_rows(jnp.concatenate([gh, g_ref[...]], axis=0), wg_ref, K, 8)
        cv = _conv_rows(jnp.concatenate([vh, v_ref[...]], axis=0), wv_ref, K, 8)
        o_ref[...] = _silu_mul(cg, cv).astype(BF16)

    return pl.pallas_call(
        body, name="ffn_act", grid=(T // tr, nc),
        in_specs=[pl.BlockSpec((tr, cb), lambda i, j: (i, j)), _prev_spec(tr, cb, 8, lambda i, j: j),
                  pl.BlockSpec((tr, cb), lambda i, j: (i, nc + j)), _prev_spec(tr, cb, 8, lambda i, j: nc + j),
                  pl.BlockSpec((K, cb), lambda i, j: (0, j)), pl.BlockSpec((K, cb), lambda i, j: (0, nc + j))],
        out_specs=pl.BlockSpec((tr, cb), lambda i, j: (i, j)),
        out_shape=jax.ShapeDtypeStruct((T, F2 // 2), BF16), compiler_params=_params("parallel", "parallel"),
    )(up, up, up, up, conv_w, conv_w)


def _conv_bwd_rows(dc_ext, x_ext, w_ref, K, tr, cs=slice(None)):
    n = tr + 8
    dx = w_ref[K - 1:K, cs] * dc_ext
    for i in range(K - 1):
        dx = dx + w_ref[i:i + 1, cs] * pltpu.roll(dc_ext, n - (K - 1 - i), 0)
    dc = dc_ext[:tr]
    dws = [jnp.sum(dc * pltpu.roll(x_ext, K - 1 - i, 0)[8:8 + tr], axis=0, keepdims=True) for i in range(K)]
    return dx[:tr], dws


def _acc_rows(ref, rows, first, cs=slice(None)):
    for i, r in enumerate(rows):
        @pl.when(first)
        def _():
            ref[i:i + 1, cs] = r

        @pl.when(jnp.logical_not(first))
        def _():
            ref[i:i + 1, cs] += r


def _ffn_act_bwd(up, conv_w, dact, *, tr=512, cb=256):
    T, F2 = up.shape
    F = F2 // 2
    nc, nb = F // cb, T // tr
    K = FFN_CONV

    def body(g_ref, gp_ref, gn_ref, v_ref, vp_ref, vn_ref, d_ref, dn_ref, wg_ref, wv_ref,
             dg_ref, dv_ref, dwg_ref, dwv_ref):
        i = pl.program_id(1)
        first, last = i == 0, i == nb - 1
        g_ext = jnp.concatenate([jnp.where(first, 0.0, gp_ref[...]), g_ref[...], jnp.where(last, 0.0, gn_ref[...])], axis=0)
        v_ext = jnp.concatenate([jnp.where(first, 0.0, vp_ref[...]), v_ref[...], jnp.where(last, 0.0, vn_ref[...])], axis=0)
        d_ext = jnp.concatenate([d_ref[...], jnp.where(last, 0.0, dn_ref[...])], axis=0)
        cg = _conv_rows(g_ext, wg_ref, K, 8)
        cv = _conv_rows(v_ext, wv_ref, K, 8)
        s = _sigmoid(cg)
        t = cg * s
        dcv = d_ext * t
        dcg = d_ext * cv * (s + t * (1.0 - s))
        dg, dwg = _conv_bwd_rows(dcg, g_ext, wg_ref, K, tr)
        dv, dwv = _conv_bwd_rows(dcv, v_ext, wv_ref, K, tr)
        dg_ref[...] = dg.astype(BF16)
        dv_ref[...] = dv.astype(BF16)
        _acc_rows(dwg_ref, dwg, first)
        _acc_rows(dwv_ref, dwv, first)

    blk = lambda off: pl.BlockSpec((tr, cb), lambda j, i: (i, off + j))
    prev = lambda off: pl.BlockSpec((8, cb), lambda j, i: (jnp.maximum(i * (tr // 8) - 1, 0), off + j))
    nxt = lambda off: pl.BlockSpec((8, cb), lambda j, i: (jnp.minimum((i + 1) * (tr // 8), T // 8 - 1), off + j))
    wsp = lambda off: pl.BlockSpec((K, cb), lambda j, i: (0, off + j))
    return pl.pallas_call(
        body, name="ffn_act_bwd", grid=(nc, nb),
        in_specs=[blk(0), prev(0), nxt(0), blk(nc), prev(nc), nxt(nc), blk(0), nxt(0), wsp(0), wsp(nc)],
        out_specs=[blk(0), blk(0), wsp(0), wsp(0)],
        out_shape=[jax.ShapeDtypeStruct((T, F), BF16)] * 2 + [jax.ShapeDtypeStruct((K, F), F32)] * 2,
        compiler_params=_params("parallel", "arbitrary"),
    )(up, up, up, up, up, up, dact, dact, conv_w, conv_w)


def _ple_fwd(hn, w_gate, p, w_ple, x, *, name, tm=1024, tn=512):
    T, Dm = x.shape
    tm = min(tm, T)

    def body(a_ref, b_ref, p_ref, wp_ref, x_ref, o_ref, gl_ref, pe_ref):
        gl = _dot(a_ref[...], b_ref[...], NN)
        pe = _dot(p_ref[...], wp_ref[...], NN)
        gl_ref[...] = gl
        pe_ref[...] = pe
        o_ref[...] = x_ref[...] + pe * _sigmoid(gl)

    o_spec = pl.BlockSpec((tm, tn), lambda i, j: (i, j))
    return pl.pallas_call(
        body, name=name, grid=(T // tm, Dm // tn),
        in_specs=[pl.BlockSpec((tm, Dm), lambda i, j: (i, 0)), pl.BlockSpec((Dm, tn), lambda i, j: (0, j)),
                  pl.BlockSpec((tm, PLE_DIM), lambda i, j: (i, 0)), pl.BlockSpec((PLE_DIM, tn), lambda i, j: (0, j)),
                  o_spec],
        out_specs=[o_spec] * 3, out_shape=[jax.ShapeDtypeStruct((T, Dm), F32)] * 3,
        compiler_params=_params("parallel", "parallel"),
    )(hn, w_gate, p, w_ple, x)


def _ple_bwd(dx, gl, pe, *, name, tr=512):
    T, Dm = dx.shape

    def body(dx_ref, gl_ref, pe_ref, dpe_ref, dgl_ref):
        g = _sigmoid(gl_ref[...])
        d = dx_ref[...]
        dpe_ref[...] = (d * g).astype(BF16)
        dgl_ref[...] = (d * pe_ref[...] * g * (1.0 - g)).astype(BF16)

    row = pl.BlockSpec((tr, Dm), lambda i: (i, 0))
    return pl.pallas_call(
        body, name=name, grid=(T // tr,), in_specs=[row] * 3, out_specs=[row] * 2,
        out_shape=[jax.ShapeDtypeStruct((T, Dm), BF16)] * 2, compiler_params=_params("parallel"),
    )(dx, gl, pe)


def _qkv_act(c, cb):
    s = c * _sigmoid(c)
    n = s * lax.rsqrt(jnp.sum(s * s, axis=-1, keepdims=True) + EPS)
    n = n * jnp.where(cb < GDN_HEADS, GDN_HEAD_DIM ** -0.5, 1.0)
    return jnp.where(cb < 2 * GDN_HEADS, n, s)


GDN_HPS = 4


def _gdn_pre(proj, conv_w, *, tr=512):
    T = proj.shape[0]
    K = GDN_CONV

    def body(x_ref, xh_ref, w_ref, o_ref):
        i, j = pl.program_id(0), pl.program_id(1)
        xh = jnp.where(i > 0, xh_ref[...], 0.0)
        for hh in range(GDN_HPS):
            cs = slice(128 * hh, 128 * (hh + 1))
            c = _conv_rows(jnp.concatenate([xh[:, cs], x_ref[:, cs]], axis=0), w_ref, K, 8, cs)
            o_ref[hh] = _qkv_act(c, GDN_HPS * j + hh)

    wide = 128 * GDN_HPS
    return pl.pallas_call(
        body, name="gdn_pre", grid=(T // tr, 24 // GDN_HPS),
        in_specs=[pl.BlockSpec((tr, wide), lambda i, j: (i, j)), _prev_spec(tr, wide, 8, lambda i, j: j),
                  pl.BlockSpec((K, wide), lambda i, j: (0, j))],
        out_specs=pl.BlockSpec((GDN_HPS, tr, 128), lambda i, j: (j, i, 0)),
        out_shape=jax.ShapeDtypeStruct((24, T, 128), F32), compiler_params=_params("parallel", "parallel"),
    )(proj, proj, conv_w)


def _gdn_pre_bwd(proj, conv_w, dqkv, *, tr=512):
    T = proj.shape[0]
    nb = T // tr
    K = GDN_CONV

    def body(x_ref, xp_ref, xn_ref, d_ref, dn_ref, w_ref, dx_ref, dw_ref):
        j, i = pl.program_id(0), pl.program_id(1)
        first, last = i == 0, i == nb - 1
        xp = jnp.where(first, 0.0, xp_ref[...])
        xn = jnp.where(last, 0.0, xn_ref[...])
        for hh in range(GDN_HPS):
            cs = slice(128 * hh, 128 * (hh + 1))
            x_ext = jnp.concatenate([xp[:, cs], x_ref[:, cs], xn[:, cs]], axis=0)
            d_ext = jnp.concatenate([d_ref[hh], jnp.where(last, 0.0, dn_ref[hh])], axis=0)
            c = _conv_rows(x_ext, w_ref, K, 8, cs)
            _, vjp = jax.vjp(lambda c_: _qkv_act(c_, GDN_HPS * j + hh), c)
            (dc,) = vjp(d_ext)
            dx, dws = _conv_bwd_rows(dc, x_ext, w_ref, K, tr, cs)
            dx_ref[:, cs] = dx.astype(BF16)
            _acc_rows(dw_ref, dws, first, cs)

    wide = 128 * GDN_HPS
    return pl.pallas_call(
        body, name="gdn_pre_bwd", grid=(24 // GDN_HPS, nb),
        in_specs=[pl.BlockSpec((tr, wide), lambda j, i: (i, j)),
                  pl.BlockSpec((8, wide), lambda j, i: (jnp.maximum(i * (tr // 8) - 1, 0), j)),
                  pl.BlockSpec((8, wide), lambda j, i: (jnp.minimum((i + 1) * (tr // 8), T // 8 - 1), j)),
                  pl.BlockSpec((GDN_HPS, tr, 128), lambda j, i: (j, i, 0)),
                  pl.BlockSpec((GDN_HPS, 8, 128), lambda j, i: (j, jnp.minimum((i + 1) * (tr // 8), T // 8 - 1), 0)),
                  pl.BlockSpec((K, wide), lambda j, i: (0, j))],
        out_specs=[pl.BlockSpec((tr, wide), lambda j, i: (i, j)), pl.BlockSpec((K, wide), lambda j, i: (0, j))],
        out_shape=[jax.ShapeDtypeStruct((T, 24 * 128), BF16), jax.ShapeDtypeStruct((K, 24 * 128), F32)],
        compiler_params=_params("parallel", "arbitrary"),
    )(proj, proj, proj, dqkv, dqkv, conv_w)


def _gate_fn(ba, alog_row, dt_row):
    lane = _iota((1, 128), 1)
    x = ba + dt_row
    sp = jnp.maximum(x, 0.0) + jnp.log(1.0 + jnp.exp(-jnp.abs(x)))
    return jnp.where(lane < GDN_HEADS, _sigmoid(ba), -jnp.exp(alog_row) * sp)


def _gdn_gate(ba, alog_row, dt_row, *, tr=512):
    T = ba.shape[0]

    def body(ba_ref, al_ref, dt_ref, b_ref, g_ref):
        val = _gate_fn(ba_ref[...], al_ref[...], dt_ref[...])
        for h in range(GDN_HEADS):
            b_ref[h] = val[:, h:h + 1]
            g_ref[h] = val[:, GDN_HEADS + h:GDN_HEADS + h + 1]

    vec = pl.BlockSpec((1, 128), lambda i: (0, 0))
    hm = pl.BlockSpec((GDN_HEADS, tr, 1), lambda i: (0, i, 0))
    return pl.pallas_call(
        body, name="gdn_gate", grid=(T // tr,), in_specs=[pl.BlockSpec((tr, 128), lambda i: (i, 0)), vec, vec],
        out_specs=[hm, hm], out_shape=[jax.ShapeDtypeStruct((GDN_HEADS, T, 1), F32)] * 2,
        compiler_params=_params("parallel"),
    )(ba, alog_row, dt_row)


def _gdn_gate_bwd(ba, alog_row, dt_row, dbeta, dg, *, tr=512):
    T = ba.shape[0]

    def body(ba_ref, al_ref, dt_ref, db_ref, dg_ref, dba_ref, dal_ref, ddt_ref):
        i = pl.program_id(0)
        lane = _iota((1, 128), 1)
        d = jnp.zeros((tr, 128), F32)
        for h in range(GDN_HEADS):
            d = d + jnp.where(lane == h, db_ref[h], 0.0) + jnp.where(lane == GDN_HEADS + h, dg_ref[h], 0.0)
        _, vjp = jax.vjp(_gate_fn, ba_ref[...], al_ref[...], dt_ref[...])
        dba, dal, ddt = vjp(d)
        dba_ref[...] = dba.astype(BF16)

        @pl.when(i == 0)
        def _():
            dal_ref[...] = dal
            ddt_ref[...] = ddt

        @pl.when(i > 0)
        def _():
            dal_ref[...] += dal
            ddt_ref[...] += ddt

    vec = pl.BlockSpec((1, 128), lambda i: (0, 0))
    hm = pl.BlockSpec((GDN_HEADS, tr, 1), lambda i: (0, i, 0))
    row = pl.BlockSpec((tr, 128), lambda i: (i, 0))
    return pl.pallas_call(
        body, name="gdn_gate_bwd", grid=(T // tr,), in_specs=[row, vec, vec, hm, hm], out_specs=[row, vec, vec],
        out_shape=[jax.ShapeDtypeStruct((T, 128), BF16), jax.ShapeDtypeStruct((1, 128), F32),
                   jax.ShapeDtypeStruct((1, 128), F32)],
        compiler_params=_params("arbitrary"),
    )(ba, alog_row, dt_row, dbeta, dg)


def _split3(x):
    x1 = x.astype(BF16)
    r = x - x1.astype(F32)
    x2 = r.astype(BF16)
    return x1, x2, (r - x2.astype(F32)).astype(BF16)


def _dot01(tri, x, dims):
    t = tri.astype(BF16)
    x1, x2, x3 = _split3(x)
    d = lambda xi: lax.dot_general(t, xi, dims, preferred_element_type=F32)
    return d(x1) + (d(x2) + d(x3))


def _dot3(a, b, dims):
    ah, al, _ = _split3(a)
    bh, bl, _ = _split3(b)
    d = lambda p, q: lax.dot_general(p, q, dims, preferred_element_type=F32)
    return d(ah, bh) + (d(ah, bl) + d(al, bh))


BNN = (((2,), (1,)), ((0,), (0,)))
BNT = (((2,), (2,)), ((0,), (0,)))
BTN = (((1,), (1,)), ((0,), (0,)))


@jax.custom_vjp
def _mm01(tri, x):
    return _dot01(tri, x, BNN)


def _mm01_fwd(tri, x):
    return _dot01(tri, x, BNN), tri


def _mm01_bwd(tri, ct):
    return jnp.zeros_like(tri), _dot01(tri, ct, BTN)


_mm01.defvjp(_mm01_fwd, _mm01_bwd)


def _unit_lower_inverse(a):
    C = a.shape[-1]
    eye = (_iota(a.shape, 1) == _iota(a.shape, 2)).astype(F32)
    pw = -a
    tinv = eye + pw
    for _ in range(5):
        pw = _dot3(pw, pw, BNN)
        tinv = tinv + _dot3(tinv, pw, BNN)
    return tinv


@jax.custom_vjp
def _unit_lower_solve(a, rv, rw):
    return _unit_lower_solve_fwd(a, rv, rw)[0]


def _unit_lower_solve_fwd(a, rv, rw):
    tinv = _unit_lower_inverse(a)
    sol = _dot3(tinv, jnp.concatenate([rv, rw], axis=2), BNN)
    n = rv.shape[2]
    return (sol[:, :, :n], sol[:, :, n:]), (tinv, sol)


def _unit_lower_solve_bwd(res, cts):
    tinv, sol = res
    n = cts[0].shape[2]
    d_rhs = _dot3(tinv, jnp.concatenate(cts, axis=2), BTN)
    return -_dot3(d_rhs, sol, BNT), d_rhs[:, :, :n], d_rhs[:, :, n:]


_unit_lower_solve.defvjp(_unit_lower_solve_fwd, _unit_lower_solve_bwd)


@jax.custom_vjp
def _mmb_nt(a, b):
    return _dot(a, b, BNT)


def _mmb_nt_fwd(a, b):
    return _dot(a, b, BNT), (a, b)


def _mmb_nt_bwd(res, ct):
    a, b = res
    return _dot(ct, b, BNN), _dot(ct, a, BTN)


_mmb_nt.defvjp(_mmb_nt_fwd, _mmb_nt_bwd)


def _gdn_chunk(q, k, v, gcol, bcol):
    nb, C = q.shape[0], GDN_CHUNK
    row, col = _iota((nb, C, C), 1), _iota((nb, C, C), 2)
    incl, strict = row >= col, row > col
    eye = (row == col).astype(F32)
    lower = incl.astype(F32)
    ones = jnp.ones((nb, C, C), F32)
    gwide = jnp.broadcast_to(gcol, (nb, C, GDN_HEAD_DIM))
    gc = _mm01(lower, gwide)
    gtot = _mm01(ones, gwide)
    gc_c = _mm01(lower, jnp.broadcast_to(gcol, (nb, C, C)))
    gc_s = _mm01(ones, gc_c * eye)
    decay = jnp.where(incl, jnp.exp(jnp.where(incl, gc_c - gc_s, 0.0)), 0.0)
    kb = k * bcol
    a = jnp.where(strict, _mmb_nt(kb, k) * decay, 0.0)
    egc = jnp.exp(gc)
    u, w = _unit_lower_solve(a, v * bcol, kb * egc)
    qk = jnp.where(incl, _mmb_nt(q, k) * decay, 0.0)
    return u, w, qk, q * egc, k * jnp.exp(gtot - gc), jnp.exp(jnp.sum(gwide, axis=1))


GDN_ROWS = 8 * GDN_CHUNK


GDN_LOCAL_CHUNKS = 32


def _gdn_specs(T):
    nch = min(GDN_LOCAL_CHUNKS, T // GDN_CHUNK)
    L = nch * GDN_CHUNK
    hd = lambda off: pl.BlockSpec((1, L, 128), lambda h, i: (off + h, i, 0))
    col = pl.BlockSpec((1, L, 1), lambda h, i: (h, i, 0))
    sq = pl.BlockSpec((1, L, GDN_CHUNK), lambda h, i: (h, i, 0))
    gl = pl.BlockSpec((1, nch, 128), lambda h, i: (h, i, 0))
    return nch, hd, col, sq, gl


def _gdn_local(qkv, g, beta):
    T = qkv.shape[1]
    nch, hd, col, sq, gl_spec = _gdn_specs(T)

    def body(q_ref, k_ref, v_ref, g_ref, b_ref, u_ref, w_ref, qk_ref, qd_ref, kd_ref, gl_ref):
        chunks = lambda ref: ref[0].reshape(nch, GDN_CHUNK, ref.shape[2])
        rows = lambda val: val.reshape(nch * GDN_CHUNK, val.shape[2])
        u, w, qk, qd, kd, gl = _gdn_chunk(chunks(q_ref), chunks(k_ref), chunks(v_ref), chunks(g_ref), chunks(b_ref))
        u_ref[0] = rows(u)
        w_ref[0] = rows(w).astype(BF16)
        qk_ref[0] = rows(qk).astype(BF16)
        qd_ref[0] = rows(qd).astype(BF16)
        kd_ref[0] = rows(kd).astype(BF16)
        gl_ref[0] = gl

    H = GDN_HEADS
    return pl.pallas_call(
        body, name="gdn_local", grid=(H, T // (nch * GDN_CHUNK)),
        in_specs=[hd(0), hd(H), hd(2 * H), col, col],
        out_specs=[hd(0), hd(0), sq, hd(0), hd(0), gl_spec],
        out_shape=[jax.ShapeDtypeStruct((H, T, 128), F32), jax.ShapeDtypeStruct((H, T, 128), BF16),
                   jax.ShapeDtypeStruct((H, T, GDN_CHUNK), BF16), jax.ShapeDtypeStruct((H, T, 128), BF16),
                   jax.ShapeDtypeStruct((H, T, 128), BF16), jax.ShapeDtypeStruct((H, T // GDN_CHUNK, 128), F32)],
        compiler_params=_params("parallel", "parallel"),
    )(qkv, qkv, qkv, g, beta)


def _gdn_local_bwd(qkv, g, beta, du, dw, dqk, dqd, dkd, dgl):
    T = qkv.shape[1]
    nch, hd, col, sq, gl_spec = _gdn_specs(T)

    def body(q_ref, k_ref, v_ref, g_ref, b_ref, du_ref, dw_ref, dqk_ref, dqd_ref, dkd_ref, dgl_ref,
             dqkv_ref, dg_ref, db_ref):
        chunks = lambda ref: ref[0].reshape(nch, GDN_CHUNK, ref.shape[2])
        rows = lambda val: val.reshape(nch * GDN_CHUNK, val.shape[2])
        _, vjp = jax.vjp(_gdn_chunk, chunks(q_ref), chunks(k_ref), chunks(v_ref), chunks(g_ref), chunks(b_ref))
        dq, dk, dv, dg, db = vjp((chunks(du_ref), chunks(dw_ref), chunks(dqk_ref), chunks(dqd_ref), chunks(dkd_ref),
                                  dgl_ref[0]))
        dqkv_ref[0, 0] = rows(dq)
        dqkv_ref[1, 0] = rows(dk)
        dqkv_ref[2, 0] = rows(dv)
        dg_ref[0] = rows(dg)
        db_ref[0] = rows(db)

    H = GDN_HEADS
    small = jax.ShapeDtypeStruct((H, T, 1), F32)
    dqkv, dg, db = pl.pallas_call(
        body, name="gdn_local_bwd", grid=(H, T // (nch * GDN_CHUNK)),
        in_specs=[hd(0), hd(H), hd(2 * H), col, col, hd(0), hd(0), sq, hd(0), hd(0), gl_spec],
        out_specs=[pl.BlockSpec((3, 1, nch * GDN_CHUNK, 128), lambda h, i: (0, h, i, 0)), col, col],
        out_shape=[jax.ShapeDtypeStruct((3, H, T, 128), F32), small, small],
        compiler_params=_params("parallel", "parallel"),
    )(qkv, qkv, qkv, g, beta, du, dw, dqk, dqd, dkd, dgl)
    return dqkv.reshape(3 * H, T, 128), dg, db


GDN_HB = 8


def _gdn_scan_specs(T, rev):
    nb = T // GDN_ROWS
    blk = (lambda i: nb - 1 - i) if rev else (lambda i: i)
    hd = pl.BlockSpec((GDN_HB, GDN_ROWS, 128), lambda h, i: (h, blk(i), 0))
    sq = pl.BlockSpec((GDN_HB, GDN_ROWS, GDN_CHUNK), lambda h, i: (h, blk(i), 0))
    gl = pl.BlockSpec((GDN_HB, 8, 128), lambda h, i: (h, blk(i), 0))
    st = pl.BlockSpec((GDN_HB, 8, 128, 128), lambda h, i: (h, blk(i), 0, 0))
    return hd, sq, gl, st


def _gdn_scan(u, w, qk, qd, kd, gl):
    H, T, _ = u.shape
    hd, sq, gl_spec, st = _gdn_scan_specs(T, False)

    def body(u_ref, w_ref, qk_ref, qd_ref, kd_ref, gl_ref, o_ref, ss_ref, vn_ref, s_scr):
        @pl.when(pl.program_id(1) == 0)
        def _():
            s_scr[...] = jnp.zeros_like(s_scr)

        dot = lambda a, b, dims: lax.dot_general(a, b, dims, preferred_element_type=F32)
        s = s_scr[...]
        for c in range(8):
            rs = slice(GDN_CHUNK * c, GDN_CHUNK * (c + 1))
            ss_ref[:, c] = s
            sb = s.astype(BF16)
            vn = u_ref[:, rs, :] - dot(w_ref[:, rs, :], sb, BNN)
            vnb = vn.astype(BF16)
            o_ref[:, rs, :] = dot(qd_ref[:, rs, :], sb, BNN) + dot(qk_ref[:, rs, :], vnb, BNN)
            vn_ref[:, rs, :] = vnb
            s = s * gl_ref[:, c:c + 1, :] + dot(kd_ref[:, rs, :], vnb, BTN)
        s_scr[...] = s

    return pl.pallas_call(
        body, name="gdn_scan", grid=(H // GDN_HB, T // GDN_ROWS),
        in_specs=[hd, hd, sq, hd, hd, gl_spec], out_specs=[hd, st, hd],
        out_shape=[jax.ShapeDtypeStruct((H, T, 128), F32), jax.ShapeDtypeStruct((H, T // GDN_CHUNK, 128, 128), F32),
                   jax.ShapeDtypeStruct((H, T, 128), BF16)],
        scratch_shapes=[pltpu.VMEM((GDN_HB, 128, 128), F32)],
        compiler_params=_params("parallel", "arbitrary"),
    )(u, w, qk, qd, kd, gl)


def _gdn_scan_bwd(do, ss, vn, w, qk, qd, kd, gl):
    H, T, _ = do.shape
    hd, sq, gl_spec, st = _gdn_scan_specs(T, True)

    def body(do_ref, ss_ref, vn_ref, w_ref, qk_ref, qd_ref, kd_ref, gl_ref,
             du_ref, dw_ref, dqk_ref, dqd_ref, dkd_ref, dgl_ref, ds_scr):
        @pl.when(pl.program_id(1) == 0)
        def _():
            ds_scr[...] = jnp.zeros_like(ds_scr)

        dot = lambda a, b, dims: lax.dot_general(a, b, dims, preferred_element_type=F32)
        ds = ds_scr[...]
        for c in reversed(range(8)):
            rs = slice(GDN_CHUNK * c, GDN_CHUNK * (c + 1))
            s = ss_ref[:, c]
            sb, dsb = s.astype(BF16), ds.astype(BF16)
            dob = do_ref[:, rs, :].astype(BF16)
            vnb = vn_ref[:, rs, :]
            dvn = dot(qk_ref[:, rs, :], dob, BTN) + dot(kd_ref[:, rs, :], dsb, BNN)
            dvnb = dvn.astype(BF16)
            du_ref[:, rs, :] = dvn
            dw_ref[:, rs, :] = -dot(dvnb, sb, BNT)
            dqk_ref[:, rs, :] = dot(dob, vnb, BNT)
            dqd_ref[:, rs, :] = dot(dob, sb, BNT)
            dkd_ref[:, rs, :] = dot(vnb, dsb, BNT)
            dgl_ref[:, c:c + 1, :] = jnp.sum(ds * s, axis=1, keepdims=True)
            ds = dot(qd_ref[:, rs, :], dob, BTN) + ds * gl_ref[:, c:c + 1, :] - dot(w_ref[:, rs, :], dvnb, BTN)
        ds_scr[...] = ds

    big = jax.ShapeDtypeStruct((H, T, 128), F32)
    return pl.pallas_call(
        body, name="gdn_scan_bwd", grid=(H // GDN_HB, T // GDN_ROWS),
        in_specs=[hd, st, hd, hd, sq, hd, hd, gl_spec], out_specs=[hd, hd, sq, hd, hd, gl_spec],
        out_shape=[big, big, jax.ShapeDtypeStruct((H, T, GDN_CHUNK), F32), big, big,
                   jax.ShapeDtypeStruct((H, T // GDN_CHUNK, 128), F32)],
        scratch_shapes=[pltpu.VMEM((GDN_HB, 128, 128), F32)],
        compiler_params=_params("parallel", "arbitrary"),
    )(do, ss, vn, w, qk, qd, kd, gl)


def _gated_norm(o, z, nw):
    on = o * lax.rsqrt(jnp.mean(o * o, axis=-1, keepdims=True) + EPS) * nw
    return on * (z * _sigmoid(z))


def _gdn_post(o, proj, norm_w, *, tr=512):
    T = proj.shape[0]

    def body(o_ref, z_ref, n_ref, y_ref):
        y_ref[...] = _gated_norm(o_ref[0], z_ref[...], n_ref[...]).astype(BF16)

    return pl.pallas_call(
        body, name="gdn_post", grid=(T // tr, GDN_HEADS),
        in_specs=[pl.BlockSpec((1, tr, 128), lambda i, h: (h, i, 0)), pl.BlockSpec((tr, 128), lambda i, h: (i, 24 + h)),
                  pl.BlockSpec((1, 128), lambda i, h: (0, 0))],
        out_specs=pl.BlockSpec((tr, 128), lambda i, h: (i, h)),
        out_shape=jax.ShapeDtypeStruct((T, 1024), BF16), compiler_params=_params("parallel", "parallel"),
    )(o, proj, norm_w)


def _gdn_post_bwd(o, proj, norm_w, dy, *, tr=512):
    T = proj.shape[0]

    def body(o_ref, z_ref, n_ref, dy_ref, do_ref, dz_ref, dn_ref):
        first = (pl.program_id(0) == 0) & (pl.program_id(1) == 0)
        _, vjp = jax.vjp(_gated_norm, o_ref[0], z_ref[...], n_ref[...])
        do, dz, dn = vjp(dy_ref[...])
        do_ref[0] = do
        dz_ref[...] = dz.astype(BF16)

        @pl.when(first)
        def _():
            dn_ref[...] = dn

        @pl.when(jnp.logical_not(first))
        def _():
            dn_ref[...] += dn

    blk = pl.BlockSpec((tr, 128), lambda i, h: (i, h))
    hm = pl.BlockSpec((1, tr, 128), lambda i, h: (h, i, 0))
    vec = pl.BlockSpec((1, 128), lambda i, h: (0, 0))
    return pl.pallas_call(
        body, name="gdn_post_bwd", grid=(T // tr, GDN_HEADS),
        in_specs=[hm, pl.BlockSpec((tr, 128), lambda i, h: (i, 24 + h)), vec, blk], out_specs=[hm, blk, vec],
        out_shape=[jax.ShapeDtypeStruct((GDN_HEADS, T, 128), F32), jax.ShapeDtypeStruct((T, 1024), BF16),
                   jax.ShapeDtypeStruct((1, 128), F32)],
        compiler_params=_params("arbitrary", "arbitrary"),
    )(o, proj, norm_w, dy)


HBM_SPEC = pl.BlockSpec(memory_space=pltpu.HBM)


def _place():
    return lax.axis_index("x"), lax.axis_index("y"), lax.axis_index("c")


def _all_gather(vs, *, name):
    n = len(vs)

    def body(*refs):
        start, forward, finish = _gather_stages(refs[:n], refs[n:2 * n], *refs[2 * n:])
        start()
        forward()
        finish()

    return pl.pallas_call(
        body, name=name, out_shape=_gather_shapes(vs), in_specs=[HBM_SPEC] * n, out_specs=[HBM_SPEC] * n,
        scratch_shapes=_gather_sems(n),
    )(*vs)


def _gather_shapes(vs):
    return [jax.ShapeDtypeStruct((N_DEV,) + v.shape, v.dtype) for v in vs]


def _gather_sems(n):
    return [pltpu.SemaphoreType.DMA((7 * n,)), pltpu.SemaphoreType.DMA((7 * n,)), pltpu.SemaphoreType.DMA((n,))]


def _gather_stages(v_refs, out_refs, send_sems, recv_sems, local_sems):
    n = len(v_refs)
    x, y, c = _place()
    me, sibling = (x, y, c), (x, y, 1 - c)
    chips = [(1 - x, y), (x, 1 - y), (1 - x, 1 - y)]

    def copy(a, k, block, to, from_input=False):
        slot = out_refs[a].at[4 * block[0] + 2 * block[1] + block[2]]
        return pltpu.make_async_remote_copy(
            src_ref=v_refs[a] if from_input else slot, dst_ref=slot,
            send_sem=send_sems.at[7 * a + k], recv_sem=recv_sems.at[7 * a + k], device_id=to, device_id_type=MESH)

    def mine():
        return [pltpu.make_async_copy(v_refs[a], out_refs[a].at[4 * x + 2 * y + c], local_sems.at[a]) for a in range(n)]

    def first():
        return ([copy(a, 0, me, sibling, True) for a in range(n)]
                + [copy(a, 1 + j, me, (*chip, c), True) for j, chip in enumerate(chips) for a in range(n)])

    def passed():
        return [copy(a, 4 + j, (*chip, c), sibling) for j, chip in enumerate(chips) for a in range(n)]

    def start():
        for cp in mine() + first():
            cp.start()

    def forward():
        for j, chip in enumerate(chips):
            for a in range(n):
                copy(a, 1 + j, (*chip, c), me).wait_recv()
                copy(a, 4 + j, (*chip, c), sibling).start()

    def finish():
        for a in range(n):
            copy(a, 0, sibling, me).wait_recv()
            for j, chip in enumerate(chips):
                copy(a, 4 + j, (*chip, 1 - c), me).wait_recv()
        for cp in first() + passed():
            cp.wait_send()
        for cp in mine():
            cp.wait()

    return start, forward, finish


def _exchange_sibling(gs, *, name):
    n = len(gs)

    def body(*refs):
        g_refs, out_refs = refs[:n], refs[n:2 * n]
        send_sems, recv_sems = refs[2 * n:]
        x, y, c = _place()
        copies = [pltpu.make_async_remote_copy(
            src_ref=g_refs[a].at[k, 1 - c], dst_ref=out_refs[a].at[k], send_sem=send_sems.at[4 * a + k],
            recv_sem=recv_sems.at[4 * a + k], device_id=(x, y, 1 - c), device_id_type=MESH)
            for a in range(n) for k in range(4)]
        for cp in copies:
            cp.start()
        for cp in copies:
            cp.wait()

    return pl.pallas_call(
        body, name=name, out_shape=[jax.ShapeDtypeStruct((4,) + g.shape[2:], g.dtype) for g in gs],
        in_specs=[HBM_SPEC] * n, out_specs=[HBM_SPEC] * n,
        scratch_shapes=[pltpu.SemaphoreType.DMA((4 * n,)), pltpu.SemaphoreType.DMA((4 * n,))],
    )(*gs)


def _exchange_chips(pcs):
    n = len(pcs)

    def body(*refs):
        start, finish = _chips_stages(refs[:n], refs[n:2 * n], *refs[2 * n:])
        start()
        finish()

    return pl.pallas_call(
        body, name="rs_chips", out_shape=_chips_shapes(pcs), in_specs=[HBM_SPEC] * n, out_specs=[HBM_SPEC] * n,
        scratch_shapes=_chips_sems(n),
    )(*pcs)


def _chips_shapes(pcs):
    return [jax.ShapeDtypeStruct((3,) + pc.shape[1:], pc.dtype) for pc in pcs]


def _chips_sems(n):
    return [pltpu.SemaphoreType.DMA((3 * n,)), pltpu.SemaphoreType.DMA((3 * n,))]


def _chips_stages(p_refs, out_refs, send_sems, recv_sems):
    n = len(p_refs)
    x, y, c = _place()
    chips = [(1 - x, y), (x, 1 - y), (1 - x, 1 - y)]

    def copies():
        return [pltpu.make_async_remote_copy(
            src_ref=p_refs[a].at[2 * cx + cy], dst_ref=out_refs[a].at[j], send_sem=send_sems.at[3 * a + j],
            recv_sem=recv_sems.at[3 * a + j], device_id=(cx, cy, c), device_id_type=MESH)
            for j, (cx, cy) in enumerate(chips) for a in range(n)]

    def start():
        for cp in copies():
            cp.start()

    def finish():
        for cp in copies():
            cp.wait()

    return start, finish


def _chip_partial(place, g, got, *, tr, name):
    R, W = g.shape[2:]

    def body(pl_ref, g_ref, r_ref, o_ref):
        o_ref[...] = (g_ref[0] + r_ref[...]).astype(BF16)

    def chip(j, pr):
        return 2 * jnp.where(j == 1, pr[0], 1 - pr[0]) + jnp.where(j == 0, pr[1], 1 - pr[1])

    return pl.pallas_call(
        body, name=name, out_shape=jax.ShapeDtypeStruct((4, R, W), BF16),
        grid_spec=pltpu.PrefetchScalarGridSpec(
            num_scalar_prefetch=1, grid=(3, R // tr),
            in_specs=[pl.BlockSpec((1, 1, tr, W), lambda j, i, pr: (chip(j, pr), pr[2], i, 0)),
                      pl.BlockSpec((1, tr, W), lambda j, i, pr: (chip(j, pr), i, 0))],
            out_specs=pl.BlockSpec((1, tr, W), lambda j, i, pr: (chip(j, pr), i, 0))),
        compiler_params=_params("parallel", "parallel"),
    )(place, g, got)


def _adamw_math(g, w, m, v):
    m = ADAM_B1 * m + (1.0 - ADAM_B1) * g
    v = ADAM_B2 * v + (1.0 - ADAM_B2) * (g * g)
    m_hat = m / (1.0 - ADAM_B1 ** ADAM_STEP)
    v_hat = v / (1.0 - ADAM_B2 ** ADAM_STEP)
    return -ADAM_LR * (m_hat / (jnp.sqrt(v_hat) + ADAM_EPS) + ADAM_WD * w), m, v


def _adamw_shard(place, g, got1, got2, w, m, v, *, tr, name):
    R, W = w.shape

    def body(pl_ref, g_ref, r1_ref, r2_ref, w_ref, m_ref, v_ref, go_ref, d_ref, mo_ref, vo_ref):
        gs = g_ref[0, 0] + r1_ref[0]
        for j in range(3):
            gs = gs + r2_ref[j].astype(F32)
        go_ref[...] = gs
        d_ref[...], mo_ref[...], vo_ref[...] = _adamw_math(gs, w_ref[...], m_ref[...], v_ref[...])

    row = pl.BlockSpec((tr, W), lambda i, pr: (i, 0))
    out = jax.ShapeDtypeStruct((R, W), F32)
    return pl.pallas_call(
        body, name=name, out_shape=[out] * 4,
        grid_spec=pltpu.PrefetchScalarGridSpec(
            num_scalar_prefetch=1, grid=(R // tr,),
            in_specs=[pl.BlockSpec((1, 1, tr, W), lambda i, pr: (2 * pr[0] + pr[1], pr[2], i, 0)),
                      pl.BlockSpec((1, tr, W), lambda i, pr: (2 * pr[0] + pr[1], i, 0)),
                      pl.BlockSpec((3, tr, W), lambda i, pr: (0, i, 0)), row, row, row],
            out_specs=[row] * 4),
        compiler_params=_params("parallel"),
    )(place, g, got1, got2, w, m, v)


def _adamw_replicated(parts, w, m, v):
    R, W = w.shape

    def body(p_ref, w_ref, m_ref, v_ref, go_ref, d_ref, mo_ref, vo_ref):
        gs = p_ref[0]
        for j in range(1, N_DEV):
            gs = gs + p_ref[j]
        go_ref[...] = gs
        d_ref[...], mo_ref[...], vo_ref[...] = _adamw_math(gs, w_ref[...], m_ref[...], v_ref[...])

    full = pl.BlockSpec((R, W), lambda i: (0, 0))
    out = jax.ShapeDtypeStruct((R, W), F32)
    return pl.pallas_call(
        body, name="adamw_replicated", grid=(1,), out_shape=[out] * 4,
        in_specs=[pl.BlockSpec((N_DEV, R, W), lambda i: (0, 0, 0)), full, full, full], out_specs=[full] * 4,
        compiler_params=_params("arbitrary"),
    )(parts, w, m, v)


GROUPS = {
    "in_e": (256, 512, ((("w_in_e", None), 1024, 1024),)),
    "in_o": (514, 512, ((("w_in_o", None), 1024, 1024),)),
    "up0": (704, 256, ((("w_up", 0), 1024, 1024),)),
    "up1": (704, 256, ((("w_up", 1), 1024, 1024),)),
    "down0": (1024, 176, ((("w_down", 0), 352, 352),)),
    "down1": (1024, 176, ((("w_down", 1), 352, 352),)),
    "square": (1024, 256, ((("w_out_e", None), 128, 128), (("w_out_o", None), 128, 128), (("w_ple_gate", None), 256, 256))),
    "ple": (128, 512, ((("w_ple", None), 512, 512),)),
    "conv_f": (704, 8, ((("ffn_conv", None), 6, 8),)),
    "norm_o": (128, 8, ((("mix_norm_o", None), 1, 8),)),
    "conv_o": (384, 8, ((("conv_qkv_o", None), 4, 8),)),
}
SHARDED = tuple(dict.fromkeys(p[0][0] for g in GROUPS.values() for p in g[2]))
COLUMN_SHARDED = ("w_in_e", "w_in_o", "w_up", "ffn_conv", "w_ple", "conv_qkv_o", "mix_norm_o")
PACK_W = 1024
REPL_LAYOUT = (
    ("mix_norm_e", (1, 1024), 8), ("pool_w", (1, 4, 128, 128), 64), ("pool_scale", (1, 512), 8),
    ("a_log_o", (1, 8), 8), ("dt_bias_o", (1, 8), 8), ("gdn_norm_o", (1, 128), 8),
    ("ffn_norm", (2, 1024), 8), ("ple_norm", (2, 1024), 8), ("final_norm", (1024,), 8),
)


def _pad_rows(a, rows):
    extra = rows - a.shape[-2]
    return a if extra == 0 else jnp.pad(a, [(0, 0)] * (a.ndim - 2) + [(0, extra), (0, 0)])


def _group_rows(pieces, gname):
    parts = [_pad_rows(pieces[name], padded) for name, _, padded in GROUPS[gname][2]]
    return parts[0] if len(parts) == 1 else jnp.concatenate(parts, axis=-2)


def _ungroup_rows(buf, gname):
    out, r0 = {}, 0
    for name, rows, padded in GROUPS[gname][2]:
        out[name] = buf[..., r0:r0 + rows, :]
        r0 += padded
    return out


def _shard_major(name, gfull, n_layers):
    per_layer = []
    for g in gfull:
        if g.ndim == 3:
            per_layer.append(g)
        elif name in COLUMN_SHARDED:
            k = g.shape[0]
            per_layer.append(jnp.moveaxis(g.reshape(k, N_DEV, g.shape[1] // N_DEV), 1, 0))
        else:
            per_layer.append(g.reshape(N_DEV, g.shape[0] // N_DEV, -1))
    return per_layer[0] if n_layers == 1 else jnp.concatenate(per_layer, axis=1)


def _natural(name, gathered, n_layers):
    rows = gathered.shape[1] // n_layers
    out = []
    for layer in range(n_layers):
        piece = gathered[:, layer * rows:(layer + 1) * rows]
        if name in COLUMN_SHARDED:
            out.append(jnp.moveaxis(piece, 0, 1).reshape(rows, N_DEV * piece.shape[2]))
        else:
            out.append(piece.reshape(N_DEV * rows, piece.shape[2]))
    return out


def _unshard_pairs(g, layer, rows, tag):
    n = g.shape[2]

    def body(g_ref, o_ref):
        o_ref[...] = jnp.concatenate([g_ref[0], g_ref[1]], axis=1)

    return pl.pallas_call(
        body, name="unshard_%s%d" % (tag, layer), grid=(N_DEV // 2,),
        in_specs=[pl.BlockSpec((2, rows, n), lambda p: (p, layer, 0))],
        out_specs=pl.BlockSpec((rows, 2 * n), lambda p: (0, p)),
        out_shape=jax.ShapeDtypeStruct((rows, N_DEV * n), g.dtype), compiler_params=_params("parallel"),
    )(g)


def _unshard_in_o(g, *, tr=256):
    rows, n = g.shape[1:]
    main = (N_DEV * n) // 128 * 128

    def body(g_ref, m_ref, b_ref):
        cat = jnp.concatenate([g_ref[i] for i in range(N_DEV)], axis=1)
        m_ref[...] = cat[:, :main]
        b_ref[...] = jnp.concatenate([cat[:, main:], jnp.zeros((tr, 128 - (N_DEV * n - main)), g.dtype)], axis=1)

    return pl.pallas_call(
        body, name="unshard_in_o", grid=(rows // tr,),
        in_specs=[pl.BlockSpec((N_DEV, tr, n), lambda p: (0, p, 0))],
        out_specs=[pl.BlockSpec((tr, main), lambda p: (p, 0)), pl.BlockSpec((tr, 128), lambda p: (p, 0))],
        out_shape=[jax.ShapeDtypeStruct((rows, main), g.dtype), jax.ShapeDtypeStruct((rows, 128), g.dtype)],
        compiler_params=_params("parallel"),
    )(g)


def _reshard_in_o(d_main, d_ba, n, *, tr=256):
    rows, main = d_main.shape
    tail = N_DEV * n - main

    def body(m_ref, b_ref, o_ref):
        cat = jnp.concatenate([m_ref[...], b_ref[:, :tail]], axis=1)
        for i in range(N_DEV):
            o_ref[i] = cat[:, n * i:n * (i + 1)]

    return pl.pallas_call(
        body, name="reshard_in_o", grid=(rows // tr,),
        in_specs=[pl.BlockSpec((tr, main), lambda p: (p, 0)), pl.BlockSpec((tr, 128), lambda p: (p, 0))],
        out_specs=pl.BlockSpec((N_DEV, tr, n), lambda p: (0, p, 0)),
        out_shape=jax.ShapeDtypeStruct((N_DEV, rows, n), F32), compiler_params=_params("parallel"),
    )(d_main, d_ba)


def _rows(a, rows):
    flat = a.reshape(-1)
    return jnp.pad(flat, (0, rows * PACK_W - flat.shape[0])).reshape(rows, PACK_W)


def _pack_repl(vals):
    return jnp.concatenate([_rows(vals[name].reshape(shape), rows) for name, shape, rows in REPL_LAYOUT], axis=0)


def _unpack_repl(buf):
    out, r0 = {}, 0
    for name, shape, rows in REPL_LAYOUT:
        n = 1
        for s in shape:
            n *= s
        out[name] = buf[r0:r0 + rows].reshape(-1)[:n].reshape(shape)
        r0 += rows
    return out


WEIGHTS = ("mix_norm_e", "w_in_e", "pool_w", "pool_scale", "w_out_e", "mix_norm_o", "w_in_o", "conv_qkv_o", "a_log_o",
           "dt_bias_o", "gdn_norm_o", "w_out_o", "ffn_norm", "w_up", "ffn_conv", "w_down", "ple_norm", "w_ple_gate",
           "w_ple", "final_norm")


def _ffn_forward(x, hn, w_up, conv_w, w_down, next_gain, tag):
    up = _mm(hn, w_up, name="ffn_up" + tag)
    act = _ffn_act(up, conv_w)
    out, out_n = _mm(act, w_down, res=x, norm_gain=next_gain, name="ffn_down" + tag)
    return out, out_n, (x, hn, up, act)


def _ffn_backward(dx, saved, norm_g, w_up, conv_w, w_down, tag):
    x, hn, up, act = saved
    dact = _mm(dx, w_down, tb=True, name="ffn_dact" + tag)
    d_w_down = _mm(act, dx, ta=True, name="ffn_dwdown" + tag)
    dgate, dval, dcg, dcv = _ffn_act_bwd(up, conv_w, dact)
    dhn = _mm(dgate, w_up, tb=True, name="ffn_dhn_g" + tag)
    dx_in, d_norm = _mm(dval, w_up, tb=True, k_block0=FFN_DIM // _mm_tile(FFN_DIM), res=dhn,
                        rms_bwd=(x, norm_g, dx), name="ffn_dhn_v" + tag)
    shard = w_up.shape[1] // N_DEV
    d_w_up = _mm(hn, dgate, ta=True, shard_cols=shard, shard_into=(N_DEV, 0, None), name="ffn_dwup_g" + tag)
    d_w_up = _mm(hn, dval, ta=True, shard_cols=shard, shard_into=(N_DEV, N_DEV // 2, d_w_up), name="ffn_dwup_v" + tag)
    return dx_in, d_norm, d_w_up, jnp.concatenate([dcg, dcv], axis=1), d_w_down


def _ple_forward(x, hn, w_gate, p, w_ple, tag):
    out, gl, pe = _ple_fwd(hn, w_gate, p, w_ple, x, name="ple_fwd" + tag)
    return out, (x, hn, gl, pe)


def _ple_backward(dx, saved, norm_g, w_gate, p, tag):
    x, hn, gl, pe = saved
    dpe, dgl = _ple_bwd(dx, gl, pe, name="ple_bwd" + tag)
    d_w_ple = _mm(p, dpe, ta=True, shard_cols=dpe.shape[1] // N_DEV, name="ple_dwple" + tag)
    d_w_gate = _mm(hn, dgl, ta=True, name="ple_dwgate" + tag)
    dx, d_norm = _mm(dgl, w_gate, tb=True, rms_bwd=(x, norm_g, dx), name="ple_dhn" + tag)
    return dx, d_norm, d_w_gate, d_w_ple


def kernel(x, p, mix_norm_e, w_in_e, pool_w, pool_scale, w_out_e, mix_norm_o, w_in_o, conv_qkv_o, a_log_o, dt_bias_o, gdn_norm_o, w_out_o, ffn_norm, w_up, ffn_conv, w_down, ple_norm, w_ple_gate, w_ple, final_norm, loss_target, m_mix_norm_e, m_w_in_e, m_pool_w, m_pool_scale, m_w_out_e, m_mix_norm_o, m_w_in_o, m_conv_qkv_o, m_a_log_o, m_dt_bias_o, m_gdn_norm_o, m_w_out_o, m_ffn_norm, m_w_up, m_ffn_conv, m_w_down, m_ple_norm, m_w_ple_gate, m_w_ple, m_final_norm, v_mix_norm_e, v_w_in_e, v_pool_w, v_pool_scale, v_w_out_e, v_mix_norm_o, v_w_in_o, v_conv_qkv_o, v_a_log_o, v_dt_bias_o, v_gdn_norm_o, v_w_out_o, v_ffn_norm, v_w_up, v_ffn_conv, v_w_down, v_ple_norm, v_w_ple_gate, v_w_ple, v_final_norm):
    given = dict(locals())
    place = jnp.stack(_place()).astype(jnp.int32)
    x0, tgt = x[0], loss_target[0]

    def pieces(prefix):
        out = {}
        for width, _, members in GROUPS.values():
            for (name, layer), rows, _ in members:
                a = given[prefix + name]
                out[(name, layer)] = (a if layer is None else a[layer]).reshape(rows, width)
        return out

    def flat2d(name):
        return given[name].reshape(-1, given[name].shape[-1])

    small = ("ffn_conv", "mix_norm_o", "conv_qkv_o")
    got = _all_gather([flat2d("w_in_e").astype(BF16)] + [_pad_rows(flat2d(k), 8) for k in small], name="ag_first")
    full = {("w_in_e", 0): _unshard_pairs(got[0], 0, D_MODEL, "in_e")}
    for i in range(2):
        full[("ffn_conv", i)] = _natural("ffn_conv", got[1][:, 3 * i:3 * i + 3], 1)[0]
    mix_norm_o_full = got[2][:, 0].reshape(1, D_MODEL)
    conv_qkv = _natural("conv_qkv_o", got[3][:, :4], 1)[0]
    alog_row = jnp.pad(a_log_o, ((0, 0), (8, 112)))
    dt_row = jnp.pad(dt_bias_o, ((0, 0), (8, 112)))
    lw = lambda name, i: full[(name, i)]

    h_e = _rms_fwd(x0, mix_norm_e, name="rms_mix_e")
    proj_e = _mm(h_e, lw("w_in_e", 0), name="in_e")
    pool_o = _pool_fwd(proj_e, pool_w[0], pool_scale)
    wide = ("w_out_e", "w_out_o", "w_down", "w_ple_gate")
    send = [jnp.concatenate([flat2d(k).astype(BF16) for k in wide], axis=0)]
    att_o, lsum, got = _sb_fwd(proj_e, gather=send + [flat2d(k).astype(BF16) for k in ("w_in_o", "w_up", "w_ple")])
    gathered, r0 = {}, 0
    for k in wide:
        gathered[k] = got[0][:, r0:r0 + flat2d(k).shape[0]]
        r0 += flat2d(k).shape[0]
    layers = {name: given[name].shape[0] if given[name].ndim == 3 else 1 for name in gathered}
    full.update({(name, i): w for name in gathered for i, w in enumerate(_natural(name, gathered[name], layers[name]))})
    w_in_o_main, w_in_o_ba = _unshard_in_o(got[1])
    for i in range(2):
        full[("w_up", i)] = _unshard_pairs(got[2], i, D_MODEL, "up")
        full[("w_ple", i)] = _unshard_pairs(got[3], i, PLE_DIM, "ple")
    mix_e = jnp.concatenate([pool_o, att_o.astype(BF16)], axis=1)
    x1, hf0 = _mm(mix_e, lw("w_out_e", 0), res=x0, norm_gain=ffn_norm[0:1], name="out_e")
    x2, hp0, ffn0 = _ffn_forward(x1, hf0, lw("w_up", 0), lw("ffn_conv", 0), lw("w_down", 0), ple_norm[0:1], "0")
    x3, ple0 = _ple_forward(x2, hp0, lw("w_ple_gate", 0), p[0, 0], lw("w_ple", 0), "0")

    h_o = _rms_fwd(x3, mix_norm_o_full, name="rms_mix_o")
    proj_o = _mm(h_o, w_in_o_main, name="in_o")
    ba = _mm(h_o, w_in_o_ba, name="in_o_ba")
    qkv = _gdn_pre(proj_o, conv_qkv)
    beta, g = _gdn_gate(ba, alog_row, dt_row)
    u, w_c, qk, qd, kd, gl = _gdn_local(qkv, g, beta)
    o, states, vnew = _gdn_scan(u, w_c, qk, qd, kd, gl)
    y_o = _gdn_post(o, proj_o, gdn_norm_o)
    x4, hf1 = _mm(y_o, lw("w_out_o", 0), res=x3, norm_gain=ffn_norm[1:2], name="out_o")
    x5, hp1, ffn1 = _ffn_forward(x4, hf1, lw("w_up", 1), lw("ffn_conv", 1), lw("w_down", 1), ple_norm[1:2], "1")
    x6, ple1 = _ple_forward(x5, hp1, lw("w_ple_gate", 1), p[1, 0], lw("w_ple", 1), "1")
    loss_row, dx, d_final = _final_loss(x6, final_norm.reshape(1, D_MODEL), tgt)

    grads, rgrads = {}, {}
    dx, d_ple1, grads[("w_ple_gate", 1)], grads[("w_ple", 1)] = _ple_backward(dx, ple1, ple_norm[1:2], lw("w_ple_gate", 1), p[1, 0], "1")
    dx, d_ffn1, grads[("w_up", 1)], grads[("ffn_conv", 1)], grads[("w_down", 1)] = _ffn_backward(
        dx, ffn1, ffn_norm[1:2], lw("w_up", 1), lw("ffn_conv", 1), lw("w_down", 1), "1")
    grads[("w_out_o", 0)] = _mm(y_o, dx, ta=True, name="dw_out_o")
    dy_o = _mm(dx, lw("w_out_o", 0), tb=True, name="dy_o")
    do, dz, rgrads["gdn_norm_o"] = _gdn_post_bwd(o, proj_o, gdn_norm_o, dy_o)
    du, dw_c, dqk, dqd, dkd, dgl = _gdn_scan_bwd(do, states, vnew, w_c, qk, qd, kd, gl)
    dqkv_heads, dg, dbeta = _gdn_local_bwd(qkv, g, beta, du, dw_c, dqk, dqd, dkd, dgl)
    dqkv, grads[("conv_qkv_o", 0)] = _gdn_pre_bwd(proj_o, conv_qkv, dqkv_heads)
    dba, d_alog, d_dt = _gdn_gate_bwd(ba, alog_row, dt_row, dbeta, dg)
    rgrads["a_log_o"], rgrads["dt_bias_o"] = d_alog[:, 8:16], d_dt[:, 8:16]
    dproj_o = jnp.concatenate([dqkv, dz], axis=1)
    dh = _mm(dproj_o, w_in_o_main, tb=True, name="dh_o")
    dx_o, d_mix_o = _mm(dba, w_in_o_ba, tb=True, res=dh, rms_bwd=(x3, mix_norm_o_full, dx), name="dh_o_ba")
    grads[("w_in_o", 0)] = _reshard_in_o(_mm(h_o, dproj_o, ta=True, name="dw_in_o"),
                                         _mm(h_o, dba, ta=True, name="dw_in_o_ba"), w_in_o.shape[2])
    dx = dx_o
    grads[("mix_norm_o", 0)] = d_mix_o

    dx, d_ple0, grads[("w_ple_gate", 0)], grads[("w_ple", 0)] = _ple_backward(dx, ple0, ple_norm[0:1], lw("w_ple_gate", 0), p[0, 0], "0")
    dx, d_ffn0, grads[("w_up", 0)], grads[("ffn_conv", 0)], grads[("w_down", 0)] = _ffn_backward(
        dx, ffn0, ffn_norm[0:1], lw("w_up", 0), lw("ffn_conv", 0), lw("w_down", 0), "0")
    grads[("w_out_e", 0)] = _mm(mix_e, dx, ta=True, name="dw_out_e")
    dmix = _mm(dx, lw("w_out_e", 0), tb=True, name="dmix_e")
    du_e, d_pool_w, rgrads["pool_scale"] = _pool_bwd(proj_e, dmix, pool_w[0], pool_scale)
    rgrads["pool_w"] = d_pool_w[None]

    def reduce_start(gnames, tag):
        smaj = {}
        for g in gnames:
            for (name, layer), _, _ in GROUPS[g][2]:
                of = [grads[(name, i)] for i in ((0, 1) if layer is None else (layer,)) if (name, i) in grads]
                smaj[(name, layer)] = _shard_major(name, of, len(of))
        gbuf = [_group_rows(smaj, g) for g in gnames]
        gbuf = [b.reshape((4, 2) + b.shape[1:]) for b in gbuf]
        got1 = _exchange_sibling(gbuf, name="rs_sibling" + tag)
        part = [_chip_partial(place, b, r, tr=GROUPS[g][1], name="rs_chip_partial_" + g)
                for g, b, r in zip(gnames, gbuf, got1)]
        return gbuf, got1, part

    early = tuple(g for g in GROUPS if g != "in_e")
    gbuf_e, got1_e, part_e = reduce_start(early, "_early")
    dq_e, dk_e, dv_e, got2_e = _sb_bwd(proj_e, lsum, dmix, exchange=part_e)
    dproj_e = jnp.concatenate([du_e, dq_e.astype(BF16), dk_e.astype(BF16), dv_e.astype(BF16)], axis=1)
    grads[("w_in_e", 0)] = _mm(h_e, dproj_e, ta=True, shard_cols=w_in_e.shape[2], name="dw_in_e")
    dx, rgrads["mix_norm_e"] = _mm(dproj_e, lw("w_in_e", 0), tb=True, rms_bwd=(x0, mix_norm_e, dx), name="dh_e")
    rgrads["ffn_norm"] = jnp.concatenate([d_ffn0, d_ffn1], axis=0)
    rgrads["ple_norm"] = jnp.concatenate([d_ple0, d_ple1], axis=0)
    rgrads["final_norm"] = d_final.reshape(D_MODEL)

    gbuf_l, got1_l, part_l = reduce_start(("in_e",), "_late")
    got2_l = _exchange_chips(part_l)
    wloc, mloc, vloc = pieces(""), pieces("m_"), pieces("v_")
    sh_out = [{}, {}, {}, {}]
    for g, b, r1, r2 in zip(early + ("in_e",), gbuf_e + gbuf_l, list(got1_e) + list(got1_l),
                            list(got2_e) + list(got2_l)):
        res = _adamw_shard(place, b, r1, r2, _group_rows(wloc, g), _group_rows(mloc, g), _group_rows(vloc, g),
                           tr=GROUPS[g][1], name="adamw_" + g)
        for kind in range(4):
            sh_out[kind].update(_ungroup_rows(res[kind], g))

    (rparts,) = _all_gather([_pack_repl(rgrads)], name="ag_repl_grads")
    rp_out = _adamw_replicated(rparts, _pack_repl({n: given[n] for n, _, _ in REPL_LAYOUT}),
                               _pack_repl({n: given["m_" + n] for n, _, _ in REPL_LAYOUT}),
                               _pack_repl({n: given["v_" + n] for n, _, _ in REPL_LAYOUT}))
    rp_out = [_unpack_repl(b) for b in rp_out]

    def leaf(kind, name):
        if name in SHARDED:
            mine = sh_out[kind]
            whole = mine[(name, None)] if (name, None) in mine else jnp.stack([mine[(name, 0)], mine[(name, 1)]])
            return whole.reshape(given[name].shape)
        return rp_out[kind][name]

    loss = lax.psum(loss_row[0, 0], ("x", "y", "c"))
    outs = [loss, dx[None]]
    for kind in range(4):
        outs += [leaf(kind, n) for n in WEIGHTS]
    return tuple(outs)
```

```python
import jax
import jax.numpy as jnp
from jax import lax
from jax.experimental import pallas as pl
from jax.experimental.pallas import tpu as pltpu

F32 = jnp.float32
BF16 = jnp.bfloat16

D_MODEL = 1024
PLE_DIM = 256
POOL_WINDOWS = (2, 4, 8, 16)
SB_HEAD_DIM = 64
SB_BLOCK = 1024
SB_BLOCK_FWD = 2048
SB_KBLOCK = 256
GDN_HEADS = 8
GDN_HEAD_DIM = 128
GDN_CONV = 4
GDN_CHUNK = 64
FFN_DIM = 2816
FFN_CONV = 3
EPS = 1e-6
ADAM_LR, ADAM_B1, ADAM_B2, ADAM_EPS, ADAM_WD, ADAM_STEP = 0.001, 0.9, 0.999, 1e-08, 0.01, 10
N_DEV = 8
MESH = pl.DeviceIdType.MESH
VMEM_LIMIT = 56 * 1024 * 1024

NN = (((1,), (0,)), ((), ()))
NT = (((1,), (1,)), ((), ()))
TN = (((0,), (0,)), ((), ()))


def _params(*sem):
    return pltpu.CompilerParams(dimension_semantics=sem if sem else None, vmem_limit_bytes=VMEM_LIMIT)


def _dot(a, b, dims):
    return lax.dot_general(a.astype(BF16), b.astype(BF16), dims, preferred_element_type=F32)


def _iota(shape, axis):
    return lax.broadcasted_iota(jnp.int32, shape, axis)


MM_TILE, MM_TILE_11 = 1024, 1408


def _mm_tile(dim):
    if dim <= MM_TILE_11:
        return dim
    return MM_TILE if dim % MM_TILE == 0 else MM_TILE_11


def _mm(a, b, *, ta=False, tb=False, res=None, norm_gain=None, rms_bwd=None, k_block0=0, shard_cols=None,
        shard_into=None, name):
    M, K = (a.shape[1], a.shape[0]) if ta else a.shape
    N = b.shape[0] if tb else b.shape[1]
    tm, tn, tk = _mm_tile(M), _mm_tile(N), _mm_tile(K)
    if rms_bwd is not None:
        tm = min(tm, 512)
    assert M % tm == 0 and N % tn == 0 and K % tk == 0, (name, M, N, K, tm, tn, tk)
    assert (norm_gain is None and rms_bwd is None) or tn == N, name
    nk = K // tk
    dims = (((0 if ta else 1,), (1 if tb else 0,)), ((), ()))
    extra = [res] if res is not None else []
    vecs = [norm_gain] if norm_gain is not None else []
    if rms_bwd is not None:
        extra += [rms_bwd[0], rms_bwd[2]]
        vecs = [rms_bwd[1]]
    n_out = 1 if (norm_gain is None and rms_bwd is None) else 2

    def body(*refs):
        a_ref, b_ref = refs[:2]
        tiles = list(refs[2:2 + len(extra)])
        vec_refs = refs[2 + len(extra):2 + len(extra) + len(vecs)]
        n_in = 2 + len(extra) + len(vecs) + len(held)
        outs = refs[n_in:n_in + n_out]
        scr = refs[n_in + n_out:]
        p = _dot(a_ref[...], b_ref[...], dims)

        def fin(acc):
            if res is not None:
                acc = acc + tiles[0][...]
            if rms_bwd is not None:
                x_ref, dres_ref = tiles[-2:]
                xv = x_ref[...]
                r = lax.rsqrt(jnp.mean(xv * xv, axis=-1, keepdims=True) + EPS)
                xn = xv * r
                dgp = jnp.sum(acc * xn, axis=0, keepdims=True)
                dyg = acc * vec_refs[0][...]
                outs[0][...] = dres_ref[...] + r * (dyg - xn * jnp.mean(dyg * xn, axis=-1, keepdims=True))
                first = pl.program_id(0) == 0

                @pl.when(first)
                def _():
                    outs[1][...] = dgp

                @pl.when(jnp.logical_not(first))
                def _():
                    outs[1][...] += dgp
                return
            if shard_cols is not None:
                for s in range(tn // shard_cols):
                    outs[0][s] = acc[:, shard_cols * s:shard_cols * (s + 1)]
                return
            outs[0][...] = acc
            if norm_gain is not None:
                r = lax.rsqrt(jnp.mean(acc * acc, axis=-1, keepdims=True) + EPS)
                outs[1][...] = (acc * r * vec_refs[0][...]).astype(BF16)

        if nk == 1:
            fin(p)
        else:
            acc_ref = scr[0]
            k = pl.program_id(2)

            @pl.when(k == 0)
            def _():
                acc_ref[...] = p

            @pl.when(k > 0)
            def _():
                acc_ref[...] += p

            @pl.when(k == nk - 1)
            def _():
                fin(acc_ref[...])

    a_spec = pl.BlockSpec((tk, tm), lambda i, j, k: (k, i)) if ta else pl.BlockSpec((tm, tk), lambda i, j, k: (i, k))
    b_spec = (pl.BlockSpec((tn, tk), lambda i, j, k: (j, k + k_block0)) if tb
              else pl.BlockSpec((tk, tn), lambda i, j, k: (k, j)))
    o_spec = pl.BlockSpec((tm, tn), lambda i, j, k: (i, j))
    v_spec = pl.BlockSpec((1, tn), lambda i, j, k: (0, j))
    out_specs, out_shape = [o_spec], [jax.ShapeDtypeStruct((M, N), F32)]
    held = []
    if shard_cols is not None:
        per = tn // shard_cols
        total, first, buf = shard_into if shard_into is not None else (N // shard_cols, 0, None)
        out_specs = [pl.BlockSpec((per, tm, shard_cols), lambda i, j, k: (j + first // per, i, 0))]
        out_shape = [jax.ShapeDtypeStruct((total, M, shard_cols), F32)]
        held = [buf] if buf is not None else []
    if norm_gain is not None:
        out_specs, out_shape = out_specs + [o_spec], out_shape + [jax.ShapeDtypeStruct((M, N), BF16)]
    if rms_bwd is not None:
        out_specs, out_shape = out_specs + [v_spec], out_shape + [jax.ShapeDtypeStruct((1, N), F32)]
    out = pl.pallas_call(
        body, name=name, grid=(M // tm, N // tn, nk),
        in_specs=([a_spec, b_spec] + [o_spec] * len(extra) + [v_spec] * len(vecs)
                  + [pl.BlockSpec(memory_space=pl.ANY)] * len(held)),
        out_specs=out_specs, out_shape=out_shape, scratch_shapes=[pltpu.VMEM((tm, tn), F32)] if nk > 1 else [],
        input_output_aliases={2 + len(extra) + len(vecs): 0} if held else {},
        compiler_params=_params("arbitrary" if rms_bwd is not None else "parallel", "parallel", "arbitrary"),
    )(a, b, *extra, *vecs, *held)
    return out[0] if n_out == 1 else out


def _rms_fwd(x, gain, *, name, tr=512):
    T, Dm = x.shape

    def body(x_ref, g_ref, o_ref):
        xv = x_ref[...]
        r = lax.rsqrt(jnp.mean(xv * xv, axis=-1, keepdims=True) + EPS)
        o_ref[...] = (xv * r * g_ref[...]).astype(BF16)

    return pl.pallas_call(
        body, name=name, grid=(T // tr,),
        in_specs=[pl.BlockSpec((tr, Dm), lambda i: (i, 0)), pl.BlockSpec((1, Dm), lambda i: (0, 0))],
        out_specs=pl.BlockSpec((tr, Dm), lambda i: (i, 0)),
        out_shape=jax.ShapeDtypeStruct((T, Dm), BF16), compiler_params=_params("parallel"),
    )(x, gain)


def _final_loss(x, gain, target, *, tr=512):
    T, Dm = x.shape

    def body(x_ref, g_ref, t_ref, loss_ref, dx_ref, dg_ref):
        i = pl.program_id(0)
        xv = x_ref[...]
        g = g_ref[...]
        r = lax.rsqrt(jnp.mean(xv * xv, axis=-1, keepdims=True) + EPS)
        xn = xv * r
        err = xn * g - t_ref[...]
        lp = jnp.zeros((1, 128), F32) + 0.5 * jnp.sum(jnp.mean(err * err, axis=-1, keepdims=True))
        dy_v = err * (1.0 / Dm)
        dgp = jnp.sum(dy_v * xn, axis=0, keepdims=True)
        dyg = dy_v * g
        dx_ref[...] = r * (dyg - xn * jnp.mean(dyg * xn, axis=-1, keepdims=True))

        @pl.when(i == 0)
        def _():
            dg_ref[...] = dgp
            loss_ref[...] = lp

        @pl.when(i > 0)
        def _():
            dg_ref[...] += dgp
            loss_ref[...] += lp

    row = pl.BlockSpec((tr, Dm), lambda i: (i, 0))
    vec = pl.BlockSpec((1, Dm), lambda i: (0, 0))
    return pl.pallas_call(
        body, name="final_loss", grid=(T // tr,), in_specs=[row, vec, row],
        out_specs=[pl.BlockSpec((1, 128), lambda i: (0, 0)), row, vec],
        out_shape=[jax.ShapeDtypeStruct((1, 128), F32), jax.ShapeDtypeStruct((T, Dm), F32),
                   jax.ShapeDtypeStruct((1, Dm), F32)],
        compiler_params=_params("arbitrary"),
    )(x, gain, target)


def _prev_spec(tr, cb, pad, col):
    return pl.BlockSpec((pad, cb), lambda *g: (jnp.maximum(g[0] * (tr // pad) - 1, 0), col(*g)))


def _next_spec(tr, cb, pad, col, T):
    return pl.BlockSpec((pad, cb), lambda *g: (jnp.minimum((g[0] + 1) * (tr // pad), T // pad - 1), col(*g)))


def _conv_rows(x_ext, w_ref, K, pad, cs=slice(None)):
    y = w_ref[K - 1:K, cs] * x_ext
    for i in range(K - 1):
        y = y + w_ref[i:i + 1, cs] * pltpu.roll(x_ext, K - 1 - i, 0)
    return y[pad:]


def _pool_y(u_ext, g, i, tr):
    s = u_ext
    for sh in (1, 2, 4, 8)[:g + 1]:
        s = s + pltpu.roll(s, sh, 0)
    t = i * tr + _iota((tr, 128), 0)
    cnt = jnp.minimum(t + 1, POOL_WINDOWS[g]).astype(F32)
    return s[16:] / cnt - u_ext[16:]


def _pool_fwd(proj, pool_w, pool_scale, *, tr=512):
    T = proj.shape[0]

    def body(u_ref, uh_ref, w_ref, s_ref, o_ref):
        i = pl.program_id(0)
        uh = jnp.where(i > 0, uh_ref[...], 0.0)
        for g in range(4):
            cs = slice(128 * g, 128 * (g + 1))
            y = _pool_y(jnp.concatenate([uh[:, cs], u_ref[:, cs]], axis=0), g, i, tr)
            o_ref[:, cs] = (_dot(y, w_ref[g], NN) * s_ref[:, cs]).astype(BF16)

    return pl.pallas_call(
        body, name="pool_fwd", grid=(T // tr,),
        in_specs=[pl.BlockSpec((tr, 512), lambda i: (i, 0)), _prev_spec(tr, 512, 16, lambda i: 0),
                  pl.BlockSpec((4, 128, 128), lambda i: (0, 0, 0)), pl.BlockSpec((1, 512), lambda i: (0, 0))],
        out_specs=pl.BlockSpec((tr, 512), lambda i: (i, 0)),
        out_shape=jax.ShapeDtypeStruct((T, 512), BF16), compiler_params=_params("parallel"),
    )(proj, proj, pool_w, pool_scale)


def _pool_bwd(proj, dout, pool_w, pool_scale, *, tr=512):
    T = proj.shape[0]
    nb = T // tr

    def body(u_ref, uh_ref, d_ref, dn_ref, w_ref, s_ref, du_ref, dw_ref, ds_ref):
        i = pl.program_id(0)
        uh = jnp.where(i > 0, uh_ref[...], 0.0)
        dn = jnp.where(i < nb - 1, dn_ref[...], 0.0)
        t_ext = i * tr + _iota((tr + 16, 128), 0)
        for g in range(4):
            cs = slice(128 * g, 128 * (g + 1))
            sc = s_ref[:, cs]
            wg = w_ref[g]
            y = _pool_y(jnp.concatenate([uh[:, cs], u_ref[:, cs]], axis=0), g, i, tr)
            dg = d_ref[:, cs]
            dsp = jnp.sum(dg * _dot(y, wg, NN), axis=0, keepdims=True)
            dyw = dg * sc
            dwp = _dot(y, dyw, TN)
            dy_ext = _dot(jnp.concatenate([dyw, dn[:, cs] * sc], axis=0), wg, NT)
            cnt = jnp.minimum(t_ext + 1, POOL_WINDOWS[g]).astype(F32)
            s = dy_ext / cnt
            for sh in (1, 2, 4, 8)[:g + 1]:
                s = s + pltpu.roll(s, tr + 16 - sh, 0)
            du_ref[:, cs] = (s[:tr] - dy_ext[:tr]).astype(BF16)

            @pl.when(i == 0)
            def _():
                dw_ref[g] = dwp
                ds_ref[:, cs] = dsp

            @pl.when(i > 0)
            def _():
                dw_ref[g] += dwp
                ds_ref[:, cs] += dsp

    row = pl.BlockSpec((tr, 512), lambda i: (i, 0))
    return pl.pallas_call(
        body, name="pool_bwd", grid=(nb,),
        in_specs=[row, _prev_spec(tr, 512, 16, lambda i: 0), row, _next_spec(tr, 512, 16, lambda i: 0, T),
                  pl.BlockSpec((4, 128, 128), lambda i: (0, 0, 0)), pl.BlockSpec((1, 512), lambda i: (0, 0))],
        out_specs=[row, pl.BlockSpec((4, 128, 128), lambda i: (0, 0, 0)), pl.BlockSpec((1, 512), lambda i: (0, 0))],
        out_shape=[jax.ShapeDtypeStruct((T, 512), BF16), jax.ShapeDtypeStruct((4, 128, 128), F32),
                   jax.ShapeDtypeStruct((1, 512), F32)],
        compiler_params=_params("arbitrary"),
    )(proj, proj, dout, dout, pool_w, pool_scale)


def _split_dot(x, tri):
    hi = x.astype(BF16)
    lo = (x - hi.astype(F32)).astype(BF16)
    return (lax.dot_general(hi, tri, NN, preferred_element_type=F32)
            + lax.dot_general(lo, tri, NN, preferred_element_type=F32))


def _log1m(z):
    return -(jnp.maximum(z, 0.0) + jnp.log(1.0 + jnp.exp(-jnp.abs(z))))


def _sb_fwd(proj, gather=()):
    T = proj.shape[0]
    B, BK = min(SB_BLOCK_FWD, T), SB_KBLOCK
    R = B // BK
    nq, n = T // B, len(gather)
    scale = SB_HEAD_DIM ** -0.5

    def body(q_ref, k_ref, v_ref, *rest):
        o_ref, ls_ref = rest[n:n + 2]
        hp, i = pl.program_id(0), pl.program_id(1)
        if n:
            start, forward, finish = _gather_stages(rest[:n], rest[n + 2:2 * n + 2], *rest[2 * n + 2:])
            pl.when((hp == 0) & (i == 0))(start)
            pl.when((hp == 3) & (i == nq - 1))(forward)
        lane = _iota((1, 128), 1)
        tri_gt = (_iota((BK, BK), 0) > _iota((BK, BK), 1)).astype(BF16)
        row, col = _iota((B, BK), 0), _iota((B, BK), 1)
        qv = q_ref[...] * scale
        hms = [(lane >= 64 * h) & (lane < 64 * (h + 1)) for h in range(2)]
        qhs = [jnp.where(hm, qv, 0.0).astype(BF16) for hm in hms]

        def tile(j, carry, d):
            rows = pl.ds(pl.multiple_of(j * BK, BK), BK)
            kj = k_ref[rows, :].astype(BF16)
            vj = v_ref[rows, :].astype(BF16)
            r0 = 0 if d is None else BK * d
            valid = None if d is None else (col[r0:] < row[:B - r0])
            out = []
            for h in range(2):
                c, acc = carry[h]
                z = lax.dot_general(qhs[h][r0:], kj, NT, preferred_element_type=F32)
                lg = _log1m(z)
                if d is not None:
                    lg = jnp.where(valid, lg, 0.0)
                a = jnp.exp(z + lg + _split_dot(lg, tri_gt) + c[r0:])
                if d is not None:
                    a = jnp.where(valid, a, 0.0)
                upd = (c[r0:] + jnp.sum(lg, axis=1, keepdims=True),
                       acc[r0:] + lax.dot_general(a.astype(BF16), vj, NN, preferred_element_type=F32))
                out.append(upd if r0 == 0 else tuple(jnp.concatenate([old[:r0], new], axis=0)
                                                     for old, new in zip((c, acc), upd)))
            return tuple(out)

        zero = (jnp.zeros((B, 1), F32), jnp.zeros((B, 128), F32))
        carry = (zero, zero)
        for d in reversed(range(R)):
            carry = tile(i * R + d, carry, d)
        carry = lax.fori_loop(0, i * R, lambda s, cr: tile(i * R - 1 - s, cr, None), carry)
        o_ref[...] = jnp.where(hms[0], carry[0][1], carry[1][1])
        ls_ref[...] = jnp.where(hms[0], carry[0][0], carry[1][0])
        if n:
            pl.when((hp == 3) & (i == nq - 1))(finish)

    blk = pl.BlockSpec((B, 128), lambda hp, i: (i, hp))
    out = pl.pallas_call(
        body, name="sb_fwd", grid=(4, nq),
        in_specs=[pl.BlockSpec((B, 128), lambda hp, i: (i, 4 + hp)),
                  pl.BlockSpec((T, 128), lambda hp, i: (0, 8 + hp)),
                  pl.BlockSpec((T, 128), lambda hp, i: (0, 12 + hp))] + [HBM_SPEC] * n,
        out_specs=[blk, blk] + [HBM_SPEC] * n,
        out_shape=[jax.ShapeDtypeStruct((T, 512), F32)] * 2 + _gather_shapes(gather),
        scratch_shapes=_gather_sems(n) if n else [],
        compiler_params=_params("arbitrary", "arbitrary"),
    )(proj, proj, proj, *gather)
    return out[0], out[1], list(out[2:])


def _sb_bwd(proj, lsum, dout, exchange=()):
    T = proj.shape[0]
    B, BK = min(SB_BLOCK, T), SB_KBLOCK
    R = B // BK
    nq, n = T // B, len(exchange)
    scale = SB_HEAD_DIM ** -0.5

    def body(q_ref, k_ref, v_ref, do_ref, ls_ref, *rest):
        dq_ref, dk_ref, dv_ref = rest[n:n + 3]
        hp, i = pl.program_id(0), pl.program_id(1)
        if n:
            start, finish = _chips_stages(rest[:n], rest[n + 3:2 * n + 3], *rest[2 * n + 3:])
            pl.when((hp == 0) & (i == 0))(start)

        @pl.when(i == 0)
        def _():
            dk_ref[...] = jnp.zeros_like(dk_ref)
            dv_ref[...] = jnp.zeros_like(dv_ref)

        lane = _iota((1, 128), 1)
        tri_le = (_iota((BK, BK), 0) <= _iota((BK, BK), 1)).astype(BF16)
        tri_lt = (_iota((BK, BK), 0) < _iota((BK, BK), 1)).astype(BF16)
        row, col = _iota((B, BK), 0), _iota((B, BK), 1)
        qv = q_ref[...] * scale
        dov = do_ref[...]
        hms = [(lane >= 64 * h) & (lane < 64 * (h + 1)) for h in range(2)]
        qhs = [jnp.where(hm, qv, 0.0).astype(BF16) for hm in hms]
        dos = [jnp.where(hm, dov, 0.0).astype(BF16) for hm in hms]
        ltots = [ls_ref[:, 64 * h:64 * h + 1] for h in range(2)]

        def tile(j, carry, d):
            rows = pl.ds(pl.multiple_of(j * BK, BK), BK)
            kj = k_ref[rows, :].astype(BF16)
            vj = v_ref[rows, :].astype(BF16)
            diag = d is not None
            r0 = BK * d if diag else 0
            valid = (col[r0:] < row[:B - r0]) if diag else None
            out = []
            dkj = jnp.zeros((BK, 128), F32)
            dvj = jnp.zeros((BK, 128), F32)
            for h in range(2):
                lbef, ebef, dqa = carry[h]
                qh, do_h = qhs[h][r0:], dos[h][r0:]
                z = lax.dot_general(qh, kj, NT, preferred_element_type=F32)
                lg = _log1m(z)
                if diag:
                    lg = jnp.where(valid, lg, 0.0)
                a = jnp.exp(z + lg + (ltots[h][r0:] - lbef[r0:] - _split_dot(lg, tri_le)))
                if diag:
                    a = jnp.where(valid, a, 0.0)
                e = a * lax.dot_general(do_h, vj, NT, preferred_element_type=F32)
                dz = e * jnp.exp(lg) - jnp.exp(z + lg) * (ebef[r0:] + _split_dot(e, tri_lt))
                if diag:
                    dz = jnp.where(valid, dz, 0.0)
                dzb = dz.astype(BF16)
                dkj = dkj + lax.dot_general(dzb, qh, TN, preferred_element_type=F32)
                dvj = dvj + lax.dot_general(a.astype(BF16), do_h, TN, preferred_element_type=F32)
                upd = (lbef[r0:] + jnp.sum(lg, axis=1, keepdims=True), ebef[r0:] + jnp.sum(e, axis=1, keepdims=True),
                       dqa[r0:] + lax.dot_general(dzb, kj, NN, preferred_element_type=F32))
                out.append(upd if r0 == 0 else tuple(jnp.concatenate([old[:r0], new], axis=0)
                                                     for old, new in zip(carry[h], upd)))
            dk_ref[rows, :] += dkj
            dv_ref[rows, :] += dvj
            return tuple(out)

        zero = (jnp.zeros((B, 1), F32), jnp.zeros((B, 1), F32), jnp.zeros((B, 128), F32))
        carry = lax.fori_loop(0, i * R, lambda j, cr: tile(j, cr, None), (zero, zero))
        for d in range(R):
            carry = tile(i * R + d, carry, d)
        dq_ref[...] = jnp.where(hms[0], carry[0][2], carry[1][2]) * scale
        if n:
            pl.when((hp == 3) & (i == nq - 1))(finish)

    full = pl.BlockSpec((T, 128), lambda hp, i: (0, hp))
    blk = pl.BlockSpec((B, 128), lambda hp, i: (i, hp))
    out = pl.pallas_call(
        body, name="sb_bwd", grid=(4, nq),
        in_specs=[pl.BlockSpec((B, 128), lambda hp, i: (i, 4 + hp)),
                  pl.BlockSpec((T, 128), lambda hp, i: (0, 8 + hp)),
                  pl.BlockSpec((T, 128), lambda hp, i: (0, 12 + hp)),
                  pl.BlockSpec((B, 128), lambda hp, i: (i, 4 + hp)), blk] + [HBM_SPEC] * n,
        out_specs=[blk, full, full] + [HBM_SPEC] * n,
        out_shape=[jax.ShapeDtypeStruct((T, 512), F32)] * 3 + _chips_shapes(exchange),
        scratch_shapes=_chips_sems(n) if n else [],
        compiler_params=_params("arbitrary", "arbitrary"),
    )(proj, proj, proj, dout, lsum, *exchange)
    return out[0], out[1], out[2], list(out[3:])


def _sigmoid(x):
    return 1.0 / (1.0 + jnp.exp(-x))


def _silu_mul(cg, cv):
    return cg * _sigmoid(cg) * cv


def _ffn_act(up, conv_w, *, tr=512, cb=256):
    T, F2 = up.shape
    nc = F2 // 2 // cb
    K = FFN_CONV

    def body(g_ref, gh_ref, v_ref, vh_ref, wg_ref, wv_ref, o_ref):
        i = pl.program_id(0)
        gh = jnp.where(i > 0, gh_ref[...], 0.0)
        vh = jnp.where(i > 0, vh_ref[...], 0.0)
        cg = _conv_rows(jnp.concatenate([gh, g_ref[...]], axis=0), wg_ref, K, 8)
        cv = _conv_rows(jnp.concatenate([vh, v_ref[...]], axis=0), wv_ref, K, 8)
        o_ref[...] = _silu_mul(cg, cv).astype(BF16)

    return pl.pallas_call(
        body, name="ffn_act", grid=(T // tr, nc),
        in_specs=[pl.BlockSpec((tr, cb), lambda i, j: (i, j)), _prev_spec(tr, cb, 8, lambda i, j: j),
                  pl.BlockSpec((tr, cb), lambda i, j: (i, nc + j)), _prev_spec(tr, cb, 8, lambda i, j: nc + j),
                  pl.BlockSpec((K, cb), lambda i, j: (0, j)), pl.BlockSpec((K, cb), lambda i, j: (0, nc + j))],
        out_specs=pl.BlockSpec((tr, cb), lambda i, j: (i, j)),
        out_shape=jax.ShapeDtypeStruct((T, F2 // 2), BF16), compiler_params=_params("parallel", "parallel"),
    )(up, up, up, up, conv_w, conv_w)


def _conv_bwd_rows(dc_ext, x_ext, w_ref, K, tr, cs=slice(None)):
    n = tr + 8
    dx = w_ref[K - 1:K, cs] * dc_ext
    for i in range(K - 1):
        dx = dx + w_ref[i:i + 1, cs] * pltpu.roll(dc_ext, n - (K - 1 - i), 0)
    dc = dc_ext[:tr]
    dws = [jnp.sum(dc * pltpu.roll(x_ext, K - 1 - i, 0)[8:8 + tr], axis=0, keepdims=True) for i in range(K)]
    return dx[:tr], dws


def _acc_rows(ref, rows, first, cs=slice(None)):
    for i, r in enumerate(rows):
        @pl.when(first)
        def _():
            ref[i:i + 1, cs] = r

        @pl.when(jnp.logical_not(first))
        def _():
            ref[i:i + 1, cs] += r


def _ffn_act_bwd(up, conv_w, dact, *, tr=512, cb=256):
    T, F2 = up.shape
    F = F2 // 2
    nc, nb = F // cb, T // tr
    K = FFN_CONV

    def body(g_ref, gp_ref, gn_ref, v_ref, vp_ref, vn_ref, d_ref, dn_ref, wg_ref, wv_ref,
             dg_ref, dv_ref, dwg_ref, dwv_ref):
        i = pl.program_id(1)
        first, last = i == 0, i == nb - 1
        g_ext = jnp.concatenate([jnp.where(first, 0.0, gp_ref[...]), g_ref[...], jnp.where(last, 0.0, gn_ref[...])], axis=0)
        v_ext = jnp.concatenate([jnp.where(first, 0.0, vp_ref[...]), v_ref[...], jnp.where(last, 0.0, vn_ref[...])], axis=0)
        d_ext = jnp.concatenate([d_ref[...], jnp.where(last, 0.0, dn_ref[...])], axis=0)
        cg = _conv_rows(g_ext, wg_ref, K, 8)
        cv = _conv_rows(v_ext, wv_ref, K, 8)
        s = _sigmoid(cg)
        t = cg * s
        dcv = d_ext * t
        dcg = d_ext * cv * (s + t * (1.0 - s))
        dg, dwg = _conv_bwd_rows(dcg, g_ext, wg_ref, K, tr)
        dv, dwv = _conv_bwd_rows(dcv, v_ext, wv_ref, K, tr)
        dg_ref[...] = dg.astype(BF16)
        dv_ref[...] = dv.astype(BF16)
        _acc_rows(dwg_ref, dwg, first)
        _acc_rows(dwv_ref, dwv, first)

    blk = lambda off: pl.BlockSpec((tr, cb), lambda j, i: (i, off + j))
    prev = lambda off: pl.BlockSpec((8, cb), lambda j, i: (jnp.maximum(i * (tr // 8) - 1, 0), off + j))
    nxt = lambda off: pl.BlockSpec((8, cb), lambda j, i: (jnp.minimum((i + 1) * (tr // 8), T // 8 - 1), off + j))
    wsp = lambda off: pl.BlockSpec((K, cb), lambda j, i: (0, off + j))
    return pl.pallas_call(
        body, name="ffn_act_bwd", grid=(nc, nb),
        in_specs=[blk(0), prev(0), nxt(0), blk(nc), prev(nc), nxt(nc), blk(0), nxt(0), wsp(0), wsp(nc)],
        out_specs=[blk(0), blk(0), wsp(0), wsp(0)],
        out_shape=[jax.ShapeDtypeStruct((T, F), BF16)] * 2 + [jax.ShapeDtypeStruct((K, F), F32)] * 2,
        compiler_params=_params("parallel", "arbitrary"),
    )(up, up, up, up, up, up, dact, dact, conv_w, conv_w)


def _ple_fwd(hn, w_gate, p, w_ple, x, *, name, tm=1024, tn=512):
    T, Dm = x.shape
    tm = min(tm, T)

    def body(a_ref, b_ref, p_ref, wp_ref, x_ref, o_ref, gl_ref, pe_ref):
        gl = _dot(a_ref[...], b_ref[...], NN)
        pe = _dot(p_ref[...], wp_ref[...], NN)
        gl_ref[...] = gl
        pe_ref[...] = pe
        o_ref[...] = x_ref[...] + pe * _sigmoid(gl)

    o_spec = pl.BlockSpec((tm, tn), lambda i, j: (i, j))
    return pl.pallas_call(
        body, name=name, grid=(T // tm, Dm // tn),
        in_specs=[pl.BlockSpec((tm, Dm), lambda i, j: (i, 0)), pl.BlockSpec((Dm, tn), lambda i, j: (0, j)),
                  pl.BlockSpec((tm, PLE_DIM), lambda i, j: (i, 0)), pl.BlockSpec((PLE_DIM, tn), lambda i, j: (0, j)),
                  o_spec],
        out_specs=[o_spec] * 3, out_shape=[jax.ShapeDtypeStruct((T, Dm), F32)] * 3,
        compiler_params=_params("parallel", "parallel"),
    )(hn, w_gate, p, w_ple, x)


def _ple_bwd(dx, gl, pe, *, name, tr=512):
    T, Dm = dx.shape

    def body(dx_ref, gl_ref, pe_ref, dpe_ref, dgl_ref):
        g = _sigmoid(gl_ref[...])
        d = dx_ref[...]
        dpe_ref[...] = (d * g).astype(BF16)
        dgl_ref[...] = (d * pe_ref[...] * g * (1.0 - g)).astype(BF16)

    row = pl.BlockSpec((tr, Dm), lambda i: (i, 0))
    return pl.pallas_call(
        body, name=name, grid=(T // tr,), in_specs=[row] * 3, out_specs=[row] * 2,
        out_shape=[jax.ShapeDtypeStruct((T, Dm), BF16)] * 2, compiler_params=_params("parallel"),
    )(dx, gl, pe)


def _qkv_act(c, cb):
    s = c * _sigmoid(c)
    n = s * lax.rsqrt(jnp.sum(s * s, axis=-1, keepdims=True) + EPS)
    n = n * jnp.where(cb < GDN_HEADS, GDN_HEAD_DIM ** -0.5, 1.0)
    return jnp.where(cb < 2 * GDN_HEADS, n, s)


GDN_HPS = 4


def _gdn_pre(proj, conv_w, *, tr=512):
    T = proj.shape[0]
    K = GDN_CONV

    def body(x_ref, xh_ref, w_ref, o_ref):
        i, j = pl.program_id(0), pl.program_id(1)
        xh = jnp.where(i > 0, xh_ref[...], 0.0)
        for hh in range(GDN_HPS):
            cs = slice(128 * hh, 128 * (hh + 1))
            c = _conv_rows(jnp.concatenate([xh[:, cs], x_ref[:, cs]], axis=0), w_ref, K, 8, cs)
            o_ref[hh] = _qkv_act(c, GDN_HPS * j + hh)

    wide = 128 * GDN_HPS
    return pl.pallas_call(
        body, name="gdn_pre", grid=(T // tr, 24 // GDN_HPS),
        in_specs=[pl.BlockSpec((tr, wide), lambda i, j: (i, j)), _prev_spec(tr, wide, 8, lambda i, j: j),
                  pl.BlockSpec((K, wide), lambda i, j: (0, j))],
        out_specs=pl.BlockSpec((GDN_HPS, tr, 128), lambda i, j: (j, i, 0)),
        out_shape=jax.ShapeDtypeStruct((24, T, 128), F32), compiler_params=_params("parallel", "parallel"),
    )(proj, proj, conv_w)


def _gdn_pre_bwd(proj, conv_w, dqkv, *, tr=512):
    T = proj.shape[0]
    nb = T // tr
    K = GDN_CONV

    def body(x_ref, xp_ref, xn_ref, d_ref, dn_ref, w_ref, dx_ref, dw_ref):
        j, i = pl.program_id(0), pl.program_id(1)
        first, last = i == 0, i == nb - 1
        xp = jnp.where(first, 0.0, xp_ref[...])
        xn = jnp.where(last, 0.0, xn_ref[...])
        for hh in range(GDN_HPS):
            cs = slice(128 * hh, 128 * (hh + 1))
            x_ext = jnp.concatenate([xp[:, cs], x_ref[:, cs], xn[:, cs]], axis=0)
            d_ext = jnp.concatenate([d_ref[hh], jnp.where(last, 0.0, dn_ref[hh])], axis=0)
            c = _conv_rows(x_ext, w_ref, K, 8, cs)
            _, vjp = jax.vjp(lambda c_: _qkv_act(c_, GDN_HPS * j + hh), c)
            (dc,) = vjp(d_ext)
            dx, dws = _conv_bwd_rows(dc, x_ext, w_ref, K, tr, cs)
            dx_ref[:, cs] = dx.astype(BF16)
            _acc_rows(dw_ref, dws, first, cs)

    wide = 128 * GDN_HPS
    return pl.pallas_call(
        body, name="gdn_pre_bwd", grid=(24 // GDN_HPS, nb),
        in_specs=[pl.BlockSpec((tr, wide), lambda j, i: (i, j)),
                  pl.BlockSpec((8, wide), lambda j, i: (jnp.maximum(i * (tr // 8) - 1, 0), j)),
                  pl.BlockSpec((8, wide), lambda j, i: (jnp.minimum((i + 1) * (tr // 8), T // 8 - 1), j)),
                  pl.BlockSpec((GDN_HPS, tr, 128), lambda j, i: (j, i, 0)),
                  pl.BlockSpec((GDN_HPS, 8, 128), lambda j, i: (j, jnp.minimum((i + 1) * (tr // 8), T // 8 - 1), 0)),
                  pl.BlockSpec((K, wide), lambda j, i: (0, j))],
        out_specs=[pl.BlockSpec((tr, wide), lambda j, i: (i, j)), pl.BlockSpec((K, wide), lambda j, i: (0, j))],
        out_shape=[jax.ShapeDtypeStruct((T, 24 * 128), BF16), jax.ShapeDtypeStruct((K, 24 * 128), F32)],
        compiler_params=_params("parallel", "arbitrary"),
    )(proj, proj, proj, dqkv, dqkv, conv_w)


def _gate_fn(ba, alog_row, dt_row):
    lane = _iota((1, 128), 1)
    x = ba + dt_row
    sp = jnp.maximum(x, 0.0) + jnp.log(1.0 + jnp.exp(-jnp.abs(x)))
    return jnp.where(lane < GDN_HEADS, _sigmoid(ba), -jnp.exp(alog_row) * sp)


def _gdn_gate(ba, alog_row, dt_row, *, tr=512):
    T = ba.shape[0]

    def body(ba_ref, al_ref, dt_ref, b_ref, g_ref):
        val = _gate_fn(ba_ref[...], al_ref[...], dt_ref[...])
        for h in range(GDN_HEADS):
            b_ref[h] = val[:, h:h + 1]
            g_ref[h] = val[:, GDN_HEADS + h:GDN_HEADS + h + 1]

    vec = pl.BlockSpec((1, 128), lambda i: (0, 0))
    hm = pl.BlockSpec((GDN_HEADS, tr, 1), lambda i: (0, i, 0))
    return pl.pallas_call(
        body, name="gdn_gate", grid=(T // tr,), in_specs=[pl.BlockSpec((tr, 128), lambda i: (i, 0)), vec, vec],
        out_specs=[hm, hm], out_shape=[jax.ShapeDtypeStruct((GDN_HEADS, T, 1), F32)] * 2,
        compiler_params=_params("parallel"),
    )(ba, alog_row, dt_row)


def _gdn_gate_bwd(ba, alog_row, dt_row, dbeta, dg, *, tr=512):
    T = ba.shape[0]

    def body(ba_ref, al_ref, dt_ref, db_ref, dg_ref, dba_ref, dal_ref, ddt_ref):
        i = pl.program_id(0)
        lane = _iota((1, 128), 1)
        d = jnp.zeros((tr, 128), F32)
        for h in range(GDN_HEADS):
            d = d + jnp.where(lane == h, db_ref[h], 0.0) + jnp.where(lane == GDN_HEADS + h, dg_ref[h], 0.0)
        _, vjp = jax.vjp(_gate_fn, ba_ref[...], al_ref[...], dt_ref[...])
        dba, dal, ddt = vjp(d)
        dba_ref[...] = dba.astype(BF16)

        @pl.when(i == 0)
        def _():
            dal_ref[...] = dal
            ddt_ref[...] = ddt

        @pl.when(i > 0)
        def _():
            dal_ref[...] += dal
            ddt_ref[...] += ddt

    vec = pl.BlockSpec((1, 128), lambda i: (0, 0))
    hm = pl.BlockSpec((GDN_HEADS, tr, 1), lambda i: (0, i, 0))
    row = pl.BlockSpec((tr, 128), lambda i: (i, 0))
    return pl.pallas_call(
        body, name="gdn_gate_bwd", grid=(T // tr,), in_specs=[row, vec, vec, hm, hm], out_specs=[row, vec, vec],
        out_shape=[jax.ShapeDtypeStruct((T, 128), BF16), jax.ShapeDtypeStruct((1, 128), F32),
                   jax.ShapeDtypeStruct((1, 128), F32)],
        compiler_params=_params("arbitrary"),
    )(ba, alog_row, dt_row, dbeta, dg)


def _split3(x):
    x1 = x.astype(BF16)
    r = x - x1.astype(F32)
    x2 = r.astype(BF16)
    return x1, x2, (r - x2.astype(F32)).astype(BF16)


def _dot01(tri, x, dims):
    t = tri.astype(BF16)
    x1, x2, x3 = _split3(x)
    d = lambda xi: lax.dot_general(t, xi, dims, preferred_element_type=F32)
    return d(x1) + (d(x2) + d(x3))


def _dot3(a, b, dims):
    ah, al, _ = _split3(a)
    bh, bl, _ = _split3(b)
    d = lambda p, q: lax.dot_general(p, q, dims, preferred_element_type=F32)
    return d(ah, bh) + (d(ah, bl) + d(al, bh))


BNN = (((2,), (1,)), ((0,), (0,)))
BNT = (((2,), (2,)), ((0,), (0,)))
BTN = (((1,), (1,)), ((0,), (0,)))


@jax.custom_vjp
def _mm01(tri, x):
    return _dot01(tri, x, BNN)


def _mm01_fwd(tri, x):
    return _dot01(tri, x, BNN), tri


def _mm01_bwd(tri, ct):
    return jnp.zeros_like(tri), _dot01(tri, ct, BTN)


_mm01.defvjp(_mm01_fwd, _mm01_bwd)


def _unit_lower_inverse(a):
    C = a.shape[-1]
    eye = (_iota(a.shape, 1) == _iota(a.shape, 2)).astype(F32)
    pw = -a
    tinv = eye + pw
    for _ in range(5):
        pw = _dot3(pw, pw, BNN)
        tinv = tinv + _dot3(tinv, pw, BNN)
    return tinv


@jax.custom_vjp
def _unit_lower_solve(a, rv, rw):
    return _unit_lower_solve_fwd(a, rv, rw)[0]


def _unit_lower_solve_fwd(a, rv, rw):
    tinv = _unit_lower_inverse(a)
    sol = _dot3(tinv, jnp.concatenate([rv, rw], axis=2), BNN)
    n = rv.shape[2]
    return (sol[:, :, :n], sol[:, :, n:]), (tinv, sol)


def _unit_lower_solve_bwd(res, cts):
    tinv, sol = res
    n = cts[0].shape[2]
    d_rhs = _dot3(tinv, jnp.concatenate(cts, axis=2), BTN)
    return -_dot3(d_rhs, sol, BNT), d_rhs[:, :, :n], d_rhs[:, :, n:]


_unit_lower_solve.defvjp(_unit_lower_solve_fwd, _unit_lower_solve_bwd)


@jax.custom_vjp
def _mmb_nt(a, b):
    return _dot(a, b, BNT)


def _mmb_nt_fwd(a, b):
    return _dot(a, b, BNT), (a, b)


def _mmb_nt_bwd(res, ct):
    a, b = res
    return _dot(ct, b, BNN), _dot(ct, a, BTN)


_mmb_nt.defvjp(_mmb_nt_fwd, _mmb_nt_bwd)


def _gdn_chunk(q, k, v, gcol, bcol):
    nb, C = q.shape[0], GDN_CHUNK
    row, col = _iota((nb, C, C), 1), _iota((nb, C, C), 2)
    incl, strict = row >= col, row > col
    eye = (row == col).astype(F32)
    lower = incl.astype(F32)
    ones = jnp.ones((nb, C, C), F32)
    gwide = jnp.broadcast_to(gcol, (nb, C, GDN_HEAD_DIM))
    gc = _mm01(lower, gwide)
    gtot = _mm01(ones, gwide)
    gc_c = _mm01(lower, jnp.broadcast_to(gcol, (nb, C, C)))
    gc_s = _mm01(ones, gc_c * eye)
    decay = jnp.where(incl, jnp.exp(jnp.where(incl, gc_c - gc_s, 0.0)), 0.0)
    kb = k * bcol
    a = jnp.where(strict, _mmb_nt(kb, k) * decay, 0.0)
    egc = jnp.exp(gc)
    u, w = _unit_lower_solve(a, v * bcol, kb * egc)
    qk = jnp.where(incl, _mmb_nt(q, k) * decay, 0.0)
    return u, w, qk, q * egc, k * jnp.exp(gtot - gc), jnp.exp(jnp.sum(gwide, axis=1))


GDN_ROWS = 8 * GDN_CHUNK


GDN_LOCAL_CHUNKS = 32


def _gdn_specs(T):
    nch = min(GDN_LOCAL_CHUNKS, T // GDN_CHUNK)
    L = nch * GDN_CHUNK
    hd = lambda off: pl.BlockSpec((1, L, 128), lambda h, i: (off + h, i, 0))
    col = pl.BlockSpec((1, L, 1), lambda h, i: (h, i, 0))
    sq = pl.BlockSpec((1, L, GDN_CHUNK), lambda h, i: (h, i, 0))
    gl = pl.BlockSpec((1, nch, 128), lambda h, i: (h, i, 0))
    return nch, hd, col, sq, gl


def _gdn_local(qkv, g, beta):
    T = qkv.shape[1]
    nch, hd, col, sq, gl_spec = _gdn_specs(T)

    def body(q_ref, k_ref, v_ref, g_ref, b_ref, u_ref, w_ref, qk_ref, qd_ref, kd_ref, gl_ref):
        chunks = lambda ref: ref[0].reshape(nch, GDN_CHUNK, ref.shape[2])
        rows = lambda val: val.reshape(nch * GDN_CHUNK, val.shape[2])
        u, w, qk, qd, kd, gl = _gdn_chunk(chunks(q_ref), chunks(k_ref), chunks(v_ref), chunks(g_ref), chunks(b_ref))
        u_ref[0] = rows(u)
        w_ref[0] = rows(w).astype(BF16)
        qk_ref[0] = rows(qk).astype(BF16)
        qd_ref[0] = rows(qd).astype(BF16)
        kd_ref[0] = rows(kd).astype(BF16)
        gl_ref[0] = gl

    H = GDN_HEADS
    return pl.pallas_call(
        body, name="gdn_local", grid=(H, T // (nch * GDN_CHUNK)),
        in_specs=[hd(0), hd(H), hd(2 * H), col, col],
        out_specs=[hd(0), hd(0), sq, hd(0), hd(0), gl_spec],
        out_shape=[jax.ShapeDtypeStruct((H, T, 128), F32), jax.ShapeDtypeStruct((H, T, 128), BF16),
                   jax.ShapeDtypeStruct((H, T, GDN_CHUNK), BF16), jax.ShapeDtypeStruct((H, T, 128), BF16),
                   jax.ShapeDtypeStruct((H, T, 128), BF16), jax.ShapeDtypeStruct((H, T // GDN_CHUNK, 128), F32)],
        compiler_params=_params("parallel", "parallel"),
    )(qkv, qkv, qkv, g, beta)


def _gdn_local_bwd(qkv, g, beta, du, dw, dqk, dqd, dkd, dgl):
    T = qkv.shape[1]
    nch, hd, col, sq, gl_spec = _gdn_specs(T)

    def body(q_ref, k_ref, v_ref, g_ref, b_ref, du_ref, dw_ref, dqk_ref, dqd_ref, dkd_ref, dgl_ref,
             dqkv_ref, dg_ref, db_ref):
        chunks = lambda ref: ref[0].reshape(nch, GDN_CHUNK, ref.shape[2])
        rows = lambda val: val.reshape(nch * GDN_CHUNK, val.shape[2])
        _, vjp = jax.vjp(_gdn_chunk, chunks(q_ref), chunks(k_ref), chunks(v_ref), chunks(g_ref), chunks(b_ref))
        dq, dk, dv, dg, db = vjp((chunks(du_ref), chunks(dw_ref), chunks(dqk_ref), chunks(dqd_ref), chunks(dkd_ref),
                                  dgl_ref[0]))
        dqkv_ref[0, 0] = rows(dq)
        dqkv_ref[1, 0] = rows(dk)
        dqkv_ref[2, 0] = rows(dv)
        dg_ref[0] = rows(dg)
        db_ref[0] = rows(db)

    H = GDN_HEADS
    small = jax.ShapeDtypeStruct((H, T, 1), F32)
    dqkv, dg, db = pl.pallas_call(
        body, name="gdn_local_bwd", grid=(H, T // (nch * GDN_CHUNK)),
        in_specs=[hd(0), hd(H), hd(2 * H), col, col, hd(0), hd(0), sq, hd(0), hd(0), gl_spec],
        out_specs=[pl.BlockSpec((3, 1, nch * GDN_CHUNK, 128), lambda h, i: (0, h, i, 0)), col, col],
        out_shape=[jax.ShapeDtypeStruct((3, H, T, 128), F32), small, small],
        compiler_params=_params("parallel", "parallel"),
    )(qkv, qkv, qkv, g, beta, du, dw, dqk, dqd, dkd, dgl)
    return dqkv.reshape(3 * H, T, 128), dg, db


GDN_HB = 8


def _gdn_scan_specs(T, rev):
    nb = T // GDN_ROWS
    blk = (lambda i: nb - 1 - i) if rev else (lambda i: i)
    hd = pl.BlockSpec((GDN_HB, GDN_ROWS, 128), lambda h, i: (h, blk(i), 0))
    sq = pl.BlockSpec((GDN_HB, GDN_ROWS, GDN_CHUNK), lambda h, i: (h, blk(i), 0))
    gl = pl.BlockSpec((GDN_HB, 8, 128), lambda h, i: (h, blk(i), 0))
    st = pl.BlockSpec((GDN_HB, 8, 128, 128), lambda h, i: (h, blk(i), 0, 0))
    return hd, sq, gl, st


def _gdn_scan(u, w, qk, qd, kd, gl):
    H, T, _ = u.shape
    hd, sq, gl_spec, st = _gdn_scan_specs(T, False)

    def body(u_ref, w_ref, qk_ref, qd_ref, kd_ref, gl_ref, o_ref, ss_ref, vn_ref, s_scr):
        @pl.when(pl.program_id(1) == 0)
        def _():
            s_scr[...] = jnp.zeros_like(s_scr)

        dot = lambda a, b, dims: lax.dot_general(a, b, dims, preferred_element_type=F32)
        s = s_scr[...]
        for c in range(8):
            rs = slice(GDN_CHUNK * c, GDN_CHUNK * (c + 1))
            ss_ref[:, c] = s
            sb = s.astype(BF16)
            vn = u_ref[:, rs, :] - dot(w_ref[:, rs, :], sb, BNN)
            vnb = vn.astype(BF16)
            o_ref[:, rs, :] = dot(qd_ref[:, rs, :], sb, BNN) + dot(qk_ref[:, rs, :], vnb, BNN)
            vn_ref[:, rs, :] = vnb
            s = s * gl_ref[:, c:c + 1, :] + dot(kd_ref[:, rs, :], vnb, BTN)
        s_scr[...] = s

    return pl.pallas_call(
        body, name="gdn_scan", grid=(H // GDN_HB, T // GDN_ROWS),
        in_specs=[hd, hd, sq, hd, hd, gl_spec], out_specs=[hd, st, hd],
        out_shape=[jax.ShapeDtypeStruct((H, T, 128), F32), jax.ShapeDtypeStruct((H, T // GDN_CHUNK, 128, 128), F32),
                   jax.ShapeDtypeStruct((H, T, 128), BF16)],
        scratch_shapes=[pltpu.VMEM((GDN_HB, 128, 128), F32)],
        compiler_params=_params("parallel", "arbitrary"),
    )(u, w, qk, qd, kd, gl)


def _gdn_scan_bwd(do, ss, vn, w, qk, qd, kd, gl):
    H, T, _ = do.shape
    hd, sq, gl_spec, st = _gdn_scan_specs(T, True)

    def body(do_ref, ss_ref, vn_ref, w_ref, qk_ref, qd_ref, kd_ref, gl_ref,
             du_ref, dw_ref, dqk_ref, dqd_ref, dkd_ref, dgl_ref, ds_scr):
        @pl.when(pl.program_id(1) == 0)
        def _():
            ds_scr[...] = jnp.zeros_like(ds_scr)

        dot = lambda a, b, dims: lax.dot_general(a, b, dims, preferred_element_type=F32)
        ds = ds_scr[...]
        for c in reversed(range(8)):
            rs = slice(GDN_CHUNK * c, GDN_CHUNK * (c + 1))
            s = ss_ref[:, c]
            sb, dsb = s.astype(BF16), ds.astype(BF16)
            dob = do_ref[:, rs, :].astype(BF16)
            vnb = vn_ref[:, rs, :]
            dvn = dot(qk_ref[:, rs, :], dob, BTN) + dot(kd_ref[:, rs, :], dsb, BNN)
            dvnb = dvn.astype(BF16)
            du_ref[:, rs, :] = dvn
            dw_ref[:, rs, :] = -dot(dvnb, sb, BNT)
            dqk_ref[:, rs, :] = dot(dob, vnb, BNT)
            dqd_ref[:, rs, :] = dot(dob, sb, BNT)
            dkd_ref[:, rs, :] = dot(vnb, dsb, BNT)
            dgl_ref[:, c:c + 1, :] = jnp.sum(ds * s, axis=1, keepdims=True)
            ds = dot(qd_ref[:, rs, :], dob, BTN) + ds * gl_ref[:, c:c + 1, :] - dot(w_ref[:, rs, :], dvnb, BTN)
        ds_scr[...] = ds

    big = jax.ShapeDtypeStruct((H, T, 128), F32)
    return pl.pallas_call(
        body, name="gdn_scan_bwd", grid=(H // GDN_HB, T // GDN_ROWS),
        in_specs=[hd, st, hd, hd, sq, hd, hd, gl_spec], out_specs=[hd, hd, sq, hd, hd, gl_spec],
        out_shape=[big, big, jax.ShapeDtypeStruct((H, T, GDN_CHUNK), F32), big, big,
                   jax.ShapeDtypeStruct((H, T // GDN_CHUNK, 128), F32)],
        scratch_shapes=[pltpu.VMEM((GDN_HB, 128, 128), F32)],
        compiler_params=_params("parallel", "arbitrary"),
    )(do, ss, vn, w, qk, qd, kd, gl)


def _gated_norm(o, z, nw):
    on = o * lax.rsqrt(jnp.mean(o * o, axis=-1, keepdims=True) + EPS) * nw
    return on * (z * _sigmoid(z))


def _gdn_post(o, proj, norm_w, *, tr=512):
    T = proj.shape[0]

    def body(o_ref, z_ref, n_ref, y_ref):
        y_ref[...] = _gated_norm(o_ref[0], z_ref[...], n_ref[...]).astype(BF16)

    return pl.pallas_call(
        body, name="gdn_post", grid=(T // tr, GDN_HEADS),
        in_specs=[pl.BlockSpec((1, tr, 128), lambda i, h: (h, i, 0)), pl.BlockSpec((tr, 128), lambda i, h: (i, 24 + h)),
                  pl.BlockSpec((1, 128), lambda i, h: (0, 0))],
        out_specs=pl.BlockSpec((tr, 128), lambda i, h: (i, h)),
        out_shape=jax.ShapeDtypeStruct((T, 1024), BF16), compiler_params=_params("parallel", "parallel"),
    )(o, proj, norm_w)


def _gdn_post_bwd(o, proj, norm_w, dy, *, tr=512):
    T = proj.shape[0]

    def body(o_ref, z_ref, n_ref, dy_ref, do_ref, dz_ref, dn_ref):
        first = (pl.program_id(0) == 0) & (pl.program_id(1) == 0)
        _, vjp = jax.vjp(_gated_norm, o_ref[0], z_ref[...], n_ref[...])
        do, dz, dn = vjp(dy_ref[...])
        do_ref[0] = do
        dz_ref[...] = dz.astype(BF16)

        @pl.when(first)
        def _():
            dn_ref[...] = dn

        @pl.when(jnp.logical_not(first))
        def _():
            dn_ref[...] += dn

    blk = pl.BlockSpec((tr, 128), lambda i, h: (i, h))
    hm = pl.BlockSpec((1, tr, 128), lambda i, h: (h, i, 0))
    vec = pl.BlockSpec((1, 128), lambda i, h: (0, 0))
    return pl.pallas_call(
        body, name="gdn_post_bwd", grid=(T // tr, GDN_HEADS),
        in_specs=[hm, pl.BlockSpec((tr, 128), lambda i, h: (i, 24 + h)), vec, blk], out_specs=[hm, blk, vec],
        out_shape=[jax.ShapeDtypeStruct((GDN_HEADS, T, 128), F32), jax.ShapeDtypeStruct((T, 1024), BF16),
                   jax.ShapeDtypeStruct((1, 128), F32)],
        compiler_params=_params("arbitrary", "arbitrary"),
    )(o, proj, norm_w, dy)


HBM_SPEC = pl.BlockSpec(memory_space=pltpu.HBM)


def _place():
    return lax.axis_index("x"), lax.axis_index("y"), lax.axis_index("c")


def _all_gather(vs, *, name):
    n = len(vs)

    def body(*refs):
        start, forward, finish = _gather_stages(refs[:n], refs[n:2 * n], *refs[2 * n:])
        start()
        forward()
        finish()

    return pl.pallas_call(
        body, name=name, out_shape=_gather_shapes(vs), in_specs=[HBM_SPEC] * n, out_specs=[HBM_SPEC] * n,
        scratch_shapes=_gather_sems(n),
    )(*vs)


def _gather_shapes(vs):
    return [jax.ShapeDtypeStruct((N_DEV,) + v.shape, v.dtype) for v in vs]


def _gather_sems(n):
    return [pltpu.SemaphoreType.DMA((7 * n,)), pltpu.SemaphoreType.DMA((7 * n,)), pltpu.SemaphoreType.DMA((n,))]


def _gather_stages(v_refs, out_refs, send_sems, recv_sems, local_sems):
    n = len(v_refs)
    x, y, c = _place()
    me, sibling = (x, y, c), (x, y, 1 - c)
    chips = [(1 - x, y), (x, 1 - y), (1 - x, 1 - y)]

    def copy(a, k, block, to, from_input=False):
        slot = out_refs[a].at[4 * block[0] + 2 * block[1] + block[2]]
        return pltpu.make_async_remote_copy(
            src_ref=v_refs[a] if from_input else slot, dst_ref=slot,
            send_sem=send_sems.at[7 * a + k], recv_sem=recv_sems.at[7 * a + k], device_id=to, device_id_type=MESH)

    def mine():
        return [pltpu.make_async_copy(v_refs[a], out_refs[a].at[4 * x + 2 * y + c], local_sems.at[a]) for a in range(n)]

    def first():
        return ([copy(a, 0, me, sibling, True) for a in range(n)]
                + [copy(a, 1 + j, me, (*chip, c), True) for j, chip in enumerate(chips) for a in range(n)])

    def passed():
        return [copy(a, 4 + j, (*chip, c), sibling) for j, chip in enumerate(chips) for a in range(n)]

    def start():
        for cp in mine() + first():
            cp.start()

    def forward():
        for j, chip in enumerate(chips):
            for a in range(n):
                copy(a, 1 + j, (*chip, c), me).wait_recv()
                copy(a, 4 + j, (*chip, c), sibling).start()

    def finish():
        for a in range(n):
            copy(a, 0, sibling, me).wait_recv()
            for j, chip in enumerate(chips):
                copy(a, 4 + j, (*chip, 1 - c), me).wait_recv()
        for cp in first() + passed():
            cp.wait_send()
        for cp in mine():
            cp.wait()

    return start, forward, finish


def _exchange_sibling(gs, *, name):
    n = len(gs)

    def body(*refs):
        g_refs, out_refs = refs[:n], refs[n:2 * n]
        send_sems, recv_sems = refs[2 * n:]
        x, y, c = _place()
        copies = [pltpu.make_async_remote_copy(
            src_ref=g_refs[a].at[k, 1 - c], dst_ref=out_refs[a].at[k], send_sem=send_sems.at[4 * a + k],
            recv_sem=recv_sems.at[4 * a + k], device_id=(x, y, 1 - c), device_id_type=MESH)
            for a in range(n) for k in range(4)]
        for cp in copies:
            cp.start()
        for cp in copies:
            cp.wait()

    return pl.pallas_call(
        body, name=name, out_shape=[jax.ShapeDtypeStruct((4,) + g.shape[2:], g.dtype) for g in gs],
        in_specs=[HBM_SPEC] * n, out_specs=[HBM_SPEC] * n,
        scratch_shapes=[pltpu.SemaphoreType.DMA((4 * n,)), pltpu.SemaphoreType.DMA((4 * n,))],
    )(*gs)


def _exchange_chips(pcs):
    n = len(pcs)

    def body(*refs):
        start, finish = _chips_stages(refs[:n], refs[n:2 * n], *refs[2 * n:])
        start()
        finish()

    return pl.pallas_call(
        body, name="rs_chips", out_shape=_chips_shapes(pcs), in_specs=[HBM_SPEC] * n, out_specs=[HBM_SPEC] * n,
        scratch_shapes=_chips_sems(n),
    )(*pcs)


def _chips_shapes(pcs):
    return [jax.ShapeDtypeStruct((3,) + pc.shape[1:], pc.dtype) for pc in pcs]


def _chips_sems(n):
    return [pltpu.SemaphoreType.DMA((3 * n,)), pltpu.SemaphoreType.DMA((3 * n,))]


def _chips_stages(p_refs, out_refs, send_sems, recv_sems):
    n = len(p_refs)
    x, y, c = _place()
    chips = [(1 - x, y), (x, 1 - y), (1 - x, 1 - y)]

    def copies():
        return [pltpu.make_async_remote_copy(
            src_ref=p_refs[a].at[2 * cx + cy], dst_ref=out_refs[a].at[j], send_sem=send_sems.at[3 * a + j],
            recv_sem=recv_sems.at[3 * a + j], device_id=(cx, cy, c), device_id_type=MESH)
            for j, (cx, cy) in enumerate(chips) for a in range(n)]

    def start():
        for cp in copies():
            cp.start()

    def finish():
        for cp in copies():
            cp.wait()

    return start, finish


def _chip_partial(place, g, got, *, tr, name):
    R, W = g.shape[2:]

    def body(pl_ref, g_ref, r_ref, o_ref):
        o_ref[...] = (g_ref[0] + r_ref[...]).astype(BF16)

    def chip(j, pr):
        return 2 * jnp.where(j == 1, pr[0], 1 - pr[0]) + jnp.where(j == 0, pr[1], 1 - pr[1])

    return pl.pallas_call(
        body, name=name, out_shape=jax.ShapeDtypeStruct((4, R, W), BF16),
        grid_spec=pltpu.PrefetchScalarGridSpec(
            num_scalar_prefetch=1, grid=(3, R // tr),
            in_specs=[pl.BlockSpec((1, 1, tr, W), lambda j, i, pr: (chip(j, pr), pr[2], i, 0)),
                      pl.BlockSpec((1, tr, W), lambda j, i, pr: (chip(j, pr), i, 0))],
            out_specs=pl.BlockSpec((1, tr, W), lambda j, i, pr: (chip(j, pr), i, 0))),
        compiler_params=_params("parallel", "parallel"),
    )(place, g, got)


def _adamw_math(g, w, m, v):
    m = ADAM_B1 * m + (1.0 - ADAM_B1) * g
    v = ADAM_B2 * v + (1.0 - ADAM_B2) * (g * g)
    m_hat = m / (1.0 - ADAM_B1 ** ADAM_STEP)
    v_hat = v / (1.0 - ADAM_B2 ** ADAM_STEP)
    return -ADAM_LR * (m_hat / (jnp.sqrt(v_hat) + ADAM_EPS) + ADAM_WD * w), m, v


def _adamw_shard(place, g, got1, got2, w, m, v, *, tr, name):
    R, W = w.shape

    def body(pl_ref, g_ref, r1_ref, r2_ref, w_ref, m_ref, v_ref, go_ref, d_ref, mo_ref, vo_ref):
        gs = g_ref[0, 0] + r1_ref[0]
        for j in range(3):
            gs = gs + r2_ref[j].astype(F32)
        go_ref[...] = gs
        d_ref[...], mo_ref[...], vo_ref[...] = _adamw_math(gs, w_ref[...], m_ref[...], v_ref[...])

    row = pl.BlockSpec((tr, W), lambda i, pr: (i, 0))
    out = jax.ShapeDtypeStruct((R, W), F32)
    return pl.pallas_call(
        body, name=name, out_shape=[out] * 4,
        grid_spec=pltpu.PrefetchScalarGridSpec(
            num_scalar_prefetch=1, grid=(R // tr,),
            in_specs=[pl.BlockSpec((1, 1, tr, W), lambda i, pr: (2 * pr[0] + pr[1], pr[2], i, 0)),
                      pl.BlockSpec((1, tr, W), lambda i, pr: (2 * pr[0] + pr[1], i, 0)),
                      pl.BlockSpec((3, tr, W), lambda i, pr: (0, i, 0)), row, row, row],
            out_specs=[row] * 4),
        compiler_params=_params("parallel"),
    )(place, g, got1, got2, w, m, v)


def _exchange_direct(g, *, name):
    def body(g_ref, out_ref, send_sems, recv_sems, local_sem):
        x, y, c = _place()
        me = 4 * x + 2 * y + c

        def copy(t):
            return pltpu.make_async_remote_copy(
                src_ref=g_ref.at[t], dst_ref=out_ref.at[me], send_sem=send_sems.at[t], recv_sem=recv_sems.at[me],
                device_id=(t // 4, (t // 2) % 2, t % 2), device_id_type=MESH)

        mine = pltpu.make_async_copy(g_ref.at[me], out_ref.at[me], local_sem)
        mine.start()
        for t in range(N_DEV):
            pl.when(t != me)(copy(t).start)
        for t in range(N_DEV):
            @pl.when(t != me)
            def _():
                pltpu.make_async_remote_copy(
                    src_ref=g_ref.at[t], dst_ref=out_ref.at[t], send_sem=send_sems.at[t], recv_sem=recv_sems.at[t],
                    device_id=(t // 4, (t // 2) % 2, t % 2), device_id_type=MESH).wait()
        mine.wait()

    return pl.pallas_call(
        body, name=name, out_shape=jax.ShapeDtypeStruct(g.shape, g.dtype), in_specs=[HBM_SPEC], out_specs=HBM_SPEC,
        scratch_shapes=[pltpu.SemaphoreType.DMA((N_DEV,)), pltpu.SemaphoreType.DMA((N_DEV,)), pltpu.SemaphoreType.DMA],
    )(g)


def _adamw_replicated(parts, w, m, v, *, name="adamw_replicated"):
    R, W = w.shape

    def body(p_ref, w_ref, m_ref, v_ref, go_ref, d_ref, mo_ref, vo_ref):
        gs = p_ref[0]
        for j in range(1, N_DEV):
            gs = gs + p_ref[j]
        go_ref[...] = gs
        d_ref[...], mo_ref[...], vo_ref[...] = _adamw_math(gs, w_ref[...], m_ref[...], v_ref[...])

    full = pl.BlockSpec((R, W), lambda i: (0, 0))
    out = jax.ShapeDtypeStruct((R, W), F32)
    return pl.pallas_call(
        body, name="adamw_replicated", grid=(1,), out_shape=[out] * 4,
        in_specs=[pl.BlockSpec((N_DEV, R, W), lambda i: (0, 0, 0)), full, full, full], out_specs=[full] * 4,
        compiler_params=_params("arbitrary"),
    )(parts, w, m, v)


GROUPS = {
    "in_e": (256, 512, ((("w_in_e", None), 1024, 1024),)),
    "in_o": (514, 512, ((("w_in_o", None), 1024, 1024),)),
    "up0": (704, 256, ((("w_up", 0), 1024, 1024),)),
    "up1": (704, 256, ((("w_up", 1), 1024, 1024),)),
    "down0": (1024, 176, ((("w_down", 0), 352, 352),)),
    "down1": (1024, 176, ((("w_down", 1), 352, 352),)),
    "square": (1024, 256, ((("w_out_e", None), 128, 128), (("w_out_o", None), 128, 128), (("w_ple_gate", None), 256, 256))),
    "ple": (128, 512, ((("w_ple", None), 512, 512),)),
    "conv_f": (704, 8, ((("ffn_conv", None), 6, 8),)),
    "norm_o": (128, 8, ((("mix_norm_o", None), 1, 8),)),
    "conv_o": (384, 8, ((("conv_qkv_o", None), 4, 8),)),
}
SHARDED = tuple(dict.fromkeys(p[0][0] for g in GROUPS.values() for p in g[2]))
COLUMN_SHARDED = ("w_in_e", "w_in_o", "w_up", "ffn_conv", "w_ple", "conv_qkv_o", "mix_norm_o")
PACK_W = 1024
REPL_LAYOUT = (
    ("mix_norm_e", (1, 1024), 8), ("pool_w", (1, 4, 128, 128), 64), ("pool_scale", (1, 512), 8),
    ("a_log_o", (1, 8), 8), ("dt_bias_o", (1, 8), 8), ("gdn_norm_o", (1, 128), 8),
    ("ffn_norm", (2, 1024), 8), ("ple_norm", (2, 1024), 8), ("final_norm", (1024,), 8),
)


def _pad_rows(a, rows):
    extra = rows - a.shape[-2]
    return a if extra == 0 else jnp.pad(a, [(0, 0)] * (a.ndim - 2) + [(0, extra), (0, 0)])


def _group_rows(pieces, gname):
    parts = [_pad_rows(pieces[name], padded) for name, _, padded in GROUPS[gname][2]]
    return parts[0] if len(parts) == 1 else jnp.concatenate(parts, axis=-2)


def _ungroup_rows(buf, gname):
    out, r0 = {}, 0
    for name, rows, padded in GROUPS[gname][2]:
        out[name] = buf[..., r0:r0 + rows, :]
        r0 += padded
    return out


def _shard_major(name, gfull, n_layers):
    per_layer = []
    for g in gfull:
        if g.ndim == 3:
            per_layer.append(g)
        elif name in COLUMN_SHARDED:
            k = g.shape[0]
            per_layer.append(jnp.moveaxis(g.reshape(k, N_DEV, g.shape[1] // N_DEV), 1, 0))
        else:
            per_layer.append(g.reshape(N_DEV, g.shape[0] // N_DEV, -1))
    return per_layer[0] if n_layers == 1 else jnp.concatenate(per_layer, axis=1)


def _natural(name, gathered, n_layers):
    rows = gathered.shape[1] // n_layers
    out = []
    for layer in range(n_layers):
        piece = gathered[:, layer * rows:(layer + 1) * rows]
        if name in COLUMN_SHARDED:
            out.append(jnp.moveaxis(piece, 0, 1).reshape(rows, N_DEV * piece.shape[2]))
        else:
            out.append(piece.reshape(N_DEV * rows, piece.shape[2]))
    return out


def _unshard_pairs(g, layer, rows, tag):
    n = g.shape[2]

    def body(g_ref, o_ref):
        o_ref[...] = jnp.concatenate([g_ref[0], g_ref[1]], axis=1)

    return pl.pallas_call(
        body, name="unshard_%s%d" % (tag, layer), grid=(N_DEV // 2,),
        in_specs=[pl.BlockSpec((2, rows, n), lambda p: (p, layer, 0))],
        out_specs=pl.BlockSpec((rows, 2 * n), lambda p: (0, p)),
        out_shape=jax.ShapeDtypeStruct((rows, N_DEV * n), g.dtype), compiler_params=_params("parallel"),
    )(g)


def _unshard_in_o(g, *, tr=256):
    rows, n = g.shape[1:]
    main = (N_DEV * n) // 128 * 128

    def body(g_ref, m_ref, b_ref):
        cat = jnp.concatenate([g_ref[i] for i in range(N_DEV)], axis=1)
        m_ref[...] = cat[:, :main]
        b_ref[...] = jnp.concatenate([cat[:, main:], jnp.zeros((tr, 128 - (N_DEV * n - main)), g.dtype)], axis=1)

    return pl.pallas_call(
        body, name="unshard_in_o", grid=(rows // tr,),
        in_specs=[pl.BlockSpec((N_DEV, tr, n), lambda p: (0, p, 0))],
        out_specs=[pl.BlockSpec((tr, main), lambda p: (p, 0)), pl.BlockSpec((tr, 128), lambda p: (p, 0))],
        out_shape=[jax.ShapeDtypeStruct((rows, main), g.dtype), jax.ShapeDtypeStruct((rows, 128), g.dtype)],
        compiler_params=_params("parallel"),
    )(g)


def _reshard_in_o(d_main, d_ba, n, *, tr=256):
    rows, main = d_main.shape
    tail = N_DEV * n - main

    def body(m_ref, b_ref, o_ref):
        cat = jnp.concatenate([m_ref[...], b_ref[:, :tail]], axis=1)
        for i in range(N_DEV):
            o_ref[i] = cat[:, n * i:n * (i + 1)]

    return pl.pallas_call(
        body, name="reshard_in_o", grid=(rows // tr,),
        in_specs=[pl.BlockSpec((tr, main), lambda p: (p, 0)), pl.BlockSpec((tr, 128), lambda p: (p, 0))],
        out_specs=pl.BlockSpec((N_DEV, tr, n), lambda p: (0, p, 0)),
        out_shape=jax.ShapeDtypeStruct((N_DEV, rows, n), F32), compiler_params=_params("parallel"),
    )(d_main, d_ba)


def _rows(a, rows):
    flat = a.reshape(-1)
    return jnp.pad(flat, (0, rows * PACK_W - flat.shape[0])).reshape(rows, PACK_W)


def _pack_repl(vals):
    return jnp.concatenate([_rows(vals[name].reshape(shape), rows) for name, shape, rows in REPL_LAYOUT], axis=0)


def _unpack_repl(buf):
    out, r0 = {}, 0
    for name, shape, rows in REPL_LAYOUT:
        n = 1
        for s in shape:
            n *= s
        out[name] = buf[r0:r0 + rows].reshape(-1)[:n].reshape(shape)
        r0 += rows
    return out


WEIGHTS = ("mix_norm_e", "w_in_e", "pool_w", "pool_scale", "w_out_e", "mix_norm_o", "w_in_o", "conv_qkv_o", "a_log_o",
           "dt_bias_o", "gdn_norm_o", "w_out_o", "ffn_norm", "w_up", "ffn_conv", "w_down", "ple_norm", "w_ple_gate",
           "w_ple", "final_norm")


def _ffn_forward(x, hn, w_up, conv_w, w_down, next_gain, tag):
    up = _mm(hn, w_up, name="ffn_up" + tag)
    act = _ffn_act(up, conv_w)
    out, out_n = _mm(act, w_down, res=x, norm_gain=next_gain, name="ffn_down" + tag)
    return out, out_n, (x, hn, up, act)


def _ffn_backward(dx, saved, norm_g, w_up, conv_w, w_down, tag):
    x, hn, up, act = saved
    dact = _mm(dx, w_down, tb=True, name="ffn_dact" + tag)
    d_w_down = _mm(act, dx, ta=True, name="ffn_dwdown" + tag)
    dgate, dval, dcg, dcv = _ffn_act_bwd(up, conv_w, dact)
    dhn = _mm(dgate, w_up, tb=True, name="ffn_dhn_g" + tag)
    dx_in, d_norm = _mm(dval, w_up, tb=True, k_block0=FFN_DIM // _mm_tile(FFN_DIM), res=dhn,
                        rms_bwd=(x, norm_g, dx), name="ffn_dhn_v" + tag)
    shard = w_up.shape[1] // N_DEV
    d_w_up = _mm(hn, dgate, ta=True, shard_cols=shard, shard_into=(N_DEV, 0, None), name="ffn_dwup_g" + tag)
    d_w_up = _mm(hn, dval, ta=True, shard_cols=shard, shard_into=(N_DEV, N_DEV // 2, d_w_up), name="ffn_dwup_v" + tag)
    return dx_in, d_norm, d_w_up, jnp.concatenate([dcg, dcv], axis=1), d_w_down


def _ple_forward(x, hn, w_gate, p, w_ple, tag):
    out, gl, pe = _ple_fwd(hn, w_gate, p, w_ple, x, name="ple_fwd" + tag)
    return out, (x, hn, gl, pe)


def _ple_backward(dx, saved, norm_g, w_gate, p, tag):
    x, hn, gl, pe = saved
    dpe, dgl = _ple_bwd(dx, gl, pe, name="ple_bwd" + tag)
    d_w_ple = _mm(p, dpe, ta=True, shard_cols=dpe.shape[1] // N_DEV, name="ple_dwple" + tag)
    d_w_gate = _mm(hn, dgl, ta=True, name="ple_dwgate" + tag)
    dx, d_norm = _mm(dgl, w_gate, tb=True, rms_bwd=(x, norm_g, dx), name="ple_dhn" + tag)
    return dx, d_norm, d_w_gate, d_w_ple


def kernel(x, p, mix_norm_e, w_in_e, pool_w, pool_scale, w_out_e, mix_norm_o, w_in_o, conv_qkv_o, a_log_o, dt_bias_o, gdn_norm_o, w_out_o, ffn_norm, w_up, ffn_conv, w_down, ple_norm, w_ple_gate, w_ple, final_norm, loss_target, m_mix_norm_e, m_w_in_e, m_pool_w, m_pool_scale, m_w_out_e, m_mix_norm_o, m_w_in_o, m_conv_qkv_o, m_a_log_o, m_dt_bias_o, m_gdn_norm_o, m_w_out_o, m_ffn_norm, m_w_up, m_ffn_conv, m_w_down, m_ple_norm, m_w_ple_gate, m_w_ple, m_final_norm, v_mix_norm_e, v_w_in_e, v_pool_w, v_pool_scale, v_w_out_e, v_mix_norm_o, v_w_in_o, v_conv_qkv_o, v_a_log_o, v_dt_bias_o, v_gdn_norm_o, v_w_out_o, v_ffn_norm, v_w_up, v_ffn_conv, v_w_down, v_ple_norm, v_w_ple_gate, v_w_ple, v_final_norm):
    given = dict(locals())
    place = jnp.stack(_place()).astype(jnp.int32)
    x0, tgt = x[0], loss_target[0]

    def pieces(prefix):
        out = {}
        for width, _, members in GROUPS.values():
            for (name, layer), rows, _ in members:
                a = given[prefix + name]
                out[(name, layer)] = (a if layer is None else a[layer]).reshape(rows, width)
        return out

    def flat2d(name):
        return given[name].reshape(-1, given[name].shape[-1])

    small = ("ffn_conv", "mix_norm_o", "conv_qkv_o")
    got = _all_gather([flat2d("w_in_e").astype(BF16)] + [_pad_rows(flat2d(k), 8) for k in small], name="ag_first")
    full = {("w_in_e", 0): _unshard_pairs(got[0], 0, D_MODEL, "in_e")}
    for i in range(2):
        full[("ffn_conv", i)] = _natural("ffn_conv", got[1][:, 3 * i:3 * i + 3], 1)[0]
    mix_norm_o_full = got[2][:, 0].reshape(1, D_MODEL)
    conv_qkv = _natural("conv_qkv_o", got[3][:, :4], 1)[0]
    alog_row = jnp.pad(a_log_o, ((0, 0), (8, 112)))
    dt_row = jnp.pad(dt_bias_o, ((0, 0), (8, 112)))
    lw = lambda name, i: full[(name, i)]

    h_e = _rms_fwd(x0, mix_norm_e, name="rms_mix_e")
    proj_e = _mm(h_e, lw("w_in_e", 0), name="in_e")
    pool_o = _pool_fwd(proj_e, pool_w[0], pool_scale)
    wide = ("w_out_e", "w_out_o", "w_down", "w_ple_gate")
    send = [jnp.concatenate([flat2d(k).astype(BF16) for k in wide], axis=0)]
    att_o, lsum, got = _sb_fwd(proj_e, gather=send + [flat2d(k).astype(BF16) for k in ("w_in_o", "w_up", "w_ple")])
    gathered, r0 = {}, 0
    for k in wide:
        gathered[k] = got[0][:, r0:r0 + flat2d(k).shape[0]]
        r0 += flat2d(k).shape[0]
    layers = {name: given[name].shape[0] if given[name].ndim == 3 else 1 for name in gathered}
    full.update({(name, i): w for name in gathered for i, w in enumerate(_natural(name, gathered[name], layers[name]))})
    w_in_o_main, w_in_o_ba = _unshard_in_o(got[1])
    for i in range(2):
        full[("w_up", i)] = _unshard_pairs(got[2], i, D_MODEL, "up")
        full[("w_ple", i)] = _unshard_pairs(got[3], i, PLE_DIM, "ple")
    mix_e = jnp.concatenate([pool_o, att_o.astype(BF16)], axis=1)
    x1, hf0 = _mm(mix_e, lw("w_out_e", 0), res=x0, norm_gain=ffn_norm[0:1], name="out_e")
    x2, hp0, ffn0 = _ffn_forward(x1, hf0, lw("w_up", 0), lw("ffn_conv", 0), lw("w_down", 0), ple_norm[0:1], "0")
    x3, ple0 = _ple_forward(x2, hp0, lw("w_ple_gate", 0), p[0, 0], lw("w_ple", 0), "0")

    h_o = _rms_fwd(x3, mix_norm_o_full, name="rms_mix_o")
    proj_o = _mm(h_o, w_in_o_main, name="in_o")
    ba = _mm(h_o, w_in_o_ba, name="in_o_ba")
    qkv = _gdn_pre(proj_o, conv_qkv)
    beta, g = _gdn_gate(ba, alog_row, dt_row)
    u, w_c, qk, qd, kd, gl = _gdn_local(qkv, g, beta)
    o, states, vnew = _gdn_scan(u, w_c, qk, qd, kd, gl)
    y_o = _gdn_post(o, proj_o, gdn_norm_o)
    x4, hf1 = _mm(y_o, lw("w_out_o", 0), res=x3, norm_gain=ffn_norm[1:2], name="out_o")
    x5, hp1, ffn1 = _ffn_forward(x4, hf1, lw("w_up", 1), lw("ffn_conv", 1), lw("w_down", 1), ple_norm[1:2], "1")
    x6, ple1 = _ple_forward(x5, hp1, lw("w_ple_gate", 1), p[1, 0], lw("w_ple", 1), "1")
    loss_row, dx, d_final = _final_loss(x6, final_norm.reshape(1, D_MODEL), tgt)

    grads, rgrads = {}, {}
    dx, d_ple1, grads[("w_ple_gate", 1)], grads[("w_ple", 1)] = _ple_backward(dx, ple1, ple_norm[1:2], lw("w_ple_gate", 1), p[1, 0], "1")
    dx, d_ffn1, grads[("w_up", 1)], grads[("ffn_conv", 1)], grads[("w_down", 1)] = _ffn_backward(
        dx, ffn1, ffn_norm[1:2], lw("w_up", 1), lw("ffn_conv", 1), lw("w_down", 1), "1")
    grads[("w_out_o", 0)] = _mm(y_o, dx, ta=True, name="dw_out_o")
    dy_o = _mm(dx, lw("w_out_o", 0), tb=True, name="dy_o")
    do, dz, rgrads["gdn_norm_o"] = _gdn_post_bwd(o, proj_o, gdn_norm_o, dy_o)
    du, dw_c, dqk, dqd, dkd, dgl = _gdn_scan_bwd(do, states, vnew, w_c, qk, qd, kd, gl)
    dqkv_heads, dg, dbeta = _gdn_local_bwd(qkv, g, beta, du, dw_c, dqk, dqd, dkd, dgl)
    dqkv, grads[("conv_qkv_o", 0)] = _gdn_pre_bwd(proj_o, conv_qkv, dqkv_heads)
    dba, d_alog, d_dt = _gdn_gate_bwd(ba, alog_row, dt_row, dbeta, dg)
    rgrads["a_log_o"], rgrads["dt_bias_o"] = d_alog[:, 8:16], d_dt[:, 8:16]
    dproj_o = jnp.concatenate([dqkv, dz], axis=1)
    dh = _mm(dproj_o, w_in_o_main, tb=True, name="dh_o")
    dx_o, d_mix_o = _mm(dba, w_in_o_ba, tb=True, res=dh, rms_bwd=(x3, mix_norm_o_full, dx), name="dh_o_ba")
    grads[("w_in_o", 0)] = _reshard_in_o(_mm(h_o, dproj_o, ta=True, name="dw_in_o"),
                                         _mm(h_o, dba, ta=True, name="dw_in_o_ba"), w_in_o.shape[2])
    dx = dx_o
    grads[("mix_norm_o", 0)] = d_mix_o

    dx, d_ple0, grads[("w_ple_gate", 0)], grads[("w_ple", 0)] = _ple_backward(dx, ple0, ple_norm[0:1], lw("w_ple_gate", 0), p[0, 0], "0")
    dx, d_ffn0, grads[("w_up", 0)], grads[("ffn_conv", 0)], grads[("w_down", 0)] = _ffn_backward(
        dx, ffn0, ffn_norm[0:1], lw("w_up", 0), lw("ffn_conv", 0), lw("w_down", 0), "0")
    grads[("w_out_e", 0)] = _mm(mix_e, dx, ta=True, name="dw_out_e")
    dmix = _mm(dx, lw("w_out_e", 0), tb=True, name="dmix_e")
    du_e, d_pool_w, rgrads["pool_scale"] = _pool_bwd(proj_e, dmix, pool_w[0], pool_scale)
    rgrads["pool_w"] = d_pool_w[None]

    def reduce_start(gnames, tag):
        smaj = {}
        for g in gnames:
            for (name, layer), _, _ in GROUPS[g][2]:
                of = [grads[(name, i)] for i in ((0, 1) if layer is None else (layer,)) if (name, i) in grads]
                smaj[(name, layer)] = _shard_major(name, of, len(of))
        gbuf = [_group_rows(smaj, g) for g in gnames]
        gbuf = [b.reshape((4, 2) + b.shape[1:]) for b in gbuf]
        got1 = _exchange_sibling(gbuf, name="rs_sibling" + tag)
        part = [_chip_partial(place, b, r, tr=GROUPS[g][1], name="rs_chip_partial_" + g)
                for g, b, r in zip(gnames, gbuf, got1)]
        return gbuf, got1, part

    early = tuple(g for g in GROUPS if g != "in_e")
    gbuf_e, got1_e, part_e = reduce_start(early, "_early")
    dq_e, dk_e, dv_e, got2_e = _sb_bwd(proj_e, lsum, dmix, exchange=part_e)
    dproj_e = jnp.concatenate([du_e, dq_e.astype(BF16), dk_e.astype(BF16), dv_e.astype(BF16)], axis=1)
    grads[("w_in_e", 0)] = _mm(h_e, dproj_e, ta=True, shard_cols=w_in_e.shape[2], name="dw_in_e")
    dx, rgrads["mix_norm_e"] = _mm(dproj_e, lw("w_in_e", 0), tb=True, rms_bwd=(x0, mix_norm_e, dx), name="dh_e")
    rgrads["ffn_norm"] = jnp.concatenate([d_ffn0, d_ffn1], axis=0)
    rgrads["ple_norm"] = jnp.concatenate([d_ple0, d_ple1], axis=0)
    rgrads["final_norm"] = d_final.reshape(D_MODEL)

    wloc, mloc, vloc = pieces(""), pieces("m_"), pieces("v_")
    late = _exchange_direct(grads[("w_in_e", 0)], name="rs_direct_late")
    res = _adamw_replicated(late, *(_group_rows(loc, "in_e") for loc in (wloc, mloc, vloc)), name="adamw_in_e")
    sh_out = [_ungroup_rows(r, "in_e") for r in res]
    for g, b, r1, r2 in zip(early, gbuf_e, got1_e, got2_e):
        res = _adamw_shard(place, b, r1, r2, _group_rows(wloc, g), _group_rows(mloc, g), _group_rows(vloc, g),
                           tr=GROUPS[g][1], name="adamw_" + g)
        for kind in range(4):
            sh_out[kind].update(_ungroup_rows(res[kind], g))

    (rparts,) = _all_gather([_pack_repl(rgrads)], name="ag_repl_grads")
    rp_out = _adamw_replicated(rparts, _pack_repl({n: given[n] for n, _, _ in REPL_LAYOUT}),
                               _pack_repl({n: given["m_" + n] for n, _, _ in REPL_LAYOUT}),
                               _pack_repl({n: given["v_" + n] for n, _, _ in REPL_LAYOUT}))
    rp_out = [_unpack_repl(b) for b in rp_out]

    def leaf(kind, name):
        if name in SHARDED:
            mine = sh_out[kind]
            whole = mine[(name, None)] if (name, None) in mine else jnp.stack([mine[(name, 0)], mine[(name, 1)]])
            return whole.reshape(given[name].shape)
        return rp_out[kind][name]

    loss = lax.psum(loss_row[0, 0], ("x", "y", "c"))
    outs = [loss, dx[None]]
    for kind in range(4):
        outs += [leaf(kind, n) for n in WEIGHTS]
    return tuple(outs)
```
